```python
import math
import jax, jax.numpy as jnp
from jax import lax
import numpy as np

D_MODEL = 1024
BATCH = 8
SEQ = 2048
DEPTH = 2

GRID_W = 64
CTX_LEN = 256

D_MIX = D_MODEL
D_FOURIER = D_MIX // 4
FOURIER_GROUPS = 4
D_FG = D_FOURIER // FOURIER_GROUPS
HEAD_DIM = 64
NA_HEADS = (D_MIX // 2) // HEAD_DIM
D_NA = NA_HEADS * HEAD_DIM
D_CONV = D_MIX - D_FOURIER - D_NA
CONV_WIDTH = 3
D_IN_PROJ = D_FOURIER + 3 * D_NA + 3 * D_CONV

NA_KH = 8
NA_KW = 16
NA_QB = 16
NA_KB = 32

N_GROUPS = 4
EXPERTS_PER_GROUP = 8
N_EXPERTS = N_GROUPS * EXPERTS_PER_GROUP
TOP_K = 2
D_EXPERT = 512

EPS = 1e-6
NEG_INF = -1e30

kernel_name = "hybrid_fourier_natten_shortconv_hmoe_dit"


def rms_norm(x, gain):
    xf = x.astype(jnp.float32)
    y = xf * lax.rsqrt(jnp.mean(xf * xf, axis=-1, keepdims=True) + EPS)
    return (y * gain.astype(jnp.float32)).astype(x.dtype)


def split_in(u):
    o = [D_FOURIER, D_FOURIER + D_NA, D_FOURIER + 2 * D_NA, D_FOURIER + 3 * D_NA,
         D_FOURIER + 3 * D_NA + D_CONV, D_FOURIER + 3 * D_NA + 2 * D_CONV]
    f, q, k, v, gb, gc, h = jnp.split(u, o, axis=-1)
    heads = lambda t: t.reshape(*t.shape[:-1], NA_HEADS, HEAD_DIM)
    return f, heads(q), heads(k), heads(v), gb, gc, h


def fourier_mix(f):
    b, n, _ = f.shape
    fg = f.reshape(b, n, FOURIER_GROUPS, D_FG).astype(jnp.float32)
    out = jnp.fft.fft2(fg, axes=(1, 3), norm="ortho").real
    return out.reshape(b, n, D_FOURIER).astype(f.dtype)


def short_conv(h, w):
    return lax.conv_general_dilated(
        h, w[:, None, :].astype(h.dtype), window_strides=(1,), padding=((1, 1),),
        dimension_numbers=("NWC", "WIO", "NWC"), feature_group_count=h.shape[-1])


def na_latent(q, k, v, kc, vc, rpb):
    b, n, h, dh = q.shape
    rows = n // GRID_W
    kh = min(NA_KH, rows)
    ncb = GRID_W // NA_QB
    r = jnp.arange(rows)
    row_start = jnp.clip(r - kh // 2, 0, rows - kh)
    row_idx = row_start[:, None] + jnp.arange(kh)[None, :]
    cols = jnp.arange(GRID_W)
    col_start = jnp.clip(cols - NA_KW // 2, 0, GRID_W - NA_KW)
    band_start = jnp.clip(col_start[::NA_QB], 0, GRID_W - NA_KB)
    band_idx = band_start[:, None] + jnp.arange(NA_KB)[None, :]
    qcol = cols.reshape(ncb, NA_QB)
    qcs = col_start.reshape(ncb, NA_QB)
    kcol = band_idx[:, None, :]
    valid = (kcol >= qcs[..., None]) & (kcol < qcs[..., None] + NA_KW)
    roff = row_idx - r[:, None] + (NA_KH - 1)
    coff = jnp.clip(kcol - qcol[..., None] + (NA_KW - 1), 0, 2 * NA_KW - 2)
    bias = rpb[:, roff[:, None, None, :, None], coff[None, :, :, None, :]]

    qg = q.reshape(b, rows, ncb, NA_QB, h, dh)
    kg = k.reshape(b, rows, GRID_W, h, dh)
    vg = v.reshape(b, rows, GRID_W, h, dh)
    ri = row_idx[:, :, None, None]
    ci = band_idx[None, None]
    kb = kg[:, ri, ci]
    vb = vg[:, ri, ci]

    s_loc = jnp.einsum("brcqhd,brkcjhd->bhrcqkj", qg, kb).astype(jnp.float32)
    s_loc = jnp.where(valid[:, :, None, :], s_loc + bias.astype(jnp.float32), NEG_INF)
    s_ctx = jnp.einsum("brcqhd,blhd->bhrcql", qg, kc).astype(jnp.float32)
    kj = kh * NA_KB
    s = jnp.concatenate([s_loc.reshape(*s_loc.shape[:5], kj), s_ctx], axis=-1)
    p = jax.nn.softmax(s, axis=-1).astype(v.dtype)
    p_loc = p[..., :kj].reshape(s_loc.shape)
    p_ctx = p[..., kj:]
    o = (jnp.einsum("bhrcqkj,brkcjhd->brcqhd", p_loc, vb)
         + jnp.einsum("bhrcql,blhd->brcqhd", p_ctx, vc))
    return o.reshape(b, n, h * dh)


def na_context(qc, kc, vc):
    s = jnp.einsum("blhd,bmhd->bhlm", qc, kc).astype(jnp.float32)
    p = jax.nn.softmax(s, axis=-1).astype(vc.dtype)
    o = jnp.einsum("bhlm,bmhd->blhd", p, vc)
    return o.reshape(*o.shape[:2], D_NA)


def mix_out(f, attn, conv, w_fourier, w_out):
    return jnp.concatenate([fourier_mix(f) @ w_fourier, attn, conv], axis=-1) @ w_out


def hier_moe(t, w_rg, b_rg, w_re, b_re, w_gate, w_up, w_down):
    n_tok = t.shape[0]
    g_logits = (t @ w_rg + b_rg).astype(jnp.float32)
    g_prob = jax.nn.softmax(g_logits, axis=-1)
    g_idx = jnp.argmax(g_logits, axis=-1)
    g_w = jnp.take_along_axis(g_prob, g_idx[:, None], axis=-1)
    e_logits = (t @ w_re + b_re).astype(jnp.float32).reshape(n_tok, N_GROUPS, EXPERTS_PER_GROUP)
    e_sel = jnp.take_along_axis(e_logits, g_idx[:, None, None], axis=1)[:, 0]
    top_v, top_i = lax.top_k(e_sel, TOP_K)
    pair_w = g_w * jax.nn.softmax(top_v, axis=-1)
    expert_id = g_idx[:, None] * EXPERTS_PER_GROUP + top_i
    combine = jnp.sum(jax.nn.one_hot(expert_id, N_EXPERTS, dtype=jnp.float32)
                      * pair_w[..., None], axis=1).astype(t.dtype)
    y = jnp.zeros_like(t)
    for e in range(N_EXPERTS):
        he = jax.nn.silu(t @ w_gate[e]) * (t @ w_up[e])
        y = y + combine[:, e:e + 1] * (he @ w_down[e])
    return y


def trunk_layer(x, xc, mod_x, mod_c, norm1, norm2, w_in, w_fourier, w_conv, rpb, w_out,
                w_rg, b_rg, w_re, b_re, w_gate, w_up, w_down, last):
    b, n, d = x.shape
    l = xc.shape[1]
    sh1, sc1, g1, sh2, sc2, g2 = jnp.split(mod_x[:, None, :], 6, axis=-1)
    csh1, csc1, cg1, csh2, csc2, cg2 = jnp.split(mod_c, 6, axis=-1)
    scale = 1.0 / math.sqrt(HEAD_DIM)

    ux = (rms_norm(x, norm1) * (1 + sc1) + sh1) @ w_in
    uc = (rms_norm(xc, norm1) * (1 + csc1) + csh1) @ w_in
    fx, qx, kx, vx, bx, cx, hx = split_in(ux)
    fc, qc, kc, vc, bc, cc, hc = split_in(uc)
    attn_x = na_latent(qx * scale, kx, vx, kc, vc, rpb)
    conv_x = bx * short_conv(cx * hx, w_conv)
    x = x + g1 * mix_out(fx, attn_x, conv_x, w_fourier, w_out)
    if not last:
        attn_c = na_context(qc * scale, kc, vc)
        conv_c = bc * short_conv(cc * hc, w_conv)
        xc = xc + cg1 * mix_out(fc, attn_c, conv_c, w_fourier, w_out)

    hx2 = rms_norm(x, norm2) * (1 + sc2) + sh2
    if last:
        y = hier_moe(hx2.reshape(-1, d), w_rg, b_rg, w_re, b_re, w_gate, w_up, w_down)
        x = x + g2 * y.reshape(b, n, d)
    else:
        hc2 = rms_norm(xc, norm2) * (1 + csc2) + csh2
        tok = jnp.concatenate([hx2.reshape(-1, d), hc2.reshape(-1, d)], axis=0)
        y = hier_moe(tok, w_rg, b_rg, w_re, b_re, w_gate, w_up, w_down)
        x = x + g2 * y[:b * n].reshape(b, n, d)
        xc = xc + cg2 * y[b * n:].reshape(b, l, d)
    return x, xc


def setup_inputs(seed: int = 0) -> dict:
    key = jax.random.key(seed)
    ks = jax.random.split(key, 24)
    d = D_MODEL
    nrm = lambda k, shape, s: jax.random.normal(k, shape, jnp.float32) * s
    return {
        "x": nrm(ks[0], (BATCH, SEQ, d), 1.0),
        "c": nrm(ks[1], (BATCH, d), 1.0),
        "ctx": nrm(ks[2], (BATCH, CTX_LEN, d), 1.0),
        "c_ctx": nrm(ks[3], (d,), 1.0),
        "w_ada": nrm(ks[4], (DEPTH, d, 6 * d), 0.5 * d ** -0.5),
        "b_ada": nrm(ks[5], (DEPTH, 6 * d), 0.02),
        "norm1": 1.0 + nrm(ks[6], (DEPTH, d), 0.05),
        "norm2": 1.0 + nrm(ks[7], (DEPTH, d), 0.05),
        "w_in": nrm(ks[8], (DEPTH, d, D_IN_PROJ), d ** -0.5),
        "w_fourier": nrm(ks[9], (DEPTH, D_FOURIER, D_FOURIER), D_FOURIER ** -0.5),
        "w_conv": nrm(ks[10], (DEPTH, CONV_WIDTH, D_CONV), CONV_WIDTH ** -0.5),
        "rpb": nrm(ks[11], (DEPTH, NA_HEADS, 2 * NA_KH - 1, 2 * NA_KW - 1), 0.1),
        "w_out": nrm(ks[12], (DEPTH, D_MIX, d), D_MIX ** -0.5),
        "w_rg": nrm(ks[13], (DEPTH, d, N_GROUPS), d ** -0.5),
        "b_rg": nrm(ks[14], (DEPTH, N_GROUPS), 0.01),
        "w_re": nrm(ks[15], (DEPTH, d, N_EXPERTS), d ** -0.5),
        "b_re": nrm(ks[16], (DEPTH, N_EXPERTS), 0.01),
        "w_gate": nrm(ks[17], (DEPTH, N_EXPERTS, d, D_EXPERT), d ** -0.5),
        "w_up": nrm(ks[18], (DEPTH, N_EXPERTS, d, D_EXPERT), d ** -0.5),
        "w_down": nrm(ks[19], (DEPTH, N_EXPERTS, D_EXPERT, d), D_EXPERT ** -0.5),
        "norm_final": 1.0 + nrm(ks[20], (d,), 0.05),
    }


def reference(x, c, ctx, c_ctx, w_ada, b_ada, norm1, norm2, w_in, w_fourier, w_conv, rpb,
              w_out, w_rg, b_rg, w_re, b_re, w_gate, w_up, w_down, norm_final):
    xc = ctx
    sc = jax.nn.silu(c)
    scc = jax.nn.silu(c_ctx)
    for li in range(DEPTH):
        mod_x = sc @ w_ada[li] + b_ada[li]
        mod_c = scc @ w_ada[li] + b_ada[li]
        x, xc = trunk_layer(x, xc, mod_x, mod_c, norm1[li], norm2[li], w_in[li], w_fourier[li],
                            w_conv[li], rpb[li], w_out[li], w_rg[li], b_rg[li], w_re[li],
                            b_re[li], w_gate[li], w_up[li], w_down[li], li == DEPTH - 1)
    return rms_norm(x, norm_final)
```

```python
import functools
import math

import numpy as np
import jax
import jax.numpy as jnp
from jax import lax
from jax.experimental import pallas as pl
from jax.experimental.pallas import tpu as pltpu

F32 = jnp.float32
BF16 = jnp.bfloat16

D_MODEL = 1024
BATCH = 8
SEQ = 2048
DEPTH = 2
GRID_W = 64
ROWS = SEQ // GRID_W
CTX_LEN = 256
T_LAT = BATCH * SEQ
T_CTX = BATCH * CTX_LEN
T_ALL = T_LAT + T_CTX

D_FOURIER = 256
D_FG = 64
HEAD_DIM = 64
NA_HEADS = 8
D_NA = NA_HEADS * HEAD_DIM
D_CONV = 256
D_IN_PROJ = D_FOURIER + 3 * D_NA + 3 * D_CONV
NA_KH = 8
NA_KW = 16
N_GROUPS = 4
EXPERTS_PER_GROUP = 8
N_EXPERTS = 32
D_EXPERT = 512
EPS = 1e-6
NEG_INF = -1e30

LANES = 128
SUBLANES = 8
MOD_ROWS = 16
TM = 512
TMX = 256
TD = 512
TC = 256
VMEM_LIMIT = 56 * 1024 * 1024


def _cparams(*sem):
    return pltpu.CompilerParams(dimension_semantics=sem, vmem_limit_bytes=VMEM_LIMIT)


def _ada_kernel(cc_ref, w_ref, b_ref, o_ref):
    s = cc_ref[...]
    s = s * jax.nn.sigmoid(s)
    acc = jnp.dot(s.astype(BF16), w_ref[0].astype(BF16), preferred_element_type=F32)
    o_ref[0] = acc + b_ref[0]


def _modulation(cc, w_ada, b_ada):
    tn = 1536
    nj = 6 * D_MODEL // tn
    return pl.pallas_call(
        _ada_kernel,
        grid=(DEPTH, nj),
        in_specs=[
            pl.BlockSpec((MOD_ROWS, D_MODEL), lambda l, j: (0, 0)),
            pl.BlockSpec((1, D_MODEL, tn), lambda l, j: (l, 0, j)),
            pl.BlockSpec((1, 1, tn), lambda l, j: (l, 0, j)),
        ],
        out_specs=pl.BlockSpec((1, MOD_ROWS, tn), lambda l, j: (l, 0, j)),
        out_shape=jax.ShapeDtypeStruct((DEPTH, MOD_ROWS, 6 * D_MODEL), F32),
        compiler_params=_cparams("arbitrary", "arbitrary"),
    )(cc, w_ada, b_ada.reshape(DEPTH, 1, 6 * D_MODEL))


def _mod_row(i, tile):
    return jnp.minimum((i * tile) // SEQ, BATCH)


def _inproj_kernel(x_ref, mod_ref, n1_ref, w_ref, dft_ref, yf_ref, qkv_ref, cv_ref):
    x = x_ref[...]
    ms = jnp.mean(x * x, axis=-1, keepdims=True)
    y = x * lax.rsqrt(ms + EPS) * n1_ref[0]
    sh = mod_ref[:, 0:D_MODEL]
    sc = mod_ref[:, D_MODEL:2 * D_MODEL]
    h = y * (1.0 + sc) + sh
    u = jnp.dot(h.astype(BF16), w_ref[0], preferred_element_type=F32)
    yf = jnp.dot(u[:, 0:D_FOURIER].astype(BF16), dft_ref[...], preferred_element_type=F32)
    yf_ref[...] = yf.astype(BF16)
    q0 = D_FOURIER
    qkv_ref[:, 0:D_NA] = (u[:, q0:q0 + D_NA] * (1.0 / math.sqrt(HEAD_DIM))).astype(BF16)
    qkv_ref[:, D_NA:3 * D_NA] = u[:, q0 + D_NA:q0 + 3 * D_NA].astype(BF16)
    cv_ref[...] = u[:, q0 + 3 * D_NA:D_IN_PROJ]


def _in_projection(xt, mod4, norm1, w_in_bf, dft_ch, li):
    nt = T_ALL // TM
    return pl.pallas_call(
        _inproj_kernel,
        grid=(nt,),
        in_specs=[
            pl.BlockSpec((TM, D_MODEL), lambda i: (i, 0)),
            pl.BlockSpec((None, None, 1, 6 * D_MODEL), lambda i: (li, _mod_row(i, TM), 0, 0)),
            pl.BlockSpec((1, 1, D_MODEL), lambda i: (li, 0, 0)),
            pl.BlockSpec((1, D_MODEL, D_IN_PROJ), lambda i: (li, 0, 0)),
            pl.BlockSpec((D_FOURIER, 2 * D_FOURIER), lambda i: (0, 0)),
        ],
        out_specs=[
            pl.BlockSpec((TM, 2 * D_FOURIER), lambda i: (i, 0)),
            pl.BlockSpec((TM, 3 * D_NA), lambda i: (i, 0)),
            pl.BlockSpec((TM, 3 * D_CONV), lambda i: (i, 0)),
        ],
        out_shape=[
            jax.ShapeDtypeStruct((T_ALL, 2 * D_FOURIER), BF16),
            jax.ShapeDtypeStruct((T_ALL, 3 * D_NA), BF16),
            jax.ShapeDtypeStruct((T_ALL, 3 * D_CONV), F32),
        ],
        compiler_params=_cparams("arbitrary"),
    )(xt, mod4, norm1.reshape(DEPTH, 1, D_MODEL), w_in_bf, dft_ch)


def _dft_tables(n):
    k = np.arange(n, dtype=np.int64)
    ang = 2.0 * np.pi * ((k[:, None] * k[None, :]) % n).astype(np.float64) / n
    s = 1.0 / math.sqrt(n)
    return np.cos(ang) * s, np.sin(ang) * s


def _channel_dft():
    c, s = _dft_tables(D_FG)
    eye = np.eye(D_FOURIER // D_FG)
    return np.concatenate([np.kron(eye, c), np.kron(eye, s)], axis=1)


def _fourier_kernel(cn_ref, sn_ref, y_ref, wf_ref, o_ref):
    z = (jnp.dot(cn_ref[...], y_ref[:, 0:D_FOURIER], preferred_element_type=F32)
         - jnp.dot(sn_ref[...], y_ref[:, D_FOURIER:2 * D_FOURIER], preferred_element_type=F32))
    o_ref[...] = jnp.dot(z.astype(BF16), wf_ref[0].astype(BF16), preferred_element_type=F32).astype(BF16)


def _fourier_mix(yf, w_fourier, li, n, first_block):
    tk = min(n, 512)
    nk = n // tk
    cn, sn = _dft_tables(n)
    cn = jnp.asarray(cn, dtype=BF16)
    sn = jnp.asarray(sn, dtype=BF16)
    return pl.pallas_call(
        _fourier_kernel,
        grid=(nk, BATCH),
        in_specs=[
            pl.BlockSpec((tk, n), lambda k, b: (k, 0)),
            pl.BlockSpec((tk, n), lambda k, b: (k, 0)),
            pl.BlockSpec((n, 2 * D_FOURIER), lambda k, b: (first_block + b, 0)),
            pl.BlockSpec((1, D_FOURIER, D_FOURIER), lambda k, b: (li, 0, 0)),
        ],
        out_specs=pl.BlockSpec((tk, D_FOURIER), lambda k, b: (b * nk + k, 0)),
        out_shape=jax.ShapeDtypeStruct((BATCH * n, D_FOURIER), BF16),
        compiler_params=_cparams("arbitrary", "arbitrary"),
    )(cn, sn, yf, w_fourier)


def _conv_kernel(cv_ref, wc_ref, o_ref, pad_ref):
    n = cv_ref.shape[0]
    gb = cv_ref[:, 0:D_CONV]
    g = cv_ref[:, D_CONV:2 * D_CONV] * cv_ref[:, 2 * D_CONV:3 * D_CONV]
    zero = jnp.zeros((SUBLANES, D_CONV), F32)
    pad_ref[0:SUBLANES, :] = zero
    pad_ref[n + SUBLANES:n + 2 * SUBLANES, :] = zero
    pad_ref[SUBLANES:n + SUBLANES, :] = g
    prev = pad_ref[SUBLANES - 1:n + SUBLANES - 1, :]
    nxt = pad_ref[SUBLANES + 1:n + SUBLANES + 1, :]
    w = wc_ref[0]
    o_ref[...] = (gb * (w[0:1, :] * prev + w[1:2, :] * g + w[2:3, :] * nxt)).astype(BF16)


def _short_conv(cv, w_conv, li, n, first_block):
    return pl.pallas_call(
        _conv_kernel,
        grid=(BATCH,),
        in_specs=[
            pl.BlockSpec((n, 3 * D_CONV), lambda b: (first_block + b, 0)),
            pl.BlockSpec((1, 3, D_CONV), lambda b: (li, 0, 0)),
        ],
        out_specs=pl.BlockSpec((n, D_CONV), lambda b: (b, 0)),
        out_shape=jax.ShapeDtypeStruct((BATCH * n, D_CONV), BF16),
        scratch_shapes=[pltpu.VMEM((n + 2 * SUBLANES, D_CONV), F32)],
        compiler_params=_cparams("arbitrary"),
    )(cv, w_conv)


RPB_H = 2 * NA_KH - 1
RPB_W = 2 * NA_KW - 1
KEYS_LOC = NA_KH * GRID_W


def _bias_kernel(rpb_ref, o_ref):
    h = pl.program_id(0)
    qi = lax.broadcasted_iota(jnp.int32, (GRID_W, LANES), 0)
    lj = lax.broadcasted_iota(jnp.int32, (GRID_W, LANES), 1)
    kc = lj & (GRID_W - 1)
    hi = lj >= GRID_W
    d = kc - qi + (NA_KW - 1)
    cs = jnp.clip(qi - NA_KW // 2, 0, GRID_W - NA_KW)
    valid = (kc >= cs) & (kc < cs + NA_KW)
    tiles = []
    for a in range(RPB_H - 1):
        acc = jnp.zeros((GRID_W, LANES), F32)
        for b in range(RPB_W):
            va = rpb_ref[(h * RPB_H + a) * RPB_W + b]
            vb = rpb_ref[(h * RPB_H + a + 1) * RPB_W + b]
            acc = jnp.where(d == b, jnp.where(hi, vb, va), acc)
        tiles.append(jnp.where(valid, acc, NEG_INF))
    for cls in range(NA_KH):
        for m in range(NA_KH // 2):
            o_ref[0, cls, :, m * LANES:(m + 1) * LANES] = tiles[2 * m - cls + NA_KH - 1]


def _bias_table(rpb_flat):
    return pl.pallas_call(
        _bias_kernel,
        grid=(NA_HEADS,),
        in_specs=[pl.BlockSpec(memory_space=pltpu.SMEM)],
        out_specs=pl.BlockSpec((1, NA_KH, GRID_W, KEYS_LOC), lambda h: (h, 0, 0, 0)),
        out_shape=jax.ShapeDtypeStruct((NA_HEADS, NA_KH, GRID_W, KEYS_LOC), F32),
        compiler_params=_cparams("arbitrary"),
    )(rpb_flat)


_NT_DIMS = (((1,), (1,)), ((), ()))


def _softmax_pv(parts):
    m = None
    for s, _ in parts:
        ms = jnp.max(s, axis=1, keepdims=True)
        m = ms if m is None else jnp.maximum(m, ms)
    l = None
    o = None
    for s, v in parts:
        p = jnp.exp(s - m)
        ls = jnp.sum(p, axis=1, keepdims=True)
        os_ = jnp.dot(p.astype(BF16), v, preferred_element_type=F32)
        l = ls if l is None else l + ls
        o = os_ if o is None else o + os_
    return o / l


def _attn_latent_kernel(q_ref, k_ref, v_ref, kc_ref, vc_ref, tab_ref, o_ref):
    lane = lax.broadcasted_iota(jnp.int32, (GRID_W, LANES), 1)
    lo = lane < HEAD_DIM
    kc = kc_ref[...]
    vc = vc_ref[...]

    def body(r, carry):
        rs = jnp.clip(r - NA_KH // 2, 0, ROWS - NA_KH)
        cls = r - rs
        q = q_ref[pl.ds(pl.multiple_of(r * GRID_W, GRID_W), GRID_W), :]
        kl = k_ref[pl.ds(pl.multiple_of(rs * GRID_W, GRID_W), KEYS_LOC), :]
        vl = v_ref[pl.ds(pl.multiple_of(rs * GRID_W, GRID_W), KEYS_LOC), :]
        outs = []
        for hh in range(2):
            qm = jnp.where(lo if hh == 0 else jnp.logical_not(lo), q, jnp.zeros_like(q))
            s1 = lax.dot_general(qm, kl, _NT_DIMS, preferred_element_type=F32) + tab_ref[hh, cls]
            s2 = lax.dot_general(qm, kc, _NT_DIMS, preferred_element_type=F32)
            outs.append(_softmax_pv([(s1, vl), (s2, vc)]))
        o_ref[pl.ds(pl.multiple_of(r * GRID_W, GRID_W), GRID_W), :] = jnp.where(lo, outs[0], outs[1]).astype(BF16)
        return carry

    lax.fori_loop(0, ROWS, body, 0)


def _attn_latent(qkv, table):
    npair = NA_HEADS // 2
    cblk = T_LAT // CTX_LEN
    return pl.pallas_call(
        _attn_latent_kernel,
        grid=(BATCH, npair),
        in_specs=[
            pl.BlockSpec((SEQ, LANES), lambda b, p: (b, p)),
            pl.BlockSpec((SEQ, LANES), lambda b, p: (b, npair + p)),
            pl.BlockSpec((SEQ, LANES), lambda b, p: (b, 2 * npair + p)),
            pl.BlockSpec((CTX_LEN, LANES), lambda b, p: (cblk + b, npair + p)),
            pl.BlockSpec((CTX_LEN, LANES), lambda b, p: (cblk + b, 2 * npair + p)),
            pl.BlockSpec((2, NA_KH, GRID_W, KEYS_LOC), lambda b, p: (p, 0, 0, 0)),
        ],
        out_specs=pl.BlockSpec((SEQ, LANES), lambda b, p: (b, p)),
        out_shape=jax.ShapeDtypeStruct((T_LAT, D_NA), BF16),
        compiler_params=_cparams("arbitrary", "arbitrary"),
    )(qkv, qkv, qkv, qkv, qkv, table)


def _attn_ctx_kernel(q_ref, k_ref, v_ref, o_ref):
    lane = lax.broadcasted_iota(jnp.int32, (CTX_LEN, LANES), 1)
    lo = lane < HEAD_DIM
    q = q_ref[...]
    k = k_ref[...]
    v = v_ref[...]
    outs = []
    for hh in range(2):
        qm = jnp.where(lo if hh == 0 else jnp.logical_not(lo), q, jnp.zeros_like(q))
        s = lax.dot_general(qm, k, _NT_DIMS, preferred_element_type=F32)
        outs.append(_softmax_pv([(s, v)]))
    o_ref[...] = jnp.where(lo, outs[0], outs[1]).astype(BF16)


def _attn_ctx(qkv):
    npair = NA_HEADS // 2
    cblk = T_LAT // CTX_LEN
    return pl.pallas_call(
        _attn_ctx_kernel,
        grid=(BATCH, npair),
        in_specs=[
            pl.BlockSpec((CTX_LEN, LANES), lambda b, p: (cblk + b, p)),
            pl.BlockSpec((CTX_LEN, LANES), lambda b, p: (cblk + b, npair + p)),
            pl.BlockSpec((CTX_LEN, LANES), lambda b, p: (cblk + b, 2 * npair + p)),
        ],
        out_specs=pl.BlockSpec((CTX_LEN, LANES), lambda b, p: (b, p)),
        out_shape=jax.ShapeDtypeStruct((T_CTX, D_NA), BF16),
        compiler_params=_cparams("arbitrary", "arbitrary"),
    )(qkv, qkv, qkv)


ROUTE_COLS = N_GROUPS + N_EXPERTS


def _to_token_major(ref, val):
    rows = val.shape[0]
    for j in range(D_MODEL // LANES):
        ref[pl.ds(j, rows, stride=SUBLANES), :] = val[:, j * LANES:(j + 1) * LANES]


def _from_token_major(ref, rows, j):
    return ref[pl.ds(j, rows, stride=SUBLANES), :]


def _outproj_kernel(x_ref, ff_ref, at_ref, cv_ref, mod_ref, n2_ref, wo_ref, wr_ref, br_ref,
                    x1_ref, hx_ref, route_ref):
    mix_in = jnp.concatenate([ff_ref[...], at_ref[...], cv_ref[...]], axis=-1)
    mix = jnp.dot(mix_in, wo_ref[0], preferred_element_type=F32)
    g1 = mod_ref[:, 2 * D_MODEL:3 * D_MODEL]
    x1 = x_ref[...] + g1 * mix
    x1_ref[...] = x1
    ms = jnp.mean(x1 * x1, axis=-1, keepdims=True)
    y = x1 * lax.rsqrt(ms + EPS) * n2_ref[0]
    sh2 = mod_ref[:, 3 * D_MODEL:4 * D_MODEL]
    sc2 = mod_ref[:, 4 * D_MODEL:5 * D_MODEL]
    hx = y * (1.0 + sc2) + sh2
    _to_token_major(hx_ref, hx)

    logits = jnp.dot(hx, wr_ref[0], preferred_element_type=F32, precision=lax.Precision.HIGHEST) + br_ref[0]
    tm = logits.shape[0]
    lane = lax.broadcasted_iota(jnp.int32, (tm, LANES), 1)
    big = jnp.int32(LANES)
    gl = jnp.where(lane < N_GROUPS, logits, -jnp.inf)
    gmax = jnp.max(gl, axis=1, keepdims=True)
    gidx = jnp.min(jnp.where(gl == gmax, lane, big), axis=1, keepdims=True)
    g_w = 1.0 / jnp.sum(jnp.exp(gl - gmax), axis=1, keepdims=True)
    e_lane = lane - N_GROUPS
    in_group = (e_lane >= 0) & (e_lane < N_EXPERTS) & ((e_lane >> 3) == gidx)
    es = jnp.where(in_group, logits, -jnp.inf)
    t1 = jnp.max(es, axis=1, keepdims=True)
    i1 = jnp.min(jnp.where(es == t1, lane, big), axis=1, keepdims=True)
    es2 = jnp.where(lane == i1, -jnp.inf, es)
    t2 = jnp.max(es2, axis=1, keepdims=True)
    i2 = jnp.min(jnp.where(es2 == t2, lane, big), axis=1, keepdims=True)
    dlt = jnp.exp(t2 - t1)
    w1 = g_w / (1.0 + dlt)
    w2 = g_w * dlt / (1.0 + dlt)
    e1 = (i1 - N_GROUPS).astype(F32)
    e2 = (i2 - N_GROUPS).astype(F32)
    route_ref[...] = jnp.where(lane == 0, e1, jnp.where(lane == 1, e2,
                               jnp.where(lane == 2, w1, jnp.where(lane == 3, w2, 0.0))))


def _out_projection(xt, ff, at, cvo, mod4, norm2, w_out_bf, w_r, b_r, li, n_tok):
    nt = n_tok // TM
    return pl.pallas_call(
        _outproj_kernel,
        grid=(nt,),
        in_specs=[
            pl.BlockSpec((TM, D_MODEL), lambda i: (i, 0)),
            pl.BlockSpec((TM, D_FOURIER), lambda i: (i, 0)),
            pl.BlockSpec((TM, D_NA), lambda i: (i, 0)),
            pl.BlockSpec((TM, D_CONV), lambda i: (i, 0)),
            pl.BlockSpec((None, None, 1, 6 * D_MODEL), lambda i: (li, _mod_row(i, TM), 0, 0)),
            pl.BlockSpec((1, 1, D_MODEL), lambda i: (li, 0, 0)),
            pl.BlockSpec((1, D_MODEL, D_MODEL), lambda i: (li, 0, 0)),
            pl.BlockSpec((1, D_MODEL, LANES), lambda i: (li, 0, 0)),
            pl.BlockSpec((1, 1, LANES), lambda i: (li, 0, 0)),
        ],
        out_specs=[
            pl.BlockSpec((TM, D_MODEL), lambda i: (i, 0)),
            pl.BlockSpec((TM * SUBLANES, LANES), lambda i: (i, 0)),
            pl.BlockSpec((TM, LANES), lambda i: (i, 0)),
        ],
        out_shape=[
            jax.ShapeDtypeStruct((n_tok, D_MODEL), F32),
            jax.ShapeDtypeStruct((n_tok * SUBLANES, LANES), F32),
            jax.ShapeDtypeStruct((n_tok, LANES), F32),
        ],
        compiler_params=_cparams("arbitrary"),
    )(xt, ff, at, cvo, mod4, norm2.reshape(DEPTH, 1, D_MODEL), w_out_bf, w_r, b_r)


def _plan_kernel(route_ref, rank_ref, cnt_ref, carry_ref):
    i = pl.program_id(0)

    @pl.when(i == 0)
    def _():
        carry_ref[...] = jnp.zeros_like(carry_ref)

    r = route_ref[...]
    tm = r.shape[0]
    lane = lax.broadcasted_iota(jnp.int32, (tm, LANES), 1)
    oh1 = lane == r[:, 0:1].astype(jnp.int32)
    oh2 = lane == r[:, 1:2].astype(jnp.int32)
    oh = jnp.where(oh1 | oh2, 1.0, 0.0)
    row = lax.broadcasted_iota(jnp.int32, (tm, tm), 0)
    col = lax.broadcasted_iota(jnp.int32, (tm, tm), 1)
    tri = jnp.where(row > col, 1.0, 0.0).astype(BF16)
    cum = jnp.dot(tri, oh.astype(BF16), preferred_element_type=F32) + carry_ref[...]
    r1 = jnp.sum(jnp.where(oh1, cum, 0.0), axis=1, keepdims=True)
    r2 = jnp.sum(jnp.where(oh2, cum, 0.0), axis=1, keepdims=True)
    rank_ref[...] = jnp.where(lane == 0, r1, jnp.where(lane == 1, r2, 0.0))
    carry_ref[...] += jnp.sum(oh, axis=0, keepdims=True)
    cnt_ref[...] = jnp.broadcast_to(carry_ref[...], cnt_ref.shape)


def _plan(route, n_tok):
    nt = n_tok // TM
    return pl.pallas_call(
        _plan_kernel,
        grid=(nt,),
        in_specs=[pl.BlockSpec((TM, LANES), lambda i: (i, 0))],
        out_specs=[
            pl.BlockSpec((TM, LANES), lambda i: (i, 0)),
            pl.BlockSpec((SUBLANES, LANES), lambda i: (0, 0)),
        ],
        out_shape=[
            jax.ShapeDtypeStruct((n_tok, LANES), F32),
            jax.ShapeDtypeStruct((SUBLANES, LANES), F32),
        ],
        scratch_shapes=[pltpu.VMEM((1, LANES), F32)],
        compiler_params=_cparams("arbitrary"),
    )(route)


def _row_tile(ref, row):
    return ref.at[pl.ds(pl.multiple_of(row * SUBLANES, SUBLANES), SUBLANES), :]


def _dispatch_kernel(pos_ref, hx_ref, init_ref, xs_ref, sem):
    del init_ref
    i = pl.program_id(0)

    def body(t, carry):
        tok = i * TD + t
        src = _row_tile(hx_ref, tok)
        for j in range(2):
            pltpu.make_async_copy(src, _row_tile(xs_ref, pos_ref[tok * 2 + j]), sem.at[0]).start()
        return carry

    lax.fori_loop(0, TD, body, 0)
    nrow = 2 * TD * SUBLANES
    pltpu.make_async_copy(hx_ref.at[pl.ds(0, nrow), :], xs_ref.at[pl.ds(0, nrow), :], sem.at[0]).wait()


def _dispatch(pos_flat, hx_tm, n_tok, n_rows):
    init = jnp.zeros((n_rows * SUBLANES, LANES), F32)
    return pl.pallas_call(
        _dispatch_kernel,
        grid_spec=pltpu.PrefetchScalarGridSpec(
            num_scalar_prefetch=1,
            grid=(n_tok // TD,),
            in_specs=[pl.BlockSpec(memory_space=pl.ANY), pl.BlockSpec(memory_space=pl.ANY)],
            out_specs=pl.BlockSpec(memory_space=pl.ANY),
            scratch_shapes=[pltpu.SemaphoreType.DMA((1,))],
        ),
        out_shape=jax.ShapeDtypeStruct((n_rows * SUBLANES, LANES), F32),
        input_output_aliases={2: 0},
        compiler_params=_cparams("arbitrary"),
    )(pos_flat, hx_tm, init)


def _moe_kernel(te_ref, na_ref, xs_ref, wg_ref, wu_ref, wd_ref, os_ref, wg_s, wu_s, wd_s):
    i = pl.program_id(0)
    prev = te_ref[jnp.maximum(i - 1, 0)]
    active = i < na_ref[0]

    @pl.when(active & ((i == 0) | (te_ref[i] != prev)))
    def _():
        wg_s[...] = wg_ref[0, 0].astype(BF16)
        wu_s[...] = wu_ref[0, 0].astype(BF16)
        wd_s[...] = wd_ref[0, 0].astype(BF16)

    @pl.when(active)
    def _():
        x = jnp.concatenate([_from_token_major(xs_ref, TMX, j).astype(BF16)
                             for j in range(D_MODEL // LANES)], axis=-1)
        g = jnp.dot(x, wg_s[...], preferred_element_type=F32)
        u = jnp.dot(x, wu_s[...], preferred_element_type=F32)
        h = (g * jax.nn.sigmoid(g) * u).astype(BF16)
        _to_token_major(os_ref, jnp.dot(h, wd_s[...], preferred_element_type=F32))

    @pl.when(jnp.logical_not(active))
    def _():
        os_ref[...] = jnp.zeros_like(os_ref)


def _moe(tile_expert, n_active, xs, w_gate, w_up, w_down, li, n_tiles):
    def row_map(i, te, na):
        return (jnp.minimum(i, na[0] - 1), 0)

    def w_map(i, te, na):
        return (li, te[jnp.minimum(i, na[0] - 1)], 0, 0)

    return pl.pallas_call(
        _moe_kernel,
        grid_spec=pltpu.PrefetchScalarGridSpec(
            num_scalar_prefetch=2,
            grid=(n_tiles,),
            in_specs=[
                pl.BlockSpec((TMX * SUBLANES, LANES), row_map),
                pl.BlockSpec((1, 1, D_MODEL, D_EXPERT), w_map),
                pl.BlockSpec((1, 1, D_MODEL, D_EXPERT), w_map),
                pl.BlockSpec((1, 1, D_EXPERT, D_MODEL), w_map),
            ],
            out_specs=pl.BlockSpec((TMX * SUBLANES, LANES), lambda i, te, na: (i, 0)),
            scratch_shapes=[
                pltpu.VMEM((D_MODEL, D_EXPERT), BF16),
                pltpu.VMEM((D_MODEL, D_EXPERT), BF16),
                pltpu.VMEM((D_EXPERT, D_MODEL), BF16),
            ],
        ),
        out_shape=jax.ShapeDtypeStruct((n_tiles * TMX * SUBLANES, LANES), F32),
        compiler_params=_cparams("arbitrary"),
    )(tile_expert, n_active, xs, w_gate, w_up, w_down)


def _combine_kernel(pos_ref, x1_ref, route_ref, mod_ref, nf_ref, os_ref, o_ref, buf0, buf1, sem, *, final_norm):
    i = pl.program_id(0)

    def body(t, carry):
        tok = i * TC + t
        pltpu.make_async_copy(_row_tile(os_ref, pos_ref[tok * 2]), _row_tile(buf0, t), sem.at[0]).start()
        pltpu.make_async_copy(_row_tile(os_ref, pos_ref[tok * 2 + 1]), _row_tile(buf1, t), sem.at[1]).start()
        return carry

    lax.fori_loop(0, TC, body, 0)
    nrow = TC * SUBLANES
    pltpu.make_async_copy(os_ref.at[pl.ds(0, nrow), :], buf0, sem.at[0]).wait()
    pltpu.make_async_copy(os_ref.at[pl.ds(0, nrow), :], buf1, sem.at[1]).wait()

    w1 = route_ref[:, 2:3]
    w2 = route_ref[:, 3:4]
    cols = []
    for j in range(D_MODEL // LANES):
        y = w1 * _from_token_major(buf0, TC, j) + w2 * _from_token_major(buf1, TC, j)
        g2 = mod_ref[:, 5 * D_MODEL + j * LANES:5 * D_MODEL + (j + 1) * LANES]
        cols.append(x1_ref[:, j * LANES:(j + 1) * LANES] + g2 * y)
    x2 = jnp.concatenate(cols, axis=-1)
    if final_norm:
        ms = jnp.mean(x2 * x2, axis=-1, keepdims=True)
        x2 = x2 * lax.rsqrt(ms + EPS) * nf_ref[...]
    o_ref[...] = x2


def _combine(pos_flat, x1, route, mod4, norm_final, osrt, li, n_tok, final_norm):
    def tmap(i, pos):
        return (i, 0)

    return pl.pallas_call(
        functools.partial(_combine_kernel, final_norm=final_norm),
        grid_spec=pltpu.PrefetchScalarGridSpec(
            num_scalar_prefetch=1,
            grid=(n_tok // TC,),
            in_specs=[
                pl.BlockSpec((TC, D_MODEL), tmap),
                pl.BlockSpec((TC, LANES), tmap),
                pl.BlockSpec((None, None, 1, 6 * D_MODEL), lambda i, pos: (li, _mod_row(i, TC), 0, 0)),
                pl.BlockSpec((1, D_MODEL), lambda i, pos: (0, 0)),
                pl.BlockSpec(memory_space=pl.ANY),
            ],
            out_specs=pl.BlockSpec((TC, D_MODEL), tmap),
            scratch_shapes=[
                pltpu.VMEM((TC * SUBLANES, LANES), F32),
                pltpu.VMEM((TC * SUBLANES, LANES), F32),
                pltpu.SemaphoreType.DMA((2,)),
            ],
        ),
        out_shape=jax.ShapeDtypeStruct((n_tok, D_MODEL), F32),
        compiler_params=_cparams("arbitrary"),
    )(pos_flat, x1, route, mod4, norm_final.reshape(1, D_MODEL), osrt)


def _moe_block(x1, hx_tm, route, mod4, norm_final, w_gate, w_up, w_down, li, n_tok, final_norm):
    n_tiles = (2 * n_tok) // TMX + N_EXPERTS
    n_rows = n_tiles * TMX
    rank, cnt = _plan(route, n_tok)
    cnt = cnt[0, :N_EXPERTS].astype(jnp.int32)
    padded = ((cnt + TMX - 1) // TMX) * TMX
    ends = jnp.cumsum(padded)
    offs = ends - padded
    eid = route[:, 0:2].astype(jnp.int32)
    pos = (offs[eid] + rank[:, 0:2].astype(jnp.int32)).reshape(-1)
    tile_start = jnp.arange(n_tiles, dtype=jnp.int32) * TMX
    tile_expert = jnp.minimum(jnp.searchsorted(ends, tile_start, side="right"), N_EXPERTS - 1).astype(jnp.int32)
    n_active = (ends[-1:] // TMX).astype(jnp.int32)

    xs = _dispatch(pos, hx_tm, n_tok, n_rows)
    osrt = _moe(tile_expert, n_active, xs, w_gate, w_up, w_down, li, n_tiles)
    return _combine(pos, x1, route, mod4, norm_final, osrt, li, n_tok, final_norm)


def kernel(x, c, ctx, c_ctx, w_ada, b_ada, norm1, norm2, w_in, w_fourier, w_conv, rpb, w_out, w_rg, b_rg,
           w_re, b_re, w_gate, w_up, w_down, norm_final):
    xt = jnp.concatenate([x.reshape(T_LAT, D_MODEL), ctx.reshape(T_CTX, D_MODEL)], axis=0)
    cc = jnp.concatenate([c, c_ctx[None, :], jnp.zeros((MOD_ROWS - BATCH - 1, D_MODEL), F32)], axis=0)
    mod4 = _modulation(cc, w_ada, b_ada).reshape(DEPTH, MOD_ROWS, 1, 6 * D_MODEL)

    w_in_bf = w_in.astype(BF16)
    w_out_bf = w_out.astype(BF16)
    pad = jnp.zeros((DEPTH, D_MODEL, LANES - ROUTE_COLS), F32)
    w_r = jnp.concatenate([w_rg, w_re, pad], axis=-1)
    b_r = jnp.concatenate([b_rg, b_re, pad[:, 0, :]], axis=-1).reshape(DEPTH, 1, LANES)
    dft_ch = jnp.asarray(_channel_dft(), dtype=BF16)
    ctx_blk = T_LAT // CTX_LEN

    for li in range(DEPTH):
        last = li == DEPTH - 1
        yf, qkv, cv = _in_projection(xt, mod4, norm1, w_in_bf, dft_ch, li)
        table = _bias_table(rpb[li].reshape(-1))
        ff = _fourier_mix(yf, w_fourier, li, SEQ, 0)
        cvo = _short_conv(cv, w_conv, li, SEQ, 0)
        at = _attn_latent(qkv, table)
        if last:
            n_tok = T_LAT
        else:
            n_tok = T_ALL
            ff = jnp.concatenate([ff, _fourier_mix(yf, w_fourier, li, CTX_LEN, ctx_blk)], axis=0)
            cvo = jnp.concatenate([cvo, _short_conv(cv, w_conv, li, CTX_LEN, ctx_blk)], axis=0)
            at = jnp.concatenate([at, _attn_ctx(qkv)], axis=0)
        x1, hx_tm, route = _out_projection(xt, ff, at, cvo, mod4, norm2, w_out_bf, w_r, b_r, li, n_tok)
        xt_new = _moe_block(x1, hx_tm, route, mod4, norm_final, w_gate, w_up, w_down, li, n_tok, last)
        if last:
            return xt_new.reshape(BATCH, SEQ, D_MODEL)
        xt = xt_new
```

```python
import functools
import math

import numpy as np
import jax
import jax.numpy as jnp
from jax import lax
from jax.experimental import pallas as pl
from jax.experimental.pallas import tpu as pltpu

F32 = jnp.float32
BF16 = jnp.bfloat16

D_MODEL = 1024
BATCH = 8
SEQ = 2048
DEPTH = 2
GRID_W = 64
ROWS = SEQ // GRID_W
CTX_LEN = 256
T_LAT = BATCH * SEQ
T_CTX = BATCH * CTX_LEN
T_ALL = T_LAT + T_CTX

D_FOURIER = 256
D_FG = 64
HEAD_DIM = 64
NA_HEADS = 8
D_NA = NA_HEADS * HEAD_DIM
D_CONV = 256
D_IN_PROJ = D_FOURIER + 3 * D_NA + 3 * D_CONV
NA_KH = 8
NA_KW = 16
N_GROUPS = 4
EXPERTS_PER_GROUP = 8
N_EXPERTS = 32
D_EXPERT = 512
EPS = 1e-6
NEG_INF = -1e30

LANES = 128
SUBLANES = 8
MOD_ROWS = 16
TM = 512
TMX = 256
TD = 512
TC = 256
VMEM_LIMIT = 56 * 1024 * 1024


def _cparams(*sem):
    return pltpu.CompilerParams(dimension_semantics=sem, vmem_limit_bytes=VMEM_LIMIT)


def _ada_kernel(cc_ref, w_ref, b_ref, o_ref):
    s = cc_ref[...]
    s = s * jax.nn.sigmoid(s)
    acc = jnp.dot(s.astype(BF16), w_ref[0].astype(BF16), preferred_element_type=F32)
    o_ref[0] = acc + b_ref[0]


def _modulation(cc, w_ada, b_ada):
    tn = 1536
    nj = 6 * D_MODEL // tn
    return pl.pallas_call(
        _ada_kernel,
        name="ada_modulation",
        grid=(DEPTH, nj),
        in_specs=[
            pl.BlockSpec((MOD_ROWS, D_MODEL), lambda l, j: (0, 0)),
            pl.BlockSpec((1, D_MODEL, tn), lambda l, j: (l, 0, j)),
            pl.BlockSpec((1, 1, tn), lambda l, j: (l, 0, j)),
        ],
        out_specs=pl.BlockSpec((1, MOD_ROWS, tn), lambda l, j: (l, 0, j)),
        out_shape=jax.ShapeDtypeStruct((DEPTH, MOD_ROWS, 6 * D_MODEL), F32),
        compiler_params=_cparams("arbitrary", "arbitrary"),
    )(cc, w_ada, b_ada.reshape(DEPTH, 1, 6 * D_MODEL))


def _mod_row(i, tile):
    return jnp.minimum((i * tile) // SEQ, BATCH)


def _inproj_kernel(x_ref, mod_ref, n1_ref, w_ref, dft_ref, yf_ref, qkv_ref, cv_ref):
    x = x_ref[...]
    ms = jnp.mean(x * x, axis=-1, keepdims=True)
    y = x * lax.rsqrt(ms + EPS) * n1_ref[0]
    sh = mod_ref[:, 0:D_MODEL]
    sc = mod_ref[:, D_MODEL:2 * D_MODEL]
    h = y * (1.0 + sc) + sh
    u = jnp.dot(h.astype(BF16), w_ref[0], preferred_element_type=F32)
    yf = jnp.dot(u[:, 0:D_FOURIER].astype(BF16), dft_ref[...], preferred_element_type=F32)
    yf_ref[...] = yf.astype(BF16)
    q0 = D_FOURIER
    qkv_ref[:, 0:D_NA] = (u[:, q0:q0 + D_NA] * (1.0 / math.sqrt(HEAD_DIM))).astype(BF16)
    qkv_ref[:, D_NA:3 * D_NA] = u[:, q0 + D_NA:q0 + 3 * D_NA].astype(BF16)
    cv_ref[...] = u[:, q0 + 3 * D_NA:D_IN_PROJ]


def _in_projection(xt, mod4, norm1, w_in_bf, dft_ch, li):
    nt = T_ALL // TM
    return pl.pallas_call(
        _inproj_kernel,
        name="in_projection",
        grid=(nt,),
        in_specs=[
            pl.BlockSpec((TM, D_MODEL), lambda i: (i, 0)),
            pl.BlockSpec((None, None, 1, 6 * D_MODEL), lambda i: (li, _mod_row(i, TM), 0, 0)),
            pl.BlockSpec((1, 1, D_MODEL), lambda i: (li, 0, 0)),
            pl.BlockSpec((1, D_MODEL, D_IN_PROJ), lambda i: (li, 0, 0)),
            pl.BlockSpec((D_FOURIER, 2 * D_FOURIER), lambda i: (0, 0)),
        ],
        out_specs=[
            pl.BlockSpec((TM, 2 * D_FOURIER), lambda i: (i, 0)),
            pl.BlockSpec((TM, 3 * D_NA), lambda i: (i, 0)),
            pl.BlockSpec((TM, 3 * D_CONV), lambda i: (i, 0)),
        ],
        out_shape=[
            jax.ShapeDtypeStruct((T_ALL, 2 * D_FOURIER), BF16),
            jax.ShapeDtypeStruct((T_ALL, 3 * D_NA), BF16),
            jax.ShapeDtypeStruct((T_ALL, 3 * D_CONV), F32),
        ],
        compiler_params=_cparams("arbitrary"),
    )(xt, mod4, norm1.reshape(DEPTH, 1, D_MODEL), w_in_bf, dft_ch)


def _dft_tables(n):
    k = np.arange(n, dtype=np.int64)
    ang = 2.0 * np.pi * ((k[:, None] * k[None, :]) % n).astype(np.float64) / n
    s = 1.0 / math.sqrt(n)
    return np.cos(ang) * s, np.sin(ang) * s


def _channel_dft():
    c, s = _dft_tables(D_FG)
    eye = np.eye(D_FOURIER // D_FG)
    return np.concatenate([np.kron(eye, c), np.kron(eye, s)], axis=1)


def _fourier_kernel(cn_ref, sn_ref, y_ref, wf_ref, o_ref):
    z = (jnp.dot(cn_ref[...], y_ref[:, 0:D_FOURIER], preferred_element_type=F32)
         - jnp.dot(sn_ref[...], y_ref[:, D_FOURIER:2 * D_FOURIER], preferred_element_type=F32))
    o_ref[...] = jnp.dot(z.astype(BF16), wf_ref[0].astype(BF16), preferred_element_type=F32).astype(BF16)


def _fourier_mix(yf, w_fourier, li, n, first_block):
    tk = min(n, 512)
    nk = n // tk
    cn, sn = _dft_tables(n)
    cn = jnp.asarray(cn, dtype=F32).astype(BF16)
    sn = jnp.asarray(sn, dtype=F32).astype(BF16)
    return pl.pallas_call(
        _fourier_kernel,
        name="fourier_mix",
        grid=(nk, BATCH),
        in_specs=[
            pl.BlockSpec((tk, n), lambda k, b: (k, 0)),
            pl.BlockSpec((tk, n), lambda k, b: (k, 0)),
            pl.BlockSpec((n, 2 * D_FOURIER), lambda k, b: (first_block + b, 0)),
            pl.BlockSpec((1, D_FOURIER, D_FOURIER), lambda k, b: (li, 0, 0)),
        ],
        out_specs=pl.BlockSpec((tk, D_FOURIER), lambda k, b: (b * nk + k, 0)),
        out_shape=jax.ShapeDtypeStruct((BATCH * n, D_FOURIER), BF16),
        compiler_params=_cparams("arbitrary", "arbitrary"),
    )(cn, sn, yf, w_fourier)


def _conv_kernel(cv_ref, wc_ref, o_ref, pad_ref):
    n = cv_ref.shape[0]
    gb = cv_ref[:, 0:D_CONV]
    g = cv_ref[:, D_CONV:2 * D_CONV] * cv_ref[:, 2 * D_CONV:3 * D_CONV]
    zero = jnp.zeros((SUBLANES, D_CONV), F32)
    pad_ref[0:SUBLANES, :] = zero
    pad_ref[n + SUBLANES:n + 2 * SUBLANES, :] = zero
    pad_ref[SUBLANES:n + SUBLANES, :] = g
    prev = pad_ref[SUBLANES - 1:n + SUBLANES - 1, :]
    nxt = pad_ref[SUBLANES + 1:n + SUBLANES + 1, :]
    w = wc_ref[0]
    o_ref[...] = (gb * (w[0:1, :] * prev + w[1:2, :] * g + w[2:3, :] * nxt)).astype(BF16)


def _short_conv(cv, w_conv, li, n, first_block):
    return pl.pallas_call(
        _conv_kernel,
        name="short_conv",
        grid=(BATCH,),
        in_specs=[
            pl.BlockSpec((n, 3 * D_CONV), lambda b: (first_block + b, 0)),
            pl.BlockSpec((1, 3, D_CONV), lambda b: (li, 0, 0)),
        ],
        out_specs=pl.BlockSpec((n, D_CONV), lambda b: (b, 0)),
        out_shape=jax.ShapeDtypeStruct((BATCH * n, D_CONV), BF16),
        scratch_shapes=[pltpu.VMEM((n + 2 * SUBLANES, D_CONV), F32)],
        compiler_params=_cparams("arbitrary"),
    )(cv, w_conv)


RPB_H = 2 * NA_KH - 1
RPB_W = 2 * NA_KW - 1
KEYS_LOC = NA_KH * GRID_W


def _bias_kernel(rpb_ref, o_ref):
    h = pl.program_id(0)
    qi = lax.broadcasted_iota(jnp.int32, (GRID_W, LANES), 0)
    lj = lax.broadcasted_iota(jnp.int32, (GRID_W, LANES), 1)
    kc = lj & (GRID_W - 1)
    hi = lj >= GRID_W
    d = kc - qi + (NA_KW - 1)
    cs = jnp.clip(qi - NA_KW // 2, 0, GRID_W - NA_KW)
    valid = (kc >= cs) & (kc < cs + NA_KW)
    tiles = []
    for a in range(RPB_H - 1):
        acc = jnp.zeros((GRID_W, LANES), F32)
        for b in range(RPB_W):
            va = rpb_ref[(h * RPB_H + a) * RPB_W + b]
            vb = rpb_ref[(h * RPB_H + a + 1) * RPB_W + b]
            acc = jnp.where(d == b, jnp.where(hi, vb, va), acc)
        tiles.append(jnp.where(valid, acc, NEG_INF))
    for cls in range(NA_KH):
        for m in range(NA_KH // 2):
            o_ref[0, cls, :, m * LANES:(m + 1) * LANES] = tiles[2 * m - cls + NA_KH - 1]


def _bias_table(rpb_flat):
    return pl.pallas_call(
        _bias_kernel,
        name="attn_bias_table",
        grid=(NA_HEADS,),
        in_specs=[pl.BlockSpec(memory_space=pltpu.SMEM)],
        out_specs=pl.BlockSpec((1, NA_KH, GRID_W, KEYS_LOC), lambda h: (h, 0, 0, 0)),
        out_shape=jax.ShapeDtypeStruct((NA_HEADS, NA_KH, GRID_W, KEYS_LOC), F32),
        compiler_params=_cparams("arbitrary"),
    )(rpb_flat)


_NT_DIMS = (((1,), (1,)), ((), ()))


def _softmax_pv(parts):
    m = None
    for s, _ in parts:
        ms = jnp.max(s, axis=1, keepdims=True)
        m = ms if m is None else jnp.maximum(m, ms)
    l = None
    o = None
    for s, v in parts:
        p = jnp.exp(s - m)
        ls = jnp.sum(p, axis=1, keepdims=True)
        os_ = jnp.dot(p.astype(BF16), v, preferred_element_type=F32)
        l = ls if l is None else l + ls
        o = os_ if o is None else o + os_
    return o / l


def _attn_latent_kernel(q_ref, k_ref, v_ref, kc_ref, vc_ref, tab_ref, o_ref):
    lane = lax.broadcasted_iota(jnp.int32, (GRID_W, LANES), 1)
    lo = lane < HEAD_DIM
    kc = kc_ref[...]
    vc = vc_ref[...]

    def body(r, carry):
        rs = jnp.clip(r - NA_KH // 2, 0, ROWS - NA_KH)
        cls = r - rs
        q = q_ref[pl.ds(pl.multiple_of(r * GRID_W, GRID_W), GRID_W), :]
        kl = k_ref[pl.ds(pl.multiple_of(rs * GRID_W, GRID_W), KEYS_LOC), :]
        vl = v_ref[pl.ds(pl.multiple_of(rs * GRID_W, GRID_W), KEYS_LOC), :]
        outs = []
        for hh in range(2):
            qm = jnp.where(lo if hh == 0 else jnp.logical_not(lo), q, jnp.zeros_like(q))
            s1 = lax.dot_general(qm, kl, _NT_DIMS, preferred_element_type=F32) + tab_ref[hh, cls]
            s2 = lax.dot_general(qm, kc, _NT_DIMS, preferred_element_type=F32)
            outs.append(_softmax_pv([(s1, vl), (s2, vc)]))
        o_ref[pl.ds(pl.multiple_of(r * GRID_W, GRID_W), GRID_W), :] = jnp.where(lo, outs[0], outs[1]).astype(BF16)
        return carry

    lax.fori_loop(0, ROWS, body, 0)


def _attn_latent(qkv, table):
    npair = NA_HEADS // 2
    cblk = T_LAT // CTX_LEN
    return pl.pallas_call(
        _attn_latent_kernel,
        name="attn_latent",
        grid=(BATCH, npair),
        in_specs=[
            pl.BlockSpec((SEQ, LANES), lambda b, p: (b, p)),
            pl.BlockSpec((SEQ, LANES), lambda b, p: (b, npair + p)),
            pl.BlockSpec((SEQ, LANES), lambda b, p: (b, 2 * npair + p)),
            pl.BlockSpec((CTX_LEN, LANES), lambda b, p: (cblk + b, npair + p)),
            pl.BlockSpec((CTX_LEN, LANES), lambda b, p: (cblk + b, 2 * npair + p)),
            pl.BlockSpec((2, NA_KH, GRID_W, KEYS_LOC), lambda b, p: (p, 0, 0, 0)),
        ],
        out_specs=pl.BlockSpec((SEQ, LANES), lambda b, p: (b, p)),
        out_shape=jax.ShapeDtypeStruct((T_LAT, D_NA), BF16),
        compiler_params=_cparams("arbitrary", "arbitrary"),
    )(qkv, qkv, qkv, qkv, qkv, table)


def _attn_ctx_kernel(q_ref, k_ref, v_ref, o_ref):
    lane = lax.broadcasted_iota(jnp.int32, (CTX_LEN, LANES), 1)
    lo = lane < HEAD_DIM
    q = q_ref[...]
    k = k_ref[...]
    v = v_ref[...]
    outs = []
    for hh in range(2):
        qm = jnp.where(lo if hh == 0 else jnp.logical_not(lo), q, jnp.zeros_like(q))
        s = lax.dot_general(qm, k, _NT_DIMS, preferred_element_type=F32)
        outs.append(_softmax_pv([(s, v)]))
    o_ref[...] = jnp.where(lo, outs[0], outs[1]).astype(BF16)


def _attn_ctx(qkv):
    npair = NA_HEADS // 2
    cblk = T_LAT // CTX_LEN
    return pl.pallas_call(
        _attn_ctx_kernel,
        name="attn_context",
        grid=(BATCH, npair),
        in_specs=[
            pl.BlockSpec((CTX_LEN, LANES), lambda b, p: (cblk + b, p)),
            pl.BlockSpec((CTX_LEN, LANES), lambda b, p: (cblk + b, npair + p)),
            pl.BlockSpec((CTX_LEN, LANES), lambda b, p: (cblk + b, 2 * npair + p)),
        ],
        out_specs=pl.BlockSpec((CTX_LEN, LANES), lambda b, p: (b, p)),
        out_shape=jax.ShapeDtypeStruct((T_CTX, D_NA), BF16),
        compiler_params=_cparams("arbitrary", "arbitrary"),
    )(qkv, qkv, qkv)


ROUTE_COLS = N_GROUPS + N_EXPERTS


def _to_token_major(ref, val):
    rows = val.shape[0]
    for j in range(D_MODEL // LANES):
        ref[pl.ds(j, rows, stride=SUBLANES), :] = val[:, j * LANES:(j + 1) * LANES]


def _from_token_major(ref, rows, j):
    return ref[pl.ds(j, rows, stride=SUBLANES), :]


def _outproj_kernel(x_ref, ff_ref, at_ref, cv_ref, mod_ref, n2_ref, wo_ref, wr_ref, br_ref,
                    x1_ref, hx_ref, route_ref):
    mix_in = jnp.concatenate([ff_ref[...], at_ref[...], cv_ref[...]], axis=-1)
    mix = jnp.dot(mix_in, wo_ref[0], preferred_element_type=F32)
    g1 = mod_ref[:, 2 * D_MODEL:3 * D_MODEL]
    x1 = x_ref[...] + g1 * mix
    x1_ref[...] = x1
    ms = jnp.mean(x1 * x1, axis=-1, keepdims=True)
    y = x1 * lax.rsqrt(ms + EPS) * n2_ref[0]
    sh2 = mod_ref[:, 3 * D_MODEL:4 * D_MODEL]
    sc2 = mod_ref[:, 4 * D_MODEL:5 * D_MODEL]
    hx = y * (1.0 + sc2) + sh2
    _to_token_major(hx_ref, hx)

    logits = jnp.dot(hx, wr_ref[0], preferred_element_type=F32, precision=lax.Precision.HIGHEST) + br_ref[0]
    tm = logits.shape[0]
    lane = lax.broadcasted_iota(jnp.int32, (tm, LANES), 1)
    big = jnp.int32(LANES)
    gl = jnp.where(lane < N_GROUPS, logits, -jnp.inf)
    gmax = jnp.max(gl, axis=1, keepdims=True)
    gidx = jnp.min(jnp.where(gl == gmax, lane, big), axis=1, keepdims=True)
    g_w = 1.0 / jnp.sum(jnp.exp(gl - gmax), axis=1, keepdims=True)
    e_lane = lane - N_GROUPS
    in_group = (e_lane >= 0) & (e_lane < N_EXPERTS) & ((e_lane >> 3) == gidx)
    es = jnp.where(in_group, logits, -jnp.inf)
    t1 = jnp.max(es, axis=1, keepdims=True)
    i1 = jnp.min(jnp.where(es == t1, lane, big), axis=1, keepdims=True)
    es2 = jnp.where(lane == i1, -jnp.inf, es)
    t2 = jnp.max(es2, axis=1, keepdims=True)
    i2 = jnp.min(jnp.where(es2 == t2, lane, big), axis=1, keepdims=True)
    dlt = jnp.exp(t2 - t1)
    w1 = g_w / (1.0 + dlt)
    w2 = g_w * dlt / (1.0 + dlt)
    e1 = (i1 - N_GROUPS).astype(F32)
    e2 = (i2 - N_GROUPS).astype(F32)
    route_ref[...] = jnp.where(lane == 0, e1, jnp.where(lane == 1, e2,
                               jnp.where(lane == 2, w1, jnp.where(lane == 3, w2, 0.0))))


def _out_projection(xt, ff, at, cvo, mod4, norm2, w_out_bf, w_r, b_r, li, n_tok):
    nt = n_tok // TM
    return pl.pallas_call(
        _outproj_kernel,
        name="out_projection",
        grid=(nt,),
        in_specs=[
            pl.BlockSpec((TM, D_MODEL), lambda i: (i, 0)),
            pl.BlockSpec((TM, D_FOURIER), lambda i: (i, 0)),
            pl.BlockSpec((TM, D_NA), lambda i: (i, 0)),
            pl.BlockSpec((TM, D_CONV), lambda i: (i, 0)),
            pl.BlockSpec((None, None, 1, 6 * D_MODEL), lambda i: (li, _mod_row(i, TM), 0, 0)),
            pl.BlockSpec((1, 1, D_MODEL), lambda i: (li, 0, 0)),
            pl.BlockSpec((1, D_MODEL, D_MODEL), lambda i: (li, 0, 0)),
            pl.BlockSpec((1, D_MODEL, LANES), lambda i: (li, 0, 0)),
            pl.BlockSpec((1, 1, LANES), lambda i: (li, 0, 0)),
        ],
        out_specs=[
            pl.BlockSpec((TM, D_MODEL), lambda i: (i, 0)),
            pl.BlockSpec((TM * SUBLANES, LANES), lambda i: (i, 0)),
            pl.BlockSpec((TM, LANES), lambda i: (i, 0)),
        ],
        out_shape=[
            jax.ShapeDtypeStruct((n_tok, D_MODEL), F32),
            jax.ShapeDtypeStruct((n_tok * SUBLANES, LANES), F32),
            jax.ShapeDtypeStruct((n_tok, LANES), F32),
        ],
        compiler_params=_cparams("arbitrary"),
    )(xt, ff, at, cvo, mod4, norm2.reshape(DEPTH, 1, D_MODEL), w_out_bf, w_r, b_r)


def _plan_kernel(route_ref, rank_ref, cnt_ref, carry_ref):
    i = pl.program_id(0)

    @pl.when(i == 0)
    def _():
        carry_ref[...] = jnp.zeros_like(carry_ref)

    r = route_ref[...]
    tm = r.shape[0]
    lane = lax.broadcasted_iota(jnp.int32, (tm, LANES), 1)
    oh1 = lane == r[:, 0:1].astype(jnp.int32)
    oh2 = lane == r[:, 1:2].astype(jnp.int32)
    oh = jnp.where(oh1 | oh2, 1.0, 0.0)
    row = lax.broadcasted_iota(jnp.int32, (tm, tm), 0)
    col = lax.broadcasted_iota(jnp.int32, (tm, tm), 1)
    tri = jnp.where(row > col, 1.0, 0.0).astype(BF16)
    cum = jnp.dot(tri, oh.astype(BF16), preferred_element_type=F32) + carry_ref[...]
    r1 = jnp.sum(jnp.where(oh1, cum, 0.0), axis=1, keepdims=True)
    r2 = jnp.sum(jnp.where(oh2, cum, 0.0), axis=1, keepdims=True)
    rank_ref[...] = jnp.where(lane == 0, r1, jnp.where(lane == 1, r2, 0.0))
    carry_ref[...] += jnp.sum(oh, axis=0, keepdims=True)
    cnt_ref[...] = jnp.broadcast_to(carry_ref[...], cnt_ref.shape)


def _plan(route, n_tok):
    nt = n_tok // TM
    return pl.pallas_call(
        _plan_kernel,
        name="moe_plan",
        grid=(nt,),
        in_specs=[pl.BlockSpec((TM, LANES), lambda i: (i, 0))],
        out_specs=[
            pl.BlockSpec((TM, LANES), lambda i: (i, 0)),
            pl.BlockSpec((SUBLANES, LANES), lambda i: (0, 0)),
        ],
        out_shape=[
            jax.ShapeDtypeStruct((n_tok, LANES), F32),
            jax.ShapeDtypeStruct((SUBLANES, LANES), F32),
        ],
        scratch_shapes=[pltpu.VMEM((1, LANES), F32)],
        compiler_params=_cparams("arbitrary"),
    )(route)


def _row_tile(ref, row):
    return ref.at[pl.ds(pl.multiple_of(row * SUBLANES, SUBLANES), SUBLANES), :]


def _dispatch_kernel(pos_ref, hx_ref, init_ref, xs_ref, sem):
    del init_ref
    i = pl.program_id(0)

    def body(t, carry):
        src = _row_tile(hx_ref, t)
        for j in range(2):
            pltpu.make_async_copy(src, _row_tile(xs_ref, pos_ref[(i * TD + t) * 2 + j]), sem.at[0]).start()
        return carry

    lax.fori_loop(0, TD, body, 0)
    nrow = TD * SUBLANES
    for j in range(2):
        pltpu.make_async_copy(hx_ref, xs_ref.at[pl.ds(0, nrow), :], sem.at[0]).wait()


def _dispatch(pos_flat, hx_tm, n_tok, n_rows):
    init = jnp.zeros((n_rows * SUBLANES, LANES), F32)
    return pl.pallas_call(
        _dispatch_kernel,
        name="moe_dispatch",
        grid_spec=pltpu.PrefetchScalarGridSpec(
            num_scalar_prefetch=1,
            grid=(n_tok // TD,),
            in_specs=[pl.BlockSpec((TD * SUBLANES, LANES), lambda i, pos: (i, 0)),
                      pl.BlockSpec(memory_space=pl.ANY)],
            out_specs=pl.BlockSpec(memory_space=pl.ANY),
            scratch_shapes=[pltpu.SemaphoreType.DMA((1,))],
        ),
        out_shape=jax.ShapeDtypeStruct((n_rows * SUBLANES, LANES), F32),
        input_output_aliases={2: 0},
        compiler_params=_cparams("arbitrary"),
    )(pos_flat, hx_tm, init)


def _moe_kernel(te_ref, na_ref, xs_ref, wg_ref, wu_ref, wd_ref, os_ref, wg_s, wu_s, wd_s):
    i = pl.program_id(0)
    prev = te_ref[jnp.maximum(i - 1, 0)]
    active = i < na_ref[0]

    @pl.when(active & ((i == 0) | (te_ref[i] != prev)))
    def _():
        wg_s[...] = wg_ref[0, 0].astype(BF16)
        wu_s[...] = wu_ref[0, 0].astype(BF16)
        wd_s[...] = wd_ref[0, 0].astype(BF16)

    @pl.when(active)
    def _():
        x = jnp.concatenate([_from_token_major(xs_ref, TMX, j).astype(BF16)
                             for j in range(D_MODEL // LANES)], axis=-1)
        g = jnp.dot(x, wg_s[...], preferred_element_type=F32)
        u = jnp.dot(x, wu_s[...], preferred_element_type=F32)
        h = (g * jax.nn.sigmoid(g) * u).astype(BF16)
        _to_token_major(os_ref, jnp.dot(h, wd_s[...], preferred_element_type=F32))

    @pl.when(jnp.logical_not(active))
    def _():
        os_ref[...] = jnp.zeros_like(os_ref)


def _moe(tile_expert, n_active, xs, w_gate, w_up, w_down, li, n_tiles):
    def row_map(i, te, na):
        return (jnp.minimum(i, na[0] - 1), 0)

    def w_map(i, te, na):
        return (li, te[jnp.minimum(i, na[0] - 1)], 0, 0)

    return pl.pallas_call(
        _moe_kernel,
        name="moe_experts",
        grid_spec=pltpu.PrefetchScalarGridSpec(
            num_scalar_prefetch=2,
            grid=(n_tiles,),
            in_specs=[
                pl.BlockSpec((TMX * SUBLANES, LANES), row_map),
                pl.BlockSpec((1, 1, D_MODEL, D_EXPERT), w_map),
                pl.BlockSpec((1, 1, D_MODEL, D_EXPERT), w_map),
                pl.BlockSpec((1, 1, D_EXPERT, D_MODEL), w_map),
            ],
            out_specs=pl.BlockSpec((TMX * SUBLANES, LANES), lambda i, te, na: (i, 0)),
            scratch_shapes=[
                pltpu.VMEM((D_MODEL, D_EXPERT), BF16),
                pltpu.VMEM((D_MODEL, D_EXPERT), BF16),
                pltpu.VMEM((D_EXPERT, D_MODEL), BF16),
            ],
        ),
        out_shape=jax.ShapeDtypeStruct((n_tiles * TMX * SUBLANES, LANES), F32),
        compiler_params=_cparams("arbitrary"),
    )(tile_expert, n_active, xs, w_gate, w_up, w_down)


def _combine_kernel(pos_ref, x1_ref, route_ref, mod_ref, nf_ref, os_ref, o_ref, buf0, buf1, sem, *, final_norm):
    i = pl.program_id(0)

    def body(t, carry):
        tok = i * TC + t
        pltpu.make_async_copy(_row_tile(os_ref, pos_ref[tok * 2]), _row_tile(buf0, t), sem.at[0]).start()
        pltpu.make_async_copy(_row_tile(os_ref, pos_ref[tok * 2 + 1]), _row_tile(buf1, t), sem.at[1]).start()
        return carry

    lax.fori_loop(0, TC, body, 0)
    nrow = TC * SUBLANES
    pltpu.make_async_copy(os_ref.at[pl.ds(0, nrow), :], buf0, sem.at[0]).wait()
    pltpu.make_async_copy(os_ref.at[pl.ds(0, nrow), :], buf1, sem.at[1]).wait()

    w1 = route_ref[:, 2:3]
    w2 = route_ref[:, 3:4]
    cols = []
    for j in range(D_MODEL // LANES):
        y = w1 * _from_token_major(buf0, TC, j) + w2 * _from_token_major(buf1, TC, j)
        g2 = mod_ref[:, 5 * D_MODEL + j * LANES:5 * D_MODEL + (j + 1) * LANES]
        cols.append(x1_ref[:, j * LANES:(j + 1) * LANES] + g2 * y)
    x2 = jnp.concatenate(cols, axis=-1)
    if final_norm:
        ms = jnp.mean(x2 * x2, axis=-1, keepdims=True)
        x2 = x2 * lax.rsqrt(ms + EPS) * nf_ref[...]
    o_ref[...] = x2


def _combine(pos_flat, x1, route, mod4, norm_final, osrt, li, n_tok, final_norm):
    def tmap(i, pos):
        return (i, 0)

    return pl.pallas_call(
        functools.partial(_combine_kernel, final_norm=final_norm),
        name="moe_combine",
        grid_spec=pltpu.PrefetchScalarGridSpec(
            num_scalar_prefetch=1,
            grid=(n_tok // TC,),
            in_specs=[
                pl.BlockSpec((TC, D_MODEL), tmap),
                pl.BlockSpec((TC, LANES), tmap),
                pl.BlockSpec((None, None, 1, 6 * D_MODEL), lambda i, pos: (li, _mod_row(i, TC), 0, 0)),
                pl.BlockSpec((1, D_MODEL), lambda i, pos: (0, 0)),
                pl.BlockSpec(memory_space=pl.ANY),
            ],
            out_specs=pl.BlockSpec((TC, D_MODEL), tmap),
            scratch_shapes=[
                pltpu.VMEM((TC * SUBLANES, LANES), F32),
                pltpu.VMEM((TC * SUBLANES, LANES), F32),
                pltpu.SemaphoreType.DMA((2,)),
            ],
        ),
        out_shape=jax.ShapeDtypeStruct((n_tok, D_MODEL), F32),
        compiler_params=_cparams("arbitrary"),
    )(pos_flat, x1, route, mod4, norm_final.reshape(1, D_MODEL), osrt)


def _moe_block(x1, hx_tm, route, mod4, norm_final, w_gate, w_up, w_down, li, n_tok, final_norm):
    n_tiles = (2 * n_tok) // TMX + N_EXPERTS
    n_rows = n_tiles * TMX
    rank, cnt = _plan(route, n_tok)
    cnt = cnt[0, :N_EXPERTS].astype(jnp.int32)
    padded = ((cnt + TMX - 1) // TMX) * TMX
    ends = jnp.cumsum(padded)
    offs = ends - padded
    eid = route[:, 0:2].astype(jnp.int32)
    pos = (offs[eid] + rank[:, 0:2].astype(jnp.int32)).reshape(-1)
    tile_start = jnp.arange(n_tiles, dtype=jnp.int32) * TMX
    tile_expert = jnp.minimum(jnp.sum((tile_start[:, None] >= ends[None, :]).astype(jnp.int32), axis=1),
                              N_EXPERTS - 1)
    n_active = (ends[-1:] // TMX).astype(jnp.int32)

    xs = _dispatch(pos, hx_tm, n_tok, n_rows)
    osrt = _moe(tile_expert, n_active, xs, w_gate, w_up, w_down, li, n_tiles)
    return _combine(pos, x1, route, mod4, norm_final, osrt, li, n_tok, final_norm)


def kernel(x, c, ctx, c_ctx, w_ada, b_ada, norm1, norm2, w_in, w_fourier, w_conv, rpb, w_out, w_rg, b_rg,
           w_re, b_re, w_gate, w_up, w_down, norm_final):
    xt = jnp.concatenate([x.reshape(T_LAT, D_MODEL), ctx.reshape(T_CTX, D_MODEL)], axis=0)
    cc = jnp.concatenate([c, c_ctx[None, :], jnp.zeros((MOD_ROWS - BATCH - 1, D_MODEL), F32)], axis=0)
    mod4 = _modulation(cc, w_ada, b_ada).reshape(DEPTH, MOD_ROWS, 1, 6 * D_MODEL)

    w_in_bf = w_in.astype(BF16)
    w_out_bf = w_out.astype(BF16)
    pad = jnp.zeros((DEPTH, D_MODEL, LANES - ROUTE_COLS), F32)
    w_r = jnp.concatenate([w_rg, w_re, pad], axis=-1)
    b_r = jnp.concatenate([b_rg, b_re, pad[:, 0, :]], axis=-1).reshape(DEPTH, 1, LANES)
    dft_ch = jnp.asarray(_channel_dft(), dtype=F32).astype(BF16)
    ctx_blk = T_LAT // CTX_LEN

    for li in range(DEPTH):
        last = li == DEPTH - 1
        yf, qkv, cv = _in_projection(xt, mod4, norm1, w_in_bf, dft_ch, li)
        table = _bias_table(rpb[li].reshape(-1))
        ff = _fourier_mix(yf, w_fourier, li, SEQ, 0)
        cvo = _short_conv(cv, w_conv, li, SEQ, 0)
        at = _attn_latent(qkv, table)
        if last:
            n_tok = T_LAT
        else:
            n_tok = T_ALL
            ff = jnp.concatenate([ff, _fourier_mix(yf, w_fourier, li, CTX_LEN, ctx_blk)], axis=0)
            cvo = jnp.concatenate([cvo, _short_conv(cv, w_conv, li, CTX_LEN, ctx_blk)], axis=0)
            at = jnp.concatenate([at, _attn_ctx(qkv)], axis=0)
        x1, hx_tm, route = _out_projection(xt, ff, at, cvo, mod4, norm2, w_out_bf, w_r, b_r, li, n_tok)
        xt_new = _moe_block(x1, hx_tm, route, mod4, norm_final, w_gate, w_up, w_down, li, n_tok, last)
        if last:
            return xt_new.reshape(BATCH, SEQ, D_MODEL)
        xt = xt_new
```

```python
import functools
import math

import numpy as np
import jax
import jax.numpy as jnp
from jax import lax
from jax.experimental import pallas as pl
from jax.experimental.pallas import tpu as pltpu

F32 = jnp.float32
BF16 = jnp.bfloat16

D_MODEL = 1024
BATCH = 8
SEQ = 2048
DEPTH = 2
GRID_W = 64
ROWS = SEQ // GRID_W
CTX_LEN = 256
T_LAT = BATCH * SEQ
T_CTX = BATCH * CTX_LEN
T_ALL = T_LAT + T_CTX

D_FOURIER = 256
D_FG = 64
HEAD_DIM = 64
NA_HEADS = 8
D_NA = NA_HEADS * HEAD_DIM
D_CONV = 256
D_IN_PROJ = D_FOURIER + 3 * D_NA + 3 * D_CONV
NA_KH = 8
NA_KW = 16
N_GROUPS = 4
EXPERTS_PER_GROUP = 8
N_EXPERTS = 32
D_EXPERT = 512
EPS = 1e-6
NEG_INF = -1e30

LANES = 128
SUBLANES = 8
MOD_ROWS = 16
TM = 512
TMX = 256
TD = 512
TC = 256
VMEM_LIMIT = 56 * 1024 * 1024


def _cparams(*sem):
    return pltpu.CompilerParams(dimension_semantics=sem, vmem_limit_bytes=VMEM_LIMIT)


def _ada_kernel(cc_ref, w_ref, b_ref, o_ref):
    s = cc_ref[...]
    s = s * jax.nn.sigmoid(s)
    acc = jnp.dot(s.astype(BF16), w_ref[0].astype(BF16), preferred_element_type=F32)
    o_ref[0] = acc + b_ref[0]


def _modulation(cc, w_ada, b_ada):
    tn = 1536
    nj = 6 * D_MODEL // tn
    return pl.pallas_call(
        _ada_kernel,
        name="ada_modulation",
        grid=(DEPTH, nj),
        in_specs=[
            pl.BlockSpec((MOD_ROWS, D_MODEL), lambda l, j: (0, 0)),
            pl.BlockSpec((1, D_MODEL, tn), lambda l, j: (l, 0, j)),
            pl.BlockSpec((1, 1, tn), lambda l, j: (l, 0, j)),
        ],
        out_specs=pl.BlockSpec((1, MOD_ROWS, tn), lambda l, j: (l, 0, j)),
        out_shape=jax.ShapeDtypeStruct((DEPTH, MOD_ROWS, 6 * D_MODEL), F32),
        compiler_params=_cparams("arbitrary", "arbitrary"),
    )(cc, w_ada, b_ada.reshape(DEPTH, 1, 6 * D_MODEL))


def _mod_row(i, tile):
    return jnp.minimum((i * tile) // SEQ, BATCH)


def _inproj_kernel(x_ref, mod_ref, n1_ref, w_ref, dft_ref, yf_ref, qkv_ref, cv_ref):
    x = x_ref[...]
    ms = jnp.mean(x * x, axis=-1, keepdims=True)
    y = x * lax.rsqrt(ms + EPS) * n1_ref[0]
    sh = mod_ref[:, 0:D_MODEL]
    sc = mod_ref[:, D_MODEL:2 * D_MODEL]
    h = y * (1.0 + sc) + sh
    u = jnp.dot(h.astype(BF16), w_ref[0], preferred_element_type=F32)
    yf = jnp.dot(u[:, 0:D_FOURIER].astype(BF16), dft_ref[...], preferred_element_type=F32)
    yf_ref[...] = yf.astype(BF16)
    q0 = D_FOURIER
    qkv_ref[:, 0:D_NA] = (u[:, q0:q0 + D_NA] * (1.0 / math.sqrt(HEAD_DIM))).astype(BF16)
    qkv_ref[:, D_NA:3 * D_NA] = u[:, q0 + D_NA:q0 + 3 * D_NA].astype(BF16)
    cv_ref[...] = u[:, q0 + 3 * D_NA:D_IN_PROJ]


def _in_projection(xt, mod4, norm1, w_in_bf, dft_ch, li):
    nt = T_ALL // TM
    return pl.pallas_call(
        _inproj_kernel,
        name="in_projection",
        grid=(nt,),
        in_specs=[
            pl.BlockSpec((TM, D_MODEL), lambda i: (i, 0)),
            pl.BlockSpec((None, None, 1, 6 * D_MODEL), lambda i: (li, _mod_row(i, TM), 0, 0)),
            pl.BlockSpec((1, 1, D_MODEL), lambda i: (li, 0, 0)),
            pl.BlockSpec((1, D_MODEL, D_IN_PROJ), lambda i: (li, 0, 0)),
            pl.BlockSpec((D_FOURIER, 2 * D_FOURIER), lambda i: (0, 0)),
        ],
        out_specs=[
            pl.BlockSpec((TM, 2 * D_FOURIER), lambda i: (i, 0)),
            pl.BlockSpec((TM, 3 * D_NA), lambda i: (i, 0)),
            pl.BlockSpec((TM, 3 * D_CONV), lambda i: (i, 0)),
        ],
        out_shape=[
            jax.ShapeDtypeStruct((T_ALL, 2 * D_FOURIER), BF16),
            jax.ShapeDtypeStruct((T_ALL, 3 * D_NA), BF16),
            jax.ShapeDtypeStruct((T_ALL, 3 * D_CONV), F32),
        ],
        compiler_params=_cparams("arbitrary"),
    )(xt, mod4, norm1.reshape(DEPTH, 1, D_MODEL), w_in_bf, dft_ch)


def _dft_tables(n):
    k = np.arange(n, dtype=np.int64)
    ang = 2.0 * np.pi * ((k[:, None] * k[None, :]) % n).astype(np.float64) / n
    s = 1.0 / math.sqrt(n)
    return np.cos(ang) * s, np.sin(ang) * s


def _channel_dft():
    c, s = _dft_tables(D_FG)
    eye = np.eye(D_FOURIER // D_FG)
    return np.concatenate([np.kron(eye, c), np.kron(eye, s)], axis=1)


def _fourier_kernel(cn_ref, sn_ref, y_ref, wf_ref, o_ref):
    z = (jnp.dot(cn_ref[...], y_ref[:, 0:D_FOURIER], preferred_element_type=F32)
         - jnp.dot(sn_ref[...], y_ref[:, D_FOURIER:2 * D_FOURIER], preferred_element_type=F32))
    o_ref[...] = jnp.dot(z.astype(BF16), wf_ref[0].astype(BF16), preferred_element_type=F32).astype(BF16)


def _fourier_mix(yf, w_fourier, li, n, first_block):
    tk = min(n, 512)
    nk = n // tk
    cn, sn = _dft_tables(n)
    cn = jnp.asarray(cn, dtype=F32).astype(BF16)
    sn = jnp.asarray(sn, dtype=F32).astype(BF16)
    return pl.pallas_call(
        _fourier_kernel,
        name="fourier_mix",
        grid=(nk, BATCH),
        in_specs=[
            pl.BlockSpec((tk, n), lambda k, b: (k, 0)),
            pl.BlockSpec((tk, n), lambda k, b: (k, 0)),
            pl.BlockSpec((n, 2 * D_FOURIER), lambda k, b: (first_block + b, 0)),
            pl.BlockSpec((1, D_FOURIER, D_FOURIER), lambda k, b: (li, 0, 0)),
        ],
        out_specs=pl.BlockSpec((tk, D_FOURIER), lambda k, b: (b * nk + k, 0)),
        out_shape=jax.ShapeDtypeStruct((BATCH * n, D_FOURIER), BF16),
        compiler_params=_cparams("arbitrary", "arbitrary"),
    )(cn, sn, yf, w_fourier)


def _conv_kernel(cv_ref, wc_ref, o_ref, pad_ref):
    n = cv_ref.shape[0]
    gb = cv_ref[:, 0:D_CONV]
    g = cv_ref[:, D_CONV:2 * D_CONV] * cv_ref[:, 2 * D_CONV:3 * D_CONV]
    zero = jnp.zeros((SUBLANES, D_CONV), F32)
    pad_ref[0:SUBLANES, :] = zero
    pad_ref[n + SUBLANES:n + 2 * SUBLANES, :] = zero
    pad_ref[SUBLANES:n + SUBLANES, :] = g
    prev = pad_ref[SUBLANES - 1:n + SUBLANES - 1, :]
    nxt = pad_ref[SUBLANES + 1:n + SUBLANES + 1, :]
    w = wc_ref[0]
    o_ref[...] = (gb * (w[0:1, :] * prev + w[1:2, :] * g + w[2:3, :] * nxt)).astype(BF16)


def _short_conv(cv, w_conv, li, n, first_block):
    return pl.pallas_call(
        _conv_kernel,
        name="short_conv",
        grid=(BATCH,),
        in_specs=[
            pl.BlockSpec((n, 3 * D_CONV), lambda b: (first_block + b, 0)),
            pl.BlockSpec((1, 3, D_CONV), lambda b: (li, 0, 0)),
        ],
        out_specs=pl.BlockSpec((n, D_CONV), lambda b: (b, 0)),
        out_shape=jax.ShapeDtypeStruct((BATCH * n, D_CONV), BF16),
        scratch_shapes=[pltpu.VMEM((n + 2 * SUBLANES, D_CONV), F32)],
        compiler_params=_cparams("arbitrary"),
    )(cv, w_conv)


RPB_H = 2 * NA_KH - 1
RPB_W = 2 * NA_KW - 1
KEYS_LOC = NA_KH * GRID_W


def _bias_kernel(rpb_ref, o_ref):
    h = pl.program_id(0)
    qi = lax.broadcasted_iota(jnp.int32, (GRID_W, LANES), 0)
    lj = lax.broadcasted_iota(jnp.int32, (GRID_W, LANES), 1)
    kc = lj & (GRID_W - 1)
    hi = lj >= GRID_W
    d = kc - qi + (NA_KW - 1)
    cs = jnp.clip(qi - NA_KW // 2, 0, GRID_W - NA_KW)
    valid = (kc >= cs) & (kc < cs + NA_KW)
    tiles = []
    for a in range(RPB_H - 1):
        acc = jnp.zeros((GRID_W, LANES), F32)
        for b in range(RPB_W):
            va = rpb_ref[(h * RPB_H + a) * RPB_W + b]
            vb = rpb_ref[(h * RPB_H + a + 1) * RPB_W + b]
            acc = jnp.where(d == b, jnp.where(hi, vb, va), acc)
        tiles.append(jnp.where(valid, acc, NEG_INF))
    for cls in range(NA_KH):
        for m in range(NA_KH // 2):
            o_ref[0, cls, :, m * LANES:(m + 1) * LANES] = tiles[2 * m - cls + NA_KH - 1]


def _bias_table(rpb_flat):
    return pl.pallas_call(
        _bias_kernel,
        name="attn_bias_table",
        grid=(NA_HEADS,),
        in_specs=[pl.BlockSpec(memory_space=pltpu.SMEM)],
        out_specs=pl.BlockSpec((1, NA_KH, GRID_W, KEYS_LOC), lambda h: (h // 2, 0, h % 2, 0)),
        out_shape=jax.ShapeDtypeStruct((NA_HEADS // 2, NA_KH, 2 * GRID_W, KEYS_LOC), F32),
        compiler_params=_cparams("arbitrary"),
    )(rpb_flat)


_NT_DIMS = (((1,), (1,)), ((), ()))


def _softmax_pv(parts):
    m = None
    for s, _ in parts:
        ms = jnp.max(s, axis=1, keepdims=True)
        m = ms if m is None else jnp.maximum(m, ms)
    l = None
    o = None
    for s, v in parts:
        p = jnp.exp(s - m)
        ls = jnp.sum(p, axis=1, keepdims=True)
        os_ = jnp.dot(p.astype(BF16), v, preferred_element_type=F32)
        l = ls if l is None else l + ls
        o = os_ if o is None else o + os_
    return o / l


ATT_UNROLL = 4


def _attn_latent_kernel(q_ref, k_ref, v_ref, kc_ref, vc_ref, tab_ref, o_ref, ve_ref, vce_ref):
    lane = lax.broadcasted_iota(jnp.int32, (GRID_W, LANES), 1)
    lo = lane < HEAD_DIM
    ve_ref[:, 0:LANES] = v_ref[...]
    ve_ref[:, LANES:2 * LANES] = jnp.ones((SEQ, LANES), BF16)
    vce_ref[:, 0:LANES] = vc_ref[...]
    vce_ref[:, LANES:2 * LANES] = jnp.ones((CTX_LEN, LANES), BF16)
    kc = kc_ref[...]
    vce = vce_ref[...]

    def one_row(r):
        rs = jnp.clip(r - NA_KH // 2, 0, ROWS - NA_KH)
        cls = r - rs
        q = q_ref[pl.ds(pl.multiple_of(r * GRID_W, GRID_W), GRID_W), :]
        kl = k_ref[pl.ds(pl.multiple_of(rs * GRID_W, GRID_W), KEYS_LOC), :]
        vl = ve_ref[pl.ds(pl.multiple_of(rs * GRID_W, GRID_W), KEYS_LOC), :]
        zero = jnp.zeros_like(q)
        q2 = jnp.concatenate([jnp.where(lo, q, zero), jnp.where(lo, zero, q)], axis=0)
        s1 = lax.dot_general(q2, kl, _NT_DIMS, preferred_element_type=F32) + tab_ref[0, cls]
        s2 = lax.dot_general(q2, kc, _NT_DIMS, preferred_element_type=F32)
        m = jnp.maximum(jnp.max(s1, axis=1, keepdims=True), jnp.max(s2, axis=1, keepdims=True))
        p1 = jnp.exp(s1 - m).astype(BF16)
        p2 = jnp.exp(s2 - m).astype(BF16)
        oe = (jnp.dot(p1, vl, preferred_element_type=F32) + jnp.dot(p2, vce, preferred_element_type=F32))
        o = oe[:, 0:LANES] / oe[:, LANES:2 * LANES]
        o_ref[pl.ds(pl.multiple_of(r * GRID_W, GRID_W), GRID_W), :] = (
            jnp.where(lo, o[0:GRID_W], o[GRID_W:2 * GRID_W]).astype(BF16))

    def body(it, carry):
        for u in range(ATT_UNROLL):
            one_row(it * ATT_UNROLL + u)
        return carry

    lax.fori_loop(0, ROWS // ATT_UNROLL, body, 0)


def _attn_latent(qkv, table):
    npair = NA_HEADS // 2
    cblk = T_LAT // CTX_LEN
    return pl.pallas_call(
        _attn_latent_kernel,
        name="attn_latent",
        grid=(BATCH, npair),
        in_specs=[
            pl.BlockSpec((SEQ, LANES), lambda b, p: (b, p)),
            pl.BlockSpec((SEQ, LANES), lambda b, p: (b, npair + p)),
            pl.BlockSpec((SEQ, LANES), lambda b, p: (b, 2 * npair + p)),
            pl.BlockSpec((CTX_LEN, LANES), lambda b, p: (cblk + b, npair + p)),
            pl.BlockSpec((CTX_LEN, LANES), lambda b, p: (cblk + b, 2 * npair + p)),
            pl.BlockSpec((1, NA_KH, 2 * GRID_W, KEYS_LOC), lambda b, p: (p, 0, 0, 0)),
        ],
        out_specs=pl.BlockSpec((SEQ, LANES), lambda b, p: (b, p)),
        out_shape=jax.ShapeDtypeStruct((T_LAT, D_NA), BF16),
        scratch_shapes=[pltpu.VMEM((SEQ, 2 * LANES), BF16), pltpu.VMEM((CTX_LEN, 2 * LANES), BF16)],
        compiler_params=_cparams("arbitrary", "arbitrary"),
    )(qkv, qkv, qkv, qkv, qkv, table)


def _attn_ctx_kernel(q_ref, k_ref, v_ref, o_ref):
    lane = lax.broadcasted_iota(jnp.int32, (CTX_LEN, LANES), 1)
    lo = lane < HEAD_DIM
    q = q_ref[...]
    k = k_ref[...]
    v = v_ref[...]
    outs = []
    for hh in range(2):
        qm = jnp.where(lo if hh == 0 else jnp.logical_not(lo), q, jnp.zeros_like(q))
        s = lax.dot_general(qm, k, _NT_DIMS, preferred_element_type=F32)
        outs.append(_softmax_pv([(s, v)]))
    o_ref[...] = jnp.where(lo, outs[0], outs[1]).astype(BF16)


def _attn_ctx(qkv):
    npair = NA_HEADS // 2
    cblk = T_LAT // CTX_LEN
    return pl.pallas_call(
        _attn_ctx_kernel,
        name="attn_context",
        grid=(BATCH, npair),
        in_specs=[
            pl.BlockSpec((CTX_LEN, LANES), lambda b, p: (cblk + b, p)),
            pl.BlockSpec((CTX_LEN, LANES), lambda b, p: (cblk + b, npair + p)),
            pl.BlockSpec((CTX_LEN, LANES), lambda b, p: (cblk + b, 2 * npair + p)),
        ],
        out_specs=pl.BlockSpec((CTX_LEN, LANES), lambda b, p: (b, p)),
        out_shape=jax.ShapeDtypeStruct((T_CTX, D_NA), BF16),
        compiler_params=_cparams("arbitrary", "arbitrary"),
    )(qkv, qkv, qkv)


ROUTE_COLS = N_GROUPS + N_EXPERTS


def _to_token_major(ref, val):
    rows = val.shape[0]
    for j in range(D_MODEL // LANES):
        ref[pl.ds(j, rows, stride=SUBLANES), :] = val[:, j * LANES:(j + 1) * LANES]


def _from_token_major(ref, rows, j):
    return ref[pl.ds(j, rows, stride=SUBLANES), :]


def _outproj_kernel(x_ref, ff_ref, at_ref, cv_ref, mod_ref, n2_ref, wo_ref, wr_ref, br_ref,
                    x1_ref, hx_ref, route_ref):
    mix_in = jnp.concatenate([ff_ref[...], at_ref[...], cv_ref[...]], axis=-1)
    mix = jnp.dot(mix_in, wo_ref[0], preferred_element_type=F32)
    g1 = mod_ref[:, 2 * D_MODEL:3 * D_MODEL]
    x1 = x_ref[...] + g1 * mix
    x1_ref[...] = x1
    ms = jnp.mean(x1 * x1, axis=-1, keepdims=True)
    y = x1 * lax.rsqrt(ms + EPS) * n2_ref[0]
    sh2 = mod_ref[:, 3 * D_MODEL:4 * D_MODEL]
    sc2 = mod_ref[:, 4 * D_MODEL:5 * D_MODEL]
    hx = y * (1.0 + sc2) + sh2
    _to_token_major(hx_ref, hx)

    logits = jnp.dot(hx, wr_ref[0], preferred_element_type=F32, precision=lax.Precision.HIGHEST) + br_ref[0]
    tm = logits.shape[0]
    lane = lax.broadcasted_iota(jnp.int32, (tm, LANES), 1)
    big = jnp.int32(LANES)
    gl = jnp.where(lane < N_GROUPS, logits, -jnp.inf)
    gmax = jnp.max(gl, axis=1, keepdims=True)
    gidx = jnp.min(jnp.where(gl == gmax, lane, big), axis=1, keepdims=True)
    g_w = 1.0 / jnp.sum(jnp.exp(gl - gmax), axis=1, keepdims=True)
    e_lane = lane - N_GROUPS
    in_group = (e_lane >= 0) & (e_lane < N_EXPERTS) & ((e_lane >> 3) == gidx)
    es = jnp.where(in_group, logits, -jnp.inf)
    t1 = jnp.max(es, axis=1, keepdims=True)
    i1 = jnp.min(jnp.where(es == t1, lane, big), axis=1, keepdims=True)
    es2 = jnp.where(lane == i1, -jnp.inf, es)
    t2 = jnp.max(es2, axis=1, keepdims=True)
    i2 = jnp.min(jnp.where(es2 == t2, lane, big), axis=1, keepdims=True)
    dlt = jnp.exp(t2 - t1)
    w1 = g_w / (1.0 + dlt)
    w2 = g_w * dlt / (1.0 + dlt)
    e1 = (i1 - N_GROUPS).astype(F32)
    e2 = (i2 - N_GROUPS).astype(F32)
    route_ref[...] = jnp.where(lane == 0, e1, jnp.where(lane == 1, e2,
                               jnp.where(lane == 2, w1, jnp.where(lane == 3, w2, 0.0))))


def _out_projection(xt, ff, at, cvo, mod4, norm2, w_out_bf, w_r, b_r, li, n_tok):
    nt = n_tok // TM
    return pl.pallas_call(
        _outproj_kernel,
        name="out_projection",
        grid=(nt,),
        in_specs=[
            pl.BlockSpec((TM, D_MODEL), lambda i: (i, 0)),
            pl.BlockSpec((TM, D_FOURIER), lambda i: (i, 0)),
            pl.BlockSpec((TM, D_NA), lambda i: (i, 0)),
            pl.BlockSpec((TM, D_CONV), lambda i: (i, 0)),
            pl.BlockSpec((None, None, 1, 6 * D_MODEL), lambda i: (li, _mod_row(i, TM), 0, 0)),
            pl.BlockSpec((1, 1, D_MODEL), lambda i: (li, 0, 0)),
            pl.BlockSpec((1, D_MODEL, D_MODEL), lambda i: (li, 0, 0)),
            pl.BlockSpec((1, D_MODEL, LANES), lambda i: (li, 0, 0)),
            pl.BlockSpec((1, 1, LANES), lambda i: (li, 0, 0)),
        ],
        out_specs=[
            pl.BlockSpec((TM, D_MODEL), lambda i: (i, 0)),
            pl.BlockSpec((TM * SUBLANES, LANES), lambda i: (i, 0)),
            pl.BlockSpec((TM, LANES), lambda i: (i, 0)),
        ],
        out_shape=[
            jax.ShapeDtypeStruct((n_tok, D_MODEL), F32),
            jax.ShapeDtypeStruct((n_tok * SUBLANES, LANES), F32),
            jax.ShapeDtypeStruct((n_tok, LANES), F32),
        ],
        compiler_params=_cparams("arbitrary"),
    )(xt, ff, at, cvo, mod4, norm2.reshape(DEPTH, 1, D_MODEL), w_out_bf, w_r, b_r)


def _plan_kernel(route_ref, rank_ref, cnt_ref, carry_ref):
    i = pl.program_id(0)

    @pl.when(i == 0)
    def _():
        carry_ref[...] = jnp.zeros_like(carry_ref)

    r = route_ref[...]
    tm = r.shape[0]
    lane = lax.broadcasted_iota(jnp.int32, (tm, LANES), 1)
    oh1 = lane == r[:, 0:1].astype(jnp.int32)
    oh2 = lane == r[:, 1:2].astype(jnp.int32)
    oh = jnp.where(oh1 | oh2, 1.0, 0.0)
    row = lax.broadcasted_iota(jnp.int32, (tm, tm), 0)
    col = lax.broadcasted_iota(jnp.int32, (tm, tm), 1)
    tri = jnp.where(row > col, 1.0, 0.0).astype(BF16)
    cum = jnp.dot(tri, oh.astype(BF16), preferred_element_type=F32) + carry_ref[...]
    r1 = jnp.sum(jnp.where(oh1, cum, 0.0), axis=1, keepdims=True)
    r2 = jnp.sum(jnp.where(oh2, cum, 0.0), axis=1, keepdims=True)
    rank_ref[...] = jnp.where(lane == 0, r1, jnp.where(lane == 1, r2, 0.0))
    carry_ref[...] += jnp.sum(oh, axis=0, keepdims=True)
    cnt_ref[...] = jnp.broadcast_to(carry_ref[...], cnt_ref.shape)


def _plan(route, n_tok):
    nt = n_tok // TM
    return pl.pallas_call(
        _plan_kernel,
        name="moe_plan",
        grid=(nt,),
        in_specs=[pl.BlockSpec((TM, LANES), lambda i: (i, 0))],
        out_specs=[
            pl.BlockSpec((TM, LANES), lambda i: (i, 0)),
            pl.BlockSpec((SUBLANES, LANES), lambda i: (0, 0)),
        ],
        out_shape=[
            jax.ShapeDtypeStruct((n_tok, LANES), F32),
            jax.ShapeDtypeStruct((SUBLANES, LANES), F32),
        ],
        scratch_shapes=[pltpu.VMEM((1, LANES), F32)],
        compiler_params=_cparams("arbitrary"),
    )(route)


def _row_tile(ref, row):
    return ref.at[pl.ds(pl.multiple_of(row * SUBLANES, SUBLANES), SUBLANES), :]


def _dispatch_kernel(pos_ref, hx_ref, init_ref, xs_ref, sem):
    del init_ref
    i = pl.program_id(0)

    def body(t, carry):
        src = _row_tile(hx_ref, t)
        for j in range(2):
            pltpu.make_async_copy(src, _row_tile(xs_ref, pos_ref[(i * TD + t) * 2 + j]), sem.at[0]).start()
        return carry

    lax.fori_loop(0, TD, body, 0)
    nrow = TD * SUBLANES
    for j in range(2):
        pltpu.make_async_copy(hx_ref, xs_ref.at[pl.ds(0, nrow), :], sem.at[0]).wait()


def _dispatch(pos_flat, hx_tm, n_tok, n_rows):
    init = jnp.zeros((n_rows * SUBLANES, LANES), F32)
    return pl.pallas_call(
        _dispatch_kernel,
        name="moe_dispatch",
        grid_spec=pltpu.PrefetchScalarGridSpec(
            num_scalar_prefetch=1,
            grid=(n_tok // TD,),
            in_specs=[pl.BlockSpec((TD * SUBLANES, LANES), lambda i, pos: (i, 0)),
                      pl.BlockSpec(memory_space=pl.ANY)],
            out_specs=pl.BlockSpec(memory_space=pl.ANY),
            scratch_shapes=[pltpu.SemaphoreType.DMA((1,))],
        ),
        out_shape=jax.ShapeDtypeStruct((n_rows * SUBLANES, LANES), F32),
        input_output_aliases={2: 0},
        compiler_params=_cparams("arbitrary"),
    )(pos_flat, hx_tm, init)


def _moe_kernel(te_ref, na_ref, xs_ref, wg_ref, wu_ref, wd_ref, os_ref, wg_s, wu_s, wd_s):
    i = pl.program_id(0)
    prev = te_ref[jnp.maximum(i - 1, 0)]
    active = i < na_ref[0]

    @pl.when(active & ((i == 0) | (te_ref[i] != prev)))
    def _():
        wg_s[...] = wg_ref[0, 0].astype(BF16)
        wu_s[...] = wu_ref[0, 0].astype(BF16)
        wd_s[...] = wd_ref[0, 0].astype(BF16)

    @pl.when(active)
    def _():
        x = jnp.concatenate([_from_token_major(xs_ref, TMX, j).astype(BF16)
                             for j in range(D_MODEL // LANES)], axis=-1)
        g = jnp.dot(x, wg_s[...], preferred_element_type=F32)
        u = jnp.dot(x, wu_s[...], preferred_element_type=F32)
        h = (g * jax.nn.sigmoid(g) * u).astype(BF16)
        _to_token_major(os_ref, jnp.dot(h, wd_s[...], preferred_element_type=F32))

    @pl.when(jnp.logical_not(active))
    def _():
        os_ref[...] = jnp.zeros_like(os_ref)


def _moe(tile_expert, n_active, xs, w_gate, w_up, w_down, li, n_tiles):
    def row_map(i, te, na):
        return (jnp.minimum(i, na[0] - 1), 0)

    def w_map(i, te, na):
        return (li, te[jnp.minimum(i, na[0] - 1)], 0, 0)

    return pl.pallas_call(
        _moe_kernel,
        name="moe_experts",
        grid_spec=pltpu.PrefetchScalarGridSpec(
            num_scalar_prefetch=2,
            grid=(n_tiles,),
            in_specs=[
                pl.BlockSpec((TMX * SUBLANES, LANES), row_map),
                pl.BlockSpec((1, 1, D_MODEL, D_EXPERT), w_map),
                pl.BlockSpec((1, 1, D_MODEL, D_EXPERT), w_map),
                pl.BlockSpec((1, 1, D_EXPERT, D_MODEL), w_map),
            ],
            out_specs=pl.BlockSpec((TMX * SUBLANES, LANES), lambda i, te, na: (i, 0)),
            scratch_shapes=[
                pltpu.VMEM((D_MODEL, D_EXPERT), BF16),
                pltpu.VMEM((D_MODEL, D_EXPERT), BF16),
                pltpu.VMEM((D_EXPERT, D_MODEL), BF16),
            ],
        ),
        out_shape=jax.ShapeDtypeStruct((n_tiles * TMX * SUBLANES, LANES), F32),
        compiler_params=_cparams("arbitrary"),
    )(tile_expert, n_active, xs, w_gate, w_up, w_down)


def _combine_kernel(pos_ref, x1_ref, route_ref, mod_ref, nf_ref, os_ref, o_ref, buf0, buf1, sem, *, final_norm):
    i = pl.program_id(0)

    def body(t, carry):
        tok = i * TC + t
        pltpu.make_async_copy(_row_tile(os_ref, pos_ref[tok * 2]), _row_tile(buf0, t), sem.at[0]).start()
        pltpu.make_async_copy(_row_tile(os_ref, pos_ref[tok * 2 + 1]), _row_tile(buf1, t), sem.at[1]).start()
        return carry

    lax.fori_loop(0, TC, body, 0)
    nrow = TC * SUBLANES
    pltpu.make_async_copy(os_ref.at[pl.ds(0, nrow), :], buf0, sem.at[0]).wait()
    pltpu.make_async_copy(os_ref.at[pl.ds(0, nrow), :], buf1, sem.at[1]).wait()

    w1 = route_ref[:, 2:3]
    w2 = route_ref[:, 3:4]
    cols = []
    for j in range(D_MODEL // LANES):
        y = w1 * _from_token_major(buf0, TC, j) + w2 * _from_token_major(buf1, TC, j)
        g2 = mod_ref[:, 5 * D_MODEL + j * LANES:5 * D_MODEL + (j + 1) * LANES]
        cols.append(x1_ref[:, j * LANES:(j + 1) * LANES] + g2 * y)
    x2 = jnp.concatenate(cols, axis=-1)
    if final_norm:
        ms = jnp.mean(x2 * x2, axis=-1, keepdims=True)
        x2 = x2 * lax.rsqrt(ms + EPS) * nf_ref[...]
    o_ref[...] = x2


def _combine(pos_flat, x1, route, mod4, norm_final, osrt, li, n_tok, final_norm):
    def tmap(i, pos):
        return (i, 0)

    return pl.pallas_call(
        functools.partial(_combine_kernel, final_norm=final_norm),
        name="moe_combine",
        grid_spec=pltpu.PrefetchScalarGridSpec(
            num_scalar_prefetch=1,
            grid=(n_tok // TC,),
            in_specs=[
                pl.BlockSpec((TC, D_MODEL), tmap),
                pl.BlockSpec((TC, LANES), tmap),
                pl.BlockSpec((None, None, 1, 6 * D_MODEL), lambda i, pos: (li, _mod_row(i, TC), 0, 0)),
                pl.BlockSpec((1, D_MODEL), lambda i, pos: (0, 0)),
                pl.BlockSpec(memory_space=pl.ANY),
            ],
            out_specs=pl.BlockSpec((TC, D_MODEL), tmap),
            scratch_shapes=[
                pltpu.VMEM((TC * SUBLANES, LANES), F32),
                pltpu.VMEM((TC * SUBLANES, LANES), F32),
                pltpu.SemaphoreType.DMA((2,)),
            ],
        ),
        out_shape=jax.ShapeDtypeStruct((n_tok, D_MODEL), F32),
        compiler_params=_cparams("arbitrary"),
    )(pos_flat, x1, route, mod4, norm_final.reshape(1, D_MODEL), osrt)


def _moe_block(x1, hx_tm, route, mod4, norm_final, w_gate, w_up, w_down, li, n_tok, final_norm):
    n_tiles = (2 * n_tok) // TMX + N_EXPERTS
    n_rows = n_tiles * TMX
    rank, cnt = _plan(route, n_tok)
    cnt = cnt[0, :N_EXPERTS].astype(jnp.int32)
    padded = ((cnt + TMX - 1) // TMX) * TMX
    ends = jnp.cumsum(padded)
    offs = ends - padded
    eid = route[:, 0:2].astype(jnp.int32)
    pos = (offs[eid] + rank[:, 0:2].astype(jnp.int32)).reshape(-1)
    tile_start = jnp.arange(n_tiles, dtype=jnp.int32) * TMX
    tile_expert = jnp.minimum(jnp.sum((tile_start[:, None] >= ends[None, :]).astype(jnp.int32), axis=1),
                              N_EXPERTS - 1)
    n_active = (ends[-1:] // TMX).astype(jnp.int32)

    xs = _dispatch(pos, hx_tm, n_tok, n_rows)
    osrt = _moe(tile_expert, n_active, xs, w_gate, w_up, w_down, li, n_tiles)
    return _combine(pos, x1, route, mod4, norm_final, osrt, li, n_tok, final_norm)


def kernel(x, c, ctx, c_ctx, w_ada, b_ada, norm1, norm2, w_in, w_fourier, w_conv, rpb, w_out, w_rg, b_rg,
           w_re, b_re, w_gate, w_up, w_down, norm_final):
    xt = jnp.concatenate([x.reshape(T_LAT, D_MODEL), ctx.reshape(T_CTX, D_MODEL)], axis=0)
    cc = jnp.concatenate([c, c_ctx[None, :], jnp.zeros((MOD_ROWS - BATCH - 1, D_MODEL), F32)], axis=0)
    mod4 = _modulation(cc, w_ada, b_ada).reshape(DEPTH, MOD_ROWS, 1, 6 * D_MODEL)

    w_in_bf = w_in.astype(BF16)
    w_out_bf = w_out.astype(BF16)
    pad = jnp.zeros((DEPTH, D_MODEL, LANES - ROUTE_COLS), F32)
    w_r = jnp.concatenate([w_rg, w_re, pad], axis=-1)
    b_r = jnp.concatenate([b_rg, b_re, pad[:, 0, :]], axis=-1).reshape(DEPTH, 1, LANES)
    dft_ch = jnp.asarray(_channel_dft(), dtype=F32).astype(BF16)
    ctx_blk = T_LAT // CTX_LEN

    for li in range(DEPTH):
        last = li == DEPTH - 1
        yf, qkv, cv = _in_projection(xt, mod4, norm1, w_in_bf, dft_ch, li)
        table = _bias_table(rpb[li].reshape(-1))
        ff = _fourier_mix(yf, w_fourier, li, SEQ, 0)
        cvo = _short_conv(cv, w_conv, li, SEQ, 0)
        at = _attn_latent(qkv, table)
        if last:
            n_tok = T_LAT
        else:
            n_tok = T_ALL
            ff = jnp.concatenate([ff, _fourier_mix(yf, w_fourier, li, CTX_LEN, ctx_blk)], axis=0)
            cvo = jnp.concatenate([cvo, _short_conv(cv, w_conv, li, CTX_LEN, ctx_blk)], axis=0)
            at = jnp.concatenate([at, _attn_ctx(qkv)], axis=0)
        x1, hx_tm, route = _out_projection(xt, ff, at, cvo, mod4, norm2, w_out_bf, w_r, b_r, li, n_tok)
        xt_new = _moe_block(x1, hx_tm, route, mod4, norm_final, w_gate, w_up, w_down, li, n_tok, last)
        if last:
            return xt_new.reshape(BATCH, SEQ, D_MODEL)
        xt = xt_new
```

```python
import functools
import math

import numpy as np
import jax
import jax.numpy as jnp
from jax import lax
from jax.experimental import pallas as pl
from jax.experimental.pallas import tpu as pltpu

F32 = jnp.float32
BF16 = jnp.bfloat16

D_MODEL = 1024
BATCH = 8
SEQ = 2048
DEPTH = 2
GRID_W = 64
ROWS = SEQ // GRID_W
CTX_LEN = 256
T_LAT = BATCH * SEQ
T_CTX = BATCH * CTX_LEN
T_ALL = T_LAT + T_CTX

D_FOURIER = 256
D_FG = 64
HEAD_DIM = 64
NA_HEADS = 8
D_NA = NA_HEADS * HEAD_DIM
D_CONV = 256
D_IN_PROJ = D_FOURIER + 3 * D_NA + 3 * D_CONV
NA_KH = 8
NA_KW = 16
N_GROUPS = 4
EXPERTS_PER_GROUP = 8
N_EXPERTS = 32
D_EXPERT = 512
EPS = 1e-6
NEG_INF = -1e30

LANES = 128
SUBLANES = 8
MOD_ROWS = 16
TM = 512
TMX = 256
TD = 512
TC = 256
VMEM_LIMIT = 56 * 1024 * 1024


def _cparams(*sem):
    return pltpu.CompilerParams(dimension_semantics=sem, vmem_limit_bytes=VMEM_LIMIT)


def _ada_kernel(cc_ref, w_ref, b_ref, o_ref):
    s = cc_ref[...]
    s = s * jax.nn.sigmoid(s)
    acc = jnp.dot(s.astype(BF16), w_ref[0].astype(BF16), preferred_element_type=F32)
    o_ref[0] = acc + b_ref[0]


def _modulation(cc, w_ada, b_ada):
    tn = 1536
    nj = 6 * D_MODEL // tn
    return pl.pallas_call(
        _ada_kernel,
        name="ada_modulation",
        grid=(DEPTH, nj),
        in_specs=[
            pl.BlockSpec((MOD_ROWS, D_MODEL), lambda l, j: (0, 0)),
            pl.BlockSpec((1, D_MODEL, tn), lambda l, j: (l, 0, j)),
            pl.BlockSpec((1, 1, tn), lambda l, j: (l, 0, j)),
        ],
        out_specs=pl.BlockSpec((1, MOD_ROWS, tn), lambda l, j: (l, 0, j)),
        out_shape=jax.ShapeDtypeStruct((DEPTH, MOD_ROWS, 6 * D_MODEL), F32),
        compiler_params=_cparams("arbitrary", "arbitrary"),
    )(cc, w_ada, b_ada.reshape(DEPTH, 1, 6 * D_MODEL))


def _mod_row(i, tile):
    return jnp.minimum((i * tile) // SEQ, BATCH)


def _inproj_kernel(x_ref, mod_ref, n1_ref, w_ref, dft_ref, yf_ref, qkv_ref, cv_ref):
    x = x_ref[...]
    ms = jnp.mean(x * x, axis=-1, keepdims=True)
    y = x * lax.rsqrt(ms + EPS) * n1_ref[0]
    sh = mod_ref[:, 0:D_MODEL]
    sc = mod_ref[:, D_MODEL:2 * D_MODEL]
    h = y * (1.0 + sc) + sh
    u = jnp.dot(h.astype(BF16), w_ref[0], preferred_element_type=F32)
    yf = jnp.dot(u[:, 0:D_FOURIER].astype(BF16), dft_ref[...], preferred_element_type=F32)
    yf_ref[...] = yf.astype(BF16)
    q0 = D_FOURIER
    qkv_ref[:, 0:D_NA] = (u[:, q0:q0 + D_NA] * (1.0 / math.sqrt(HEAD_DIM))).astype(BF16)
    qkv_ref[:, D_NA:3 * D_NA] = u[:, q0 + D_NA:q0 + 3 * D_NA].astype(BF16)
    cv_ref[...] = u[:, q0 + 3 * D_NA:D_IN_PROJ]


def _in_projection(xt, mod4, norm1, w_in_bf, dft_ch, li):
    nt = T_ALL // TM
    return pl.pallas_call(
        _inproj_kernel,
        name="in_projection",
        grid=(nt,),
        in_specs=[
            pl.BlockSpec((TM, D_MODEL), lambda i: (i, 0)),
            pl.BlockSpec((None, None, 1, 6 * D_MODEL), lambda i: (li, _mod_row(i, TM), 0, 0)),
            pl.BlockSpec((1, 1, D_MODEL), lambda i: (li, 0, 0)),
            pl.BlockSpec((1, D_MODEL, D_IN_PROJ), lambda i: (li, 0, 0)),
            pl.BlockSpec((D_FOURIER, 2 * D_FOURIER), lambda i: (0, 0)),
        ],
        out_specs=[
            pl.BlockSpec((TM, 2 * D_FOURIER), lambda i: (i, 0)),
            pl.BlockSpec((TM, 3 * D_NA), lambda i: (i, 0)),
            pl.BlockSpec((TM, 3 * D_CONV), lambda i: (i, 0)),
        ],
        out_shape=[
            jax.ShapeDtypeStruct((T_ALL, 2 * D_FOURIER), BF16),
            jax.ShapeDtypeStruct((T_ALL, 3 * D_NA), BF16),
            jax.ShapeDtypeStruct((T_ALL, 3 * D_CONV), F32),
        ],
        compiler_params=_cparams("arbitrary"),
    )(xt, mod4, norm1.reshape(DEPTH, 1, D_MODEL), w_in_bf, dft_ch)


def _dft_tables(n):
    k = np.arange(n, dtype=np.int64)
    ang = 2.0 * np.pi * ((k[:, None] * k[None, :]) % n).astype(np.float64) / n
    s = 1.0 / math.sqrt(n)
    return np.cos(ang) * s, np.sin(ang) * s


def _channel_dft():
    c, s = _dft_tables(D_FG)
    eye = np.eye(D_FOURIER // D_FG)
    return np.concatenate([np.kron(eye, c), np.kron(eye, s)], axis=1)


def _fourier_kernel(cn_ref, sn_ref, y_ref, wf_ref, o_ref):
    z = (jnp.dot(cn_ref[...], y_ref[:, 0:D_FOURIER], preferred_element_type=F32)
         - jnp.dot(sn_ref[...], y_ref[:, D_FOURIER:2 * D_FOURIER], preferred_element_type=F32))
    o_ref[...] = jnp.dot(z.astype(BF16), wf_ref[0].astype(BF16), preferred_element_type=F32).astype(BF16)


def _fourier_mix(yf, w_fourier, li, n, first_block):
    tk = min(n, 512)
    nk = n // tk
    cn, sn = _dft_tables(n)
    cn = jnp.asarray(cn, dtype=F32).astype(BF16)
    sn = jnp.asarray(sn, dtype=F32).astype(BF16)
    return pl.pallas_call(
        _fourier_kernel,
        name="fourier_mix",
        grid=(nk, BATCH),
        in_specs=[
            pl.BlockSpec((tk, n), lambda k, b: (k, 0)),
            pl.BlockSpec((tk, n), lambda k, b: (k, 0)),
            pl.BlockSpec((n, 2 * D_FOURIER), lambda k, b: (first_block + b, 0)),
            pl.BlockSpec((1, D_FOURIER, D_FOURIER), lambda k, b: (li, 0, 0)),
        ],
        out_specs=pl.BlockSpec((tk, D_FOURIER), lambda k, b: (b * nk + k, 0)),
        out_shape=jax.ShapeDtypeStruct((BATCH * n, D_FOURIER), BF16),
        compiler_params=_cparams("arbitrary", "arbitrary"),
    )(cn, sn, yf, w_fourier)


def _conv_kernel(cv_ref, wc_ref, o_ref, pad_ref):
    n = cv_ref.shape[0]
    gb = cv_ref[:, 0:D_CONV]
    g = cv_ref[:, D_CONV:2 * D_CONV] * cv_ref[:, 2 * D_CONV:3 * D_CONV]
    zero = jnp.zeros((SUBLANES, D_CONV), F32)
    pad_ref[0:SUBLANES, :] = zero
    pad_ref[n + SUBLANES:n + 2 * SUBLANES, :] = zero
    pad_ref[SUBLANES:n + SUBLANES, :] = g
    prev = pad_ref[SUBLANES - 1:n + SUBLANES - 1, :]
    nxt = pad_ref[SUBLANES + 1:n + SUBLANES + 1, :]
    w = wc_ref[0]
    o_ref[...] = (gb * (w[0:1, :] * prev + w[1:2, :] * g + w[2:3, :] * nxt)).astype(BF16)


def _short_conv(cv, w_conv, li, n, first_block):
    return pl.pallas_call(
        _conv_kernel,
        name="short_conv",
        grid=(BATCH,),
        in_specs=[
            pl.BlockSpec((n, 3 * D_CONV), lambda b: (first_block + b, 0)),
            pl.BlockSpec((1, 3, D_CONV), lambda b: (li, 0, 0)),
        ],
        out_specs=pl.BlockSpec((n, D_CONV), lambda b: (b, 0)),
        out_shape=jax.ShapeDtypeStruct((BATCH * n, D_CONV), BF16),
        scratch_shapes=[pltpu.VMEM((n + 2 * SUBLANES, D_CONV), F32)],
        compiler_params=_cparams("arbitrary"),
    )(cv, w_conv)


RPB_H = 2 * NA_KH - 1
RPB_W = 2 * NA_KW - 1
KEYS_LOC = NA_KH * GRID_W


def _bias_kernel(rpb_ref, o_ref):
    h = pl.program_id(0)
    qi = lax.broadcasted_iota(jnp.int32, (GRID_W, LANES), 0)
    lj = lax.broadcasted_iota(jnp.int32, (GRID_W, LANES), 1)
    kc = lj & (GRID_W - 1)
    hi = lj >= GRID_W
    d = kc - qi + (NA_KW - 1)
    cs = jnp.clip(qi - NA_KW // 2, 0, GRID_W - NA_KW)
    valid = (kc >= cs) & (kc < cs + NA_KW)
    tiles = []
    for a in range(RPB_H - 1):
        acc = jnp.zeros((GRID_W, LANES), F32)
        for b in range(RPB_W):
            va = rpb_ref[(h * RPB_H + a) * RPB_W + b]
            vb = rpb_ref[(h * RPB_H + a + 1) * RPB_W + b]
            acc = jnp.where(d == b, jnp.where(hi, vb, va), acc)
        tiles.append(jnp.where(valid, acc, NEG_INF))
    for cls in range(NA_KH):
        for m in range(NA_KH // 2):
            o_ref[0, cls, :, m * LANES:(m + 1) * LANES] = tiles[2 * m - cls + NA_KH - 1]


def _bias_table(rpb_flat):
    return pl.pallas_call(
        _bias_kernel,
        name="attn_bias_table",
        grid=(NA_HEADS,),
        in_specs=[pl.BlockSpec(memory_space=pltpu.SMEM)],
        out_specs=pl.BlockSpec((1, NA_KH, GRID_W, KEYS_LOC), lambda h: (h // 2, 0, h % 2, 0)),
        out_shape=jax.ShapeDtypeStruct((NA_HEADS // 2, NA_KH, 2 * GRID_W, KEYS_LOC), F32),
        compiler_params=_cparams("arbitrary"),
    )(rpb_flat)


_NT_DIMS = (((1,), (1,)), ((), ()))


def _softmax_pv(parts):
    m = None
    for s, _ in parts:
        ms = jnp.max(s, axis=1, keepdims=True)
        m = ms if m is None else jnp.maximum(m, ms)
    l = None
    o = None
    for s, v in parts:
        p = jnp.exp(s - m)
        ls = jnp.sum(p, axis=1, keepdims=True)
        os_ = jnp.dot(p.astype(BF16), v, preferred_element_type=F32)
        l = ls if l is None else l + ls
        o = os_ if o is None else o + os_
    return o / l


ATT_UNROLL = 4


def _attn_latent_kernel(q_ref, k_ref, v_ref, kc_ref, vc_ref, tab_ref, o_ref, ve_ref, vce_ref):
    lane = lax.broadcasted_iota(jnp.int32, (GRID_W, LANES), 1)
    lo = lane < HEAD_DIM
    ve_ref[:, 0:LANES] = v_ref[...]
    ve_ref[:, LANES:2 * LANES] = jnp.ones((SEQ, LANES), BF16)
    vce_ref[:, 0:LANES] = vc_ref[...]
    vce_ref[:, LANES:2 * LANES] = jnp.ones((CTX_LEN, LANES), BF16)
    kc = kc_ref[...]
    vce = vce_ref[...]

    def one_row(r):
        rs = jnp.clip(r - NA_KH // 2, 0, ROWS - NA_KH)
        cls = r - rs
        q = q_ref[pl.ds(pl.multiple_of(r * GRID_W, GRID_W), GRID_W), :]
        kl = k_ref[pl.ds(pl.multiple_of(rs * GRID_W, GRID_W), KEYS_LOC), :]
        vl = ve_ref[pl.ds(pl.multiple_of(rs * GRID_W, GRID_W), KEYS_LOC), :]
        zero = jnp.zeros_like(q)
        q2 = jnp.concatenate([jnp.where(lo, q, zero), jnp.where(lo, zero, q)], axis=0)
        s1 = lax.dot_general(q2, kl, _NT_DIMS, preferred_element_type=F32) + tab_ref[0, cls]
        s2 = lax.dot_general(q2, kc, _NT_DIMS, preferred_element_type=F32)
        m = jnp.maximum(jnp.max(s1, axis=1, keepdims=True), jnp.max(s2, axis=1, keepdims=True))
        p1 = jnp.exp(s1 - m).astype(BF16)
        p2 = jnp.exp(s2 - m).astype(BF16)
        oe = (jnp.dot(p1, vl, preferred_element_type=F32) + jnp.dot(p2, vce, preferred_element_type=F32))
        o = oe[:, 0:LANES] / oe[:, LANES:2 * LANES]
        o_ref[pl.ds(pl.multiple_of(r * GRID_W, GRID_W), GRID_W), :] = (
            jnp.where(lo, o[0:GRID_W], o[GRID_W:2 * GRID_W]).astype(BF16))

    def body(it, carry):
        for u in range(ATT_UNROLL):
            one_row(it * ATT_UNROLL + u)
        return carry

    lax.fori_loop(0, ROWS // ATT_UNROLL, body, 0)


def _attn_latent(qkv, table):
    npair = NA_HEADS // 2
    cblk = T_LAT // CTX_LEN
    return pl.pallas_call(
        _attn_latent_kernel,
        name="attn_latent",
        grid=(BATCH, npair),
        in_specs=[
            pl.BlockSpec((SEQ, LANES), lambda b, p: (b, p)),
            pl.BlockSpec((SEQ, LANES), lambda b, p: (b, npair + p)),
            pl.BlockSpec((SEQ, LANES), lambda b, p: (b, 2 * npair + p)),
            pl.BlockSpec((CTX_LEN, LANES), lambda b, p: (cblk + b, npair + p)),
            pl.BlockSpec((CTX_LEN, LANES), lambda b, p: (cblk + b, 2 * npair + p)),
            pl.BlockSpec((1, NA_KH, 2 * GRID_W, KEYS_LOC), lambda b, p: (p, 0, 0, 0)),
        ],
        out_specs=pl.BlockSpec((SEQ, LANES), lambda b, p: (b, p)),
        out_shape=jax.ShapeDtypeStruct((T_LAT, D_NA), BF16),
        scratch_shapes=[pltpu.VMEM((SEQ, 2 * LANES), BF16), pltpu.VMEM((CTX_LEN, 2 * LANES), BF16)],
        compiler_params=_cparams("arbitrary", "arbitrary"),
    )(qkv, qkv, qkv, qkv, qkv, table)


def _attn_ctx_kernel(q_ref, k_ref, v_ref, o_ref):
    lane = lax.broadcasted_iota(jnp.int32, (CTX_LEN, LANES), 1)
    lo = lane < HEAD_DIM
    q = q_ref[...]
    k = k_ref[...]
    v = v_ref[...]
    outs = []
    for hh in range(2):
        qm = jnp.where(lo if hh == 0 else jnp.logical_not(lo), q, jnp.zeros_like(q))
        s = lax.dot_general(qm, k, _NT_DIMS, preferred_element_type=F32)
        outs.append(_softmax_pv([(s, v)]))
    o_ref[...] = jnp.where(lo, outs[0], outs[1]).astype(BF16)


def _attn_ctx(qkv):
    npair = NA_HEADS // 2
    cblk = T_LAT // CTX_LEN
    return pl.pallas_call(
        _attn_ctx_kernel,
        name="attn_context",
        grid=(BATCH, npair),
        in_specs=[
            pl.BlockSpec((CTX_LEN, LANES), lambda b, p: (cblk + b, p)),
            pl.BlockSpec((CTX_LEN, LANES), lambda b, p: (cblk + b, npair + p)),
            pl.BlockSpec((CTX_LEN, LANES), lambda b, p: (cblk + b, 2 * npair + p)),
        ],
        out_specs=pl.BlockSpec((CTX_LEN, LANES), lambda b, p: (b, p)),
        out_shape=jax.ShapeDtypeStruct((T_CTX, D_NA), BF16),
        compiler_params=_cparams("arbitrary", "arbitrary"),
    )(qkv, qkv, qkv)


ROUTE_COLS = N_GROUPS + N_EXPERTS


def _to_token_major(ref, val):
    rows = val.shape[0]
    for j in range(D_MODEL // LANES):
        ref[pl.ds(j, rows, stride=SUBLANES), :] = val[:, j * LANES:(j + 1) * LANES]


def _from_token_major(ref, rows, j):
    return ref[pl.ds(j, rows, stride=SUBLANES), :]


def _outproj_kernel(x_ref, ff_ref, at_ref, cv_ref, mod_ref, n2_ref, wo_ref, wr_ref, br_ref,
                    x1_ref, hx_ref, route_ref):
    mix_in = jnp.concatenate([ff_ref[...], at_ref[...], cv_ref[...]], axis=-1)
    mix = jnp.dot(mix_in, wo_ref[0], preferred_element_type=F32)
    g1 = mod_ref[:, 2 * D_MODEL:3 * D_MODEL]
    x1 = x_ref[...] + g1 * mix
    x1_ref[...] = x1
    ms = jnp.mean(x1 * x1, axis=-1, keepdims=True)
    y = x1 * lax.rsqrt(ms + EPS) * n2_ref[0]
    sh2 = mod_ref[:, 3 * D_MODEL:4 * D_MODEL]
    sc2 = mod_ref[:, 4 * D_MODEL:5 * D_MODEL]
    hx = y * (1.0 + sc2) + sh2
    _to_token_major(hx_ref, hx)

    hx_hi = hx.astype(BF16)
    hx_lo = (hx - hx_hi.astype(F32)).astype(BF16)
    part = (jnp.dot(hx_hi, wr_ref[0], preferred_element_type=F32)
            + jnp.dot(hx_lo, wr_ref[0], preferred_element_type=F32))
    logits = part[:, 0:LANES] + part[:, LANES:2 * LANES] + br_ref[0]
    tm = logits.shape[0]
    lane = lax.broadcasted_iota(jnp.int32, (tm, LANES), 1)
    big = jnp.int32(LANES)
    gl = jnp.where(lane < N_GROUPS, logits, -jnp.inf)
    gmax = jnp.max(gl, axis=1, keepdims=True)
    gidx = jnp.min(jnp.where(gl == gmax, lane, big), axis=1, keepdims=True)
    g_w = 1.0 / jnp.sum(jnp.exp(gl - gmax), axis=1, keepdims=True)
    e_lane = lane - N_GROUPS
    in_group = (e_lane >= 0) & (e_lane < N_EXPERTS) & ((e_lane >> 3) == gidx)
    es = jnp.where(in_group, logits, -jnp.inf)
    t1 = jnp.max(es, axis=1, keepdims=True)
    i1 = jnp.min(jnp.where(es == t1, lane, big), axis=1, keepdims=True)
    es2 = jnp.where(lane == i1, -jnp.inf, es)
    t2 = jnp.max(es2, axis=1, keepdims=True)
    i2 = jnp.min(jnp.where(es2 == t2, lane, big), axis=1, keepdims=True)
    dlt = jnp.exp(t2 - t1)
    w1 = g_w / (1.0 + dlt)
    w2 = g_w * dlt / (1.0 + dlt)
    e1 = (i1 - N_GROUPS).astype(F32)
    e2 = (i2 - N_GROUPS).astype(F32)
    route_ref[...] = jnp.where(lane == 0, e1, jnp.where(lane == 1, e2,
                               jnp.where(lane == 2, w1, jnp.where(lane == 3, w2, 0.0))))


def _out_projection(xt, ff, at, cvo, mod4, norm2, w_out_bf, w_r, b_r, li, n_tok):
    nt = n_tok // TM
    return pl.pallas_call(
        _outproj_kernel,
        name="out_projection",
        grid=(nt,),
        in_specs=[
            pl.BlockSpec((TM, D_MODEL), lambda i: (i, 0)),
            pl.BlockSpec((TM, D_FOURIER), lambda i: (i, 0)),
            pl.BlockSpec((TM, D_NA), lambda i: (i, 0)),
            pl.BlockSpec((TM, D_CONV), lambda i: (i, 0)),
            pl.BlockSpec((None, None, 1, 6 * D_MODEL), lambda i: (li, _mod_row(i, TM), 0, 0)),
            pl.BlockSpec((1, 1, D_MODEL), lambda i: (li, 0, 0)),
            pl.BlockSpec((1, D_MODEL, D_MODEL), lambda i: (li, 0, 0)),
            pl.BlockSpec((1, D_MODEL, 2 * LANES), lambda i: (li, 0, 0)),
            pl.BlockSpec((1, 1, LANES), lambda i: (li, 0, 0)),
        ],
        out_specs=[
            pl.BlockSpec((TM, D_MODEL), lambda i: (i, 0)),
            pl.BlockSpec((TM * SUBLANES, LANES), lambda i: (i, 0)),
            pl.BlockSpec((TM, LANES), lambda i: (i, 0)),
        ],
        out_shape=[
            jax.ShapeDtypeStruct((n_tok, D_MODEL), F32),
            jax.ShapeDtypeStruct((n_tok * SUBLANES, LANES), F32),
            jax.ShapeDtypeStruct((n_tok, LANES), F32),
        ],
        compiler_params=_cparams("arbitrary"),
    )(xt, ff, at, cvo, mod4, norm2.reshape(DEPTH, 1, D_MODEL), w_out_bf, w_r, b_r)


def _plan_kernel(route_ref, rank_ref, cnt_ref, carry_ref):
    i = pl.program_id(0)

    @pl.when(i == 0)
    def _():
        carry_ref[...] = jnp.zeros_like(carry_ref)

    r = route_ref[...]
    tm = r.shape[0]
    lane = lax.broadcasted_iota(jnp.int32, (tm, LANES), 1)
    oh1 = lane == r[:, 0:1].astype(jnp.int32)
    oh2 = lane == r[:, 1:2].astype(jnp.int32)
    oh = jnp.where(oh1 | oh2, 1.0, 0.0)
    row = lax.broadcasted_iota(jnp.int32, (tm, tm), 0)
    col = lax.broadcasted_iota(jnp.int32, (tm, tm), 1)
    tri = jnp.where(row > col, 1.0, 0.0).astype(BF16)
    cum = jnp.dot(tri, oh.astype(BF16), preferred_element_type=F32) + carry_ref[...]
    r1 = jnp.sum(jnp.where(oh1, cum, 0.0), axis=1, keepdims=True)
    r2 = jnp.sum(jnp.where(oh2, cum, 0.0), axis=1, keepdims=True)
    rank_ref[...] = jnp.where(lane == 0, r1, jnp.where(lane == 1, r2, 0.0))
    carry_ref[...] += jnp.sum(oh, axis=0, keepdims=True)
    cnt_ref[...] = jnp.broadcast_to(carry_ref[...], cnt_ref.shape)


def _plan(route, n_tok):
    nt = n_tok // TM
    return pl.pallas_call(
        _plan_kernel,
        name="moe_plan",
        grid=(nt,),
        in_specs=[pl.BlockSpec((TM, LANES), lambda i: (i, 0))],
        out_specs=[
            pl.BlockSpec((TM, LANES), lambda i: (i, 0)),
            pl.BlockSpec((SUBLANES, LANES), lambda i: (0, 0)),
        ],
        out_shape=[
            jax.ShapeDtypeStruct((n_tok, LANES), F32),
            jax.ShapeDtypeStruct((SUBLANES, LANES), F32),
        ],
        scratch_shapes=[pltpu.VMEM((1, LANES), F32)],
        compiler_params=_cparams("arbitrary"),
    )(route)


def _row_tile(ref, row):
    return ref.at[pl.ds(pl.multiple_of(row * SUBLANES, SUBLANES), SUBLANES), :]


def _dispatch_kernel(pos_ref, hx_ref, init_ref, xs_ref, sem):
    del init_ref
    i = pl.program_id(0)

    def body(t, carry):
        src = _row_tile(hx_ref, t)
        for j in range(2):
            pltpu.make_async_copy(src, _row_tile(xs_ref, pos_ref[(i * TD + t) * 2 + j]), sem.at[0]).start(priority=j)
        return carry

    lax.fori_loop(0, TD, body, 0)
    nrow = TD * SUBLANES
    for j in range(2):
        pltpu.make_async_copy(hx_ref, xs_ref.at[pl.ds(0, nrow), :], sem.at[0]).wait()


def _dispatch(pos_flat, hx_tm, n_tok, n_rows):
    init = jnp.zeros((n_rows * SUBLANES, LANES), F32)
    return pl.pallas_call(
        _dispatch_kernel,
        name="moe_dispatch",
        grid_spec=pltpu.PrefetchScalarGridSpec(
            num_scalar_prefetch=1,
            grid=(n_tok // TD,),
            in_specs=[pl.BlockSpec((TD * SUBLANES, LANES), lambda i, pos: (i, 0)),
                      pl.BlockSpec(memory_space=pl.ANY)],
            out_specs=pl.BlockSpec(memory_space=pl.ANY),
            scratch_shapes=[pltpu.SemaphoreType.DMA((1,))],
        ),
        out_shape=jax.ShapeDtypeStruct((n_rows * SUBLANES, LANES), F32),
        input_output_aliases={2: 0},
        compiler_params=_cparams("arbitrary"),
    )(pos_flat, hx_tm, init)


def _moe_kernel(te_ref, na_ref, xs_ref, wg_ref, wu_ref, wd_ref, os_ref, wg_s, wu_s, wd_s):
    i = pl.program_id(0)
    prev = te_ref[jnp.maximum(i - 1, 0)]
    active = i < na_ref[0]

    @pl.when(active & ((i == 0) | (te_ref[i] != prev)))
    def _():
        wg_s[...] = wg_ref[0, 0].astype(BF16)
        wu_s[...] = wu_ref[0, 0].astype(BF16)
        wd_s[...] = wd_ref[0, 0].astype(BF16)

    @pl.when(active)
    def _():
        x = jnp.concatenate([_from_token_major(xs_ref, TMX, j).astype(BF16)
                             for j in range(D_MODEL // LANES)], axis=-1)
        g = jnp.dot(x, wg_s[...], preferred_element_type=F32)
        u = jnp.dot(x, wu_s[...], preferred_element_type=F32)
        h = (g * jax.nn.sigmoid(g) * u).astype(BF16)
        _to_token_major(os_ref, jnp.dot(h, wd_s[...], preferred_element_type=F32))

    @pl.when(jnp.logical_not(active))
    def _():
        os_ref[...] = jnp.zeros_like(os_ref)


def _moe(tile_expert, n_active, xs, w_gate, w_up, w_down, li, n_tiles):
    def row_map(i, te, na):
        return (jnp.minimum(i, na[0] - 1), 0)

    def w_map(i, te, na):
        return (li, te[jnp.minimum(i, na[0] - 1)], 0, 0)

    return pl.pallas_call(
        _moe_kernel,
        name="moe_experts",
        grid_spec=pltpu.PrefetchScalarGridSpec(
            num_scalar_prefetch=2,
            grid=(n_tiles,),
            in_specs=[
                pl.BlockSpec((TMX * SUBLANES, LANES), row_map),
                pl.BlockSpec((1, 1, D_MODEL, D_EXPERT), w_map),
                pl.BlockSpec((1, 1, D_MODEL, D_EXPERT), w_map),
                pl.BlockSpec((1, 1, D_EXPERT, D_MODEL), w_map),
            ],
            out_specs=pl.BlockSpec((TMX * SUBLANES, LANES), lambda i, te, na: (i, 0)),
            scratch_shapes=[
                pltpu.VMEM((D_MODEL, D_EXPERT), BF16),
                pltpu.VMEM((D_MODEL, D_EXPERT), BF16),
                pltpu.VMEM((D_EXPERT, D_MODEL), BF16),
            ],
        ),
        out_shape=jax.ShapeDtypeStruct((n_tiles * TMX * SUBLANES, LANES), F32),
        compiler_params=_cparams("arbitrary"),
    )(tile_expert, n_active, xs, w_gate, w_up, w_down)


def _combine_kernel(pos_ref, x1_ref, route_ref, mod_ref, nf_ref, os_ref, o_ref, buf0, buf1, sem, *, final_norm):
    i = pl.program_id(0)

    def body(t, carry):
        tok = i * TC + t
        pltpu.make_async_copy(_row_tile(os_ref, pos_ref[tok * 2]), _row_tile(buf0, t), sem.at[0]).start(priority=0)
        pltpu.make_async_copy(_row_tile(os_ref, pos_ref[tok * 2 + 1]), _row_tile(buf1, t), sem.at[1]).start(priority=1)
        return carry

    lax.fori_loop(0, TC, body, 0)
    nrow = TC * SUBLANES
    pltpu.make_async_copy(os_ref.at[pl.ds(0, nrow), :], buf0, sem.at[0]).wait()
    pltpu.make_async_copy(os_ref.at[pl.ds(0, nrow), :], buf1, sem.at[1]).wait()

    w1 = route_ref[:, 2:3]
    w2 = route_ref[:, 3:4]
    cols = []
    for j in range(D_MODEL // LANES):
        y = w1 * _from_token_major(buf0, TC, j) + w2 * _from_token_major(buf1, TC, j)
        g2 = mod_ref[:, 5 * D_MODEL + j * LANES:5 * D_MODEL + (j + 1) * LANES]
        cols.append(x1_ref[:, j * LANES:(j + 1) * LANES] + g2 * y)
    x2 = jnp.concatenate(cols, axis=-1)
    if final_norm:
        ms = jnp.mean(x2 * x2, axis=-1, keepdims=True)
        x2 = x2 * lax.rsqrt(ms + EPS) * nf_ref[...]
    o_ref[...] = x2


def _combine(pos_flat, x1, route, mod4, norm_final, osrt, li, n_tok, final_norm):
    def tmap(i, pos):
        return (i, 0)

    return pl.pallas_call(
        functools.partial(_combine_kernel, final_norm=final_norm),
        name="moe_combine",
        grid_spec=pltpu.PrefetchScalarGridSpec(
            num_scalar_prefetch=1,
            grid=(n_tok // TC,),
            in_specs=[
                pl.BlockSpec((TC, D_MODEL), tmap),
                pl.BlockSpec((TC, LANES), tmap),
                pl.BlockSpec((None, None, 1, 6 * D_MODEL), lambda i, pos: (li, _mod_row(i, TC), 0, 0)),
                pl.BlockSpec((1, D_MODEL), lambda i, pos: (0, 0)),
                pl.BlockSpec(memory_space=pl.ANY),
            ],
            out_specs=pl.BlockSpec((TC, D_MODEL), tmap),
            scratch_shapes=[
                pltpu.VMEM((TC * SUBLANES, LANES), F32),
                pltpu.VMEM((TC * SUBLANES, LANES), F32),
                pltpu.SemaphoreType.DMA((2,)),
            ],
        ),
        out_shape=jax.ShapeDtypeStruct((n_tok, D_MODEL), F32),
        compiler_params=_cparams("arbitrary"),
    )(pos_flat, x1, route, mod4, norm_final.reshape(1, D_MODEL), osrt)


def _moe_block(x1, hx_tm, route, mod4, norm_final, w_gate, w_up, w_down, li, n_tok, final_norm):
    n_tiles = (2 * n_tok) // TMX + N_EXPERTS
    n_rows = n_tiles * TMX
    rank, cnt = _plan(route, n_tok)
    cnt = cnt[0, :N_EXPERTS].astype(jnp.int32)
    padded = ((cnt + TMX - 1) // TMX) * TMX
    ends = jnp.cumsum(padded)
    offs = ends - padded
    eid = route[:, 0:2].astype(jnp.int32)
    pos = (offs[eid] + rank[:, 0:2].astype(jnp.int32)).reshape(-1)
    tile_start = jnp.arange(n_tiles, dtype=jnp.int32) * TMX
    tile_expert = jnp.minimum(jnp.sum((tile_start[:, None] >= ends[None, :]).astype(jnp.int32), axis=1),
                              N_EXPERTS - 1)
    n_active = (ends[-1:] // TMX).astype(jnp.int32)

    xs = _dispatch(pos, hx_tm, n_tok, n_rows)
    osrt = _moe(tile_expert, n_active, xs, w_gate, w_up, w_down, li, n_tiles)
    return _combine(pos, x1, route, mod4, norm_final, osrt, li, n_tok, final_norm)


def kernel(x, c, ctx, c_ctx, w_ada, b_ada, norm1, norm2, w_in, w_fourier, w_conv, rpb, w_out, w_rg, b_rg,
           w_re, b_re, w_gate, w_up, w_down, norm_final):
    xt = jnp.concatenate([x.reshape(T_LAT, D_MODEL), ctx.reshape(T_CTX, D_MODEL)], axis=0)
    cc = jnp.concatenate([c, c_ctx[None, :], jnp.zeros((MOD_ROWS - BATCH - 1, D_MODEL), F32)], axis=0)
    mod4 = _modulation(cc, w_ada, b_ada).reshape(DEPTH, MOD_ROWS, 1, 6 * D_MODEL)

    w_in_bf = w_in.astype(BF16)
    w_out_bf = w_out.astype(BF16)
    pad = jnp.zeros((DEPTH, D_MODEL, LANES - ROUTE_COLS), F32)
    w_r = jnp.concatenate([w_rg, w_re, pad], axis=-1)
    w_r_hi = w_r.astype(BF16)
    w_r = jnp.concatenate([w_r_hi, (w_r - w_r_hi.astype(F32)).astype(BF16)], axis=-1)
    b_r = jnp.concatenate([b_rg, b_re, pad[:, 0, :]], axis=-1).reshape(DEPTH, 1, LANES)
    dft_ch = jnp.asarray(_channel_dft(), dtype=F32).astype(BF16)
    ctx_blk = T_LAT // CTX_LEN

    for li in range(DEPTH):
        last = li == DEPTH - 1
        yf, qkv, cv = _in_projection(xt, mod4, norm1, w_in_bf, dft_ch, li)
        table = _bias_table(rpb[li].reshape(-1))
        ff = _fourier_mix(yf, w_fourier, li, SEQ, 0)
        cvo = _short_conv(cv, w_conv, li, SEQ, 0)
        at = _attn_latent(qkv, table)
        if last:
            n_tok = T_LAT
        else:
            n_tok = T_ALL
            ff = jnp.concatenate([ff, _fourier_mix(yf, w_fourier, li, CTX_LEN, ctx_blk)], axis=0)
            cvo = jnp.concatenate([cvo, _short_conv(cv, w_conv, li, CTX_LEN, ctx_blk)], axis=0)
            at = jnp.concatenate([at, _attn_ctx(qkv)], axis=0)
        x1, hx_tm, route = _out_projection(xt, ff, at, cvo, mod4, norm2, w_out_bf, w_r, b_r, li, n_tok)
        xt_new = _moe_block(x1, hx_tm, route, mod4, norm_final, w_gate, w_up, w_down, li, n_tok, last)
        if last:
            return xt_new.reshape(BATCH, SEQ, D_MODEL)
        xt = xt_new
```

```python
import functools
import math

import numpy as np
import jax
import jax.numpy as jnp
from jax import lax
from jax.experimental import pallas as pl
from jax.experimental.pallas import tpu as pltpu

F32 = jnp.float32
BF16 = jnp.bfloat16

D_MODEL = 1024
BATCH = 8
SEQ = 2048
DEPTH = 2
GRID_W = 64
ROWS = SEQ // GRID_W
CTX_LEN = 256
T_LAT = BATCH * SEQ
T_CTX = BATCH * CTX_LEN
T_ALL = T_LAT + T_CTX

D_FOURIER = 256
D_FG = 64
HEAD_DIM = 64
NA_HEADS = 8
D_NA = NA_HEADS * HEAD_DIM
D_CONV = 256
D_IN_PROJ = D_FOURIER + 3 * D_NA + 3 * D_CONV
NA_KH = 8
NA_KW = 16
N_GROUPS = 4
EXPERTS_PER_GROUP = 8
N_EXPERTS = 32
D_EXPERT = 512
EPS = 1e-6
NEG_INF = -1e30

LANES = 128
SUBLANES = 8
MOD_ROWS = 16
TM = 512
TMX = 256
TD = 512
TC = 256
VMEM_LIMIT = 56 * 1024 * 1024


def _cparams(*sem):
    return pltpu.CompilerParams(dimension_semantics=sem, vmem_limit_bytes=VMEM_LIMIT)


def _ada_kernel(cc_ref, w_ref, b_ref, o_ref):
    s = cc_ref[...]
    s = s * jax.nn.sigmoid(s)
    acc = jnp.dot(s.astype(BF16), w_ref[0].astype(BF16), preferred_element_type=F32)
    o_ref[0] = acc + b_ref[0]


def _modulation(cc, w_ada, b_ada):
    tn = 1536
    nj = 6 * D_MODEL // tn
    return pl.pallas_call(
        _ada_kernel,
        name="ada_modulation",
        grid=(DEPTH, nj),
        in_specs=[
            pl.BlockSpec((MOD_ROWS, D_MODEL), lambda l, j: (0, 0)),
            pl.BlockSpec((1, D_MODEL, tn), lambda l, j: (l, 0, j)),
            pl.BlockSpec((1, 1, tn), lambda l, j: (l, 0, j)),
        ],
        out_specs=pl.BlockSpec((1, MOD_ROWS, tn), lambda l, j: (l, 0, j)),
        out_shape=jax.ShapeDtypeStruct((DEPTH, MOD_ROWS, 6 * D_MODEL), F32),
        compiler_params=_cparams("arbitrary", "arbitrary"),
    )(cc, w_ada, b_ada.reshape(DEPTH, 1, 6 * D_MODEL))


def _mod_row(i, tile):
    return jnp.minimum((i * tile) // SEQ, BATCH)


def _inproj_kernel(x_ref, mod_ref, n1_ref, w_ref, dft_ref, yf_ref, qkv_ref, cv_ref):
    x = x_ref[...]
    ms = jnp.mean(x * x, axis=-1, keepdims=True)
    y = x * lax.rsqrt(ms + EPS) * n1_ref[0]
    sh = mod_ref[:, 0:D_MODEL]
    sc = mod_ref[:, D_MODEL:2 * D_MODEL]
    h = y * (1.0 + sc) + sh
    u = jnp.dot(h.astype(BF16), w_ref[0], preferred_element_type=F32)
    yf = jnp.dot(u[:, 0:D_FOURIER].astype(BF16), dft_ref[...], preferred_element_type=F32)
    yf_ref[...] = yf.astype(BF16)
    q0 = D_FOURIER
    qkv_ref[:, 0:D_NA] = (u[:, q0:q0 + D_NA] * (1.0 / math.sqrt(HEAD_DIM))).astype(BF16)
    qkv_ref[:, D_NA:3 * D_NA] = u[:, q0 + D_NA:q0 + 3 * D_NA].astype(BF16)
    cv_ref[...] = u[:, q0 + 3 * D_NA:D_IN_PROJ]


def _in_projection(xt, mod4, norm1, w_in_bf, dft_ch, li):
    nt = T_ALL // TM
    return pl.pallas_call(
        _inproj_kernel,
        name="in_projection",
        grid=(nt,),
        in_specs=[
            pl.BlockSpec((TM, D_MODEL), lambda i: (i, 0)),
            pl.BlockSpec((None, None, 1, 6 * D_MODEL), lambda i: (li, _mod_row(i, TM), 0, 0)),
            pl.BlockSpec((1, 1, D_MODEL), lambda i: (li, 0, 0)),
            pl.BlockSpec((1, D_MODEL, D_IN_PROJ), lambda i: (li, 0, 0)),
            pl.BlockSpec((D_FOURIER, 2 * D_FOURIER), lambda i: (0, 0)),
        ],
        out_specs=[
            pl.BlockSpec((TM, 2 * D_FOURIER), lambda i: (i, 0)),
            pl.BlockSpec((TM, 3 * D_NA), lambda i: (i, 0)),
            pl.BlockSpec((TM, 3 * D_CONV), lambda i: (i, 0)),
        ],
        out_shape=[
            jax.ShapeDtypeStruct((T_ALL, 2 * D_FOURIER), BF16),
            jax.ShapeDtypeStruct((T_ALL, 3 * D_NA), BF16),
            jax.ShapeDtypeStruct((T_ALL, 3 * D_CONV), F32),
        ],
        compiler_params=_cparams("arbitrary"),
    )(xt, mod4, norm1.reshape(DEPTH, 1, D_MODEL), w_in_bf, dft_ch)


def _dft_tables(n):
    k = np.arange(n, dtype=np.int64)
    ang = 2.0 * np.pi * ((k[:, None] * k[None, :]) % n).astype(np.float64) / n
    s = 1.0 / math.sqrt(n)
    return np.cos(ang) * s, np.sin(ang) * s


def _channel_dft():
    c, s = _dft_tables(D_FG)
    eye = np.eye(D_FOURIER // D_FG)
    return np.concatenate([np.kron(eye, c), np.kron(eye, s)], axis=1)


def _fourier_kernel(cn_ref, sn_ref, y_ref, wf_ref, o_ref):
    z = (jnp.dot(cn_ref[...], y_ref[:, 0:D_FOURIER], preferred_element_type=F32)
         - jnp.dot(sn_ref[...], y_ref[:, D_FOURIER:2 * D_FOURIER], preferred_element_type=F32))
    o_ref[...] = jnp.dot(z.astype(BF16), wf_ref[0].astype(BF16), preferred_element_type=F32).astype(BF16)


def _fourier_mix(yf, w_fourier, li, n, first_block):
    tk = min(n, 512)
    nk = n // tk
    cn, sn = _dft_tables(n)
    cn = jnp.asarray(cn, dtype=F32).astype(BF16)
    sn = jnp.asarray(sn, dtype=F32).astype(BF16)
    return pl.pallas_call(
        _fourier_kernel,
        name="fourier_mix",
        grid=(nk, BATCH),
        in_specs=[
            pl.BlockSpec((tk, n), lambda k, b: (k, 0)),
            pl.BlockSpec((tk, n), lambda k, b: (k, 0)),
            pl.BlockSpec((n, 2 * D_FOURIER), lambda k, b: (first_block + b, 0)),
            pl.BlockSpec((1, D_FOURIER, D_FOURIER), lambda k, b: (li, 0, 0)),
        ],
        out_specs=pl.BlockSpec((tk, D_FOURIER), lambda k, b: (b * nk + k, 0)),
        out_shape=jax.ShapeDtypeStruct((BATCH * n, D_FOURIER), BF16),
        compiler_params=_cparams("arbitrary", "arbitrary"),
    )(cn, sn, yf, w_fourier)


def _conv_kernel(cv_ref, wc_ref, o_ref, pad_ref):
    n = cv_ref.shape[0]
    gb = cv_ref[:, 0:D_CONV]
    g = cv_ref[:, D_CONV:2 * D_CONV] * cv_ref[:, 2 * D_CONV:3 * D_CONV]
    zero = jnp.zeros((SUBLANES, D_CONV), F32)
    pad_ref[0:SUBLANES, :] = zero
    pad_ref[n + SUBLANES:n + 2 * SUBLANES, :] = zero
    pad_ref[SUBLANES:n + SUBLANES, :] = g
    prev = pad_ref[SUBLANES - 1:n + SUBLANES - 1, :]
    nxt = pad_ref[SUBLANES + 1:n + SUBLANES + 1, :]
    w = wc_ref[0]
    o_ref[...] = (gb * (w[0:1, :] * prev + w[1:2, :] * g + w[2:3, :] * nxt)).astype(BF16)


def _short_conv(cv, w_conv, li, n, first_block):
    return pl.pallas_call(
        _conv_kernel,
        name="short_conv",
        grid=(BATCH,),
        in_specs=[
            pl.BlockSpec((n, 3 * D_CONV), lambda b: (first_block + b, 0)),
            pl.BlockSpec((1, 3, D_CONV), lambda b: (li, 0, 0)),
        ],
        out_specs=pl.BlockSpec((n, D_CONV), lambda b: (b, 0)),
        out_shape=jax.ShapeDtypeStruct((BATCH * n, D_CONV), BF16),
        scratch_shapes=[pltpu.VMEM((n + 2 * SUBLANES, D_CONV), F32)],
        compiler_params=_cparams("arbitrary"),
    )(cv, w_conv)


RPB_H = 2 * NA_KH - 1
RPB_W = 2 * NA_KW - 1
KEYS_LOC = NA_KH * GRID_W


def _bias_kernel(rpb_ref, o_ref):
    h = pl.program_id(0)
    qi = lax.broadcasted_iota(jnp.int32, (GRID_W, LANES), 0)
    lj = lax.broadcasted_iota(jnp.int32, (GRID_W, LANES), 1)
    kc = lj & (GRID_W - 1)
    hi = lj >= GRID_W
    d = kc - qi + (NA_KW - 1)
    cs = jnp.clip(qi - NA_KW // 2, 0, GRID_W - NA_KW)
    valid = (kc >= cs) & (kc < cs + NA_KW)
    tiles = []
    for a in range(RPB_H - 1):
        acc = jnp.zeros((GRID_W, LANES), F32)
        for b in range(RPB_W):
            va = rpb_ref[(h * RPB_H + a) * RPB_W + b]
            vb = rpb_ref[(h * RPB_H + a + 1) * RPB_W + b]
            acc = jnp.where(d == b, jnp.where(hi, vb, va), acc)
        tiles.append(jnp.where(valid, acc, NEG_INF))
    for cls in range(NA_KH):
        for m in range(NA_KH // 2):
            o_ref[0, cls, :, m * LANES:(m + 1) * LANES] = tiles[2 * m - cls + NA_KH - 1]


def _bias_table(rpb_flat):
    return pl.pallas_call(
        _bias_kernel,
        name="attn_bias_table",
        grid=(NA_HEADS,),
        in_specs=[pl.BlockSpec(memory_space=pltpu.SMEM)],
        out_specs=pl.BlockSpec((1, NA_KH, GRID_W, KEYS_LOC), lambda h: (h // 2, 0, h % 2, 0)),
        out_shape=jax.ShapeDtypeStruct((NA_HEADS // 2, NA_KH, 2 * GRID_W, KEYS_LOC), F32),
        compiler_params=_cparams("arbitrary"),
    )(rpb_flat)


_NT_DIMS = (((1,), (1,)), ((), ()))


def _softmax_pv(parts):
    m = None
    for s, _ in parts:
        ms = jnp.max(s, axis=1, keepdims=True)
        m = ms if m is None else jnp.maximum(m, ms)
    l = None
    o = None
    for s, v in parts:
        p = jnp.exp(s - m)
        ls = jnp.sum(p, axis=1, keepdims=True)
        os_ = jnp.dot(p.astype(BF16), v, preferred_element_type=F32)
        l = ls if l is None else l + ls
        o = os_ if o is None else o + os_
    return o / l


ATT_UNROLL = 4


def _attn_latent_kernel(q_ref, k_ref, v_ref, kc_ref, vc_ref, tab_ref, o_ref, ve_ref, vce_ref):
    lane = lax.broadcasted_iota(jnp.int32, (GRID_W, LANES), 1)
    lo = lane < HEAD_DIM
    ve_ref[:, 0:LANES] = v_ref[...]
    ve_ref[:, LANES:2 * LANES] = jnp.ones((SEQ, LANES), BF16)
    vce_ref[:, 0:LANES] = vc_ref[...]
    vce_ref[:, LANES:2 * LANES] = jnp.ones((CTX_LEN, LANES), BF16)
    kc = kc_ref[...]
    vce = vce_ref[...]

    def one_row(r):
        rs = jnp.clip(r - NA_KH // 2, 0, ROWS - NA_KH)
        cls = r - rs
        q = q_ref[pl.ds(pl.multiple_of(r * GRID_W, GRID_W), GRID_W), :]
        kl = k_ref[pl.ds(pl.multiple_of(rs * GRID_W, GRID_W), KEYS_LOC), :]
        vl = ve_ref[pl.ds(pl.multiple_of(rs * GRID_W, GRID_W), KEYS_LOC), :]
        zero = jnp.zeros_like(q)
        q2 = jnp.concatenate([jnp.where(lo, q, zero), jnp.where(lo, zero, q)], axis=0)
        s1 = lax.dot_general(q2, kl, _NT_DIMS, preferred_element_type=F32) + tab_ref[0, cls]
        s2 = lax.dot_general(q2, kc, _NT_DIMS, preferred_element_type=F32)
        m = jnp.maximum(jnp.max(s1, axis=1, keepdims=True), jnp.max(s2, axis=1, keepdims=True))
        p1 = jnp.exp(s1 - m).astype(BF16)
        p2 = jnp.exp(s2 - m).astype(BF16)
        oe = (jnp.dot(p1, vl, preferred_element_type=F32) + jnp.dot(p2, vce, preferred_element_type=F32))
        o = oe[:, 0:LANES] / oe[:, LANES:2 * LANES]
        o_ref[pl.ds(pl.multiple_of(r * GRID_W, GRID_W), GRID_W), :] = (
            jnp.where(lo, o[0:GRID_W], o[GRID_W:2 * GRID_W]).astype(BF16))

    def body(it, carry):
        for u in range(ATT_UNROLL):
            one_row(it * ATT_UNROLL + u)
        return carry

    lax.fori_loop(0, ROWS // ATT_UNROLL, body, 0)


def _attn_latent(qkv, table):
    npair = NA_HEADS // 2
    cblk = T_LAT // CTX_LEN
    return pl.pallas_call(
        _attn_latent_kernel,
        name="attn_latent",
        grid=(BATCH, npair),
        in_specs=[
            pl.BlockSpec((SEQ, LANES), lambda b, p: (b, p)),
            pl.BlockSpec((SEQ, LANES), lambda b, p: (b, npair + p)),
            pl.BlockSpec((SEQ, LANES), lambda b, p: (b, 2 * npair + p)),
            pl.BlockSpec((CTX_LEN, LANES), lambda b, p: (cblk + b, npair + p)),
            pl.BlockSpec((CTX_LEN, LANES), lambda b, p: (cblk + b, 2 * npair + p)),
            pl.BlockSpec((1, NA_KH, 2 * GRID_W, KEYS_LOC), lambda b, p: (p, 0, 0, 0)),
        ],
        out_specs=pl.BlockSpec((SEQ, LANES), lambda b, p: (b, p)),
        out_shape=jax.ShapeDtypeStruct((T_LAT, D_NA), BF16),
        scratch_shapes=[pltpu.VMEM((SEQ, 2 * LANES), BF16), pltpu.VMEM((CTX_LEN, 2 * LANES), BF16)],
        compiler_params=_cparams("arbitrary", "arbitrary"),
    )(qkv, qkv, qkv, qkv, qkv, table)


def _attn_ctx_kernel(q_ref, k_ref, v_ref, o_ref):
    lane = lax.broadcasted_iota(jnp.int32, (CTX_LEN, LANES), 1)
    lo = lane < HEAD_DIM
    q = q_ref[...]
    k = k_ref[...]
    v = v_ref[...]
    outs = []
    for hh in range(2):
        qm = jnp.where(lo if hh == 0 else jnp.logical_not(lo), q, jnp.zeros_like(q))
        s = lax.dot_general(qm, k, _NT_DIMS, preferred_element_type=F32)
        outs.append(_softmax_pv([(s, v)]))
    o_ref[...] = jnp.where(lo, outs[0], outs[1]).astype(BF16)


def _attn_ctx(qkv):
    npair = NA_HEADS // 2
    cblk = T_LAT // CTX_LEN
    return pl.pallas_call(
        _attn_ctx_kernel,
        name="attn_context",
        grid=(BATCH, npair),
        in_specs=[
            pl.BlockSpec((CTX_LEN, LANES), lambda b, p: (cblk + b, p)),
            pl.BlockSpec((CTX_LEN, LANES), lambda b, p: (cblk + b, npair + p)),
            pl.BlockSpec((CTX_LEN, LANES), lambda b, p: (cblk + b, 2 * npair + p)),
        ],
        out_specs=pl.BlockSpec((CTX_LEN, LANES), lambda b, p: (b, p)),
        out_shape=jax.ShapeDtypeStruct((T_CTX, D_NA), BF16),
        compiler_params=_cparams("arbitrary", "arbitrary"),
    )(qkv, qkv, qkv)


ROUTE_COLS = N_GROUPS + N_EXPERTS


def _to_token_major(ref, val):
    rows = val.shape[0]
    for j in range(D_MODEL // LANES):
        ref[pl.ds(j, rows, stride=SUBLANES), :] = val[:, j * LANES:(j + 1) * LANES]


def _from_token_major(ref, rows, j):
    return ref[pl.ds(j, rows, stride=SUBLANES), :]


def _outproj_kernel(x_ref, ff_ref, at_ref, cv_ref, mod_ref, n2_ref, wo_ref, wr_ref, br_ref,
                    x1_ref, hx_ref, route_ref):
    mix_in = jnp.concatenate([ff_ref[...], at_ref[...], cv_ref[...]], axis=-1)
    mix = jnp.dot(mix_in, wo_ref[0], preferred_element_type=F32)
    g1 = mod_ref[:, 2 * D_MODEL:3 * D_MODEL]
    x1 = x_ref[...] + g1 * mix
    x1_ref[...] = x1
    ms = jnp.mean(x1 * x1, axis=-1, keepdims=True)
    y = x1 * lax.rsqrt(ms + EPS) * n2_ref[0]
    sh2 = mod_ref[:, 3 * D_MODEL:4 * D_MODEL]
    sc2 = mod_ref[:, 4 * D_MODEL:5 * D_MODEL]
    hx = y * (1.0 + sc2) + sh2
    _to_token_major(hx_ref, hx)

    hx_hi = hx.astype(BF16)
    hx_lo = (hx - hx_hi.astype(F32)).astype(BF16)
    part = (jnp.dot(hx_hi, wr_ref[0], preferred_element_type=F32)
            + jnp.dot(hx_lo, wr_ref[0], preferred_element_type=F32))
    logits = part[:, 0:LANES] + part[:, LANES:2 * LANES] + br_ref[0]
    tm = logits.shape[0]
    lane = lax.broadcasted_iota(jnp.int32, (tm, LANES), 1)
    big = jnp.int32(LANES)
    gl = jnp.where(lane < N_GROUPS, logits, -jnp.inf)
    gmax = jnp.max(gl, axis=1, keepdims=True)
    gidx = jnp.min(jnp.where(gl == gmax, lane, big), axis=1, keepdims=True)
    g_w = 1.0 / jnp.sum(jnp.exp(gl - gmax), axis=1, keepdims=True)
    e_lane = lane - N_GROUPS
    in_group = (e_lane >= 0) & (e_lane < N_EXPERTS) & ((e_lane >> 3) == gidx)
    es = jnp.where(in_group, logits, -jnp.inf)
    t1 = jnp.max(es, axis=1, keepdims=True)
    i1 = jnp.min(jnp.where(es == t1, lane, big), axis=1, keepdims=True)
    es2 = jnp.where(lane == i1, -jnp.inf, es)
    t2 = jnp.max(es2, axis=1, keepdims=True)
    i2 = jnp.min(jnp.where(es2 == t2, lane, big), axis=1, keepdims=True)
    dlt = jnp.exp(t2 - t1)
    w1 = g_w / (1.0 + dlt)
    w2 = g_w * dlt / (1.0 + dlt)
    e1 = (i1 - N_GROUPS).astype(F32)
    e2 = (i2 - N_GROUPS).astype(F32)
    route_ref[...] = jnp.where(lane == 0, e1, jnp.where(lane == 1, e2,
                               jnp.where(lane == 2, w1, jnp.where(lane == 3, w2, 0.0))))


def _out_projection(xt, ff, at, cvo, mod4, norm2, w_out_bf, w_r, b_r, li, n_tok):
    nt = n_tok // TM
    return pl.pallas_call(
        _outproj_kernel,
        name="out_projection",
        grid=(nt,),
        in_specs=[
            pl.BlockSpec((TM, D_MODEL), lambda i: (i, 0)),
            pl.BlockSpec((TM, D_FOURIER), lambda i: (i, 0)),
            pl.BlockSpec((TM, D_NA), lambda i: (i, 0)),
            pl.BlockSpec((TM, D_CONV), lambda i: (i, 0)),
            pl.BlockSpec((None, None, 1, 6 * D_MODEL), lambda i: (li, _mod_row(i, TM), 0, 0)),
            pl.BlockSpec((1, 1, D_MODEL), lambda i: (li, 0, 0)),
            pl.BlockSpec((1, D_MODEL, D_MODEL), lambda i: (li, 0, 0)),
            pl.BlockSpec((1, D_MODEL, 2 * LANES), lambda i: (li, 0, 0)),
            pl.BlockSpec((1, 1, LANES), lambda i: (li, 0, 0)),
        ],
        out_specs=[
            pl.BlockSpec((TM, D_MODEL), lambda i: (i, 0)),
            pl.BlockSpec((TM * SUBLANES, LANES), lambda i: (i, 0)),
            pl.BlockSpec((TM, LANES), lambda i: (i, 0)),
        ],
        out_shape=[
            jax.ShapeDtypeStruct((n_tok, D_MODEL), F32),
            jax.ShapeDtypeStruct((n_tok * SUBLANES, LANES), F32),
            jax.ShapeDtypeStruct((n_tok, LANES), F32),
        ],
        compiler_params=_cparams("arbitrary"),
    )(xt, ff, at, cvo, mod4, norm2.reshape(DEPTH, 1, D_MODEL), w_out_bf, w_r, b_r)


def _plan_kernel(route_ref, rank_ref, cnt_ref, carry_ref):
    i = pl.program_id(0)

    @pl.when(i == 0)
    def _():
        carry_ref[...] = jnp.zeros_like(carry_ref)

    r = route_ref[...]
    tm = r.shape[0]
    lane = lax.broadcasted_iota(jnp.int32, (tm, LANES), 1)
    oh1 = lane == r[:, 0:1].astype(jnp.int32)
    oh2 = lane == r[:, 1:2].astype(jnp.int32)
    oh = jnp.where(oh1 | oh2, 1.0, 0.0)
    row = lax.broadcasted_iota(jnp.int32, (tm, tm), 0)
    col = lax.broadcasted_iota(jnp.int32, (tm, tm), 1)
    tri = jnp.where(row > col, 1.0, 0.0).astype(BF16)
    cum = jnp.dot(tri, oh.astype(BF16), preferred_element_type=F32) + carry_ref[...]
    r1 = jnp.sum(jnp.where(oh1, cum, 0.0), axis=1, keepdims=True)
    r2 = jnp.sum(jnp.where(oh2, cum, 0.0), axis=1, keepdims=True)
    rank_ref[...] = jnp.where(lane == 0, r1, jnp.where(lane == 1, r2, 0.0))
    carry_ref[...] += jnp.sum(oh, axis=0, keepdims=True)
    cnt_ref[...] = jnp.broadcast_to(carry_ref[...], cnt_ref.shape)


def _plan(route, n_tok):
    nt = n_tok // TM
    return pl.pallas_call(
        _plan_kernel,
        name="moe_plan",
        grid=(nt,),
        in_specs=[pl.BlockSpec((TM, LANES), lambda i: (i, 0))],
        out_specs=[
            pl.BlockSpec((TM, LANES), lambda i: (i, 0)),
            pl.BlockSpec((SUBLANES, LANES), lambda i: (0, 0)),
        ],
        out_shape=[
            jax.ShapeDtypeStruct((n_tok, LANES), F32),
            jax.ShapeDtypeStruct((SUBLANES, LANES), F32),
        ],
        scratch_shapes=[pltpu.VMEM((1, LANES), F32)],
        compiler_params=_cparams("arbitrary"),
    )(route)


def _row_tile(ref, row):
    return ref.at[pl.ds(pl.multiple_of(row * SUBLANES, SUBLANES), SUBLANES), :]


INV_UNROLL = 8


def _inverse_kernel(pos_ref, src_ref):
    n_rows = src_ref.shape[0]
    n_tok = pos_ref.shape[0] // 2

    def zero(k, carry):
        for u in range(INV_UNROLL):
            src_ref[k * INV_UNROLL + u] = 0
        return carry

    lax.fori_loop(0, n_rows // INV_UNROLL, zero, 0)

    def body(k, carry):
        for u in range(INV_UNROLL // 2):
            t = k * (INV_UNROLL // 2) + u
            src_ref[pos_ref[2 * t]] = t
            src_ref[pos_ref[2 * t + 1]] = t
        return carry

    lax.fori_loop(0, n_tok // (INV_UNROLL // 2), body, 0)


def _inverse_map(pos_flat, n_rows):
    return pl.pallas_call(
        _inverse_kernel,
        name="moe_row_map",
        in_specs=[pl.BlockSpec(memory_space=pltpu.SMEM)],
        out_specs=pl.BlockSpec(memory_space=pltpu.SMEM),
        out_shape=jax.ShapeDtypeStruct((n_rows,), jnp.int32),
    )(pos_flat)


GATHER_UNROLL = 8


def _moe_kernel(te_ref, na_ref, src_ref, hx_ref, wg_ref, wu_ref, wd_ref, os_ref, xbuf, wg_s, wu_s, wd_s, sem):
    i = pl.program_id(0)
    n_act = na_ref[0]
    slot = i % 2
    prev = te_ref[jnp.maximum(i - 1, 0)]
    active = i < n_act

    def gather(tile, dst_slot):
        def body(k, carry):
            for u in range(GATHER_UNROLL):
                r = k * GATHER_UNROLL + u
                pltpu.make_async_copy(_row_tile(hx_ref, src_ref[tile * TMX + r]),
                                      _row_tile(xbuf.at[dst_slot], r), sem.at[dst_slot]).start(priority=u % 2)
            return carry

        lax.fori_loop(0, TMX // GATHER_UNROLL, body, 0)

    @pl.when(i == 0)
    def _():
        gather(0, 0)

    @pl.when(i + 1 < n_act)
    def _():
        gather(i + 1, 1 - slot)

    @pl.when(active & ((i == 0) | (te_ref[i] != prev)))
    def _():
        wg_s[...] = wg_ref[0, 0].astype(BF16)
        wu_s[...] = wu_ref[0, 0].astype(BF16)
        wd_s[...] = wd_ref[0, 0].astype(BF16)

    @pl.when(active)
    def _():
        pltpu.make_async_copy(hx_ref.at[pl.ds(0, TMX * SUBLANES), :], xbuf.at[slot], sem.at[slot]).wait()
        x = jnp.concatenate([_from_token_major(xbuf.at[slot], TMX, j).astype(BF16)
                             for j in range(D_MODEL // LANES)], axis=-1)
        g = jnp.dot(x, wg_s[...], preferred_element_type=F32)
        u = jnp.dot(x, wu_s[...], preferred_element_type=F32)
        h = (g * jax.nn.sigmoid(g) * u).astype(BF16)
        _to_token_major(os_ref, jnp.dot(h, wd_s[...], preferred_element_type=F32))

    @pl.when(jnp.logical_not(active))
    def _():
        os_ref[...] = jnp.zeros_like(os_ref)


def _moe(tile_expert, n_active, src_tok, hx_tm, w_gate, w_up, w_down, li, n_tiles):
    def w_map(i, te, na, src):
        return (li, te[jnp.minimum(i, na[0] - 1)], 0, 0)

    return pl.pallas_call(
        _moe_kernel,
        name="moe_experts",
        grid_spec=pltpu.PrefetchScalarGridSpec(
            num_scalar_prefetch=3,
            grid=(n_tiles,),
            in_specs=[
                pl.BlockSpec(memory_space=pl.ANY),
                pl.BlockSpec((1, 1, D_MODEL, D_EXPERT), w_map),
                pl.BlockSpec((1, 1, D_MODEL, D_EXPERT), w_map),
                pl.BlockSpec((1, 1, D_EXPERT, D_MODEL), w_map),
            ],
            out_specs=pl.BlockSpec((TMX * SUBLANES, LANES), lambda i, te, na, src: (i, 0)),
            scratch_shapes=[
                pltpu.VMEM((2, TMX * SUBLANES, LANES), F32),
                pltpu.VMEM((D_MODEL, D_EXPERT), BF16),
                pltpu.VMEM((D_MODEL, D_EXPERT), BF16),
                pltpu.VMEM((D_EXPERT, D_MODEL), BF16),
                pltpu.SemaphoreType.DMA((2,)),
            ],
        ),
        out_shape=jax.ShapeDtypeStruct((n_tiles * TMX * SUBLANES, LANES), F32),
        compiler_params=_cparams("arbitrary"),
    )(tile_expert, n_active, src_tok, hx_tm, w_gate, w_up, w_down)


def _combine_kernel(pos_ref, x1_ref, route_ref, mod_ref, nf_ref, os_ref, o_ref, buf0, buf1, sem, *, final_norm):
    i = pl.program_id(0)

    def body(t, carry):
        tok = i * TC + t
        pltpu.make_async_copy(_row_tile(os_ref, pos_ref[tok * 2]), _row_tile(buf0, t), sem.at[0]).start(priority=0)
        pltpu.make_async_copy(_row_tile(os_ref, pos_ref[tok * 2 + 1]), _row_tile(buf1, t), sem.at[1]).start(priority=1)
        return carry

    lax.fori_loop(0, TC, body, 0)
    nrow = TC * SUBLANES
    pltpu.make_async_copy(os_ref.at[pl.ds(0, nrow), :], buf0, sem.at[0]).wait()
    pltpu.make_async_copy(os_ref.at[pl.ds(0, nrow), :], buf1, sem.at[1]).wait()

    w1 = route_ref[:, 2:3]
    w2 = route_ref[:, 3:4]
    cols = []
    for j in range(D_MODEL // LANES):
        y = w1 * _from_token_major(buf0, TC, j) + w2 * _from_token_major(buf1, TC, j)
        g2 = mod_ref[:, 5 * D_MODEL + j * LANES:5 * D_MODEL + (j + 1) * LANES]
        cols.append(x1_ref[:, j * LANES:(j + 1) * LANES] + g2 * y)
    x2 = jnp.concatenate(cols, axis=-1)
    if final_norm:
        ms = jnp.mean(x2 * x2, axis=-1, keepdims=True)
        x2 = x2 * lax.rsqrt(ms + EPS) * nf_ref[...]
    o_ref[...] = x2


def _combine(pos_flat, x1, route, mod4, norm_final, osrt, li, n_tok, final_norm):
    def tmap(i, pos):
        return (i, 0)

    return pl.pallas_call(
        functools.partial(_combine_kernel, final_norm=final_norm),
        name="moe_combine",
        grid_spec=pltpu.PrefetchScalarGridSpec(
            num_scalar_prefetch=1,
            grid=(n_tok // TC,),
            in_specs=[
                pl.BlockSpec((TC, D_MODEL), tmap),
                pl.BlockSpec((TC, LANES), tmap),
                pl.BlockSpec((None, None, 1, 6 * D_MODEL), lambda i, pos: (li, _mod_row(i, TC), 0, 0)),
                pl.BlockSpec((1, D_MODEL), lambda i, pos: (0, 0)),
                pl.BlockSpec(memory_space=pl.ANY),
            ],
            out_specs=pl.BlockSpec((TC, D_MODEL), tmap),
            scratch_shapes=[
                pltpu.VMEM((TC * SUBLANES, LANES), F32),
                pltpu.VMEM((TC * SUBLANES, LANES), F32),
                pltpu.SemaphoreType.DMA((2,)),
            ],
        ),
        out_shape=jax.ShapeDtypeStruct((n_tok, D_MODEL), F32),
        compiler_params=_cparams("arbitrary"),
    )(pos_flat, x1, route, mod4, norm_final.reshape(1, D_MODEL), osrt)


def _moe_block(x1, hx_tm, route, mod4, norm_final, w_gate, w_up, w_down, li, n_tok, final_norm):
    n_tiles = (2 * n_tok) // TMX + N_EXPERTS
    n_rows = n_tiles * TMX
    rank, cnt = _plan(route, n_tok)
    cnt = cnt[0, :N_EXPERTS].astype(jnp.int32)
    padded = ((cnt + TMX - 1) // TMX) * TMX
    ends = jnp.cumsum(padded)
    offs = ends - padded
    eid = route[:, 0:2].astype(jnp.int32)
    pos = (offs[eid] + rank[:, 0:2].astype(jnp.int32)).reshape(-1)
    tile_start = jnp.arange(n_tiles, dtype=jnp.int32) * TMX
    tile_expert = jnp.minimum(jnp.sum((tile_start[:, None] >= ends[None, :]).astype(jnp.int32), axis=1),
                              N_EXPERTS - 1)
    n_active = (ends[-1:] // TMX).astype(jnp.int32)

    src_tok = _inverse_map(pos, n_rows)
    osrt = _moe(tile_expert, n_active, src_tok, hx_tm, w_gate, w_up, w_down, li, n_tiles)
    return _combine(pos, x1, route, mod4, norm_final, osrt, li, n_tok, final_norm)


def kernel(x, c, ctx, c_ctx, w_ada, b_ada, norm1, norm2, w_in, w_fourier, w_conv, rpb, w_out, w_rg, b_rg,
           w_re, b_re, w_gate, w_up, w_down, norm_final):
    xt = jnp.concatenate([x.reshape(T_LAT, D_MODEL), ctx.reshape(T_CTX, D_MODEL)], axis=0)
    cc = jnp.concatenate([c, c_ctx[None, :], jnp.zeros((MOD_ROWS - BATCH - 1, D_MODEL), F32)], axis=0)
    mod4 = _modulation(cc, w_ada, b_ada).reshape(DEPTH, MOD_ROWS, 1, 6 * D_MODEL)

    w_in_bf = w_in.astype(BF16)
    w_out_bf = w_out.astype(BF16)
    pad = jnp.zeros((DEPTH, D_MODEL, LANES - ROUTE_COLS), F32)
    w_r = jnp.concatenate([w_rg, w_re, pad], axis=-1)
    w_r_hi = w_r.astype(BF16)
    w_r = jnp.concatenate([w_r_hi, (w_r - w_r_hi.astype(F32)).astype(BF16)], axis=-1)
    b_r = jnp.concatenate([b_rg, b_re, pad[:, 0, :]], axis=-1).reshape(DEPTH, 1, LANES)
    dft_ch = jnp.asarray(_channel_dft(), dtype=F32).astype(BF16)
    ctx_blk = T_LAT // CTX_LEN

    for li in range(DEPTH):
        last = li == DEPTH - 1
        yf, qkv, cv = _in_projection(xt, mod4, norm1, w_in_bf, dft_ch, li)
        table = _bias_table(rpb[li].reshape(-1))
        ff = _fourier_mix(yf, w_fourier, li, SEQ, 0)
        cvo = _short_conv(cv, w_conv, li, SEQ, 0)
        at = _attn_latent(qkv, table)
        if last:
            n_tok = T_LAT
        else:
            n_tok = T_ALL
            ff = jnp.concatenate([ff, _fourier_mix(yf, w_fourier, li, CTX_LEN, ctx_blk)], axis=0)
            cvo = jnp.concatenate([cvo, _short_conv(cv, w_conv, li, CTX_LEN, ctx_blk)], axis=0)
            at = jnp.concatenate([at, _attn_ctx(qkv)], axis=0)
        x1, hx_tm, route = _out_projection(xt, ff, at, cvo, mod4, norm2, w_out_bf, w_r, b_r, li, n_tok)
        xt_new = _moe_block(x1, hx_tm, route, mod4, norm_final, w_gate, w_up, w_down, li, n_tok, last)
        if last:
            return xt_new.reshape(BATCH, SEQ, D_MODEL)
        xt = xt_new
```

```python
import functools
import math

import numpy as np
import jax
import jax.numpy as jnp
from jax import lax
from jax.experimental import pallas as pl
from jax.experimental.pallas import tpu as pltpu

F32 = jnp.float32
BF16 = jnp.bfloat16

D_MODEL = 1024
BATCH = 8
SEQ = 2048
DEPTH = 2
GRID_W = 64
ROWS = SEQ // GRID_W
CTX_LEN = 256
T_LAT = BATCH * SEQ
T_CTX = BATCH * CTX_LEN
T_ALL = T_LAT + T_CTX

D_FOURIER = 256
D_FG = 64
HEAD_DIM = 64
NA_HEADS = 8
D_NA = NA_HEADS * HEAD_DIM
D_CONV = 256
D_IN_PROJ = D_FOURIER + 3 * D_NA + 3 * D_CONV
NA_KH = 8
NA_KW = 16
N_GROUPS = 4
EXPERTS_PER_GROUP = 8
N_EXPERTS = 32
D_EXPERT = 512
EPS = 1e-6
NEG_INF = -1e30

LANES = 128
SUBLANES = 8
MOD_ROWS = 16
TM = 512
TMX = 256
TD = 512
TC = 256
VMEM_LIMIT = 56 * 1024 * 1024


def _cparams(*sem):
    return pltpu.CompilerParams(dimension_semantics=sem, vmem_limit_bytes=VMEM_LIMIT)


def _ada_kernel(cc_ref, w_ref, b_ref, o_ref):
    s = cc_ref[...]
    s = s * jax.nn.sigmoid(s)
    acc = jnp.dot(s.astype(BF16), w_ref[0].astype(BF16), preferred_element_type=F32)
    o_ref[0] = acc + b_ref[0]


def _modulation(cc, w_ada, b_ada):
    tn = 1536
    nj = 6 * D_MODEL // tn
    return pl.pallas_call(
        _ada_kernel,
        name="ada_modulation",
        grid=(DEPTH, nj),
        in_specs=[
            pl.BlockSpec((MOD_ROWS, D_MODEL), lambda l, j: (0, 0)),
            pl.BlockSpec((1, D_MODEL, tn), lambda l, j: (l, 0, j)),
            pl.BlockSpec((1, 1, tn), lambda l, j: (l, 0, j)),
        ],
        out_specs=pl.BlockSpec((1, MOD_ROWS, tn), lambda l, j: (l, 0, j)),
        out_shape=jax.ShapeDtypeStruct((DEPTH, MOD_ROWS, 6 * D_MODEL), F32),
        compiler_params=_cparams("arbitrary", "arbitrary"),
    )(cc, w_ada, b_ada.reshape(DEPTH, 1, 6 * D_MODEL))


def _mod_row(i, tile):
    return jnp.minimum((i * tile) // SEQ, BATCH)


NT_LAT = T_LAT // TM


def _lat_spec(cols):
    return pl.BlockSpec((TM, cols), lambda i: (jnp.minimum(i, NT_LAT - 1), 0))


def _ctx_spec(cols, first_block):
    return pl.BlockSpec((TM, cols), lambda i: (jnp.maximum(i - NT_LAT, 0) + first_block, 0))


def _pick_stream(lat_ref, ctx_ref):
    return jnp.where(pl.program_id(0) >= NT_LAT, ctx_ref[...], lat_ref[...])


def _inproj_kernel(xa_ref, xb_ref, mod_ref, n1_ref, w_ref, dft_ref, yf_ref, qkv_ref, cv_ref):
    x = _pick_stream(xa_ref, xb_ref)
    ms = jnp.mean(x * x, axis=-1, keepdims=True)
    y = x * lax.rsqrt(ms + EPS) * n1_ref[0]
    sh = mod_ref[:, 0:D_MODEL]
    sc = mod_ref[:, D_MODEL:2 * D_MODEL]
    h = y * (1.0 + sc) + sh
    u = jnp.dot(h.astype(BF16), w_ref[0], preferred_element_type=F32)
    yf = jnp.dot(u[:, 0:D_FOURIER].astype(BF16), dft_ref[...], preferred_element_type=F32)
    yf_ref[...] = yf.astype(BF16)
    q0 = D_FOURIER
    qkv_ref[:, 0:D_NA] = (u[:, q0:q0 + D_NA] * (1.0 / math.sqrt(HEAD_DIM))).astype(BF16)
    qkv_ref[:, D_NA:3 * D_NA] = u[:, q0 + D_NA:q0 + 3 * D_NA].astype(BF16)
    cv_ref[...] = u[:, q0 + 3 * D_NA:D_IN_PROJ]


def _in_projection(x_lat, x_ctx, ctx_first_block, mod4, norm1, w_in_bf, dft_ch, li):
    nt = T_ALL // TM
    return pl.pallas_call(
        _inproj_kernel,
        name="in_projection",
        grid=(nt,),
        in_specs=[
            _lat_spec(D_MODEL),
            _ctx_spec(D_MODEL, ctx_first_block),
            pl.BlockSpec((None, None, 1, 6 * D_MODEL), lambda i: (li, _mod_row(i, TM), 0, 0)),
            pl.BlockSpec((1, 1, D_MODEL), lambda i: (li, 0, 0)),
            pl.BlockSpec((1, D_MODEL, D_IN_PROJ), lambda i: (li, 0, 0)),
            pl.BlockSpec((D_FOURIER, 2 * D_FOURIER), lambda i: (0, 0)),
        ],
        out_specs=[
            pl.BlockSpec((TM, 2 * D_FOURIER), lambda i: (i, 0)),
            pl.BlockSpec((TM, 3 * D_NA), lambda i: (i, 0)),
            pl.BlockSpec((TM, 3 * D_CONV), lambda i: (i, 0)),
        ],
        out_shape=[
            jax.ShapeDtypeStruct((T_ALL, 2 * D_FOURIER), BF16),
            jax.ShapeDtypeStruct((T_ALL, 3 * D_NA), BF16),
            jax.ShapeDtypeStruct((T_ALL, 3 * D_CONV), F32),
        ],
        compiler_params=_cparams("arbitrary"),
    )(x_lat, x_ctx, mod4, norm1.reshape(DEPTH, 1, D_MODEL), w_in_bf, dft_ch)


def _dft_tables(n):
    k = np.arange(n, dtype=np.int64)
    ang = 2.0 * np.pi * ((k[:, None] * k[None, :]) % n).astype(np.float64) / n
    s = 1.0 / math.sqrt(n)
    return np.cos(ang) * s, np.sin(ang) * s


def _channel_dft():
    c, s = _dft_tables(D_FG)
    eye = np.eye(D_FOURIER // D_FG)
    return np.concatenate([np.kron(eye, c), np.kron(eye, s)], axis=1)


def _fourier_kernel(cn_ref, sn_ref, y_ref, wf_ref, o_ref):
    z = (jnp.dot(cn_ref[...], y_ref[:, 0:D_FOURIER], preferred_element_type=F32)
         - jnp.dot(sn_ref[...], y_ref[:, D_FOURIER:2 * D_FOURIER], preferred_element_type=F32))
    o_ref[...] = jnp.dot(z.astype(BF16), wf_ref[0].astype(BF16), preferred_element_type=F32).astype(BF16)


def _fourier_mix(yf, w_fourier, li, n, first_block):
    tk = min(n, 512)
    nk = n // tk
    cn, sn = _dft_tables(n)
    cn = jnp.asarray(cn, dtype=F32).astype(BF16)
    sn = jnp.asarray(sn, dtype=F32).astype(BF16)
    return pl.pallas_call(
        _fourier_kernel,
        name="fourier_mix",
        grid=(nk, BATCH),
        in_specs=[
            pl.BlockSpec((tk, n), lambda k, b: (k, 0)),
            pl.BlockSpec((tk, n), lambda k, b: (k, 0)),
            pl.BlockSpec((n, 2 * D_FOURIER), lambda k, b: (first_block + b, 0)),
            pl.BlockSpec((1, D_FOURIER, D_FOURIER), lambda k, b: (li, 0, 0)),
        ],
        out_specs=pl.BlockSpec((tk, D_FOURIER), lambda k, b: (b * nk + k, 0)),
        out_shape=jax.ShapeDtypeStruct((BATCH * n, D_FOURIER), BF16),
        compiler_params=_cparams("arbitrary", "arbitrary"),
    )(cn, sn, yf, w_fourier)


def _conv_kernel(cv_ref, wc_ref, o_ref, pad_ref):
    n = cv_ref.shape[0]
    gb = cv_ref[:, 0:D_CONV]
    g = cv_ref[:, D_CONV:2 * D_CONV] * cv_ref[:, 2 * D_CONV:3 * D_CONV]
    zero = jnp.zeros((SUBLANES, D_CONV), F32)
    pad_ref[0:SUBLANES, :] = zero
    pad_ref[n + SUBLANES:n + 2 * SUBLANES, :] = zero
    pad_ref[SUBLANES:n + SUBLANES, :] = g
    prev = pad_ref[SUBLANES - 1:n + SUBLANES - 1, :]
    nxt = pad_ref[SUBLANES + 1:n + SUBLANES + 1, :]
    w = wc_ref[0]
    o_ref[...] = (gb * (w[0:1, :] * prev + w[1:2, :] * g + w[2:3, :] * nxt)).astype(BF16)


def _short_conv(cv, w_conv, li, n, first_block):
    return pl.pallas_call(
        _conv_kernel,
        name="short_conv",
        grid=(BATCH,),
        in_specs=[
            pl.BlockSpec((n, 3 * D_CONV), lambda b: (first_block + b, 0)),
            pl.BlockSpec((1, 3, D_CONV), lambda b: (li, 0, 0)),
        ],
        out_specs=pl.BlockSpec((n, D_CONV), lambda b: (b, 0)),
        out_shape=jax.ShapeDtypeStruct((BATCH * n, D_CONV), BF16),
        scratch_shapes=[pltpu.VMEM((n + 2 * SUBLANES, D_CONV), F32)],
        compiler_params=_cparams("arbitrary"),
    )(cv, w_conv)


RPB_H = 2 * NA_KH - 1
RPB_W = 2 * NA_KW - 1
KEYS_LOC = NA_KH * GRID_W


def _bias_kernel(rpb_ref, o_ref):
    h = pl.program_id(0)
    qi = lax.broadcasted_iota(jnp.int32, (GRID_W, LANES), 0)
    lj = lax.broadcasted_iota(jnp.int32, (GRID_W, LANES), 1)
    kc = lj & (GRID_W - 1)
    hi = lj >= GRID_W
    d = kc - qi + (NA_KW - 1)
    cs = jnp.clip(qi - NA_KW // 2, 0, GRID_W - NA_KW)
    valid = (kc >= cs) & (kc < cs + NA_KW)
    tiles = []
    for a in range(RPB_H - 1):
        acc = jnp.zeros((GRID_W, LANES), F32)
        for b in range(RPB_W):
            va = rpb_ref[(h * RPB_H + a) * RPB_W + b]
            vb = rpb_ref[(h * RPB_H + a + 1) * RPB_W + b]
            acc = jnp.where(d == b, jnp.where(hi, vb, va), acc)
        tiles.append(jnp.where(valid, acc, NEG_INF))
    for cls in range(NA_KH):
        for m in range(NA_KH // 2):
            o_ref[0, cls, :, m * LANES:(m + 1) * LANES] = tiles[2 * m - cls + NA_KH - 1]


def _bias_table(rpb_flat):
    return pl.pallas_call(
        _bias_kernel,
        name="attn_bias_table",
        grid=(NA_HEADS,),
        in_specs=[pl.BlockSpec(memory_space=pltpu.SMEM)],
        out_specs=pl.BlockSpec((1, NA_KH, GRID_W, KEYS_LOC), lambda h: (h // 2, 0, h % 2, 0)),
        out_shape=jax.ShapeDtypeStruct((NA_HEADS // 2, NA_KH, 2 * GRID_W, KEYS_LOC), F32),
        compiler_params=_cparams("arbitrary"),
    )(rpb_flat)


_NT_DIMS = (((1,), (1,)), ((), ()))


def _softmax_pv(parts):
    m = None
    for s, _ in parts:
        ms = jnp.max(s, axis=1, keepdims=True)
        m = ms if m is None else jnp.maximum(m, ms)
    l = None
    o = None
    for s, v in parts:
        p = jnp.exp(s - m)
        ls = jnp.sum(p, axis=1, keepdims=True)
        os_ = jnp.dot(p.astype(BF16), v, preferred_element_type=F32)
        l = ls if l is None else l + ls
        o = os_ if o is None else o + os_
    return o / l


ATT_UNROLL = 4


def _attn_latent_kernel(q_ref, k_ref, v_ref, kc_ref, vc_ref, tab_ref, o_ref, ve_ref, vce_ref):
    lane = lax.broadcasted_iota(jnp.int32, (GRID_W, LANES), 1)
    lo = lane < HEAD_DIM
    ve_ref[:, 0:LANES] = v_ref[...]
    ve_ref[:, LANES:2 * LANES] = jnp.ones((SEQ, LANES), BF16)
    vce_ref[:, 0:LANES] = vc_ref[...]
    vce_ref[:, LANES:2 * LANES] = jnp.ones((CTX_LEN, LANES), BF16)
    kc = kc_ref[...]
    vce = vce_ref[...]

    def one_row(r):
        rs = jnp.clip(r - NA_KH // 2, 0, ROWS - NA_KH)
        cls = r - rs
        q = q_ref[pl.ds(pl.multiple_of(r * GRID_W, GRID_W), GRID_W), :]
        kl = k_ref[pl.ds(pl.multiple_of(rs * GRID_W, GRID_W), KEYS_LOC), :]
        vl = ve_ref[pl.ds(pl.multiple_of(rs * GRID_W, GRID_W), KEYS_LOC), :]
        zero = jnp.zeros_like(q)
        q2 = jnp.concatenate([jnp.where(lo, q, zero), jnp.where(lo, zero, q)], axis=0)
        s1 = lax.dot_general(q2, kl, _NT_DIMS, preferred_element_type=F32) + tab_ref[0, cls]
        s2 = lax.dot_general(q2, kc, _NT_DIMS, preferred_element_type=F32)
        m = jnp.maximum(jnp.max(s1, axis=1, keepdims=True), jnp.max(s2, axis=1, keepdims=True))
        p1 = jnp.exp(s1 - m).astype(BF16)
        p2 = jnp.exp(s2 - m).astype(BF16)
        oe = (jnp.dot(p1, vl, preferred_element_type=F32) + jnp.dot(p2, vce, preferred_element_type=F32))
        o = oe[:, 0:LANES] / oe[:, LANES:2 * LANES]
        o_ref[pl.ds(pl.multiple_of(r * GRID_W, GRID_W), GRID_W), :] = (
            jnp.where(lo, o[0:GRID_W], o[GRID_W:2 * GRID_W]).astype(BF16))

    def body(it, carry):
        for u in range(ATT_UNROLL):
            one_row(it * ATT_UNROLL + u)
        return carry

    lax.fori_loop(0, ROWS // ATT_UNROLL, body, 0)


def _attn_latent(qkv, table):
    npair = NA_HEADS // 2
    cblk = T_LAT // CTX_LEN
    return pl.pallas_call(
        _attn_latent_kernel,
        name="attn_latent",
        grid=(BATCH, npair),
        in_specs=[
            pl.BlockSpec((SEQ, LANES), lambda b, p: (b, p)),
            pl.BlockSpec((SEQ, LANES), lambda b, p: (b, npair + p)),
            pl.BlockSpec((SEQ, LANES), lambda b, p: (b, 2 * npair + p)),
            pl.BlockSpec((CTX_LEN, LANES), lambda b, p: (cblk + b, npair + p)),
            pl.BlockSpec((CTX_LEN, LANES), lambda b, p: (cblk + b, 2 * npair + p)),
            pl.BlockSpec((1, NA_KH, 2 * GRID_W, KEYS_LOC), lambda b, p: (p, 0, 0, 0)),
        ],
        out_specs=pl.BlockSpec((SEQ, LANES), lambda b, p: (b, p)),
        out_shape=jax.ShapeDtypeStruct((T_LAT, D_NA), BF16),
        scratch_shapes=[pltpu.VMEM((SEQ, 2 * LANES), BF16), pltpu.VMEM((CTX_LEN, 2 * LANES), BF16)],
        compiler_params=_cparams("arbitrary", "arbitrary"),
    )(qkv, qkv, qkv, qkv, qkv, table)


def _attn_ctx_kernel(q_ref, k_ref, v_ref, o_ref):
    lane = lax.broadcasted_iota(jnp.int32, (CTX_LEN, LANES), 1)
    lo = lane < HEAD_DIM
    q = q_ref[...]
    k = k_ref[...]
    v = v_ref[...]
    outs = []
    for hh in range(2):
        qm = jnp.where(lo if hh == 0 else jnp.logical_not(lo), q, jnp.zeros_like(q))
        s = lax.dot_general(qm, k, _NT_DIMS, preferred_element_type=F32)
        outs.append(_softmax_pv([(s, v)]))
    o_ref[...] = jnp.where(lo, outs[0], outs[1]).astype(BF16)


def _attn_ctx(qkv):
    npair = NA_HEADS // 2
    cblk = T_LAT // CTX_LEN
    return pl.pallas_call(
        _attn_ctx_kernel,
        name="attn_context",
        grid=(BATCH, npair),
        in_specs=[
            pl.BlockSpec((CTX_LEN, LANES), lambda b, p: (cblk + b, p)),
            pl.BlockSpec((CTX_LEN, LANES), lambda b, p: (cblk + b, npair + p)),
            pl.BlockSpec((CTX_LEN, LANES), lambda b, p: (cblk + b, 2 * npair + p)),
        ],
        out_specs=pl.BlockSpec((CTX_LEN, LANES), lambda b, p: (b, p)),
        out_shape=jax.ShapeDtypeStruct((T_CTX, D_NA), BF16),
        compiler_params=_cparams("arbitrary", "arbitrary"),
    )(qkv, qkv, qkv)


ROUTE_COLS = N_GROUPS + N_EXPERTS


def _to_token_major(ref, val):
    rows = val.shape[0]
    for j in range(D_MODEL // LANES):
        ref[pl.ds(j, rows, stride=SUBLANES), :] = val[:, j * LANES:(j + 1) * LANES]


def _from_token_major(ref, rows, j):
    return ref[pl.ds(j, rows, stride=SUBLANES), :]


def _outproj_kernel(xa_ref, xb_ref, ffa_ref, ffb_ref, ata_ref, atb_ref, cva_ref, cvb_ref,
                    mod_ref, n2_ref, wo_ref, wr_ref, br_ref, x1_ref, hx_ref, route_ref):
    mix_in = jnp.concatenate([_pick_stream(ffa_ref, ffb_ref), _pick_stream(ata_ref, atb_ref),
                              _pick_stream(cva_ref, cvb_ref)], axis=-1)
    mix = jnp.dot(mix_in, wo_ref[0], preferred_element_type=F32)
    g1 = mod_ref[:, 2 * D_MODEL:3 * D_MODEL]
    x1 = _pick_stream(xa_ref, xb_ref) + g1 * mix
    x1_ref[...] = x1
    ms = jnp.mean(x1 * x1, axis=-1, keepdims=True)
    y = x1 * lax.rsqrt(ms + EPS) * n2_ref[0]
    sh2 = mod_ref[:, 3 * D_MODEL:4 * D_MODEL]
    sc2 = mod_ref[:, 4 * D_MODEL:5 * D_MODEL]
    hx = y * (1.0 + sc2) + sh2
    _to_token_major(hx_ref, hx)

    hx_hi = hx.astype(BF16)
    hx_lo = (hx - hx_hi.astype(F32)).astype(BF16)
    part = (jnp.dot(hx_hi, wr_ref[0], preferred_element_type=F32)
            + jnp.dot(hx_lo, wr_ref[0], preferred_element_type=F32))
    logits = part[:, 0:LANES] + part[:, LANES:2 * LANES] + br_ref[0]
    tm = logits.shape[0]
    lane = lax.broadcasted_iota(jnp.int32, (tm, LANES), 1)
    big = jnp.int32(LANES)
    gl = jnp.where(lane < N_GROUPS, logits, -jnp.inf)
    gmax = jnp.max(gl, axis=1, keepdims=True)
    gidx = jnp.min(jnp.where(gl == gmax, lane, big), axis=1, keepdims=True)
    g_w = 1.0 / jnp.sum(jnp.exp(gl - gmax), axis=1, keepdims=True)
    e_lane = lane - N_GROUPS
    in_group = (e_lane >= 0) & (e_lane < N_EXPERTS) & ((e_lane >> 3) == gidx)
    es = jnp.where(in_group, logits, -jnp.inf)
    t1 = jnp.max(es, axis=1, keepdims=True)
    i1 = jnp.min(jnp.where(es == t1, lane, big), axis=1, keepdims=True)
    es2 = jnp.where(lane == i1, -jnp.inf, es)
    t2 = jnp.max(es2, axis=1, keepdims=True)
    i2 = jnp.min(jnp.where(es2 == t2, lane, big), axis=1, keepdims=True)
    dlt = jnp.exp(t2 - t1)
    w1 = g_w / (1.0 + dlt)
    w2 = g_w * dlt / (1.0 + dlt)
    e1 = (i1 - N_GROUPS).astype(F32)
    e2 = (i2 - N_GROUPS).astype(F32)
    route_ref[...] = jnp.where(lane == 0, e1, jnp.where(lane == 1, e2,
                               jnp.where(lane == 2, w1, jnp.where(lane == 3, w2, 0.0))))


def _out_projection(x_pair, ctx_first_block, ff_pair, at_pair, cv_pair, mod4, norm2, w_out_bf, w_r, b_r, li, n_tok):
    nt = n_tok // TM
    return pl.pallas_call(
        _outproj_kernel,
        name="out_projection",
        grid=(nt,),
        in_specs=[
            _lat_spec(D_MODEL), _ctx_spec(D_MODEL, ctx_first_block),
            _lat_spec(D_FOURIER), _ctx_spec(D_FOURIER, 0),
            _lat_spec(D_NA), _ctx_spec(D_NA, 0),
            _lat_spec(D_CONV), _ctx_spec(D_CONV, 0),
            pl.BlockSpec((None, None, 1, 6 * D_MODEL), lambda i: (li, _mod_row(i, TM), 0, 0)),
            pl.BlockSpec((1, 1, D_MODEL), lambda i: (li, 0, 0)),
            pl.BlockSpec((1, D_MODEL, D_MODEL), lambda i: (li, 0, 0)),
            pl.BlockSpec((1, D_MODEL, 2 * LANES), lambda i: (li, 0, 0)),
            pl.BlockSpec((1, 1, LANES), lambda i: (li, 0, 0)),
        ],
        out_specs=[
            pl.BlockSpec((TM, D_MODEL), lambda i: (i, 0)),
            pl.BlockSpec((TM * SUBLANES, LANES), lambda i: (i, 0)),
            pl.BlockSpec((TM, LANES), lambda i: (i, 0)),
        ],
        out_shape=[
            jax.ShapeDtypeStruct((n_tok, D_MODEL), F32),
            jax.ShapeDtypeStruct((n_tok * SUBLANES, LANES), F32),
            jax.ShapeDtypeStruct((n_tok, LANES), F32),
        ],
        compiler_params=_cparams("arbitrary"),
    )(*x_pair, *ff_pair, *at_pair, *cv_pair, mod4, norm2.reshape(DEPTH, 1, D_MODEL), w_out_bf, w_r, b_r)


def _plan_kernel(route_ref, pos_ref, cnt_ref, ends_ref, carry_ref, offs_ref):
    ph = pl.program_id(0)
    i = pl.program_id(1)
    r = route_ref[...]
    tm = r.shape[0]
    lane = lax.broadcasted_iota(jnp.int32, (tm, LANES), 1)
    oh1 = lane == r[:, 0:1].astype(jnp.int32)
    oh2 = lane == r[:, 1:2].astype(jnp.int32)
    oh = jnp.where(oh1 | oh2, 1.0, 0.0)

    @pl.when((ph == 0) & (i == 0))
    def _():
        carry_ref[...] = jnp.zeros_like(carry_ref)

    @pl.when((ph == 1) & (i == 0))
    def _():
        cnt = carry_ref[...]
        cnt_ref[...] = jnp.broadcast_to(cnt, cnt_ref.shape)
        tiles = jnp.ceil(cnt * (1.0 / TMX))
        a = lax.broadcasted_iota(jnp.int32, (LANES, LANES), 0)
        b = lax.broadcasted_iota(jnp.int32, (LANES, LANES), 1)
        upper = jnp.where(a < b, 1.0, 0.0).astype(BF16)
        tiles8 = jnp.broadcast_to(tiles, (SUBLANES, LANES))
        first = jnp.dot(tiles8.astype(BF16), upper, preferred_element_type=F32)
        offs_ref[...] = first[0:1, :] * TMX
        ends_ref[...] = (first + tiles8) * TMX
        carry_ref[...] = jnp.zeros_like(carry_ref)

    @pl.when(ph == 1)
    def _():
        row = lax.broadcasted_iota(jnp.int32, (tm, tm), 0)
        col = lax.broadcasted_iota(jnp.int32, (tm, tm), 1)
        tri = jnp.where(row > col, 1.0, 0.0).astype(BF16)
        cum = jnp.dot(tri, oh.astype(BF16), preferred_element_type=F32) + (carry_ref[...] + offs_ref[...])
        p1 = jnp.sum(jnp.where(oh1, cum, 0.0), axis=1, keepdims=True)
        p2 = jnp.sum(jnp.where(oh2, cum, 0.0), axis=1, keepdims=True)
        pos_ref[...] = jnp.where(lane == 0, p1, jnp.where(lane == 1, p2, 0.0)).astype(jnp.int32)

    carry_ref[...] += jnp.sum(oh, axis=0, keepdims=True)


def _plan(route, n_tok):
    nt = n_tok // TM
    return pl.pallas_call(
        _plan_kernel,
        name="moe_plan",
        grid=(2, nt),
        in_specs=[pl.BlockSpec((TM, LANES), lambda ph, i: (i, 0))],
        out_specs=[
            pl.BlockSpec((TM, LANES), lambda ph, i: (i * ph, 0)),
            pl.BlockSpec((SUBLANES, LANES), lambda ph, i: (0, 0)),
            pl.BlockSpec((SUBLANES, LANES), lambda ph, i: (0, 0)),
        ],
        out_shape=[
            jax.ShapeDtypeStruct((n_tok, LANES), jnp.int32),
            jax.ShapeDtypeStruct((SUBLANES, LANES), F32),
            jax.ShapeDtypeStruct((SUBLANES, LANES), F32),
        ],
        scratch_shapes=[pltpu.VMEM((1, LANES), F32), pltpu.VMEM((1, LANES), F32)],
        compiler_params=_cparams("arbitrary", "arbitrary"),
    )(route)


def _row_tile(ref, row):
    return ref.at[pl.ds(pl.multiple_of(row * SUBLANES, SUBLANES), SUBLANES), :]


PAD_CHUNKS = tuple(1 << b for b in reversed(range(TMX.bit_length() - 1)))


def _dispatch_kernel(pos_ref, pstart_ref, plen_ref, tail_ref, hx_ref, xs_ref, zbuf, sem, zsem):
    i = pl.program_id(0)

    def pad_copies(fn):
        def ebody(e, carry):
            n = plen_ref[e]
            off = pstart_ref[e]
            for rows in PAD_CHUNKS:
                @pl.when((n & rows) != 0)
                def _():
                    fn(pltpu.make_async_copy(
                        zbuf.at[pl.ds(0, rows * SUBLANES), :],
                        xs_ref.at[pl.ds(pl.multiple_of(off * SUBLANES, SUBLANES), rows * SUBLANES), :], zsem.at[0]))
                off = off + (n & rows)
            return carry

        lax.fori_loop(0, N_EXPERTS, ebody, 0)

        def tbody(k, carry):
            row = pl.multiple_of((tail_ref[0] + k * PAD_CHUNKS[0]) * SUBLANES, SUBLANES)
            fn(pltpu.make_async_copy(zbuf, xs_ref.at[pl.ds(row, PAD_CHUNKS[0] * SUBLANES), :], zsem.at[0]))
            return carry

        lax.fori_loop(0, tail_ref[1], tbody, 0)

    @pl.when(i == 0)
    def _():
        zbuf[...] = jnp.zeros_like(zbuf)
        pad_copies(lambda cp: cp.start())

    def body(t, carry):
        src = _row_tile(hx_ref, t)
        for j in range(2):
            pltpu.make_async_copy(src, _row_tile(xs_ref, pos_ref[(i * TD + t) * 2 + j]), sem.at[0]).start(priority=j)
        return carry

    lax.fori_loop(0, TD, body, 0)
    nrow = TD * SUBLANES
    for j in range(2):
        pltpu.make_async_copy(hx_ref, xs_ref.at[pl.ds(0, nrow), :], sem.at[0]).wait()

    @pl.when(i == 0)
    def _():
        pad_copies(lambda cp: cp.wait())


def _dispatch(pos_flat, pad_start, pad_len, tail, hx_tm, n_tok, n_rows):
    return pl.pallas_call(
        _dispatch_kernel,
        name="moe_dispatch",
        grid_spec=pltpu.PrefetchScalarGridSpec(
            num_scalar_prefetch=4,
            grid=(n_tok // TD,),
            in_specs=[pl.BlockSpec((TD * SUBLANES, LANES), lambda i, pos, ps, pn, tl: (i, 0))],
            out_specs=pl.BlockSpec(memory_space=pl.ANY),
            scratch_shapes=[pltpu.VMEM((PAD_CHUNKS[0] * SUBLANES, LANES), F32),
                            pltpu.SemaphoreType.DMA((1,)), pltpu.SemaphoreType.DMA((1,))],
        ),
        out_shape=jax.ShapeDtypeStruct((n_rows * SUBLANES, LANES), F32),
        compiler_params=_cparams("arbitrary"),
    )(pos_flat, pad_start, pad_len, tail, hx_tm)


def _moe_kernel(te_ref, na_ref, xs_ref, wg_ref, wu_ref, wd_ref, os_ref, wg_s, wu_s, wd_s):
    i = pl.program_id(0)
    prev = te_ref[jnp.maximum(i - 1, 0)]
    active = i < na_ref[0]

    @pl.when(active & ((i == 0) | (te_ref[i] != prev)))
    def _():
        wg_s[...] = wg_ref[0, 0].astype(BF16)
        wu_s[...] = wu_ref[0, 0].astype(BF16)
        wd_s[...] = wd_ref[0, 0].astype(BF16)

    @pl.when(active)
    def _():
        x = jnp.concatenate([_from_token_major(xs_ref, TMX, j).astype(BF16)
                             for j in range(D_MODEL // LANES)], axis=-1)
        g = jnp.dot(x, wg_s[...], preferred_element_type=F32)
        u = jnp.dot(x, wu_s[...], preferred_element_type=F32)
        h = (g * jax.nn.sigmoid(g) * u).astype(BF16)
        _to_token_major(os_ref, jnp.dot(h, wd_s[...], preferred_element_type=F32))

    @pl.when(jnp.logical_not(active))
    def _():
        os_ref[...] = jnp.zeros_like(os_ref)


def _moe(tile_expert, n_active, xs, w_gate, w_up, w_down, li, n_tiles):
    def row_map(i, te, na):
        return (jnp.minimum(i, na[0] - 1), 0)

    def w_map(i, te, na):
        return (li, te[jnp.minimum(i, na[0] - 1)], 0, 0)

    return pl.pallas_call(
        _moe_kernel,
        name="moe_experts",
        grid_spec=pltpu.PrefetchScalarGridSpec(
            num_scalar_prefetch=2,
            grid=(n_tiles,),
            in_specs=[
                pl.BlockSpec((TMX * SUBLANES, LANES), row_map),
                pl.BlockSpec((1, 1, D_MODEL, D_EXPERT), w_map),
                pl.BlockSpec((1, 1, D_MODEL, D_EXPERT), w_map),
                pl.BlockSpec((1, 1, D_EXPERT, D_MODEL), w_map),
            ],
            out_specs=pl.BlockSpec((TMX * SUBLANES, LANES), lambda i, te, na: (i, 0)),
            scratch_shapes=[
                pltpu.VMEM((D_MODEL, D_EXPERT), BF16),
                pltpu.VMEM((D_MODEL, D_EXPERT), BF16),
                pltpu.VMEM((D_EXPERT, D_MODEL), BF16),
            ],
        ),
        out_shape=jax.ShapeDtypeStruct((n_tiles * TMX * SUBLANES, LANES), F32),
        compiler_params=_cparams("arbitrary"),
    )(tile_expert, n_active, xs, w_gate, w_up, w_down)


def _combine_kernel(pos_ref, x1_ref, route_ref, mod_ref, nf_ref, os_ref, o_ref, buf0, buf1, sem, *, final_norm):
    i = pl.program_id(0)

    def body(t, carry):
        tok = i * TC + t
        pltpu.make_async_copy(_row_tile(os_ref, pos_ref[tok * 2]), _row_tile(buf0, t), sem.at[0]).start(priority=0)
        pltpu.make_async_copy(_row_tile(os_ref, pos_ref[tok * 2 + 1]), _row_tile(buf1, t), sem.at[1]).start(priority=1)
        return carry

    lax.fori_loop(0, TC, body, 0)
    nrow = TC * SUBLANES
    pltpu.make_async_copy(os_ref.at[pl.ds(0, nrow), :], buf0, sem.at[0]).wait()
    pltpu.make_async_copy(os_ref.at[pl.ds(0, nrow), :], buf1, sem.at[1]).wait()

    w1 = route_ref[:, 2:3]
    w2 = route_ref[:, 3:4]
    cols = []
    for j in range(D_MODEL // LANES):
        y = w1 * _from_token_major(buf0, TC, j) + w2 * _from_token_major(buf1, TC, j)
        g2 = mod_ref[:, 5 * D_MODEL + j * LANES:5 * D_MODEL + (j + 1) * LANES]
        cols.append(x1_ref[:, j * LANES:(j + 1) * LANES] + g2 * y)
    x2 = jnp.concatenate(cols, axis=-1)
    if final_norm:
        ms = jnp.mean(x2 * x2, axis=-1, keepdims=True)
        x2 = x2 * lax.rsqrt(ms + EPS) * nf_ref[...]
    o_ref[...] = x2


def _combine(pos_flat, x1, route, mod4, norm_final, osrt, li, n_tok, final_norm):
    def tmap(i, pos):
        return (i, 0)

    return pl.pallas_call(
        functools.partial(_combine_kernel, final_norm=final_norm),
        name="moe_combine",
        grid_spec=pltpu.PrefetchScalarGridSpec(
            num_scalar_prefetch=1,
            grid=(n_tok // TC,),
            in_specs=[
                pl.BlockSpec((TC, D_MODEL), tmap),
                pl.BlockSpec((TC, LANES), tmap),
                pl.BlockSpec((None, None, 1, 6 * D_MODEL), lambda i, pos: (li, _mod_row(i, TC), 0, 0)),
                pl.BlockSpec((1, D_MODEL), lambda i, pos: (0, 0)),
                pl.BlockSpec(memory_space=pl.ANY),
            ],
            out_specs=pl.BlockSpec((TC, D_MODEL), tmap),
            scratch_shapes=[
                pltpu.VMEM((TC * SUBLANES, LANES), F32),
                pltpu.VMEM((TC * SUBLANES, LANES), F32),
                pltpu.SemaphoreType.DMA((2,)),
            ],
        ),
        out_shape=jax.ShapeDtypeStruct((n_tok, D_MODEL), F32),
        compiler_params=_cparams("arbitrary"),
    )(pos_flat, x1, route, mod4, norm_final.reshape(1, D_MODEL), osrt)


def _moe_block(x1, hx_tm, route, mod4, norm_final, w_gate, w_up, w_down, li, n_tok, final_norm):
    n_tiles = (2 * n_tok) // TMX + N_EXPERTS
    n_rows = n_tiles * TMX
    pos, cnt, ends = _plan(route, n_tok)
    pos = pos[:, 0:2].reshape(-1)
    cnt = cnt[0, :N_EXPERTS].astype(jnp.int32)
    ends = ends[0, :N_EXPERTS].astype(jnp.int32)
    padded = ((cnt + TMX - 1) // TMX) * TMX
    pad_start = ends - padded + cnt
    tile_start = jnp.arange(n_tiles, dtype=jnp.int32) * TMX
    tile_expert = jnp.minimum(jnp.sum((tile_start[:, None] >= ends[None, :]).astype(jnp.int32), axis=1),
                              N_EXPERTS - 1)
    n_active = (ends[-1:] // TMX).astype(jnp.int32)
    tail = jnp.concatenate([ends[-1:], (n_rows - ends[-1:]) // PAD_CHUNKS[0]])

    xs = _dispatch(pos, pad_start, padded - cnt, tail, hx_tm, n_tok, n_rows)
    osrt = _moe(tile_expert, n_active, xs, w_gate, w_up, w_down, li, n_tiles)
    return _combine(pos, x1, route, mod4, norm_final, osrt, li, n_tok, final_norm)


def kernel(x, c, ctx, c_ctx, w_ada, b_ada, norm1, norm2, w_in, w_fourier, w_conv, rpb, w_out, w_rg, b_rg,
           w_re, b_re, w_gate, w_up, w_down, norm_final):
    x_pair, ctx_first = (x.reshape(T_LAT, D_MODEL), ctx.reshape(T_CTX, D_MODEL)), 0
    cc =jnp.concatenate([c, c_ctx[None, :], jnp.zeros((MOD_ROWS - BATCH - 1, D_MODEL), F32)], axis=0)
    mod4 = _modulation(cc, w_ada, b_ada).reshape(DEPTH, MOD_ROWS, 1, 6 * D_MODEL)

    w_in_bf = w_in.astype(BF16)
    w_out_bf = w_out.astype(BF16)
    pad = jnp.zeros((DEPTH, D_MODEL, LANES - ROUTE_COLS), F32)
    w_r = jnp.concatenate([w_rg, w_re, pad], axis=-1)
    w_r_hi = w_r.astype(BF16)
    w_r = jnp.concatenate([w_r_hi, (w_r - w_r_hi.astype(F32)).astype(BF16)], axis=-1)
    b_r = jnp.concatenate([b_rg, b_re, pad[:, 0, :]], axis=-1).reshape(DEPTH, 1, LANES)
    dft_ch = jnp.asarray(_channel_dft(), dtype=F32).astype(BF16)
    ctx_blk = T_LAT // CTX_LEN

    for li in range(DEPTH):
        last = li == DEPTH - 1
        yf, qkv, cv = _in_projection(*x_pair, ctx_first, mod4, norm1, w_in_bf, dft_ch, li)
        table = _bias_table(rpb[li].reshape(-1))
        ff = _fourier_mix(yf, w_fourier, li, SEQ, 0)
        cvo = _short_conv(cv, w_conv, li, SEQ, 0)
        at = _attn_latent(qkv, table)
        if last:
            n_tok = T_LAT
            ff_c, cvo_c, at_c = ff, cvo, at
        else:
            n_tok = T_ALL
            ff_c = _fourier_mix(yf, w_fourier, li, CTX_LEN, ctx_blk)
            cvo_c = _short_conv(cv, w_conv, li, CTX_LEN, ctx_blk)
            at_c = _attn_ctx(qkv)
        x1, hx_tm, route = _out_projection(x_pair, ctx_first, (ff, ff_c), (at, at_c), (cvo, cvo_c), mod4, norm2,
                                           w_out_bf, w_r, b_r, li, n_tok)
        xt = _moe_block(x1, hx_tm, route, mod4, norm_final, w_gate, w_up, w_down, li, n_tok, last)
        if last:
            return xt.reshape(BATCH, SEQ, D_MODEL)
        x_pair, ctx_first = (xt, xt), NT_LAT
```

```python
import functools
import math

import numpy as np
import jax
import jax.numpy as jnp
from jax import lax
from jax.experimental import pallas as pl
from jax.experimental.pallas import tpu as pltpu

F32 = jnp.float32
BF16 = jnp.bfloat16

D_MODEL = 1024
BATCH = 8
SEQ = 2048
DEPTH = 2
GRID_W = 64
ROWS = SEQ // GRID_W
CTX_LEN = 256
T_LAT = BATCH * SEQ
T_CTX = BATCH * CTX_LEN
T_ALL = T_LAT + T_CTX

D_FOURIER = 256
D_FG = 64
HEAD_DIM = 64
NA_HEADS = 8
D_NA = NA_HEADS * HEAD_DIM
D_CONV = 256
D_IN_PROJ = D_FOURIER + 3 * D_NA + 3 * D_CONV
NA_KH = 8
NA_KW = 16
N_GROUPS = 4
EXPERTS_PER_GROUP = 8
N_EXPERTS = 32
D_EXPERT = 512
EPS = 1e-6
NEG_INF = -1e30

LANES = 128
SUBLANES = 8
MOD_ROWS = 16
TM = 512
TMX = 512
TD = 512
TC = 256
VMEM_LIMIT = 56 * 1024 * 1024


def _cparams(*sem):
    return pltpu.CompilerParams(dimension_semantics=sem, vmem_limit_bytes=VMEM_LIMIT)


def _ada_kernel(cc_ref, w_ref, b_ref, o_ref):
    s = cc_ref[...]
    s = s * jax.nn.sigmoid(s)
    acc = jnp.dot(s.astype(BF16), w_ref[0].astype(BF16), preferred_element_type=F32)
    o_ref[0] = acc + b_ref[0]


def _modulation(cc, w_ada, b_ada):
    tn = 1536
    nj = 6 * D_MODEL // tn
    return pl.pallas_call(
        _ada_kernel,
        name="ada_modulation",
        grid=(DEPTH, nj),
        in_specs=[
            pl.BlockSpec((MOD_ROWS, D_MODEL), lambda l, j: (0, 0)),
            pl.BlockSpec((1, D_MODEL, tn), lambda l, j: (l, 0, j)),
            pl.BlockSpec((1, 1, tn), lambda l, j: (l, 0, j)),
        ],
        out_specs=pl.BlockSpec((1, MOD_ROWS, tn), lambda l, j: (l, 0, j)),
        out_shape=jax.ShapeDtypeStruct((DEPTH, MOD_ROWS, 6 * D_MODEL), F32),
        compiler_params=_cparams("arbitrary", "arbitrary"),
    )(cc, w_ada, b_ada.reshape(DEPTH, 1, 6 * D_MODEL))


def _mod_row(i, tile):
    return jnp.minimum((i * tile) // SEQ, BATCH)


NT_LAT = T_LAT // TM


def _lat_spec(cols):
    return pl.BlockSpec((TM, cols), lambda i: (jnp.minimum(i, NT_LAT - 1), 0))


def _ctx_spec(cols, first_block):
    return pl.BlockSpec((TM, cols), lambda i: (jnp.maximum(i - NT_LAT, 0) + first_block, 0))


def _pick_stream(lat_ref, ctx_ref):
    return jnp.where(pl.program_id(0) >= NT_LAT, ctx_ref[...], lat_ref[...])


def _inproj_kernel(xa_ref, xb_ref, mod_ref, n1_ref, w_ref, dft_ref, yf_ref, qkv_ref, cv_ref):
    x = _pick_stream(xa_ref, xb_ref)
    ms = jnp.mean(x * x, axis=-1, keepdims=True)
    y = x * lax.rsqrt(ms + EPS) * n1_ref[0]
    sh = mod_ref[:, 0:D_MODEL]
    sc = mod_ref[:, D_MODEL:2 * D_MODEL]
    h = y * (1.0 + sc) + sh
    u = jnp.dot(h.astype(BF16), w_ref[0], preferred_element_type=F32)
    yf = jnp.dot(u[:, 0:D_FOURIER].astype(BF16), dft_ref[...], preferred_element_type=F32)
    yf_ref[...] = yf.astype(BF16)
    q0 = D_FOURIER
    qkv_ref[:, 0:D_NA] = (u[:, q0:q0 + D_NA] * (1.0 / math.sqrt(HEAD_DIM))).astype(BF16)
    qkv_ref[:, D_NA:3 * D_NA] = u[:, q0 + D_NA:q0 + 3 * D_NA].astype(BF16)
    cv_ref[...] = u[:, q0 + 3 * D_NA:D_IN_PROJ]


def _in_projection(x_lat, x_ctx, ctx_first_block, mod4, norm1, w_in_bf, dft_ch, li):
    nt = T_ALL // TM
    return pl.pallas_call(
        _inproj_kernel,
        name="in_projection",
        grid=(nt,),
        in_specs=[
            _lat_spec(D_MODEL),
            _ctx_spec(D_MODEL, ctx_first_block),
            pl.BlockSpec((None, None, 1, 6 * D_MODEL), lambda i: (li, _mod_row(i, TM), 0, 0)),
            pl.BlockSpec((1, 1, D_MODEL), lambda i: (li, 0, 0)),
            pl.BlockSpec((1, D_MODEL, D_IN_PROJ), lambda i: (li, 0, 0)),
            pl.BlockSpec((D_FOURIER, 2 * D_FOURIER), lambda i: (0, 0)),
        ],
        out_specs=[
            pl.BlockSpec((TM, 2 * D_FOURIER), lambda i: (i, 0)),
            pl.BlockSpec((TM, 3 * D_NA), lambda i: (i, 0)),
            pl.BlockSpec((TM, 3 * D_CONV), lambda i: (i, 0)),
        ],
        out_shape=[
            jax.ShapeDtypeStruct((T_ALL, 2 * D_FOURIER), BF16),
            jax.ShapeDtypeStruct((T_ALL, 3 * D_NA), BF16),
            jax.ShapeDtypeStruct((T_ALL, 3 * D_CONV), F32),
        ],
        compiler_params=_cparams("arbitrary"),
    )(x_lat, x_ctx, mod4, norm1.reshape(DEPTH, 1, D_MODEL), w_in_bf, dft_ch)


def _dft_tables(n):
    k = np.arange(n, dtype=np.int64)
    ang = 2.0 * np.pi * ((k[:, None] * k[None, :]) % n).astype(np.float64) / n
    s = 1.0 / math.sqrt(n)
    return np.cos(ang) * s, np.sin(ang) * s


def _channel_dft():
    c, s = _dft_tables(D_FG)
    eye = np.eye(D_FOURIER // D_FG)
    return np.concatenate([np.kron(eye, c), np.kron(eye, s)], axis=1)


def _fourier_kernel(cn_ref, sn_ref, y_ref, wf_ref, o_ref):
    z = (jnp.dot(cn_ref[...], y_ref[:, 0:D_FOURIER], preferred_element_type=F32)
         - jnp.dot(sn_ref[...], y_ref[:, D_FOURIER:2 * D_FOURIER], preferred_element_type=F32))
    o_ref[...] = jnp.dot(z.astype(BF16), wf_ref[0].astype(BF16), preferred_element_type=F32).astype(BF16)


def _fourier_mix(yf, w_fourier, li, n, first_block):
    tk = min(n, 512)
    nk = n // tk
    cn, sn = _dft_tables(n)
    cn = jnp.asarray(cn, dtype=F32).astype(BF16)
    sn = jnp.asarray(sn, dtype=F32).astype(BF16)
    return pl.pallas_call(
        _fourier_kernel,
        name="fourier_mix",
        grid=(nk, BATCH),
        in_specs=[
            pl.BlockSpec((tk, n), lambda k, b: (k, 0)),
            pl.BlockSpec((tk, n), lambda k, b: (k, 0)),
            pl.BlockSpec((n, 2 * D_FOURIER), lambda k, b: (first_block + b, 0)),
            pl.BlockSpec((1, D_FOURIER, D_FOURIER), lambda k, b: (li, 0, 0)),
        ],
        out_specs=pl.BlockSpec((tk, D_FOURIER), lambda k, b: (b * nk + k, 0)),
        out_shape=jax.ShapeDtypeStruct((BATCH * n, D_FOURIER), BF16),
        compiler_params=_cparams("arbitrary", "arbitrary"),
    )(cn, sn, yf, w_fourier)


def _conv_kernel(cv_ref, wc_ref, o_ref, pad_ref):
    n = cv_ref.shape[0]
    gb = cv_ref[:, 0:D_CONV]
    g = cv_ref[:, D_CONV:2 * D_CONV] * cv_ref[:, 2 * D_CONV:3 * D_CONV]
    zero = jnp.zeros((SUBLANES, D_CONV), F32)
    pad_ref[0:SUBLANES, :] = zero
    pad_ref[n + SUBLANES:n + 2 * SUBLANES, :] = zero
    pad_ref[SUBLANES:n + SUBLANES, :] = g
    prev = pad_ref[SUBLANES - 1:n + SUBLANES - 1, :]
    nxt = pad_ref[SUBLANES + 1:n + SUBLANES + 1, :]
    w = wc_ref[0]
    o_ref[...] = (gb * (w[0:1, :] * prev + w[1:2, :] * g + w[2:3, :] * nxt)).astype(BF16)


def _short_conv(cv, w_conv, li, n, first_block):
    return pl.pallas_call(
        _conv_kernel,
        name="short_conv",
        grid=(BATCH,),
        in_specs=[
            pl.BlockSpec((n, 3 * D_CONV), lambda b: (first_block + b, 0)),
            pl.BlockSpec((1, 3, D_CONV), lambda b: (li, 0, 0)),
        ],
        out_specs=pl.BlockSpec((n, D_CONV), lambda b: (b, 0)),
        out_shape=jax.ShapeDtypeStruct((BATCH * n, D_CONV), BF16),
        scratch_shapes=[pltpu.VMEM((n + 2 * SUBLANES, D_CONV), F32)],
        compiler_params=_cparams("arbitrary"),
    )(cv, w_conv)


RPB_H = 2 * NA_KH - 1
RPB_W = 2 * NA_KW - 1
KEYS_LOC = NA_KH * GRID_W


def _bias_kernel(rpb_ref, o_ref):
    h = pl.program_id(0)
    qi = lax.broadcasted_iota(jnp.int32, (GRID_W, LANES), 0)
    lj = lax.broadcasted_iota(jnp.int32, (GRID_W, LANES), 1)
    kc = lj & (GRID_W - 1)
    hi = lj >= GRID_W
    d = kc - qi + (NA_KW - 1)
    cs = jnp.clip(qi - NA_KW // 2, 0, GRID_W - NA_KW)
    valid = (kc >= cs) & (kc < cs + NA_KW)
    tiles = []
    for a in range(RPB_H - 1):
        acc = jnp.zeros((GRID_W, LANES), F32)
        for b in range(RPB_W):
            va = rpb_ref[(h * RPB_H + a) * RPB_W + b]
            vb = rpb_ref[(h * RPB_H + a + 1) * RPB_W + b]
            acc = jnp.where(d == b, jnp.where(hi, vb, va), acc)
        tiles.append(jnp.where(valid, acc, NEG_INF))
    for cls in range(NA_KH):
        for m in range(NA_KH // 2):
            o_ref[0, cls, :, m * LANES:(m + 1) * LANES] = tiles[2 * m - cls + NA_KH - 1]


def _bias_table(rpb_flat):
    return pl.pallas_call(
        _bias_kernel,
        name="attn_bias_table",
        grid=(NA_HEADS,),
        in_specs=[pl.BlockSpec(memory_space=pltpu.SMEM)],
        out_specs=pl.BlockSpec((1, NA_KH, GRID_W, KEYS_LOC), lambda h: (h // 2, 0, h % 2, 0)),
        out_shape=jax.ShapeDtypeStruct((NA_HEADS // 2, NA_KH, 2 * GRID_W, KEYS_LOC), F32),
        compiler_params=_cparams("arbitrary"),
    )(rpb_flat)


_NT_DIMS = (((1,), (1,)), ((), ()))


def _softmax_pv(parts):
    m = None
    for s, _ in parts:
        ms = jnp.max(s, axis=1, keepdims=True)
        m = ms if m is None else jnp.maximum(m, ms)
    l = None
    o = None
    for s, v in parts:
        p = jnp.exp(s - m)
        ls = jnp.sum(p, axis=1, keepdims=True)
        os_ = jnp.dot(p.astype(BF16), v, preferred_element_type=F32)
        l = ls if l is None else l + ls
        o = os_ if o is None else o + os_
    return o / l


ATT_UNROLL = 4


def _attn_latent_kernel(q_ref, k_ref, v_ref, kc_ref, vc_ref, tab_ref, o_ref, ve_ref, vce_ref):
    lane = lax.broadcasted_iota(jnp.int32, (GRID_W, LANES), 1)
    lo = lane < HEAD_DIM
    ve_ref[:, 0:LANES] = v_ref[...]
    ve_ref[:, LANES:2 * LANES] = jnp.ones((SEQ, LANES), BF16)
    vce_ref[:, 0:LANES] = vc_ref[...]
    vce_ref[:, LANES:2 * LANES] = jnp.ones((CTX_LEN, LANES), BF16)
    kc = kc_ref[...]
    vce = vce_ref[...]

    def one_row(r):
        rs = jnp.clip(r - NA_KH // 2, 0, ROWS - NA_KH)
        cls = r - rs
        q = q_ref[pl.ds(pl.multiple_of(r * GRID_W, GRID_W), GRID_W), :]
        kl = k_ref[pl.ds(pl.multiple_of(rs * GRID_W, GRID_W), KEYS_LOC), :]
        vl = ve_ref[pl.ds(pl.multiple_of(rs * GRID_W, GRID_W), KEYS_LOC), :]
        zero = jnp.zeros_like(q)
        q2 = jnp.concatenate([jnp.where(lo, q, zero), jnp.where(lo, zero, q)], axis=0)
        s1 = lax.dot_general(q2, kl, _NT_DIMS, preferred_element_type=F32) + tab_ref[0, cls]
        s2 = lax.dot_general(q2, kc, _NT_DIMS, preferred_element_type=F32)
        m = jnp.maximum(jnp.max(s1, axis=1, keepdims=True), jnp.max(s2, axis=1, keepdims=True))
        p1 = jnp.exp(s1 - m).astype(BF16)
        p2 = jnp.exp(s2 - m).astype(BF16)
        oe = (jnp.dot(p1, vl, preferred_element_type=F32) + jnp.dot(p2, vce, preferred_element_type=F32))
        o = oe[:, 0:LANES] / oe[:, LANES:2 * LANES]
        o_ref[pl.ds(pl.multiple_of(r * GRID_W, GRID_W), GRID_W), :] = (
            jnp.where(lo, o[0:GRID_W], o[GRID_W:2 * GRID_W]).astype(BF16))

    def body(it, carry):
        for u in range(ATT_UNROLL):
            one_row(it * ATT_UNROLL + u)
        return carry

    lax.fori_loop(0, ROWS // ATT_UNROLL, body, 0)


def _attn_latent(qkv, table):
    npair = NA_HEADS // 2
    cblk = T_LAT // CTX_LEN
    return pl.pallas_call(
        _attn_latent_kernel,
        name="attn_latent",
        grid=(BATCH, npair),
        in_specs=[
            pl.BlockSpec((SEQ, LANES), lambda b, p: (b, p)),
            pl.BlockSpec((SEQ, LANES), lambda b, p: (b, npair + p)),
            pl.BlockSpec((SEQ, LANES), lambda b, p: (b, 2 * npair + p)),
            pl.BlockSpec((CTX_LEN, LANES), lambda b, p: (cblk + b, npair + p)),
            pl.BlockSpec((CTX_LEN, LANES), lambda b, p: (cblk + b, 2 * npair + p)),
            pl.BlockSpec((1, NA_KH, 2 * GRID_W, KEYS_LOC), lambda b, p: (p, 0, 0, 0)),
        ],
        out_specs=pl.BlockSpec((SEQ, LANES), lambda b, p: (b, p)),
        out_shape=jax.ShapeDtypeStruct((T_LAT, D_NA), BF16),
        scratch_shapes=[pltpu.VMEM((SEQ, 2 * LANES), BF16), pltpu.VMEM((CTX_LEN, 2 * LANES), BF16)],
        compiler_params=_cparams("arbitrary", "arbitrary"),
    )(qkv, qkv, qkv, qkv, qkv, table)


def _attn_ctx_kernel(q_ref, k_ref, v_ref, o_ref):
    lane = lax.broadcasted_iota(jnp.int32, (CTX_LEN, LANES), 1)
    lo = lane < HEAD_DIM
    q = q_ref[...]
    k = k_ref[...]
    v = v_ref[...]
    outs = []
    for hh in range(2):
        qm = jnp.where(lo if hh == 0 else jnp.logical_not(lo), q, jnp.zeros_like(q))
        s = lax.dot_general(qm, k, _NT_DIMS, preferred_element_type=F32)
        outs.append(_softmax_pv([(s, v)]))
    o_ref[...] = jnp.where(lo, outs[0], outs[1]).astype(BF16)


def _attn_ctx(qkv):
    npair = NA_HEADS // 2
    cblk = T_LAT // CTX_LEN
    return pl.pallas_call(
        _attn_ctx_kernel,
        name="attn_context",
        grid=(BATCH, npair),
        in_specs=[
            pl.BlockSpec((CTX_LEN, LANES), lambda b, p: (cblk + b, p)),
            pl.BlockSpec((CTX_LEN, LANES), lambda b, p: (cblk + b, npair + p)),
            pl.BlockSpec((CTX_LEN, LANES), lambda b, p: (cblk + b, 2 * npair + p)),
        ],
        out_specs=pl.BlockSpec((CTX_LEN, LANES), lambda b, p: (b, p)),
        out_shape=jax.ShapeDtypeStruct((T_CTX, D_NA), BF16),
        compiler_params=_cparams("arbitrary", "arbitrary"),
    )(qkv, qkv, qkv)


ROUTE_COLS = N_GROUPS + N_EXPERTS


def _to_token_major(ref, val):
    rows = val.shape[0]
    for j in range(D_MODEL // LANES):
        ref[pl.ds(j, rows, stride=SUBLANES), :] = val[:, j * LANES:(j + 1) * LANES]


def _from_token_major(ref, rows, j):
    return ref[pl.ds(j, rows, stride=SUBLANES), :]


def _outproj_kernel(xa_ref, xb_ref, ffa_ref, ffb_ref, ata_ref, atb_ref, cva_ref, cvb_ref,
                    mod_ref, n2_ref, wo_ref, wr_ref, br_ref, x1_ref, hx_ref, route_ref):
    mix_in = jnp.concatenate([_pick_stream(ffa_ref, ffb_ref), _pick_stream(ata_ref, atb_ref),
                              _pick_stream(cva_ref, cvb_ref)], axis=-1)
    mix = jnp.dot(mix_in, wo_ref[0], preferred_element_type=F32)
    g1 = mod_ref[:, 2 * D_MODEL:3 * D_MODEL]
    x1 = _pick_stream(xa_ref, xb_ref) + g1 * mix
    x1_ref[...] = x1
    ms = jnp.mean(x1 * x1, axis=-1, keepdims=True)
    y = x1 * lax.rsqrt(ms + EPS) * n2_ref[0]
    sh2 = mod_ref[:, 3 * D_MODEL:4 * D_MODEL]
    sc2 = mod_ref[:, 4 * D_MODEL:5 * D_MODEL]
    hx = y * (1.0 + sc2) + sh2
    _to_token_major(hx_ref, hx)

    hx_hi = hx.astype(BF16)
    hx_lo = (hx - hx_hi.astype(F32)).astype(BF16)
    part = (jnp.dot(hx_hi, wr_ref[0], preferred_element_type=F32)
            + jnp.dot(hx_lo, wr_ref[0], preferred_element_type=F32))
    logits = part[:, 0:LANES] + part[:, LANES:2 * LANES] + br_ref[0]
    tm = logits.shape[0]
    lane = lax.broadcasted_iota(jnp.int32, (tm, LANES), 1)
    big = jnp.int32(LANES)
    gl = jnp.where(lane < N_GROUPS, logits, -jnp.inf)
    gmax = jnp.max(gl, axis=1, keepdims=True)
    gidx = jnp.min(jnp.where(gl == gmax, lane, big), axis=1, keepdims=True)
    g_w = 1.0 / jnp.sum(jnp.exp(gl - gmax), axis=1, keepdims=True)
    e_lane = lane - N_GROUPS
    in_group = (e_lane >= 0) & (e_lane < N_EXPERTS) & ((e_lane >> 3) == gidx)
    es = jnp.where(in_group, logits, -jnp.inf)
    t1 = jnp.max(es, axis=1, keepdims=True)
    i1 = jnp.min(jnp.where(es == t1, lane, big), axis=1, keepdims=True)
    es2 = jnp.where(lane == i1, -jnp.inf, es)
    t2 = jnp.max(es2, axis=1, keepdims=True)
    i2 = jnp.min(jnp.where(es2 == t2, lane, big), axis=1, keepdims=True)
    dlt = jnp.exp(t2 - t1)
    w1 = g_w / (1.0 + dlt)
    w2 = g_w * dlt / (1.0 + dlt)
    e1 = (i1 - N_GROUPS).astype(F32)
    e2 = (i2 - N_GROUPS).astype(F32)
    route_ref[...] = jnp.where(lane == 0, e1, jnp.where(lane == 1, e2,
                               jnp.where(lane == 2, w1, jnp.where(lane == 3, w2, 0.0))))


def _out_projection(x_pair, ctx_first_block, ff_pair, at_pair, cv_pair, mod4, norm2, w_out_bf, w_r, b_r, li, n_tok):
    nt = n_tok // TM
    return pl.pallas_call(
        _outproj_kernel,
        name="out_projection",
        grid=(nt,),
        in_specs=[
            _lat_spec(D_MODEL), _ctx_spec(D_MODEL, ctx_first_block),
            _lat_spec(D_FOURIER), _ctx_spec(D_FOURIER, 0),
            _lat_spec(D_NA), _ctx_spec(D_NA, 0),
            _lat_spec(D_CONV), _ctx_spec(D_CONV, 0),
            pl.BlockSpec((None, None, 1, 6 * D_MODEL), lambda i: (li, _mod_row(i, TM), 0, 0)),
            pl.BlockSpec((1, 1, D_MODEL), lambda i: (li, 0, 0)),
            pl.BlockSpec((1, D_MODEL, D_MODEL), lambda i: (li, 0, 0)),
            pl.BlockSpec((1, D_MODEL, 2 * LANES), lambda i: (li, 0, 0)),
            pl.BlockSpec((1, 1, LANES), lambda i: (li, 0, 0)),
        ],
        out_specs=[
            pl.BlockSpec((TM, D_MODEL), lambda i: (i, 0)),
            pl.BlockSpec((TM * SUBLANES, LANES), lambda i: (i, 0)),
            pl.BlockSpec((TM, LANES), lambda i: (i, 0)),
        ],
        out_shape=[
            jax.ShapeDtypeStruct((n_tok, D_MODEL), F32),
            jax.ShapeDtypeStruct((n_tok * SUBLANES, LANES), F32),
            jax.ShapeDtypeStruct((n_tok, LANES), F32),
        ],
        compiler_params=_cparams("arbitrary"),
    )(*x_pair, *ff_pair, *at_pair, *cv_pair, mod4, norm2.reshape(DEPTH, 1, D_MODEL), w_out_bf, w_r, b_r)


def _plan_kernel(route_ref, pos_ref, cnt_ref, ends_ref, carry_ref, offs_ref):
    ph = pl.program_id(0)
    i = pl.program_id(1)
    r = route_ref[...]
    tm = r.shape[0]
    lane = lax.broadcasted_iota(jnp.int32, (tm, LANES), 1)
    oh1 = lane == r[:, 0:1].astype(jnp.int32)
    oh2 = lane == r[:, 1:2].astype(jnp.int32)
    oh = jnp.where(oh1 | oh2, 1.0, 0.0)

    @pl.when((ph == 0) & (i == 0))
    def _():
        carry_ref[...] = jnp.zeros_like(carry_ref)

    @pl.when((ph == 1) & (i == 0))
    def _():
        cnt = carry_ref[...]
        cnt_ref[...] = jnp.broadcast_to(cnt, cnt_ref.shape)
        tiles = jnp.ceil(cnt * (1.0 / TMX))
        a = lax.broadcasted_iota(jnp.int32, (LANES, LANES), 0)
        b = lax.broadcasted_iota(jnp.int32, (LANES, LANES), 1)
        upper = jnp.where(a < b, 1.0, 0.0).astype(BF16)
        tiles8 = jnp.broadcast_to(tiles, (SUBLANES, LANES))
        first = jnp.dot(tiles8.astype(BF16), upper, preferred_element_type=F32)
        offs_ref[...] = first[0:1, :] * TMX
        ends_ref[...] = (first + tiles8) * TMX
        carry_ref[...] = jnp.zeros_like(carry_ref)

    @pl.when(ph == 1)
    def _():
        row = lax.broadcasted_iota(jnp.int32, (tm, tm), 0)
        col = lax.broadcasted_iota(jnp.int32, (tm, tm), 1)
        tri = jnp.where(row > col, 1.0, 0.0).astype(BF16)
        cum = jnp.dot(tri, oh.astype(BF16), preferred_element_type=F32) + (carry_ref[...] + offs_ref[...])
        p1 = jnp.sum(jnp.where(oh1, cum, 0.0), axis=1, keepdims=True)
        p2 = jnp.sum(jnp.where(oh2, cum, 0.0), axis=1, keepdims=True)
        pos_ref[...] = jnp.where(lane == 0, p1, jnp.where(lane == 1, p2, 0.0)).astype(jnp.int32)

    carry_ref[...] += jnp.sum(oh, axis=0, keepdims=True)


def _plan(route, n_tok):
    nt = n_tok // TM
    return pl.pallas_call(
        _plan_kernel,
        name="moe_plan",
        grid=(2, nt),
        in_specs=[pl.BlockSpec((TM, LANES), lambda ph, i: (i, 0))],
        out_specs=[
            pl.BlockSpec((TM, LANES), lambda ph, i: (i * ph, 0)),
            pl.BlockSpec((SUBLANES, LANES), lambda ph, i: (0, 0)),
            pl.BlockSpec((SUBLANES, LANES), lambda ph, i: (0, 0)),
        ],
        out_shape=[
            jax.ShapeDtypeStruct((n_tok, LANES), jnp.int32),
            jax.ShapeDtypeStruct((SUBLANES, LANES), F32),
            jax.ShapeDtypeStruct((SUBLANES, LANES), F32),
        ],
        scratch_shapes=[pltpu.VMEM((1, LANES), F32), pltpu.VMEM((1, LANES), F32)],
        compiler_params=_cparams("arbitrary", "arbitrary"),
    )(route)


def _row_tile(ref, row):
    return ref.at[pl.ds(pl.multiple_of(row * SUBLANES, SUBLANES), SUBLANES), :]


PAD_CHUNKS = tuple(1 << b for b in reversed(range(TMX.bit_length() - 1)))


def _dispatch_kernel(pos_ref, pstart_ref, plen_ref, tail_ref, hx_ref, xs_ref, zbuf, sem, zsem):
    i = pl.program_id(0)

    def pad_copies(fn):
        def ebody(e, carry):
            n = plen_ref[e]
            off = pstart_ref[e]
            for rows in PAD_CHUNKS:
                @pl.when((n & rows) != 0)
                def _():
                    fn(pltpu.make_async_copy(
                        zbuf.at[pl.ds(0, rows * SUBLANES), :],
                        xs_ref.at[pl.ds(pl.multiple_of(off * SUBLANES, SUBLANES), rows * SUBLANES), :], zsem.at[0]))
                off = off + (n & rows)
            return carry

        lax.fori_loop(0, N_EXPERTS, ebody, 0)

        def tbody(k, carry):
            row = pl.multiple_of((tail_ref[0] + k * PAD_CHUNKS[0]) * SUBLANES, SUBLANES)
            fn(pltpu.make_async_copy(zbuf, xs_ref.at[pl.ds(row, PAD_CHUNKS[0] * SUBLANES), :], zsem.at[0]))
            return carry

        lax.fori_loop(0, tail_ref[1], tbody, 0)

    @pl.when(i == 0)
    def _():
        zbuf[...] = jnp.zeros_like(zbuf)
        pad_copies(lambda cp: cp.start())

    def body(t, carry):
        src = _row_tile(hx_ref, t)
        for j in range(2):
            pltpu.make_async_copy(src, _row_tile(xs_ref, pos_ref[(i * TD + t) * 2 + j]), sem.at[0]).start(priority=j)
        return carry

    lax.fori_loop(0, TD, body, 0)
    nrow = TD * SUBLANES
    for j in range(2):
        pltpu.make_async_copy(hx_ref, xs_ref.at[pl.ds(0, nrow), :], sem.at[0]).wait()

    @pl.when(i == 0)
    def _():
        pad_copies(lambda cp: cp.wait())


def _dispatch(pos_flat, pad_start, pad_len, tail, hx_tm, n_tok, n_rows):
    return pl.pallas_call(
        _dispatch_kernel,
        name="moe_dispatch",
        grid_spec=pltpu.PrefetchScalarGridSpec(
            num_scalar_prefetch=4,
            grid=(n_tok // TD,),
            in_specs=[pl.BlockSpec((TD * SUBLANES, LANES), lambda i, pos, ps, pn, tl: (i, 0))],
            out_specs=pl.BlockSpec(memory_space=pl.ANY),
            scratch_shapes=[pltpu.VMEM((PAD_CHUNKS[0] * SUBLANES, LANES), F32),
                            pltpu.SemaphoreType.DMA((1,)), pltpu.SemaphoreType.DMA((1,))],
        ),
        out_shape=jax.ShapeDtypeStruct((n_rows * SUBLANES, LANES), F32),
        compiler_params=_cparams("arbitrary"),
    )(pos_flat, pad_start, pad_len, tail, hx_tm)


def _moe_kernel(te_ref, na_ref, xs_ref, wg_ref, wu_ref, wd_ref, os_ref, wg_s, wu_s, wd_s):
    i = pl.program_id(0)
    prev = te_ref[jnp.maximum(i - 1, 0)]
    active = i < na_ref[0]

    @pl.when(active & ((i == 0) | (te_ref[i] != prev)))
    def _():
        wg_s[...] = wg_ref[0, 0].astype(BF16)
        wu_s[...] = wu_ref[0, 0].astype(BF16)
        wd_s[...] = wd_ref[0, 0].astype(BF16)

    @pl.when(active)
    def _():
        x = jnp.concatenate([_from_token_major(xs_ref, TMX, j).astype(BF16)
                             for j in range(D_MODEL // LANES)], axis=-1)
        g = jnp.dot(x, wg_s[...], preferred_element_type=F32)
        u = jnp.dot(x, wu_s[...], preferred_element_type=F32)
        h = (g * jax.nn.sigmoid(g) * u).astype(BF16)
        _to_token_major(os_ref, jnp.dot(h, wd_s[...], preferred_element_type=F32))

    @pl.when(jnp.logical_not(active))
    def _():
        os_ref[...] = jnp.zeros_like(os_ref)


def _moe(tile_expert, n_active, xs, w_gate, w_up, w_down, li, n_tiles):
    def row_map(i, te, na):
        return (jnp.minimum(i, na[0] - 1), 0)

    def w_map(i, te, na):
        return (li, te[jnp.minimum(i, na[0] - 1)], 0, 0)

    return pl.pallas_call(
        _moe_kernel,
        name="moe_experts",
        grid_spec=pltpu.PrefetchScalarGridSpec(
            num_scalar_prefetch=2,
            grid=(n_tiles,),
            in_specs=[
                pl.BlockSpec((TMX * SUBLANES, LANES), row_map),
                pl.BlockSpec((1, 1, D_MODEL, D_EXPERT), w_map),
                pl.BlockSpec((1, 1, D_MODEL, D_EXPERT), w_map),
                pl.BlockSpec((1, 1, D_EXPERT, D_MODEL), w_map),
            ],
            out_specs=pl.BlockSpec((TMX * SUBLANES, LANES), lambda i, te, na: (i, 0)),
            scratch_shapes=[
                pltpu.VMEM((D_MODEL, D_EXPERT), BF16),
                pltpu.VMEM((D_MODEL, D_EXPERT), BF16),
                pltpu.VMEM((D_EXPERT, D_MODEL), BF16),
            ],
        ),
        out_shape=jax.ShapeDtypeStruct((n_tiles * TMX * SUBLANES, LANES), F32),
        compiler_params=_cparams("arbitrary"),
    )(tile_expert, n_active, xs, w_gate, w_up, w_down)


def _combine_kernel(pos_ref, x1_ref, route_ref, mod_ref, nf_ref, os_ref, o_ref, buf0, buf1, sem, *, final_norm):
    i = pl.program_id(0)

    def body(t, carry):
        tok = i * TC + t
        pltpu.make_async_copy(_row_tile(os_ref, pos_ref[tok * 2]), _row_tile(buf0, t), sem.at[0]).start(priority=0)
        pltpu.make_async_copy(_row_tile(os_ref, pos_ref[tok * 2 + 1]), _row_tile(buf1, t), sem.at[1]).start(priority=1)
        return carry

    lax.fori_loop(0, TC, body, 0)
    nrow = TC * SUBLANES
    pltpu.make_async_copy(os_ref.at[pl.ds(0, nrow), :], buf0, sem.at[0]).wait()
    pltpu.make_async_copy(os_ref.at[pl.ds(0, nrow), :], buf1, sem.at[1]).wait()

    w1 = route_ref[:, 2:3]
    w2 = route_ref[:, 3:4]
    cols = []
    for j in range(D_MODEL // LANES):
        y = w1 * _from_token_major(buf0, TC, j) + w2 * _from_token_major(buf1, TC, j)
        g2 = mod_ref[:, 5 * D_MODEL + j * LANES:5 * D_MODEL + (j + 1) * LANES]
        cols.append(x1_ref[:, j * LANES:(j + 1) * LANES] + g2 * y)
    x2 = jnp.concatenate(cols, axis=-1)
    if final_norm:
        ms = jnp.mean(x2 * x2, axis=-1, keepdims=True)
        x2 = x2 * lax.rsqrt(ms + EPS) * nf_ref[...]
    o_ref[...] = x2


def _combine(pos_flat, x1, route, mod4, norm_final, osrt, li, n_tok, final_norm):
    def tmap(i, pos):
        return (i, 0)

    return pl.pallas_call(
        functools.partial(_combine_kernel, final_norm=final_norm),
        name="moe_combine",
        grid_spec=pltpu.PrefetchScalarGridSpec(
            num_scalar_prefetch=1,
            grid=(n_tok // TC,),
            in_specs=[
                pl.BlockSpec((TC, D_MODEL), tmap),
                pl.BlockSpec((TC, LANES), tmap),
                pl.BlockSpec((None, None, 1, 6 * D_MODEL), lambda i, pos: (li, _mod_row(i, TC), 0, 0)),
                pl.BlockSpec((1, D_MODEL), lambda i, pos: (0, 0)),
                pl.BlockSpec(memory_space=pl.ANY),
            ],
            out_specs=pl.BlockSpec((TC, D_MODEL), tmap),
            scratch_shapes=[
                pltpu.VMEM((TC * SUBLANES, LANES), F32),
                pltpu.VMEM((TC * SUBLANES, LANES), F32),
                pltpu.SemaphoreType.DMA((2,)),
            ],
        ),
        out_shape=jax.ShapeDtypeStruct((n_tok, D_MODEL), F32),
        compiler_params=_cparams("arbitrary"),
    )(pos_flat, x1, route, mod4, norm_final.reshape(1, D_MODEL), osrt)


def _moe_block(x1, hx_tm, route, mod4, norm_final, w_gate, w_up, w_down, li, n_tok, final_norm):
    n_tiles = (2 * n_tok) // TMX + N_EXPERTS
    n_rows = n_tiles * TMX
    pos, cnt, ends = _plan(route, n_tok)
    pos = pos[:, 0:2].reshape(-1)
    cnt = cnt[0, :N_EXPERTS].astype(jnp.int32)
    ends = ends[0, :N_EXPERTS].astype(jnp.int32)
    padded = ((cnt + TMX - 1) // TMX) * TMX
    pad_start = ends - padded + cnt
    tile_start = jnp.arange(n_tiles, dtype=jnp.int32) * TMX
    tile_expert = jnp.minimum(jnp.sum((tile_start[:, None] >= ends[None, :]).astype(jnp.int32), axis=1),
                              N_EXPERTS - 1)
    n_active = (ends[-1:] // TMX).astype(jnp.int32)
    tail = jnp.concatenate([ends[-1:], (n_rows - ends[-1:]) // PAD_CHUNKS[0]])

    xs = _dispatch(pos, pad_start, padded - cnt, tail, hx_tm, n_tok, n_rows)
    osrt = _moe(tile_expert, n_active, xs, w_gate, w_up, w_down, li, n_tiles)
    return _combine(pos, x1, route, mod4, norm_final, osrt, li, n_tok, final_norm)


def kernel(x, c, ctx, c_ctx, w_ada, b_ada, norm1, norm2, w_in, w_fourier, w_conv, rpb, w_out, w_rg, b_rg,
           w_re, b_re, w_gate, w_up, w_down, norm_final):
    x_pair, ctx_first = (x.reshape(T_LAT, D_MODEL), ctx.reshape(T_CTX, D_MODEL)), 0
    cc =jnp.concatenate([c, c_ctx[None, :], jnp.zeros((MOD_ROWS - BATCH - 1, D_MODEL), F32)], axis=0)
    mod4 = _modulation(cc, w_ada, b_ada).reshape(DEPTH, MOD_ROWS, 1, 6 * D_MODEL)

    w_in_bf = w_in.astype(BF16)
    w_out_bf = w_out.astype(BF16)
    pad = jnp.zeros((DEPTH, D_MODEL, LANES - ROUTE_COLS), F32)
    w_r = jnp.concatenate([w_rg, w_re, pad], axis=-1)
    w_r_hi = w_r.astype(BF16)
    w_r = jnp.concatenate([w_r_hi, (w_r - w_r_hi.astype(F32)).astype(BF16)], axis=-1)
    b_r = jnp.concatenate([b_rg, b_re, pad[:, 0, :]], axis=-1).reshape(DEPTH, 1, LANES)
    dft_ch = jnp.asarray(_channel_dft(), dtype=F32).astype(BF16)
    ctx_blk = T_LAT // CTX_LEN

    for li in range(DEPTH):
        last = li == DEPTH - 1
        yf, qkv, cv = _in_projection(*x_pair, ctx_first, mod4, norm1, w_in_bf, dft_ch, li)
        table = _bias_table(rpb[li].reshape(-1))
        ff = _fourier_mix(yf, w_fourier, li, SEQ, 0)
        cvo = _short_conv(cv, w_conv, li, SEQ, 0)
        at = _attn_latent(qkv, table)
        if last:
            n_tok = T_LAT
            ff_c, cvo_c, at_c = ff, cvo, at
        else:
            n_tok = T_ALL
            ff_c = _fourier_mix(yf, w_fourier, li, CTX_LEN, ctx_blk)
            cvo_c = _short_conv(cv, w_conv, li, CTX_LEN, ctx_blk)
            at_c = _attn_ctx(qkv)
        x1, hx_tm, route = _out_projection(x_pair, ctx_first, (ff, ff_c), (at, at_c), (cvo, cvo_c), mod4, norm2,
                                           w_out_bf, w_r, b_r, li, n_tok)
        xt = _moe_block(x1, hx_tm, route, mod4, norm_final, w_gate, w_up, w_down, li, n_tok, last)
        if last:
            return xt.reshape(BATCH, SEQ, D_MODEL)
        x_pair, ctx_first = (xt, xt), NT_LAT
```

```python
import functools
import math

import numpy as np
import jax
import jax.numpy as jnp
from jax import lax
from jax.experimental import pallas as pl
from jax.experimental.pallas import tpu as pltpu

F32 = jnp.float32
BF16 = jnp.bfloat16

D_MODEL = 1024
BATCH = 8
SEQ = 2048
DEPTH = 2
GRID_W = 64
ROWS = SEQ // GRID_W
CTX_LEN = 256
T_LAT = BATCH * SEQ
T_CTX = BATCH * CTX_LEN
T_ALL = T_LAT + T_CTX

D_FOURIER = 256
D_FG = 64
HEAD_DIM = 64
NA_HEADS = 8
D_NA = NA_HEADS * HEAD_DIM
D_CONV = 256
D_IN_PROJ = D_FOURIER + 3 * D_NA + 3 * D_CONV
NA_KH = 8
NA_KW = 16
N_GROUPS = 4
EXPERTS_PER_GROUP = 8
N_EXPERTS = 32
D_EXPERT = 512
EPS = 1e-6
NEG_INF = -1e30

LANES = 128
SUBLANES = 8
MOD_ROWS = 16
TM = 512
TMX = 512
TD = 512
TC = 256
VMEM_LIMIT = 56 * 1024 * 1024


def _cparams(*sem):
    return pltpu.CompilerParams(dimension_semantics=sem, vmem_limit_bytes=VMEM_LIMIT)


def _ada_kernel(cc_ref, w_ref, b_ref, o_ref):
    s = cc_ref[...]
    s = s * jax.nn.sigmoid(s)
    acc = jnp.dot(s.astype(BF16), w_ref[0].astype(BF16), preferred_element_type=F32)
    o_ref[0] = acc + b_ref[0]


def _modulation(cc, w_ada, b_ada):
    tn = 1536
    nj = 6 * D_MODEL // tn
    return pl.pallas_call(
        _ada_kernel,
        name="ada_modulation",
        grid=(DEPTH, nj),
        in_specs=[
            pl.BlockSpec((MOD_ROWS, D_MODEL), lambda l, j: (0, 0)),
            pl.BlockSpec((1, D_MODEL, tn), lambda l, j: (l, 0, j)),
            pl.BlockSpec((1, 1, tn), lambda l, j: (l, 0, j)),
        ],
        out_specs=pl.BlockSpec((1, MOD_ROWS, tn), lambda l, j: (l, 0, j)),
        out_shape=jax.ShapeDtypeStruct((DEPTH, MOD_ROWS, 6 * D_MODEL), F32),
        compiler_params=_cparams("arbitrary", "arbitrary"),
    )(cc, w_ada, b_ada.reshape(DEPTH, 1, 6 * D_MODEL))


def _mod_row(i, tile):
    return jnp.minimum((i * tile) // SEQ, BATCH)


NT_LAT = T_LAT // TM


def _lat_spec(cols):
    return pl.BlockSpec((TM, cols), lambda i: (jnp.minimum(i, NT_LAT - 1), 0))


def _ctx_spec(cols, first_block):
    return pl.BlockSpec((TM, cols), lambda i: (jnp.maximum(i - NT_LAT, 0) + first_block, 0))


def _pick_stream(lat_ref, ctx_ref):
    return jnp.where(pl.program_id(0) >= NT_LAT, ctx_ref[...], lat_ref[...])


def _inproj_kernel(xa_ref, xb_ref, mod_ref, n1_ref, w_ref, dft_ref, yf_ref, qkv_ref, cv_ref):
    x = _pick_stream(xa_ref, xb_ref)
    ms = jnp.mean(x * x, axis=-1, keepdims=True)
    y = x * lax.rsqrt(ms + EPS) * n1_ref[0]
    sh = mod_ref[:, 0:D_MODEL]
    sc = mod_ref[:, D_MODEL:2 * D_MODEL]
    h = y * (1.0 + sc) + sh
    u = jnp.dot(h.astype(BF16), w_ref[0], preferred_element_type=F32)
    yf = jnp.dot(u[:, 0:D_FOURIER].astype(BF16), dft_ref[...], preferred_element_type=F32)
    yf_ref[...] = yf.astype(BF16)
    q0 = D_FOURIER
    qkv_ref[:, 0:D_NA] = (u[:, q0:q0 + D_NA] * (1.0 / math.sqrt(HEAD_DIM))).astype(BF16)
    qkv_ref[:, D_NA:3 * D_NA] = u[:, q0 + D_NA:q0 + 3 * D_NA].astype(BF16)
    cv_ref[...] = u[:, q0 + 3 * D_NA:D_IN_PROJ]


def _in_projection(x_lat, x_ctx, ctx_first_block, mod4, norm1, w_in_bf, dft_ch, li):
    nt = T_ALL // TM
    return pl.pallas_call(
        _inproj_kernel,
        name="in_projection",
        grid=(nt,),
        in_specs=[
            _lat_spec(D_MODEL),
            _ctx_spec(D_MODEL, ctx_first_block),
            pl.BlockSpec((None, None, 1, 6 * D_MODEL), lambda i: (li, _mod_row(i, TM), 0, 0)),
            pl.BlockSpec((1, 1, D_MODEL), lambda i: (li, 0, 0)),
            pl.BlockSpec((1, D_MODEL, D_IN_PROJ), lambda i: (li, 0, 0)),
            pl.BlockSpec((D_FOURIER, 2 * D_FOURIER), lambda i: (0, 0)),
        ],
        out_specs=[
            pl.BlockSpec((TM, 2 * D_FOURIER), lambda i: (i, 0)),
            pl.BlockSpec((TM, 3 * D_NA), lambda i: (i, 0)),
            pl.BlockSpec((TM, 3 * D_CONV), lambda i: (i, 0)),
        ],
        out_shape=[
            jax.ShapeDtypeStruct((T_ALL, 2 * D_FOURIER), BF16),
            jax.ShapeDtypeStruct((T_ALL, 3 * D_NA), BF16),
            jax.ShapeDtypeStruct((T_ALL, 3 * D_CONV), F32),
        ],
        compiler_params=_cparams("arbitrary"),
    )(x_lat, x_ctx, mod4, norm1.reshape(DEPTH, 1, D_MODEL), w_in_bf, dft_ch)


def _dft_tables(n):
    k = np.arange(n, dtype=np.int64)
    ang = 2.0 * np.pi * ((k[:, None] * k[None, :]) % n).astype(np.float64) / n
    s = 1.0 / math.sqrt(n)
    return np.cos(ang) * s, np.sin(ang) * s


def _channel_dft():
    c, s = _dft_tables(D_FG)
    eye = np.eye(D_FOURIER // D_FG)
    return np.concatenate([np.kron(eye, c), np.kron(eye, s)], axis=1)


def _fourier_kernel(cn_ref, sn_ref, y_ref, wf_ref, o_ref):
    z = (jnp.dot(cn_ref[...], y_ref[:, 0:D_FOURIER], preferred_element_type=F32)
         - jnp.dot(sn_ref[...], y_ref[:, D_FOURIER:2 * D_FOURIER], preferred_element_type=F32))
    o_ref[...] = jnp.dot(z.astype(BF16), wf_ref[0].astype(BF16), preferred_element_type=F32).astype(BF16)


def _fourier_mix(yf, w_fourier, li, n, first_block):
    tk = min(n, 512)
    nk = n // tk
    cn, sn = _dft_tables(n)
    cn = jnp.asarray(cn, dtype=F32).astype(BF16)
    sn = jnp.asarray(sn, dtype=F32).astype(BF16)
    return pl.pallas_call(
        _fourier_kernel,
        name="fourier_mix",
        grid=(nk, BATCH),
        in_specs=[
            pl.BlockSpec((tk, n), lambda k, b: (k, 0)),
            pl.BlockSpec((tk, n), lambda k, b: (k, 0)),
            pl.BlockSpec((n, 2 * D_FOURIER), lambda k, b: (first_block + b, 0)),
            pl.BlockSpec((1, D_FOURIER, D_FOURIER), lambda k, b: (li, 0, 0)),
        ],
        out_specs=pl.BlockSpec((tk, D_FOURIER), lambda k, b: (b * nk + k, 0)),
        out_shape=jax.ShapeDtypeStruct((BATCH * n, D_FOURIER), BF16),
        compiler_params=_cparams("arbitrary", "arbitrary"),
    )(cn, sn, yf, w_fourier)


def _conv_kernel(cv_ref, wc_ref, o_ref, pad_ref):
    n = cv_ref.shape[0]
    gb = cv_ref[:, 0:D_CONV]
    g = cv_ref[:, D_CONV:2 * D_CONV] * cv_ref[:, 2 * D_CONV:3 * D_CONV]
    zero = jnp.zeros((SUBLANES, D_CONV), F32)
    pad_ref[0:SUBLANES, :] = zero
    pad_ref[n + SUBLANES:n + 2 * SUBLANES, :] = zero
    pad_ref[SUBLANES:n + SUBLANES, :] = g
    prev = pad_ref[SUBLANES - 1:n + SUBLANES - 1, :]
    nxt = pad_ref[SUBLANES + 1:n + SUBLANES + 1, :]
    w = wc_ref[0]
    o_ref[...] = (gb * (w[0:1, :] * prev + w[1:2, :] * g + w[2:3, :] * nxt)).astype(BF16)


def _short_conv(cv, w_conv, li, n, first_block):
    return pl.pallas_call(
        _conv_kernel,
        name="short_conv",
        grid=(BATCH,),
        in_specs=[
            pl.BlockSpec((n, 3 * D_CONV), lambda b: (first_block + b, 0)),
            pl.BlockSpec((1, 3, D_CONV), lambda b: (li, 0, 0)),
        ],
        out_specs=pl.BlockSpec((n, D_CONV), lambda b: (b, 0)),
        out_shape=jax.ShapeDtypeStruct((BATCH * n, D_CONV), BF16),
        scratch_shapes=[pltpu.VMEM((n + 2 * SUBLANES, D_CONV), F32)],
        compiler_params=_cparams("arbitrary"),
    )(cv, w_conv)


RPB_H = 2 * NA_KH - 1
RPB_W = 2 * NA_KW - 1
KEYS_LOC = NA_KH * GRID_W


def _bias_kernel(rpb_ref, o_ref):
    h = pl.program_id(0)
    qi = lax.broadcasted_iota(jnp.int32, (GRID_W, LANES), 0)
    lj = lax.broadcasted_iota(jnp.int32, (GRID_W, LANES), 1)
    kc = lj & (GRID_W - 1)
    hi = lj >= GRID_W
    d = kc - qi + (NA_KW - 1)
    cs = jnp.clip(qi - NA_KW // 2, 0, GRID_W - NA_KW)
    valid = (kc >= cs) & (kc < cs + NA_KW)
    tiles = []
    for a in range(RPB_H - 1):
        acc = jnp.zeros((GRID_W, LANES), F32)
        for b in range(RPB_W):
            va = rpb_ref[(h * RPB_H + a) * RPB_W + b]
            vb = rpb_ref[(h * RPB_H + a + 1) * RPB_W + b]
            acc = jnp.where(d == b, jnp.where(hi, vb, va), acc)
        tiles.append(jnp.where(valid, acc, NEG_INF))
    for cls in range(NA_KH):
        for m in range(NA_KH // 2):
            o_ref[0, cls, :, m * LANES:(m + 1) * LANES] = tiles[2 * m - cls + NA_KH - 1]


def _bias_table(rpb_flat):
    return pl.pallas_call(
        _bias_kernel,
        name="attn_bias_table",
        grid=(NA_HEADS,),
        in_specs=[pl.BlockSpec(memory_space=pltpu.SMEM)],
        out_specs=pl.BlockSpec((1, NA_KH, GRID_W, KEYS_LOC), lambda h: (h // 2, 0, h % 2, 0)),
        out_shape=jax.ShapeDtypeStruct((NA_HEADS // 2, NA_KH, 2 * GRID_W, KEYS_LOC), F32),
        compiler_params=_cparams("arbitrary"),
    )(rpb_flat)


_NT_DIMS = (((1,), (1,)), ((), ()))


def _softmax_pv(parts):
    m = None
    for s, _ in parts:
        ms = jnp.max(s, axis=1, keepdims=True)
        m = ms if m is None else jnp.maximum(m, ms)
    l = None
    o = None
    for s, v in parts:
        p = jnp.exp(s - m)
        ls = jnp.sum(p, axis=1, keepdims=True)
        os_ = jnp.dot(p.astype(BF16), v, preferred_element_type=F32)
        l = ls if l is None else l + ls
        o = os_ if o is None else o + os_
    return o / l


def _attn_latent_kernel(q_ref, k_ref, v_ref, kc_ref, vc_ref, tab_ref, o_ref, ve_ref, vce_ref):
    lane = lax.broadcasted_iota(jnp.int32, (GRID_W, LANES), 1)
    lo = lane < HEAD_DIM
    ve_ref[:, 0:LANES] = v_ref[...]
    ve_ref[:, LANES:2 * LANES] = jnp.ones((SEQ, LANES), BF16)
    vce_ref[:, 0:LANES] = vc_ref[...]
    vce_ref[:, LANES:2 * LANES] = jnp.ones((CTX_LEN, LANES), BF16)
    kc = kc_ref[...]
    vce = vce_ref[...]

    for r in range(ROWS):
        rs = min(max(r - NA_KH // 2, 0), ROWS - NA_KH)
        cls = r - rs
        q = q_ref[r * GRID_W:(r + 1) * GRID_W, :]
        kl = k_ref[rs * GRID_W:rs * GRID_W + KEYS_LOC, :]
        vl = ve_ref[rs * GRID_W:rs * GRID_W + KEYS_LOC, :]
        zero = jnp.zeros_like(q)
        q2 = jnp.concatenate([jnp.where(lo, q, zero), jnp.where(lo, zero, q)], axis=0)
        s1 = lax.dot_general(q2, kl, _NT_DIMS, preferred_element_type=F32) + tab_ref[0, cls]
        s2 = lax.dot_general(q2, kc, _NT_DIMS, preferred_element_type=F32)
        m = jnp.maximum(jnp.max(s1, axis=1, keepdims=True), jnp.max(s2, axis=1, keepdims=True))
        p1 = jnp.exp(s1 - m).astype(BF16)
        p2 = jnp.exp(s2 - m).astype(BF16)
        oe = (jnp.dot(p1, vl, preferred_element_type=F32) + jnp.dot(p2, vce, preferred_element_type=F32))
        o = oe[:, 0:LANES] / oe[:, LANES:2 * LANES]
        o_ref[r * GRID_W:(r + 1) * GRID_W, :] = jnp.where(lo, o[0:GRID_W], o[GRID_W:2 * GRID_W]).astype(BF16)


def _attn_latent(qkv, table):
    npair = NA_HEADS // 2
    cblk = T_LAT // CTX_LEN
    return pl.pallas_call(
        _attn_latent_kernel,
        name="attn_latent",
        grid=(BATCH, npair),
        in_specs=[
            pl.BlockSpec((SEQ, LANES), lambda b, p: (b, p)),
            pl.BlockSpec((SEQ, LANES), lambda b, p: (b, npair + p)),
            pl.BlockSpec((SEQ, LANES), lambda b, p: (b, 2 * npair + p)),
            pl.BlockSpec((CTX_LEN, LANES), lambda b, p: (cblk + b, npair + p)),
            pl.BlockSpec((CTX_LEN, LANES), lambda b, p: (cblk + b, 2 * npair + p)),
            pl.BlockSpec((1, NA_KH, 2 * GRID_W, KEYS_LOC), lambda b, p: (p, 0, 0, 0)),
        ],
        out_specs=pl.BlockSpec((SEQ, LANES), lambda b, p: (b, p)),
        out_shape=jax.ShapeDtypeStruct((T_LAT, D_NA), BF16),
        scratch_shapes=[pltpu.VMEM((SEQ, 2 * LANES), BF16), pltpu.VMEM((CTX_LEN, 2 * LANES), BF16)],
        compiler_params=_cparams("arbitrary", "arbitrary"),
    )(qkv, qkv, qkv, qkv, qkv, table)


def _attn_ctx_kernel(q_ref, k_ref, v_ref, o_ref):
    lane = lax.broadcasted_iota(jnp.int32, (CTX_LEN, LANES), 1)
    lo = lane < HEAD_DIM
    q = q_ref[...]
    k = k_ref[...]
    v = v_ref[...]
    outs = []
    for hh in range(2):
        qm = jnp.where(lo if hh == 0 else jnp.logical_not(lo), q, jnp.zeros_like(q))
        s = lax.dot_general(qm, k, _NT_DIMS, preferred_element_type=F32)
        outs.append(_softmax_pv([(s, v)]))
    o_ref[...] = jnp.where(lo, outs[0], outs[1]).astype(BF16)


def _attn_ctx(qkv):
    npair = NA_HEADS // 2
    cblk = T_LAT // CTX_LEN
    return pl.pallas_call(
        _attn_ctx_kernel,
        name="attn_context",
        grid=(BATCH, npair),
        in_specs=[
            pl.BlockSpec((CTX_LEN, LANES), lambda b, p: (cblk + b, p)),
            pl.BlockSpec((CTX_LEN, LANES), lambda b, p: (cblk + b, npair + p)),
            pl.BlockSpec((CTX_LEN, LANES), lambda b, p: (cblk + b, 2 * npair + p)),
        ],
        out_specs=pl.BlockSpec((CTX_LEN, LANES), lambda b, p: (b, p)),
        out_shape=jax.ShapeDtypeStruct((T_CTX, D_NA), BF16),
        compiler_params=_cparams("arbitrary", "arbitrary"),
    )(qkv, qkv, qkv)


ROUTE_COLS = N_GROUPS + N_EXPERTS


def _to_token_major(ref, val):
    rows = val.shape[0]
    for j in range(D_MODEL // LANES):
        ref[pl.ds(j, rows, stride=SUBLANES), :] = val[:, j * LANES:(j + 1) * LANES]


def _from_token_major(ref, rows, j):
    return ref[pl.ds(j, rows, stride=SUBLANES), :]


def _outproj_kernel(xa_ref, xb_ref, ffa_ref, ffb_ref, ata_ref, atb_ref, cva_ref, cvb_ref,
                    mod_ref, n2_ref, wo_ref, wr_ref, br_ref, x1_ref, hx_ref, route_ref):
    mix_in = jnp.concatenate([_pick_stream(ffa_ref, ffb_ref), _pick_stream(ata_ref, atb_ref),
                              _pick_stream(cva_ref, cvb_ref)], axis=-1)
    mix = jnp.dot(mix_in, wo_ref[0], preferred_element_type=F32)
    g1 = mod_ref[:, 2 * D_MODEL:3 * D_MODEL]
    x1 = _pick_stream(xa_ref, xb_ref) + g1 * mix
    x1_ref[...] = x1
    ms = jnp.mean(x1 * x1, axis=-1, keepdims=True)
    y = x1 * lax.rsqrt(ms + EPS) * n2_ref[0]
    sh2 = mod_ref[:, 3 * D_MODEL:4 * D_MODEL]
    sc2 = mod_ref[:, 4 * D_MODEL:5 * D_MODEL]
    hx = y * (1.0 + sc2) + sh2
    _to_token_major(hx_ref, hx)

    hx_hi = hx.astype(BF16)
    hx_lo = (hx - hx_hi.astype(F32)).astype(BF16)
    part = (jnp.dot(hx_hi, wr_ref[0], preferred_element_type=F32)
            + jnp.dot(hx_lo, wr_ref[0], preferred_element_type=F32))
    logits = part[:, 0:LANES] + part[:, LANES:2 * LANES] + br_ref[0]
    tm = logits.shape[0]
    lane = lax.broadcasted_iota(jnp.int32, (tm, LANES), 1)
    big = jnp.int32(LANES)
    gl = jnp.where(lane < N_GROUPS, logits, -jnp.inf)
    gmax = jnp.max(gl, axis=1, keepdims=True)
    gidx = jnp.min(jnp.where(gl == gmax, lane, big), axis=1, keepdims=True)
    g_w = 1.0 / jnp.sum(jnp.exp(gl - gmax), axis=1, keepdims=True)
    e_lane = lane - N_GROUPS
    in_group = (e_lane >= 0) & (e_lane < N_EXPERTS) & ((e_lane >> 3) == gidx)
    es = jnp.where(in_group, logits, -jnp.inf)
    t1 = jnp.max(es, axis=1, keepdims=True)
    i1 = jnp.min(jnp.where(es == t1, lane, big), axis=1, keepdims=True)
    es2 = jnp.where(lane == i1, -jnp.inf, es)
    t2 = jnp.max(es2, axis=1, keepdims=True)
    i2 = jnp.min(jnp.where(es2 == t2, lane, big), axis=1, keepdims=True)
    dlt = jnp.exp(t2 - t1)
    w1 = g_w / (1.0 + dlt)
    w2 = g_w * dlt / (1.0 + dlt)
    e1 = (i1 - N_GROUPS).astype(F32)
    e2 = (i2 - N_GROUPS).astype(F32)
    route_ref[...] = jnp.where(lane == 0, e1, jnp.where(lane == 1, e2,
                               jnp.where(lane == 2, w1, jnp.where(lane == 3, w2, 0.0))))


def _out_projection(x_pair, ctx_first_block, ff_pair, at_pair, cv_pair, mod4, norm2, w_out_bf, w_r, b_r, li, n_tok):
    nt = n_tok // TM
    return pl.pallas_call(
        _outproj_kernel,
        name="out_projection",
        grid=(nt,),
        in_specs=[
            _lat_spec(D_MODEL), _ctx_spec(D_MODEL, ctx_first_block),
            _lat_spec(D_FOURIER), _ctx_spec(D_FOURIER, 0),
            _lat_spec(D_NA), _ctx_spec(D_NA, 0),
            _lat_spec(D_CONV), _ctx_spec(D_CONV, 0),
            pl.BlockSpec((None, None, 1, 6 * D_MODEL), lambda i: (li, _mod_row(i, TM), 0, 0)),
            pl.BlockSpec((1, 1, D_MODEL), lambda i: (li, 0, 0)),
            pl.BlockSpec((1, D_MODEL, D_MODEL), lambda i: (li, 0, 0)),
            pl.BlockSpec((1, D_MODEL, 2 * LANES), lambda i: (li, 0, 0)),
            pl.BlockSpec((1, 1, LANES), lambda i: (li, 0, 0)),
        ],
        out_specs=[
            pl.BlockSpec((TM, D_MODEL), lambda i: (i, 0)),
            pl.BlockSpec((TM * SUBLANES, LANES), lambda i: (i, 0)),
            pl.BlockSpec((TM, LANES), lambda i: (i, 0)),
        ],
        out_shape=[
            jax.ShapeDtypeStruct((n_tok, D_MODEL), F32),
            jax.ShapeDtypeStruct((n_tok * SUBLANES, LANES), F32),
            jax.ShapeDtypeStruct((n_tok, LANES), F32),
        ],
        compiler_params=_cparams("arbitrary"),
    )(*x_pair, *ff_pair, *at_pair, *cv_pair, mod4, norm2.reshape(DEPTH, 1, D_MODEL), w_out_bf, w_r, b_r)


def _plan_kernel(route_ref, pos_ref, cnt_ref, ends_ref, carry_ref, offs_ref):
    ph = pl.program_id(0)
    i = pl.program_id(1)
    r = route_ref[...]
    tm = r.shape[0]
    lane = lax.broadcasted_iota(jnp.int32, (tm, LANES), 1)
    oh1 = lane == r[:, 0:1].astype(jnp.int32)
    oh2 = lane == r[:, 1:2].astype(jnp.int32)
    oh = jnp.where(oh1 | oh2, 1.0, 0.0)

    @pl.when((ph == 0) & (i == 0))
    def _():
        carry_ref[...] = jnp.zeros_like(carry_ref)

    @pl.when((ph == 1) & (i == 0))
    def _():
        cnt = carry_ref[...]
        cnt_ref[...] = jnp.broadcast_to(cnt, cnt_ref.shape)
        tiles = jnp.ceil(cnt * (1.0 / TMX))
        a = lax.broadcasted_iota(jnp.int32, (LANES, LANES), 0)
        b = lax.broadcasted_iota(jnp.int32, (LANES, LANES), 1)
        upper = jnp.where(a < b, 1.0, 0.0).astype(BF16)
        tiles8 = jnp.broadcast_to(tiles, (SUBLANES, LANES))
        first = jnp.dot(tiles8.astype(BF16), upper, preferred_element_type=F32)
        offs_ref[...] = first[0:1, :] * TMX
        ends_ref[...] = (first + tiles8) * TMX
        carry_ref[...] = jnp.zeros_like(carry_ref)

    @pl.when(ph == 1)
    def _():
        row = lax.broadcasted_iota(jnp.int32, (tm, tm), 0)
        col = lax.broadcasted_iota(jnp.int32, (tm, tm), 1)
        tri = jnp.where(row > col, 1.0, 0.0).astype(BF16)
        cum = jnp.dot(tri, oh.astype(BF16), preferred_element_type=F32) + (carry_ref[...] + offs_ref[...])
        p1 = jnp.sum(jnp.where(oh1, cum, 0.0), axis=1, keepdims=True)
        p2 = jnp.sum(jnp.where(oh2, cum, 0.0), axis=1, keepdims=True)
        pos_ref[...] = jnp.where(lane == 0, p1, jnp.where(lane == 1, p2, 0.0)).astype(jnp.int32)

    carry_ref[...] += jnp.sum(oh, axis=0, keepdims=True)


def _plan(route, n_tok):
    nt = n_tok // TM
    return pl.pallas_call(
        _plan_kernel,
        name="moe_plan",
        grid=(2, nt),
        in_specs=[pl.BlockSpec((TM, LANES), lambda ph, i: (i, 0))],
        out_specs=[
            pl.BlockSpec((TM, LANES), lambda ph, i: (i * ph, 0)),
            pl.BlockSpec((SUBLANES, LANES), lambda ph, i: (0, 0)),
            pl.BlockSpec((SUBLANES, LANES), lambda ph, i: (0, 0)),
        ],
        out_shape=[
            jax.ShapeDtypeStruct((n_tok, LANES), jnp.int32),
            jax.ShapeDtypeStruct((SUBLANES, LANES), F32),
            jax.ShapeDtypeStruct((SUBLANES, LANES), F32),
        ],
        scratch_shapes=[pltpu.VMEM((1, LANES), F32), pltpu.VMEM((1, LANES), F32)],
        compiler_params=_cparams("arbitrary", "arbitrary"),
    )(route)


def _row_tile(ref, row):
    return ref.at[pl.ds(pl.multiple_of(row * SUBLANES, SUBLANES), SUBLANES), :]


PAD_CHUNKS = tuple(1 << b for b in reversed(range(TMX.bit_length() - 1)))


def _dispatch_kernel(pos_ref, pstart_ref, plen_ref, tail_ref, hx_ref, xs_ref, zbuf, sem, zsem):
    i = pl.program_id(0)

    def pad_copies(fn):
        def ebody(e, carry):
            n = plen_ref[e]
            off = pstart_ref[e]
            for rows in PAD_CHUNKS:
                @pl.when((n & rows) != 0)
                def _():
                    fn(pltpu.make_async_copy(
                        zbuf.at[pl.ds(0, rows * SUBLANES), :],
                        xs_ref.at[pl.ds(pl.multiple_of(off * SUBLANES, SUBLANES), rows * SUBLANES), :], zsem.at[0]))
                off = off + (n & rows)
            return carry

        lax.fori_loop(0, N_EXPERTS, ebody, 0)

        def tbody(k, carry):
            row = pl.multiple_of((tail_ref[0] + k * PAD_CHUNKS[0]) * SUBLANES, SUBLANES)
            fn(pltpu.make_async_copy(zbuf, xs_ref.at[pl.ds(row, PAD_CHUNKS[0] * SUBLANES), :], zsem.at[0]))
            return carry

        lax.fori_loop(0, tail_ref[1], tbody, 0)

    @pl.when(i == 0)
    def _():
        zbuf[...] = jnp.zeros_like(zbuf)
        pad_copies(lambda cp: cp.start())

    def body(t, carry):
        src = _row_tile(hx_ref, t)
        for j in range(2):
            pltpu.make_async_copy(src, _row_tile(xs_ref, pos_ref[(i * TD + t) * 2 + j]), sem.at[0]).start(priority=j)
        return carry

    lax.fori_loop(0, TD, body, 0)
    nrow = TD * SUBLANES
    for j in range(2):
        pltpu.make_async_copy(hx_ref, xs_ref.at[pl.ds(0, nrow), :], sem.at[0]).wait()

    @pl.when(i == 0)
    def _():
        pad_copies(lambda cp: cp.wait())


def _dispatch(pos_flat, pad_start, pad_len, tail, hx_tm, n_tok, n_rows):
    return pl.pallas_call(
        _dispatch_kernel,
        name="moe_dispatch",
        grid_spec=pltpu.PrefetchScalarGridSpec(
            num_scalar_prefetch=4,
            grid=(n_tok // TD,),
            in_specs=[pl.BlockSpec((TD * SUBLANES, LANES), lambda i, pos, ps, pn, tl: (i, 0))],
            out_specs=pl.BlockSpec(memory_space=pl.ANY),
            scratch_shapes=[pltpu.VMEM((PAD_CHUNKS[0] * SUBLANES, LANES), F32),
                            pltpu.SemaphoreType.DMA((1,)), pltpu.SemaphoreType.DMA((1,))],
        ),
        out_shape=jax.ShapeDtypeStruct((n_rows * SUBLANES, LANES), F32),
        compiler_params=_cparams("arbitrary"),
    )(pos_flat, pad_start, pad_len, tail, hx_tm)


def _moe_kernel(te_ref, na_ref, xs_ref, wg_ref, wu_ref, wd_ref, os_ref, wg_s, wu_s, wd_s):
    i = pl.program_id(0)
    prev = te_ref[jnp.maximum(i - 1, 0)]
    active = i < na_ref[0]

    @pl.when(active & ((i == 0) | (te_ref[i] != prev)))
    def _():
        wg_s[...] = wg_ref[0, 0].astype(BF16)
        wu_s[...] = wu_ref[0, 0].astype(BF16)
        wd_s[...] = wd_ref[0, 0].astype(BF16)

    @pl.when(active)
    def _():
        x = jnp.concatenate([_from_token_major(xs_ref, TMX, j).astype(BF16)
                             for j in range(D_MODEL // LANES)], axis=-1)
        g = jnp.dot(x, wg_s[...], preferred_element_type=F32)
        u = jnp.dot(x, wu_s[...], preferred_element_type=F32)
        h = (g * jax.nn.sigmoid(g) * u).astype(BF16)
        _to_token_major(os_ref, jnp.dot(h, wd_s[...], preferred_element_type=F32))

    @pl.when(jnp.logical_not(active))
    def _():
        os_ref[...] = jnp.zeros_like(os_ref)


def _moe(tile_expert, n_active, xs, w_gate, w_up, w_down, li, n_tiles):
    def row_map(i, te, na):
        return (jnp.minimum(i, na[0] - 1), 0)

    def w_map(i, te, na):
        return (li, te[jnp.minimum(i, na[0] - 1)], 0, 0)

    return pl.pallas_call(
        _moe_kernel,
        name="moe_experts",
        grid_spec=pltpu.PrefetchScalarGridSpec(
            num_scalar_prefetch=2,
            grid=(n_tiles,),
            in_specs=[
                pl.BlockSpec((TMX * SUBLANES, LANES), row_map),
                pl.BlockSpec((1, 1, D_MODEL, D_EXPERT), w_map),
                pl.BlockSpec((1, 1, D_MODEL, D_EXPERT), w_map),
                pl.BlockSpec((1, 1, D_EXPERT, D_MODEL), w_map),
            ],
            out_specs=pl.BlockSpec((TMX * SUBLANES, LANES), lambda i, te, na: (i, 0)),
            scratch_shapes=[
                pltpu.VMEM((D_MODEL, D_EXPERT), BF16),
                pltpu.VMEM((D_MODEL, D_EXPERT), BF16),
                pltpu.VMEM((D_EXPERT, D_MODEL), BF16),
            ],
        ),
        out_shape=jax.ShapeDtypeStruct((n_tiles * TMX * SUBLANES, LANES), F32),
        compiler_params=_cparams("arbitrary"),
    )(tile_expert, n_active, xs, w_gate, w_up, w_down)


def _combine_kernel(pos_ref, x1_ref, route_ref, mod_ref, nf_ref, os_ref, o_ref, buf0, buf1, sem, *, final_norm):
    i = pl.program_id(0)

    def body(t, carry):
        tok = i * TC + t
        pltpu.make_async_copy(_row_tile(os_ref, pos_ref[tok * 2]), _row_tile(buf0, t), sem.at[0]).start(priority=0)
        pltpu.make_async_copy(_row_tile(os_ref, pos_ref[tok * 2 + 1]), _row_tile(buf1, t), sem.at[1]).start(priority=1)
        return carry

    lax.fori_loop(0, TC, body, 0)
    nrow = TC * SUBLANES
    pltpu.make_async_copy(os_ref.at[pl.ds(0, nrow), :], buf0, sem.at[0]).wait()
    pltpu.make_async_copy(os_ref.at[pl.ds(0, nrow), :], buf1, sem.at[1]).wait()

    w1 = route_ref[:, 2:3]
    w2 = route_ref[:, 3:4]
    cols = []
    for j in range(D_MODEL // LANES):
        y = w1 * _from_token_major(buf0, TC, j) + w2 * _from_token_major(buf1, TC, j)
        g2 = mod_ref[:, 5 * D_MODEL + j * LANES:5 * D_MODEL + (j + 1) * LANES]
        cols.append(x1_ref[:, j * LANES:(j + 1) * LANES] + g2 * y)
    x2 = jnp.concatenate(cols, axis=-1)
    if final_norm:
        ms = jnp.mean(x2 * x2, axis=-1, keepdims=True)
        x2 = x2 * lax.rsqrt(ms + EPS) * nf_ref[...]
    o_ref[...] = x2


def _combine(pos_flat, x1, route, mod4, norm_final, osrt, li, n_tok, final_norm):
    def tmap(i, pos):
        return (i, 0)

    return pl.pallas_call(
        functools.partial(_combine_kernel, final_norm=final_norm),
        name="moe_combine",
        grid_spec=pltpu.PrefetchScalarGridSpec(
            num_scalar_prefetch=1,
            grid=(n_tok // TC,),
            in_specs=[
                pl.BlockSpec((TC, D_MODEL), tmap),
                pl.BlockSpec((TC, LANES), tmap),
                pl.BlockSpec((None, None, 1, 6 * D_MODEL), lambda i, pos: (li, _mod_row(i, TC), 0, 0)),
                pl.BlockSpec((1, D_MODEL), lambda i, pos: (0, 0)),
                pl.BlockSpec(memory_space=pl.ANY),
            ],
            out_specs=pl.BlockSpec((TC, D_MODEL), tmap),
            scratch_shapes=[
                pltpu.VMEM((TC * SUBLANES, LANES), F32),
                pltpu.VMEM((TC * SUBLANES, LANES), F32),
                pltpu.SemaphoreType.DMA((2,)),
            ],
        ),
        out_shape=jax.ShapeDtypeStruct((n_tok, D_MODEL), F32),
        compiler_params=_cparams("arbitrary"),
    )(pos_flat, x1, route, mod4, norm_final.reshape(1, D_MODEL), osrt)


def _moe_block(x1, hx_tm, route, mod4, norm_final, w_gate, w_up, w_down, li, n_tok, final_norm):
    n_tiles = (2 * n_tok) // TMX + N_EXPERTS
    n_rows = n_tiles * TMX
    pos, cnt, ends = _plan(route, n_tok)
    pos = pos[:, 0:2].reshape(-1)
    cnt = cnt[0, :N_EXPERTS].astype(jnp.int32)
    ends = ends[0, :N_EXPERTS].astype(jnp.int32)
    padded = ((cnt + TMX - 1) // TMX) * TMX
    pad_start = ends - padded + cnt
    tile_start = jnp.arange(n_tiles, dtype=jnp.int32) * TMX
    tile_expert = jnp.minimum(jnp.sum((tile_start[:, None] >= ends[None, :]).astype(jnp.int32), axis=1),
                              N_EXPERTS - 1)
    n_active = (ends[-1:] // TMX).astype(jnp.int32)
    tail = jnp.concatenate([ends[-1:], (n_rows - ends[-1:]) // PAD_CHUNKS[0]])

    xs = _dispatch(pos, pad_start, padded - cnt, tail, hx_tm, n_tok, n_rows)
    osrt = _moe(tile_expert, n_active, xs, w_gate, w_up, w_down, li, n_tiles)
    return _combine(pos, x1, route, mod4, norm_final, osrt, li, n_tok, final_norm)


def kernel(x, c, ctx, c_ctx, w_ada, b_ada, norm1, norm2, w_in, w_fourier, w_conv, rpb, w_out, w_rg, b_rg,
           w_re, b_re, w_gate, w_up, w_down, norm_final):
    x_pair, ctx_first = (x.reshape(T_LAT, D_MODEL), ctx.reshape(T_CTX, D_MODEL)), 0
    cc =jnp.concatenate([c, c_ctx[None, :], jnp.zeros((MOD_ROWS - BATCH - 1, D_MODEL), F32)], axis=0)
    mod4 = _modulation(cc, w_ada, b_ada).reshape(DEPTH, MOD_ROWS, 1, 6 * D_MODEL)

    w_in_bf = w_in.astype(BF16)
    w_out_bf = w_out.astype(BF16)
    pad = jnp.zeros((DEPTH, D_MODEL, LANES - ROUTE_COLS), F32)
    w_r = jnp.concatenate([w_rg, w_re, pad], axis=-1)
    w_r_hi = w_r.astype(BF16)
    w_r = jnp.concatenate([w_r_hi, (w_r - w_r_hi.astype(F32)).astype(BF16)], axis=-1)
    b_r = jnp.concatenate([b_rg, b_re, pad[:, 0, :]], axis=-1).reshape(DEPTH, 1, LANES)
    dft_ch = jnp.asarray(_channel_dft(), dtype=F32).astype(BF16)
    ctx_blk = T_LAT // CTX_LEN

    for li in range(DEPTH):
        last = li == DEPTH - 1
        yf, qkv, cv = _in_projection(*x_pair, ctx_first, mod4, norm1, w_in_bf, dft_ch, li)
        table = _bias_table(rpb[li].reshape(-1))
        ff = _fourier_mix(yf, w_fourier, li, SEQ, 0)
        cvo = _short_conv(cv, w_conv, li, SEQ, 0)
        at = _attn_latent(qkv, table)
        if last:
            n_tok = T_LAT
            ff_c, cvo_c, at_c = ff, cvo, at
        else:
            n_tok = T_ALL
            ff_c = _fourier_mix(yf, w_fourier, li, CTX_LEN, ctx_blk)
            cvo_c = _short_conv(cv, w_conv, li, CTX_LEN, ctx_blk)
            at_c = _attn_ctx(qkv)
        x1, hx_tm, route = _out_projection(x_pair, ctx_first, (ff, ff_c), (at, at_c), (cvo, cvo_c), mod4, norm2,
                                           w_out_bf, w_r, b_r, li, n_tok)
        xt = _moe_block(x1, hx_tm, route, mod4, norm_final, w_gate, w_up, w_down, li, n_tok, last)
        if last:
            return xt.reshape(BATCH, SEQ, D_MODEL)
        x_pair, ctx_first = (xt, xt), NT_LAT
```

```python
import functools
import math

import numpy as np
import jax
import jax.numpy as jnp
from jax import lax
from jax.experimental import pallas as pl
from jax.experimental.pallas import tpu as pltpu

F32 = jnp.float32
BF16 = jnp.bfloat16

D_MODEL = 1024
BATCH = 8
SEQ = 2048
DEPTH = 2
GRID_W = 64
ROWS = SEQ // GRID_W
CTX_LEN = 256
T_LAT = BATCH * SEQ
T_CTX = BATCH * CTX_LEN
T_ALL = T_LAT + T_CTX

D_FOURIER = 256
D_FG = 64
HEAD_DIM = 64
NA_HEADS = 8
D_NA = NA_HEADS * HEAD_DIM
D_CONV = 256
D_IN_PROJ = D_FOURIER + 3 * D_NA + 3 * D_CONV
NA_KH = 8
NA_KW = 16
N_GROUPS = 4
EXPERTS_PER_GROUP = 8
N_EXPERTS = 32
D_EXPERT = 512
EPS = 1e-6
NEG_INF = -1e30

LANES = 128
SUBLANES = 8
MOD_ROWS = 16
TM = 512
TMX = 512
TD = 512
TC = 256
VMEM_LIMIT = 56 * 1024 * 1024


def _cparams(*sem):
    return pltpu.CompilerParams(dimension_semantics=sem, vmem_limit_bytes=VMEM_LIMIT)


def _ada_kernel(cc_ref, w_ref, b_ref, o_ref):
    s = cc_ref[...]
    s = s * jax.nn.sigmoid(s)
    acc = jnp.dot(s.astype(BF16), w_ref[0].astype(BF16), preferred_element_type=F32)
    o_ref[0] = acc + b_ref[0]


def _modulation(cc, w_ada, b_ada):
    tn = 1536
    nj = 6 * D_MODEL // tn
    return pl.pallas_call(
        _ada_kernel,
        name="ada_modulation",
        grid=(DEPTH, nj),
        in_specs=[
            pl.BlockSpec((MOD_ROWS, D_MODEL), lambda l, j: (0, 0)),
            pl.BlockSpec((1, D_MODEL, tn), lambda l, j: (l, 0, j)),
            pl.BlockSpec((1, 1, tn), lambda l, j: (l, 0, j)),
        ],
        out_specs=pl.BlockSpec((1, MOD_ROWS, tn), lambda l, j: (l, 0, j)),
        out_shape=jax.ShapeDtypeStruct((DEPTH, MOD_ROWS, 6 * D_MODEL), F32),
        compiler_params=_cparams("arbitrary", "arbitrary"),
    )(cc, w_ada, b_ada.reshape(DEPTH, 1, 6 * D_MODEL))


def _mod_row(i, tile):
    return jnp.minimum((i * tile) // SEQ, BATCH)


NT_LAT = T_LAT // TM


def _lat_spec(cols):
    return pl.BlockSpec((TM, cols), lambda i: (jnp.minimum(i, NT_LAT - 1), 0))


def _ctx_spec(cols, first_block):
    return pl.BlockSpec((TM, cols), lambda i: (jnp.maximum(i - NT_LAT, 0) + first_block, 0))


def _pick_stream(lat_ref, ctx_ref):
    return jnp.where(pl.program_id(0) >= NT_LAT, ctx_ref[...], lat_ref[...])


def _inproj_kernel(xa_ref, xb_ref, mod_ref, n1_ref, w_ref, dft_ref, yf_ref, qkv_ref, cv_ref):
    _inproj_math(_pick_stream(xa_ref, xb_ref), mod_ref, n1_ref, w_ref, dft_ref, yf_ref, qkv_ref, cv_ref)


def _inproj_math(x, mod_ref, n1_ref, w_ref, dft_ref, yf_ref, qkv_ref, cv_ref):
    ms = jnp.mean(x * x, axis=-1, keepdims=True)
    y = x * lax.rsqrt(ms + EPS) * n1_ref[0]
    sh = mod_ref[:, 0:D_MODEL]
    sc = mod_ref[:, D_MODEL:2 * D_MODEL]
    h = y * (1.0 + sc) + sh
    u = jnp.dot(h.astype(BF16), w_ref[0], preferred_element_type=F32)
    yf = jnp.dot(u[:, 0:D_FOURIER].astype(BF16), dft_ref[...], preferred_element_type=F32)
    yf_ref[...] = yf.astype(BF16)
    q0 = D_FOURIER
    qkv_ref[:, 0:D_NA] = (u[:, q0:q0 + D_NA] * (1.0 / math.sqrt(HEAD_DIM))).astype(BF16)
    qkv_ref[:, D_NA:3 * D_NA] = u[:, q0 + D_NA:q0 + 3 * D_NA].astype(BF16)
    cv_ref[...] = u[:, q0 + 3 * D_NA:D_IN_PROJ]


def _in_projection(x_lat, x_ctx, ctx_first_block, mod4, norm1, w_in_bf, dft_ch, li):
    nt = T_ALL // TM
    return pl.pallas_call(
        _inproj_kernel,
        name="in_projection",
        grid=(nt,),
        in_specs=[
            _lat_spec(D_MODEL),
            _ctx_spec(D_MODEL, ctx_first_block),
            pl.BlockSpec((None, None, 1, 6 * D_MODEL), lambda i: (li, _mod_row(i, TM), 0, 0)),
            pl.BlockSpec((1, 1, D_MODEL), lambda i: (li, 0, 0)),
            pl.BlockSpec((1, D_MODEL, D_IN_PROJ), lambda i: (li, 0, 0)),
            pl.BlockSpec((D_FOURIER, 2 * D_FOURIER), lambda i: (0, 0)),
        ],
        out_specs=[
            pl.BlockSpec((TM, 2 * D_FOURIER), lambda i: (i, 0)),
            pl.BlockSpec((TM, 3 * D_NA), lambda i: (i, 0)),
            pl.BlockSpec((TM, 3 * D_CONV), lambda i: (i, 0)),
        ],
        out_shape=[
            jax.ShapeDtypeStruct((T_ALL, 2 * D_FOURIER), BF16),
            jax.ShapeDtypeStruct((T_ALL, 3 * D_NA), BF16),
            jax.ShapeDtypeStruct((T_ALL, 3 * D_CONV), F32),
        ],
        compiler_params=_cparams("arbitrary"),
    )(x_lat, x_ctx, mod4, norm1.reshape(DEPTH, 1, D_MODEL), w_in_bf, dft_ch)


def _dft_tables(n):
    k = np.arange(n, dtype=np.int64)
    ang = 2.0 * np.pi * ((k[:, None] * k[None, :]) % n).astype(np.float64) / n
    s = 1.0 / math.sqrt(n)
    return np.cos(ang) * s, np.sin(ang) * s


def _channel_dft():
    c, s = _dft_tables(D_FG)
    eye = np.eye(D_FOURIER // D_FG)
    return np.concatenate([np.kron(eye, c), np.kron(eye, s)], axis=1)


def _fourier_kernel(cn_ref, sn_ref, y_ref, wf_ref, o_ref):
    z = (jnp.dot(cn_ref[...], y_ref[:, 0:D_FOURIER], preferred_element_type=F32)
         - jnp.dot(sn_ref[...], y_ref[:, D_FOURIER:2 * D_FOURIER], preferred_element_type=F32))
    o_ref[...] = jnp.dot(z.astype(BF16), wf_ref[0].astype(BF16), preferred_element_type=F32).astype(BF16)


def _fourier_mix(yf, w_fourier, li, n, first_block):
    tk = min(n, 512)
    nk = n // tk
    cn, sn = _dft_tables(n)
    cn = jnp.asarray(cn, dtype=F32).astype(BF16)
    sn = jnp.asarray(sn, dtype=F32).astype(BF16)
    return pl.pallas_call(
        _fourier_kernel,
        name="fourier_mix",
        grid=(nk, BATCH),
        in_specs=[
            pl.BlockSpec((tk, n), lambda k, b: (k, 0)),
            pl.BlockSpec((tk, n), lambda k, b: (k, 0)),
            pl.BlockSpec((n, 2 * D_FOURIER), lambda k, b: (first_block + b, 0)),
            pl.BlockSpec((1, D_FOURIER, D_FOURIER), lambda k, b: (li, 0, 0)),
        ],
        out_specs=pl.BlockSpec((tk, D_FOURIER), lambda k, b: (b * nk + k, 0)),
        out_shape=jax.ShapeDtypeStruct((BATCH * n, D_FOURIER), BF16),
        compiler_params=_cparams("arbitrary", "arbitrary"),
    )(cn, sn, yf, w_fourier)


def _conv_kernel(cv_ref, wc_ref, o_ref, pad_ref):
    n = cv_ref.shape[0]
    gb = cv_ref[:, 0:D_CONV]
    g = cv_ref[:, D_CONV:2 * D_CONV] * cv_ref[:, 2 * D_CONV:3 * D_CONV]
    zero = jnp.zeros((SUBLANES, D_CONV), F32)
    pad_ref[0:SUBLANES, :] = zero
    pad_ref[n + SUBLANES:n + 2 * SUBLANES, :] = zero
    pad_ref[SUBLANES:n + SUBLANES, :] = g
    prev = pad_ref[SUBLANES - 1:n + SUBLANES - 1, :]
    nxt = pad_ref[SUBLANES + 1:n + SUBLANES + 1, :]
    w = wc_ref[0]
    o_ref[...] = (gb * (w[0:1, :] * prev + w[1:2, :] * g + w[2:3, :] * nxt)).astype(BF16)


def _short_conv(cv, w_conv, li, n, first_block):
    return pl.pallas_call(
        _conv_kernel,
        name="short_conv",
        grid=(BATCH,),
        in_specs=[
            pl.BlockSpec((n, 3 * D_CONV), lambda b: (first_block + b, 0)),
            pl.BlockSpec((1, 3, D_CONV), lambda b: (li, 0, 0)),
        ],
        out_specs=pl.BlockSpec((n, D_CONV), lambda b: (b, 0)),
        out_shape=jax.ShapeDtypeStruct((BATCH * n, D_CONV), BF16),
        scratch_shapes=[pltpu.VMEM((n + 2 * SUBLANES, D_CONV), F32)],
        compiler_params=_cparams("arbitrary"),
    )(cv, w_conv)


RPB_H = 2 * NA_KH - 1
RPB_W = 2 * NA_KW - 1
KEYS_LOC = NA_KH * GRID_W


def _bias_kernel(rpb_ref, o_ref):
    h = pl.program_id(0)
    qi = lax.broadcasted_iota(jnp.int32, (GRID_W, LANES), 0)
    lj = lax.broadcasted_iota(jnp.int32, (GRID_W, LANES), 1)
    kc = lj & (GRID_W - 1)
    hi = lj >= GRID_W
    d = kc - qi + (NA_KW - 1)
    cs = jnp.clip(qi - NA_KW // 2, 0, GRID_W - NA_KW)
    valid = (kc >= cs) & (kc < cs + NA_KW)
    tiles = []
    for a in range(RPB_H - 1):
        acc = jnp.zeros((GRID_W, LANES), F32)
        for b in range(RPB_W):
            va = rpb_ref[(h * RPB_H + a) * RPB_W + b]
            vb = rpb_ref[(h * RPB_H + a + 1) * RPB_W + b]
            acc = jnp.where(d == b, jnp.where(hi, vb, va), acc)
        tiles.append(jnp.where(valid, acc, NEG_INF))
    for cls in range(NA_KH):
        for m in range(NA_KH // 2):
            o_ref[0, cls, :, m * LANES:(m + 1) * LANES] = tiles[2 * m - cls + NA_KH - 1]


def _bias_table(rpb_flat):
    return pl.pallas_call(
        _bias_kernel,
        name="attn_bias_table",
        grid=(NA_HEADS,),
        in_specs=[pl.BlockSpec(memory_space=pltpu.SMEM)],
        out_specs=pl.BlockSpec((1, NA_KH, GRID_W, KEYS_LOC), lambda h: (h // 2, 0, h % 2, 0)),
        out_shape=jax.ShapeDtypeStruct((NA_HEADS // 2, NA_KH, 2 * GRID_W, KEYS_LOC), F32),
        compiler_params=_cparams("arbitrary"),
    )(rpb_flat)


_NT_DIMS = (((1,), (1,)), ((), ()))


def _softmax_pv(parts):
    m = None
    for s, _ in parts:
        ms = jnp.max(s, axis=1, keepdims=True)
        m = ms if m is None else jnp.maximum(m, ms)
    l = None
    o = None
    for s, v in parts:
        p = jnp.exp(s - m)
        ls = jnp.sum(p, axis=1, keepdims=True)
        os_ = jnp.dot(p.astype(BF16), v, preferred_element_type=F32)
        l = ls if l is None else l + ls
        o = os_ if o is None else o + os_
    return o / l


def _attn_latent_kernel(q_ref, k_ref, v_ref, kc_ref, vc_ref, tab_ref, o_ref, ve_ref, vce_ref):
    lane = lax.broadcasted_iota(jnp.int32, (GRID_W, LANES), 1)
    lo = lane < HEAD_DIM
    ve_ref[:, 0:LANES] = v_ref[...]
    ve_ref[:, LANES:2 * LANES] = jnp.ones((SEQ, LANES), BF16)
    vce_ref[:, 0:LANES] = vc_ref[...]
    vce_ref[:, LANES:2 * LANES] = jnp.ones((CTX_LEN, LANES), BF16)
    kc = kc_ref[...]
    vce = vce_ref[...]

    for r in range(ROWS):
        rs = min(max(r - NA_KH // 2, 0), ROWS - NA_KH)
        cls = r - rs
        q = q_ref[r * GRID_W:(r + 1) * GRID_W, :]
        kl = k_ref[rs * GRID_W:rs * GRID_W + KEYS_LOC, :]
        vl = ve_ref[rs * GRID_W:rs * GRID_W + KEYS_LOC, :]
        zero = jnp.zeros_like(q)
        q2 = jnp.concatenate([jnp.where(lo, q, zero), jnp.where(lo, zero, q)], axis=0)
        s1 = lax.dot_general(q2, kl, _NT_DIMS, preferred_element_type=F32) + tab_ref[0, cls]
        s2 = lax.dot_general(q2, kc, _NT_DIMS, preferred_element_type=F32)
        m = jnp.maximum(jnp.max(s1, axis=1, keepdims=True), jnp.max(s2, axis=1, keepdims=True))
        p1 = jnp.exp(s1 - m).astype(BF16)
        p2 = jnp.exp(s2 - m).astype(BF16)
        oe = (jnp.dot(p1, vl, preferred_element_type=F32) + jnp.dot(p2, vce, preferred_element_type=F32))
        o = oe[:, 0:LANES] / oe[:, LANES:2 * LANES]
        o_ref[r * GRID_W:(r + 1) * GRID_W, :] = jnp.where(lo, o[0:GRID_W], o[GRID_W:2 * GRID_W]).astype(BF16)


def _attn_latent(qkv, table):
    npair = NA_HEADS // 2
    cblk = T_LAT // CTX_LEN
    return pl.pallas_call(
        _attn_latent_kernel,
        name="attn_latent",
        grid=(BATCH, npair),
        in_specs=[
            pl.BlockSpec((SEQ, LANES), lambda b, p: (b, p)),
            pl.BlockSpec((SEQ, LANES), lambda b, p: (b, npair + p)),
            pl.BlockSpec((SEQ, LANES), lambda b, p: (b, 2 * npair + p)),
            pl.BlockSpec((CTX_LEN, LANES), lambda b, p: (cblk + b, npair + p)),
            pl.BlockSpec((CTX_LEN, LANES), lambda b, p: (cblk + b, 2 * npair + p)),
            pl.BlockSpec((1, NA_KH, 2 * GRID_W, KEYS_LOC), lambda b, p: (p, 0, 0, 0)),
        ],
        out_specs=pl.BlockSpec((SEQ, LANES), lambda b, p: (b, p)),
        out_shape=jax.ShapeDtypeStruct((T_LAT, D_NA), BF16),
        scratch_shapes=[pltpu.VMEM((SEQ, 2 * LANES), BF16), pltpu.VMEM((CTX_LEN, 2 * LANES), BF16)],
        compiler_params=_cparams("arbitrary", "arbitrary"),
    )(qkv, qkv, qkv, qkv, qkv, table)


def _attn_ctx_kernel(q_ref, k_ref, v_ref, o_ref):
    lane = lax.broadcasted_iota(jnp.int32, (CTX_LEN, LANES), 1)
    lo = lane < HEAD_DIM
    q = q_ref[...]
    k = k_ref[...]
    v = v_ref[...]
    outs = []
    for hh in range(2):
        qm = jnp.where(lo if hh == 0 else jnp.logical_not(lo), q, jnp.zeros_like(q))
        s = lax.dot_general(qm, k, _NT_DIMS, preferred_element_type=F32)
        outs.append(_softmax_pv([(s, v)]))
    o_ref[...] = jnp.where(lo, outs[0], outs[1]).astype(BF16)


def _attn_ctx(qkv):
    npair = NA_HEADS // 2
    cblk = T_LAT // CTX_LEN
    return pl.pallas_call(
        _attn_ctx_kernel,
        name="attn_context",
        grid=(BATCH, npair),
        in_specs=[
            pl.BlockSpec((CTX_LEN, LANES), lambda b, p: (cblk + b, p)),
            pl.BlockSpec((CTX_LEN, LANES), lambda b, p: (cblk + b, npair + p)),
            pl.BlockSpec((CTX_LEN, LANES), lambda b, p: (cblk + b, 2 * npair + p)),
        ],
        out_specs=pl.BlockSpec((CTX_LEN, LANES), lambda b, p: (b, p)),
        out_shape=jax.ShapeDtypeStruct((T_CTX, D_NA), BF16),
        compiler_params=_cparams("arbitrary", "arbitrary"),
    )(qkv, qkv, qkv)


ROUTE_COLS = N_GROUPS + N_EXPERTS


def _to_token_major(ref, val):
    rows = val.shape[0]
    for j in range(D_MODEL // LANES):
        ref[pl.ds(j, rows, stride=SUBLANES), :] = val[:, j * LANES:(j + 1) * LANES]


def _from_token_major(ref, rows, j):
    return ref[pl.ds(j, rows, stride=SUBLANES), :]


def _outproj_kernel(xa_ref, xb_ref, ffa_ref, ffb_ref, ata_ref, atb_ref, cva_ref, cvb_ref,
                    mod_ref, n2_ref, wo_ref, wr_ref, br_ref, x1_ref, hx_ref, route_ref):
    mix_in = jnp.concatenate([_pick_stream(ffa_ref, ffb_ref), _pick_stream(ata_ref, atb_ref),
                              _pick_stream(cva_ref, cvb_ref)], axis=-1)
    mix = jnp.dot(mix_in, wo_ref[0], preferred_element_type=F32)
    g1 = mod_ref[:, 2 * D_MODEL:3 * D_MODEL]
    x1 = _pick_stream(xa_ref, xb_ref) + g1 * mix
    x1_ref[...] = x1
    ms = jnp.mean(x1 * x1, axis=-1, keepdims=True)
    y = x1 * lax.rsqrt(ms + EPS) * n2_ref[0]
    sh2 = mod_ref[:, 3 * D_MODEL:4 * D_MODEL]
    sc2 = mod_ref[:, 4 * D_MODEL:5 * D_MODEL]
    hx = y * (1.0 + sc2) + sh2
    _to_token_major(hx_ref, hx)

    hx_hi = hx.astype(BF16)
    hx_lo = (hx - hx_hi.astype(F32)).astype(BF16)
    part = (jnp.dot(hx_hi, wr_ref[0], preferred_element_type=F32)
            + jnp.dot(hx_lo, wr_ref[0], preferred_element_type=F32))
    logits = part[:, 0:LANES] + part[:, LANES:2 * LANES] + br_ref[0]
    tm = logits.shape[0]
    lane = lax.broadcasted_iota(jnp.int32, (tm, LANES), 1)
    big = jnp.int32(LANES)
    gl = jnp.where(lane < N_GROUPS, logits, -jnp.inf)
    gmax = jnp.max(gl, axis=1, keepdims=True)
    gidx = jnp.min(jnp.where(gl == gmax, lane, big), axis=1, keepdims=True)
    g_w = 1.0 / jnp.sum(jnp.exp(gl - gmax), axis=1, keepdims=True)
    e_lane = lane - N_GROUPS
    in_group = (e_lane >= 0) & (e_lane < N_EXPERTS) & ((e_lane >> 3) == gidx)
    es = jnp.where(in_group, logits, -jnp.inf)
    t1 = jnp.max(es, axis=1, keepdims=True)
    i1 = jnp.min(jnp.where(es == t1, lane, big), axis=1, keepdims=True)
    es2 = jnp.where(lane == i1, -jnp.inf, es)
    t2 = jnp.max(es2, axis=1, keepdims=True)
    i2 = jnp.min(jnp.where(es2 == t2, lane, big), axis=1, keepdims=True)
    dlt = jnp.exp(t2 - t1)
    w1 = g_w / (1.0 + dlt)
    w2 = g_w * dlt / (1.0 + dlt)
    e1 = (i1 - N_GROUPS).astype(F32)
    e2 = (i2 - N_GROUPS).astype(F32)
    route_ref[...] = jnp.where(lane == 0, e1, jnp.where(lane == 1, e2,
                               jnp.where(lane == 2, w1, jnp.where(lane == 3, w2, 0.0))))


def _out_projection(x_pair, ctx_first_block, ff_pair, at_pair, cv_pair, mod4, norm2, w_out_bf, w_r, b_r, li, n_tok):
    nt = n_tok // TM
    return pl.pallas_call(
        _outproj_kernel,
        name="out_projection",
        grid=(nt,),
        in_specs=[
            _lat_spec(D_MODEL), _ctx_spec(D_MODEL, ctx_first_block),
            _lat_spec(D_FOURIER), _ctx_spec(D_FOURIER, 0),
            _lat_spec(D_NA), _ctx_spec(D_NA, 0),
            _lat_spec(D_CONV), _ctx_spec(D_CONV, 0),
            pl.BlockSpec((None, None, 1, 6 * D_MODEL), lambda i: (li, _mod_row(i, TM), 0, 0)),
            pl.BlockSpec((1, 1, D_MODEL), lambda i: (li, 0, 0)),
            pl.BlockSpec((1, D_MODEL, D_MODEL), lambda i: (li, 0, 0)),
            pl.BlockSpec((1, D_MODEL, 2 * LANES), lambda i: (li, 0, 0)),
            pl.BlockSpec((1, 1, LANES), lambda i: (li, 0, 0)),
        ],
        out_specs=[
            pl.BlockSpec((TM, D_MODEL), lambda i: (i, 0)),
            pl.BlockSpec((TM * SUBLANES, LANES), lambda i: (i, 0)),
            pl.BlockSpec((TM, LANES), lambda i: (i, 0)),
        ],
        out_shape=[
            jax.ShapeDtypeStruct((n_tok, D_MODEL), F32),
            jax.ShapeDtypeStruct((n_tok * SUBLANES, LANES), F32),
            jax.ShapeDtypeStruct((n_tok, LANES), F32),
        ],
        compiler_params=_cparams("arbitrary"),
    )(*x_pair, *ff_pair, *at_pair, *cv_pair, mod4, norm2.reshape(DEPTH, 1, D_MODEL), w_out_bf, w_r, b_r)


def _plan_kernel(route_ref, pos_ref, cnt_ref, ends_ref, carry_ref, offs_ref):
    ph = pl.program_id(0)
    i = pl.program_id(1)
    r = route_ref[...]
    tm = r.shape[0]
    lane = lax.broadcasted_iota(jnp.int32, (tm, LANES), 1)
    oh1 = lane == r[:, 0:1].astype(jnp.int32)
    oh2 = lane == r[:, 1:2].astype(jnp.int32)
    oh = jnp.where(oh1 | oh2, 1.0, 0.0)

    @pl.when((ph == 0) & (i == 0))
    def _():
        carry_ref[...] = jnp.zeros_like(carry_ref)

    @pl.when((ph == 1) & (i == 0))
    def _():
        cnt = carry_ref[...]
        cnt_ref[...] = jnp.broadcast_to(cnt, cnt_ref.shape)
        tiles = jnp.ceil(cnt * (1.0 / TMX))
        a = lax.broadcasted_iota(jnp.int32, (LANES, LANES), 0)
        b = lax.broadcasted_iota(jnp.int32, (LANES, LANES), 1)
        upper = jnp.where(a < b, 1.0, 0.0).astype(BF16)
        tiles8 = jnp.broadcast_to(tiles, (SUBLANES, LANES))
        first = jnp.dot(tiles8.astype(BF16), upper, preferred_element_type=F32)
        offs_ref[...] = first[0:1, :] * TMX
        ends_ref[...] = (first + tiles8) * TMX
        carry_ref[...] = jnp.zeros_like(carry_ref)

    @pl.when(ph == 1)
    def _():
        row = lax.broadcasted_iota(jnp.int32, (tm, tm), 0)
        col = lax.broadcasted_iota(jnp.int32, (tm, tm), 1)
        tri = jnp.where(row > col, 1.0, 0.0).astype(BF16)
        cum = jnp.dot(tri, oh.astype(BF16), preferred_element_type=F32) + (carry_ref[...] + offs_ref[...])
        p1 = jnp.sum(jnp.where(oh1, cum, 0.0), axis=1, keepdims=True)
        p2 = jnp.sum(jnp.where(oh2, cum, 0.0), axis=1, keepdims=True)
        pos_ref[...] = jnp.where(lane == 0, p1, jnp.where(lane == 1, p2, 0.0)).astype(jnp.int32)

    carry_ref[...] += jnp.sum(oh, axis=0, keepdims=True)


def _plan(route, n_tok):
    nt = n_tok // TM
    return pl.pallas_call(
        _plan_kernel,
        name="moe_plan",
        grid=(2, nt),
        in_specs=[pl.BlockSpec((TM, LANES), lambda ph, i: (i, 0))],
        out_specs=[
            pl.BlockSpec((TM, LANES), lambda ph, i: (i * ph, 0)),
            pl.BlockSpec((SUBLANES, LANES), lambda ph, i: (0, 0)),
            pl.BlockSpec((SUBLANES, LANES), lambda ph, i: (0, 0)),
        ],
        out_shape=[
            jax.ShapeDtypeStruct((n_tok, LANES), jnp.int32),
            jax.ShapeDtypeStruct((SUBLANES, LANES), F32),
            jax.ShapeDtypeStruct((SUBLANES, LANES), F32),
        ],
        scratch_shapes=[pltpu.VMEM((1, LANES), F32), pltpu.VMEM((1, LANES), F32)],
        compiler_params=_cparams("arbitrary", "arbitrary"),
    )(route)


def _row_tile(ref, row):
    return ref.at[pl.ds(pl.multiple_of(row * SUBLANES, SUBLANES), SUBLANES), :]


PAD_CHUNKS = tuple(1 << b for b in reversed(range(TMX.bit_length() - 1)))


def _dispatch_kernel(pos_ref, pstart_ref, plen_ref, tail_ref, hx_ref, xs_ref, zbuf, sem, zsem):
    i = pl.program_id(0)

    def pad_copies(fn):
        def ebody(e, carry):
            n = plen_ref[e]
            off = pstart_ref[e]
            for rows in PAD_CHUNKS:
                @pl.when((n & rows) != 0)
                def _():
                    fn(pltpu.make_async_copy(
                        zbuf.at[pl.ds(0, rows * SUBLANES), :],
                        xs_ref.at[pl.ds(pl.multiple_of(off * SUBLANES, SUBLANES), rows * SUBLANES), :], zsem.at[0]))
                off = off + (n & rows)
            return carry

        lax.fori_loop(0, N_EXPERTS, ebody, 0)

        def tbody(k, carry):
            row = pl.multiple_of((tail_ref[0] + k * PAD_CHUNKS[0]) * SUBLANES, SUBLANES)
            fn(pltpu.make_async_copy(zbuf, xs_ref.at[pl.ds(row, PAD_CHUNKS[0] * SUBLANES), :], zsem.at[0]))
            return carry

        lax.fori_loop(0, tail_ref[1], tbody, 0)

    @pl.when(i == 0)
    def _():
        zbuf[...] = jnp.zeros_like(zbuf)
        pad_copies(lambda cp: cp.start())

    def body(t, carry):
        src = _row_tile(hx_ref, t)
        for j in range(2):
            pltpu.make_async_copy(src, _row_tile(xs_ref, pos_ref[(i * TD + t) * 2 + j]), sem.at[0]).start(priority=j)
        return carry

    lax.fori_loop(0, TD, body, 0)
    nrow = TD * SUBLANES
    for j in range(2):
        pltpu.make_async_copy(hx_ref, xs_ref.at[pl.ds(0, nrow), :], sem.at[0]).wait()

    @pl.when(i == 0)
    def _():
        pad_copies(lambda cp: cp.wait())


def _dispatch(pos_flat, pad_start, pad_len, tail, hx_tm, n_tok, n_rows):
    return pl.pallas_call(
        _dispatch_kernel,
        name="moe_dispatch",
        grid_spec=pltpu.PrefetchScalarGridSpec(
            num_scalar_prefetch=4,
            grid=(n_tok // TD,),
            in_specs=[pl.BlockSpec((TD * SUBLANES, LANES), lambda i, pos, ps, pn, tl: (i, 0))],
            out_specs=pl.BlockSpec(memory_space=pl.ANY),
            scratch_shapes=[pltpu.VMEM((PAD_CHUNKS[0] * SUBLANES, LANES), F32),
                            pltpu.SemaphoreType.DMA((1,)), pltpu.SemaphoreType.DMA((1,))],
        ),
        out_shape=jax.ShapeDtypeStruct((n_rows * SUBLANES, LANES), F32),
        compiler_params=_cparams("arbitrary"),
    )(pos_flat, pad_start, pad_len, tail, hx_tm)


def _moe_kernel(te_ref, na_ref, xs_ref, wg_ref, wu_ref, wd_ref, os_ref, wg_s, wu_s, wd_s):
    i = pl.program_id(0)
    prev = te_ref[jnp.maximum(i - 1, 0)]
    active = i < na_ref[0]

    @pl.when(active & ((i == 0) | (te_ref[i] != prev)))
    def _():
        wg_s[...] = wg_ref[0, 0].astype(BF16)
        wu_s[...] = wu_ref[0, 0].astype(BF16)
        wd_s[...] = wd_ref[0, 0].astype(BF16)

    @pl.when(active)
    def _():
        x = jnp.concatenate([_from_token_major(xs_ref, TMX, j).astype(BF16)
                             for j in range(D_MODEL // LANES)], axis=-1)
        g = jnp.dot(x, wg_s[...], preferred_element_type=F32)
        u = jnp.dot(x, wu_s[...], preferred_element_type=F32)
        h = (g * jax.nn.sigmoid(g) * u).astype(BF16)
        _to_token_major(os_ref, jnp.dot(h, wd_s[...], preferred_element_type=F32))

    @pl.when(jnp.logical_not(active))
    def _():
        os_ref[...] = jnp.zeros_like(os_ref)


def _moe(tile_expert, n_active, xs, w_gate, w_up, w_down, li, n_tiles):
    def row_map(i, te, na):
        return (jnp.minimum(i, na[0] - 1), 0)

    def w_map(i, te, na):
        return (li, te[jnp.minimum(i, na[0] - 1)], 0, 0)

    return pl.pallas_call(
        _moe_kernel,
        name="moe_experts",
        grid_spec=pltpu.PrefetchScalarGridSpec(
            num_scalar_prefetch=2,
            grid=(n_tiles,),
            in_specs=[
                pl.BlockSpec((TMX * SUBLANES, LANES), row_map),
                pl.BlockSpec((1, 1, D_MODEL, D_EXPERT), w_map),
                pl.BlockSpec((1, 1, D_MODEL, D_EXPERT), w_map),
                pl.BlockSpec((1, 1, D_EXPERT, D_MODEL), w_map),
            ],
            out_specs=pl.BlockSpec((TMX * SUBLANES, LANES), lambda i, te, na: (i, 0)),
            scratch_shapes=[
                pltpu.VMEM((D_MODEL, D_EXPERT), BF16),
                pltpu.VMEM((D_MODEL, D_EXPERT), BF16),
                pltpu.VMEM((D_EXPERT, D_MODEL), BF16),
            ],
        ),
        out_shape=jax.ShapeDtypeStruct((n_tiles * TMX * SUBLANES, LANES), F32),
        compiler_params=_cparams("arbitrary"),
    )(tile_expert, n_active, xs, w_gate, w_up, w_down)


def _start_row_gathers(pos_ref, os_ref, buf, sem, tile, slot, n):
    def body(t, carry):
        tok = tile * n + t
        for j in range(2):
            pltpu.make_async_copy(_row_tile(os_ref, pos_ref[tok * 2 + j]), _row_tile(buf.at[slot, j], t),
                                  sem.at[slot, j]).start(priority=j)
        return carry

    lax.fori_loop(0, n, body, 0)


def _combined_tile(pos_ref, x1_ref, route_ref, mod_ref, os_ref, buf, sem, n):
    i = pl.program_id(0)
    slot = i % 2

    @pl.when(i == 0)
    def _():
        _start_row_gathers(pos_ref, os_ref, buf, sem, 0, 0, n)

    @pl.when(i + 1 < pl.num_programs(0))
    def _():
        _start_row_gathers(pos_ref, os_ref, buf, sem, i + 1, 1 - slot, n)

    for j in range(2):
        pltpu.make_async_copy(os_ref.at[pl.ds(0, n * SUBLANES), :], buf.at[slot, j], sem.at[slot, j]).wait()
    w1 = route_ref[:, 2:3]
    w2 = route_ref[:, 3:4]
    cols = []
    for c in range(D_MODEL // LANES):
        y = w1 * _from_token_major(buf.at[slot, 0], n, c) + w2 * _from_token_major(buf.at[slot, 1], n, c)
        g2 = mod_ref[:, 5 * D_MODEL + c * LANES:5 * D_MODEL + (c + 1) * LANES]
        cols.append(x1_ref[:, c * LANES:(c + 1) * LANES] + g2 * y)
    return jnp.concatenate(cols, axis=-1)


def _combine_final_kernel(pos_ref, x1_ref, route_ref, mod_ref, nf_ref, os_ref, o_ref, buf, sem):
    x2 = _combined_tile(pos_ref, x1_ref, route_ref, mod_ref, os_ref, buf, sem, TC)
    ms = jnp.mean(x2 * x2, axis=-1, keepdims=True)
    o_ref[...] = x2 * lax.rsqrt(ms + EPS) * nf_ref[...]


def _gather_scratch(n):
    return [pltpu.VMEM((2, 2, n * SUBLANES, LANES), F32), pltpu.SemaphoreType.DMA((2, 2))]


def _combine_inproj_kernel(pos_ref, x1_ref, route_ref, mod_ref, modn_ref, n1_ref, w_ref, dft_ref, os_ref,
                           x2_ref, yf_ref, qkv_ref, cv_ref, buf, sem):
    x2 = _combined_tile(pos_ref, x1_ref, route_ref, mod_ref, os_ref, buf, sem, TM)
    x2_ref[...] = x2
    _inproj_math(x2, modn_ref, n1_ref, w_ref, dft_ref, yf_ref, qkv_ref, cv_ref)


def _combine_inproj(pos_flat, x1, route, mod4, norm1, w_in_bf, dft_ch, osrt, li):
    def tmap(i, pos):
        return (i, 0)

    return pl.pallas_call(
        _combine_inproj_kernel,
        name="moe_combine_in_projection",
        grid_spec=pltpu.PrefetchScalarGridSpec(
            num_scalar_prefetch=1,
            grid=(T_ALL // TM,),
            in_specs=[
                pl.BlockSpec((TM, D_MODEL), tmap),
                pl.BlockSpec((TM, LANES), tmap),
                pl.BlockSpec((None, None, 1, 6 * D_MODEL), lambda i, pos: (li, _mod_row(i, TM), 0, 0)),
                pl.BlockSpec((None, None, 1, 6 * D_MODEL), lambda i, pos: (li + 1, _mod_row(i, TM), 0, 0)),
                pl.BlockSpec((1, 1, D_MODEL), lambda i, pos: (li + 1, 0, 0)),
                pl.BlockSpec((1, D_MODEL, D_IN_PROJ), lambda i, pos: (li + 1, 0, 0)),
                pl.BlockSpec((D_FOURIER, 2 * D_FOURIER), lambda i, pos: (0, 0)),
                pl.BlockSpec(memory_space=pl.ANY),
            ],
            out_specs=[
                pl.BlockSpec((TM, D_MODEL), tmap),
                pl.BlockSpec((TM, 2 * D_FOURIER), tmap),
                pl.BlockSpec((TM, 3 * D_NA), tmap),
                pl.BlockSpec((TM, 3 * D_CONV), tmap),
            ],
            scratch_shapes=_gather_scratch(TM),
        ),
        out_shape=[
            jax.ShapeDtypeStruct((T_ALL, D_MODEL), F32),
            jax.ShapeDtypeStruct((T_ALL, 2 * D_FOURIER), BF16),
            jax.ShapeDtypeStruct((T_ALL, 3 * D_NA), BF16),
            jax.ShapeDtypeStruct((T_ALL, 3 * D_CONV), F32),
        ],
        compiler_params=_cparams("arbitrary"),
    )(pos_flat, x1, route, mod4, mod4, norm1.reshape(DEPTH, 1, D_MODEL), w_in_bf, dft_ch, osrt)


def _combine_final(pos_flat, x1, route, mod4, norm_final, osrt, li, n_tok):
    def tmap(i, pos):
        return (i, 0)

    return pl.pallas_call(
        _combine_final_kernel,
        name="moe_combine_final",
        grid_spec=pltpu.PrefetchScalarGridSpec(
            num_scalar_prefetch=1,
            grid=(n_tok // TC,),
            in_specs=[
                pl.BlockSpec((TC, D_MODEL), tmap),
                pl.BlockSpec((TC, LANES), tmap),
                pl.BlockSpec((None, None, 1, 6 * D_MODEL), lambda i, pos: (li, _mod_row(i, TC), 0, 0)),
                pl.BlockSpec((1, D_MODEL), lambda i, pos: (0, 0)),
                pl.BlockSpec(memory_space=pl.ANY),
            ],
            out_specs=pl.BlockSpec((TC, D_MODEL), tmap),
            scratch_shapes=_gather_scratch(TC),
        ),
        out_shape=jax.ShapeDtypeStruct((n_tok, D_MODEL), F32),
        compiler_params=_cparams("arbitrary"),
    )(pos_flat, x1, route, mod4, norm_final.reshape(1, D_MODEL), osrt)


def _moe_block(hx_tm, route, w_gate, w_up, w_down, li, n_tok):
    n_tiles = (2 * n_tok) // TMX + N_EXPERTS
    n_rows = n_tiles * TMX
    pos, cnt, ends = _plan(route, n_tok)
    pos = pos[:, 0:2].reshape(-1)
    cnt = cnt[0, :N_EXPERTS].astype(jnp.int32)
    ends = ends[0, :N_EXPERTS].astype(jnp.int32)
    padded = ((cnt + TMX - 1) // TMX) * TMX
    pad_start = ends - padded + cnt
    tile_start = jnp.arange(n_tiles, dtype=jnp.int32) * TMX
    tile_expert = jnp.minimum(jnp.sum((tile_start[:, None] >= ends[None, :]).astype(jnp.int32), axis=1),
                              N_EXPERTS - 1)
    n_active = (ends[-1:] // TMX).astype(jnp.int32)
    tail = jnp.concatenate([ends[-1:], (n_rows - ends[-1:]) // PAD_CHUNKS[0]])

    xs = _dispatch(pos, pad_start, padded - cnt, tail, hx_tm, n_tok, n_rows)
    return pos, _moe(tile_expert, n_active, xs, w_gate, w_up, w_down, li, n_tiles)


def kernel(x, c, ctx, c_ctx, w_ada, b_ada, norm1, norm2, w_in, w_fourier, w_conv, rpb, w_out, w_rg, b_rg,
           w_re, b_re, w_gate, w_up, w_down, norm_final):
    x_pair, ctx_first = (x.reshape(T_LAT, D_MODEL), ctx.reshape(T_CTX, D_MODEL)), 0
    cc =jnp.concatenate([c, c_ctx[None, :], jnp.zeros((MOD_ROWS - BATCH - 1, D_MODEL), F32)], axis=0)
    mod4 = _modulation(cc, w_ada, b_ada).reshape(DEPTH, MOD_ROWS, 1, 6 * D_MODEL)

    w_in_bf = w_in.astype(BF16)
    w_out_bf = w_out.astype(BF16)
    pad = jnp.zeros((DEPTH, D_MODEL, LANES - ROUTE_COLS), F32)
    w_r = jnp.concatenate([w_rg, w_re, pad], axis=-1)
    w_r_hi = w_r.astype(BF16)
    w_r = jnp.concatenate([w_r_hi, (w_r - w_r_hi.astype(F32)).astype(BF16)], axis=-1)
    b_r = jnp.concatenate([b_rg, b_re, pad[:, 0, :]], axis=-1).reshape(DEPTH, 1, LANES)
    dft_ch = jnp.asarray(_channel_dft(), dtype=F32).astype(BF16)
    ctx_blk = T_LAT // CTX_LEN

    for li in range(DEPTH):
        last = li == DEPTH - 1
        if li == 0:
            yf, qkv, cv = _in_projection(*x_pair, ctx_first, mod4, norm1, w_in_bf, dft_ch, li)
        table = _bias_table(rpb[li].reshape(-1))
        ff = _fourier_mix(yf, w_fourier, li, SEQ, 0)
        cvo = _short_conv(cv, w_conv, li, SEQ, 0)
        at = _attn_latent(qkv, table)
        if last:
            n_tok = T_LAT
            ff_c, cvo_c, at_c = ff, cvo, at
        else:
            n_tok = T_ALL
            ff_c = _fourier_mix(yf, w_fourier, li, CTX_LEN, ctx_blk)
            cvo_c = _short_conv(cv, w_conv, li, CTX_LEN, ctx_blk)
            at_c = _attn_ctx(qkv)
        x1, hx_tm, route = _out_projection(x_pair, ctx_first, (ff, ff_c), (at, at_c), (cvo, cvo_c), mod4, norm2,
                                           w_out_bf, w_r, b_r, li, n_tok)
        pos, osrt = _moe_block(hx_tm, route, w_gate, w_up, w_down, li, n_tok)
        if last:
            out = _combine_final(pos, x1, route, mod4, norm_final, osrt, li, n_tok)
            return out.reshape(BATCH, SEQ, D_MODEL)
        xt, yf, qkv, cv = _combine_inproj(pos, x1, route, mod4, norm1, w_in_bf, dft_ch, osrt, li)
        x_pair, ctx_first = (xt, xt), NT_LAT
```

```python
import functools
import math

import numpy as np
import jax
import jax.numpy as jnp
from jax import lax
from jax.experimental import pallas as pl
from jax.experimental.pallas import tpu as pltpu

F32 = jnp.float32
BF16 = jnp.bfloat16

D_MODEL = 1024
BATCH = 8
SEQ = 2048
DEPTH = 2
GRID_W = 64
ROWS = SEQ // GRID_W
CTX_LEN = 256
T_LAT = BATCH * SEQ
T_CTX = BATCH * CTX_LEN
T_ALL = T_LAT + T_CTX

D_FOURIER = 256
D_FG = 64
HEAD_DIM = 64
NA_HEADS = 8
D_NA = NA_HEADS * HEAD_DIM
D_CONV = 256
D_IN_PROJ = D_FOURIER + 3 * D_NA + 3 * D_CONV
NA_KH = 8
NA_KW = 16
N_GROUPS = 4
EXPERTS_PER_GROUP = 8
N_EXPERTS = 32
D_EXPERT = 512
EPS = 1e-6
NEG_INF = -1e30

LANES = 128
SUBLANES = 8
MOD_ROWS = 16
TM = 512
TMX = 512
TD = 512
TC = 256
VMEM_LIMIT = 56 * 1024 * 1024


def _cparams(*sem):
    return pltpu.CompilerParams(dimension_semantics=sem, vmem_limit_bytes=VMEM_LIMIT)


def _ada_kernel(cc_ref, w_ref, b_ref, o_ref):
    s = cc_ref[...]
    s = s * jax.nn.sigmoid(s)
    acc = jnp.dot(s.astype(BF16), w_ref[0].astype(BF16), preferred_element_type=F32)
    o_ref[0] = acc + b_ref[0]


def _modulation(cc, w_ada, b_ada):
    tn = 1536
    nj = 6 * D_MODEL // tn
    return pl.pallas_call(
        _ada_kernel,
        name="ada_modulation",
        grid=(DEPTH, nj),
        in_specs=[
            pl.BlockSpec((MOD_ROWS, D_MODEL), lambda l, j: (0, 0)),
            pl.BlockSpec((1, D_MODEL, tn), lambda l, j: (l, 0, j)),
            pl.BlockSpec((1, 1, tn), lambda l, j: (l, 0, j)),
        ],
        out_specs=pl.BlockSpec((1, MOD_ROWS, tn), lambda l, j: (l, 0, j)),
        out_shape=jax.ShapeDtypeStruct((DEPTH, MOD_ROWS, 6 * D_MODEL), F32),
        compiler_params=_cparams("arbitrary", "arbitrary"),
    )(cc, w_ada, b_ada.reshape(DEPTH, 1, 6 * D_MODEL))


def _mod_row(i, tile):
    return jnp.minimum((i * tile) // SEQ, BATCH)


NT_LAT = T_LAT // TM


def _lat_spec(cols):
    return pl.BlockSpec((TM, cols), lambda i: (jnp.minimum(i, NT_LAT - 1), 0))


def _ctx_spec(cols, first_block):
    return pl.BlockSpec((TM, cols), lambda i: (jnp.maximum(i - NT_LAT, 0) + first_block, 0))


def _pick_stream(lat_ref, ctx_ref):
    return jnp.where(pl.program_id(0) >= NT_LAT, ctx_ref[...], lat_ref[...])


def _inproj_kernel(xa_ref, xb_ref, mod_ref, n1_ref, w_ref, dft_ref, yf_ref, qkv_ref, cv_ref):
    _inproj_math(_pick_stream(xa_ref, xb_ref), mod_ref, n1_ref, w_ref, dft_ref, yf_ref, qkv_ref, cv_ref)


def _inproj_math(x, mod_ref, n1_ref, w_ref, dft_ref, yf_ref, qkv_ref, cv_ref):
    ms = jnp.mean(x * x, axis=-1, keepdims=True)
    y = x * lax.rsqrt(ms + EPS) * n1_ref[0]
    sh = mod_ref[:, 0:D_MODEL]
    sc = mod_ref[:, D_MODEL:2 * D_MODEL]
    h = y * (1.0 + sc) + sh
    u = jnp.dot(h.astype(BF16), w_ref[0], preferred_element_type=F32)
    yf = jnp.dot(u[:, 0:D_FOURIER].astype(BF16), dft_ref[...], preferred_element_type=F32)
    yf_ref[...] = yf.astype(BF16)
    q0 = D_FOURIER
    qkv_ref[:, 0:D_NA] = (u[:, q0:q0 + D_NA] * (1.0 / math.sqrt(HEAD_DIM))).astype(BF16)
    qkv_ref[:, D_NA:3 * D_NA] = u[:, q0 + D_NA:q0 + 3 * D_NA].astype(BF16)
    cv_ref[...] = u[:, q0 + 3 * D_NA:D_IN_PROJ]


def _in_projection(x_lat, x_ctx, ctx_first_block, mod4, norm1, w_in_bf, dft_ch, li):
    nt = T_ALL // TM
    return pl.pallas_call(
        _inproj_kernel,
        name="in_projection",
        grid=(nt,),
        in_specs=[
            _lat_spec(D_MODEL),
            _ctx_spec(D_MODEL, ctx_first_block),
            pl.BlockSpec((None, None, 1, 6 * D_MODEL), lambda i: (li, _mod_row(i, TM), 0, 0)),
            pl.BlockSpec((1, 1, D_MODEL), lambda i: (li, 0, 0)),
            pl.BlockSpec((1, D_MODEL, D_IN_PROJ), lambda i: (li, 0, 0)),
            pl.BlockSpec((D_FOURIER, 2 * D_FOURIER), lambda i: (0, 0)),
        ],
        out_specs=[
            pl.BlockSpec((TM, 2 * D_FOURIER), lambda i: (i, 0)),
            pl.BlockSpec((TM, 3 * D_NA), lambda i: (i, 0)),
            pl.BlockSpec((TM, 3 * D_CONV), lambda i: (i, 0)),
        ],
        out_shape=[
            jax.ShapeDtypeStruct((T_ALL, 2 * D_FOURIER), BF16),
            jax.ShapeDtypeStruct((T_ALL, 3 * D_NA), BF16),
            jax.ShapeDtypeStruct((T_ALL, 3 * D_CONV), F32),
        ],
        compiler_params=_cparams("arbitrary"),
    )(x_lat, x_ctx, mod4, norm1.reshape(DEPTH, 1, D_MODEL), w_in_bf, dft_ch)


def _dft_tables(n):
    k = np.arange(n, dtype=np.int64)
    ang = 2.0 * np.pi * ((k[:, None] * k[None, :]) % n).astype(np.float64) / n
    s = 1.0 / math.sqrt(n)
    return np.cos(ang) * s, np.sin(ang) * s


def _channel_dft():
    c, s = _dft_tables(D_FG)
    eye = np.eye(D_FOURIER // D_FG)
    return np.concatenate([np.kron(eye, c), np.kron(eye, s)], axis=1)


def _fourier_kernel(cn_ref, sn_ref, y_ref, wf_ref, o_ref):
    z = (jnp.dot(cn_ref[...], y_ref[:, 0:D_FOURIER], preferred_element_type=F32)
         - jnp.dot(sn_ref[...], y_ref[:, D_FOURIER:2 * D_FOURIER], preferred_element_type=F32))
    o_ref[...] = jnp.dot(z.astype(BF16), wf_ref[0].astype(BF16), preferred_element_type=F32).astype(BF16)


def _fourier_mix(yf, w_fourier, li, n, first_block):
    tk = min(n, 512)
    nk = n // tk
    cn, sn = _dft_tables(n)
    cn = jnp.asarray(cn, dtype=F32).astype(BF16)
    sn = jnp.asarray(sn, dtype=F32).astype(BF16)
    return pl.pallas_call(
        _fourier_kernel,
        name="fourier_mix",
        grid=(nk, BATCH),
        in_specs=[
            pl.BlockSpec((tk, n), lambda k, b: (k, 0)),
            pl.BlockSpec((tk, n), lambda k, b: (k, 0)),
            pl.BlockSpec((n, 2 * D_FOURIER), lambda k, b: (first_block + b, 0)),
            pl.BlockSpec((1, D_FOURIER, D_FOURIER), lambda k, b: (li, 0, 0)),
        ],
        out_specs=pl.BlockSpec((tk, D_FOURIER), lambda k, b: (b * nk + k, 0)),
        out_shape=jax.ShapeDtypeStruct((BATCH * n, D_FOURIER), BF16),
        compiler_params=_cparams("arbitrary", "arbitrary"),
    )(cn, sn, yf, w_fourier)


def _conv_kernel(cv_ref, wc_ref, o_ref, pad_ref):
    n = cv_ref.shape[0]
    gb = cv_ref[:, 0:D_CONV]
    g = cv_ref[:, D_CONV:2 * D_CONV] * cv_ref[:, 2 * D_CONV:3 * D_CONV]
    zero = jnp.zeros((SUBLANES, D_CONV), F32)
    pad_ref[0:SUBLANES, :] = zero
    pad_ref[n + SUBLANES:n + 2 * SUBLANES, :] = zero
    pad_ref[SUBLANES:n + SUBLANES, :] = g
    prev = pad_ref[SUBLANES - 1:n + SUBLANES - 1, :]
    nxt = pad_ref[SUBLANES + 1:n + SUBLANES + 1, :]
    w = wc_ref[0]
    o_ref[...] = (gb * (w[0:1, :] * prev + w[1:2, :] * g + w[2:3, :] * nxt)).astype(BF16)


def _short_conv(cv, w_conv, li, n, first_block):
    return pl.pallas_call(
        _conv_kernel,
        name="short_conv",
        grid=(BATCH,),
        in_specs=[
            pl.BlockSpec((n, 3 * D_CONV), lambda b: (first_block + b, 0)),
            pl.BlockSpec((1, 3, D_CONV), lambda b: (li, 0, 0)),
        ],
        out_specs=pl.BlockSpec((n, D_CONV), lambda b: (b, 0)),
        out_shape=jax.ShapeDtypeStruct((BATCH * n, D_CONV), BF16),
        scratch_shapes=[pltpu.VMEM((n + 2 * SUBLANES, D_CONV), F32)],
        compiler_params=_cparams("arbitrary"),
    )(cv, w_conv)


RPB_H = 2 * NA_KH - 1
RPB_W = 2 * NA_KW - 1
KEYS_LOC = NA_KH * GRID_W


def _bias_kernel(rpb_ref, o_ref):
    h = pl.program_id(0)
    qi = lax.broadcasted_iota(jnp.int32, (GRID_W, LANES), 0)
    lj = lax.broadcasted_iota(jnp.int32, (GRID_W, LANES), 1)
    kc = lj & (GRID_W - 1)
    hi = lj >= GRID_W
    d = kc - qi + (NA_KW - 1)
    cs = jnp.clip(qi - NA_KW // 2, 0, GRID_W - NA_KW)
    valid = (kc >= cs) & (kc < cs + NA_KW)
    tiles = []
    for a in range(RPB_H - 1):
        acc = jnp.zeros((GRID_W, LANES), F32)
        for b in range(RPB_W):
            va = rpb_ref[(h * RPB_H + a) * RPB_W + b]
            vb = rpb_ref[(h * RPB_H + a + 1) * RPB_W + b]
            acc = jnp.where(d == b, jnp.where(hi, vb, va), acc)
        tiles.append(jnp.where(valid, acc, NEG_INF))
    for cls in range(NA_KH):
        for m in range(NA_KH // 2):
            o_ref[0, cls, :, m * LANES:(m + 1) * LANES] = tiles[2 * m - cls + NA_KH - 1]


def _bias_table(rpb_flat):
    return pl.pallas_call(
        _bias_kernel,
        name="attn_bias_table",
        grid=(NA_HEADS,),
        in_specs=[pl.BlockSpec(memory_space=pltpu.SMEM)],
        out_specs=pl.BlockSpec((1, NA_KH, GRID_W, KEYS_LOC), lambda h: (h // 2, 0, h % 2, 0)),
        out_shape=jax.ShapeDtypeStruct((NA_HEADS // 2, NA_KH, 2 * GRID_W, KEYS_LOC), F32),
        compiler_params=_cparams("arbitrary"),
    )(rpb_flat)


_NT_DIMS = (((1,), (1,)), ((), ()))


def _softmax_pv(parts):
    m = None
    for s, _ in parts:
        ms = jnp.max(s, axis=1, keepdims=True)
        m = ms if m is None else jnp.maximum(m, ms)
    l = None
    o = None
    for s, v in parts:
        p = jnp.exp(s - m)
        ls = jnp.sum(p, axis=1, keepdims=True)
        os_ = jnp.dot(p.astype(BF16), v, preferred_element_type=F32)
        l = ls if l is None else l + ls
        o = os_ if o is None else o + os_
    return o / l


def _attn_latent_kernel(q_ref, k_ref, v_ref, kc_ref, vc_ref, tab_ref, o_ref, ve_ref, vce_ref):
    lane = lax.broadcasted_iota(jnp.int32, (GRID_W, LANES), 1)
    lo = lane < HEAD_DIM
    ve_ref[:, 0:LANES] = v_ref[...]
    ve_ref[:, LANES:2 * LANES] = jnp.ones((SEQ, LANES), BF16)
    vce_ref[:, 0:LANES] = vc_ref[...]
    vce_ref[:, LANES:2 * LANES] = jnp.ones((CTX_LEN, LANES), BF16)
    kc = kc_ref[...]
    vce = vce_ref[...]

    for r in range(ROWS):
        rs = min(max(r - NA_KH // 2, 0), ROWS - NA_KH)
        cls = r - rs
        q = q_ref[r * GRID_W:(r + 1) * GRID_W, :]
        kl = k_ref[rs * GRID_W:rs * GRID_W + KEYS_LOC, :]
        vl = ve_ref[rs * GRID_W:rs * GRID_W + KEYS_LOC, :]
        zero = jnp.zeros_like(q)
        q2 = jnp.concatenate([jnp.where(lo, q, zero), jnp.where(lo, zero, q)], axis=0)
        s1 = lax.dot_general(q2, kl, _NT_DIMS, preferred_element_type=F32) + tab_ref[0, cls]
        s2 = lax.dot_general(q2, kc, _NT_DIMS, preferred_element_type=F32)
        m = jnp.maximum(jnp.max(s1, axis=1, keepdims=True), jnp.max(s2, axis=1, keepdims=True))
        p1 = jnp.exp(s1 - m).astype(BF16)
        p2 = jnp.exp(s2 - m).astype(BF16)
        oe = (jnp.dot(p1, vl, preferred_element_type=F32) + jnp.dot(p2, vce, preferred_element_type=F32))
        o = oe[:, 0:LANES] / oe[:, LANES:2 * LANES]
        o_ref[r * GRID_W:(r + 1) * GRID_W, :] = jnp.where(lo, o[0:GRID_W], o[GRID_W:2 * GRID_W]).astype(BF16)


def _attn_latent(qkv, table):
    npair = NA_HEADS // 2
    cblk = T_LAT // CTX_LEN
    return pl.pallas_call(
        _attn_latent_kernel,
        name="attn_latent",
        grid=(BATCH, npair),
        in_specs=[
            pl.BlockSpec((SEQ, LANES), lambda b, p: (b, p)),
            pl.BlockSpec((SEQ, LANES), lambda b, p: (b, npair + p)),
            pl.BlockSpec((SEQ, LANES), lambda b, p: (b, 2 * npair + p)),
            pl.BlockSpec((CTX_LEN, LANES), lambda b, p: (cblk + b, npair + p)),
            pl.BlockSpec((CTX_LEN, LANES), lambda b, p: (cblk + b, 2 * npair + p)),
            pl.BlockSpec((1, NA_KH, 2 * GRID_W, KEYS_LOC), lambda b, p: (p, 0, 0, 0)),
        ],
        out_specs=pl.BlockSpec((SEQ, LANES), lambda b, p: (b, p)),
        out_shape=jax.ShapeDtypeStruct((T_LAT, D_NA), BF16),
        scratch_shapes=[pltpu.VMEM((SEQ, 2 * LANES), BF16), pltpu.VMEM((CTX_LEN, 2 * LANES), BF16)],
        compiler_params=_cparams("arbitrary", "arbitrary"),
    )(qkv, qkv, qkv, qkv, qkv, table)


def _attn_ctx_kernel(q_ref, k_ref, v_ref, o_ref):
    lane = lax.broadcasted_iota(jnp.int32, (CTX_LEN, LANES), 1)
    lo = lane < HEAD_DIM
    q = q_ref[...]
    k = k_ref[...]
    v = v_ref[...]
    outs = []
    for hh in range(2):
        qm = jnp.where(lo if hh == 0 else jnp.logical_not(lo), q, jnp.zeros_like(q))
        s = lax.dot_general(qm, k, _NT_DIMS, preferred_element_type=F32)
        outs.append(_softmax_pv([(s, v)]))
    o_ref[...] = jnp.where(lo, outs[0], outs[1]).astype(BF16)


def _attn_ctx(qkv):
    npair = NA_HEADS // 2
    cblk = T_LAT // CTX_LEN
    return pl.pallas_call(
        _attn_ctx_kernel,
        name="attn_context",
        grid=(BATCH, npair),
        in_specs=[
            pl.BlockSpec((CTX_LEN, LANES), lambda b, p: (cblk + b, p)),
            pl.BlockSpec((CTX_LEN, LANES), lambda b, p: (cblk + b, npair + p)),
            pl.BlockSpec((CTX_LEN, LANES), lambda b, p: (cblk + b, 2 * npair + p)),
        ],
        out_specs=pl.BlockSpec((CTX_LEN, LANES), lambda b, p: (b, p)),
        out_shape=jax.ShapeDtypeStruct((T_CTX, D_NA), BF16),
        compiler_params=_cparams("arbitrary", "arbitrary"),
    )(qkv, qkv, qkv)


ROUTE_COLS = N_GROUPS + N_EXPERTS


def _to_token_major(ref, val):
    rows = val.shape[0]
    for j in range(D_MODEL // LANES):
        ref[pl.ds(j, rows, stride=SUBLANES), :] = val[:, j * LANES:(j + 1) * LANES]


def _from_token_major(ref, rows, j):
    return ref[pl.ds(j, rows, stride=SUBLANES), :]


def _outproj_kernel(xa_ref, xb_ref, ffa_ref, ffb_ref, ata_ref, atb_ref, cva_ref, cvb_ref,
                    mod_ref, n2_ref, wo_ref, wr_ref, br_ref, x1_ref, hx_ref, route_ref):
    mix_in = jnp.concatenate([_pick_stream(ffa_ref, ffb_ref), _pick_stream(ata_ref, atb_ref),
                              _pick_stream(cva_ref, cvb_ref)], axis=-1)
    mix = jnp.dot(mix_in, wo_ref[0], preferred_element_type=F32)
    g1 = mod_ref[:, 2 * D_MODEL:3 * D_MODEL]
    x1 = _pick_stream(xa_ref, xb_ref) + g1 * mix
    x1_ref[...] = x1
    ms = jnp.mean(x1 * x1, axis=-1, keepdims=True)
    y = x1 * lax.rsqrt(ms + EPS) * n2_ref[0]
    sh2 = mod_ref[:, 3 * D_MODEL:4 * D_MODEL]
    sc2 = mod_ref[:, 4 * D_MODEL:5 * D_MODEL]
    hx = y * (1.0 + sc2) + sh2
    _to_token_major(hx_ref, hx)

    hx_hi = hx.astype(BF16)
    hx_lo = (hx - hx_hi.astype(F32)).astype(BF16)
    part = (jnp.dot(hx_hi, wr_ref[0], preferred_element_type=F32)
            + jnp.dot(hx_lo, wr_ref[0], preferred_element_type=F32))
    logits = part[:, 0:LANES] + part[:, LANES:2 * LANES] + br_ref[0]
    tm = logits.shape[0]
    lane = lax.broadcasted_iota(jnp.int32, (tm, LANES), 1)
    big = jnp.int32(LANES)
    gl = jnp.where(lane < N_GROUPS, logits, -jnp.inf)
    gmax = jnp.max(gl, axis=1, keepdims=True)
    gidx = jnp.min(jnp.where(gl == gmax, lane, big), axis=1, keepdims=True)
    g_w = 1.0 / jnp.sum(jnp.exp(gl - gmax), axis=1, keepdims=True)
    e_lane = lane - N_GROUPS
    in_group = (e_lane >= 0) & (e_lane < N_EXPERTS) & ((e_lane >> 3) == gidx)
    es = jnp.where(in_group, logits, -jnp.inf)
    t1 = jnp.max(es, axis=1, keepdims=True)
    i1 = jnp.min(jnp.where(es == t1, lane, big), axis=1, keepdims=True)
    es2 = jnp.where(lane == i1, -jnp.inf, es)
    t2 = jnp.max(es2, axis=1, keepdims=True)
    i2 = jnp.min(jnp.where(es2 == t2, lane, big), axis=1, keepdims=True)
    dlt = jnp.exp(t2 - t1)
    w1 = g_w / (1.0 + dlt)
    w2 = g_w * dlt / (1.0 + dlt)
    e1 = (i1 - N_GROUPS).astype(F32)
    e2 = (i2 - N_GROUPS).astype(F32)
    route_ref[...] = jnp.where(lane == 0, e1, jnp.where(lane == 1, e2,
                               jnp.where(lane == 2, w1, jnp.where(lane == 3, w2, 0.0))))


def _out_projection(x_pair, ctx_first_block, ff_pair, at_pair, cv_pair, mod4, norm2, w_out_bf, w_r, b_r, li, n_tok):
    nt = n_tok // TM
    return pl.pallas_call(
        _outproj_kernel,
        name="out_projection",
        grid=(nt,),
        in_specs=[
            _lat_spec(D_MODEL), _ctx_spec(D_MODEL, ctx_first_block),
            _lat_spec(D_FOURIER), _ctx_spec(D_FOURIER, 0),
            _lat_spec(D_NA), _ctx_spec(D_NA, 0),
            _lat_spec(D_CONV), _ctx_spec(D_CONV, 0),
            pl.BlockSpec((None, None, 1, 6 * D_MODEL), lambda i: (li, _mod_row(i, TM), 0, 0)),
            pl.BlockSpec((1, 1, D_MODEL), lambda i: (li, 0, 0)),
            pl.BlockSpec((1, D_MODEL, D_MODEL), lambda i: (li, 0, 0)),
            pl.BlockSpec((1, D_MODEL, 2 * LANES), lambda i: (li, 0, 0)),
            pl.BlockSpec((1, 1, LANES), lambda i: (li, 0, 0)),
        ],
        out_specs=[
            pl.BlockSpec((TM, D_MODEL), lambda i: (i, 0)),
            pl.BlockSpec((TM * SUBLANES, LANES), lambda i: (i, 0)),
            pl.BlockSpec((TM, LANES), lambda i: (i, 0)),
        ],
        out_shape=[
            jax.ShapeDtypeStruct((n_tok, D_MODEL), F32),
            jax.ShapeDtypeStruct((n_tok * SUBLANES, LANES), F32),
            jax.ShapeDtypeStruct((n_tok, LANES), F32),
        ],
        compiler_params=_cparams("arbitrary"),
    )(*x_pair, *ff_pair, *at_pair, *cv_pair, mod4, norm2.reshape(DEPTH, 1, D_MODEL), w_out_bf, w_r, b_r)


def _plan_kernel(route_ref, pos_ref, cnt_ref, ends_ref, carry_ref, offs_ref):
    ph = pl.program_id(0)
    i = pl.program_id(1)
    r = route_ref[...]
    tm = r.shape[0]
    lane = lax.broadcasted_iota(jnp.int32, (tm, LANES), 1)
    oh1 = lane == r[:, 0:1].astype(jnp.int32)
    oh2 = lane == r[:, 1:2].astype(jnp.int32)
    oh = jnp.where(oh1 | oh2, 1.0, 0.0)

    @pl.when((ph == 0) & (i == 0))
    def _():
        carry_ref[...] = jnp.zeros_like(carry_ref)

    @pl.when((ph == 1) & (i == 0))
    def _():
        cnt = carry_ref[...]
        cnt_ref[...] = jnp.broadcast_to(cnt, cnt_ref.shape)
        tiles = jnp.ceil(cnt * (1.0 / TMX))
        a = lax.broadcasted_iota(jnp.int32, (LANES, LANES), 0)
        b = lax.broadcasted_iota(jnp.int32, (LANES, LANES), 1)
        upper = jnp.where(a < b, 1.0, 0.0).astype(BF16)
        tiles8 = jnp.broadcast_to(tiles, (SUBLANES, LANES))
        first = jnp.dot(tiles8.astype(BF16), upper, preferred_element_type=F32)
        offs_ref[...] = first[0:1, :] * TMX
        ends_ref[...] = (first + tiles8) * TMX
        carry_ref[...] = jnp.zeros_like(carry_ref)

    @pl.when(ph == 1)
    def _():
        row = lax.broadcasted_iota(jnp.int32, (tm, tm), 0)
        col = lax.broadcasted_iota(jnp.int32, (tm, tm), 1)
        tri = jnp.where(row > col, 1.0, 0.0).astype(BF16)
        cum = jnp.dot(tri, oh.astype(BF16), preferred_element_type=F32) + (carry_ref[...] + offs_ref[...])
        p1 = jnp.sum(jnp.where(oh1, cum, 0.0), axis=1, keepdims=True)
        p2 = jnp.sum(jnp.where(oh2, cum, 0.0), axis=1, keepdims=True)
        pos_ref[...] = jnp.where(lane == 0, p1, jnp.where(lane == 1, p2, 0.0)).astype(jnp.int32)

    carry_ref[...] += jnp.sum(oh, axis=0, keepdims=True)


def _plan(route, n_tok):
    nt = n_tok // TM
    return pl.pallas_call(
        _plan_kernel,
        name="moe_plan",
        grid=(2, nt),
        in_specs=[pl.BlockSpec((TM, LANES), lambda ph, i: (i, 0))],
        out_specs=[
            pl.BlockSpec((TM, LANES), lambda ph, i: (i * ph, 0)),
            pl.BlockSpec((SUBLANES, LANES), lambda ph, i: (0, 0)),
            pl.BlockSpec((SUBLANES, LANES), lambda ph, i: (0, 0)),
        ],
        out_shape=[
            jax.ShapeDtypeStruct((n_tok, LANES), jnp.int32),
            jax.ShapeDtypeStruct((SUBLANES, LANES), F32),
            jax.ShapeDtypeStruct((SUBLANES, LANES), F32),
        ],
        scratch_shapes=[pltpu.VMEM((1, LANES), F32), pltpu.VMEM((1, LANES), F32)],
        compiler_params=_cparams("arbitrary", "arbitrary"),
    )(route)


def _row_tile(ref, row):
    return ref.at[pl.ds(pl.multiple_of(row * SUBLANES, SUBLANES), SUBLANES), :]


PAD_CHUNKS = tuple(1 << b for b in reversed(range(TMX.bit_length() - 1)))


def _dispatch_kernel(pos_ref, pstart_ref, plen_ref, tail_ref, hx_ref, xs_ref, zbuf, sem, zsem):
    i = pl.program_id(0)

    def pad_copies(fn):
        def ebody(e, carry):
            n = plen_ref[e]
            off = pstart_ref[e]
            for rows in PAD_CHUNKS:
                @pl.when((n & rows) != 0)
                def _():
                    fn(pltpu.make_async_copy(
                        zbuf.at[pl.ds(0, rows * SUBLANES), :],
                        xs_ref.at[pl.ds(pl.multiple_of(off * SUBLANES, SUBLANES), rows * SUBLANES), :], zsem.at[0]))
                off = off + (n & rows)
            return carry

        lax.fori_loop(0, N_EXPERTS, ebody, 0)

        def tbody(k, carry):
            row = pl.multiple_of((tail_ref[0] + k * PAD_CHUNKS[0]) * SUBLANES, SUBLANES)
            fn(pltpu.make_async_copy(zbuf, xs_ref.at[pl.ds(row, PAD_CHUNKS[0] * SUBLANES), :], zsem.at[0]))
            return carry

        lax.fori_loop(0, tail_ref[1], tbody, 0)

    @pl.when(i == 0)
    def _():
        zbuf[...] = jnp.zeros_like(zbuf)
        pad_copies(lambda cp: cp.start())

    def body(t, carry):
        src = _row_tile(hx_ref, t)
        for j in range(2):
            pltpu.make_async_copy(src, _row_tile(xs_ref, pos_ref[(i * TD + t) * 2 + j]), sem.at[0]).start(priority=j)
        return carry

    lax.fori_loop(0, TD, body, 0)
    nrow = TD * SUBLANES
    for j in range(2):
        pltpu.make_async_copy(hx_ref, xs_ref.at[pl.ds(0, nrow), :], sem.at[0]).wait()

    @pl.when(i == 0)
    def _():
        pad_copies(lambda cp: cp.wait())


def _dispatch(pos_flat, pad_start, pad_len, tail, hx_tm, n_tok, n_rows):
    return pl.pallas_call(
        _dispatch_kernel,
        name="moe_dispatch",
        grid_spec=pltpu.PrefetchScalarGridSpec(
            num_scalar_prefetch=4,
            grid=(n_tok // TD,),
            in_specs=[pl.BlockSpec((TD * SUBLANES, LANES), lambda i, pos, ps, pn, tl: (i, 0))],
            out_specs=pl.BlockSpec(memory_space=pl.ANY),
            scratch_shapes=[pltpu.VMEM((PAD_CHUNKS[0] * SUBLANES, LANES), F32),
                            pltpu.SemaphoreType.DMA((1,)), pltpu.SemaphoreType.DMA((1,))],
        ),
        out_shape=jax.ShapeDtypeStruct((n_rows * SUBLANES, LANES), F32),
        compiler_params=_cparams("arbitrary"),
    )(pos_flat, pad_start, pad_len, tail, hx_tm)


def _moe_kernel(te_ref, na_ref, xs_ref, wg_ref, wu_ref, wd_ref, os_ref, wg_s, wu_s, wd_s):
    i = pl.program_id(0)
    prev = te_ref[jnp.maximum(i - 1, 0)]
    active = i < na_ref[0]

    @pl.when(active & ((i == 0) | (te_ref[i] != prev)))
    def _():
        wg_s[...] = wg_ref[0, 0].astype(BF16)
        wu_s[...] = wu_ref[0, 0].astype(BF16)
        wd_s[...] = wd_ref[0, 0].astype(BF16)

    @pl.when(active)
    def _():
        x = jnp.concatenate([_from_token_major(xs_ref, TMX, j).astype(BF16)
                             for j in range(D_MODEL // LANES)], axis=-1)
        g = jnp.dot(x, wg_s[...], preferred_element_type=F32)
        u = jnp.dot(x, wu_s[...], preferred_element_type=F32)
        h = (g * jax.nn.sigmoid(g) * u).astype(BF16)
        _to_token_major(os_ref, jnp.dot(h, wd_s[...], preferred_element_type=F32))

    @pl.when(jnp.logical_not(active))
    def _():
        os_ref[...] = jnp.zeros_like(os_ref)


def _moe(tile_expert, n_active, xs, w_gate, w_up, w_down, li, n_tiles):
    def row_map(i, te, na):
        return (jnp.minimum(i, na[0] - 1), 0)

    def w_map(i, te, na):
        return (li, te[jnp.minimum(i, na[0] - 1)], 0, 0)

    return pl.pallas_call(
        _moe_kernel,
        name="moe_experts",
        grid_spec=pltpu.PrefetchScalarGridSpec(
            num_scalar_prefetch=2,
            grid=(n_tiles,),
            in_specs=[
                pl.BlockSpec((TMX * SUBLANES, LANES), row_map),
                pl.BlockSpec((1, 1, D_MODEL, D_EXPERT), w_map),
                pl.BlockSpec((1, 1, D_MODEL, D_EXPERT), w_map),
                pl.BlockSpec((1, 1, D_EXPERT, D_MODEL), w_map),
            ],
            out_specs=pl.BlockSpec((TMX * SUBLANES, LANES), lambda i, te, na: (i, 0)),
            scratch_shapes=[
                pltpu.VMEM((D_MODEL, D_EXPERT), BF16),
                pltpu.VMEM((D_MODEL, D_EXPERT), BF16),
                pltpu.VMEM((D_EXPERT, D_MODEL), BF16),
            ],
        ),
        out_shape=jax.ShapeDtypeStruct((n_tiles * TMX * SUBLANES, LANES), F32),
        compiler_params=_cparams("arbitrary"),
    )(tile_expert, n_active, xs, w_gate, w_up, w_down)


def _start_row_gathers(pos_ref, os_ref, buf, sem, tile, slot, n):
    def body(t, carry):
        tok = tile * n + t
        for j in range(2):
            pltpu.make_async_copy(_row_tile(os_ref, pos_ref[tok * 2 + j]), _row_tile(buf.at[slot, j], t),
                                  sem.at[slot, j]).start(priority=1)
        return carry

    lax.fori_loop(0, n, body, 0)


def _combined_tile(pos_ref, x1_ref, route_ref, mod_ref, os_ref, buf, sem, n):
    i = pl.program_id(0)
    slot = i % 2

    @pl.when(i == 0)
    def _():
        _start_row_gathers(pos_ref, os_ref, buf, sem, 0, 0, n)

    @pl.when(i + 1 < pl.num_programs(0))
    def _():
        _start_row_gathers(pos_ref, os_ref, buf, sem, i + 1, 1 - slot, n)

    for j in range(2):
        pltpu.make_async_copy(os_ref.at[pl.ds(0, n * SUBLANES), :], buf.at[slot, j], sem.at[slot, j]).wait()
    w1 = route_ref[:, 2:3]
    w2 = route_ref[:, 3:4]
    cols = []
    for c in range(D_MODEL // LANES):
        y = w1 * _from_token_major(buf.at[slot, 0], n, c) + w2 * _from_token_major(buf.at[slot, 1], n, c)
        g2 = mod_ref[:, 5 * D_MODEL + c * LANES:5 * D_MODEL + (c + 1) * LANES]
        cols.append(x1_ref[:, c * LANES:(c + 1) * LANES] + g2 * y)
    return jnp.concatenate(cols, axis=-1)


def _combine_final_kernel(pos_ref, x1_ref, route_ref, mod_ref, nf_ref, os_ref, o_ref, buf, sem):
    x2 = _combined_tile(pos_ref, x1_ref, route_ref, mod_ref, os_ref, buf, sem, TC)
    ms = jnp.mean(x2 * x2, axis=-1, keepdims=True)
    o_ref[...] = x2 * lax.rsqrt(ms + EPS) * nf_ref[...]


def _gather_scratch(n):
    return [pltpu.VMEM((2, 2, n * SUBLANES, LANES), F32), pltpu.SemaphoreType.DMA((2, 2))]


def _combine_inproj_kernel(pos_ref, x1_ref, route_ref, mod_ref, modn_ref, n1_ref, w_ref, dft_ref, os_ref,
                           x2_ref, yf_ref, qkv_ref, cv_ref, buf, sem):
    x2 = _combined_tile(pos_ref, x1_ref, route_ref, mod_ref, os_ref, buf, sem, TM)
    x2_ref[...] = x2
    _inproj_math(x2, modn_ref, n1_ref, w_ref, dft_ref, yf_ref, qkv_ref, cv_ref)


def _combine_inproj(pos_flat, x1, route, mod4, norm1, w_in_bf, dft_ch, osrt, li):
    def tmap(i, pos):
        return (i, 0)

    return pl.pallas_call(
        _combine_inproj_kernel,
        name="moe_combine_in_projection",
        grid_spec=pltpu.PrefetchScalarGridSpec(
            num_scalar_prefetch=1,
            grid=(T_ALL // TM,),
            in_specs=[
                pl.BlockSpec((TM, D_MODEL), tmap),
                pl.BlockSpec((TM, LANES), tmap),
                pl.BlockSpec((None, None, 1, 6 * D_MODEL), lambda i, pos: (li, _mod_row(i, TM), 0, 0)),
                pl.BlockSpec((None, None, 1, 6 * D_MODEL), lambda i, pos: (li + 1, _mod_row(i, TM), 0, 0)),
                pl.BlockSpec((1, 1, D_MODEL), lambda i, pos: (li + 1, 0, 0)),
                pl.BlockSpec((1, D_MODEL, D_IN_PROJ), lambda i, pos: (li + 1, 0, 0)),
                pl.BlockSpec((D_FOURIER, 2 * D_FOURIER), lambda i, pos: (0, 0)),
                pl.BlockSpec(memory_space=pl.ANY),
            ],
            out_specs=[
                pl.BlockSpec((TM, D_MODEL), tmap),
                pl.BlockSpec((TM, 2 * D_FOURIER), tmap),
                pl.BlockSpec((TM, 3 * D_NA), tmap),
                pl.BlockSpec((TM, 3 * D_CONV), tmap),
            ],
            scratch_shapes=_gather_scratch(TM),
        ),
        out_shape=[
            jax.ShapeDtypeStruct((T_ALL, D_MODEL), F32),
            jax.ShapeDtypeStruct((T_ALL, 2 * D_FOURIER), BF16),
            jax.ShapeDtypeStruct((T_ALL, 3 * D_NA), BF16),
            jax.ShapeDtypeStruct((T_ALL, 3 * D_CONV), F32),
        ],
        compiler_params=_cparams("arbitrary"),
    )(pos_flat, x1, route, mod4, mod4, norm1.reshape(DEPTH, 1, D_MODEL), w_in_bf, dft_ch, osrt)


def _combine_final(pos_flat, x1, route, mod4, norm_final, osrt, li, n_tok):
    def tmap(i, pos):
        return (i, 0)

    return pl.pallas_call(
        _combine_final_kernel,
        name="moe_combine_final",
        grid_spec=pltpu.PrefetchScalarGridSpec(
            num_scalar_prefetch=1,
            grid=(n_tok // TC,),
            in_specs=[
                pl.BlockSpec((TC, D_MODEL), tmap),
                pl.BlockSpec((TC, LANES), tmap),
                pl.BlockSpec((None, None, 1, 6 * D_MODEL), lambda i, pos: (li, _mod_row(i, TC), 0, 0)),
                pl.BlockSpec((1, D_MODEL), lambda i, pos: (0, 0)),
                pl.BlockSpec(memory_space=pl.ANY),
            ],
            out_specs=pl.BlockSpec((TC, D_MODEL), tmap),
            scratch_shapes=_gather_scratch(TC),
        ),
        out_shape=jax.ShapeDtypeStruct((n_tok, D_MODEL), F32),
        compiler_params=_cparams("arbitrary"),
    )(pos_flat, x1, route, mod4, norm_final.reshape(1, D_MODEL), osrt)


def _moe_block(hx_tm, route, w_gate, w_up, w_down, li, n_tok):
    n_tiles = (2 * n_tok) // TMX + N_EXPERTS
    n_rows = n_tiles * TMX
    pos, cnt, ends = _plan(route, n_tok)
    pos = pos[:, 0:2].reshape(-1)
    cnt = cnt[0, :N_EXPERTS].astype(jnp.int32)
    ends = ends[0, :N_EXPERTS].astype(jnp.int32)
    padded = ((cnt + TMX - 1) // TMX) * TMX
    pad_start = ends - padded + cnt
    tile_start = jnp.arange(n_tiles, dtype=jnp.int32) * TMX
    tile_expert = jnp.minimum(jnp.sum((tile_start[:, None] >= ends[None, :]).astype(jnp.int32), axis=1),
                              N_EXPERTS - 1)
    n_active = (ends[-1:] // TMX).astype(jnp.int32)
    tail = jnp.concatenate([ends[-1:], (n_rows - ends[-1:]) // PAD_CHUNKS[0]])

    xs = _dispatch(pos, pad_start, padded - cnt, tail, hx_tm, n_tok, n_rows)
    return pos, _moe(tile_expert, n_active, xs, w_gate, w_up, w_down, li, n_tiles)


def kernel(x, c, ctx, c_ctx, w_ada, b_ada, norm1, norm2, w_in, w_fourier, w_conv, rpb, w_out, w_rg, b_rg,
           w_re, b_re, w_gate, w_up, w_down, norm_final):
    x_pair, ctx_first = (x.reshape(T_LAT, D_MODEL), ctx.reshape(T_CTX, D_MODEL)), 0
    cc =jnp.concatenate([c, c_ctx[None, :], jnp.zeros((MOD_ROWS - BATCH - 1, D_MODEL), F32)], axis=0)
    mod4 = _modulation(cc, w_ada, b_ada).reshape(DEPTH, MOD_ROWS, 1, 6 * D_MODEL)

    w_in_bf = w_in.astype(BF16)
    w_out_bf = w_out.astype(BF16)
    pad = jnp.zeros((DEPTH, D_MODEL, LANES - ROUTE_COLS), F32)
    w_r = jnp.concatenate([w_rg, w_re, pad], axis=-1)
    w_r_hi = w_r.astype(BF16)
    w_r = jnp.concatenate([w_r_hi, (w_r - w_r_hi.astype(F32)).astype(BF16)], axis=-1)
    b_r = jnp.concatenate([b_rg, b_re, pad[:, 0, :]], axis=-1).reshape(DEPTH, 1, LANES)
    dft_ch = jnp.asarray(_channel_dft(), dtype=F32).astype(BF16)
    ctx_blk = T_LAT // CTX_LEN

    for li in range(DEPTH):
        last = li == DEPTH - 1
        if li == 0:
            yf, qkv, cv = _in_projection(*x_pair, ctx_first, mod4, norm1, w_in_bf, dft_ch, li)
        table = _bias_table(rpb[li].reshape(-1))
        ff = _fourier_mix(yf, w_fourier, li, SEQ, 0)
        cvo = _short_conv(cv, w_conv, li, SEQ, 0)
        at = _attn_latent(qkv, table)
        if last:
            n_tok = T_LAT
            ff_c, cvo_c, at_c = ff, cvo, at
        else:
            n_tok = T_ALL
            ff_c = _fourier_mix(yf, w_fourier, li, CTX_LEN, ctx_blk)
            cvo_c = _short_conv(cv, w_conv, li, CTX_LEN, ctx_blk)
            at_c = _attn_ctx(qkv)
        x1, hx_tm, route = _out_projection(x_pair, ctx_first, (ff, ff_c), (at, at_c), (cvo, cvo_c), mod4, norm2,
                                           w_out_bf, w_r, b_r, li, n_tok)
        pos, osrt = _moe_block(hx_tm, route, w_gate, w_up, w_down, li, n_tok)
        if last:
            out = _combine_final(pos, x1, route, mod4, norm_final, osrt, li, n_tok)
            return out.reshape(BATCH, SEQ, D_MODEL)
        xt, yf, qkv, cv = _combine_inproj(pos, x1, route, mod4, norm1, w_in_bf, dft_ch, osrt, li)
        x_pair, ctx_first = (xt, xt), NT_LAT
```

```python
import functools
import math

import numpy as np
import jax
import jax.numpy as jnp
from jax import lax
from jax.experimental import pallas as pl
from jax.experimental.pallas import tpu as pltpu

F32 = jnp.float32
BF16 = jnp.bfloat16

D_MODEL = 1024
BATCH = 8
SEQ = 2048
DEPTH = 2
GRID_W = 64
ROWS = SEQ // GRID_W
CTX_LEN = 256
T_LAT = BATCH * SEQ
T_CTX = BATCH * CTX_LEN
T_ALL = T_LAT + T_CTX

D_FOURIER = 256
D_FG = 64
HEAD_DIM = 64
NA_HEADS = 8
D_NA = NA_HEADS * HEAD_DIM
D_CONV = 256
D_IN_PROJ = D_FOURIER + 3 * D_NA + 3 * D_CONV
NA_KH = 8
NA_KW = 16
N_GROUPS = 4
EXPERTS_PER_GROUP = 8
N_EXPERTS = 32
D_EXPERT = 512
EPS = 1e-6
NEG_INF = -1e30

LANES = 128
SUBLANES = 8
MOD_ROWS = 16
TM = 512
TMX = 512
TD = 512
TC = 256
VMEM_LIMIT = 56 * 1024 * 1024


def _cparams(*sem):
    return pltpu.CompilerParams(dimension_semantics=sem, vmem_limit_bytes=VMEM_LIMIT)


def _ada_kernel(cc_ref, w_ref, b_ref, o_ref):
    s = cc_ref[...]
    s = s * jax.nn.sigmoid(s)
    acc = jnp.dot(s.astype(BF16), w_ref[0].astype(BF16), preferred_element_type=F32)
    o_ref[0] = acc + b_ref[0]


def _modulation(cc, w_ada, b_ada):
    tn = 1536
    nj = 6 * D_MODEL // tn
    return pl.pallas_call(
        _ada_kernel,
        name="ada_modulation",
        grid=(DEPTH, nj),
        in_specs=[
            pl.BlockSpec((MOD_ROWS, D_MODEL), lambda l, j: (0, 0)),
            pl.BlockSpec((1, D_MODEL, tn), lambda l, j: (l, 0, j)),
            pl.BlockSpec((1, 1, tn), lambda l, j: (l, 0, j)),
        ],
        out_specs=pl.BlockSpec((1, MOD_ROWS, tn), lambda l, j: (l, 0, j)),
        out_shape=jax.ShapeDtypeStruct((DEPTH, MOD_ROWS, 6 * D_MODEL), F32),
        compiler_params=_cparams("arbitrary", "arbitrary"),
    )(cc, w_ada, b_ada.reshape(DEPTH, 1, 6 * D_MODEL))


def _mod_row(i, tile):
    return jnp.minimum((i * tile) // SEQ, BATCH)


NT_LAT = T_LAT // TM


def _lat_spec(cols):
    return pl.BlockSpec((TM, cols), lambda i: (jnp.minimum(i, NT_LAT - 1), 0))


def _ctx_spec(cols, first_block):
    return pl.BlockSpec((TM, cols), lambda i: (jnp.maximum(i - NT_LAT, 0) + first_block, 0))


def _pick_stream(lat_ref, ctx_ref):
    return jnp.where(pl.program_id(0) >= NT_LAT, ctx_ref[...], lat_ref[...])


def _inproj_kernel(xa_ref, xb_ref, mod_ref, n1_ref, w_ref, dft_ref, yf_ref, qkv_ref, cv_ref):
    _inproj_math(_pick_stream(xa_ref, xb_ref), mod_ref, n1_ref, w_ref, dft_ref, yf_ref, qkv_ref, cv_ref)


INPROJ_CHUNK = 256
INPROJ_CHUNKS = D_IN_PROJ // INPROJ_CHUNK


def _inproj_math(x, mod_ref, n1_ref, w_ref, dft_ref, yf_ref, qkv_ref, cv_ref, before_chunk=None):
    ms = jnp.mean(x * x, axis=-1, keepdims=True)
    y = x * lax.rsqrt(ms + EPS) * n1_ref[0]
    sh = mod_ref[:, 0:D_MODEL]
    sc = mod_ref[:, D_MODEL:2 * D_MODEL]
    h = (y * (1.0 + sc) + sh).astype(BF16)
    q0 = D_FOURIER
    c0_conv = q0 + 3 * D_NA
    for c in range(INPROJ_CHUNKS):
        if before_chunk is not None:
            before_chunk(c)
        c0 = c * INPROJ_CHUNK
        u = jnp.dot(h, w_ref[0, :, c0:c0 + INPROJ_CHUNK], preferred_element_type=F32)
        if c0 < q0:
            yf_ref[...] = jnp.dot(u.astype(BF16), dft_ref[...], preferred_element_type=F32).astype(BF16)
        elif c0 < q0 + D_NA:
            qkv_ref[:, c0 - q0:c0 - q0 + INPROJ_CHUNK] = (u * (1.0 / math.sqrt(HEAD_DIM))).astype(BF16)
        elif c0 < c0_conv:
            qkv_ref[:, c0 - q0:c0 - q0 + INPROJ_CHUNK] = u.astype(BF16)
        else:
            cv_ref[:, c0 - c0_conv:c0 - c0_conv + INPROJ_CHUNK] = u


def _in_projection(x_lat, x_ctx, ctx_first_block, mod4, norm1, w_in_bf, dft_ch, li):
    nt = T_ALL // TM
    return pl.pallas_call(
        _inproj_kernel,
        name="in_projection",
        grid=(nt,),
        in_specs=[
            _lat_spec(D_MODEL),
            _ctx_spec(D_MODEL, ctx_first_block),
            pl.BlockSpec((None, None, 1, 6 * D_MODEL), lambda i: (li, _mod_row(i, TM), 0, 0)),
            pl.BlockSpec((1, 1, D_MODEL), lambda i: (li, 0, 0)),
            pl.BlockSpec((1, D_MODEL, D_IN_PROJ), lambda i: (li, 0, 0)),
            pl.BlockSpec((D_FOURIER, 2 * D_FOURIER), lambda i: (0, 0)),
        ],
        out_specs=[
            pl.BlockSpec((TM, 2 * D_FOURIER), lambda i: (i, 0)),
            pl.BlockSpec((TM, 3 * D_NA), lambda i: (i, 0)),
            pl.BlockSpec((TM, 3 * D_CONV), lambda i: (i, 0)),
        ],
        out_shape=[
            jax.ShapeDtypeStruct((T_ALL, 2 * D_FOURIER), BF16),
            jax.ShapeDtypeStruct((T_ALL, 3 * D_NA), BF16),
            jax.ShapeDtypeStruct((T_ALL, 3 * D_CONV), F32),
        ],
        compiler_params=_cparams("arbitrary"),
    )(x_lat, x_ctx, mod4, norm1.reshape(DEPTH, 1, D_MODEL), w_in_bf, dft_ch)


def _dft_tables(n):
    k = np.arange(n, dtype=np.int64)
    ang = 2.0 * np.pi * ((k[:, None] * k[None, :]) % n).astype(np.float64) / n
    s = 1.0 / math.sqrt(n)
    return np.cos(ang) * s, np.sin(ang) * s


def _channel_dft():
    c, s = _dft_tables(D_FG)
    eye = np.eye(D_FOURIER // D_FG)
    return np.concatenate([np.kron(eye, c), np.kron(eye, s)], axis=1)


def _fourier_kernel(cn_ref, sn_ref, y_ref, wf_ref, o_ref):
    z = (jnp.dot(cn_ref[...], y_ref[:, 0:D_FOURIER], preferred_element_type=F32)
         - jnp.dot(sn_ref[...], y_ref[:, D_FOURIER:2 * D_FOURIER], preferred_element_type=F32))
    o_ref[...] = jnp.dot(z.astype(BF16), wf_ref[0].astype(BF16), preferred_element_type=F32).astype(BF16)


def _fourier_mix(yf, w_fourier, li, n, first_block):
    tk = min(n, 512)
    nk = n // tk
    cn, sn = _dft_tables(n)
    cn = jnp.asarray(cn, dtype=F32).astype(BF16)
    sn = jnp.asarray(sn, dtype=F32).astype(BF16)
    return pl.pallas_call(
        _fourier_kernel,
        name="fourier_mix",
        grid=(nk, BATCH),
        in_specs=[
            pl.BlockSpec((tk, n), lambda k, b: (k, 0)),
            pl.BlockSpec((tk, n), lambda k, b: (k, 0)),
            pl.BlockSpec((n, 2 * D_FOURIER), lambda k, b: (first_block + b, 0)),
            pl.BlockSpec((1, D_FOURIER, D_FOURIER), lambda k, b: (li, 0, 0)),
        ],
        out_specs=pl.BlockSpec((tk, D_FOURIER), lambda k, b: (b * nk + k, 0)),
        out_shape=jax.ShapeDtypeStruct((BATCH * n, D_FOURIER), BF16),
        compiler_params=_cparams("arbitrary", "arbitrary"),
    )(cn, sn, yf, w_fourier)


def _conv_kernel(cv_ref, wc_ref, o_ref, pad_ref):
    n = cv_ref.shape[0]
    gb = cv_ref[:, 0:D_CONV]
    g = cv_ref[:, D_CONV:2 * D_CONV] * cv_ref[:, 2 * D_CONV:3 * D_CONV]
    zero = jnp.zeros((SUBLANES, D_CONV), F32)
    pad_ref[0:SUBLANES, :] = zero
    pad_ref[n + SUBLANES:n + 2 * SUBLANES, :] = zero
    pad_ref[SUBLANES:n + SUBLANES, :] = g
    prev = pad_ref[SUBLANES - 1:n + SUBLANES - 1, :]
    nxt = pad_ref[SUBLANES + 1:n + SUBLANES + 1, :]
    w = wc_ref[0]
    o_ref[...] = (gb * (w[0:1, :] * prev + w[1:2, :] * g + w[2:3, :] * nxt)).astype(BF16)


def _short_conv(cv, w_conv, li, n, first_block):
    return pl.pallas_call(
        _conv_kernel,
        name="short_conv",
        grid=(BATCH,),
        in_specs=[
            pl.BlockSpec((n, 3 * D_CONV), lambda b: (first_block + b, 0)),
            pl.BlockSpec((1, 3, D_CONV), lambda b: (li, 0, 0)),
        ],
        out_specs=pl.BlockSpec((n, D_CONV), lambda b: (b, 0)),
        out_shape=jax.ShapeDtypeStruct((BATCH * n, D_CONV), BF16),
        scratch_shapes=[pltpu.VMEM((n + 2 * SUBLANES, D_CONV), F32)],
        compiler_params=_cparams("arbitrary"),
    )(cv, w_conv)


RPB_H = 2 * NA_KH - 1
RPB_W = 2 * NA_KW - 1
KEYS_LOC = NA_KH * GRID_W


def _bias_kernel(rpb_ref, o_ref):
    h = pl.program_id(0)
    qi = lax.broadcasted_iota(jnp.int32, (GRID_W, LANES), 0)
    lj = lax.broadcasted_iota(jnp.int32, (GRID_W, LANES), 1)
    kc = lj & (GRID_W - 1)
    hi = lj >= GRID_W
    d = kc - qi + (NA_KW - 1)
    cs = jnp.clip(qi - NA_KW // 2, 0, GRID_W - NA_KW)
    valid = (kc >= cs) & (kc < cs + NA_KW)
    tiles = []
    for a in range(RPB_H - 1):
        acc = jnp.zeros((GRID_W, LANES), F32)
        for b in range(RPB_W):
            va = rpb_ref[(h * RPB_H + a) * RPB_W + b]
            vb = rpb_ref[(h * RPB_H + a + 1) * RPB_W + b]
            acc = jnp.where(d == b, jnp.where(hi, vb, va), acc)
        tiles.append(jnp.where(valid, acc, NEG_INF))
    for cls in range(NA_KH):
        for m in range(NA_KH // 2):
            o_ref[0, cls, :, m * LANES:(m + 1) * LANES] = tiles[2 * m - cls + NA_KH - 1]


def _bias_table(rpb_flat):
    return pl.pallas_call(
        _bias_kernel,
        name="attn_bias_table",
        grid=(NA_HEADS,),
        in_specs=[pl.BlockSpec(memory_space=pltpu.SMEM)],
        out_specs=pl.BlockSpec((1, NA_KH, GRID_W, KEYS_LOC), lambda h: (h // 2, 0, h % 2, 0)),
        out_shape=jax.ShapeDtypeStruct((NA_HEADS // 2, NA_KH, 2 * GRID_W, KEYS_LOC), F32),
        compiler_params=_cparams("arbitrary"),
    )(rpb_flat)


_NT_DIMS = (((1,), (1,)), ((), ()))


def _softmax_pv(parts):
    m = None
    for s, _ in parts:
        ms = jnp.max(s, axis=1, keepdims=True)
        m = ms if m is None else jnp.maximum(m, ms)
    l = None
    o = None
    for s, v in parts:
        p = jnp.exp(s - m)
        ls = jnp.sum(p, axis=1, keepdims=True)
        os_ = jnp.dot(p.astype(BF16), v, preferred_element_type=F32)
        l = ls if l is None else l + ls
        o = os_ if o is None else o + os_
    return o / l


def _attn_latent_kernel(q_ref, k_ref, v_ref, kc_ref, vc_ref, tab_ref, o_ref, ve_ref, vce_ref):
    lane = lax.broadcasted_iota(jnp.int32, (GRID_W, LANES), 1)
    lo = lane < HEAD_DIM
    ve_ref[:, 0:LANES] = v_ref[...]
    ve_ref[:, LANES:2 * LANES] = jnp.ones((SEQ, LANES), BF16)
    vce_ref[:, 0:LANES] = vc_ref[...]
    vce_ref[:, LANES:2 * LANES] = jnp.ones((CTX_LEN, LANES), BF16)
    kc = kc_ref[...]
    vce = vce_ref[...]

    for r in range(ROWS):
        rs = min(max(r - NA_KH // 2, 0), ROWS - NA_KH)
        cls = r - rs
        q = q_ref[r * GRID_W:(r + 1) * GRID_W, :]
        kl = k_ref[rs * GRID_W:rs * GRID_W + KEYS_LOC, :]
        vl = ve_ref[rs * GRID_W:rs * GRID_W + KEYS_LOC, :]
        zero = jnp.zeros_like(q)
        q2 = jnp.concatenate([jnp.where(lo, q, zero), jnp.where(lo, zero, q)], axis=0)
        s1 = lax.dot_general(q2, kl, _NT_DIMS, preferred_element_type=F32) + tab_ref[0, cls]
        s2 = lax.dot_general(q2, kc, _NT_DIMS, preferred_element_type=F32)
        m = jnp.maximum(jnp.max(s1, axis=1, keepdims=True), jnp.max(s2, axis=1, keepdims=True))
        p1 = jnp.exp(s1 - m).astype(BF16)
        p2 = jnp.exp(s2 - m).astype(BF16)
        oe = (jnp.dot(p1, vl, preferred_element_type=F32) + jnp.dot(p2, vce, preferred_element_type=F32))
        o = oe[:, 0:LANES] / oe[:, LANES:2 * LANES]
        o_ref[r * GRID_W:(r + 1) * GRID_W, :] = jnp.where(lo, o[0:GRID_W], o[GRID_W:2 * GRID_W]).astype(BF16)


def _attn_latent(qkv, table):
    npair = NA_HEADS // 2
    cblk = T_LAT // CTX_LEN
    return pl.pallas_call(
        _attn_latent_kernel,
        name="attn_latent",
        grid=(BATCH, npair),
        in_specs=[
            pl.BlockSpec((SEQ, LANES), lambda b, p: (b, p)),
            pl.BlockSpec((SEQ, LANES), lambda b, p: (b, npair + p)),
            pl.BlockSpec((SEQ, LANES), lambda b, p: (b, 2 * npair + p)),
            pl.BlockSpec((CTX_LEN, LANES), lambda b, p: (cblk + b, npair + p)),
            pl.BlockSpec((CTX_LEN, LANES), lambda b, p: (cblk + b, 2 * npair + p)),
            pl.BlockSpec((1, NA_KH, 2 * GRID_W, KEYS_LOC), lambda b, p: (p, 0, 0, 0)),
        ],
        out_specs=pl.BlockSpec((SEQ, LANES), lambda b, p: (b, p)),
        out_shape=jax.ShapeDtypeStruct((T_LAT, D_NA), BF16),
        scratch_shapes=[pltpu.VMEM((SEQ, 2 * LANES), BF16), pltpu.VMEM((CTX_LEN, 2 * LANES), BF16)],
        compiler_params=_cparams("arbitrary", "arbitrary"),
    )(qkv, qkv, qkv, qkv, qkv, table)


def _attn_ctx_kernel(q_ref, k_ref, v_ref, o_ref):
    lane = lax.broadcasted_iota(jnp.int32, (CTX_LEN, LANES), 1)
    lo = lane < HEAD_DIM
    q = q_ref[...]
    k = k_ref[...]
    v = v_ref[...]
    outs = []
    for hh in range(2):
        qm = jnp.where(lo if hh == 0 else jnp.logical_not(lo), q, jnp.zeros_like(q))
        s = lax.dot_general(qm, k, _NT_DIMS, preferred_element_type=F32)
        outs.append(_softmax_pv([(s, v)]))
    o_ref[...] = jnp.where(lo, outs[0], outs[1]).astype(BF16)


def _attn_ctx(qkv):
    npair = NA_HEADS // 2
    cblk = T_LAT // CTX_LEN
    return pl.pallas_call(
        _attn_ctx_kernel,
        name="attn_context",
        grid=(BATCH, npair),
        in_specs=[
            pl.BlockSpec((CTX_LEN, LANES), lambda b, p: (cblk + b, p)),
            pl.BlockSpec((CTX_LEN, LANES), lambda b, p: (cblk + b, npair + p)),
            pl.BlockSpec((CTX_LEN, LANES), lambda b, p: (cblk + b, 2 * npair + p)),
        ],
        out_specs=pl.BlockSpec((CTX_LEN, LANES), lambda b, p: (b, p)),
        out_shape=jax.ShapeDtypeStruct((T_CTX, D_NA), BF16),
        compiler_params=_cparams("arbitrary", "arbitrary"),
    )(qkv, qkv, qkv)


ROUTE_COLS = N_GROUPS + N_EXPERTS


def _to_token_major(ref, val):
    rows = val.shape[0]
    for j in range(D_MODEL // LANES):
        ref[pl.ds(j, rows, stride=SUBLANES), :] = val[:, j * LANES:(j + 1) * LANES]


def _from_token_major(ref, rows, j):
    return ref[pl.ds(j, rows, stride=SUBLANES), :]


def _outproj_kernel(xa_ref, xb_ref, ffa_ref, ffb_ref, ata_ref, atb_ref, cva_ref, cvb_ref,
                    mod_ref, n2_ref, wo_ref, wr_ref, br_ref, x1_ref, hx_ref, route_ref):
    mix_in = jnp.concatenate([_pick_stream(ffa_ref, ffb_ref), _pick_stream(ata_ref, atb_ref),
                              _pick_stream(cva_ref, cvb_ref)], axis=-1)
    mix = jnp.dot(mix_in, wo_ref[0], preferred_element_type=F32)
    g1 = mod_ref[:, 2 * D_MODEL:3 * D_MODEL]
    x1 = _pick_stream(xa_ref, xb_ref) + g1 * mix
    x1_ref[...] = x1
    ms = jnp.mean(x1 * x1, axis=-1, keepdims=True)
    y = x1 * lax.rsqrt(ms + EPS) * n2_ref[0]
    sh2 = mod_ref[:, 3 * D_MODEL:4 * D_MODEL]
    sc2 = mod_ref[:, 4 * D_MODEL:5 * D_MODEL]
    hx = y * (1.0 + sc2) + sh2
    _to_token_major(hx_ref, hx)

    hx_hi = hx.astype(BF16)
    hx_lo = (hx - hx_hi.astype(F32)).astype(BF16)
    part = (jnp.dot(hx_hi, wr_ref[0], preferred_element_type=F32)
            + jnp.dot(hx_lo, wr_ref[0], preferred_element_type=F32))
    logits = part[:, 0:LANES] + part[:, LANES:2 * LANES] + br_ref[0]
    tm = logits.shape[0]
    lane = lax.broadcasted_iota(jnp.int32, (tm, LANES), 1)
    big = jnp.int32(LANES)
    gl = jnp.where(lane < N_GROUPS, logits, -jnp.inf)
    gmax = jnp.max(gl, axis=1, keepdims=True)
    gidx = jnp.min(jnp.where(gl == gmax, lane, big), axis=1, keepdims=True)
    g_w = 1.0 / jnp.sum(jnp.exp(gl - gmax), axis=1, keepdims=True)
    e_lane = lane - N_GROUPS
    in_group = (e_lane >= 0) & (e_lane < N_EXPERTS) & ((e_lane >> 3) == gidx)
    es = jnp.where(in_group, logits, -jnp.inf)
    t1 = jnp.max(es, axis=1, keepdims=True)
    i1 = jnp.min(jnp.where(es == t1, lane, big), axis=1, keepdims=True)
    es2 = jnp.where(lane == i1, -jnp.inf, es)
    t2 = jnp.max(es2, axis=1, keepdims=True)
    i2 = jnp.min(jnp.where(es2 == t2, lane, big), axis=1, keepdims=True)
    dlt = jnp.exp(t2 - t1)
    w1 = g_w / (1.0 + dlt)
    w2 = g_w * dlt / (1.0 + dlt)
    e1 = (i1 - N_GROUPS).astype(F32)
    e2 = (i2 - N_GROUPS).astype(F32)
    route_ref[...] = jnp.where(lane == 0, e1, jnp.where(lane == 1, e2,
                               jnp.where(lane == 2, w1, jnp.where(lane == 3, w2, 0.0))))


def _out_projection(x_pair, ctx_first_block, ff_pair, at_pair, cv_pair, mod4, norm2, w_out_bf, w_r, b_r, li, n_tok):
    nt = n_tok // TM
    return pl.pallas_call(
        _outproj_kernel,
        name="out_projection",
        grid=(nt,),
        in_specs=[
            _lat_spec(D_MODEL), _ctx_spec(D_MODEL, ctx_first_block),
            _lat_spec(D_FOURIER), _ctx_spec(D_FOURIER, 0),
            _lat_spec(D_NA), _ctx_spec(D_NA, 0),
            _lat_spec(D_CONV), _ctx_spec(D_CONV, 0),
            pl.BlockSpec((None, None, 1, 6 * D_MODEL), lambda i: (li, _mod_row(i, TM), 0, 0)),
            pl.BlockSpec((1, 1, D_MODEL), lambda i: (li, 0, 0)),
            pl.BlockSpec((1, D_MODEL, D_MODEL), lambda i: (li, 0, 0)),
            pl.BlockSpec((1, D_MODEL, 2 * LANES), lambda i: (li, 0, 0)),
            pl.BlockSpec((1, 1, LANES), lambda i: (li, 0, 0)),
        ],
        out_specs=[
            pl.BlockSpec((TM, D_MODEL), lambda i: (i, 0)),
            pl.BlockSpec((TM * SUBLANES, LANES), lambda i: (i, 0)),
            pl.BlockSpec((TM, LANES), lambda i: (i, 0)),
        ],
        out_shape=[
            jax.ShapeDtypeStruct((n_tok, D_MODEL), F32),
            jax.ShapeDtypeStruct((n_tok * SUBLANES, LANES), F32),
            jax.ShapeDtypeStruct((n_tok, LANES), F32),
        ],
        compiler_params=_cparams("arbitrary"),
    )(*x_pair, *ff_pair, *at_pair, *cv_pair, mod4, norm2.reshape(DEPTH, 1, D_MODEL), w_out_bf, w_r, b_r)


def _plan_kernel(route_ref, pos_ref, cnt_ref, ends_ref, carry_ref, offs_ref):
    ph = pl.program_id(0)
    i = pl.program_id(1)
    r = route_ref[...]
    tm = r.shape[0]
    lane = lax.broadcasted_iota(jnp.int32, (tm, LANES), 1)
    oh1 = lane == r[:, 0:1].astype(jnp.int32)
    oh2 = lane == r[:, 1:2].astype(jnp.int32)
    oh = jnp.where(oh1 | oh2, 1.0, 0.0)

    @pl.when((ph == 0) & (i == 0))
    def _():
        carry_ref[...] = jnp.zeros_like(carry_ref)

    @pl.when((ph == 1) & (i == 0))
    def _():
        cnt = carry_ref[...]
        cnt_ref[...] = jnp.broadcast_to(cnt, cnt_ref.shape)
        tiles = jnp.ceil(cnt * (1.0 / TMX))
        a = lax.broadcasted_iota(jnp.int32, (LANES, LANES), 0)
        b = lax.broadcasted_iota(jnp.int32, (LANES, LANES), 1)
        upper = jnp.where(a < b, 1.0, 0.0).astype(BF16)
        tiles8 = jnp.broadcast_to(tiles, (SUBLANES, LANES))
        first = jnp.dot(tiles8.astype(BF16), upper, preferred_element_type=F32)
        offs_ref[...] = first[0:1, :] * TMX
        ends_ref[...] = (first + tiles8) * TMX
        carry_ref[...] = jnp.zeros_like(carry_ref)

    @pl.when(ph == 1)
    def _():
        row = lax.broadcasted_iota(jnp.int32, (tm, tm), 0)
        col = lax.broadcasted_iota(jnp.int32, (tm, tm), 1)
        tri = jnp.where(row > col, 1.0, 0.0).astype(BF16)
        cum = jnp.dot(tri, oh.astype(BF16), preferred_element_type=F32) + (carry_ref[...] + offs_ref[...])
        p1 = jnp.sum(jnp.where(oh1, cum, 0.0), axis=1, keepdims=True)
        p2 = jnp.sum(jnp.where(oh2, cum, 0.0), axis=1, keepdims=True)
        pos_ref[...] = jnp.where(lane == 0, p1, jnp.where(lane == 1, p2, 0.0)).astype(jnp.int32)

    carry_ref[...] += jnp.sum(oh, axis=0, keepdims=True)


def _plan(route, n_tok):
    nt = n_tok // TM
    return pl.pallas_call(
        _plan_kernel,
        name="moe_plan",
        grid=(2, nt),
        in_specs=[pl.BlockSpec((TM, LANES), lambda ph, i: (i, 0))],
        out_specs=[
            pl.BlockSpec((TM, LANES), lambda ph, i: (i * ph, 0)),
            pl.BlockSpec((SUBLANES, LANES), lambda ph, i: (0, 0)),
            pl.BlockSpec((SUBLANES, LANES), lambda ph, i: (0, 0)),
        ],
        out_shape=[
            jax.ShapeDtypeStruct((n_tok, LANES), jnp.int32),
            jax.ShapeDtypeStruct((SUBLANES, LANES), F32),
            jax.ShapeDtypeStruct((SUBLANES, LANES), F32),
        ],
        scratch_shapes=[pltpu.VMEM((1, LANES), F32), pltpu.VMEM((1, LANES), F32)],
        compiler_params=_cparams("arbitrary", "arbitrary"),
    )(route)


def _row_tile(ref, row):
    return ref.at[pl.ds(pl.multiple_of(row * SUBLANES, SUBLANES), SUBLANES), :]


PAD_CHUNKS = tuple(1 << b for b in reversed(range(TMX.bit_length() - 1)))


def _dispatch_kernel(pos_ref, pstart_ref, plen_ref, tail_ref, hx_ref, xs_ref, zbuf, sem, zsem):
    i = pl.program_id(0)

    def pad_copies(fn):
        def ebody(e, carry):
            n = plen_ref[e]
            off = pstart_ref[e]
            for rows in PAD_CHUNKS:
                @pl.when((n & rows) != 0)
                def _():
                    fn(pltpu.make_async_copy(
                        zbuf.at[pl.ds(0, rows * SUBLANES), :],
                        xs_ref.at[pl.ds(pl.multiple_of(off * SUBLANES, SUBLANES), rows * SUBLANES), :], zsem.at[0]))
                off = off + (n & rows)
            return carry

        lax.fori_loop(0, N_EXPERTS, ebody, 0)

        def tbody(k, carry):
            row = pl.multiple_of((tail_ref[0] + k * PAD_CHUNKS[0]) * SUBLANES, SUBLANES)
            fn(pltpu.make_async_copy(zbuf, xs_ref.at[pl.ds(row, PAD_CHUNKS[0] * SUBLANES), :], zsem.at[0]))
            return carry

        lax.fori_loop(0, tail_ref[1], tbody, 0)

    @pl.when(i == 0)
    def _():
        zbuf[...] = jnp.zeros_like(zbuf)
        pad_copies(lambda cp: cp.start())

    def body(t, carry):
        src = _row_tile(hx_ref, t)
        for j in range(2):
            pltpu.make_async_copy(src, _row_tile(xs_ref, pos_ref[(i * TD + t) * 2 + j]), sem.at[0]).start(priority=j)
        return carry

    lax.fori_loop(0, TD, body, 0)
    nrow = TD * SUBLANES
    for j in range(2):
        pltpu.make_async_copy(hx_ref, xs_ref.at[pl.ds(0, nrow), :], sem.at[0]).wait()

    @pl.when(i == 0)
    def _():
        pad_copies(lambda cp: cp.wait())


def _dispatch(pos_flat, pad_start, pad_len, tail, hx_tm, n_tok, n_rows):
    return pl.pallas_call(
        _dispatch_kernel,
        name="moe_dispatch",
        grid_spec=pltpu.PrefetchScalarGridSpec(
            num_scalar_prefetch=4,
            grid=(n_tok // TD,),
            in_specs=[pl.BlockSpec((TD * SUBLANES, LANES), lambda i, pos, ps, pn, tl: (i, 0))],
            out_specs=pl.BlockSpec(memory_space=pl.ANY),
            scratch_shapes=[pltpu.VMEM((PAD_CHUNKS[0] * SUBLANES, LANES), F32),
                            pltpu.SemaphoreType.DMA((1,)), pltpu.SemaphoreType.DMA((1,))],
        ),
        out_shape=jax.ShapeDtypeStruct((n_rows * SUBLANES, LANES), F32),
        compiler_params=_cparams("arbitrary"),
    )(pos_flat, pad_start, pad_len, tail, hx_tm)


def _moe_kernel(te_ref, na_ref, xs_ref, wg_ref, wu_ref, wd_ref, os_ref, wg_s, wu_s, wd_s):
    i = pl.program_id(0)
    prev = te_ref[jnp.maximum(i - 1, 0)]
    active = i < na_ref[0]

    @pl.when(active & ((i == 0) | (te_ref[i] != prev)))
    def _():
        wg_s[...] = wg_ref[0, 0].astype(BF16)
        wu_s[...] = wu_ref[0, 0].astype(BF16)
        wd_s[...] = wd_ref[0, 0].astype(BF16)

    @pl.when(active)
    def _():
        x = jnp.concatenate([_from_token_major(xs_ref, TMX, j).astype(BF16)
                             for j in range(D_MODEL // LANES)], axis=-1)
        g = jnp.dot(x, wg_s[...], preferred_element_type=F32)
        u = jnp.dot(x, wu_s[...], preferred_element_type=F32)
        h = (g * jax.nn.sigmoid(g) * u).astype(BF16)
        _to_token_major(os_ref, jnp.dot(h, wd_s[...], preferred_element_type=F32))

    @pl.when(jnp.logical_not(active))
    def _():
        os_ref[...] = jnp.zeros_like(os_ref)


def _moe(tile_expert, n_active, xs, w_gate, w_up, w_down, li, n_tiles):
    def row_map(i, te, na):
        return (jnp.minimum(i, na[0] - 1), 0)

    def w_map(i, te, na):
        return (li, te[jnp.minimum(i, na[0] - 1)], 0, 0)

    return pl.pallas_call(
        _moe_kernel,
        name="moe_experts",
        grid_spec=pltpu.PrefetchScalarGridSpec(
            num_scalar_prefetch=2,
            grid=(n_tiles,),
            in_specs=[
                pl.BlockSpec((TMX * SUBLANES, LANES), row_map),
                pl.BlockSpec((1, 1, D_MODEL, D_EXPERT), w_map),
                pl.BlockSpec((1, 1, D_MODEL, D_EXPERT), w_map),
                pl.BlockSpec((1, 1, D_EXPERT, D_MODEL), w_map),
            ],
            out_specs=pl.BlockSpec((TMX * SUBLANES, LANES), lambda i, te, na: (i, 0)),
            scratch_shapes=[
                pltpu.VMEM((D_MODEL, D_EXPERT), BF16),
                pltpu.VMEM((D_MODEL, D_EXPERT), BF16),
                pltpu.VMEM((D_EXPERT, D_MODEL), BF16),
            ],
        ),
        out_shape=jax.ShapeDtypeStruct((n_tiles * TMX * SUBLANES, LANES), F32),
        compiler_params=_cparams("arbitrary"),
    )(tile_expert, n_active, xs, w_gate, w_up, w_down)


def _start_row_gathers(pos_ref, os_ref, buf, sem, tile, slot, n, t0=0, t1=None):
    def body(t, carry):
        tok = tile * n + t
        for j in range(2):
            pltpu.make_async_copy(_row_tile(os_ref, pos_ref[tok * 2 + j]), _row_tile(buf.at[slot, j], t),
                                  sem.at[slot, j]).start(priority=j)
        return carry

    lax.fori_loop(t0, n if t1 is None else t1, body, 0)


def _combined_tile(pos_ref, x1_ref, route_ref, mod_ref, os_ref, buf, sem, n, prefetch_next):
    i = pl.program_id(0)
    slot = i % 2

    @pl.when(i == 0)
    def _():
        _start_row_gathers(pos_ref, os_ref, buf, sem, 0, 0, n)

    if prefetch_next:
        @pl.when(i + 1 < pl.num_programs(0))
        def _():
            _start_row_gathers(pos_ref, os_ref, buf, sem, i + 1, 1 - slot, n)

    for j in range(2):
        pltpu.make_async_copy(os_ref.at[pl.ds(0, n * SUBLANES), :], buf.at[slot, j], sem.at[slot, j]).wait()
    w1 = route_ref[:, 2:3]
    w2 = route_ref[:, 3:4]
    cols = []
    for c in range(D_MODEL // LANES):
        y = w1 * _from_token_major(buf.at[slot, 0], n, c) + w2 * _from_token_major(buf.at[slot, 1], n, c)
        g2 = mod_ref[:, 5 * D_MODEL + c * LANES:5 * D_MODEL + (c + 1) * LANES]
        cols.append(x1_ref[:, c * LANES:(c + 1) * LANES] + g2 * y)
    return jnp.concatenate(cols, axis=-1)


def _combine_final_kernel(pos_ref, x1_ref, route_ref, mod_ref, nf_ref, os_ref, o_ref, buf, sem):
    x2 = _combined_tile(pos_ref, x1_ref, route_ref, mod_ref, os_ref, buf, sem, TC, prefetch_next=True)
    ms = jnp.mean(x2 * x2, axis=-1, keepdims=True)
    o_ref[...] = x2 * lax.rsqrt(ms + EPS) * nf_ref[...]


def _gather_scratch(n):
    return [pltpu.VMEM((2, 2, n * SUBLANES, LANES), F32), pltpu.SemaphoreType.DMA((2, 2))]


def _combine_inproj_kernel(pos_ref, x1_ref, route_ref, mod_ref, modn_ref, n1_ref, w_ref, dft_ref, os_ref,
                           x2_ref, yf_ref, qkv_ref, cv_ref, buf, sem):
    x2 = _combined_tile(pos_ref, x1_ref, route_ref, mod_ref, os_ref, buf, sem, TM, prefetch_next=False)
    x2_ref[...] = x2
    i = pl.program_id(0)
    per_chunk = -(-TM // INPROJ_CHUNKS)

    def start_next_rows(c):
        @pl.when(i + 1 < pl.num_programs(0))
        def _():
            _start_row_gathers(pos_ref, os_ref, buf, sem, i + 1, 1 - i % 2, TM,
                               min(c * per_chunk, TM), min((c + 1) * per_chunk, TM))

    _inproj_math(x2, modn_ref, n1_ref, w_ref, dft_ref, yf_ref, qkv_ref, cv_ref, before_chunk=start_next_rows)


def _combine_inproj(pos_flat, x1, route, mod4, norm1, w_in_bf, dft_ch, osrt, li):
    def tmap(i, pos):
        return (i, 0)

    return pl.pallas_call(
        _combine_inproj_kernel,
        name="moe_combine_in_projection",
        grid_spec=pltpu.PrefetchScalarGridSpec(
            num_scalar_prefetch=1,
            grid=(T_ALL // TM,),
            in_specs=[
                pl.BlockSpec((TM, D_MODEL), tmap),
                pl.BlockSpec((TM, LANES), tmap),
                pl.BlockSpec((None, None, 1, 6 * D_MODEL), lambda i, pos: (li, _mod_row(i, TM), 0, 0)),
                pl.BlockSpec((None, None, 1, 6 * D_MODEL), lambda i, pos: (li + 1, _mod_row(i, TM), 0, 0)),
                pl.BlockSpec((1, 1, D_MODEL), lambda i, pos: (li + 1, 0, 0)),
                pl.BlockSpec((1, D_MODEL, D_IN_PROJ), lambda i, pos: (li + 1, 0, 0)),
                pl.BlockSpec((D_FOURIER, 2 * D_FOURIER), lambda i, pos: (0, 0)),
                pl.BlockSpec(memory_space=pl.ANY),
            ],
            out_specs=[
                pl.BlockSpec((TM, D_MODEL), tmap),
                pl.BlockSpec((TM, 2 * D_FOURIER), tmap),
                pl.BlockSpec((TM, 3 * D_NA), tmap),
                pl.BlockSpec((TM, 3 * D_CONV), tmap),
            ],
            scratch_shapes=_gather_scratch(TM),
        ),
        out_shape=[
            jax.ShapeDtypeStruct((T_ALL, D_MODEL), F32),
            jax.ShapeDtypeStruct((T_ALL, 2 * D_FOURIER), BF16),
            jax.ShapeDtypeStruct((T_ALL, 3 * D_NA), BF16),
            jax.ShapeDtypeStruct((T_ALL, 3 * D_CONV), F32),
        ],
        compiler_params=_cparams("arbitrary"),
    )(pos_flat, x1, route, mod4, mod4, norm1.reshape(DEPTH, 1, D_MODEL), w_in_bf, dft_ch, osrt)


def _combine_final(pos_flat, x1, route, mod4, norm_final, osrt, li, n_tok):
    def tmap(i, pos):
        return (i, 0)

    return pl.pallas_call(
        _combine_final_kernel,
        name="moe_combine_final",
        grid_spec=pltpu.PrefetchScalarGridSpec(
            num_scalar_prefetch=1,
            grid=(n_tok // TC,),
            in_specs=[
                pl.BlockSpec((TC, D_MODEL), tmap),
                pl.BlockSpec((TC, LANES), tmap),
                pl.BlockSpec((None, None, 1, 6 * D_MODEL), lambda i, pos: (li, _mod_row(i, TC), 0, 0)),
                pl.BlockSpec((1, D_MODEL), lambda i, pos: (0, 0)),
                pl.BlockSpec(memory_space=pl.ANY),
            ],
            out_specs=pl.BlockSpec((TC, D_MODEL), tmap),
            scratch_shapes=_gather_scratch(TC),
        ),
        out_shape=jax.ShapeDtypeStruct((n_tok, D_MODEL), F32),
        compiler_params=_cparams("arbitrary"),
    )(pos_flat, x1, route, mod4, norm_final.reshape(1, D_MODEL), osrt)


def _moe_block(hx_tm, route, w_gate, w_up, w_down, li, n_tok):
    n_tiles = (2 * n_tok) // TMX + N_EXPERTS
    n_rows = n_tiles * TMX
    pos, cnt, ends = _plan(route, n_tok)
    pos = pos[:, 0:2].reshape(-1)
    cnt = cnt[0, :N_EXPERTS].astype(jnp.int32)
    ends = ends[0, :N_EXPERTS].astype(jnp.int32)
    padded = ((cnt + TMX - 1) // TMX) * TMX
    pad_start = ends - padded + cnt
    tile_start = jnp.arange(n_tiles, dtype=jnp.int32) * TMX
    tile_expert = jnp.minimum(jnp.sum((tile_start[:, None] >= ends[None, :]).astype(jnp.int32), axis=1),
                              N_EXPERTS - 1)
    n_active = (ends[-1:] // TMX).astype(jnp.int32)
    tail = jnp.concatenate([ends[-1:], (n_rows - ends[-1:]) // PAD_CHUNKS[0]])

    xs = _dispatch(pos, pad_start, padded - cnt, tail, hx_tm, n_tok, n_rows)
    return pos, _moe(tile_expert, n_active, xs, w_gate, w_up, w_down, li, n_tiles)


def kernel(x, c, ctx, c_ctx, w_ada, b_ada, norm1, norm2, w_in, w_fourier, w_conv, rpb, w_out, w_rg, b_rg,
           w_re, b_re, w_gate, w_up, w_down, norm_final):
    x_pair, ctx_first = (x.reshape(T_LAT, D_MODEL), ctx.reshape(T_CTX, D_MODEL)), 0
    cc =jnp.concatenate([c, c_ctx[None, :], jnp.zeros((MOD_ROWS - BATCH - 1, D_MODEL), F32)], axis=0)
    mod4 = _modulation(cc, w_ada, b_ada).reshape(DEPTH, MOD_ROWS, 1, 6 * D_MODEL)

    w_in_bf = w_in.astype(BF16)
    w_out_bf = w_out.astype(BF16)
    pad = jnp.zeros((DEPTH, D_MODEL, LANES - ROUTE_COLS), F32)
    w_r = jnp.concatenate([w_rg, w_re, pad], axis=-1)
    w_r_hi = w_r.astype(BF16)
    w_r = jnp.concatenate([w_r_hi, (w_r - w_r_hi.astype(F32)).astype(BF16)], axis=-1)
    b_r = jnp.concatenate([b_rg, b_re, pad[:, 0, :]], axis=-1).reshape(DEPTH, 1, LANES)
    dft_ch = jnp.asarray(_channel_dft(), dtype=F32).astype(BF16)
    ctx_blk = T_LAT // CTX_LEN

    for li in range(DEPTH):
        last = li == DEPTH - 1
        if li == 0:
            yf, qkv, cv = _in_projection(*x_pair, ctx_first, mod4, norm1, w_in_bf, dft_ch, li)
        table = _bias_table(rpb[li].reshape(-1))
        ff = _fourier_mix(yf, w_fourier, li, SEQ, 0)
        cvo = _short_conv(cv, w_conv, li, SEQ, 0)
        at = _attn_latent(qkv, table)
        if last:
            n_tok = T_LAT
            ff_c, cvo_c, at_c = ff, cvo, at
        else:
            n_tok = T_ALL
            ff_c = _fourier_mix(yf, w_fourier, li, CTX_LEN, ctx_blk)
            cvo_c = _short_conv(cv, w_conv, li, CTX_LEN, ctx_blk)
            at_c = _attn_ctx(qkv)
        x1, hx_tm, route = _out_projection(x_pair, ctx_first, (ff, ff_c), (at, at_c), (cvo, cvo_c), mod4, norm2,
                                           w_out_bf, w_r, b_r, li, n_tok)
        pos, osrt = _moe_block(hx_tm, route, w_gate, w_up, w_down, li, n_tok)
        if last:
            out = _combine_final(pos, x1, route, mod4, norm_final, osrt, li, n_tok)
            return out.reshape(BATCH, SEQ, D_MODEL)
        xt, yf, qkv, cv = _combine_inproj(pos, x1, route, mod4, norm1, w_in_bf, dft_ch, osrt, li)
        x_pair, ctx_first = (xt, xt), NT_LAT
```

```python
import functools
import math

import numpy as np
import jax
import jax.numpy as jnp
from jax import lax
from jax.experimental import pallas as pl
from jax.experimental.pallas import tpu as pltpu

F32 = jnp.float32
BF16 = jnp.bfloat16

D_MODEL = 1024
BATCH = 8
SEQ = 2048
DEPTH = 2
GRID_W = 64
ROWS = SEQ // GRID_W
CTX_LEN = 256
T_LAT = BATCH * SEQ
T_CTX = BATCH * CTX_LEN
T_ALL = T_LAT + T_CTX

D_FOURIER = 256
D_FG = 64
HEAD_DIM = 64
NA_HEADS = 8
D_NA = NA_HEADS * HEAD_DIM
D_CONV = 256
D_IN_PROJ = D_FOURIER + 3 * D_NA + 3 * D_CONV
NA_KH = 8
NA_KW = 16
N_GROUPS = 4
EXPERTS_PER_GROUP = 8
N_EXPERTS = 32
D_EXPERT = 512
EPS = 1e-6
NEG_INF = -1e30

LANES = 128
SUBLANES = 8
MOD_ROWS = 16
TM = 512
TMX = 512
TD = 512
TC = 256
VMEM_LIMIT = 56 * 1024 * 1024


def _cparams(*sem):
    return pltpu.CompilerParams(dimension_semantics=sem, vmem_limit_bytes=VMEM_LIMIT)


def _ada_kernel(cc_ref, w_ref, b_ref, o_ref):
    s = cc_ref[...]
    s = s * jax.nn.sigmoid(s)
    acc = jnp.dot(s.astype(BF16), w_ref[0].astype(BF16), preferred_element_type=F32)
    o_ref[0] = acc + b_ref[0]


def _modulation(cc, w_ada, b_ada):
    tn = 1536
    nj = 6 * D_MODEL // tn
    return pl.pallas_call(
        _ada_kernel,
        name="ada_modulation",
        grid=(DEPTH, nj),
        in_specs=[
            pl.BlockSpec((MOD_ROWS, D_MODEL), lambda l, j: (0, 0)),
            pl.BlockSpec((1, D_MODEL, tn), lambda l, j: (l, 0, j)),
            pl.BlockSpec((1, 1, tn), lambda l, j: (l, 0, j)),
        ],
        out_specs=pl.BlockSpec((1, MOD_ROWS, tn), lambda l, j: (l, 0, j)),
        out_shape=jax.ShapeDtypeStruct((DEPTH, MOD_ROWS, 6 * D_MODEL), F32),
        compiler_params=_cparams("arbitrary", "arbitrary"),
    )(cc, w_ada, b_ada.reshape(DEPTH, 1, 6 * D_MODEL))


def _mod_row(i, tile):
    return jnp.minimum((i * tile) // SEQ, BATCH)


NT_LAT = T_LAT // TM


def _lat_spec(cols):
    return pl.BlockSpec((TM, cols), lambda i: (jnp.minimum(i, NT_LAT - 1), 0))


def _ctx_spec(cols, first_block):
    return pl.BlockSpec((TM, cols), lambda i: (jnp.maximum(i - NT_LAT, 0) + first_block, 0))


def _pick_stream(lat_ref, ctx_ref):
    return jnp.where(pl.program_id(0) >= NT_LAT, ctx_ref[...], lat_ref[...])


def _inproj_kernel(xa_ref, xb_ref, mod_ref, n1_ref, w_ref, dft_ref, yf_ref, qkv_ref, cv_ref):
    _inproj_math(_pick_stream(xa_ref, xb_ref), mod_ref, n1_ref, w_ref, dft_ref, yf_ref, qkv_ref, cv_ref)


def _inproj_math(x, mod_ref, n1_ref, w_ref, dft_ref, yf_ref, qkv_ref, cv_ref):
    ms = jnp.mean(x * x, axis=-1, keepdims=True)
    y = x * lax.rsqrt(ms + EPS) * n1_ref[0]
    sh = mod_ref[:, 0:D_MODEL]
    sc = mod_ref[:, D_MODEL:2 * D_MODEL]
    h = y * (1.0 + sc) + sh
    u = jnp.dot(h.astype(BF16), w_ref[0], preferred_element_type=F32)
    yf = jnp.dot(u[:, 0:D_FOURIER].astype(BF16), dft_ref[...], preferred_element_type=F32)
    yf_ref[...] = yf.astype(BF16)
    q0 = D_FOURIER
    qkv_ref[:, 0:D_NA] = (u[:, q0:q0 + D_NA] * (1.0 / math.sqrt(HEAD_DIM))).astype(BF16)
    qkv_ref[:, D_NA:3 * D_NA] = u[:, q0 + D_NA:q0 + 3 * D_NA].astype(BF16)
    cv_ref[...] = u[:, q0 + 3 * D_NA:D_IN_PROJ]


def _in_projection(x_lat, x_ctx, ctx_first_block, mod4, norm1, w_in_bf, dft_ch, li):
    nt = T_ALL // TM
    return pl.pallas_call(
        _inproj_kernel,
        name="in_projection",
        grid=(nt,),
        in_specs=[
            _lat_spec(D_MODEL),
            _ctx_spec(D_MODEL, ctx_first_block),
            pl.BlockSpec((None, None, 1, 6 * D_MODEL), lambda i: (li, _mod_row(i, TM), 0, 0)),
            pl.BlockSpec((1, 1, D_MODEL), lambda i: (li, 0, 0)),
            pl.BlockSpec((1, D_MODEL, D_IN_PROJ), lambda i: (li, 0, 0)),
            pl.BlockSpec((D_FOURIER, 2 * D_FOURIER), lambda i: (0, 0)),
        ],
        out_specs=[
            pl.BlockSpec((TM, 2 * D_FOURIER), lambda i: (i, 0)),
            pl.BlockSpec((TM, 3 * D_NA), lambda i: (i, 0)),
            pl.BlockSpec((TM, 3 * D_CONV), lambda i: (i, 0)),
        ],
        out_shape=[
            jax.ShapeDtypeStruct((T_ALL, 2 * D_FOURIER), BF16),
            jax.ShapeDtypeStruct((T_ALL, 3 * D_NA), BF16),
            jax.ShapeDtypeStruct((T_ALL, 3 * D_CONV), F32),
        ],
        compiler_params=_cparams("arbitrary"),
    )(x_lat, x_ctx, mod4, norm1.reshape(DEPTH, 1, D_MODEL), w_in_bf, dft_ch)


def _dft_tables(n):
    k = np.arange(n, dtype=np.int64)
    ang = 2.0 * np.pi * ((k[:, None] * k[None, :]) % n).astype(np.float64) / n
    s = 1.0 / math.sqrt(n)
    return np.cos(ang) * s, np.sin(ang) * s


def _channel_dft():
    c, s = _dft_tables(D_FG)
    eye = np.eye(D_FOURIER // D_FG)
    return np.concatenate([np.kron(eye, c), np.kron(eye, s)], axis=1)


def _fourier_kernel(cn_ref, sn_ref, y_ref, wf_ref, o_ref):
    z = (jnp.dot(cn_ref[...], y_ref[:, 0:D_FOURIER], preferred_element_type=F32)
         - jnp.dot(sn_ref[...], y_ref[:, D_FOURIER:2 * D_FOURIER], preferred_element_type=F32))
    o_ref[...] = jnp.dot(z.astype(BF16), wf_ref[0].astype(BF16), preferred_element_type=F32).astype(BF16)


def _fourier_mix(yf, w_fourier, li, n, first_block):
    tk = min(n, 512)
    nk = n // tk
    cn, sn = _dft_tables(n)
    cn = jnp.asarray(cn, dtype=F32).astype(BF16)
    sn = jnp.asarray(sn, dtype=F32).astype(BF16)
    return pl.pallas_call(
        _fourier_kernel,
        name="fourier_mix",
        grid=(nk, BATCH),
        in_specs=[
            pl.BlockSpec((tk, n), lambda k, b: (k, 0)),
            pl.BlockSpec((tk, n), lambda k, b: (k, 0)),
            pl.BlockSpec((n, 2 * D_FOURIER), lambda k, b: (first_block + b, 0)),
            pl.BlockSpec((1, D_FOURIER, D_FOURIER), lambda k, b: (li, 0, 0)),
        ],
        out_specs=pl.BlockSpec((tk, D_FOURIER), lambda k, b: (b * nk + k, 0)),
        out_shape=jax.ShapeDtypeStruct((BATCH * n, D_FOURIER), BF16),
        compiler_params=_cparams("arbitrary", "arbitrary"),
    )(cn, sn, yf, w_fourier)


def _conv_kernel(cv_ref, wc_ref, o_ref, pad_ref):
    n = cv_ref.shape[0]
    gb = cv_ref[:, 0:D_CONV]
    g = cv_ref[:, D_CONV:2 * D_CONV] * cv_ref[:, 2 * D_CONV:3 * D_CONV]
    zero = jnp.zeros((SUBLANES, D_CONV), F32)
    pad_ref[0:SUBLANES, :] = zero
    pad_ref[n + SUBLANES:n + 2 * SUBLANES, :] = zero
    pad_ref[SUBLANES:n + SUBLANES, :] = g
    prev = pad_ref[SUBLANES - 1:n + SUBLANES - 1, :]
    nxt = pad_ref[SUBLANES + 1:n + SUBLANES + 1, :]
    w = wc_ref[0]
    o_ref[...] = (gb * (w[0:1, :] * prev + w[1:2, :] * g + w[2:3, :] * nxt)).astype(BF16)


def _short_conv(cv, w_conv, li, n, first_block):
    return pl.pallas_call(
        _conv_kernel,
        name="short_conv",
        grid=(BATCH,),
        in_specs=[
            pl.BlockSpec((n, 3 * D_CONV), lambda b: (first_block + b, 0)),
            pl.BlockSpec((1, 3, D_CONV), lambda b: (li, 0, 0)),
        ],
        out_specs=pl.BlockSpec((n, D_CONV), lambda b: (b, 0)),
        out_shape=jax.ShapeDtypeStruct((BATCH * n, D_CONV), BF16),
        scratch_shapes=[pltpu.VMEM((n + 2 * SUBLANES, D_CONV), F32)],
        compiler_params=_cparams("arbitrary"),
    )(cv, w_conv)


RPB_H = 2 * NA_KH - 1
RPB_W = 2 * NA_KW - 1
KEYS_LOC = NA_KH * GRID_W


def _bias_kernel(rpb_ref, o_ref):
    h = pl.program_id(0)
    qi = lax.broadcasted_iota(jnp.int32, (GRID_W, LANES), 0)
    lj = lax.broadcasted_iota(jnp.int32, (GRID_W, LANES), 1)
    kc = lj & (GRID_W - 1)
    hi = lj >= GRID_W
    d = kc - qi + (NA_KW - 1)
    cs = jnp.clip(qi - NA_KW // 2, 0, GRID_W - NA_KW)
    valid = (kc >= cs) & (kc < cs + NA_KW)
    tiles = []
    for a in range(RPB_H - 1):
        acc = jnp.zeros((GRID_W, LANES), F32)
        for b in range(RPB_W):
            va = rpb_ref[(h * RPB_H + a) * RPB_W + b]
            vb = rpb_ref[(h * RPB_H + a + 1) * RPB_W + b]
            acc = jnp.where(d == b, jnp.where(hi, vb, va), acc)
        tiles.append(jnp.where(valid, acc, NEG_INF))
    for cls in range(NA_KH):
        for m in range(NA_KH // 2):
            o_ref[0, cls, :, m * LANES:(m + 1) * LANES] = tiles[2 * m - cls + NA_KH - 1]


def _bias_table(rpb_flat):
    return pl.pallas_call(
        _bias_kernel,
        name="attn_bias_table",
        grid=(NA_HEADS,),
        in_specs=[pl.BlockSpec(memory_space=pltpu.SMEM)],
        out_specs=pl.BlockSpec((1, NA_KH, GRID_W, KEYS_LOC), lambda h: (h // 2, 0, h % 2, 0)),
        out_shape=jax.ShapeDtypeStruct((NA_HEADS // 2, NA_KH, 2 * GRID_W, KEYS_LOC), F32),
        compiler_params=_cparams("arbitrary"),
    )(rpb_flat)


_NT_DIMS = (((1,), (1,)), ((), ()))


def _softmax_pv(parts):
    m = None
    for s, _ in parts:
        ms = jnp.max(s, axis=1, keepdims=True)
        m = ms if m is None else jnp.maximum(m, ms)
    l = None
    o = None
    for s, v in parts:
        p = jnp.exp(s - m)
        ls = jnp.sum(p, axis=1, keepdims=True)
        os_ = jnp.dot(p.astype(BF16), v, preferred_element_type=F32)
        l = ls if l is None else l + ls
        o = os_ if o is None else o + os_
    return o / l


def _attn_latent_kernel(q_ref, k_ref, v_ref, kc_ref, vc_ref, tab_ref, o_ref, ve_ref, vce_ref):
    lane = lax.broadcasted_iota(jnp.int32, (GRID_W, LANES), 1)
    lo = lane < HEAD_DIM
    ve_ref[:, 0:LANES] = v_ref[...]
    ve_ref[:, LANES:2 * LANES] = jnp.ones((SEQ, LANES), BF16)
    vce_ref[:, 0:LANES] = vc_ref[...]
    vce_ref[:, LANES:2 * LANES] = jnp.ones((CTX_LEN, LANES), BF16)
    kc = kc_ref[...]
    vce = vce_ref[...]

    for r in range(ROWS):
        rs = min(max(r - NA_KH // 2, 0), ROWS - NA_KH)
        cls = r - rs
        q = q_ref[r * GRID_W:(r + 1) * GRID_W, :]
        kl = k_ref[rs * GRID_W:rs * GRID_W + KEYS_LOC, :]
        vl = ve_ref[rs * GRID_W:rs * GRID_W + KEYS_LOC, :]
        zero = jnp.zeros_like(q)
        q2 = jnp.concatenate([jnp.where(lo, q, zero), jnp.where(lo, zero, q)], axis=0)
        s1 = lax.dot_general(q2, kl, _NT_DIMS, preferred_element_type=F32) + tab_ref[0, cls]
        s2 = lax.dot_general(q2, kc, _NT_DIMS, preferred_element_type=F32)
        m = jnp.maximum(jnp.max(s1, axis=1, keepdims=True), jnp.max(s2, axis=1, keepdims=True))
        p1 = jnp.exp(s1 - m).astype(BF16)
        p2 = jnp.exp(s2 - m).astype(BF16)
        oe = (jnp.dot(p1, vl, preferred_element_type=F32) + jnp.dot(p2, vce, preferred_element_type=F32))
        o = oe[:, 0:LANES] / oe[:, LANES:2 * LANES]
        o_ref[r * GRID_W:(r + 1) * GRID_W, :] = jnp.where(lo, o[0:GRID_W], o[GRID_W:2 * GRID_W]).astype(BF16)


def _attn_latent(qkv, table):
    npair = NA_HEADS // 2
    cblk = T_LAT // CTX_LEN
    return pl.pallas_call(
        _attn_latent_kernel,
        name="attn_latent",
        grid=(BATCH, npair),
        in_specs=[
            pl.BlockSpec((SEQ, LANES), lambda b, p: (b, p)),
            pl.BlockSpec((SEQ, LANES), lambda b, p: (b, npair + p)),
            pl.BlockSpec((SEQ, LANES), lambda b, p: (b, 2 * npair + p)),
            pl.BlockSpec((CTX_LEN, LANES), lambda b, p: (cblk + b, npair + p)),
            pl.BlockSpec((CTX_LEN, LANES), lambda b, p: (cblk + b, 2 * npair + p)),
            pl.BlockSpec((1, NA_KH, 2 * GRID_W, KEYS_LOC), lambda b, p: (p, 0, 0, 0)),
        ],
        out_specs=pl.BlockSpec((SEQ, LANES), lambda b, p: (b, p)),
        out_shape=jax.ShapeDtypeStruct((T_LAT, D_NA), BF16),
        scratch_shapes=[pltpu.VMEM((SEQ, 2 * LANES), BF16), pltpu.VMEM((CTX_LEN, 2 * LANES), BF16)],
        compiler_params=_cparams("arbitrary", "arbitrary"),
    )(qkv, qkv, qkv, qkv, qkv, table)


def _attn_ctx_kernel(q_ref, k_ref, v_ref, o_ref):
    lane = lax.broadcasted_iota(jnp.int32, (CTX_LEN, LANES), 1)
    lo = lane < HEAD_DIM
    q = q_ref[...]
    k = k_ref[...]
    v = v_ref[...]
    outs = []
    for hh in range(2):
        qm = jnp.where(lo if hh == 0 else jnp.logical_not(lo), q, jnp.zeros_like(q))
        s = lax.dot_general(qm, k, _NT_DIMS, preferred_element_type=F32)
        outs.append(_softmax_pv([(s, v)]))
    o_ref[...] = jnp.where(lo, outs[0], outs[1]).astype(BF16)


def _attn_ctx(qkv):
    npair = NA_HEADS // 2
    cblk = T_LAT // CTX_LEN
    return pl.pallas_call(
        _attn_ctx_kernel,
        name="attn_context",
        grid=(BATCH, npair),
        in_specs=[
            pl.BlockSpec((CTX_LEN, LANES), lambda b, p: (cblk + b, p)),
            pl.BlockSpec((CTX_LEN, LANES), lambda b, p: (cblk + b, npair + p)),
            pl.BlockSpec((CTX_LEN, LANES), lambda b, p: (cblk + b, 2 * npair + p)),
        ],
        out_specs=pl.BlockSpec((CTX_LEN, LANES), lambda b, p: (b, p)),
        out_shape=jax.ShapeDtypeStruct((T_CTX, D_NA), BF16),
        compiler_params=_cparams("arbitrary", "arbitrary"),
    )(qkv, qkv, qkv)


ROUTE_COLS = N_GROUPS + N_EXPERTS


def _to_token_major(ref, val):
    rows = val.shape[0]
    for j in range(D_MODEL // LANES):
        ref[pl.ds(j, rows, stride=SUBLANES), :] = val[:, j * LANES:(j + 1) * LANES]


def _from_token_major(ref, rows, j):
    return ref[pl.ds(j, rows, stride=SUBLANES), :]


def _outproj_kernel(xa_ref, xb_ref, ffa_ref, ffb_ref, ata_ref, atb_ref, cva_ref, cvb_ref,
                    mod_ref, n2_ref, wo_ref, wr_ref, br_ref, x1_ref, hx_ref, route_ref):
    mix_in = jnp.concatenate([_pick_stream(ffa_ref, ffb_ref), _pick_stream(ata_ref, atb_ref),
                              _pick_stream(cva_ref, cvb_ref)], axis=-1)
    mix = jnp.dot(mix_in, wo_ref[0], preferred_element_type=F32)
    g1 = mod_ref[:, 2 * D_MODEL:3 * D_MODEL]
    x1 = _pick_stream(xa_ref, xb_ref) + g1 * mix
    x1_ref[...] = x1
    ms = jnp.mean(x1 * x1, axis=-1, keepdims=True)
    y = x1 * lax.rsqrt(ms + EPS) * n2_ref[0]
    sh2 = mod_ref[:, 3 * D_MODEL:4 * D_MODEL]
    sc2 = mod_ref[:, 4 * D_MODEL:5 * D_MODEL]
    hx = y * (1.0 + sc2) + sh2
    _to_token_major(hx_ref, hx)

    hx_hi = hx.astype(BF16)
    hx_lo = (hx - hx_hi.astype(F32)).astype(BF16)
    part = (jnp.dot(hx_hi, wr_ref[0], preferred_element_type=F32)
            + jnp.dot(hx_lo, wr_ref[0], preferred_element_type=F32))
    logits = part[:, 0:LANES] + part[:, LANES:2 * LANES] + br_ref[0]
    tm = logits.shape[0]
    lane = lax.broadcasted_iota(jnp.int32, (tm, LANES), 1)
    big = jnp.int32(LANES)
    gl = jnp.where(lane < N_GROUPS, logits, -jnp.inf)
    gmax = jnp.max(gl, axis=1, keepdims=True)
    gidx = jnp.min(jnp.where(gl == gmax, lane, big), axis=1, keepdims=True)
    g_w = 1.0 / jnp.sum(jnp.exp(gl - gmax), axis=1, keepdims=True)
    e_lane = lane - N_GROUPS
    in_group = (e_lane >= 0) & (e_lane < N_EXPERTS) & ((e_lane >> 3) == gidx)
    es = jnp.where(in_group, logits, -jnp.inf)
    t1 = jnp.max(es, axis=1, keepdims=True)
    i1 = jnp.min(jnp.where(es == t1, lane, big), axis=1, keepdims=True)
    es2 = jnp.where(lane == i1, -jnp.inf, es)
    t2 = jnp.max(es2, axis=1, keepdims=True)
    i2 = jnp.min(jnp.where(es2 == t2, lane, big), axis=1, keepdims=True)
    dlt = jnp.exp(t2 - t1)
    w1 = g_w / (1.0 + dlt)
    w2 = g_w * dlt / (1.0 + dlt)
    e1 = (i1 - N_GROUPS).astype(F32)
    e2 = (i2 - N_GROUPS).astype(F32)
    route_ref[...] = jnp.where(lane == 0, e1, jnp.where(lane == 1, e2,
                               jnp.where(lane == 2, w1, jnp.where(lane == 3, w2, 0.0))))


def _out_projection(x_pair, ctx_first_block, ff_pair, at_pair, cv_pair, mod4, norm2, w_out_bf, w_r, b_r, li, n_tok):
    nt = n_tok // TM
    return pl.pallas_call(
        _outproj_kernel,
        name="out_projection",
        grid=(nt,),
        in_specs=[
            _lat_spec(D_MODEL), _ctx_spec(D_MODEL, ctx_first_block),
            _lat_spec(D_FOURIER), _ctx_spec(D_FOURIER, 0),
            _lat_spec(D_NA), _ctx_spec(D_NA, 0),
            _lat_spec(D_CONV), _ctx_spec(D_CONV, 0),
            pl.BlockSpec((None, None, 1, 6 * D_MODEL), lambda i: (li, _mod_row(i, TM), 0, 0)),
            pl.BlockSpec((1, 1, D_MODEL), lambda i: (li, 0, 0)),
            pl.BlockSpec((1, D_MODEL, D_MODEL), lambda i: (li, 0, 0)),
            pl.BlockSpec((1, D_MODEL, 2 * LANES), lambda i: (li, 0, 0)),
            pl.BlockSpec((1, 1, LANES), lambda i: (li, 0, 0)),
        ],
        out_specs=[
            pl.BlockSpec((TM, D_MODEL), lambda i: (i, 0)),
            pl.BlockSpec((TM * SUBLANES, LANES), lambda i: (i, 0)),
            pl.BlockSpec((TM, LANES), lambda i: (i, 0)),
        ],
        out_shape=[
            jax.ShapeDtypeStruct((n_tok, D_MODEL), F32),
            jax.ShapeDtypeStruct((n_tok * SUBLANES, LANES), F32),
            jax.ShapeDtypeStruct((n_tok, LANES), F32),
        ],
        compiler_params=_cparams("arbitrary"),
    )(*x_pair, *ff_pair, *at_pair, *cv_pair, mod4, norm2.reshape(DEPTH, 1, D_MODEL), w_out_bf, w_r, b_r)


def _plan_kernel(route_ref, pos_ref, cnt_ref, ends_ref, carry_ref, offs_ref):
    ph = pl.program_id(0)
    i = pl.program_id(1)
    r = route_ref[...]
    tm = r.shape[0]
    lane = lax.broadcasted_iota(jnp.int32, (tm, LANES), 1)
    oh1 = lane == r[:, 0:1].astype(jnp.int32)
    oh2 = lane == r[:, 1:2].astype(jnp.int32)
    oh = jnp.where(oh1 | oh2, 1.0, 0.0)

    @pl.when((ph == 0) & (i == 0))
    def _():
        carry_ref[...] = jnp.zeros_like(carry_ref)

    @pl.when((ph == 1) & (i == 0))
    def _():
        cnt = carry_ref[...]
        cnt_ref[...] = jnp.broadcast_to(cnt, cnt_ref.shape)
        tiles = jnp.ceil(cnt * (1.0 / TMX))
        a = lax.broadcasted_iota(jnp.int32, (LANES, LANES), 0)
        b = lax.broadcasted_iota(jnp.int32, (LANES, LANES), 1)
        upper = jnp.where(a < b, 1.0, 0.0).astype(BF16)
        tiles8 = jnp.broadcast_to(tiles, (SUBLANES, LANES))
        first = jnp.dot(tiles8.astype(BF16), upper, preferred_element_type=F32)
        offs_ref[...] = first[0:1, :] * TMX
        ends_ref[...] = (first + tiles8) * TMX
        carry_ref[...] = jnp.zeros_like(carry_ref)

    @pl.when(ph == 1)
    def _():
        row = lax.broadcasted_iota(jnp.int32, (tm, tm), 0)
        col = lax.broadcasted_iota(jnp.int32, (tm, tm), 1)
        tri = jnp.where(row > col, 1.0, 0.0).astype(BF16)
        cum = jnp.dot(tri, oh.astype(BF16), preferred_element_type=F32) + (carry_ref[...] + offs_ref[...])
        p1 = jnp.sum(jnp.where(oh1, cum, 0.0), axis=1, keepdims=True)
        p2 = jnp.sum(jnp.where(oh2, cum, 0.0), axis=1, keepdims=True)
        pos_ref[...] = jnp.where(lane == 0, p1, jnp.where(lane == 1, p2, 0.0)).astype(jnp.int32)

    carry_ref[...] += jnp.sum(oh, axis=0, keepdims=True)


def _plan(route, n_tok):
    nt = n_tok // TM
    return pl.pallas_call(
        _plan_kernel,
        name="moe_plan",
        grid=(2, nt),
        in_specs=[pl.BlockSpec((TM, LANES), lambda ph, i: (i, 0))],
        out_specs=[
            pl.BlockSpec((TM, LANES), lambda ph, i: (i * ph, 0)),
            pl.BlockSpec((SUBLANES, LANES), lambda ph, i: (0, 0)),
            pl.BlockSpec((SUBLANES, LANES), lambda ph, i: (0, 0)),
        ],
        out_shape=[
            jax.ShapeDtypeStruct((n_tok, LANES), jnp.int32),
            jax.ShapeDtypeStruct((SUBLANES, LANES), F32),
            jax.ShapeDtypeStruct((SUBLANES, LANES), F32),
        ],
        scratch_shapes=[pltpu.VMEM((1, LANES), F32), pltpu.VMEM((1, LANES), F32)],
        compiler_params=_cparams("arbitrary", "arbitrary"),
    )(route)


def _row_tile(ref, row):
    return ref.at[pl.ds(pl.multiple_of(row * SUBLANES, SUBLANES), SUBLANES), :]


PAD_CHUNKS = tuple(1 << b for b in reversed(range(TMX.bit_length() - 1)))


def _dispatch_kernel(pos_ref, pstart_ref, plen_ref, tail_ref, hx_ref, xs_ref, zbuf, sem, zsem):
    i = pl.program_id(0)

    def pad_copies(fn):
        def ebody(e, carry):
            n = plen_ref[e]
            off = pstart_ref[e]
            for rows in PAD_CHUNKS:
                @pl.when((n & rows) != 0)
                def _():
                    fn(pltpu.make_async_copy(
                        zbuf.at[pl.ds(0, rows * SUBLANES), :],
                        xs_ref.at[pl.ds(pl.multiple_of(off * SUBLANES, SUBLANES), rows * SUBLANES), :], zsem.at[0]))
                off = off + (n & rows)
            return carry

        lax.fori_loop(0, N_EXPERTS, ebody, 0)

        def tbody(k, carry):
            row = pl.multiple_of((tail_ref[0] + k * PAD_CHUNKS[0]) * SUBLANES, SUBLANES)
            fn(pltpu.make_async_copy(zbuf, xs_ref.at[pl.ds(row, PAD_CHUNKS[0] * SUBLANES), :], zsem.at[0]))
            return carry

        lax.fori_loop(0, tail_ref[1], tbody, 0)

    @pl.when(i == 0)
    def _():
        zbuf[...] = jnp.zeros_like(zbuf)
        pad_copies(lambda cp: cp.start())

    def body(t, carry):
        src = _row_tile(hx_ref, t)
        for j in range(2):
            pltpu.make_async_copy(src, _row_tile(xs_ref, pos_ref[(i * TD + t) * 2 + j]), sem.at[0]).start(priority=j)
        return carry

    lax.fori_loop(0, TD, body, 0)
    nrow = TD * SUBLANES
    for j in range(2):
        pltpu.make_async_copy(hx_ref, xs_ref.at[pl.ds(0, nrow), :], sem.at[0]).wait()

    @pl.when(i == 0)
    def _():
        pad_copies(lambda cp: cp.wait())


def _dispatch(pos_flat, pad_start, pad_len, tail, hx_tm, n_tok, n_rows):
    return pl.pallas_call(
        _dispatch_kernel,
        name="moe_dispatch",
        grid_spec=pltpu.PrefetchScalarGridSpec(
            num_scalar_prefetch=4,
            grid=(n_tok // TD,),
            in_specs=[pl.BlockSpec((TD * SUBLANES, LANES), lambda i, pos, ps, pn, tl: (i, 0))],
            out_specs=pl.BlockSpec(memory_space=pl.ANY),
            scratch_shapes=[pltpu.VMEM((PAD_CHUNKS[0] * SUBLANES, LANES), F32),
                            pltpu.SemaphoreType.DMA((1,)), pltpu.SemaphoreType.DMA((1,))],
        ),
        out_shape=jax.ShapeDtypeStruct((n_rows * SUBLANES, LANES), F32),
        compiler_params=_cparams("arbitrary"),
    )(pos_flat, pad_start, pad_len, tail, hx_tm)


def _moe_kernel(te_ref, na_ref, first_ref, wslot_ref, nxt_ref, xs_ref, wg_hbm, wu_hbm, wd_hbm, os_ref,
                wg_f, wu_f, wd_f, wg_s, wu_s, wd_s, wsem, *, li):
    i = pl.program_id(0)
    active = i < na_ref[0]

    def weight_copies(e, slot):
        return [pltpu.make_async_copy(wg_hbm.at[li, e], wg_f.at[slot], wsem.at[slot, 0]),
                pltpu.make_async_copy(wu_hbm.at[li, e], wu_f.at[slot], wsem.at[slot, 1]),
                pltpu.make_async_copy(wd_hbm.at[li, e], wd_f.at[slot], wsem.at[slot, 2])]

    @pl.when(active & (first_ref[i] == 1))
    def _():
        slot = wslot_ref[i]

        @pl.when(i == 0)
        def _():
            for cp in weight_copies(te_ref[0], 0):
                cp.start()

        @pl.when(nxt_ref[i] >= 0)
        def _():
            for cp in weight_copies(nxt_ref[i], 1 - slot):
                cp.start()

        for cp in weight_copies(te_ref[i], slot):
            cp.wait()
        wg_s[...] = wg_f[slot].astype(BF16)
        wu_s[...] = wu_f[slot].astype(BF16)
        wd_s[...] = wd_f[slot].astype(BF16)

    @pl.when(active)
    def _():
        x = jnp.concatenate([_from_token_major(xs_ref, TMX, j).astype(BF16)
                             for j in range(D_MODEL // LANES)], axis=-1)
        g = jnp.dot(x, wg_s[...], preferred_element_type=F32)
        u = jnp.dot(x, wu_s[...], preferred_element_type=F32)
        h = (g * jax.nn.sigmoid(g) * u).astype(BF16)
        _to_token_major(os_ref, jnp.dot(h, wd_s[...], preferred_element_type=F32))

    @pl.when(jnp.logical_not(active))
    def _():
        os_ref[...] = jnp.zeros_like(os_ref)


def _moe(tile_expert, n_active, has_rows, xs, w_gate, w_up, w_down, li, n_tiles):
    eid = jnp.arange(N_EXPERTS, dtype=jnp.int32)
    later = jnp.where(has_rows[None, :] & (eid[None, :] > eid[:, None]), eid[None, :], N_EXPERTS)
    next_of_expert = jnp.min(later, axis=1)
    next_of_expert = jnp.where(next_of_expert == N_EXPERTS, -1, next_of_expert)
    first = jnp.concatenate([jnp.ones((1,), jnp.int32),
                             (tile_expert[1:] != tile_expert[:-1]).astype(jnp.int32)])
    wslot = (jnp.cumsum(first) - 1) % 2
    nxt = jnp.sum(jnp.where(tile_expert[:, None] == eid[None, :], next_of_expert[None, :], 0), axis=1)

    def row_map(i, te, na, fi, ws, nx):
        return (jnp.minimum(i, na[0] - 1), 0)

    return pl.pallas_call(
        functools.partial(_moe_kernel, li=li),
        name="moe_experts",
        grid_spec=pltpu.PrefetchScalarGridSpec(
            num_scalar_prefetch=5,
            grid=(n_tiles,),
            in_specs=[
                pl.BlockSpec((TMX * SUBLANES, LANES), row_map),
                pl.BlockSpec(memory_space=pl.ANY),
                pl.BlockSpec(memory_space=pl.ANY),
                pl.BlockSpec(memory_space=pl.ANY),
            ],
            out_specs=pl.BlockSpec((TMX * SUBLANES, LANES), lambda i, te, na, fi, ws, nx: (i, 0)),
            scratch_shapes=[
                pltpu.VMEM((2, D_MODEL, D_EXPERT), F32),
                pltpu.VMEM((2, D_MODEL, D_EXPERT), F32),
                pltpu.VMEM((2, D_EXPERT, D_MODEL), F32),
                pltpu.VMEM((D_MODEL, D_EXPERT), BF16),
                pltpu.VMEM((D_MODEL, D_EXPERT), BF16),
                pltpu.VMEM((D_EXPERT, D_MODEL), BF16),
                pltpu.SemaphoreType.DMA((2, 3)),
            ],
        ),
        out_shape=jax.ShapeDtypeStruct((n_tiles * TMX * SUBLANES, LANES), F32),
        compiler_params=_cparams("arbitrary"),
    )(tile_expert, n_active, first, wslot.astype(jnp.int32), nxt.astype(jnp.int32), xs, w_gate, w_up, w_down)


def _start_row_gathers(pos_ref, os_ref, buf, sem, tile, slot, n):
    def body(t, carry):
        tok = tile * n + t
        for j in range(2):
            pltpu.make_async_copy(_row_tile(os_ref, pos_ref[tok * 2 + j]), _row_tile(buf.at[slot, j], t),
                                  sem.at[slot, j]).start(priority=j)
        return carry

    lax.fori_loop(0, n, body, 0)


def _combined_tile(pos_ref, x1_ref, route_ref, mod_ref, os_ref, buf, sem, n):
    i = pl.program_id(0)
    slot = i % 2

    @pl.when(i == 0)
    def _():
        _start_row_gathers(pos_ref, os_ref, buf, sem, 0, 0, n)

    @pl.when(i + 1 < pl.num_programs(0))
    def _():
        _start_row_gathers(pos_ref, os_ref, buf, sem, i + 1, 1 - slot, n)

    for j in range(2):
        pltpu.make_async_copy(os_ref.at[pl.ds(0, n * SUBLANES), :], buf.at[slot, j], sem.at[slot, j]).wait()
    w1 = route_ref[:, 2:3]
    w2 = route_ref[:, 3:4]
    cols = []
    for c in range(D_MODEL // LANES):
        y = w1 * _from_token_major(buf.at[slot, 0], n, c) + w2 * _from_token_major(buf.at[slot, 1], n, c)
        g2 = mod_ref[:, 5 * D_MODEL + c * LANES:5 * D_MODEL + (c + 1) * LANES]
        cols.append(x1_ref[:, c * LANES:(c + 1) * LANES] + g2 * y)
    return jnp.concatenate(cols, axis=-1)


def _combine_final_kernel(pos_ref, x1_ref, route_ref, mod_ref, nf_ref, os_ref, o_ref, buf, sem):
    x2 = _combined_tile(pos_ref, x1_ref, route_ref, mod_ref, os_ref, buf, sem, TC)
    ms = jnp.mean(x2 * x2, axis=-1, keepdims=True)
    o_ref[...] = x2 * lax.rsqrt(ms + EPS) * nf_ref[...]


def _gather_scratch(n):
    return [pltpu.VMEM((2, 2, n * SUBLANES, LANES), F32), pltpu.SemaphoreType.DMA((2, 2))]


def _combine_inproj_kernel(pos_ref, x1_ref, route_ref, mod_ref, modn_ref, n1_ref, w_ref, dft_ref, os_ref,
                           x2_ref, yf_ref, qkv_ref, cv_ref, buf, sem):
    x2 = _combined_tile(pos_ref, x1_ref, route_ref, mod_ref, os_ref, buf, sem, TM)
    x2_ref[...] = x2
    _inproj_math(x2, modn_ref, n1_ref, w_ref, dft_ref, yf_ref, qkv_ref, cv_ref)


def _combine_inproj(pos_flat, x1, route, mod4, norm1, w_in_bf, dft_ch, osrt, li):
    def tmap(i, pos):
        return (i, 0)

    return pl.pallas_call(
        _combine_inproj_kernel,
        name="moe_combine_in_projection",
        grid_spec=pltpu.PrefetchScalarGridSpec(
            num_scalar_prefetch=1,
            grid=(T_ALL // TM,),
            in_specs=[
                pl.BlockSpec((TM, D_MODEL), tmap),
                pl.BlockSpec((TM, LANES), tmap),
                pl.BlockSpec((None, None, 1, 6 * D_MODEL), lambda i, pos: (li, _mod_row(i, TM), 0, 0)),
                pl.BlockSpec((None, None, 1, 6 * D_MODEL), lambda i, pos: (li + 1, _mod_row(i, TM), 0, 0)),
                pl.BlockSpec((1, 1, D_MODEL), lambda i, pos: (li + 1, 0, 0)),
                pl.BlockSpec((1, D_MODEL, D_IN_PROJ), lambda i, pos: (li + 1, 0, 0)),
                pl.BlockSpec((D_FOURIER, 2 * D_FOURIER), lambda i, pos: (0, 0)),
                pl.BlockSpec(memory_space=pl.ANY),
            ],
            out_specs=[
                pl.BlockSpec((TM, D_MODEL), tmap),
                pl.BlockSpec((TM, 2 * D_FOURIER), tmap),
                pl.BlockSpec((TM, 3 * D_NA), tmap),
                pl.BlockSpec((TM, 3 * D_CONV), tmap),
            ],
            scratch_shapes=_gather_scratch(TM),
        ),
        out_shape=[
            jax.ShapeDtypeStruct((T_ALL, D_MODEL), F32),
            jax.ShapeDtypeStruct((T_ALL, 2 * D_FOURIER), BF16),
            jax.ShapeDtypeStruct((T_ALL, 3 * D_NA), BF16),
            jax.ShapeDtypeStruct((T_ALL, 3 * D_CONV), F32),
        ],
        compiler_params=_cparams("arbitrary"),
    )(pos_flat, x1, route, mod4, mod4, norm1.reshape(DEPTH, 1, D_MODEL), w_in_bf, dft_ch, osrt)


def _combine_final(pos_flat, x1, route, mod4, norm_final, osrt, li, n_tok):
    def tmap(i, pos):
        return (i, 0)

    return pl.pallas_call(
        _combine_final_kernel,
        name="moe_combine_final",
        grid_spec=pltpu.PrefetchScalarGridSpec(
            num_scalar_prefetch=1,
            grid=(n_tok // TC,),
            in_specs=[
                pl.BlockSpec((TC, D_MODEL), tmap),
                pl.BlockSpec((TC, LANES), tmap),
                pl.BlockSpec((None, None, 1, 6 * D_MODEL), lambda i, pos: (li, _mod_row(i, TC), 0, 0)),
                pl.BlockSpec((1, D_MODEL), lambda i, pos: (0, 0)),
                pl.BlockSpec(memory_space=pl.ANY),
            ],
            out_specs=pl.BlockSpec((TC, D_MODEL), tmap),
            scratch_shapes=_gather_scratch(TC),
        ),
        out_shape=jax.ShapeDtypeStruct((n_tok, D_MODEL), F32),
        compiler_params=_cparams("arbitrary"),
    )(pos_flat, x1, route, mod4, norm_final.reshape(1, D_MODEL), osrt)


def _moe_block(hx_tm, route, w_gate, w_up, w_down, li, n_tok):
    n_tiles = (2 * n_tok) // TMX + N_EXPERTS
    n_rows = n_tiles * TMX
    pos, cnt, ends = _plan(route, n_tok)
    pos = pos[:, 0:2].reshape(-1)
    cnt = cnt[0, :N_EXPERTS].astype(jnp.int32)
    ends = ends[0, :N_EXPERTS].astype(jnp.int32)
    padded = ((cnt + TMX - 1) // TMX) * TMX
    pad_start = ends - padded + cnt
    tile_start = jnp.arange(n_tiles, dtype=jnp.int32) * TMX
    tile_expert = jnp.minimum(jnp.sum((tile_start[:, None] >= ends[None, :]).astype(jnp.int32), axis=1),
                              N_EXPERTS - 1)
    n_active = (ends[-1:] // TMX).astype(jnp.int32)
    tail = jnp.concatenate([ends[-1:], (n_rows - ends[-1:]) // PAD_CHUNKS[0]])

    xs = _dispatch(pos, pad_start, padded - cnt, tail, hx_tm, n_tok, n_rows)
    return pos, _moe(tile_expert, n_active, padded > 0, xs, w_gate, w_up, w_down, li, n_tiles)


def kernel(x, c, ctx, c_ctx, w_ada, b_ada, norm1, norm2, w_in, w_fourier, w_conv, rpb, w_out, w_rg, b_rg,
           w_re, b_re, w_gate, w_up, w_down, norm_final):
    x_pair, ctx_first = (x.reshape(T_LAT, D_MODEL), ctx.reshape(T_CTX, D_MODEL)), 0
    cc =jnp.concatenate([c, c_ctx[None, :], jnp.zeros((MOD_ROWS - BATCH - 1, D_MODEL), F32)], axis=0)
    mod4 = _modulation(cc, w_ada, b_ada).reshape(DEPTH, MOD_ROWS, 1, 6 * D_MODEL)

    w_in_bf = w_in.astype(BF16)
    w_out_bf = w_out.astype(BF16)
    pad = jnp.zeros((DEPTH, D_MODEL, LANES - ROUTE_COLS), F32)
    w_r = jnp.concatenate([w_rg, w_re, pad], axis=-1)
    w_r_hi = w_r.astype(BF16)
    w_r = jnp.concatenate([w_r_hi, (w_r - w_r_hi.astype(F32)).astype(BF16)], axis=-1)
    b_r = jnp.concatenate([b_rg, b_re, pad[:, 0, :]], axis=-1).reshape(DEPTH, 1, LANES)
    dft_ch = jnp.asarray(_channel_dft(), dtype=F32).astype(BF16)
    ctx_blk = T_LAT // CTX_LEN

    for li in range(DEPTH):
        last = li == DEPTH - 1
        if li == 0:
            yf, qkv, cv = _in_projection(*x_pair, ctx_first, mod4, norm1, w_in_bf, dft_ch, li)
        table = _bias_table(rpb[li].reshape(-1))
        ff = _fourier_mix(yf, w_fourier, li, SEQ, 0)
        cvo = _short_conv(cv, w_conv, li, SEQ, 0)
        at = _attn_latent(qkv, table)
        if last:
            n_tok = T_LAT
            ff_c, cvo_c, at_c = ff, cvo, at
        else:
            n_tok = T_ALL
            ff_c = _fourier_mix(yf, w_fourier, li, CTX_LEN, ctx_blk)
            cvo_c = _short_conv(cv, w_conv, li, CTX_LEN, ctx_blk)
            at_c = _attn_ctx(qkv)
        x1, hx_tm, route = _out_projection(x_pair, ctx_first, (ff, ff_c), (at, at_c), (cvo, cvo_c), mod4, norm2,
                                           w_out_bf, w_r, b_r, li, n_tok)
        pos, osrt = _moe_block(hx_tm, route, w_gate, w_up, w_down, li, n_tok)
        if last:
            out = _combine_final(pos, x1, route, mod4, norm_final, osrt, li, n_tok)
            return out.reshape(BATCH, SEQ, D_MODEL)
        xt, yf, qkv, cv = _combine_inproj(pos, x1, route, mod4, norm1, w_in_bf, dft_ch, osrt, li)
        x_pair, ctx_first = (xt, xt), NT_LAT
```

```python
import functools
import math

import numpy as np
import jax
import jax.numpy as jnp
from jax import lax
from jax.experimental import pallas as pl
from jax.experimental.pallas import tpu as pltpu

F32 = jnp.float32
BF16 = jnp.bfloat16

D_MODEL = 1024
BATCH = 8
SEQ = 2048
DEPTH = 2
GRID_W = 64
ROWS = SEQ // GRID_W
CTX_LEN = 256
T_LAT = BATCH * SEQ
T_CTX = BATCH * CTX_LEN
T_ALL = T_LAT + T_CTX

D_FOURIER = 256
D_FG = 64
HEAD_DIM = 64
NA_HEADS = 8
D_NA = NA_HEADS * HEAD_DIM
D_CONV = 256
D_IN_PROJ = D_FOURIER + 3 * D_NA + 3 * D_CONV
NA_KH = 8
NA_KW = 16
N_GROUPS = 4
EXPERTS_PER_GROUP = 8
N_EXPERTS = 32
D_EXPERT = 512
EPS = 1e-6
NEG_INF = -1e30

LANES = 128
SUBLANES = 8
MOD_ROWS = 16
TM = 512
TMX = 512
TD = 512
TC = 256
VMEM_LIMIT = 56 * 1024 * 1024


def _cparams(*sem):
    return pltpu.CompilerParams(dimension_semantics=sem, vmem_limit_bytes=VMEM_LIMIT)


def _ada_kernel(cc_ref, w_ref, b_ref, o_ref):
    s = cc_ref[...]
    s = s * jax.nn.sigmoid(s)
    acc = jnp.dot(s.astype(BF16), w_ref[0].astype(BF16), preferred_element_type=F32)
    o_ref[0] = acc + b_ref[0]


def _modulation(cc, w_ada, b_ada):
    tn = 1536
    nj = 6 * D_MODEL // tn
    return pl.pallas_call(
        _ada_kernel,
        name="ada_modulation",
        grid=(DEPTH, nj),
        in_specs=[
            pl.BlockSpec((MOD_ROWS, D_MODEL), lambda l, j: (0, 0)),
            pl.BlockSpec((1, D_MODEL, tn), lambda l, j: (l, 0, j)),
            pl.BlockSpec((1, 1, tn), lambda l, j: (l, 0, j)),
        ],
        out_specs=pl.BlockSpec((1, MOD_ROWS, tn), lambda l, j: (l, 0, j)),
        out_shape=jax.ShapeDtypeStruct((DEPTH, MOD_ROWS, 6 * D_MODEL), F32),
        compiler_params=_cparams("arbitrary", "arbitrary"),
    )(cc, w_ada, b_ada.reshape(DEPTH, 1, 6 * D_MODEL))


def _mod_row(i, tile):
    return jnp.minimum((i * tile) // SEQ, BATCH)


NT_LAT = T_LAT // TM


def _lat_spec(cols):
    return pl.BlockSpec((TM, cols), lambda i: (jnp.minimum(i, NT_LAT - 1), 0))


def _ctx_spec(cols, first_block):
    return pl.BlockSpec((TM, cols), lambda i: (jnp.maximum(i - NT_LAT, 0) + first_block, 0))


def _pick_stream(lat_ref, ctx_ref):
    return jnp.where(pl.program_id(0) >= NT_LAT, ctx_ref[...], lat_ref[...])


def _inproj_kernel(xa_ref, xb_ref, mod_ref, n1_ref, w_ref, dft_ref, yf_ref, qkv_ref, cv_ref):
    _inproj_math(_pick_stream(xa_ref, xb_ref), mod_ref, n1_ref, w_ref, dft_ref, yf_ref, qkv_ref, cv_ref)


def _inproj_math(x, mod_ref, n1_ref, w_ref, dft_ref, yf_ref, qkv_ref, cv_ref):
    ms = jnp.mean(x * x, axis=-1, keepdims=True)
    y = x * lax.rsqrt(ms + EPS) * n1_ref[0]
    sh = mod_ref[:, 0:D_MODEL]
    sc = mod_ref[:, D_MODEL:2 * D_MODEL]
    h = y * (1.0 + sc) + sh
    u = jnp.dot(h.astype(BF16), w_ref[0], preferred_element_type=F32)
    yf = jnp.dot(u[:, 0:D_FOURIER].astype(BF16), dft_ref[...], preferred_element_type=F32)
    yf_ref[...] = yf.astype(BF16)
    q0 = D_FOURIER
    qkv_ref[:, 0:D_NA] = (u[:, q0:q0 + D_NA] * (1.0 / math.sqrt(HEAD_DIM))).astype(BF16)
    qkv_ref[:, D_NA:3 * D_NA] = u[:, q0 + D_NA:q0 + 3 * D_NA].astype(BF16)
    cv_ref[...] = u[:, q0 + 3 * D_NA:D_IN_PROJ]


def _in_projection(x_lat, x_ctx, ctx_first_block, mod4, norm1, w_in_bf, dft_ch, li):
    nt = T_ALL // TM
    return pl.pallas_call(
        _inproj_kernel,
        name="in_projection",
        grid=(nt,),
        in_specs=[
            _lat_spec(D_MODEL),
            _ctx_spec(D_MODEL, ctx_first_block),
            pl.BlockSpec((None, None, 1, 6 * D_MODEL), lambda i: (li, _mod_row(i, TM), 0, 0)),
            pl.BlockSpec((1, 1, D_MODEL), lambda i: (li, 0, 0)),
            pl.BlockSpec((1, D_MODEL, D_IN_PROJ), lambda i: (li, 0, 0)),
            pl.BlockSpec((D_FOURIER, 2 * D_FOURIER), lambda i: (0, 0)),
        ],
        out_specs=[
            pl.BlockSpec((TM, 2 * D_FOURIER), lambda i: (i, 0)),
            pl.BlockSpec((TM, 3 * D_NA), lambda i: (i, 0)),
            pl.BlockSpec((TM, 3 * D_CONV), lambda i: (i, 0)),
        ],
        out_shape=[
            jax.ShapeDtypeStruct((T_ALL, 2 * D_FOURIER), BF16),
            jax.ShapeDtypeStruct((T_ALL, 3 * D_NA), BF16),
            jax.ShapeDtypeStruct((T_ALL, 3 * D_CONV), F32),
        ],
        compiler_params=_cparams("arbitrary"),
    )(x_lat, x_ctx, mod4, norm1.reshape(DEPTH, 1, D_MODEL), w_in_bf, dft_ch)


def _dft_tables(n):
    k = np.arange(n, dtype=np.int64)
    ang = 2.0 * np.pi * ((k[:, None] * k[None, :]) % n).astype(np.float64) / n
    s = 1.0 / math.sqrt(n)
    return np.cos(ang) * s, np.sin(ang) * s


def _channel_dft():
    c, s = _dft_tables(D_FG)
    eye = np.eye(D_FOURIER // D_FG)
    return np.concatenate([np.kron(eye, c), np.kron(eye, s)], axis=1)


def _fourier_kernel(cn_ref, sn_ref, y_ref, wf_ref, o_ref):
    z = (jnp.dot(cn_ref[...], y_ref[:, 0:D_FOURIER], preferred_element_type=F32)
         - jnp.dot(sn_ref[...], y_ref[:, D_FOURIER:2 * D_FOURIER], preferred_element_type=F32))
    o_ref[...] = jnp.dot(z.astype(BF16), wf_ref[0].astype(BF16), preferred_element_type=F32).astype(BF16)


def _fourier_mix(yf, w_fourier, li, n, first_block):
    tk = min(n, 512)
    nk = n // tk
    cn, sn = _dft_tables(n)
    cn = jnp.asarray(cn, dtype=F32).astype(BF16)
    sn = jnp.asarray(sn, dtype=F32).astype(BF16)
    return pl.pallas_call(
        _fourier_kernel,
        name="fourier_mix",
        grid=(nk, BATCH),
        in_specs=[
            pl.BlockSpec((tk, n), lambda k, b: (k, 0)),
            pl.BlockSpec((tk, n), lambda k, b: (k, 0)),
            pl.BlockSpec((n, 2 * D_FOURIER), lambda k, b: (first_block + b, 0)),
            pl.BlockSpec((1, D_FOURIER, D_FOURIER), lambda k, b: (li, 0, 0)),
        ],
        out_specs=pl.BlockSpec((tk, D_FOURIER), lambda k, b: (b * nk + k, 0)),
        out_shape=jax.ShapeDtypeStruct((BATCH * n, D_FOURIER), BF16),
        compiler_params=_cparams("arbitrary", "arbitrary"),
    )(cn, sn, yf, w_fourier)


def _conv_kernel(cv_ref, wc_ref, o_ref, pad_ref):
    n = cv_ref.shape[0]
    gb = cv_ref[:, 0:D_CONV]
    g = cv_ref[:, D_CONV:2 * D_CONV] * cv_ref[:, 2 * D_CONV:3 * D_CONV]
    zero = jnp.zeros((SUBLANES, D_CONV), F32)
    pad_ref[0:SUBLANES, :] = zero
    pad_ref[n + SUBLANES:n + 2 * SUBLANES, :] = zero
    pad_ref[SUBLANES:n + SUBLANES, :] = g
    prev = pad_ref[SUBLANES - 1:n + SUBLANES - 1, :]
    nxt = pad_ref[SUBLANES + 1:n + SUBLANES + 1, :]
    w = wc_ref[0]
    o_ref[...] = (gb * (w[0:1, :] * prev + w[1:2, :] * g + w[2:3, :] * nxt)).astype(BF16)


def _short_conv(cv, w_conv, li, n, first_block):
    return pl.pallas_call(
        _conv_kernel,
        name="short_conv",
        grid=(BATCH,),
        in_specs=[
            pl.BlockSpec((n, 3 * D_CONV), lambda b: (first_block + b, 0)),
            pl.BlockSpec((1, 3, D_CONV), lambda b: (li, 0, 0)),
        ],
        out_specs=pl.BlockSpec((n, D_CONV), lambda b: (b, 0)),
        out_shape=jax.ShapeDtypeStruct((BATCH * n, D_CONV), BF16),
        scratch_shapes=[pltpu.VMEM((n + 2 * SUBLANES, D_CONV), F32)],
        compiler_params=_cparams("arbitrary"),
    )(cv, w_conv)


RPB_H = 2 * NA_KH - 1
RPB_W = 2 * NA_KW - 1
KEYS_LOC = NA_KH * GRID_W


def _bias_kernel(rpb_ref, o_ref):
    h = pl.program_id(0)
    qi = lax.broadcasted_iota(jnp.int32, (GRID_W, LANES), 0)
    lj = lax.broadcasted_iota(jnp.int32, (GRID_W, LANES), 1)
    kc = lj & (GRID_W - 1)
    hi = lj >= GRID_W
    d = kc - qi + (NA_KW - 1)
    cs = jnp.clip(qi - NA_KW // 2, 0, GRID_W - NA_KW)
    valid = (kc >= cs) & (kc < cs + NA_KW)
    tiles = []
    for a in range(RPB_H - 1):
        acc = jnp.zeros((GRID_W, LANES), F32)
        for b in range(RPB_W):
            va = rpb_ref[(h * RPB_H + a) * RPB_W + b]
            vb = rpb_ref[(h * RPB_H + a + 1) * RPB_W + b]
            acc = jnp.where(d == b, jnp.where(hi, vb, va), acc)
        tiles.append(jnp.where(valid, acc, NEG_INF))
    for cls in range(NA_KH):
        for m in range(NA_KH // 2):
            o_ref[0, cls, :, m * LANES:(m + 1) * LANES] = tiles[2 * m - cls + NA_KH - 1]


def _bias_table(rpb_flat):
    return pl.pallas_call(
        _bias_kernel,
        name="attn_bias_table",
        grid=(NA_HEADS,),
        in_specs=[pl.BlockSpec(memory_space=pltpu.SMEM)],
        out_specs=pl.BlockSpec((1, NA_KH, GRID_W, KEYS_LOC), lambda h: (h // 2, 0, h % 2, 0)),
        out_shape=jax.ShapeDtypeStruct((NA_HEADS // 2, NA_KH, 2 * GRID_W, KEYS_LOC), F32),
        compiler_params=_cparams("arbitrary"),
    )(rpb_flat)


_NT_DIMS = (((1,), (1,)), ((), ()))


def _softmax_pv(parts):
    m = None
    for s, _ in parts:
        ms = jnp.max(s, axis=1, keepdims=True)
        m = ms if m is None else jnp.maximum(m, ms)
    l = None
    o = None
    for s, v in parts:
        p = jnp.exp(s - m)
        ls = jnp.sum(p, axis=1, keepdims=True)
        os_ = jnp.dot(p.astype(BF16), v, preferred_element_type=F32)
        l = ls if l is None else l + ls
        o = os_ if o is None else o + os_
    return o / l


def _attn_latent_kernel(q_ref, k_ref, v_ref, kc_ref, vc_ref, tab_ref, o_ref, ve_ref, vce_ref):
    lane = lax.broadcasted_iota(jnp.int32, (GRID_W, LANES), 1)
    lo = lane < HEAD_DIM
    ve_ref[:, 0:LANES] = v_ref[...]
    ve_ref[:, LANES:2 * LANES] = jnp.ones((SEQ, LANES), BF16)
    vce_ref[:, 0:LANES] = vc_ref[...]
    vce_ref[:, LANES:2 * LANES] = jnp.ones((CTX_LEN, LANES), BF16)
    kc = kc_ref[...]
    vce = vce_ref[...]

    for r in range(ROWS):
        rs = min(max(r - NA_KH // 2, 0), ROWS - NA_KH)
        cls = r - rs
        q = q_ref[r * GRID_W:(r + 1) * GRID_W, :]
        kl = k_ref[rs * GRID_W:rs * GRID_W + KEYS_LOC, :]
        vl = ve_ref[rs * GRID_W:rs * GRID_W + KEYS_LOC, :]
        zero = jnp.zeros_like(q)
        q2 = jnp.concatenate([jnp.where(lo, q, zero), jnp.where(lo, zero, q)], axis=0)
        s1 = lax.dot_general(q2, kl, _NT_DIMS, preferred_element_type=F32) + tab_ref[0, cls]
        s2 = lax.dot_general(q2, kc, _NT_DIMS, preferred_element_type=F32)
        m = jnp.maximum(jnp.max(s1, axis=1, keepdims=True), jnp.max(s2, axis=1, keepdims=True))
        p1 = jnp.exp(s1 - m).astype(BF16)
        p2 = jnp.exp(s2 - m).astype(BF16)
        oe = (jnp.dot(p1, vl, preferred_element_type=F32) + jnp.dot(p2, vce, preferred_element_type=F32))
        o = oe[:, 0:LANES] / oe[:, LANES:2 * LANES]
        o_ref[r * GRID_W:(r + 1) * GRID_W, :] = jnp.where(lo, o[0:GRID_W], o[GRID_W:2 * GRID_W]).astype(BF16)


def _attn_latent(qkv, table):
    npair = NA_HEADS // 2
    cblk = T_LAT // CTX_LEN
    return pl.pallas_call(
        _attn_latent_kernel,
        name="attn_latent",
        grid=(BATCH, npair),
        in_specs=[
            pl.BlockSpec((SEQ, LANES), lambda b, p: (b, p)),
            pl.BlockSpec((SEQ, LANES), lambda b, p: (b, npair + p)),
            pl.BlockSpec((SEQ, LANES), lambda b, p: (b, 2 * npair + p)),
            pl.BlockSpec((CTX_LEN, LANES), lambda b, p: (cblk + b, npair + p)),
            pl.BlockSpec((CTX_LEN, LANES), lambda b, p: (cblk + b, 2 * npair + p)),
            pl.BlockSpec((1, NA_KH, 2 * GRID_W, KEYS_LOC), lambda b, p: (p, 0, 0, 0)),
        ],
        out_specs=pl.BlockSpec((SEQ, LANES), lambda b, p: (b, p)),
        out_shape=jax.ShapeDtypeStruct((T_LAT, D_NA), BF16),
        scratch_shapes=[pltpu.VMEM((SEQ, 2 * LANES), BF16), pltpu.VMEM((CTX_LEN, 2 * LANES), BF16)],
        compiler_params=_cparams("arbitrary", "arbitrary"),
    )(qkv, qkv, qkv, qkv, qkv, table)


def _attn_ctx_kernel(q_ref, k_ref, v_ref, o_ref):
    lane = lax.broadcasted_iota(jnp.int32, (CTX_LEN, LANES), 1)
    lo = lane < HEAD_DIM
    q = q_ref[...]
    k = k_ref[...]
    v = v_ref[...]
    outs = []
    for hh in range(2):
        qm = jnp.where(lo if hh == 0 else jnp.logical_not(lo), q, jnp.zeros_like(q))
        s = lax.dot_general(qm, k, _NT_DIMS, preferred_element_type=F32)
        outs.append(_softmax_pv([(s, v)]))
    o_ref[...] = jnp.where(lo, outs[0], outs[1]).astype(BF16)


def _attn_ctx(qkv):
    npair = NA_HEADS // 2
    cblk = T_LAT // CTX_LEN
    return pl.pallas_call(
        _attn_ctx_kernel,
        name="attn_context",
        grid=(BATCH, npair),
        in_specs=[
            pl.BlockSpec((CTX_LEN, LANES), lambda b, p: (cblk + b, p)),
            pl.BlockSpec((CTX_LEN, LANES), lambda b, p: (cblk + b, npair + p)),
            pl.BlockSpec((CTX_LEN, LANES), lambda b, p: (cblk + b, 2 * npair + p)),
        ],
        out_specs=pl.BlockSpec((CTX_LEN, LANES), lambda b, p: (b, p)),
        out_shape=jax.ShapeDtypeStruct((T_CTX, D_NA), BF16),
        compiler_params=_cparams("arbitrary", "arbitrary"),
    )(qkv, qkv, qkv)


ROUTE_COLS = N_GROUPS + N_EXPERTS


def _to_token_major(ref, val):
    rows = val.shape[0]
    for j in range(D_MODEL // LANES):
        ref[pl.ds(j, rows, stride=SUBLANES), :] = val[:, j * LANES:(j + 1) * LANES]


def _from_token_major(ref, rows, j):
    return ref[pl.ds(j, rows, stride=SUBLANES), :]


def _outproj_kernel(xa_ref, xb_ref, ffa_ref, ffb_ref, ata_ref, atb_ref, cva_ref, cvb_ref,
                    mod_ref, n2_ref, wo_ref, wr_ref, br_ref, x1_ref, hx_ref, route_ref):
    mix_in = jnp.concatenate([_pick_stream(ffa_ref, ffb_ref), _pick_stream(ata_ref, atb_ref),
                              _pick_stream(cva_ref, cvb_ref)], axis=-1)
    mix = jnp.dot(mix_in, wo_ref[0], preferred_element_type=F32)
    g1 = mod_ref[:, 2 * D_MODEL:3 * D_MODEL]
    x1 = _pick_stream(xa_ref, xb_ref) + g1 * mix
    x1_ref[...] = x1
    ms = jnp.mean(x1 * x1, axis=-1, keepdims=True)
    y = x1 * lax.rsqrt(ms + EPS) * n2_ref[0]
    sh2 = mod_ref[:, 3 * D_MODEL:4 * D_MODEL]
    sc2 = mod_ref[:, 4 * D_MODEL:5 * D_MODEL]
    hx = y * (1.0 + sc2) + sh2
    _to_token_major(hx_ref, hx)

    hx_hi = hx.astype(BF16)
    hx_lo = (hx - hx_hi.astype(F32)).astype(BF16)
    part = (jnp.dot(hx_hi, wr_ref[0], preferred_element_type=F32)
            + jnp.dot(hx_lo, wr_ref[0], preferred_element_type=F32))
    logits = part[:, 0:LANES] + part[:, LANES:2 * LANES] + br_ref[0]
    tm = logits.shape[0]
    lane = lax.broadcasted_iota(jnp.int32, (tm, LANES), 1)
    big = jnp.int32(LANES)
    gl = jnp.where(lane < N_GROUPS, logits, -jnp.inf)
    gmax = jnp.max(gl, axis=1, keepdims=True)
    gidx = jnp.min(jnp.where(gl == gmax, lane, big), axis=1, keepdims=True)
    g_w = 1.0 / jnp.sum(jnp.exp(gl - gmax), axis=1, keepdims=True)
    e_lane = lane - N_GROUPS
    in_group = (e_lane >= 0) & (e_lane < N_EXPERTS) & ((e_lane >> 3) == gidx)
    es = jnp.where(in_group, logits, -jnp.inf)
    t1 = jnp.max(es, axis=1, keepdims=True)
    i1 = jnp.min(jnp.where(es == t1, lane, big), axis=1, keepdims=True)
    es2 = jnp.where(lane == i1, -jnp.inf, es)
    t2 = jnp.max(es2, axis=1, keepdims=True)
    i2 = jnp.min(jnp.where(es2 == t2, lane, big), axis=1, keepdims=True)
    dlt = jnp.exp(t2 - t1)
    w1 = g_w / (1.0 + dlt)
    w2 = g_w * dlt / (1.0 + dlt)
    e1 = (i1 - N_GROUPS).astype(F32)
    e2 = (i2 - N_GROUPS).astype(F32)
    route_ref[...] = jnp.where(lane == 0, e1, jnp.where(lane == 1, e2,
                               jnp.where(lane == 2, w1, jnp.where(lane == 3, w2, 0.0))))


def _out_projection(x_pair, ctx_first_block, ff_pair, at_pair, cv_pair, mod4, norm2, w_out_bf, w_r, b_r, li, n_tok):
    nt = n_tok // TM
    return pl.pallas_call(
        _outproj_kernel,
        name="out_projection",
        grid=(nt,),
        in_specs=[
            _lat_spec(D_MODEL), _ctx_spec(D_MODEL, ctx_first_block),
            _lat_spec(D_FOURIER), _ctx_spec(D_FOURIER, 0),
            _lat_spec(D_NA), _ctx_spec(D_NA, 0),
            _lat_spec(D_CONV), _ctx_spec(D_CONV, 0),
            pl.BlockSpec((None, None, 1, 6 * D_MODEL), lambda i: (li, _mod_row(i, TM), 0, 0)),
            pl.BlockSpec((1, 1, D_MODEL), lambda i: (li, 0, 0)),
            pl.BlockSpec((1, D_MODEL, D_MODEL), lambda i: (li, 0, 0)),
            pl.BlockSpec((1, D_MODEL, 2 * LANES), lambda i: (li, 0, 0)),
            pl.BlockSpec((1, 1, LANES), lambda i: (li, 0, 0)),
        ],
        out_specs=[
            pl.BlockSpec((TM, D_MODEL), lambda i: (i, 0)),
            pl.BlockSpec((TM * SUBLANES, LANES), lambda i: (i, 0)),
            pl.BlockSpec((TM, LANES), lambda i: (i, 0)),
        ],
        out_shape=[
            jax.ShapeDtypeStruct((n_tok, D_MODEL), F32),
            jax.ShapeDtypeStruct((n_tok * SUBLANES, LANES), F32),
            jax.ShapeDtypeStruct((n_tok, LANES), F32),
        ],
        compiler_params=_cparams("arbitrary"),
    )(*x_pair, *ff_pair, *at_pair, *cv_pair, mod4, norm2.reshape(DEPTH, 1, D_MODEL), w_out_bf, w_r, b_r)


def _plan_kernel(route_ref, pos_ref, cnt_ref, ends_ref, carry_ref, offs_ref):
    ph = pl.program_id(0)
    i = pl.program_id(1)
    tm = PLAN_SUB
    lane = lax.broadcasted_iota(jnp.int32, (tm, LANES), 1)

    def one_hots(k):
        r = route_ref[k * tm:(k + 1) * tm, :]
        oh1 = lane == r[:, 0:1].astype(jnp.int32)
        oh2 = lane == r[:, 1:2].astype(jnp.int32)
        return oh1, oh2, jnp.where(oh1 | oh2, 1.0, 0.0)

    @pl.when((ph == 0) & (i == 0))
    def _():
        carry_ref[...] = jnp.zeros_like(carry_ref)

    @pl.when(ph == 0)
    def _():
        total = carry_ref[...]
        for k in range(PLAN_TM // tm):
            total = total + jnp.sum(one_hots(k)[2], axis=0, keepdims=True)
        carry_ref[...] = total

    @pl.when((ph == 1) & (i == 0))
    def _():
        cnt = carry_ref[...]
        cnt_ref[...] = jnp.broadcast_to(cnt, cnt_ref.shape)
        tiles = jnp.ceil(cnt * (1.0 / TMX))
        a = lax.broadcasted_iota(jnp.int32, (LANES, LANES), 0)
        b = lax.broadcasted_iota(jnp.int32, (LANES, LANES), 1)
        upper = jnp.where(a < b, 1.0, 0.0).astype(BF16)
        tiles8 = jnp.broadcast_to(tiles, (SUBLANES, LANES))
        first = jnp.dot(tiles8.astype(BF16), upper, preferred_element_type=F32)
        offs_ref[...] = first[0:1, :] * TMX
        ends_ref[...] = (first + tiles8) * TMX
        carry_ref[...] = jnp.zeros_like(carry_ref)

    @pl.when(ph == 1)
    def _():
        row = lax.broadcasted_iota(jnp.int32, (tm, tm), 0)
        col = lax.broadcasted_iota(jnp.int32, (tm, tm), 1)
        tri = jnp.where(row > col, 1.0, 0.0).astype(BF16)
        base = carry_ref[...] + offs_ref[...]
        for k in range(PLAN_TM // tm):
            oh1, oh2, oh = one_hots(k)
            cum = jnp.dot(tri, oh.astype(BF16), preferred_element_type=F32) + base
            p1 = jnp.sum(jnp.where(oh1, cum, 0.0), axis=1, keepdims=True)
            p2 = jnp.sum(jnp.where(oh2, cum, 0.0), axis=1, keepdims=True)
            pos_ref[k * tm:(k + 1) * tm, :] = (
                jnp.where(lane == 0, p1, jnp.where(lane == 1, p2, 0.0)).astype(jnp.int32))
            base = base + jnp.sum(oh, axis=0, keepdims=True)
        carry_ref[...] = base - offs_ref[...]


PLAN_TM = 2048
PLAN_SUB = 512


def _plan(route, n_tok):
    nt = n_tok // PLAN_TM
    return pl.pallas_call(
        _plan_kernel,
        name="moe_plan",
        grid=(2, nt),
        in_specs=[pl.BlockSpec((PLAN_TM, LANES), lambda ph, i: (i, 0))],
        out_specs=[
            pl.BlockSpec((PLAN_TM, LANES), lambda ph, i: (i * ph, 0)),
            pl.BlockSpec((SUBLANES, LANES), lambda ph, i: (0, 0)),
            pl.BlockSpec((SUBLANES, LANES), lambda ph, i: (0, 0)),
        ],
        out_shape=[
            jax.ShapeDtypeStruct((n_tok, LANES), jnp.int32),
            jax.ShapeDtypeStruct((SUBLANES, LANES), F32),
            jax.ShapeDtypeStruct((SUBLANES, LANES), F32),
        ],
        scratch_shapes=[pltpu.VMEM((1, LANES), F32), pltpu.VMEM((1, LANES), F32)],
        compiler_params=_cparams("arbitrary", "arbitrary"),
    )(route)


def _row_tile(ref, row):
    return ref.at[pl.ds(pl.multiple_of(row * SUBLANES, SUBLANES), SUBLANES), :]


DISPATCH_UNROLL = 4
PAD_CHUNKS = tuple(1 << b for b in reversed(range(TMX.bit_length() - 1)))


def _dispatch_kernel(pos_ref, pstart_ref, plen_ref, tail_ref, hx_ref, xs_ref, zbuf, sem, zsem):
    i = pl.program_id(0)

    def pad_copies(fn):
        def ebody(e, carry):
            n = plen_ref[e]
            off = pstart_ref[e]
            for rows in PAD_CHUNKS:
                @pl.when((n & rows) != 0)
                def _():
                    fn(pltpu.make_async_copy(
                        zbuf.at[pl.ds(0, rows * SUBLANES), :],
                        xs_ref.at[pl.ds(pl.multiple_of(off * SUBLANES, SUBLANES), rows * SUBLANES), :], zsem.at[0]))
                off = off + (n & rows)
            return carry

        lax.fori_loop(0, N_EXPERTS, ebody, 0)

        def tbody(k, carry):
            row = pl.multiple_of((tail_ref[0] + k * PAD_CHUNKS[0]) * SUBLANES, SUBLANES)
            fn(pltpu.make_async_copy(zbuf, xs_ref.at[pl.ds(row, PAD_CHUNKS[0] * SUBLANES), :], zsem.at[0]))
            return carry

        lax.fori_loop(0, tail_ref[1], tbody, 0)

    @pl.when(i == 0)
    def _():
        zbuf[...] = jnp.zeros_like(zbuf)
        pad_copies(lambda cp: cp.start())

    def body(k, carry):
        for u in range(DISPATCH_UNROLL):
            t = k * DISPATCH_UNROLL + u
            src = _row_tile(hx_ref, t)
            for j in range(2):
                pltpu.make_async_copy(src, _row_tile(xs_ref, pos_ref[(i * TD + t) * 2 + j]),
                                      sem.at[0]).start(priority=j)
        return carry

    lax.fori_loop(0, TD // DISPATCH_UNROLL, body, 0)
    nrow = TD * SUBLANES
    for j in range(2):
        pltpu.make_async_copy(hx_ref, xs_ref.at[pl.ds(0, nrow), :], sem.at[0]).wait()

    @pl.when(i == 0)
    def _():
        pad_copies(lambda cp: cp.wait())


def _dispatch(pos_flat, pad_start, pad_len, tail, hx_tm, n_tok, n_rows):
    return pl.pallas_call(
        _dispatch_kernel,
        name="moe_dispatch",
        grid_spec=pltpu.PrefetchScalarGridSpec(
            num_scalar_prefetch=4,
            grid=(n_tok // TD,),
            in_specs=[pl.BlockSpec((TD * SUBLANES, LANES), lambda i, pos, ps, pn, tl: (i, 0))],
            out_specs=pl.BlockSpec(memory_space=pl.ANY),
            scratch_shapes=[pltpu.VMEM((PAD_CHUNKS[0] * SUBLANES, LANES), F32),
                            pltpu.SemaphoreType.DMA((1,)), pltpu.SemaphoreType.DMA((1,))],
        ),
        out_shape=jax.ShapeDtypeStruct((n_rows * SUBLANES, LANES), F32),
        compiler_params=_cparams("arbitrary"),
    )(pos_flat, pad_start, pad_len, tail, hx_tm)


def _moe_kernel(te_ref, na_ref, first_ref, wslot_ref, nxt_ref, xs_ref, wg_hbm, wu_hbm, wd_hbm, os_ref,
                wg_f, wu_f, wd_f, wg_s, wu_s, wd_s, wsem, *, li):
    i = pl.program_id(0)
    active = i < na_ref[0]

    def weight_copies(e, slot):
        return [pltpu.make_async_copy(wg_hbm.at[li, e], wg_f.at[slot], wsem.at[slot, 0]),
                pltpu.make_async_copy(wu_hbm.at[li, e], wu_f.at[slot], wsem.at[slot, 1]),
                pltpu.make_async_copy(wd_hbm.at[li, e], wd_f.at[slot], wsem.at[slot, 2])]

    @pl.when(active & (first_ref[i] == 1))
    def _():
        slot = wslot_ref[i]

        @pl.when(i == 0)
        def _():
            for cp in weight_copies(te_ref[0], 0):
                cp.start()

        @pl.when(nxt_ref[i] >= 0)
        def _():
            for cp in weight_copies(nxt_ref[i], 1 - slot):
                cp.start()

        for cp in weight_copies(te_ref[i], slot):
            cp.wait()
        wg_s[...] = wg_f[slot].astype(BF16)
        wu_s[...] = wu_f[slot].astype(BF16)
        wd_s[...] = wd_f[slot].astype(BF16)

    @pl.when(active)
    def _():
        x = jnp.concatenate([_from_token_major(xs_ref, TMX, j).astype(BF16)
                             for j in range(D_MODEL // LANES)], axis=-1)
        g = jnp.dot(x, wg_s[...], preferred_element_type=F32)
        u = jnp.dot(x, wu_s[...], preferred_element_type=F32)
        h = (g * jax.nn.sigmoid(g) * u).astype(BF16)
        _to_token_major(os_ref, jnp.dot(h, wd_s[...], preferred_element_type=F32))

    @pl.when(jnp.logical_not(active))
    def _():
        os_ref[...] = jnp.zeros_like(os_ref)


def _moe(tile_expert, n_active, has_rows, xs, w_gate, w_up, w_down, li, n_tiles):
    eid = jnp.arange(N_EXPERTS, dtype=jnp.int32)
    later = jnp.where(has_rows[None, :] & (eid[None, :] > eid[:, None]), eid[None, :], N_EXPERTS)
    next_of_expert = jnp.min(later, axis=1)
    next_of_expert = jnp.where(next_of_expert == N_EXPERTS, -1, next_of_expert)
    first = jnp.concatenate([jnp.ones((1,), jnp.int32),
                             (tile_expert[1:] != tile_expert[:-1]).astype(jnp.int32)])
    wslot = (jnp.cumsum(first) - 1) % 2
    nxt = jnp.sum(jnp.where(tile_expert[:, None] == eid[None, :], next_of_expert[None, :], 0), axis=1)

    def row_map(i, te, na, fi, ws, nx):
        return (jnp.minimum(i, na[0] - 1), 0)

    return pl.pallas_call(
        functools.partial(_moe_kernel, li=li),
        name="moe_experts",
        grid_spec=pltpu.PrefetchScalarGridSpec(
            num_scalar_prefetch=5,
            grid=(n_tiles,),
            in_specs=[
                pl.BlockSpec((TMX * SUBLANES, LANES), row_map),
                pl.BlockSpec(memory_space=pl.ANY),
                pl.BlockSpec(memory_space=pl.ANY),
                pl.BlockSpec(memory_space=pl.ANY),
            ],
            out_specs=pl.BlockSpec((TMX * SUBLANES, LANES), lambda i, te, na, fi, ws, nx: (i, 0)),
            scratch_shapes=[
                pltpu.VMEM((2, D_MODEL, D_EXPERT), F32),
                pltpu.VMEM((2, D_MODEL, D_EXPERT), F32),
                pltpu.VMEM((2, D_EXPERT, D_MODEL), F32),
                pltpu.VMEM((D_MODEL, D_EXPERT), BF16),
                pltpu.VMEM((D_MODEL, D_EXPERT), BF16),
                pltpu.VMEM((D_EXPERT, D_MODEL), BF16),
                pltpu.SemaphoreType.DMA((2, 3)),
            ],
        ),
        out_shape=jax.ShapeDtypeStruct((n_tiles * TMX * SUBLANES, LANES), F32),
        compiler_params=_cparams("arbitrary"),
    )(tile_expert, n_active, first, wslot.astype(jnp.int32), nxt.astype(jnp.int32), xs, w_gate, w_up, w_down)


def _start_row_gathers(pos_ref, os_ref, buf, sem, tile, slot, n):
    def body(t, carry):
        tok = tile * n + t
        for j in range(2):
            pltpu.make_async_copy(_row_tile(os_ref, pos_ref[tok * 2 + j]), _row_tile(buf.at[slot, j], t),
                                  sem.at[slot, j]).start(priority=j)
        return carry

    lax.fori_loop(0, n, body, 0)


def _combined_tile(pos_ref, x1_ref, route_ref, mod_ref, os_ref, buf, sem, n):
    i = pl.program_id(0)
    slot = i % 2

    @pl.when(i == 0)
    def _():
        _start_row_gathers(pos_ref, os_ref, buf, sem, 0, 0, n)

    @pl.when(i + 1 < pl.num_programs(0))
    def _():
        _start_row_gathers(pos_ref, os_ref, buf, sem, i + 1, 1 - slot, n)

    for j in range(2):
        pltpu.make_async_copy(os_ref.at[pl.ds(0, n * SUBLANES), :], buf.at[slot, j], sem.at[slot, j]).wait()
    w1 = route_ref[:, 2:3]
    w2 = route_ref[:, 3:4]
    cols = []
    for c in range(D_MODEL // LANES):
        y = w1 * _from_token_major(buf.at[slot, 0], n, c) + w2 * _from_token_major(buf.at[slot, 1], n, c)
        g2 = mod_ref[:, 5 * D_MODEL + c * LANES:5 * D_MODEL + (c + 1) * LANES]
        cols.append(x1_ref[:, c * LANES:(c + 1) * LANES] + g2 * y)
    return jnp.concatenate(cols, axis=-1)


def _combine_final_kernel(pos_ref, x1_ref, route_ref, mod_ref, nf_ref, os_ref, o_ref, buf, sem):
    x2 = _combined_tile(pos_ref, x1_ref, route_ref, mod_ref, os_ref, buf, sem, TC)
    ms = jnp.mean(x2 * x2, axis=-1, keepdims=True)
    o_ref[...] = x2 * lax.rsqrt(ms + EPS) * nf_ref[...]


def _gather_scratch(n):
    return [pltpu.VMEM((2, 2, n * SUBLANES, LANES), F32), pltpu.SemaphoreType.DMA((2, 2))]


def _combine_inproj_kernel(pos_ref, x1_ref, route_ref, mod_ref, modn_ref, n1_ref, w_ref, dft_ref, os_ref,
                           x2_ref, yf_ref, qkv_ref, cv_ref, buf, sem):
    x2 = _combined_tile(pos_ref, x1_ref, route_ref, mod_ref, os_ref, buf, sem, TM)
    x2_ref[...] = x2
    _inproj_math(x2, modn_ref, n1_ref, w_ref, dft_ref, yf_ref, qkv_ref, cv_ref)


def _combine_inproj(pos_flat, x1, route, mod4, norm1, w_in_bf, dft_ch, osrt, li):
    def tmap(i, pos):
        return (i, 0)

    return pl.pallas_call(
        _combine_inproj_kernel,
        name="moe_combine_in_projection",
        grid_spec=pltpu.PrefetchScalarGridSpec(
            num_scalar_prefetch=1,
            grid=(T_ALL // TM,),
            in_specs=[
                pl.BlockSpec((TM, D_MODEL), tmap),
                pl.BlockSpec((TM, LANES), tmap),
                pl.BlockSpec((None, None, 1, 6 * D_MODEL), lambda i, pos: (li, _mod_row(i, TM), 0, 0)),
                pl.BlockSpec((None, None, 1, 6 * D_MODEL), lambda i, pos: (li + 1, _mod_row(i, TM), 0, 0)),
                pl.BlockSpec((1, 1, D_MODEL), lambda i, pos: (li + 1, 0, 0)),
                pl.BlockSpec((1, D_MODEL, D_IN_PROJ), lambda i, pos: (li + 1, 0, 0)),
                pl.BlockSpec((D_FOURIER, 2 * D_FOURIER), lambda i, pos: (0, 0)),
                pl.BlockSpec(memory_space=pl.ANY),
            ],
            out_specs=[
                pl.BlockSpec((TM, D_MODEL), tmap),
                pl.BlockSpec((TM, 2 * D_FOURIER), tmap),
                pl.BlockSpec((TM, 3 * D_NA), tmap),
                pl.BlockSpec((TM, 3 * D_CONV), tmap),
            ],
            scratch_shapes=_gather_scratch(TM),
        ),
        out_shape=[
            jax.ShapeDtypeStruct((T_ALL, D_MODEL), F32),
            jax.ShapeDtypeStruct((T_ALL, 2 * D_FOURIER), BF16),
            jax.ShapeDtypeStruct((T_ALL, 3 * D_NA), BF16),
            jax.ShapeDtypeStruct((T_ALL, 3 * D_CONV), F32),
        ],
        compiler_params=_cparams("arbitrary"),
    )(pos_flat, x1, route, mod4, mod4, norm1.reshape(DEPTH, 1, D_MODEL), w_in_bf, dft_ch, osrt)


def _combine_final(pos_flat, x1, route, mod4, norm_final, osrt, li, n_tok):
    def tmap(i, pos):
        return (i, 0)

    return pl.pallas_call(
        _combine_final_kernel,
        name="moe_combine_final",
        grid_spec=pltpu.PrefetchScalarGridSpec(
            num_scalar_prefetch=1,
            grid=(n_tok // TC,),
            in_specs=[
                pl.BlockSpec((TC, D_MODEL), tmap),
                pl.BlockSpec((TC, LANES), tmap),
                pl.BlockSpec((None, None, 1, 6 * D_MODEL), lambda i, pos: (li, _mod_row(i, TC), 0, 0)),
                pl.BlockSpec((1, D_MODEL), lambda i, pos: (0, 0)),
                pl.BlockSpec(memory_space=pl.ANY),
            ],
            out_specs=pl.BlockSpec((TC, D_MODEL), tmap),
            scratch_shapes=_gather_scratch(TC),
        ),
        out_shape=jax.ShapeDtypeStruct((n_tok, D_MODEL), F32),
        compiler_params=_cparams("arbitrary"),
    )(pos_flat, x1, route, mod4, norm_final.reshape(1, D_MODEL), osrt)


def _moe_block(hx_tm, route, w_gate, w_up, w_down, li, n_tok):
    n_tiles = (2 * n_tok) // TMX + N_EXPERTS
    n_rows = n_tiles * TMX
    pos, cnt, ends = _plan(route, n_tok)
    pos = pos[:, 0:2].reshape(-1)
    cnt = cnt[0, :N_EXPERTS].astype(jnp.int32)
    ends = ends[0, :N_EXPERTS].astype(jnp.int32)
    padded = ((cnt + TMX - 1) // TMX) * TMX
    pad_start = ends - padded + cnt
    tile_start = jnp.arange(n_tiles, dtype=jnp.int32) * TMX
    tile_expert = jnp.minimum(jnp.sum((tile_start[:, None] >= ends[None, :]).astype(jnp.int32), axis=1),
                              N_EXPERTS - 1)
    n_active = (ends[-1:] // TMX).astype(jnp.int32)
    tail = jnp.concatenate([ends[-1:], (n_rows - ends[-1:]) // PAD_CHUNKS[0]])

    xs = _dispatch(pos, pad_start, padded - cnt, tail, hx_tm, n_tok, n_rows)
    return pos, _moe(tile_expert, n_active, padded > 0, xs, w_gate, w_up, w_down, li, n_tiles)


def kernel(x, c, ctx, c_ctx, w_ada, b_ada, norm1, norm2, w_in, w_fourier, w_conv, rpb, w_out, w_rg, b_rg,
           w_re, b_re, w_gate, w_up, w_down, norm_final):
    x_pair, ctx_first = (x.reshape(T_LAT, D_MODEL), ctx.reshape(T_CTX, D_MODEL)), 0
    cc =jnp.concatenate([c, c_ctx[None, :], jnp.zeros((MOD_ROWS - BATCH - 1, D_MODEL), F32)], axis=0)
    mod4 = _modulation(cc, w_ada, b_ada).reshape(DEPTH, MOD_ROWS, 1, 6 * D_MODEL)

    w_in_bf = w_in.astype(BF16)
    w_out_bf = w_out.astype(BF16)
    pad = jnp.zeros((DEPTH, D_MODEL, LANES - ROUTE_COLS), F32)
    w_r = jnp.concatenate([w_rg, w_re, pad], axis=-1)
    w_r_hi = w_r.astype(BF16)
    w_r = jnp.concatenate([w_r_hi, (w_r - w_r_hi.astype(F32)).astype(BF16)], axis=-1)
    b_r = jnp.concatenate([b_rg, b_re, pad[:, 0, :]], axis=-1).reshape(DEPTH, 1, LANES)
    dft_ch = jnp.asarray(_channel_dft(), dtype=F32).astype(BF16)
    ctx_blk = T_LAT // CTX_LEN

    for li in range(DEPTH):
        last = li == DEPTH - 1
        if li == 0:
            yf, qkv, cv = _in_projection(*x_pair, ctx_first, mod4, norm1, w_in_bf, dft_ch, li)
        table = _bias_table(rpb[li].reshape(-1))
        ff = _fourier_mix(yf, w_fourier, li, SEQ, 0)
        cvo = _short_conv(cv, w_conv, li, SEQ, 0)
        at = _attn_latent(qkv, table)
        if last:
            n_tok = T_LAT
            ff_c, cvo_c, at_c = ff, cvo, at
        else:
            n_tok = T_ALL
            ff_c = _fourier_mix(yf, w_fourier, li, CTX_LEN, ctx_blk)
            cvo_c = _short_conv(cv, w_conv, li, CTX_LEN, ctx_blk)
            at_c = _attn_ctx(qkv)
        x1, hx_tm, route = _out_projection(x_pair, ctx_first, (ff, ff_c), (at, at_c), (cvo, cvo_c), mod4, norm2,
                                           w_out_bf, w_r, b_r, li, n_tok)
        pos, osrt = _moe_block(hx_tm, route, w_gate, w_up, w_down, li, n_tok)
        if last:
            out = _combine_final(pos, x1, route, mod4, norm_final, osrt, li, n_tok)
            return out.reshape(BATCH, SEQ, D_MODEL)
        xt, yf, qkv, cv = _combine_inproj(pos, x1, route, mod4, norm1, w_in_bf, dft_ch, osrt, li)
        x_pair, ctx_first = (xt, xt), NT_LAT
```

```python
import functools
import math

import numpy as np
import jax
import jax.numpy as jnp
from jax import lax
from jax.experimental import pallas as pl
from jax.experimental.pallas import tpu as pltpu

F32 = jnp.float32
BF16 = jnp.bfloat16

D_MODEL = 1024
BATCH = 8
SEQ = 2048
DEPTH = 2
GRID_W = 64
ROWS = SEQ // GRID_W
CTX_LEN = 256
T_LAT = BATCH * SEQ
T_CTX = BATCH * CTX_LEN
T_ALL = T_LAT + T_CTX

D_FOURIER = 256
D_FG = 64
HEAD_DIM = 64
NA_HEADS = 8
D_NA = NA_HEADS * HEAD_DIM
D_CONV = 256
D_IN_PROJ = D_FOURIER + 3 * D_NA + 3 * D_CONV
NA_KH = 8
NA_KW = 16
N_GROUPS = 4
EXPERTS_PER_GROUP = 8
N_EXPERTS = 32
D_EXPERT = 512
EPS = 1e-6
NEG_INF = -1e30

LANES = 128
SUBLANES = 8
MOD_ROWS = 16
TM = 512
TMX = 512
TD = 512
TC = 256
VMEM_LIMIT = 56 * 1024 * 1024


def _cparams(*sem):
    return pltpu.CompilerParams(dimension_semantics=sem, vmem_limit_bytes=VMEM_LIMIT)


def _ada_kernel(cc_ref, w_ref, b_ref, o_ref):
    s = cc_ref[...]
    s = s * jax.nn.sigmoid(s)
    acc = jnp.dot(s.astype(BF16), w_ref[0].astype(BF16), preferred_element_type=F32)
    o_ref[0] = acc + b_ref[0]


def _modulation(cc, w_ada, b_ada):
    tn = 1536
    nj = 6 * D_MODEL // tn
    return pl.pallas_call(
        _ada_kernel,
        name="ada_modulation",
        grid=(DEPTH, nj),
        in_specs=[
            pl.BlockSpec((MOD_ROWS, D_MODEL), lambda l, j: (0, 0)),
            pl.BlockSpec((1, D_MODEL, tn), lambda l, j: (l, 0, j)),
            pl.BlockSpec((1, 1, tn), lambda l, j: (l, 0, j)),
        ],
        out_specs=pl.BlockSpec((1, MOD_ROWS, tn), lambda l, j: (l, 0, j)),
        out_shape=jax.ShapeDtypeStruct((DEPTH, MOD_ROWS, 6 * D_MODEL), F32),
        compiler_params=_cparams("arbitrary", "arbitrary"),
    )(cc, w_ada, b_ada.reshape(DEPTH, 1, 6 * D_MODEL))


def _mod_row(i, tile):
    return jnp.minimum((i * tile) // SEQ, BATCH)


NT_LAT = T_LAT // TM


def _lat_spec(cols):
    return pl.BlockSpec((TM, cols), lambda i: (jnp.minimum(i, NT_LAT - 1), 0))


def _ctx_spec(cols, first_block):
    return pl.BlockSpec((TM, cols), lambda i: (jnp.maximum(i - NT_LAT, 0) + first_block, 0))


def _pick_stream(lat_ref, ctx_ref):
    return jnp.where(pl.program_id(0) >= NT_LAT, ctx_ref[...], lat_ref[...])


def _inproj_kernel(xa_ref, xb_ref, mod_ref, n1_ref, w_ref, dft_ref, yf_ref, qkv_ref, cv_ref):
    _inproj_math(_pick_stream(xa_ref, xb_ref), mod_ref, n1_ref, w_ref, dft_ref, yf_ref, qkv_ref, cv_ref)


def _inproj_math(x, mod_ref, n1_ref, w_ref, dft_ref, yf_ref, qkv_ref, cv_ref):
    ms = jnp.mean(x * x, axis=-1, keepdims=True)
    y = x * lax.rsqrt(ms + EPS) * n1_ref[0]
    sh = mod_ref[:, 0:D_MODEL]
    sc = mod_ref[:, D_MODEL:2 * D_MODEL]
    h = y * (1.0 + sc) + sh
    u = jnp.dot(h.astype(BF16), w_ref[0], preferred_element_type=F32)
    yf = jnp.dot(u[:, 0:D_FOURIER].astype(BF16), dft_ref[...], preferred_element_type=F32)
    yf_ref[...] = yf.astype(BF16)
    q0 = D_FOURIER
    qkv_ref[:, 0:D_NA] = (u[:, q0:q0 + D_NA] * (1.0 / math.sqrt(HEAD_DIM))).astype(BF16)
    qkv_ref[:, D_NA:3 * D_NA] = u[:, q0 + D_NA:q0 + 3 * D_NA].astype(BF16)
    cv_ref[...] = u[:, q0 + 3 * D_NA:D_IN_PROJ]


def _in_projection(x_lat, x_ctx, ctx_first_block, mod4, norm1, w_in_bf, dft_ch, li):
    nt = T_ALL // TM
    return pl.pallas_call(
        _inproj_kernel,
        name="in_projection",
        grid=(nt,),
        in_specs=[
            _lat_spec(D_MODEL),
            _ctx_spec(D_MODEL, ctx_first_block),
            pl.BlockSpec((None, None, 1, 6 * D_MODEL), lambda i: (li, _mod_row(i, TM), 0, 0)),
            pl.BlockSpec((1, 1, D_MODEL), lambda i: (li, 0, 0)),
            pl.BlockSpec((1, D_MODEL, D_IN_PROJ), lambda i: (li, 0, 0)),
            pl.BlockSpec((D_FOURIER, 2 * D_FOURIER), lambda i: (0, 0)),
        ],
        out_specs=[
            pl.BlockSpec((TM, 2 * D_FOURIER), lambda i: (i, 0)),
            pl.BlockSpec((TM, 3 * D_NA), lambda i: (i, 0)),
            pl.BlockSpec((TM, 3 * D_CONV), lambda i: (i, 0)),
        ],
        out_shape=[
            jax.ShapeDtypeStruct((T_ALL, 2 * D_FOURIER), BF16),
            jax.ShapeDtypeStruct((T_ALL, 3 * D_NA), BF16),
            jax.ShapeDtypeStruct((T_ALL, 3 * D_CONV), F32),
        ],
        compiler_params=_cparams("arbitrary"),
    )(x_lat, x_ctx, mod4, norm1.reshape(DEPTH, 1, D_MODEL), w_in_bf, dft_ch)


def _dft_tables(n):
    k = np.arange(n, dtype=np.int64)
    ang = 2.0 * np.pi * ((k[:, None] * k[None, :]) % n).astype(np.float64) / n
    s = 1.0 / math.sqrt(n)
    return np.cos(ang) * s, np.sin(ang) * s


def _channel_dft():
    c, s = _dft_tables(D_FG)
    eye = np.eye(D_FOURIER // D_FG)
    return np.concatenate([np.kron(eye, c), np.kron(eye, s)], axis=1)


def _fourier_kernel(cn_ref, sn_ref, y_ref, wf_ref, o_ref):
    z = (jnp.dot(cn_ref[...], y_ref[:, 0:D_FOURIER], preferred_element_type=F32)
         - jnp.dot(sn_ref[...], y_ref[:, D_FOURIER:2 * D_FOURIER], preferred_element_type=F32))
    o_ref[...] = jnp.dot(z.astype(BF16), wf_ref[0].astype(BF16), preferred_element_type=F32).astype(BF16)


def _fourier_mix(yf, w_fourier, li, n, first_block):
    tk = min(n, 512)
    nk = n // tk
    cn, sn = _dft_tables(n)
    cn = jnp.asarray(cn, dtype=F32).astype(BF16)
    sn = jnp.asarray(sn, dtype=F32).astype(BF16)
    return pl.pallas_call(
        _fourier_kernel,
        name="fourier_mix",
        grid=(nk, BATCH),
        in_specs=[
            pl.BlockSpec((tk, n), lambda k, b: (k, 0)),
            pl.BlockSpec((tk, n), lambda k, b: (k, 0)),
            pl.BlockSpec((n, 2 * D_FOURIER), lambda k, b: (first_block + b, 0)),
            pl.BlockSpec((1, D_FOURIER, D_FOURIER), lambda k, b: (li, 0, 0)),
        ],
        out_specs=pl.BlockSpec((tk, D_FOURIER), lambda k, b: (b * nk + k, 0)),
        out_shape=jax.ShapeDtypeStruct((BATCH * n, D_FOURIER), BF16),
        compiler_params=_cparams("arbitrary", "arbitrary"),
    )(cn, sn, yf, w_fourier)


def _conv_kernel(cv_ref, wc_ref, o_ref, pad_ref):
    n = cv_ref.shape[0]
    gb = cv_ref[:, 0:D_CONV]
    g = cv_ref[:, D_CONV:2 * D_CONV] * cv_ref[:, 2 * D_CONV:3 * D_CONV]
    zero = jnp.zeros((SUBLANES, D_CONV), F32)
    pad_ref[0:SUBLANES, :] = zero
    pad_ref[n + SUBLANES:n + 2 * SUBLANES, :] = zero
    pad_ref[SUBLANES:n + SUBLANES, :] = g
    prev = pad_ref[SUBLANES - 1:n + SUBLANES - 1, :]
    nxt = pad_ref[SUBLANES + 1:n + SUBLANES + 1, :]
    w = wc_ref[0]
    o_ref[...] = (gb * (w[0:1, :] * prev + w[1:2, :] * g + w[2:3, :] * nxt)).astype(BF16)


def _short_conv(cv, w_conv, li, n, first_block):
    return pl.pallas_call(
        _conv_kernel,
        name="short_conv",
        grid=(BATCH,),
        in_specs=[
            pl.BlockSpec((n, 3 * D_CONV), lambda b: (first_block + b, 0)),
            pl.BlockSpec((1, 3, D_CONV), lambda b: (li, 0, 0)),
        ],
        out_specs=pl.BlockSpec((n, D_CONV), lambda b: (b, 0)),
        out_shape=jax.ShapeDtypeStruct((BATCH * n, D_CONV), BF16),
        scratch_shapes=[pltpu.VMEM((n + 2 * SUBLANES, D_CONV), F32)],
        compiler_params=_cparams("arbitrary"),
    )(cv, w_conv)


RPB_H = 2 * NA_KH - 1
RPB_W = 2 * NA_KW - 1
KEYS_LOC = NA_KH * GRID_W


def _bias_kernel(rpb_ref, o_ref):
    h = pl.program_id(0)
    qi = lax.broadcasted_iota(jnp.int32, (GRID_W, LANES), 0)
    lj = lax.broadcasted_iota(jnp.int32, (GRID_W, LANES), 1)
    kc = lj & (GRID_W - 1)
    hi = lj >= GRID_W
    d = kc - qi + (NA_KW - 1)
    cs = jnp.clip(qi - NA_KW // 2, 0, GRID_W - NA_KW)
    valid = (kc >= cs) & (kc < cs + NA_KW)
    tiles = []
    for a in range(RPB_H - 1):
        acc = jnp.zeros((GRID_W, LANES), F32)
        for b in range(RPB_W):
            va = rpb_ref[(h * RPB_H + a) * RPB_W + b]
            vb = rpb_ref[(h * RPB_H + a + 1) * RPB_W + b]
            acc = jnp.where(d == b, jnp.where(hi, vb, va), acc)
        tiles.append(jnp.where(valid, acc, NEG_INF))
    for cls in range(NA_KH):
        for m in range(NA_KH // 2):
            o_ref[0, cls, :, m * LANES:(m + 1) * LANES] = tiles[2 * m - cls + NA_KH - 1]


def _bias_table(rpb_flat):
    return pl.pallas_call(
        _bias_kernel,
        name="attn_bias_table",
        grid=(NA_HEADS,),
        in_specs=[pl.BlockSpec(memory_space=pltpu.SMEM)],
        out_specs=pl.BlockSpec((1, NA_KH, GRID_W, KEYS_LOC), lambda h: (h // 2, 0, h % 2, 0)),
        out_shape=jax.ShapeDtypeStruct((NA_HEADS // 2, NA_KH, 2 * GRID_W, KEYS_LOC), F32),
        compiler_params=_cparams("arbitrary"),
    )(rpb_flat)


_NT_DIMS = (((1,), (1,)), ((), ()))


def _softmax_pv(parts):
    m = None
    for s, _ in parts:
        ms = jnp.max(s, axis=1, keepdims=True)
        m = ms if m is None else jnp.maximum(m, ms)
    l = None
    o = None
    for s, v in parts:
        p = jnp.exp(s - m)
        ls = jnp.sum(p, axis=1, keepdims=True)
        os_ = jnp.dot(p.astype(BF16), v, preferred_element_type=F32)
        l = ls if l is None else l + ls
        o = os_ if o is None else o + os_
    return o / l


def _attn_latent_kernel(q_ref, k_ref, v_ref, kc_ref, vc_ref, tab_ref, o_ref, ve_ref, vce_ref):
    lane = lax.broadcasted_iota(jnp.int32, (GRID_W, LANES), 1)
    lo = lane < HEAD_DIM
    ve_ref[:, 0:LANES] = v_ref[...]
    ve_ref[:, LANES:2 * LANES] = jnp.ones((SEQ, LANES), BF16)
    vce_ref[:, 0:LANES] = vc_ref[...]
    vce_ref[:, LANES:2 * LANES] = jnp.ones((CTX_LEN, LANES), BF16)
    kc = kc_ref[...]
    vce = vce_ref[...]

    for r in range(ROWS):
        rs = min(max(r - NA_KH // 2, 0), ROWS - NA_KH)
        cls = r - rs
        q = q_ref[r * GRID_W:(r + 1) * GRID_W, :]
        kl = k_ref[rs * GRID_W:rs * GRID_W + KEYS_LOC, :]
        vl = ve_ref[rs * GRID_W:rs * GRID_W + KEYS_LOC, :]
        zero = jnp.zeros_like(q)
        q2 = jnp.concatenate([jnp.where(lo, q, zero), jnp.where(lo, zero, q)], axis=0)
        s1 = lax.dot_general(q2, kl, _NT_DIMS, preferred_element_type=F32) + tab_ref[0, cls]
        s2 = lax.dot_general(q2, kc, _NT_DIMS, preferred_element_type=F32)
        m = jnp.maximum(jnp.max(s1, axis=1, keepdims=True), jnp.max(s2, axis=1, keepdims=True))
        p1 = jnp.exp(s1 - m).astype(BF16)
        p2 = jnp.exp(s2 - m).astype(BF16)
        oe = (jnp.dot(p1, vl, preferred_element_type=F32) + jnp.dot(p2, vce, preferred_element_type=F32))
        o = oe[:, 0:LANES] / oe[:, LANES:2 * LANES]
        o_ref[r * GRID_W:(r + 1) * GRID_W, :] = jnp.where(lo, o[0:GRID_W], o[GRID_W:2 * GRID_W]).astype(BF16)


def _attn_latent(qkv, table):
    npair = NA_HEADS // 2
    cblk = T_LAT // CTX_LEN
    return pl.pallas_call(
        _attn_latent_kernel,
        name="attn_latent",
        grid=(BATCH, npair),
        in_specs=[
            pl.BlockSpec((SEQ, LANES), lambda b, p: (b, p)),
            pl.BlockSpec((SEQ, LANES), lambda b, p: (b, npair + p)),
            pl.BlockSpec((SEQ, LANES), lambda b, p: (b, 2 * npair + p)),
            pl.BlockSpec((CTX_LEN, LANES), lambda b, p: (cblk + b, npair + p)),
            pl.BlockSpec((CTX_LEN, LANES), lambda b, p: (cblk + b, 2 * npair + p)),
            pl.BlockSpec((1, NA_KH, 2 * GRID_W, KEYS_LOC), lambda b, p: (p, 0, 0, 0)),
        ],
        out_specs=pl.BlockSpec((SEQ, LANES), lambda b, p: (b, p)),
        out_shape=jax.ShapeDtypeStruct((T_LAT, D_NA), BF16),
        scratch_shapes=[pltpu.VMEM((SEQ, 2 * LANES), BF16), pltpu.VMEM((CTX_LEN, 2 * LANES), BF16)],
        compiler_params=_cparams("arbitrary", "arbitrary"),
    )(qkv, qkv, qkv, qkv, qkv, table)


def _attn_ctx_kernel(q_ref, k_ref, v_ref, o_ref):
    lane = lax.broadcasted_iota(jnp.int32, (CTX_LEN, LANES), 1)
    lo = lane < HEAD_DIM
    q = q_ref[...]
    k = k_ref[...]
    v = v_ref[...]
    outs = []
    for hh in range(2):
        qm = jnp.where(lo if hh == 0 else jnp.logical_not(lo), q, jnp.zeros_like(q))
        s = lax.dot_general(qm, k, _NT_DIMS, preferred_element_type=F32)
        outs.append(_softmax_pv([(s, v)]))
    o_ref[...] = jnp.where(lo, outs[0], outs[1]).astype(BF16)


def _attn_ctx(qkv):
    npair = NA_HEADS // 2
    cblk = T_LAT // CTX_LEN
    return pl.pallas_call(
        _attn_ctx_kernel,
        name="attn_context",
        grid=(BATCH, npair),
        in_specs=[
            pl.BlockSpec((CTX_LEN, LANES), lambda b, p: (cblk + b, p)),
            pl.BlockSpec((CTX_LEN, LANES), lambda b, p: (cblk + b, npair + p)),
            pl.BlockSpec((CTX_LEN, LANES), lambda b, p: (cblk + b, 2 * npair + p)),
        ],
        out_specs=pl.BlockSpec((CTX_LEN, LANES), lambda b, p: (b, p)),
        out_shape=jax.ShapeDtypeStruct((T_CTX, D_NA), BF16),
        compiler_params=_cparams("arbitrary", "arbitrary"),
    )(qkv, qkv, qkv)


ROUTE_COLS = N_GROUPS + N_EXPERTS


def _to_token_major(ref, val):
    rows = val.shape[0]
    for j in range(D_MODEL // LANES):
        ref[pl.ds(j, rows, stride=SUBLANES), :] = val[:, j * LANES:(j + 1) * LANES]


def _from_token_major(ref, rows, j):
    return ref[pl.ds(j, rows, stride=SUBLANES), :]


def _outproj_kernel(xa_ref, xb_ref, ffa_ref, ffb_ref, ata_ref, atb_ref, cva_ref, cvb_ref,
                    mod_ref, n2_ref, wo_ref, wr_ref, br_ref, x1_ref, hx_ref, route_ref):
    mix_in = jnp.concatenate([_pick_stream(ffa_ref, ffb_ref), _pick_stream(ata_ref, atb_ref),
                              _pick_stream(cva_ref, cvb_ref)], axis=-1)
    mix = jnp.dot(mix_in, wo_ref[0], preferred_element_type=F32)
    g1 = mod_ref[:, 2 * D_MODEL:3 * D_MODEL]
    x1 = _pick_stream(xa_ref, xb_ref) + g1 * mix
    x1_ref[...] = x1
    ms = jnp.mean(x1 * x1, axis=-1, keepdims=True)
    y = x1 * lax.rsqrt(ms + EPS) * n2_ref[0]
    sh2 = mod_ref[:, 3 * D_MODEL:4 * D_MODEL]
    sc2 = mod_ref[:, 4 * D_MODEL:5 * D_MODEL]
    hx = y * (1.0 + sc2) + sh2
    _to_token_major(hx_ref, hx)

    hx_hi = hx.astype(BF16)
    hx_lo = (hx - hx_hi.astype(F32)).astype(BF16)
    part = (jnp.dot(hx_hi, wr_ref[0], preferred_element_type=F32)
            + jnp.dot(hx_lo, wr_ref[0], preferred_element_type=F32))
    logits = part[:, 0:LANES] + part[:, LANES:2 * LANES] + br_ref[0]
    tm = logits.shape[0]
    lane = lax.broadcasted_iota(jnp.int32, (tm, LANES), 1)
    big = jnp.int32(LANES)
    gl = jnp.where(lane < N_GROUPS, logits, -jnp.inf)
    gmax = jnp.max(gl, axis=1, keepdims=True)
    gidx = jnp.min(jnp.where(gl == gmax, lane, big), axis=1, keepdims=True)
    g_w = 1.0 / jnp.sum(jnp.exp(gl - gmax), axis=1, keepdims=True)
    e_lane = lane - N_GROUPS
    in_group = (e_lane >= 0) & (e_lane < N_EXPERTS) & ((e_lane >> 3) == gidx)
    es = jnp.where(in_group, logits, -jnp.inf)
    t1 = jnp.max(es, axis=1, keepdims=True)
    i1 = jnp.min(jnp.where(es == t1, lane, big), axis=1, keepdims=True)
    es2 = jnp.where(lane == i1, -jnp.inf, es)
    t2 = jnp.max(es2, axis=1, keepdims=True)
    i2 = jnp.min(jnp.where(es2 == t2, lane, big), axis=1, keepdims=True)
    dlt = jnp.exp(t2 - t1)
    w1 = g_w / (1.0 + dlt)
    w2 = g_w * dlt / (1.0 + dlt)
    e1 = (i1 - N_GROUPS).astype(F32)
    e2 = (i2 - N_GROUPS).astype(F32)
    route_ref[...] = jnp.where(lane == 0, e1, jnp.where(lane == 1, e2,
                               jnp.where(lane == 2, w1, jnp.where(lane == 3, w2, 0.0))))


def _out_projection(x_pair, ctx_first_block, ff_pair, at_pair, cv_pair, mod4, norm2, w_out_bf, w_r, b_r, li, n_tok):
    nt = n_tok // TM
    return pl.pallas_call(
        _outproj_kernel,
        name="out_projection",
        grid=(nt,),
        in_specs=[
            _lat_spec(D_MODEL), _ctx_spec(D_MODEL, ctx_first_block),
            _lat_spec(D_FOURIER), _ctx_spec(D_FOURIER, 0),
            _lat_spec(D_NA), _ctx_spec(D_NA, 0),
            _lat_spec(D_CONV), _ctx_spec(D_CONV, 0),
            pl.BlockSpec((None, None, 1, 6 * D_MODEL), lambda i: (li, _mod_row(i, TM), 0, 0)),
            pl.BlockSpec((1, 1, D_MODEL), lambda i: (li, 0, 0)),
            pl.BlockSpec((1, D_MODEL, D_MODEL), lambda i: (li, 0, 0)),
            pl.BlockSpec((1, D_MODEL, 2 * LANES), lambda i: (li, 0, 0)),
            pl.BlockSpec((1, 1, LANES), lambda i: (li, 0, 0)),
        ],
        out_specs=[
            pl.BlockSpec((TM, D_MODEL), lambda i: (i, 0)),
            pl.BlockSpec((TM * SUBLANES, LANES), lambda i: (i, 0)),
            pl.BlockSpec((TM, LANES), lambda i: (i, 0)),
        ],
        out_shape=[
            jax.ShapeDtypeStruct((n_tok, D_MODEL), F32),
            jax.ShapeDtypeStruct((n_tok * SUBLANES, LANES), F32),
            jax.ShapeDtypeStruct((n_tok, LANES), F32),
        ],
        compiler_params=_cparams("arbitrary"),
    )(*x_pair, *ff_pair, *at_pair, *cv_pair, mod4, norm2.reshape(DEPTH, 1, D_MODEL), w_out_bf, w_r, b_r)


def _plan_kernel(route_ref, pos_ref, cnt_ref, ends_ref, carry_ref, offs_ref):
    ph = pl.program_id(0)
    i = pl.program_id(1)
    tm = PLAN_SUB
    lane = lax.broadcasted_iota(jnp.int32, (tm, LANES), 1)

    def one_hots(k):
        r = route_ref[k * tm:(k + 1) * tm, :]
        oh1 = lane == r[:, 0:1].astype(jnp.int32)
        oh2 = lane == r[:, 1:2].astype(jnp.int32)
        return oh1, oh2, jnp.where(oh1 | oh2, 1.0, 0.0)

    @pl.when((ph == 0) & (i == 0))
    def _():
        carry_ref[...] = jnp.zeros_like(carry_ref)

    @pl.when(ph == 0)
    def _():
        total = carry_ref[...]
        for k in range(PLAN_TM // tm):
            total = total + jnp.sum(one_hots(k)[2], axis=0, keepdims=True)
        carry_ref[...] = total

    @pl.when((ph == 1) & (i == 0))
    def _():
        cnt = carry_ref[...]
        cnt_ref[...] = jnp.broadcast_to(cnt, cnt_ref.shape)
        tiles = jnp.ceil(cnt * (1.0 / TMX))
        a = lax.broadcasted_iota(jnp.int32, (LANES, LANES), 0)
        b = lax.broadcasted_iota(jnp.int32, (LANES, LANES), 1)
        upper = jnp.where(a < b, 1.0, 0.0).astype(BF16)
        tiles8 = jnp.broadcast_to(tiles, (SUBLANES, LANES))
        first = jnp.dot(tiles8.astype(BF16), upper, preferred_element_type=F32)
        offs_ref[...] = first[0:1, :] * TMX
        ends_ref[...] = (first + tiles8) * TMX
        carry_ref[...] = jnp.zeros_like(carry_ref)

    @pl.when(ph == 1)
    def _():
        row = lax.broadcasted_iota(jnp.int32, (tm, tm), 0)
        col = lax.broadcasted_iota(jnp.int32, (tm, tm), 1)
        tri = jnp.where(row > col, 1.0, 0.0).astype(BF16)
        base = carry_ref[...] + offs_ref[...]
        for k in range(PLAN_TM // tm):
            oh1, oh2, oh = one_hots(k)
            cum = jnp.dot(tri, oh.astype(BF16), preferred_element_type=F32) + base
            p1 = jnp.sum(jnp.where(oh1, cum, 0.0), axis=1, keepdims=True)
            p2 = jnp.sum(jnp.where(oh2, cum, 0.0), axis=1, keepdims=True)
            pos_ref[k * tm:(k + 1) * tm, :] = (
                jnp.where(lane == 0, p1, jnp.where(lane == 1, p2, 0.0)).astype(jnp.int32))
            base = base + jnp.sum(oh, axis=0, keepdims=True)
        carry_ref[...] = base - offs_ref[...]


PLAN_TM = 2048
PLAN_SUB = 512


def _plan(route, n_tok):
    nt = n_tok // PLAN_TM
    return pl.pallas_call(
        _plan_kernel,
        name="moe_plan",
        grid=(2, nt),
        in_specs=[pl.BlockSpec((PLAN_TM, LANES), lambda ph, i: (i, 0))],
        out_specs=[
            pl.BlockSpec((PLAN_TM, LANES), lambda ph, i: (i * ph, 0)),
            pl.BlockSpec((SUBLANES, LANES), lambda ph, i: (0, 0)),
            pl.BlockSpec((SUBLANES, LANES), lambda ph, i: (0, 0)),
        ],
        out_shape=[
            jax.ShapeDtypeStruct((n_tok, LANES), jnp.int32),
            jax.ShapeDtypeStruct((SUBLANES, LANES), F32),
            jax.ShapeDtypeStruct((SUBLANES, LANES), F32),
        ],
        scratch_shapes=[pltpu.VMEM((1, LANES), F32), pltpu.VMEM((1, LANES), F32)],
        compiler_params=_cparams("arbitrary", "arbitrary"),
    )(route)


def _row_tile(ref, row):
    return ref.at[pl.ds(pl.multiple_of(row * SUBLANES, SUBLANES), SUBLANES), :]


DISPATCH_UNROLL = 8
PAD_CHUNKS = tuple(1 << b for b in reversed(range(TMX.bit_length() - 1)))


def _dispatch_kernel(pos_ref, pstart_ref, plen_ref, tail_ref, hx_ref, xs_ref, zbuf, sem, zsem):
    i = pl.program_id(0)

    def pad_copies(fn):
        def ebody(e, carry):
            n = plen_ref[e]
            off = pstart_ref[e]
            for rows in PAD_CHUNKS:
                @pl.when((n & rows) != 0)
                def _():
                    fn(pltpu.make_async_copy(
                        zbuf.at[pl.ds(0, rows * SUBLANES), :],
                        xs_ref.at[pl.ds(pl.multiple_of(off * SUBLANES, SUBLANES), rows * SUBLANES), :], zsem.at[0]))
                off = off + (n & rows)
            return carry

        lax.fori_loop(0, N_EXPERTS, ebody, 0)

        def tbody(k, carry):
            row = pl.multiple_of((tail_ref[0] + k * PAD_CHUNKS[0]) * SUBLANES, SUBLANES)
            fn(pltpu.make_async_copy(zbuf, xs_ref.at[pl.ds(row, PAD_CHUNKS[0] * SUBLANES), :], zsem.at[0]))
            return carry

        lax.fori_loop(0, tail_ref[1], tbody, 0)

    @pl.when(i == 0)
    def _():
        zbuf[...] = jnp.zeros_like(zbuf)
        pad_copies(lambda cp: cp.start())

    def body(k, carry):
        for u in range(DISPATCH_UNROLL):
            t = k * DISPATCH_UNROLL + u
            src = _row_tile(hx_ref, t)
            for j in range(2):
                pltpu.make_async_copy(src, _row_tile(xs_ref, pos_ref[(i * TD + t) * 2 + j]),
                                      sem.at[0]).start(priority=j)
        return carry

    lax.fori_loop(0, TD // DISPATCH_UNROLL, body, 0)
    nrow = TD * SUBLANES
    for j in range(2):
        pltpu.make_async_copy(hx_ref, xs_ref.at[pl.ds(0, nrow), :], sem.at[0]).wait()

    @pl.when(i == 0)
    def _():
        pad_copies(lambda cp: cp.wait())


def _dispatch(pos_flat, pad_start, pad_len, tail, hx_tm, n_tok, n_rows):
    return pl.pallas_call(
        _dispatch_kernel,
        name="moe_dispatch",
        grid_spec=pltpu.PrefetchScalarGridSpec(
            num_scalar_prefetch=4,
            grid=(n_tok // TD,),
            in_specs=[pl.BlockSpec((TD * SUBLANES, LANES), lambda i, pos, ps, pn, tl: (i, 0))],
            out_specs=pl.BlockSpec(memory_space=pl.ANY),
            scratch_shapes=[pltpu.VMEM((PAD_CHUNKS[0] * SUBLANES, LANES), F32),
                            pltpu.SemaphoreType.DMA((1,)), pltpu.SemaphoreType.DMA((1,))],
        ),
        out_shape=jax.ShapeDtypeStruct((n_rows * SUBLANES, LANES), F32),
        compiler_params=_cparams("arbitrary"),
    )(pos_flat, pad_start, pad_len, tail, hx_tm)


def _moe_kernel(te_ref, na_ref, first_ref, wslot_ref, nxt_ref, xs_ref, wg_hbm, wu_hbm, wd_hbm, os_ref,
                wg_f, wu_f, wd_f, wg_s, wu_s, wd_s, wsem, *, li):
    i = pl.program_id(0)
    active = i < na_ref[0]

    def weight_copies(e, slot):
        return [pltpu.make_async_copy(wg_hbm.at[li, e], wg_f.at[slot], wsem.at[slot, 0]),
                pltpu.make_async_copy(wu_hbm.at[li, e], wu_f.at[slot], wsem.at[slot, 1]),
                pltpu.make_async_copy(wd_hbm.at[li, e], wd_f.at[slot], wsem.at[slot, 2])]

    @pl.when(active & (first_ref[i] == 1))
    def _():
        slot = wslot_ref[i]

        @pl.when(i == 0)
        def _():
            for cp in weight_copies(te_ref[0], 0):
                cp.start()

        @pl.when(nxt_ref[i] >= 0)
        def _():
            for cp in weight_copies(nxt_ref[i], 1 - slot):
                cp.start()

        for cp in weight_copies(te_ref[i], slot):
            cp.wait()
        wg_s[...] = wg_f[slot].astype(BF16)
        wu_s[...] = wu_f[slot].astype(BF16)
        wd_s[...] = wd_f[slot].astype(BF16)

    @pl.when(active)
    def _():
        x = jnp.concatenate([_from_token_major(xs_ref, TMX, j).astype(BF16)
                             for j in range(D_MODEL // LANES)], axis=-1)
        g = jnp.dot(x, wg_s[...], preferred_element_type=F32)
        u = jnp.dot(x, wu_s[...], preferred_element_type=F32)
        h = (g * jax.nn.sigmoid(g) * u).astype(BF16)
        _to_token_major(os_ref, jnp.dot(h, wd_s[...], preferred_element_type=F32))

    @pl.when(jnp.logical_not(active))
    def _():
        os_ref[...] = jnp.zeros_like(os_ref)


def _moe(tile_expert, n_active, has_rows, xs, w_gate, w_up, w_down, li, n_tiles):
    eid = jnp.arange(N_EXPERTS, dtype=jnp.int32)
    later = jnp.where(has_rows[None, :] & (eid[None, :] > eid[:, None]), eid[None, :], N_EXPERTS)
    next_of_expert = jnp.min(later, axis=1)
    next_of_expert = jnp.where(next_of_expert == N_EXPERTS, -1, next_of_expert)
    first = jnp.concatenate([jnp.ones((1,), jnp.int32),
                             (tile_expert[1:] != tile_expert[:-1]).astype(jnp.int32)])
    wslot = (jnp.cumsum(first) - 1) % 2
    nxt = jnp.sum(jnp.where(tile_expert[:, None] == eid[None, :], next_of_expert[None, :], 0), axis=1)

    def row_map(i, te, na, fi, ws, nx):
        return (jnp.minimum(i, na[0] - 1), 0)

    return pl.pallas_call(
        functools.partial(_moe_kernel, li=li),
        name="moe_experts",
        grid_spec=pltpu.PrefetchScalarGridSpec(
            num_scalar_prefetch=5,
            grid=(n_tiles,),
            in_specs=[
                pl.BlockSpec((TMX * SUBLANES, LANES), row_map),
                pl.BlockSpec(memory_space=pl.ANY),
                pl.BlockSpec(memory_space=pl.ANY),
                pl.BlockSpec(memory_space=pl.ANY),
            ],
            out_specs=pl.BlockSpec((TMX * SUBLANES, LANES), lambda i, te, na, fi, ws, nx: (i, 0)),
            scratch_shapes=[
                pltpu.VMEM((2, D_MODEL, D_EXPERT), F32),
                pltpu.VMEM((2, D_MODEL, D_EXPERT), F32),
                pltpu.VMEM((2, D_EXPERT, D_MODEL), F32),
                pltpu.VMEM((D_MODEL, D_EXPERT), BF16),
                pltpu.VMEM((D_MODEL, D_EXPERT), BF16),
                pltpu.VMEM((D_EXPERT, D_MODEL), BF16),
                pltpu.SemaphoreType.DMA((2, 3)),
            ],
        ),
        out_shape=jax.ShapeDtypeStruct((n_tiles * TMX * SUBLANES, LANES), F32),
        compiler_params=_cparams("arbitrary"),
    )(tile_expert, n_active, first, wslot.astype(jnp.int32), nxt.astype(jnp.int32), xs, w_gate, w_up, w_down)


GATHER_UNROLL = 8


def _start_row_gathers(pos_ref, os_ref, buf, sem, tile, slot, n):
    def body(k, carry):
        for u in range(GATHER_UNROLL):
            t = k * GATHER_UNROLL + u
            for j in range(2):
                pltpu.make_async_copy(_row_tile(os_ref, pos_ref[(tile * n + t) * 2 + j]),
                                      _row_tile(buf.at[slot, j], t), sem.at[slot, j]).start(priority=j)
        return carry

    lax.fori_loop(0, n // GATHER_UNROLL, body, 0)


def _combined_tile(pos_ref, x1_ref, route_ref, mod_ref, os_ref, buf, sem, n):
    i = pl.program_id(0)
    slot = i % 2

    @pl.when(i == 0)
    def _():
        _start_row_gathers(pos_ref, os_ref, buf, sem, 0, 0, n)

    @pl.when(i + 1 < pl.num_programs(0))
    def _():
        _start_row_gathers(pos_ref, os_ref, buf, sem, i + 1, 1 - slot, n)

    for j in range(2):
        pltpu.make_async_copy(os_ref.at[pl.ds(0, n * SUBLANES), :], buf.at[slot, j], sem.at[slot, j]).wait()
    w1 = route_ref[:, 2:3]
    w2 = route_ref[:, 3:4]
    cols = []
    for c in range(D_MODEL // LANES):
        y = w1 * _from_token_major(buf.at[slot, 0], n, c) + w2 * _from_token_major(buf.at[slot, 1], n, c)
        g2 = mod_ref[:, 5 * D_MODEL + c * LANES:5 * D_MODEL + (c + 1) * LANES]
        cols.append(x1_ref[:, c * LANES:(c + 1) * LANES] + g2 * y)
    return jnp.concatenate(cols, axis=-1)


def _combine_final_kernel(pos_ref, x1_ref, route_ref, mod_ref, nf_ref, os_ref, o_ref, buf, sem):
    x2 = _combined_tile(pos_ref, x1_ref, route_ref, mod_ref, os_ref, buf, sem, TC)
    ms = jnp.mean(x2 * x2, axis=-1, keepdims=True)
    o_ref[...] = x2 * lax.rsqrt(ms + EPS) * nf_ref[...]


def _gather_scratch(n):
    return [pltpu.VMEM((2, 2, n * SUBLANES, LANES), F32), pltpu.SemaphoreType.DMA((2, 2))]


def _combine_inproj_kernel(pos_ref, x1_ref, route_ref, mod_ref, modn_ref, n1_ref, w_ref, dft_ref, os_ref,
                           x2_ref, yf_ref, qkv_ref, cv_ref, buf, sem):
    x2 = _combined_tile(pos_ref, x1_ref, route_ref, mod_ref, os_ref, buf, sem, TM)
    x2_ref[...] = x2
    _inproj_math(x2, modn_ref, n1_ref, w_ref, dft_ref, yf_ref, qkv_ref, cv_ref)


def _combine_inproj(pos_flat, x1, route, mod4, norm1, w_in_bf, dft_ch, osrt, li):
    def tmap(i, pos):
        return (i, 0)

    return pl.pallas_call(
        _combine_inproj_kernel,
        name="moe_combine_in_projection",
        grid_spec=pltpu.PrefetchScalarGridSpec(
            num_scalar_prefetch=1,
            grid=(T_ALL // TM,),
            in_specs=[
                pl.BlockSpec((TM, D_MODEL), tmap),
                pl.BlockSpec((TM, LANES), tmap),
                pl.BlockSpec((None, None, 1, 6 * D_MODEL), lambda i, pos: (li, _mod_row(i, TM), 0, 0)),
                pl.BlockSpec((None, None, 1, 6 * D_MODEL), lambda i, pos: (li + 1, _mod_row(i, TM), 0, 0)),
                pl.BlockSpec((1, 1, D_MODEL), lambda i, pos: (li + 1, 0, 0)),
                pl.BlockSpec((1, D_MODEL, D_IN_PROJ), lambda i, pos: (li + 1, 0, 0)),
                pl.BlockSpec((D_FOURIER, 2 * D_FOURIER), lambda i, pos: (0, 0)),
                pl.BlockSpec(memory_space=pl.ANY),
            ],
            out_specs=[
                pl.BlockSpec((TM, D_MODEL), tmap),
                pl.BlockSpec((TM, 2 * D_FOURIER), tmap),
                pl.BlockSpec((TM, 3 * D_NA), tmap),
                pl.BlockSpec((TM, 3 * D_CONV), tmap),
            ],
            scratch_shapes=_gather_scratch(TM),
        ),
        out_shape=[
            jax.ShapeDtypeStruct((T_ALL, D_MODEL), F32),
            jax.ShapeDtypeStruct((T_ALL, 2 * D_FOURIER), BF16),
            jax.ShapeDtypeStruct((T_ALL, 3 * D_NA), BF16),
            jax.ShapeDtypeStruct((T_ALL, 3 * D_CONV), F32),
        ],
        compiler_params=_cparams("arbitrary"),
    )(pos_flat, x1, route, mod4, mod4, norm1.reshape(DEPTH, 1, D_MODEL), w_in_bf, dft_ch, osrt)


def _combine_final(pos_flat, x1, route, mod4, norm_final, osrt, li, n_tok):
    def tmap(i, pos):
        return (i, 0)

    return pl.pallas_call(
        _combine_final_kernel,
        name="moe_combine_final",
        grid_spec=pltpu.PrefetchScalarGridSpec(
            num_scalar_prefetch=1,
            grid=(n_tok // TC,),
            in_specs=[
                pl.BlockSpec((TC, D_MODEL), tmap),
                pl.BlockSpec((TC, LANES), tmap),
                pl.BlockSpec((None, None, 1, 6 * D_MODEL), lambda i, pos: (li, _mod_row(i, TC), 0, 0)),
                pl.BlockSpec((1, D_MODEL), lambda i, pos: (0, 0)),
                pl.BlockSpec(memory_space=pl.ANY),
            ],
            out_specs=pl.BlockSpec((TC, D_MODEL), tmap),
            scratch_shapes=_gather_scratch(TC),
        ),
        out_shape=jax.ShapeDtypeStruct((n_tok, D_MODEL), F32),
        compiler_params=_cparams("arbitrary"),
    )(pos_flat, x1, route, mod4, norm_final.reshape(1, D_MODEL), osrt)


def _moe_block(hx_tm, route, w_gate, w_up, w_down, li, n_tok):
    n_tiles = (2 * n_tok) // TMX + N_EXPERTS
    n_rows = n_tiles * TMX
    pos, cnt, ends = _plan(route, n_tok)
    pos = pos[:, 0:2].reshape(-1)
    cnt = cnt[0, :N_EXPERTS].astype(jnp.int32)
    ends = ends[0, :N_EXPERTS].astype(jnp.int32)
    padded = ((cnt + TMX - 1) // TMX) * TMX
    pad_start = ends - padded + cnt
    tile_start = jnp.arange(n_tiles, dtype=jnp.int32) * TMX
    tile_expert = jnp.minimum(jnp.sum((tile_start[:, None] >= ends[None, :]).astype(jnp.int32), axis=1),
                              N_EXPERTS - 1)
    n_active = (ends[-1:] // TMX).astype(jnp.int32)
    tail = jnp.concatenate([ends[-1:], (n_rows - ends[-1:]) // PAD_CHUNKS[0]])

    xs = _dispatch(pos, pad_start, padded - cnt, tail, hx_tm, n_tok, n_rows)
    return pos, _moe(tile_expert, n_active, padded > 0, xs, w_gate, w_up, w_down, li, n_tiles)


def kernel(x, c, ctx, c_ctx, w_ada, b_ada, norm1, norm2, w_in, w_fourier, w_conv, rpb, w_out, w_rg, b_rg,
           w_re, b_re, w_gate, w_up, w_down, norm_final):
    x_pair, ctx_first = (x.reshape(T_LAT, D_MODEL), ctx.reshape(T_CTX, D_MODEL)), 0
    cc =jnp.concatenate([c, c_ctx[None, :], jnp.zeros((MOD_ROWS - BATCH - 1, D_MODEL), F32)], axis=0)
    mod4 = _modulation(cc, w_ada, b_ada).reshape(DEPTH, MOD_ROWS, 1, 6 * D_MODEL)

    w_in_bf = w_in.astype(BF16)
    w_out_bf = w_out.astype(BF16)
    pad = jnp.zeros((DEPTH, D_MODEL, LANES - ROUTE_COLS), F32)
    w_r = jnp.concatenate([w_rg, w_re, pad], axis=-1)
    w_r_hi = w_r.astype(BF16)
    w_r = jnp.concatenate([w_r_hi, (w_r - w_r_hi.astype(F32)).astype(BF16)], axis=-1)
    b_r = jnp.concatenate([b_rg, b_re, pad[:, 0, :]], axis=-1).reshape(DEPTH, 1, LANES)
    dft_ch = jnp.asarray(_channel_dft(), dtype=F32).astype(BF16)
    ctx_blk = T_LAT // CTX_LEN

    for li in range(DEPTH):
        last = li == DEPTH - 1
        if li == 0:
            yf, qkv, cv = _in_projection(*x_pair, ctx_first, mod4, norm1, w_in_bf, dft_ch, li)
        table = _bias_table(rpb[li].reshape(-1))
        ff = _fourier_mix(yf, w_fourier, li, SEQ, 0)
        cvo = _short_conv(cv, w_conv, li, SEQ, 0)
        at = _attn_latent(qkv, table)
        if last:
            n_tok = T_LAT
            ff_c, cvo_c, at_c = ff, cvo, at
        else:
            n_tok = T_ALL
            ff_c = _fourier_mix(yf, w_fourier, li, CTX_LEN, ctx_blk)
            cvo_c = _short_conv(cv, w_conv, li, CTX_LEN, ctx_blk)
            at_c = _attn_ctx(qkv)
        x1, hx_tm, route = _out_projection(x_pair, ctx_first, (ff, ff_c), (at, at_c), (cvo, cvo_c), mod4, norm2,
                                           w_out_bf, w_r, b_r, li, n_tok)
        pos, osrt = _moe_block(hx_tm, route, w_gate, w_up, w_down, li, n_tok)
        if last:
            out = _combine_final(pos, x1, route, mod4, norm_final, osrt, li, n_tok)
            return out.reshape(BATCH, SEQ, D_MODEL)
        xt, yf, qkv, cv = _combine_inproj(pos, x1, route, mod4, norm1, w_in_bf, dft_ch, osrt, li)
        x_pair, ctx_first = (xt, xt), NT_LAT
```

```python
import functools
import math

import numpy as np
import jax
import jax.numpy as jnp
from jax import lax
from jax.experimental import pallas as pl
from jax.experimental.pallas import tpu as pltpu

F32 = jnp.float32
BF16 = jnp.bfloat16

D_MODEL = 1024
BATCH = 8
SEQ = 2048
DEPTH = 2
GRID_W = 64
ROWS = SEQ // GRID_W
CTX_LEN = 256
T_LAT = BATCH * SEQ
T_CTX = BATCH * CTX_LEN
T_ALL = T_LAT + T_CTX

D_FOURIER = 256
D_FG = 64
HEAD_DIM = 64
NA_HEADS = 8
D_NA = NA_HEADS * HEAD_DIM
D_CONV = 256
D_IN_PROJ = D_FOURIER + 3 * D_NA + 3 * D_CONV
NA_KH = 8
NA_KW = 16
N_GROUPS = 4
EXPERTS_PER_GROUP = 8
N_EXPERTS = 32
D_EXPERT = 512
EPS = 1e-6
NEG_INF = -1e30

LANES = 128
SUBLANES = 8
MOD_ROWS = 16
TM = 512
TMX = 512
TD = 512
TC = 256
VMEM_LIMIT = 56 * 1024 * 1024


def _cparams(*sem):
    return pltpu.CompilerParams(dimension_semantics=sem, vmem_limit_bytes=VMEM_LIMIT)


def _ada_kernel(cc_ref, w_ref, b_ref, o_ref):
    s = cc_ref[...]
    s = s * jax.nn.sigmoid(s)
    acc = jnp.dot(s.astype(BF16), w_ref[0].astype(BF16), preferred_element_type=F32)
    o_ref[0] = acc + b_ref[0]


def _modulation(cc, w_ada, b_ada):
    tn = 1536
    nj = 6 * D_MODEL // tn
    return pl.pallas_call(
        _ada_kernel,
        name="ada_modulation",
        grid=(DEPTH, nj),
        in_specs=[
            pl.BlockSpec((MOD_ROWS, D_MODEL), lambda l, j: (0, 0)),
            pl.BlockSpec((1, D_MODEL, tn), lambda l, j: (l, 0, j)),
            pl.BlockSpec((1, 1, tn), lambda l, j: (l, 0, j)),
        ],
        out_specs=pl.BlockSpec((1, MOD_ROWS, tn), lambda l, j: (l, 0, j)),
        out_shape=jax.ShapeDtypeStruct((DEPTH, MOD_ROWS, 6 * D_MODEL), F32),
        compiler_params=_cparams("arbitrary", "arbitrary"),
    )(cc, w_ada, b_ada.reshape(DEPTH, 1, 6 * D_MODEL))


def _mod_row(i, tile):
    return jnp.minimum((i * tile) // SEQ, BATCH)


NT_LAT = T_LAT // TM


def _lat_spec(cols):
    return pl.BlockSpec((TM, cols), lambda i: (jnp.minimum(i, NT_LAT - 1), 0))


def _ctx_spec(cols, first_block):
    return pl.BlockSpec((TM, cols), lambda i: (jnp.maximum(i - NT_LAT, 0) + first_block, 0))


def _pick_stream(lat_ref, ctx_ref):
    return jnp.where(pl.program_id(0) >= NT_LAT, ctx_ref[...], lat_ref[...])


def _inproj_kernel(xa_ref, xb_ref, mod_ref, n1_ref, w_ref, dft_ref, yf_ref, qkv_ref, cv_ref):
    _inproj_math(_pick_stream(xa_ref, xb_ref), mod_ref, n1_ref, w_ref, dft_ref, yf_ref, qkv_ref, cv_ref)


def _inproj_math(x, mod_ref, n1_ref, w_ref, dft_ref, yf_ref, qkv_ref, cv_ref):
    ms = jnp.mean(x * x, axis=-1, keepdims=True)
    y = x * lax.rsqrt(ms + EPS) * n1_ref[0]
    sh = mod_ref[:, 0:D_MODEL]
    sc = mod_ref[:, D_MODEL:2 * D_MODEL]
    h = y * (1.0 + sc) + sh
    u = jnp.dot(h.astype(BF16), w_ref[0], preferred_element_type=F32)
    yf = jnp.dot(u[:, 0:D_FOURIER].astype(BF16), dft_ref[...], preferred_element_type=F32)
    yf_ref[...] = yf.astype(BF16)
    q0 = D_FOURIER
    qkv_ref[:, 0:D_NA] = (u[:, q0:q0 + D_NA] * (1.0 / math.sqrt(HEAD_DIM))).astype(BF16)
    qkv_ref[:, D_NA:3 * D_NA] = u[:, q0 + D_NA:q0 + 3 * D_NA].astype(BF16)
    cv_ref[...] = u[:, q0 + 3 * D_NA:D_IN_PROJ]


def _in_projection(x_lat, x_ctx, ctx_first_block, mod4, norm1, w_in_bf, dft_ch, li):
    nt = T_ALL // TM
    return pl.pallas_call(
        _inproj_kernel,
        name="in_projection",
        grid=(nt,),
        in_specs=[
            _lat_spec(D_MODEL),
            _ctx_spec(D_MODEL, ctx_first_block),
            pl.BlockSpec((None, None, 1, 6 * D_MODEL), lambda i: (li, _mod_row(i, TM), 0, 0)),
            pl.BlockSpec((1, 1, D_MODEL), lambda i: (li, 0, 0)),
            pl.BlockSpec((1, D_MODEL, D_IN_PROJ), lambda i: (li, 0, 0)),
            pl.BlockSpec((D_FOURIER, 2 * D_FOURIER), lambda i: (0, 0)),
        ],
        out_specs=[
            pl.BlockSpec((TM, 2 * D_FOURIER), lambda i: (i, 0)),
            pl.BlockSpec((TM, 3 * D_NA), lambda i: (i, 0)),
            pl.BlockSpec((TM, 3 * D_CONV), lambda i: (i, 0)),
        ],
        out_shape=[
            jax.ShapeDtypeStruct((T_ALL, 2 * D_FOURIER), BF16),
            jax.ShapeDtypeStruct((T_ALL, 3 * D_NA), BF16),
            jax.ShapeDtypeStruct((T_ALL, 3 * D_CONV), F32),
        ],
        compiler_params=_cparams("arbitrary"),
    )(x_lat, x_ctx, mod4, norm1.reshape(DEPTH, 1, D_MODEL), w_in_bf, dft_ch)


def _dft_tables(n):
    k = np.arange(n, dtype=np.int64)
    ang = 2.0 * np.pi * ((k[:, None] * k[None, :]) % n).astype(np.float64) / n
    s = 1.0 / math.sqrt(n)
    return np.cos(ang) * s, np.sin(ang) * s


def _channel_dft():
    c, s = _dft_tables(D_FG)
    eye = np.eye(D_FOURIER // D_FG)
    return np.concatenate([np.kron(eye, c), np.kron(eye, s)], axis=1)


def _fourier_kernel(ch_ref, sh_ref, j_ref, y_ref, wf_ref, o_ref, fold_ref, rev_ref, *, n):
    h = n // 2
    nb = h // LANES
    df = D_FOURIER
    scale = 1.0 / math.sqrt(n)
    rev = j_ref[...]
    zero_rows = jnp.zeros((SUBLANES, 2 * df), F32)

    fold_ref[0:SUBLANES, :] = zero_rows
    for b in range(nb):
        blk = y_ref[n - LANES * (b + 1):n - LANES * b, :]
        fold_ref[SUBLANES + LANES * b:SUBLANES + LANES * (b + 1), :] = jnp.dot(rev, blk, preferred_element_type=F32)
    mirrored = fold_ref[SUBLANES - 1:SUBLANES - 1 + h, :]
    y_lo = y_ref[0:h, :].astype(F32)
    yc_even = (y_lo[:, 0:df] + mirrored[:, 0:df]).astype(BF16)
    ys_odd = (y_lo[:, df:2 * df] - mirrored[:, df:2 * df]).astype(BF16)

    parity = lax.broadcasted_iota(jnp.int32, (h, 1), 0) & 1
    sign = jnp.where(parity == 1, -scale, scale)
    y_mid = y_ref[h:h + 1, 0:df].astype(F32)
    p = jnp.dot(ch_ref[...], yc_even, preferred_element_type=F32) + sign * y_mid
    q = jnp.dot(sh_ref[...], ys_odd, preferred_element_type=F32)
    lo = (p - q).astype(BF16)
    w = (p + q).astype(BF16)
    mid = jnp.sum(sign * (y_lo[:, 0:df] + y_ref[h:n, 0:df].astype(F32)), axis=0, keepdims=True)

    for b in range(nb):
        blk = w[h - LANES * (b + 1):h - LANES * b, :]
        rev_ref[SUBLANES + LANES * b:SUBLANES + LANES * (b + 1), :] = jnp.dot(rev, blk, preferred_element_type=F32)
    rev_ref[0:SUBLANES, :] = jnp.broadcast_to(mid, (SUBLANES, df))
    hi = rev_ref[SUBLANES - 1:SUBLANES - 1 + h, :].astype(BF16)
    z = jnp.concatenate([lo, hi], axis=0)
    o_ref[...] = jnp.dot(z, wf_ref[0].astype(BF16), preferred_element_type=F32).astype(BF16)


def _fourier_mix(yf, w_fourier, li, n, first_block):
    h = n // 2
    cn, sn = _dft_tables(n)
    ch = jnp.asarray(cn[:h, :h], dtype=F32).astype(BF16)
    sh = jnp.asarray(sn[:h, :h], dtype=F32).astype(BF16)
    anti = jnp.asarray(np.eye(LANES)[::-1].copy(), dtype=F32).astype(BF16)
    return pl.pallas_call(
        functools.partial(_fourier_kernel, n=n),
        name="fourier_mix",
        grid=(BATCH,),
        in_specs=[
            pl.BlockSpec((h, h), lambda b: (0, 0)),
            pl.BlockSpec((h, h), lambda b: (0, 0)),
            pl.BlockSpec((LANES, LANES), lambda b: (0, 0)),
            pl.BlockSpec((n, 2 * D_FOURIER), lambda b: (first_block + b, 0)),
            pl.BlockSpec((1, D_FOURIER, D_FOURIER), lambda b: (li, 0, 0)),
        ],
        out_specs=pl.BlockSpec((n, D_FOURIER), lambda b: (b, 0)),
        out_shape=jax.ShapeDtypeStruct((BATCH * n, D_FOURIER), BF16),
        scratch_shapes=[pltpu.VMEM((h + SUBLANES, 2 * D_FOURIER), F32), pltpu.VMEM((h + SUBLANES, D_FOURIER), F32)],
        compiler_params=_cparams("arbitrary"),
    )(ch, sh, anti, yf, w_fourier)


def _conv_kernel(cv_ref, wc_ref, o_ref, pad_ref):
    n = cv_ref.shape[0]
    gb = cv_ref[:, 0:D_CONV]
    g = cv_ref[:, D_CONV:2 * D_CONV] * cv_ref[:, 2 * D_CONV:3 * D_CONV]
    zero = jnp.zeros((SUBLANES, D_CONV), F32)
    pad_ref[0:SUBLANES, :] = zero
    pad_ref[n + SUBLANES:n + 2 * SUBLANES, :] = zero
    pad_ref[SUBLANES:n + SUBLANES, :] = g
    prev = pad_ref[SUBLANES - 1:n + SUBLANES - 1, :]
    nxt = pad_ref[SUBLANES + 1:n + SUBLANES + 1, :]
    w = wc_ref[0]
    o_ref[...] = (gb * (w[0:1, :] * prev + w[1:2, :] * g + w[2:3, :] * nxt)).astype(BF16)


def _short_conv(cv, w_conv, li, n, first_block):
    return pl.pallas_call(
        _conv_kernel,
        name="short_conv",
        grid=(BATCH,),
        in_specs=[
            pl.BlockSpec((n, 3 * D_CONV), lambda b: (first_block + b, 0)),
            pl.BlockSpec((1, 3, D_CONV), lambda b: (li, 0, 0)),
        ],
        out_specs=pl.BlockSpec((n, D_CONV), lambda b: (b, 0)),
        out_shape=jax.ShapeDtypeStruct((BATCH * n, D_CONV), BF16),
        scratch_shapes=[pltpu.VMEM((n + 2 * SUBLANES, D_CONV), F32)],
        compiler_params=_cparams("arbitrary"),
    )(cv, w_conv)


RPB_H = 2 * NA_KH - 1
RPB_W = 2 * NA_KW - 1
KEYS_LOC = NA_KH * GRID_W


def _bias_kernel(rpb_ref, o_ref):
    h = pl.program_id(0)
    qi = lax.broadcasted_iota(jnp.int32, (GRID_W, LANES), 0)
    lj = lax.broadcasted_iota(jnp.int32, (GRID_W, LANES), 1)
    kc = lj & (GRID_W - 1)
    hi = lj >= GRID_W
    d = kc - qi + (NA_KW - 1)
    cs = jnp.clip(qi - NA_KW // 2, 0, GRID_W - NA_KW)
    valid = (kc >= cs) & (kc < cs + NA_KW)
    tiles = []
    for a in range(RPB_H - 1):
        acc = jnp.zeros((GRID_W, LANES), F32)
        for b in range(RPB_W):
            va = rpb_ref[(h * RPB_H + a) * RPB_W + b]
            vb = rpb_ref[(h * RPB_H + a + 1) * RPB_W + b]
            acc = jnp.where(d == b, jnp.where(hi, vb, va), acc)
        tiles.append(jnp.where(valid, acc, NEG_INF))
    for cls in range(NA_KH):
        for m in range(NA_KH // 2):
            o_ref[0, cls, :, m * LANES:(m + 1) * LANES] = tiles[2 * m - cls + NA_KH - 1]


def _bias_table(rpb_flat):
    return pl.pallas_call(
        _bias_kernel,
        name="attn_bias_table",
        grid=(NA_HEADS,),
        in_specs=[pl.BlockSpec(memory_space=pltpu.SMEM)],
        out_specs=pl.BlockSpec((1, NA_KH, GRID_W, KEYS_LOC), lambda h: (h // 2, 0, h % 2, 0)),
        out_shape=jax.ShapeDtypeStruct((NA_HEADS // 2, NA_KH, 2 * GRID_W, KEYS_LOC), F32),
        compiler_params=_cparams("arbitrary"),
    )(rpb_flat)


_NT_DIMS = (((1,), (1,)), ((), ()))


def _softmax_pv(parts):
    m = None
    for s, _ in parts:
        ms = jnp.max(s, axis=1, keepdims=True)
        m = ms if m is None else jnp.maximum(m, ms)
    l = None
    o = None
    for s, v in parts:
        p = jnp.exp(s - m)
        ls = jnp.sum(p, axis=1, keepdims=True)
        os_ = jnp.dot(p.astype(BF16), v, preferred_element_type=F32)
        l = ls if l is None else l + ls
        o = os_ if o is None else o + os_
    return o / l


def _attn_latent_kernel(q_ref, k_ref, v_ref, kc_ref, vc_ref, tab_ref, o_ref, ve_ref, vce_ref):
    lane = lax.broadcasted_iota(jnp.int32, (GRID_W, LANES), 1)
    lo = lane < HEAD_DIM
    ve_ref[:, 0:LANES] = v_ref[...]
    ve_ref[:, LANES:2 * LANES] = jnp.ones((SEQ, LANES), BF16)
    vce_ref[:, 0:LANES] = vc_ref[...]
    vce_ref[:, LANES:2 * LANES] = jnp.ones((CTX_LEN, LANES), BF16)
    kc = kc_ref[...]
    vce = vce_ref[...]

    for r in range(ROWS):
        rs = min(max(r - NA_KH // 2, 0), ROWS - NA_KH)
        cls = r - rs
        q = q_ref[r * GRID_W:(r + 1) * GRID_W, :]
        kl = k_ref[rs * GRID_W:rs * GRID_W + KEYS_LOC, :]
        vl = ve_ref[rs * GRID_W:rs * GRID_W + KEYS_LOC, :]
        zero = jnp.zeros_like(q)
        q2 = jnp.concatenate([jnp.where(lo, q, zero), jnp.where(lo, zero, q)], axis=0)
        s1 = lax.dot_general(q2, kl, _NT_DIMS, preferred_element_type=F32) + tab_ref[0, cls]
        s2 = lax.dot_general(q2, kc, _NT_DIMS, preferred_element_type=F32)
        m = jnp.maximum(jnp.max(s1, axis=1, keepdims=True), jnp.max(s2, axis=1, keepdims=True))
        p1 = jnp.exp(s1 - m).astype(BF16)
        p2 = jnp.exp(s2 - m).astype(BF16)
        oe = (jnp.dot(p1, vl, preferred_element_type=F32) + jnp.dot(p2, vce, preferred_element_type=F32))
        o = oe[:, 0:LANES] / oe[:, LANES:2 * LANES]
        o_ref[r * GRID_W:(r + 1) * GRID_W, :] = jnp.where(lo, o[0:GRID_W], o[GRID_W:2 * GRID_W]).astype(BF16)


def _attn_latent(qkv, table):
    npair = NA_HEADS // 2
    cblk = T_LAT // CTX_LEN
    return pl.pallas_call(
        _attn_latent_kernel,
        name="attn_latent",
        grid=(BATCH, npair),
        in_specs=[
            pl.BlockSpec((SEQ, LANES), lambda b, p: (b, p)),
            pl.BlockSpec((SEQ, LANES), lambda b, p: (b, npair + p)),
            pl.BlockSpec((SEQ, LANES), lambda b, p: (b, 2 * npair + p)),
            pl.BlockSpec((CTX_LEN, LANES), lambda b, p: (cblk + b, npair + p)),
            pl.BlockSpec((CTX_LEN, LANES), lambda b, p: (cblk + b, 2 * npair + p)),
            pl.BlockSpec((1, NA_KH, 2 * GRID_W, KEYS_LOC), lambda b, p: (p, 0, 0, 0)),
        ],
        out_specs=pl.BlockSpec((SEQ, LANES), lambda b, p: (b, p)),
        out_shape=jax.ShapeDtypeStruct((T_LAT, D_NA), BF16),
        scratch_shapes=[pltpu.VMEM((SEQ, 2 * LANES), BF16), pltpu.VMEM((CTX_LEN, 2 * LANES), BF16)],
        compiler_params=_cparams("arbitrary", "arbitrary"),
    )(qkv, qkv, qkv, qkv, qkv, table)


def _attn_ctx_kernel(q_ref, k_ref, v_ref, o_ref):
    lane = lax.broadcasted_iota(jnp.int32, (CTX_LEN, LANES), 1)
    lo = lane < HEAD_DIM
    q = q_ref[...]
    k = k_ref[...]
    v = v_ref[...]
    outs = []
    for hh in range(2):
        qm = jnp.where(lo if hh == 0 else jnp.logical_not(lo), q, jnp.zeros_like(q))
        s = lax.dot_general(qm, k, _NT_DIMS, preferred_element_type=F32)
        outs.append(_softmax_pv([(s, v)]))
    o_ref[...] = jnp.where(lo, outs[0], outs[1]).astype(BF16)


def _attn_ctx(qkv):
    npair = NA_HEADS // 2
    cblk = T_LAT // CTX_LEN
    return pl.pallas_call(
        _attn_ctx_kernel,
        name="attn_context",
        grid=(BATCH, npair),
        in_specs=[
            pl.BlockSpec((CTX_LEN, LANES), lambda b, p: (cblk + b, p)),
            pl.BlockSpec((CTX_LEN, LANES), lambda b, p: (cblk + b, npair + p)),
            pl.BlockSpec((CTX_LEN, LANES), lambda b, p: (cblk + b, 2 * npair + p)),
        ],
        out_specs=pl.BlockSpec((CTX_LEN, LANES), lambda b, p: (b, p)),
        out_shape=jax.ShapeDtypeStruct((T_CTX, D_NA), BF16),
        compiler_params=_cparams("arbitrary", "arbitrary"),
    )(qkv, qkv, qkv)


ROUTE_COLS = N_GROUPS + N_EXPERTS


def _to_token_major(ref, val):
    rows = val.shape[0]
    for j in range(D_MODEL // LANES):
        ref[pl.ds(j, rows, stride=SUBLANES), :] = val[:, j * LANES:(j + 1) * LANES]


def _from_token_major(ref, rows, j):
    return ref[pl.ds(j, rows, stride=SUBLANES), :]


def _outproj_kernel(xa_ref, xb_ref, ffa_ref, ffb_ref, ata_ref, atb_ref, cva_ref, cvb_ref,
                    mod_ref, n2_ref, wo_ref, wr_ref, br_ref, x1_ref, hx_ref, route_ref):
    mix_in = jnp.concatenate([_pick_stream(ffa_ref, ffb_ref), _pick_stream(ata_ref, atb_ref),
                              _pick_stream(cva_ref, cvb_ref)], axis=-1)
    mix = jnp.dot(mix_in, wo_ref[0], preferred_element_type=F32)
    g1 = mod_ref[:, 2 * D_MODEL:3 * D_MODEL]
    x1 = _pick_stream(xa_ref, xb_ref) + g1 * mix
    x1_ref[...] = x1
    ms = jnp.mean(x1 * x1, axis=-1, keepdims=True)
    y = x1 * lax.rsqrt(ms + EPS) * n2_ref[0]
    sh2 = mod_ref[:, 3 * D_MODEL:4 * D_MODEL]
    sc2 = mod_ref[:, 4 * D_MODEL:5 * D_MODEL]
    hx = y * (1.0 + sc2) + sh2
    _to_token_major(hx_ref, hx)

    hx_hi = hx.astype(BF16)
    hx_lo = (hx - hx_hi.astype(F32)).astype(BF16)
    part = (jnp.dot(hx_hi, wr_ref[0], preferred_element_type=F32)
            + jnp.dot(hx_lo, wr_ref[0], preferred_element_type=F32))
    logits = part[:, 0:LANES] + part[:, LANES:2 * LANES] + br_ref[0]
    tm = logits.shape[0]
    lane = lax.broadcasted_iota(jnp.int32, (tm, LANES), 1)
    big = jnp.int32(LANES)
    gl = jnp.where(lane < N_GROUPS, logits, -jnp.inf)
    gmax = jnp.max(gl, axis=1, keepdims=True)
    gidx = jnp.min(jnp.where(gl == gmax, lane, big), axis=1, keepdims=True)
    g_w = 1.0 / jnp.sum(jnp.exp(gl - gmax), axis=1, keepdims=True)
    e_lane = lane - N_GROUPS
    in_group = (e_lane >= 0) & (e_lane < N_EXPERTS) & ((e_lane >> 3) == gidx)
    es = jnp.where(in_group, logits, -jnp.inf)
    t1 = jnp.max(es, axis=1, keepdims=True)
    i1 = jnp.min(jnp.where(es == t1, lane, big), axis=1, keepdims=True)
    es2 = jnp.where(lane == i1, -jnp.inf, es)
    t2 = jnp.max(es2, axis=1, keepdims=True)
    i2 = jnp.min(jnp.where(es2 == t2, lane, big), axis=1, keepdims=True)
    dlt = jnp.exp(t2 - t1)
    w1 = g_w / (1.0 + dlt)
    w2 = g_w * dlt / (1.0 + dlt)
    e1 = (i1 - N_GROUPS).astype(F32)
    e2 = (i2 - N_GROUPS).astype(F32)
    route_ref[...] = jnp.where(lane == 0, e1, jnp.where(lane == 1, e2,
                               jnp.where(lane == 2, w1, jnp.where(lane == 3, w2, 0.0))))


def _out_projection(x_pair, ctx_first_block, ff_pair, at_pair, cv_pair, mod4, norm2, w_out_bf, w_r, b_r, li, n_tok):
    nt = n_tok // TM
    return pl.pallas_call(
        _outproj_kernel,
        name="out_projection",
        grid=(nt,),
        in_specs=[
            _lat_spec(D_MODEL), _ctx_spec(D_MODEL, ctx_first_block),
            _lat_spec(D_FOURIER), _ctx_spec(D_FOURIER, 0),
            _lat_spec(D_NA), _ctx_spec(D_NA, 0),
            _lat_spec(D_CONV), _ctx_spec(D_CONV, 0),
            pl.BlockSpec((None, None, 1, 6 * D_MODEL), lambda i: (li, _mod_row(i, TM), 0, 0)),
            pl.BlockSpec((1, 1, D_MODEL), lambda i: (li, 0, 0)),
            pl.BlockSpec((1, D_MODEL, D_MODEL), lambda i: (li, 0, 0)),
            pl.BlockSpec((1, D_MODEL, 2 * LANES), lambda i: (li, 0, 0)),
            pl.BlockSpec((1, 1, LANES), lambda i: (li, 0, 0)),
        ],
        out_specs=[
            pl.BlockSpec((TM, D_MODEL), lambda i: (i, 0)),
            pl.BlockSpec((TM * SUBLANES, LANES), lambda i: (i, 0)),
            pl.BlockSpec((TM, LANES), lambda i: (i, 0)),
        ],
        out_shape=[
            jax.ShapeDtypeStruct((n_tok, D_MODEL), F32),
            jax.ShapeDtypeStruct((n_tok * SUBLANES, LANES), F32),
            jax.ShapeDtypeStruct((n_tok, LANES), F32),
        ],
        compiler_params=_cparams("arbitrary"),
    )(*x_pair, *ff_pair, *at_pair, *cv_pair, mod4, norm2.reshape(DEPTH, 1, D_MODEL), w_out_bf, w_r, b_r)


def _plan_kernel(route_ref, pos_ref, cnt_ref, ends_ref, carry_ref, offs_ref):
    ph = pl.program_id(0)
    i = pl.program_id(1)
    tm = PLAN_SUB
    lane = lax.broadcasted_iota(jnp.int32, (tm, LANES), 1)

    def one_hots(k):
        r = route_ref[k * tm:(k + 1) * tm, :]
        oh1 = lane == r[:, 0:1].astype(jnp.int32)
        oh2 = lane == r[:, 1:2].astype(jnp.int32)
        return oh1, oh2, jnp.where(oh1 | oh2, 1.0, 0.0)

    @pl.when((ph == 0) & (i == 0))
    def _():
        carry_ref[...] = jnp.zeros_like(carry_ref)

    @pl.when(ph == 0)
    def _():
        total = carry_ref[...]
        for k in range(PLAN_TM // tm):
            total = total + jnp.sum(one_hots(k)[2], axis=0, keepdims=True)
        carry_ref[...] = total

    @pl.when((ph == 1) & (i == 0))
    def _():
        cnt = carry_ref[...]
        cnt_ref[...] = jnp.broadcast_to(cnt, cnt_ref.shape)
        tiles = jnp.ceil(cnt * (1.0 / TMX))
        a = lax.broadcasted_iota(jnp.int32, (LANES, LANES), 0)
        b = lax.broadcasted_iota(jnp.int32, (LANES, LANES), 1)
        upper = jnp.where(a < b, 1.0, 0.0).astype(BF16)
        tiles8 = jnp.broadcast_to(tiles, (SUBLANES, LANES))
        first = jnp.dot(tiles8.astype(BF16), upper, preferred_element_type=F32)
        offs_ref[...] = first[0:1, :] * TMX
        ends_ref[...] = (first + tiles8) * TMX
        carry_ref[...] = jnp.zeros_like(carry_ref)

    @pl.when(ph == 1)
    def _():
        row = lax.broadcasted_iota(jnp.int32, (tm, tm), 0)
        col = lax.broadcasted_iota(jnp.int32, (tm, tm), 1)
        tri = jnp.where(row > col, 1.0, 0.0).astype(BF16)
        base = carry_ref[...] + offs_ref[...]
        for k in range(PLAN_TM // tm):
            oh1, oh2, oh = one_hots(k)
            cum = jnp.dot(tri, oh.astype(BF16), preferred_element_type=F32) + base
            p1 = jnp.sum(jnp.where(oh1, cum, 0.0), axis=1, keepdims=True)
            p2 = jnp.sum(jnp.where(oh2, cum, 0.0), axis=1, keepdims=True)
            pos_ref[k * tm:(k + 1) * tm, :] = (
                jnp.where(lane == 0, p1, jnp.where(lane == 1, p2, 0.0)).astype(jnp.int32))
            base = base + jnp.sum(oh, axis=0, keepdims=True)
        carry_ref[...] = base - offs_ref[...]


PLAN_TM = 2048
PLAN_SUB = 512


def _plan(route, n_tok):
    nt = n_tok // PLAN_TM
    return pl.pallas_call(
        _plan_kernel,
        name="moe_plan",
        grid=(2, nt),
        in_specs=[pl.BlockSpec((PLAN_TM, LANES), lambda ph, i: (i, 0))],
        out_specs=[
            pl.BlockSpec((PLAN_TM, LANES), lambda ph, i: (i * ph, 0)),
            pl.BlockSpec((SUBLANES, LANES), lambda ph, i: (0, 0)),
            pl.BlockSpec((SUBLANES, LANES), lambda ph, i: (0, 0)),
        ],
        out_shape=[
            jax.ShapeDtypeStruct((n_tok, LANES), jnp.int32),
            jax.ShapeDtypeStruct((SUBLANES, LANES), F32),
            jax.ShapeDtypeStruct((SUBLANES, LANES), F32),
        ],
        scratch_shapes=[pltpu.VMEM((1, LANES), F32), pltpu.VMEM((1, LANES), F32)],
        compiler_params=_cparams("arbitrary", "arbitrary"),
    )(route)


def _row_tile(ref, row):
    return ref.at[pl.ds(pl.multiple_of(row * SUBLANES, SUBLANES), SUBLANES), :]


DISPATCH_UNROLL = 8
PAD_CHUNKS = tuple(1 << b for b in reversed(range(TMX.bit_length() - 1)))


def _dispatch_kernel(pos_ref, pstart_ref, plen_ref, tail_ref, hx_ref, xs_ref, zbuf, sem, zsem):
    i = pl.program_id(0)

    def pad_copies(fn):
        def ebody(e, carry):
            n = plen_ref[e]
            off = pstart_ref[e]
            for rows in PAD_CHUNKS:
                @pl.when((n & rows) != 0)
                def _():
                    fn(pltpu.make_async_copy(
                        zbuf.at[pl.ds(0, rows * SUBLANES), :],
                        xs_ref.at[pl.ds(pl.multiple_of(off * SUBLANES, SUBLANES), rows * SUBLANES), :], zsem.at[0]))
                off = off + (n & rows)
            return carry

        lax.fori_loop(0, N_EXPERTS, ebody, 0)

        def tbody(k, carry):
            row = pl.multiple_of((tail_ref[0] + k * PAD_CHUNKS[0]) * SUBLANES, SUBLANES)
            fn(pltpu.make_async_copy(zbuf, xs_ref.at[pl.ds(row, PAD_CHUNKS[0] * SUBLANES), :], zsem.at[0]))
            return carry

        lax.fori_loop(0, tail_ref[1], tbody, 0)

    @pl.when(i == 0)
    def _():
        zbuf[...] = jnp.zeros_like(zbuf)
        pad_copies(lambda cp: cp.start())

    def body(k, carry):
        for u in range(DISPATCH_UNROLL):
            t = k * DISPATCH_UNROLL + u
            src = _row_tile(hx_ref, t)
            for j in range(2):
                pltpu.make_async_copy(src, _row_tile(xs_ref, pos_ref[(i * TD + t) * 2 + j]),
                                      sem.at[0]).start(priority=j)
        return carry

    lax.fori_loop(0, TD // DISPATCH_UNROLL, body, 0)
    nrow = TD * SUBLANES
    for j in range(2):
        pltpu.make_async_copy(hx_ref, xs_ref.at[pl.ds(0, nrow), :], sem.at[0]).wait()

    @pl.when(i == 0)
    def _():
        pad_copies(lambda cp: cp.wait())


def _dispatch(pos_flat, pad_start, pad_len, tail, hx_tm, n_tok, n_rows):
    return pl.pallas_call(
        _dispatch_kernel,
        name="moe_dispatch",
        grid_spec=pltpu.PrefetchScalarGridSpec(
            num_scalar_prefetch=4,
            grid=(n_tok // TD,),
            in_specs=[pl.BlockSpec((TD * SUBLANES, LANES), lambda i, pos, ps, pn, tl: (i, 0))],
            out_specs=pl.BlockSpec(memory_space=pl.ANY),
            scratch_shapes=[pltpu.VMEM((PAD_CHUNKS[0] * SUBLANES, LANES), F32),
                            pltpu.SemaphoreType.DMA((1,)), pltpu.SemaphoreType.DMA((1,))],
        ),
        out_shape=jax.ShapeDtypeStruct((n_rows * SUBLANES, LANES), F32),
        compiler_params=_cparams("arbitrary"),
    )(pos_flat, pad_start, pad_len, tail, hx_tm)


def _moe_kernel(te_ref, na_ref, first_ref, wslot_ref, nxt_ref, xs_ref, wg_hbm, wu_hbm, wd_hbm, os_ref,
                wg_f, wu_f, wd_f, wg_s, wu_s, wd_s, wsem, *, li):
    i = pl.program_id(0)
    active = i < na_ref[0]

    def weight_copies(e, slot):
        return [pltpu.make_async_copy(wg_hbm.at[li, e], wg_f.at[slot], wsem.at[slot, 0]),
                pltpu.make_async_copy(wu_hbm.at[li, e], wu_f.at[slot], wsem.at[slot, 1]),
                pltpu.make_async_copy(wd_hbm.at[li, e], wd_f.at[slot], wsem.at[slot, 2])]

    @pl.when(active & (first_ref[i] == 1))
    def _():
        slot = wslot_ref[i]

        @pl.when(i == 0)
        def _():
            for cp in weight_copies(te_ref[0], 0):
                cp.start()

        @pl.when(nxt_ref[i] >= 0)
        def _():
            for cp in weight_copies(nxt_ref[i], 1 - slot):
                cp.start()

        for cp in weight_copies(te_ref[i], slot):
            cp.wait()
        wg_s[...] = wg_f[slot].astype(BF16)
        wu_s[...] = wu_f[slot].astype(BF16)
        wd_s[...] = wd_f[slot].astype(BF16)

    @pl.when(active)
    def _():
        x = jnp.concatenate([_from_token_major(xs_ref, TMX, j).astype(BF16)
                             for j in range(D_MODEL // LANES)], axis=-1)
        g = jnp.dot(x, wg_s[...], preferred_element_type=F32)
        u = jnp.dot(x, wu_s[...], preferred_element_type=F32)
        h = (g * jax.nn.sigmoid(g) * u).astype(BF16)
        _to_token_major(os_ref, jnp.dot(h, wd_s[...], preferred_element_type=F32))

    @pl.when(jnp.logical_not(active))
    def _():
        os_ref[...] = jnp.zeros_like(os_ref)


def _moe(tile_expert, n_active, has_rows, xs, w_gate, w_up, w_down, li, n_tiles):
    eid = jnp.arange(N_EXPERTS, dtype=jnp.int32)
    later = jnp.where(has_rows[None, :] & (eid[None, :] > eid[:, None]), eid[None, :], N_EXPERTS)
    next_of_expert = jnp.min(later, axis=1)
    next_of_expert = jnp.where(next_of_expert == N_EXPERTS, -1, next_of_expert)
    first = jnp.concatenate([jnp.ones((1,), jnp.int32),
                             (tile_expert[1:] != tile_expert[:-1]).astype(jnp.int32)])
    wslot = (jnp.cumsum(first) - 1) % 2
    nxt = jnp.sum(jnp.where(tile_expert[:, None] == eid[None, :], next_of_expert[None, :], 0), axis=1)

    def row_map(i, te, na, fi, ws, nx):
        return (jnp.minimum(i, na[0] - 1), 0)

    return pl.pallas_call(
        functools.partial(_moe_kernel, li=li),
        name="moe_experts",
        grid_spec=pltpu.PrefetchScalarGridSpec(
            num_scalar_prefetch=5,
            grid=(n_tiles,),
            in_specs=[
                pl.BlockSpec((TMX * SUBLANES, LANES), row_map),
                pl.BlockSpec(memory_space=pl.ANY),
                pl.BlockSpec(memory_space=pl.ANY),
                pl.BlockSpec(memory_space=pl.ANY),
            ],
            out_specs=pl.BlockSpec((TMX * SUBLANES, LANES), lambda i, te, na, fi, ws, nx: (i, 0)),
            scratch_shapes=[
                pltpu.VMEM((2, D_MODEL, D_EXPERT), F32),
                pltpu.VMEM((2, D_MODEL, D_EXPERT), F32),
                pltpu.VMEM((2, D_EXPERT, D_MODEL), F32),
                pltpu.VMEM((D_MODEL, D_EXPERT), BF16),
                pltpu.VMEM((D_MODEL, D_EXPERT), BF16),
                pltpu.VMEM((D_EXPERT, D_MODEL), BF16),
                pltpu.SemaphoreType.DMA((2, 3)),
            ],
        ),
        out_shape=jax.ShapeDtypeStruct((n_tiles * TMX * SUBLANES, LANES), F32),
        compiler_params=_cparams("arbitrary"),
    )(tile_expert, n_active, first, wslot.astype(jnp.int32), nxt.astype(jnp.int32), xs, w_gate, w_up, w_down)


GATHER_UNROLL = 8


def _start_row_gathers(pos_ref, os_ref, buf, sem, tile, slot, n):
    def body(k, carry):
        for u in range(GATHER_UNROLL):
            t = k * GATHER_UNROLL + u
            for j in range(2):
                pltpu.make_async_copy(_row_tile(os_ref, pos_ref[(tile * n + t) * 2 + j]),
                                      _row_tile(buf.at[slot, j], t), sem.at[slot, j]).start(priority=j)
        return carry

    lax.fori_loop(0, n // GATHER_UNROLL, body, 0)


def _combined_tile(pos_ref, x1_ref, route_ref, mod_ref, os_ref, buf, sem, n):
    i = pl.program_id(0)
    slot = i % 2

    @pl.when(i == 0)
    def _():
        _start_row_gathers(pos_ref, os_ref, buf, sem, 0, 0, n)

    @pl.when(i + 1 < pl.num_programs(0))
    def _():
        _start_row_gathers(pos_ref, os_ref, buf, sem, i + 1, 1 - slot, n)

    for j in range(2):
        pltpu.make_async_copy(os_ref.at[pl.ds(0, n * SUBLANES), :], buf.at[slot, j], sem.at[slot, j]).wait()
    w1 = route_ref[:, 2:3]
    w2 = route_ref[:, 3:4]
    cols = []
    for c in range(D_MODEL // LANES):
        y = w1 * _from_token_major(buf.at[slot, 0], n, c) + w2 * _from_token_major(buf.at[slot, 1], n, c)
        g2 = mod_ref[:, 5 * D_MODEL + c * LANES:5 * D_MODEL + (c + 1) * LANES]
        cols.append(x1_ref[:, c * LANES:(c + 1) * LANES] + g2 * y)
    return jnp.concatenate(cols, axis=-1)


def _combine_final_kernel(pos_ref, x1_ref, route_ref, mod_ref, nf_ref, os_ref, o_ref, buf, sem):
    x2 = _combined_tile(pos_ref, x1_ref, route_ref, mod_ref, os_ref, buf, sem, TC)
    ms = jnp.mean(x2 * x2, axis=-1, keepdims=True)
    o_ref[...] = x2 * lax.rsqrt(ms + EPS) * nf_ref[...]


def _gather_scratch(n):
    return [pltpu.VMEM((2, 2, n * SUBLANES, LANES), F32), pltpu.SemaphoreType.DMA((2, 2))]


def _combine_inproj_kernel(pos_ref, x1_ref, route_ref, mod_ref, modn_ref, n1_ref, w_ref, dft_ref, os_ref,
                           x2_ref, yf_ref, qkv_ref, cv_ref, buf, sem):
    x2 = _combined_tile(pos_ref, x1_ref, route_ref, mod_ref, os_ref, buf, sem, TM)
    x2_ref[...] = x2
    _inproj_math(x2, modn_ref, n1_ref, w_ref, dft_ref, yf_ref, qkv_ref, cv_ref)


def _combine_inproj(pos_flat, x1, route, mod4, norm1, w_in_bf, dft_ch, osrt, li):
    def tmap(i, pos):
        return (i, 0)

    return pl.pallas_call(
        _combine_inproj_kernel,
        name="moe_combine_in_projection",
        grid_spec=pltpu.PrefetchScalarGridSpec(
            num_scalar_prefetch=1,
            grid=(T_ALL // TM,),
            in_specs=[
                pl.BlockSpec((TM, D_MODEL), tmap),
                pl.BlockSpec((TM, LANES), tmap),
                pl.BlockSpec((None, None, 1, 6 * D_MODEL), lambda i, pos: (li, _mod_row(i, TM), 0, 0)),
                pl.BlockSpec((None, None, 1, 6 * D_MODEL), lambda i, pos: (li + 1, _mod_row(i, TM), 0, 0)),
                pl.BlockSpec((1, 1, D_MODEL), lambda i, pos: (li + 1, 0, 0)),
                pl.BlockSpec((1, D_MODEL, D_IN_PROJ), lambda i, pos: (li + 1, 0, 0)),
                pl.BlockSpec((D_FOURIER, 2 * D_FOURIER), lambda i, pos: (0, 0)),
                pl.BlockSpec(memory_space=pl.ANY),
            ],
            out_specs=[
                pl.BlockSpec((TM, D_MODEL), tmap),
                pl.BlockSpec((TM, 2 * D_FOURIER), tmap),
                pl.BlockSpec((TM, 3 * D_NA), tmap),
                pl.BlockSpec((TM, 3 * D_CONV), tmap),
            ],
            scratch_shapes=_gather_scratch(TM),
        ),
        out_shape=[
            jax.ShapeDtypeStruct((T_ALL, D_MODEL), F32),
            jax.ShapeDtypeStruct((T_ALL, 2 * D_FOURIER), BF16),
            jax.ShapeDtypeStruct((T_ALL, 3 * D_NA), BF16),
            jax.ShapeDtypeStruct((T_ALL, 3 * D_CONV), F32),
        ],
        compiler_params=_cparams("arbitrary"),
    )(pos_flat, x1, route, mod4, mod4, norm1.reshape(DEPTH, 1, D_MODEL), w_in_bf, dft_ch, osrt)


def _combine_final(pos_flat, x1, route, mod4, norm_final, osrt, li, n_tok):
    def tmap(i, pos):
        return (i, 0)

    return pl.pallas_call(
        _combine_final_kernel,
        name="moe_combine_final",
        grid_spec=pltpu.PrefetchScalarGridSpec(
            num_scalar_prefetch=1,
            grid=(n_tok // TC,),
            in_specs=[
                pl.BlockSpec((TC, D_MODEL), tmap),
                pl.BlockSpec((TC, LANES), tmap),
                pl.BlockSpec((None, None, 1, 6 * D_MODEL), lambda i, pos: (li, _mod_row(i, TC), 0, 0)),
                pl.BlockSpec((1, D_MODEL), lambda i, pos: (0, 0)),
                pl.BlockSpec(memory_space=pl.ANY),
            ],
            out_specs=pl.BlockSpec((TC, D_MODEL), tmap),
            scratch_shapes=_gather_scratch(TC),
        ),
        out_shape=jax.ShapeDtypeStruct((n_tok, D_MODEL), F32),
        compiler_params=_cparams("arbitrary"),
    )(pos_flat, x1, route, mod4, norm_final.reshape(1, D_MODEL), osrt)


def _moe_block(hx_tm, route, w_gate, w_up, w_down, li, n_tok):
    n_tiles = (2 * n_tok) // TMX + N_EXPERTS
    n_rows = n_tiles * TMX
    pos, cnt, ends = _plan(route, n_tok)
    pos = pos[:, 0:2].reshape(-1)
    cnt = cnt[0, :N_EXPERTS].astype(jnp.int32)
    ends = ends[0, :N_EXPERTS].astype(jnp.int32)
    padded = ((cnt + TMX - 1) // TMX) * TMX
    pad_start = ends - padded + cnt
    tile_start = jnp.arange(n_tiles, dtype=jnp.int32) * TMX
    tile_expert = jnp.minimum(jnp.sum((tile_start[:, None] >= ends[None, :]).astype(jnp.int32), axis=1),
                              N_EXPERTS - 1)
    n_active = (ends[-1:] // TMX).astype(jnp.int32)
    tail = jnp.concatenate([ends[-1:], (n_rows - ends[-1:]) // PAD_CHUNKS[0]])

    xs = _dispatch(pos, pad_start, padded - cnt, tail, hx_tm, n_tok, n_rows)
    return pos, _moe(tile_expert, n_active, padded > 0, xs, w_gate, w_up, w_down, li, n_tiles)


def kernel(x, c, ctx, c_ctx, w_ada, b_ada, norm1, norm2, w_in, w_fourier, w_conv, rpb, w_out, w_rg, b_rg,
           w_re, b_re, w_gate, w_up, w_down, norm_final):
    x_pair, ctx_first = (x.reshape(T_LAT, D_MODEL), ctx.reshape(T_CTX, D_MODEL)), 0
    cc =jnp.concatenate([c, c_ctx[None, :], jnp.zeros((MOD_ROWS - BATCH - 1, D_MODEL), F32)], axis=0)
    mod4 = _modulation(cc, w_ada, b_ada).reshape(DEPTH, MOD_ROWS, 1, 6 * D_MODEL)

    w_in_bf = w_in.astype(BF16)
    w_out_bf = w_out.astype(BF16)
    pad = jnp.zeros((DEPTH, D_MODEL, LANES - ROUTE_COLS), F32)
    w_r = jnp.concatenate([w_rg, w_re, pad], axis=-1)
    w_r_hi = w_r.astype(BF16)
    w_r = jnp.concatenate([w_r_hi, (w_r - w_r_hi.astype(F32)).astype(BF16)], axis=-1)
    b_r = jnp.concatenate([b_rg, b_re, pad[:, 0, :]], axis=-1).reshape(DEPTH, 1, LANES)
    dft_ch = jnp.asarray(_channel_dft(), dtype=F32).astype(BF16)
    ctx_blk = T_LAT // CTX_LEN

    for li in range(DEPTH):
        last = li == DEPTH - 1
        if li == 0:
            yf, qkv, cv = _in_projection(*x_pair, ctx_first, mod4, norm1, w_in_bf, dft_ch, li)
        table = _bias_table(rpb[li].reshape(-1))
        ff = _fourier_mix(yf, w_fourier, li, SEQ, 0)
        cvo = _short_conv(cv, w_conv, li, SEQ, 0)
        at = _attn_latent(qkv, table)
        if last:
            n_tok = T_LAT
            ff_c, cvo_c, at_c = ff, cvo, at
        else:
            n_tok = T_ALL
            ff_c = _fourier_mix(yf, w_fourier, li, CTX_LEN, ctx_blk)
            cvo_c = _short_conv(cv, w_conv, li, CTX_LEN, ctx_blk)
            at_c = _attn_ctx(qkv)
        x1, hx_tm, route = _out_projection(x_pair, ctx_first, (ff, ff_c), (at, at_c), (cvo, cvo_c), mod4, norm2,
                                           w_out_bf, w_r, b_r, li, n_tok)
        pos, osrt = _moe_block(hx_tm, route, w_gate, w_up, w_down, li, n_tok)
        if last:
            out = _combine_final(pos, x1, route, mod4, norm_final, osrt, li, n_tok)
            return out.reshape(BATCH, SEQ, D_MODEL)
        xt, yf, qkv, cv = _combine_inproj(pos, x1, route, mod4, norm1, w_in_bf, dft_ch, osrt, li)
        x_pair, ctx_first = (xt, xt), NT_LAT
```

```python
import functools
import math

import numpy as np
import jax
import jax.numpy as jnp
from jax import lax
from jax.experimental import pallas as pl
from jax.experimental.pallas import tpu as pltpu

F32 = jnp.float32
BF16 = jnp.bfloat16

D_MODEL = 1024
BATCH = 8
SEQ = 2048
DEPTH = 2
GRID_W = 64
ROWS = SEQ // GRID_W
CTX_LEN = 256
T_LAT = BATCH * SEQ
T_CTX = BATCH * CTX_LEN
T_ALL = T_LAT + T_CTX

D_FOURIER = 256
D_FG = 64
HEAD_DIM = 64
NA_HEADS = 8
D_NA = NA_HEADS * HEAD_DIM
D_CONV = 256
D_IN_PROJ = D_FOURIER + 3 * D_NA + 3 * D_CONV
NA_KH = 8
NA_KW = 16
N_GROUPS = 4
EXPERTS_PER_GROUP = 8
N_EXPERTS = 32
D_EXPERT = 512
EPS = 1e-6
NEG_INF = -1e30

LANES = 128
SUBLANES = 8
MOD_ROWS = 16
TM = 512
TMX = 512
TD = 512
TC = 256
VMEM_LIMIT = 56 * 1024 * 1024


def _cparams(*sem):
    return pltpu.CompilerParams(dimension_semantics=sem, vmem_limit_bytes=VMEM_LIMIT)


def _ada_kernel(cc_ref, w_ref, b_ref, o_ref):
    s = cc_ref[...]
    s = s * jax.nn.sigmoid(s)
    acc = jnp.dot(s.astype(BF16), w_ref[0].astype(BF16), preferred_element_type=F32)
    o_ref[0] = acc + b_ref[0]


def _modulation(cc, w_ada, b_ada):
    tn = 1536
    nj = 6 * D_MODEL // tn
    return pl.pallas_call(
        _ada_kernel,
        name="ada_modulation",
        grid=(DEPTH, nj),
        in_specs=[
            pl.BlockSpec((MOD_ROWS, D_MODEL), lambda l, j: (0, 0)),
            pl.BlockSpec((1, D_MODEL, tn), lambda l, j: (l, 0, j)),
            pl.BlockSpec((1, 1, tn), lambda l, j: (l, 0, j)),
        ],
        out_specs=pl.BlockSpec((1, MOD_ROWS, tn), lambda l, j: (l, 0, j)),
        out_shape=jax.ShapeDtypeStruct((DEPTH, MOD_ROWS, 6 * D_MODEL), F32),
        compiler_params=_cparams("arbitrary", "arbitrary"),
    )(cc, w_ada, b_ada.reshape(DEPTH, 1, 6 * D_MODEL))


def _mod_row(i, tile):
    return jnp.minimum((i * tile) // SEQ, BATCH)


NT_LAT = T_LAT // TM


def _lat_spec(cols):
    return pl.BlockSpec((TM, cols), lambda i: (jnp.minimum(i, NT_LAT - 1), 0))


def _ctx_spec(cols, first_block):
    return pl.BlockSpec((TM, cols), lambda i: (jnp.maximum(i - NT_LAT, 0) + first_block, 0))


def _pick_stream(lat_ref, ctx_ref):
    return jnp.where(pl.program_id(0) >= NT_LAT, ctx_ref[...], lat_ref[...])


def _inproj_kernel(xa_ref, xb_ref, mod_ref, n1_ref, w_ref, dft_ref, yf_ref, qkv_ref, cv_ref):
    rows = TM // ROW_SPLIT
    for s in range(ROW_SPLIT):
        sl = pl.ds(s * rows, rows)
        _inproj_math(_pick_stream(xa_ref.at[sl], xb_ref.at[sl]), mod_ref, n1_ref, w_ref, dft_ref,
                     yf_ref.at[sl], qkv_ref.at[sl], cv_ref.at[sl])


def _inproj_math(x, mod_ref, n1_ref, w_ref, dft_ref, yf_ref, qkv_ref, cv_ref):
    ms = jnp.mean(x * x, axis=-1, keepdims=True)
    y = x * lax.rsqrt(ms + EPS) * n1_ref[0]
    sh = mod_ref[:, 0:D_MODEL]
    sc = mod_ref[:, D_MODEL:2 * D_MODEL]
    h = y * (1.0 + sc) + sh
    u = jnp.dot(h.astype(BF16), w_ref[0], preferred_element_type=F32)
    yf = jnp.dot(u[:, 0:D_FOURIER].astype(BF16), dft_ref[...], preferred_element_type=F32)
    yf_ref[...] = yf.astype(BF16)
    q0 = D_FOURIER
    qkv_ref[:, 0:D_NA] = (u[:, q0:q0 + D_NA] * (1.0 / math.sqrt(HEAD_DIM))).astype(BF16)
    qkv_ref[:, D_NA:3 * D_NA] = u[:, q0 + D_NA:q0 + 3 * D_NA].astype(BF16)
    cv_ref[...] = u[:, q0 + 3 * D_NA:D_IN_PROJ]


def _in_projection(x_lat, x_ctx, ctx_first_block, mod4, norm1, w_in_bf, dft_ch, li):
    nt = T_ALL // TM
    return pl.pallas_call(
        _inproj_kernel,
        name="in_projection",
        grid=(nt,),
        in_specs=[
            _lat_spec(D_MODEL),
            _ctx_spec(D_MODEL, ctx_first_block),
            pl.BlockSpec((None, None, 1, 6 * D_MODEL), lambda i: (li, _mod_row(i, TM), 0, 0)),
            pl.BlockSpec((1, 1, D_MODEL), lambda i: (li, 0, 0)),
            pl.BlockSpec((1, D_MODEL, D_IN_PROJ), lambda i: (li, 0, 0)),
            pl.BlockSpec((D_FOURIER, 2 * D_FOURIER), lambda i: (0, 0)),
        ],
        out_specs=[
            pl.BlockSpec((TM, 2 * D_FOURIER), lambda i: (i, 0)),
            pl.BlockSpec((TM, 3 * D_NA), lambda i: (i, 0)),
            pl.BlockSpec((TM, 3 * D_CONV), lambda i: (i, 0)),
        ],
        out_shape=[
            jax.ShapeDtypeStruct((T_ALL, 2 * D_FOURIER), BF16),
            jax.ShapeDtypeStruct((T_ALL, 3 * D_NA), BF16),
            jax.ShapeDtypeStruct((T_ALL, 3 * D_CONV), F32),
        ],
        compiler_params=_cparams("arbitrary"),
    )(x_lat, x_ctx, mod4, norm1.reshape(DEPTH, 1, D_MODEL), w_in_bf, dft_ch)


def _dft_tables(n):
    k = np.arange(n, dtype=np.int64)
    ang = 2.0 * np.pi * ((k[:, None] * k[None, :]) % n).astype(np.float64) / n
    s = 1.0 / math.sqrt(n)
    return np.cos(ang) * s, np.sin(ang) * s


def _channel_dft():
    c, s = _dft_tables(D_FG)
    eye = np.eye(D_FOURIER // D_FG)
    return np.concatenate([np.kron(eye, c), np.kron(eye, s)], axis=1)


def _fourier_kernel(ch_ref, sh_ref, j_ref, y_ref, wf_ref, o_ref, fold_ref, rev_ref, *, n):
    h = n // 2
    nb = h // LANES
    df = D_FOURIER
    scale = 1.0 / math.sqrt(n)
    rev = j_ref[...]
    zero_rows = jnp.zeros((SUBLANES, 2 * df), F32)

    fold_ref[0:SUBLANES, :] = zero_rows
    for b in range(nb):
        blk = y_ref[n - LANES * (b + 1):n - LANES * b, :]
        fold_ref[SUBLANES + LANES * b:SUBLANES + LANES * (b + 1), :] = jnp.dot(rev, blk, preferred_element_type=F32)
    mirrored = fold_ref[SUBLANES - 1:SUBLANES - 1 + h, :]
    y_lo = y_ref[0:h, :].astype(F32)
    yc_even = (y_lo[:, 0:df] + mirrored[:, 0:df]).astype(BF16)
    ys_odd = (y_lo[:, df:2 * df] - mirrored[:, df:2 * df]).astype(BF16)

    parity = lax.broadcasted_iota(jnp.int32, (h, 1), 0) & 1
    sign = jnp.where(parity == 1, -scale, scale)
    y_mid = y_ref[h:h + 1, 0:df].astype(F32)
    p = jnp.dot(ch_ref[...], yc_even, preferred_element_type=F32) + sign * y_mid
    q = jnp.dot(sh_ref[...], ys_odd, preferred_element_type=F32)
    lo = (p - q).astype(BF16)
    w = (p + q).astype(BF16)
    mid = jnp.sum(sign * (y_lo[:, 0:df] + y_ref[h:n, 0:df].astype(F32)), axis=0, keepdims=True)

    for b in range(nb):
        blk = w[h - LANES * (b + 1):h - LANES * b, :]
        rev_ref[SUBLANES + LANES * b:SUBLANES + LANES * (b + 1), :] = jnp.dot(rev, blk, preferred_element_type=F32)
    rev_ref[0:SUBLANES, :] = jnp.broadcast_to(mid, (SUBLANES, df))
    hi = rev_ref[SUBLANES - 1:SUBLANES - 1 + h, :].astype(BF16)
    z = jnp.concatenate([lo, hi], axis=0)
    o_ref[...] = jnp.dot(z, wf_ref[0].astype(BF16), preferred_element_type=F32).astype(BF16)


def _fourier_mix(yf, w_fourier, li, n, first_block):
    h = n // 2
    cn, sn = _dft_tables(n)
    ch = jnp.asarray(cn[:h, :h], dtype=F32).astype(BF16)
    sh = jnp.asarray(sn[:h, :h], dtype=F32).astype(BF16)
    anti = jnp.asarray(np.eye(LANES)[::-1].copy(), dtype=F32).astype(BF16)
    return pl.pallas_call(
        functools.partial(_fourier_kernel, n=n),
        name="fourier_mix",
        grid=(BATCH,),
        in_specs=[
            pl.BlockSpec((h, h), lambda b: (0, 0)),
            pl.BlockSpec((h, h), lambda b: (0, 0)),
            pl.BlockSpec((LANES, LANES), lambda b: (0, 0)),
            pl.BlockSpec((n, 2 * D_FOURIER), lambda b: (first_block + b, 0)),
            pl.BlockSpec((1, D_FOURIER, D_FOURIER), lambda b: (li, 0, 0)),
        ],
        out_specs=pl.BlockSpec((n, D_FOURIER), lambda b: (b, 0)),
        out_shape=jax.ShapeDtypeStruct((BATCH * n, D_FOURIER), BF16),
        scratch_shapes=[pltpu.VMEM((h + SUBLANES, 2 * D_FOURIER), F32), pltpu.VMEM((h + SUBLANES, D_FOURIER), F32)],
        compiler_params=_cparams("arbitrary"),
    )(ch, sh, anti, yf, w_fourier)


def _conv_kernel(cv_ref, wc_ref, o_ref, pad_ref):
    n = cv_ref.shape[0]
    gb = cv_ref[:, 0:D_CONV]
    g = cv_ref[:, D_CONV:2 * D_CONV] * cv_ref[:, 2 * D_CONV:3 * D_CONV]
    zero = jnp.zeros((SUBLANES, D_CONV), F32)
    pad_ref[0:SUBLANES, :] = zero
    pad_ref[n + SUBLANES:n + 2 * SUBLANES, :] = zero
    pad_ref[SUBLANES:n + SUBLANES, :] = g
    prev = pad_ref[SUBLANES - 1:n + SUBLANES - 1, :]
    nxt = pad_ref[SUBLANES + 1:n + SUBLANES + 1, :]
    w = wc_ref[0]
    o_ref[...] = (gb * (w[0:1, :] * prev + w[1:2, :] * g + w[2:3, :] * nxt)).astype(BF16)


def _short_conv(cv, w_conv, li, n, first_block):
    return pl.pallas_call(
        _conv_kernel,
        name="short_conv",
        grid=(BATCH,),
        in_specs=[
            pl.BlockSpec((n, 3 * D_CONV), lambda b: (first_block + b, 0)),
            pl.BlockSpec((1, 3, D_CONV), lambda b: (li, 0, 0)),
        ],
        out_specs=pl.BlockSpec((n, D_CONV), lambda b: (b, 0)),
        out_shape=jax.ShapeDtypeStruct((BATCH * n, D_CONV), BF16),
        scratch_shapes=[pltpu.VMEM((n + 2 * SUBLANES, D_CONV), F32)],
        compiler_params=_cparams("arbitrary"),
    )(cv, w_conv)


RPB_H = 2 * NA_KH - 1
RPB_W = 2 * NA_KW - 1
KEYS_LOC = NA_KH * GRID_W


def _bias_kernel(rpb_ref, o_ref):
    h = pl.program_id(0)
    qi = lax.broadcasted_iota(jnp.int32, (GRID_W, LANES), 0)
    lj = lax.broadcasted_iota(jnp.int32, (GRID_W, LANES), 1)
    kc = lj & (GRID_W - 1)
    hi = lj >= GRID_W
    d = kc - qi + (NA_KW - 1)
    cs = jnp.clip(qi - NA_KW // 2, 0, GRID_W - NA_KW)
    valid = (kc >= cs) & (kc < cs + NA_KW)
    tiles = []
    for a in range(RPB_H - 1):
        acc = jnp.zeros((GRID_W, LANES), F32)
        for b in range(RPB_W):
            va = rpb_ref[(h * RPB_H + a) * RPB_W + b]
            vb = rpb_ref[(h * RPB_H + a + 1) * RPB_W + b]
            acc = jnp.where(d == b, jnp.where(hi, vb, va), acc)
        tiles.append(jnp.where(valid, acc, NEG_INF))
    for cls in range(NA_KH):
        for m in range(NA_KH // 2):
            o_ref[0, cls, :, m * LANES:(m + 1) * LANES] = tiles[2 * m - cls + NA_KH - 1]


def _bias_table(rpb_flat):
    return pl.pallas_call(
        _bias_kernel,
        name="attn_bias_table",
        grid=(NA_HEADS,),
        in_specs=[pl.BlockSpec(memory_space=pltpu.SMEM)],
        out_specs=pl.BlockSpec((1, NA_KH, GRID_W, KEYS_LOC), lambda h: (h // 2, 0, h % 2, 0)),
        out_shape=jax.ShapeDtypeStruct((NA_HEADS // 2, NA_KH, 2 * GRID_W, KEYS_LOC), F32),
        compiler_params=_cparams("arbitrary"),
    )(rpb_flat)


_NT_DIMS = (((1,), (1,)), ((), ()))


def _softmax_pv(parts):
    m = None
    for s, _ in parts:
        ms = jnp.max(s, axis=1, keepdims=True)
        m = ms if m is None else jnp.maximum(m, ms)
    l = None
    o = None
    for s, v in parts:
        p = jnp.exp(s - m)
        ls = jnp.sum(p, axis=1, keepdims=True)
        os_ = jnp.dot(p.astype(BF16), v, preferred_element_type=F32)
        l = ls if l is None else l + ls
        o = os_ if o is None else o + os_
    return o / l


def _attn_latent_kernel(q_ref, k_ref, v_ref, kc_ref, vc_ref, tab_ref, o_ref, ve_ref, vce_ref):
    lane = lax.broadcasted_iota(jnp.int32, (GRID_W, LANES), 1)
    lo = lane < HEAD_DIM
    ve_ref[:, 0:LANES] = v_ref[...]
    ve_ref[:, LANES:2 * LANES] = jnp.ones((SEQ, LANES), BF16)
    vce_ref[:, 0:LANES] = vc_ref[...]
    vce_ref[:, LANES:2 * LANES] = jnp.ones((CTX_LEN, LANES), BF16)
    kc = kc_ref[...]
    vce = vce_ref[...]

    for r in range(ROWS):
        rs = min(max(r - NA_KH // 2, 0), ROWS - NA_KH)
        cls = r - rs
        q = q_ref[r * GRID_W:(r + 1) * GRID_W, :]
        kl = k_ref[rs * GRID_W:rs * GRID_W + KEYS_LOC, :]
        vl = ve_ref[rs * GRID_W:rs * GRID_W + KEYS_LOC, :]
        zero = jnp.zeros_like(q)
        q2 = jnp.concatenate([jnp.where(lo, q, zero), jnp.where(lo, zero, q)], axis=0)
        s1 = lax.dot_general(q2, kl, _NT_DIMS, preferred_element_type=F32) + tab_ref[0, cls]
        s2 = lax.dot_general(q2, kc, _NT_DIMS, preferred_element_type=F32)
        m = jnp.maximum(jnp.max(s1, axis=1, keepdims=True), jnp.max(s2, axis=1, keepdims=True))
        p1 = jnp.exp(s1 - m).astype(BF16)
        p2 = jnp.exp(s2 - m).astype(BF16)
        oe = (jnp.dot(p1, vl, preferred_element_type=F32) + jnp.dot(p2, vce, preferred_element_type=F32))
        o = oe[:, 0:LANES] / oe[:, LANES:2 * LANES]
        o_ref[r * GRID_W:(r + 1) * GRID_W, :] = jnp.where(lo, o[0:GRID_W], o[GRID_W:2 * GRID_W]).astype(BF16)


def _attn_latent(qkv, table):
    npair = NA_HEADS // 2
    cblk = T_LAT // CTX_LEN
    return pl.pallas_call(
        _attn_latent_kernel,
        name="attn_latent",
        grid=(BATCH, npair),
        in_specs=[
            pl.BlockSpec((SEQ, LANES), lambda b, p: (b, p)),
            pl.BlockSpec((SEQ, LANES), lambda b, p: (b, npair + p)),
            pl.BlockSpec((SEQ, LANES), lambda b, p: (b, 2 * npair + p)),
            pl.BlockSpec((CTX_LEN, LANES), lambda b, p: (cblk + b, npair + p)),
            pl.BlockSpec((CTX_LEN, LANES), lambda b, p: (cblk + b, 2 * npair + p)),
            pl.BlockSpec((1, NA_KH, 2 * GRID_W, KEYS_LOC), lambda b, p: (p, 0, 0, 0)),
        ],
        out_specs=pl.BlockSpec((SEQ, LANES), lambda b, p: (b, p)),
        out_shape=jax.ShapeDtypeStruct((T_LAT, D_NA), BF16),
        scratch_shapes=[pltpu.VMEM((SEQ, 2 * LANES), BF16), pltpu.VMEM((CTX_LEN, 2 * LANES), BF16)],
        compiler_params=_cparams("arbitrary", "arbitrary"),
    )(qkv, qkv, qkv, qkv, qkv, table)


def _attn_ctx_kernel(q_ref, k_ref, v_ref, o_ref):
    lane = lax.broadcasted_iota(jnp.int32, (CTX_LEN, LANES), 1)
    lo = lane < HEAD_DIM
    q = q_ref[...]
    k = k_ref[...]
    v = v_ref[...]
    outs = []
    for hh in range(2):
        qm = jnp.where(lo if hh == 0 else jnp.logical_not(lo), q, jnp.zeros_like(q))
        s = lax.dot_general(qm, k, _NT_DIMS, preferred_element_type=F32)
        outs.append(_softmax_pv([(s, v)]))
    o_ref[...] = jnp.where(lo, outs[0], outs[1]).astype(BF16)


def _attn_ctx(qkv):
    npair = NA_HEADS // 2
    cblk = T_LAT // CTX_LEN
    return pl.pallas_call(
        _attn_ctx_kernel,
        name="attn_context",
        grid=(BATCH, npair),
        in_specs=[
            pl.BlockSpec((CTX_LEN, LANES), lambda b, p: (cblk + b, p)),
            pl.BlockSpec((CTX_LEN, LANES), lambda b, p: (cblk + b, npair + p)),
            pl.BlockSpec((CTX_LEN, LANES), lambda b, p: (cblk + b, 2 * npair + p)),
        ],
        out_specs=pl.BlockSpec((CTX_LEN, LANES), lambda b, p: (b, p)),
        out_shape=jax.ShapeDtypeStruct((T_CTX, D_NA), BF16),
        compiler_params=_cparams("arbitrary", "arbitrary"),
    )(qkv, qkv, qkv)


ROUTE_COLS = N_GROUPS + N_EXPERTS


def _to_token_major(ref, val):
    rows = val.shape[0]
    for j in range(D_MODEL // LANES):
        ref[pl.ds(j, rows, stride=SUBLANES), :] = val[:, j * LANES:(j + 1) * LANES]


def _from_token_major(ref, rows, j):
    return ref[pl.ds(j, rows, stride=SUBLANES), :]


OUT_SPLIT = 2


def _outproj_kernel(xa_ref, xb_ref, ffa_ref, ffb_ref, ata_ref, atb_ref, cva_ref, cvb_ref,
                    mod_ref, n2_ref, wo_ref, wr_ref, br_ref, x1_ref, hx_ref, route_ref):
    rows = TM // OUT_SPLIT
    for s in range(OUT_SPLIT):
        sl = pl.ds(s * rows, rows)
        ins = [r.at[sl] for r in (xa_ref, xb_ref, ffa_ref, ffb_ref, ata_ref, atb_ref, cva_ref, cvb_ref)]
        _outproj_rows(*ins, mod_ref, n2_ref, wo_ref, wr_ref, br_ref, x1_ref.at[sl],
                      hx_ref.at[pl.ds(s * rows * SUBLANES, rows * SUBLANES)], route_ref.at[sl])


def _outproj_rows(xa_ref, xb_ref, ffa_ref, ffb_ref, ata_ref, atb_ref, cva_ref, cvb_ref,
                  mod_ref, n2_ref, wo_ref, wr_ref, br_ref, x1_ref, hx_ref, route_ref):
    r1 = D_FOURIER
    r2 = D_FOURIER + D_NA
    mix = (jnp.dot(_pick_stream(ffa_ref, ffb_ref), wo_ref[0, 0:r1, :], preferred_element_type=F32)
           + jnp.dot(_pick_stream(ata_ref, atb_ref), wo_ref[0, r1:r2, :], preferred_element_type=F32)
           + jnp.dot(_pick_stream(cva_ref, cvb_ref), wo_ref[0, r2:D_MODEL, :], preferred_element_type=F32))
    g1 = mod_ref[:, 2 * D_MODEL:3 * D_MODEL]
    x1 = _pick_stream(xa_ref, xb_ref) + g1 * mix
    x1_ref[...] = x1
    ms = jnp.mean(x1 * x1, axis=-1, keepdims=True)
    y = x1 * lax.rsqrt(ms + EPS) * n2_ref[0]
    sh2 = mod_ref[:, 3 * D_MODEL:4 * D_MODEL]
    sc2 = mod_ref[:, 4 * D_MODEL:5 * D_MODEL]
    hx = y * (1.0 + sc2) + sh2
    _to_token_major(hx_ref, hx)

    hx_hi = hx.astype(BF16)
    hx_lo = (hx - hx_hi.astype(F32)).astype(BF16)
    part = (jnp.dot(hx_hi, wr_ref[0], preferred_element_type=F32)
            + jnp.dot(hx_lo, wr_ref[0], preferred_element_type=F32))
    logits = part[:, 0:LANES] + part[:, LANES:2 * LANES] + br_ref[0]
    tm = logits.shape[0]
    lane = lax.broadcasted_iota(jnp.int32, (tm, LANES), 1)
    lane_f = lane.astype(F32)
    group_of_lane = ((lane - N_GROUPS) >> 3).astype(F32)
    big = float(LANES)
    gl = jnp.where(lane < N_GROUPS, logits, -jnp.inf)
    gmax = jnp.max(gl, axis=1, keepdims=True)
    gidx = jnp.min(jnp.where(gl == gmax, lane_f, big), axis=1, keepdims=True)
    g_w = 1.0 / jnp.sum(jnp.exp(gl - gmax), axis=1, keepdims=True)
    in_group = (lane >= N_GROUPS) & (lane < N_GROUPS + N_EXPERTS) & (group_of_lane == gidx)
    es = jnp.where(in_group, logits, -jnp.inf)
    t1 = jnp.max(es, axis=1, keepdims=True)
    i1 = jnp.min(jnp.where(es == t1, lane_f, big), axis=1, keepdims=True)
    es2 = jnp.where(lane_f == i1, -jnp.inf, es)
    t2 = jnp.max(es2, axis=1, keepdims=True)
    i2 = jnp.min(jnp.where(es2 == t2, lane_f, big), axis=1, keepdims=True)
    dlt = jnp.exp(t2 - t1)
    w1 = g_w / (1.0 + dlt)
    w2 = g_w * dlt / (1.0 + dlt)
    e1 = i1 - float(N_GROUPS)
    e2 = i2 - float(N_GROUPS)
    route_ref[...] = jnp.where(lane == 0, e1, jnp.where(lane == 1, e2,
                               jnp.where(lane == 2, w1, jnp.where(lane == 3, w2, 0.0))))


def _out_projection(x_pair, ctx_first_block, ff_pair, at_pair, cv_pair, mod4, norm2, w_out_bf, w_r, b_r, li, n_tok):
    nt = n_tok // TM
    return pl.pallas_call(
        _outproj_kernel,
        name="out_projection",
        grid=(nt,),
        in_specs=[
            _lat_spec(D_MODEL), _ctx_spec(D_MODEL, ctx_first_block),
            _lat_spec(D_FOURIER), _ctx_spec(D_FOURIER, 0),
            _lat_spec(D_NA), _ctx_spec(D_NA, 0),
            _lat_spec(D_CONV), _ctx_spec(D_CONV, 0),
            pl.BlockSpec((None, None, 1, 6 * D_MODEL), lambda i: (li, _mod_row(i, TM), 0, 0)),
            pl.BlockSpec((1, 1, D_MODEL), lambda i: (li, 0, 0)),
            pl.BlockSpec((1, D_MODEL, D_MODEL), lambda i: (li, 0, 0)),
            pl.BlockSpec((1, D_MODEL, 2 * LANES), lambda i: (li, 0, 0)),
            pl.BlockSpec((1, 1, LANES), lambda i: (li, 0, 0)),
        ],
        out_specs=[
            pl.BlockSpec((TM, D_MODEL), lambda i: (i, 0)),
            pl.BlockSpec((TM * SUBLANES, LANES), lambda i: (i, 0)),
            pl.BlockSpec((TM, LANES), lambda i: (i, 0)),
        ],
        out_shape=[
            jax.ShapeDtypeStruct((n_tok, D_MODEL), F32),
            jax.ShapeDtypeStruct((n_tok * SUBLANES, LANES), F32),
            jax.ShapeDtypeStruct((n_tok, LANES), F32),
        ],
        compiler_params=_cparams("arbitrary"),
    )(*x_pair, *ff_pair, *at_pair, *cv_pair, mod4, norm2.reshape(DEPTH, 1, D_MODEL), w_out_bf, w_r, b_r)


def _plan_kernel(route_ref, pos_ref, cnt_ref, ends_ref, carry_ref, offs_ref):
    ph = pl.program_id(0)
    i = pl.program_id(1)
    tm = PLAN_SUB
    lane = lax.broadcasted_iota(jnp.int32, (tm, LANES), 1)

    def one_hots(k):
        r = route_ref[k * tm:(k + 1) * tm, :]
        oh1 = lane == r[:, 0:1].astype(jnp.int32)
        oh2 = lane == r[:, 1:2].astype(jnp.int32)
        return oh1, oh2, jnp.where(oh1 | oh2, 1.0, 0.0)

    @pl.when((ph == 0) & (i == 0))
    def _():
        carry_ref[...] = jnp.zeros_like(carry_ref)

    @pl.when(ph == 0)
    def _():
        total = carry_ref[...]
        for k in range(PLAN_TM // tm):
            total = total + jnp.sum(one_hots(k)[2], axis=0, keepdims=True)
        carry_ref[...] = total

    @pl.when((ph == 1) & (i == 0))
    def _():
        cnt = carry_ref[...]
        cnt_ref[...] = jnp.broadcast_to(cnt, cnt_ref.shape)
        tiles = jnp.ceil(cnt * (1.0 / TMX))
        a = lax.broadcasted_iota(jnp.int32, (LANES, LANES), 0)
        b = lax.broadcasted_iota(jnp.int32, (LANES, LANES), 1)
        upper = jnp.where(a < b, 1.0, 0.0).astype(BF16)
        tiles8 = jnp.broadcast_to(tiles, (SUBLANES, LANES))
        first = jnp.dot(tiles8.astype(BF16), upper, preferred_element_type=F32)
        offs_ref[...] = first[0:1, :] * TMX
        ends_ref[...] = (first + tiles8) * TMX
        carry_ref[...] = jnp.zeros_like(carry_ref)

    @pl.when(ph == 1)
    def _():
        row = lax.broadcasted_iota(jnp.int32, (tm, tm), 0)
        col = lax.broadcasted_iota(jnp.int32, (tm, tm), 1)
        tri = jnp.where(row > col, 1.0, 0.0).astype(BF16)
        base = carry_ref[...] + offs_ref[...]
        for k in range(PLAN_TM // tm):
            oh1, oh2, oh = one_hots(k)
            cum = jnp.dot(tri, oh.astype(BF16), preferred_element_type=F32) + base
            p1 = jnp.sum(jnp.where(oh1, cum, 0.0), axis=1, keepdims=True)
            p2 = jnp.sum(jnp.where(oh2, cum, 0.0), axis=1, keepdims=True)
            pos_ref[k * tm:(k + 1) * tm, :] = (
                jnp.where(lane == 0, p1, jnp.where(lane == 1, p2, 0.0)).astype(jnp.int32))
            base = base + jnp.sum(oh, axis=0, keepdims=True)
        carry_ref[...] = base - offs_ref[...]


PLAN_TM = 2048
PLAN_SUB = 512


def _plan(route, n_tok):
    nt = n_tok // PLAN_TM
    return pl.pallas_call(
        _plan_kernel,
        name="moe_plan",
        grid=(2, nt),
        in_specs=[pl.BlockSpec((PLAN_TM, LANES), lambda ph, i: (i, 0))],
        out_specs=[
            pl.BlockSpec((PLAN_TM, LANES), lambda ph, i: (i * ph, 0)),
            pl.BlockSpec((SUBLANES, LANES), lambda ph, i: (0, 0)),
            pl.BlockSpec((SUBLANES, LANES), lambda ph, i: (0, 0)),
        ],
        out_shape=[
            jax.ShapeDtypeStruct((n_tok, LANES), jnp.int32),
            jax.ShapeDtypeStruct((SUBLANES, LANES), F32),
            jax.ShapeDtypeStruct((SUBLANES, LANES), F32),
        ],
        scratch_shapes=[pltpu.VMEM((1, LANES), F32), pltpu.VMEM((1, LANES), F32)],
        compiler_params=_cparams("arbitrary", "arbitrary"),
    )(route)


def _row_tile(ref, row):
    return ref.at[pl.ds(pl.multiple_of(row * SUBLANES, SUBLANES), SUBLANES), :]


DISPATCH_UNROLL = 8
PAD_CHUNKS = tuple(1 << b for b in reversed(range(TMX.bit_length() - 1)))


def _dispatch_kernel(pos_ref, pstart_ref, plen_ref, tail_ref, hx_ref, xs_ref, zbuf, sem, zsem):
    i = pl.program_id(0)

    def pad_copies(fn):
        def ebody(e, carry):
            n = plen_ref[e]
            off = pstart_ref[e]
            for rows in PAD_CHUNKS:
                @pl.when((n & rows) != 0)
                def _():
                    fn(pltpu.make_async_copy(
                        zbuf.at[pl.ds(0, rows * SUBLANES), :],
                        xs_ref.at[pl.ds(pl.multiple_of(off * SUBLANES, SUBLANES), rows * SUBLANES), :], zsem.at[0]))
                off = off + (n & rows)
            return carry

        lax.fori_loop(0, N_EXPERTS, ebody, 0)

        def tbody(k, carry):
            row = pl.multiple_of((tail_ref[0] + k * PAD_CHUNKS[0]) * SUBLANES, SUBLANES)
            fn(pltpu.make_async_copy(zbuf, xs_ref.at[pl.ds(row, PAD_CHUNKS[0] * SUBLANES), :], zsem.at[0]))
            return carry

        lax.fori_loop(0, tail_ref[1], tbody, 0)

    @pl.when(i == 0)
    def _():
        zbuf[...] = jnp.zeros_like(zbuf)
        pad_copies(lambda cp: cp.start())

    def body(k, carry):
        for u in range(DISPATCH_UNROLL):
            t = k * DISPATCH_UNROLL + u
            src = _row_tile(hx_ref, t)
            for j in range(2):
                pltpu.make_async_copy(src, _row_tile(xs_ref, pos_ref[(i * TD + t) * 2 + j]),
                                      sem.at[0]).start(priority=j)
        return carry

    lax.fori_loop(0, TD // DISPATCH_UNROLL, body, 0)
    nrow = TD * SUBLANES
    for j in range(2):
        pltpu.make_async_copy(hx_ref, xs_ref.at[pl.ds(0, nrow), :], sem.at[0]).wait()

    @pl.when(i == 0)
    def _():
        pad_copies(lambda cp: cp.wait())


def _dispatch(pos_flat, pad_start, pad_len, tail, hx_tm, n_tok, n_rows):
    return pl.pallas_call(
        _dispatch_kernel,
        name="moe_dispatch",
        grid_spec=pltpu.PrefetchScalarGridSpec(
            num_scalar_prefetch=4,
            grid=(n_tok // TD,),
            in_specs=[pl.BlockSpec((TD * SUBLANES, LANES), lambda i, pos, ps, pn, tl: (i, 0))],
            out_specs=pl.BlockSpec(memory_space=pl.ANY),
            scratch_shapes=[pltpu.VMEM((PAD_CHUNKS[0] * SUBLANES, LANES), F32),
                            pltpu.SemaphoreType.DMA((1,)), pltpu.SemaphoreType.DMA((1,))],
        ),
        out_shape=jax.ShapeDtypeStruct((n_rows * SUBLANES, LANES), F32),
        compiler_params=_cparams("arbitrary"),
    )(pos_flat, pad_start, pad_len, tail, hx_tm)


def _moe_kernel(te_ref, na_ref, first_ref, wslot_ref, nxt_ref, xs_ref, wg_hbm, wu_hbm, wd_hbm, os_ref,
                wg_f, wu_f, wd_f, wg_s, wu_s, wd_s, wsem, *, li):
    i = pl.program_id(0)
    active = i < na_ref[0]

    def weight_copies(e, slot):
        return [pltpu.make_async_copy(wg_hbm.at[li, e], wg_f.at[slot], wsem.at[slot, 0]),
                pltpu.make_async_copy(wu_hbm.at[li, e], wu_f.at[slot], wsem.at[slot, 1]),
                pltpu.make_async_copy(wd_hbm.at[li, e], wd_f.at[slot], wsem.at[slot, 2])]

    @pl.when(active & (first_ref[i] == 1))
    def _():
        slot = wslot_ref[i]

        @pl.when(i == 0)
        def _():
            for cp in weight_copies(te_ref[0], 0):
                cp.start()

        @pl.when(nxt_ref[i] >= 0)
        def _():
            for cp in weight_copies(nxt_ref[i], 1 - slot):
                cp.start()

        for cp in weight_copies(te_ref[i], slot):
            cp.wait()
        wg_s[...] = wg_f[slot].astype(BF16)
        wu_s[...] = wu_f[slot].astype(BF16)
        wd_s[...] = wd_f[slot].astype(BF16)

    @pl.when(active)
    def _():
        x = jnp.concatenate([_from_token_major(xs_ref, TMX, j).astype(BF16)
                             for j in range(D_MODEL // LANES)], axis=-1)
        g = jnp.dot(x, wg_s[...], preferred_element_type=F32)
        u = jnp.dot(x, wu_s[...], preferred_element_type=F32)
        h = (g * jax.nn.sigmoid(g) * u).astype(BF16)
        _to_token_major(os_ref, jnp.dot(h, wd_s[...], preferred_element_type=F32))

    @pl.when(jnp.logical_not(active))
    def _():
        os_ref[...] = jnp.zeros_like(os_ref)


def _moe(tile_expert, n_active, has_rows, xs, w_gate, w_up, w_down, li, n_tiles):
    eid = jnp.arange(N_EXPERTS, dtype=jnp.int32)
    later = jnp.where(has_rows[None, :] & (eid[None, :] > eid[:, None]), eid[None, :], N_EXPERTS)
    next_of_expert = jnp.min(later, axis=1)
    next_of_expert = jnp.where(next_of_expert == N_EXPERTS, -1, next_of_expert)
    first = jnp.concatenate([jnp.ones((1,), jnp.int32),
                             (tile_expert[1:] != tile_expert[:-1]).astype(jnp.int32)])
    wslot = (jnp.cumsum(first) - 1) % 2
    nxt = jnp.sum(jnp.where(tile_expert[:, None] == eid[None, :], next_of_expert[None, :], 0), axis=1)

    def row_map(i, te, na, fi, ws, nx):
        return (jnp.minimum(i, na[0] - 1), 0)

    return pl.pallas_call(
        functools.partial(_moe_kernel, li=li),
        name="moe_experts",
        grid_spec=pltpu.PrefetchScalarGridSpec(
            num_scalar_prefetch=5,
            grid=(n_tiles,),
            in_specs=[
                pl.BlockSpec((TMX * SUBLANES, LANES), row_map),
                pl.BlockSpec(memory_space=pl.ANY),
                pl.BlockSpec(memory_space=pl.ANY),
                pl.BlockSpec(memory_space=pl.ANY),
            ],
            out_specs=pl.BlockSpec((TMX * SUBLANES, LANES), lambda i, te, na, fi, ws, nx: (i, 0)),
            scratch_shapes=[
                pltpu.VMEM((2, D_MODEL, D_EXPERT), F32),
                pltpu.VMEM((2, D_MODEL, D_EXPERT), F32),
                pltpu.VMEM((2, D_EXPERT, D_MODEL), F32),
                pltpu.VMEM((D_MODEL, D_EXPERT), BF16),
                pltpu.VMEM((D_MODEL, D_EXPERT), BF16),
                pltpu.VMEM((D_EXPERT, D_MODEL), BF16),
                pltpu.SemaphoreType.DMA((2, 3)),
            ],
        ),
        out_shape=jax.ShapeDtypeStruct((n_tiles * TMX * SUBLANES, LANES), F32),
        compiler_params=_cparams("arbitrary"),
    )(tile_expert, n_active, first, wslot.astype(jnp.int32), nxt.astype(jnp.int32), xs, w_gate, w_up, w_down)


GATHER_UNROLL = 8


def _start_row_gathers(pos_ref, os_ref, buf, sem, tile, slot, n):
    def body(k, carry):
        for u in range(GATHER_UNROLL):
            t = k * GATHER_UNROLL + u
            for j in range(2):
                pltpu.make_async_copy(_row_tile(os_ref, pos_ref[(tile * n + t) * 2 + j]),
                                      _row_tile(buf.at[slot, j], t), sem.at[slot, j]).start(priority=j)
        return carry

    lax.fori_loop(0, n // GATHER_UNROLL, body, 0)


def _rows_ready(pos_ref, os_ref, buf, sem, n):
    i = pl.program_id(0)
    slot = i % 2

    @pl.when(i == 0)
    def _():
        _start_row_gathers(pos_ref, os_ref, buf, sem, 0, 0, n)

    @pl.when(i + 1 < pl.num_programs(0))
    def _():
        _start_row_gathers(pos_ref, os_ref, buf, sem, i + 1, 1 - slot, n)

    for j in range(2):
        pltpu.make_async_copy(os_ref.at[pl.ds(0, n * SUBLANES), :], buf.at[slot, j], sem.at[slot, j]).wait()
    return slot


def _combined_rows(x1_ref, route_ref, mod_ref, buf, slot, r0, rows):
    sl = pl.ds(r0, rows)
    w1 = route_ref[sl, 2:3]
    w2 = route_ref[sl, 3:4]
    tm_rows = pl.ds(r0 * SUBLANES, rows * SUBLANES)
    cols = []
    for c in range(D_MODEL // LANES):
        y = (w1 * _from_token_major(buf.at[slot, 0, tm_rows], rows, c)
             + w2 * _from_token_major(buf.at[slot, 1, tm_rows], rows, c))
        g2 = mod_ref[:, 5 * D_MODEL + c * LANES:5 * D_MODEL + (c + 1) * LANES]
        cols.append(x1_ref[sl, c * LANES:(c + 1) * LANES] + g2 * y)
    return jnp.concatenate(cols, axis=-1)


ROW_SPLIT = 2


def _combine_final_kernel(pos_ref, x1_ref, route_ref, mod_ref, nf_ref, os_ref, o_ref, buf, sem):
    slot = _rows_ready(pos_ref, os_ref, buf, sem, TC)
    x2 = _combined_rows(x1_ref, route_ref, mod_ref, buf, slot, 0, TC)
    ms = jnp.mean(x2 * x2, axis=-1, keepdims=True)
    o_ref[...] = x2 * lax.rsqrt(ms + EPS) * nf_ref[...]


def _gather_scratch(n):
    return [pltpu.VMEM((2, 2, n * SUBLANES, LANES), F32), pltpu.SemaphoreType.DMA((2, 2))]


def _combine_inproj_kernel(pos_ref, x1_ref, route_ref, mod_ref, modn_ref, n1_ref, w_ref, dft_ref, os_ref,
                           x2_ref, yf_ref, qkv_ref, cv_ref, buf, sem):
    slot = _rows_ready(pos_ref, os_ref, buf, sem, TM)
    rows = TM // ROW_SPLIT
    for s in range(ROW_SPLIT):
        sl = pl.ds(s * rows, rows)
        x2 = _combined_rows(x1_ref, route_ref, mod_ref, buf, slot, s * rows, rows)
        x2_ref[sl, :] = x2
        _inproj_math(x2, modn_ref, n1_ref, w_ref, dft_ref, yf_ref.at[sl], qkv_ref.at[sl], cv_ref.at[sl])


def _combine_inproj(pos_flat, x1, route, mod4, norm1, w_in_bf, dft_ch, osrt, li):
    def tmap(i, pos):
        return (i, 0)

    return pl.pallas_call(
        _combine_inproj_kernel,
        name="moe_combine_in_projection",
        grid_spec=pltpu.PrefetchScalarGridSpec(
            num_scalar_prefetch=1,
            grid=(T_ALL // TM,),
            in_specs=[
                pl.BlockSpec((TM, D_MODEL), tmap),
                pl.BlockSpec((TM, LANES), tmap),
                pl.BlockSpec((None, None, 1, 6 * D_MODEL), lambda i, pos: (li, _mod_row(i, TM), 0, 0)),
                pl.BlockSpec((None, None, 1, 6 * D_MODEL), lambda i, pos: (li + 1, _mod_row(i, TM), 0, 0)),
                pl.BlockSpec((1, 1, D_MODEL), lambda i, pos: (li + 1, 0, 0)),
                pl.BlockSpec((1, D_MODEL, D_IN_PROJ), lambda i, pos: (li + 1, 0, 0)),
                pl.BlockSpec((D_FOURIER, 2 * D_FOURIER), lambda i, pos: (0, 0)),
                pl.BlockSpec(memory_space=pl.ANY),
            ],
            out_specs=[
                pl.BlockSpec((TM, D_MODEL), tmap),
                pl.BlockSpec((TM, 2 * D_FOURIER), tmap),
                pl.BlockSpec((TM, 3 * D_NA), tmap),
                pl.BlockSpec((TM, 3 * D_CONV), tmap),
            ],
            scratch_shapes=_gather_scratch(TM),
        ),
        out_shape=[
            jax.ShapeDtypeStruct((T_ALL, D_MODEL), F32),
            jax.ShapeDtypeStruct((T_ALL, 2 * D_FOURIER), BF16),
            jax.ShapeDtypeStruct((T_ALL, 3 * D_NA), BF16),
            jax.ShapeDtypeStruct((T_ALL, 3 * D_CONV), F32),
        ],
        compiler_params=_cparams("arbitrary"),
    )(pos_flat, x1, route, mod4, mod4, norm1.reshape(DEPTH, 1, D_MODEL), w_in_bf, dft_ch, osrt)


def _combine_final(pos_flat, x1, route, mod4, norm_final, osrt, li, n_tok):
    def tmap(i, pos):
        return (i, 0)

    return pl.pallas_call(
        _combine_final_kernel,
        name="moe_combine_final",
        grid_spec=pltpu.PrefetchScalarGridSpec(
            num_scalar_prefetch=1,
            grid=(n_tok // TC,),
            in_specs=[
                pl.BlockSpec((TC, D_MODEL), tmap),
                pl.BlockSpec((TC, LANES), tmap),
                pl.BlockSpec((None, None, 1, 6 * D_MODEL), lambda i, pos: (li, _mod_row(i, TC), 0, 0)),
                pl.BlockSpec((1, D_MODEL), lambda i, pos: (0, 0)),
                pl.BlockSpec(memory_space=pl.ANY),
            ],
            out_specs=pl.BlockSpec((TC, D_MODEL), tmap),
            scratch_shapes=_gather_scratch(TC),
        ),
        out_shape=jax.ShapeDtypeStruct((n_tok, D_MODEL), F32),
        compiler_params=_cparams("arbitrary"),
    )(pos_flat, x1, route, mod4, norm_final.reshape(1, D_MODEL), osrt)


def _moe_block(hx_tm, route, w_gate, w_up, w_down, li, n_tok):
    n_tiles = (2 * n_tok) // TMX + N_EXPERTS
    n_rows = n_tiles * TMX
    pos, cnt, ends = _plan(route, n_tok)
    pos = pos[:, 0:2].reshape(-1)
    cnt = cnt[0, :N_EXPERTS].astype(jnp.int32)
    ends = ends[0, :N_EXPERTS].astype(jnp.int32)
    padded = ((cnt + TMX - 1) // TMX) * TMX
    pad_start = ends - padded + cnt
    tile_start = jnp.arange(n_tiles, dtype=jnp.int32) * TMX
    tile_expert = jnp.minimum(jnp.sum((tile_start[:, None] >= ends[None, :]).astype(jnp.int32), axis=1),
                              N_EXPERTS - 1)
    n_active = (ends[-1:] // TMX).astype(jnp.int32)
    tail = jnp.concatenate([ends[-1:], (n_rows - ends[-1:]) // PAD_CHUNKS[0]])

    xs = _dispatch(pos, pad_start, padded - cnt, tail, hx_tm, n_tok, n_rows)
    return pos, _moe(tile_expert, n_active, padded > 0, xs, w_gate, w_up, w_down, li, n_tiles)


def kernel(x, c, ctx, c_ctx, w_ada, b_ada, norm1, norm2, w_in, w_fourier, w_conv, rpb, w_out, w_rg, b_rg,
           w_re, b_re, w_gate, w_up, w_down, norm_final):
    x_pair, ctx_first = (x.reshape(T_LAT, D_MODEL), ctx.reshape(T_CTX, D_MODEL)), 0
    cc =jnp.concatenate([c, c_ctx[None, :], jnp.zeros((MOD_ROWS - BATCH - 1, D_MODEL), F32)], axis=0)
    mod4 = _modulation(cc, w_ada, b_ada).reshape(DEPTH, MOD_ROWS, 1, 6 * D_MODEL)

    w_in_bf = w_in.astype(BF16)
    w_out_bf = w_out.astype(BF16)
    pad = jnp.zeros((DEPTH, D_MODEL, LANES - ROUTE_COLS), F32)
    w_r = jnp.concatenate([w_rg, w_re, pad], axis=-1)
    w_r_hi = w_r.astype(BF16)
    w_r = jnp.concatenate([w_r_hi, (w_r - w_r_hi.astype(F32)).astype(BF16)], axis=-1)
    b_r = jnp.concatenate([b_rg, b_re, pad[:, 0, :]], axis=-1).reshape(DEPTH, 1, LANES)
    dft_ch = jnp.asarray(_channel_dft(), dtype=F32).astype(BF16)
    ctx_blk = T_LAT // CTX_LEN

    for li in range(DEPTH):
        last = li == DEPTH - 1
        if li == 0:
            yf, qkv, cv = _in_projection(*x_pair, ctx_first, mod4, norm1, w_in_bf, dft_ch, li)
        table = _bias_table(rpb[li].reshape(-1))
        ff = _fourier_mix(yf, w_fourier, li, SEQ, 0)
        cvo = _short_conv(cv, w_conv, li, SEQ, 0)
        at = _attn_latent(qkv, table)
        if last:
            n_tok = T_LAT
            ff_c, cvo_c, at_c = ff, cvo, at
        else:
            n_tok = T_ALL
            ff_c = _fourier_mix(yf, w_fourier, li, CTX_LEN, ctx_blk)
            cvo_c = _short_conv(cv, w_conv, li, CTX_LEN, ctx_blk)
            at_c = _attn_ctx(qkv)
        x1, hx_tm, route = _out_projection(x_pair, ctx_first, (ff, ff_c), (at, at_c), (cvo, cvo_c), mod4, norm2,
                                           w_out_bf, w_r, b_r, li, n_tok)
        pos, osrt = _moe_block(hx_tm, route, w_gate, w_up, w_down, li, n_tok)
        if last:
            out = _combine_final(pos, x1, route, mod4, norm_final, osrt, li, n_tok)
            return out.reshape(BATCH, SEQ, D_MODEL)
        xt, yf, qkv, cv = _combine_inproj(pos, x1, route, mod4, norm1, w_in_bf, dft_ch, osrt, li)
        x_pair, ctx_first = (xt, xt), NT_LAT
```

```python
import functools
import math

import numpy as np
import jax
import jax.numpy as jnp
from jax import lax
from jax.experimental import pallas as pl
from jax.experimental.pallas import tpu as pltpu

F32 = jnp.float32
BF16 = jnp.bfloat16

D_MODEL = 1024
BATCH = 8
SEQ = 2048
DEPTH = 2
GRID_W = 64
ROWS = SEQ // GRID_W
CTX_LEN = 256
T_LAT = BATCH * SEQ
T_CTX = BATCH * CTX_LEN
T_ALL = T_LAT + T_CTX

D_FOURIER = 256
D_FG = 64
HEAD_DIM = 64
NA_HEADS = 8
D_NA = NA_HEADS * HEAD_DIM
D_CONV = 256
D_IN_PROJ = D_FOURIER + 3 * D_NA + 3 * D_CONV
NA_KH = 8
NA_KW = 16
N_GROUPS = 4
EXPERTS_PER_GROUP = 8
N_EXPERTS = 32
D_EXPERT = 512
EPS = 1e-6
NEG_INF = -1e30

LANES = 128
SUBLANES = 8
MOD_ROWS = 16
TM = 512
TMX = 512
TD = 512
TC = 512
VMEM_LIMIT = 56 * 1024 * 1024


def _cparams(*sem):
    return pltpu.CompilerParams(dimension_semantics=sem, vmem_limit_bytes=VMEM_LIMIT)


def _ada_kernel(cc_ref, w_ref, b_ref, o_ref):
    s = cc_ref[...]
    s = s * jax.nn.sigmoid(s)
    acc = jnp.dot(s.astype(BF16), w_ref[0].astype(BF16), preferred_element_type=F32)
    o_ref[0] = acc + b_ref[0]


def _modulation(cc, w_ada, b_ada):
    tn = 1536
    nj = 6 * D_MODEL // tn
    return pl.pallas_call(
        _ada_kernel,
        name="ada_modulation",
        grid=(DEPTH, nj),
        in_specs=[
            pl.BlockSpec((MOD_ROWS, D_MODEL), lambda l, j: (0, 0)),
            pl.BlockSpec((1, D_MODEL, tn), lambda l, j: (l, 0, j)),
            pl.BlockSpec((1, 1, tn), lambda l, j: (l, 0, j)),
        ],
        out_specs=pl.BlockSpec((1, MOD_ROWS, tn), lambda l, j: (l, 0, j)),
        out_shape=jax.ShapeDtypeStruct((DEPTH, MOD_ROWS, 6 * D_MODEL), F32),
        compiler_params=_cparams("arbitrary", "arbitrary"),
    )(cc, w_ada, b_ada.reshape(DEPTH, 1, 6 * D_MODEL))


def _mod_row(i, tile):
    return jnp.minimum((i * tile) // SEQ, BATCH)


NT_LAT = T_LAT // TM


def _lat_spec(cols):
    return pl.BlockSpec((TM, cols), lambda i: (jnp.minimum(i, NT_LAT - 1), 0))


def _ctx_spec(cols, first_block):
    return pl.BlockSpec((TM, cols), lambda i: (jnp.maximum(i - NT_LAT, 0) + first_block, 0))


def _pick_stream(lat_ref, ctx_ref):
    return jnp.where(pl.program_id(0) >= NT_LAT, ctx_ref[...], lat_ref[...])


def _inproj_kernel(xa_ref, xb_ref, mod_ref, n1_ref, w_ref, dft_ref, yf_ref, qkv_ref, cv_ref):
    rows = TM // ROW_SPLIT
    for s in range(ROW_SPLIT):
        sl = pl.ds(s * rows, rows)
        _inproj_math(_pick_stream(xa_ref.at[sl], xb_ref.at[sl]), mod_ref, n1_ref, w_ref, dft_ref,
                     yf_ref.at[sl], qkv_ref.at[sl], cv_ref.at[sl])


def _inproj_math(x, mod_ref, n1_ref, w_ref, dft_ref, yf_ref, qkv_ref, cv_ref):
    ms = jnp.mean(x * x, axis=-1, keepdims=True)
    y = x * lax.rsqrt(ms + EPS) * n1_ref[0]
    sh = mod_ref[:, 0:D_MODEL]
    sc = mod_ref[:, D_MODEL:2 * D_MODEL]
    h = y * (1.0 + sc) + sh
    u = jnp.dot(h.astype(BF16), w_ref[0], preferred_element_type=F32)
    yf = jnp.dot(u[:, 0:D_FOURIER].astype(BF16), dft_ref[...], preferred_element_type=F32)
    yf_ref[...] = yf.astype(BF16)
    q0 = D_FOURIER
    qkv_ref[:, 0:D_NA] = (u[:, q0:q0 + D_NA] * (1.0 / math.sqrt(HEAD_DIM))).astype(BF16)
    qkv_ref[:, D_NA:3 * D_NA] = u[:, q0 + D_NA:q0 + 3 * D_NA].astype(BF16)
    cv_ref[...] = u[:, q0 + 3 * D_NA:D_IN_PROJ]


def _in_projection(x_lat, x_ctx, ctx_first_block, mod4, norm1, w_in_bf, dft_ch, li):
    nt = T_ALL // TM
    return pl.pallas_call(
        _inproj_kernel,
        name="in_projection",
        grid=(nt,),
        in_specs=[
            _lat_spec(D_MODEL),
            _ctx_spec(D_MODEL, ctx_first_block),
            pl.BlockSpec((None, None, 1, 6 * D_MODEL), lambda i: (li, _mod_row(i, TM), 0, 0)),
            pl.BlockSpec((1, 1, D_MODEL), lambda i: (li, 0, 0)),
            pl.BlockSpec((1, D_MODEL, D_IN_PROJ), lambda i: (li, 0, 0)),
            pl.BlockSpec((D_FOURIER, 2 * D_FOURIER), lambda i: (0, 0)),
        ],
        out_specs=[
            pl.BlockSpec((TM, 2 * D_FOURIER), lambda i: (i, 0)),
            pl.BlockSpec((TM, 3 * D_NA), lambda i: (i, 0)),
            pl.BlockSpec((TM, 3 * D_CONV), lambda i: (i, 0)),
        ],
        out_shape=[
            jax.ShapeDtypeStruct((T_ALL, 2 * D_FOURIER), BF16),
            jax.ShapeDtypeStruct((T_ALL, 3 * D_NA), BF16),
            jax.ShapeDtypeStruct((T_ALL, 3 * D_CONV), F32),
        ],
        compiler_params=_cparams("arbitrary"),
    )(x_lat, x_ctx, mod4, norm1.reshape(DEPTH, 1, D_MODEL), w_in_bf, dft_ch)


def _dft_tables(n):
    k = np.arange(n, dtype=np.int64)
    ang = 2.0 * np.pi * ((k[:, None] * k[None, :]) % n).astype(np.float64) / n
    s = 1.0 / math.sqrt(n)
    return np.cos(ang) * s, np.sin(ang) * s


def _channel_dft():
    c, s = _dft_tables(D_FG)
    eye = np.eye(D_FOURIER // D_FG)
    return np.concatenate([np.kron(eye, c), np.kron(eye, s)], axis=1)


def _fourier_kernel(ch_ref, sh_ref, j_ref, y_ref, wf_ref, o_ref, fold_ref, rev_ref, *, n):
    h = n // 2
    nb = h // LANES
    df = D_FOURIER
    scale = 1.0 / math.sqrt(n)
    rev = j_ref[...]
    zero_rows = jnp.zeros((SUBLANES, 2 * df), F32)

    fold_ref[0:SUBLANES, :] = zero_rows
    for b in range(nb):
        blk = y_ref[n - LANES * (b + 1):n - LANES * b, :]
        fold_ref[SUBLANES + LANES * b:SUBLANES + LANES * (b + 1), :] = jnp.dot(rev, blk, preferred_element_type=F32)
    mirrored = fold_ref[SUBLANES - 1:SUBLANES - 1 + h, :]
    y_lo = y_ref[0:h, :].astype(F32)
    yc_even = (y_lo[:, 0:df] + mirrored[:, 0:df]).astype(BF16)
    ys_odd = (y_lo[:, df:2 * df] - mirrored[:, df:2 * df]).astype(BF16)

    parity = lax.broadcasted_iota(jnp.int32, (h, 1), 0) & 1
    sign = jnp.where(parity == 1, -scale, scale)
    y_mid = y_ref[h:h + 1, 0:df].astype(F32)
    p = jnp.dot(ch_ref[...], yc_even, preferred_element_type=F32) + sign * y_mid
    q = jnp.dot(sh_ref[...], ys_odd, preferred_element_type=F32)
    lo = (p - q).astype(BF16)
    w = (p + q).astype(BF16)
    mid = jnp.sum(sign * (y_lo[:, 0:df] + y_ref[h:n, 0:df].astype(F32)), axis=0, keepdims=True)

    for b in range(nb):
        blk = w[h - LANES * (b + 1):h - LANES * b, :]
        rev_ref[SUBLANES + LANES * b:SUBLANES + LANES * (b + 1), :] = jnp.dot(rev, blk, preferred_element_type=F32)
    rev_ref[0:SUBLANES, :] = jnp.broadcast_to(mid, (SUBLANES, df))
    hi = rev_ref[SUBLANES - 1:SUBLANES - 1 + h, :].astype(BF16)
    z = jnp.concatenate([lo, hi], axis=0)
    o_ref[...] = jnp.dot(z, wf_ref[0].astype(BF16), preferred_element_type=F32).astype(BF16)


def _fourier_mix(yf, w_fourier, li, n, first_block):
    h = n // 2
    cn, sn = _dft_tables(n)
    ch = jnp.asarray(cn[:h, :h], dtype=F32).astype(BF16)
    sh = jnp.asarray(sn[:h, :h], dtype=F32).astype(BF16)
    anti = jnp.asarray(np.eye(LANES)[::-1].copy(), dtype=F32).astype(BF16)
    return pl.pallas_call(
        functools.partial(_fourier_kernel, n=n),
        name="fourier_mix",
        grid=(BATCH,),
        in_specs=[
            pl.BlockSpec((h, h), lambda b: (0, 0)),
            pl.BlockSpec((h, h), lambda b: (0, 0)),
            pl.BlockSpec((LANES, LANES), lambda b: (0, 0)),
            pl.BlockSpec((n, 2 * D_FOURIER), lambda b: (first_block + b, 0)),
            pl.BlockSpec((1, D_FOURIER, D_FOURIER), lambda b: (li, 0, 0)),
        ],
        out_specs=pl.BlockSpec((n, D_FOURIER), lambda b: (b, 0)),
        out_shape=jax.ShapeDtypeStruct((BATCH * n, D_FOURIER), BF16),
        scratch_shapes=[pltpu.VMEM((h + SUBLANES, 2 * D_FOURIER), F32), pltpu.VMEM((h + SUBLANES, D_FOURIER), F32)],
        compiler_params=_cparams("arbitrary"),
    )(ch, sh, anti, yf, w_fourier)


def _conv_kernel(cv_ref, wc_ref, o_ref, pad_ref):
    n = cv_ref.shape[0]
    gb = cv_ref[:, 0:D_CONV]
    g = cv_ref[:, D_CONV:2 * D_CONV] * cv_ref[:, 2 * D_CONV:3 * D_CONV]
    zero = jnp.zeros((SUBLANES, D_CONV), F32)
    pad_ref[0:SUBLANES, :] = zero
    pad_ref[n + SUBLANES:n + 2 * SUBLANES, :] = zero
    pad_ref[SUBLANES:n + SUBLANES, :] = g
    prev = pad_ref[SUBLANES - 1:n + SUBLANES - 1, :]
    nxt = pad_ref[SUBLANES + 1:n + SUBLANES + 1, :]
    w = wc_ref[0]
    o_ref[...] = (gb * (w[0:1, :] * prev + w[1:2, :] * g + w[2:3, :] * nxt)).astype(BF16)


def _short_conv(cv, w_conv, li, n, first_block):
    return pl.pallas_call(
        _conv_kernel,
        name="short_conv",
        grid=(BATCH,),
        in_specs=[
            pl.BlockSpec((n, 3 * D_CONV), lambda b: (first_block + b, 0)),
            pl.BlockSpec((1, 3, D_CONV), lambda b: (li, 0, 0)),
        ],
        out_specs=pl.BlockSpec((n, D_CONV), lambda b: (b, 0)),
        out_shape=jax.ShapeDtypeStruct((BATCH * n, D_CONV), BF16),
        scratch_shapes=[pltpu.VMEM((n + 2 * SUBLANES, D_CONV), F32)],
        compiler_params=_cparams("arbitrary"),
    )(cv, w_conv)


RPB_H = 2 * NA_KH - 1
RPB_W = 2 * NA_KW - 1
KEYS_LOC = NA_KH * GRID_W


def _bias_kernel(rpb_ref, o_ref):
    h = pl.program_id(0)
    qi = lax.broadcasted_iota(jnp.int32, (GRID_W, LANES), 0)
    lj = lax.broadcasted_iota(jnp.int32, (GRID_W, LANES), 1)
    kc = lj & (GRID_W - 1)
    hi = lj >= GRID_W
    d = kc - qi + (NA_KW - 1)
    cs = jnp.clip(qi - NA_KW // 2, 0, GRID_W - NA_KW)
    valid = (kc >= cs) & (kc < cs + NA_KW)
    tiles = []
    for a in range(RPB_H - 1):
        acc = jnp.zeros((GRID_W, LANES), F32)
        for b in range(RPB_W):
            va = rpb_ref[(h * RPB_H + a) * RPB_W + b]
            vb = rpb_ref[(h * RPB_H + a + 1) * RPB_W + b]
            acc = jnp.where(d == b, jnp.where(hi, vb, va), acc)
        tiles.append(jnp.where(valid, acc, NEG_INF))
    for cls in range(NA_KH):
        for m in range(NA_KH // 2):
            o_ref[0, cls, :, m * LANES:(m + 1) * LANES] = tiles[2 * m - cls + NA_KH - 1]


def _bias_table(rpb_flat):
    return pl.pallas_call(
        _bias_kernel,
        name="attn_bias_table",
        grid=(NA_HEADS,),
        in_specs=[pl.BlockSpec(memory_space=pltpu.SMEM)],
        out_specs=pl.BlockSpec((1, NA_KH, GRID_W, KEYS_LOC), lambda h: (h // 2, 0, h % 2, 0)),
        out_shape=jax.ShapeDtypeStruct((NA_HEADS // 2, NA_KH, 2 * GRID_W, KEYS_LOC), F32),
        compiler_params=_cparams("arbitrary"),
    )(rpb_flat)


_NT_DIMS = (((1,), (1,)), ((), ()))


def _softmax_pv(parts):
    m = None
    for s, _ in parts:
        ms = jnp.max(s, axis=1, keepdims=True)
        m = ms if m is None else jnp.maximum(m, ms)
    l = None
    o = None
    for s, v in parts:
        p = jnp.exp(s - m)
        ls = jnp.sum(p, axis=1, keepdims=True)
        os_ = jnp.dot(p.astype(BF16), v, preferred_element_type=F32)
        l = ls if l is None else l + ls
        o = os_ if o is None else o + os_
    return o / l


def _attn_latent_kernel(q_ref, k_ref, v_ref, kc_ref, vc_ref, tab_ref, o_ref, ve_ref, vce_ref):
    lane = lax.broadcasted_iota(jnp.int32, (GRID_W, LANES), 1)
    lo = lane < HEAD_DIM
    ve_ref[:, 0:LANES] = v_ref[...]
    ve_ref[:, LANES:2 * LANES] = jnp.ones((SEQ, LANES), BF16)
    vce_ref[:, 0:LANES] = vc_ref[...]
    vce_ref[:, LANES:2 * LANES] = jnp.ones((CTX_LEN, LANES), BF16)
    kc = kc_ref[...]
    vce = vce_ref[...]

    for r in range(ROWS):
        rs = min(max(r - NA_KH // 2, 0), ROWS - NA_KH)
        cls = r - rs
        q = q_ref[r * GRID_W:(r + 1) * GRID_W, :]
        kl = k_ref[rs * GRID_W:rs * GRID_W + KEYS_LOC, :]
        vl = ve_ref[rs * GRID_W:rs * GRID_W + KEYS_LOC, :]
        zero = jnp.zeros_like(q)
        q2 = jnp.concatenate([jnp.where(lo, q, zero), jnp.where(lo, zero, q)], axis=0)
        s1 = lax.dot_general(q2, kl, _NT_DIMS, preferred_element_type=F32) + tab_ref[0, cls]
        s2 = lax.dot_general(q2, kc, _NT_DIMS, preferred_element_type=F32)
        m = jnp.maximum(jnp.max(s1, axis=1, keepdims=True), jnp.max(s2, axis=1, keepdims=True))
        p1 = jnp.exp(s1 - m).astype(BF16)
        p2 = jnp.exp(s2 - m).astype(BF16)
        oe = (jnp.dot(p1, vl, preferred_element_type=F32) + jnp.dot(p2, vce, preferred_element_type=F32))
        o = oe[:, 0:LANES] / oe[:, LANES:2 * LANES]
        o_ref[r * GRID_W:(r + 1) * GRID_W, :] = jnp.where(lo, o[0:GRID_W], o[GRID_W:2 * GRID_W]).astype(BF16)


def _attn_latent(qkv, table):
    npair = NA_HEADS // 2
    cblk = T_LAT // CTX_LEN
    return pl.pallas_call(
        _attn_latent_kernel,
        name="attn_latent",
        grid=(BATCH, npair),
        in_specs=[
            pl.BlockSpec((SEQ, LANES), lambda b, p: (b, p)),
            pl.BlockSpec((SEQ, LANES), lambda b, p: (b, npair + p)),
            pl.BlockSpec((SEQ, LANES), lambda b, p: (b, 2 * npair + p)),
            pl.BlockSpec((CTX_LEN, LANES), lambda b, p: (cblk + b, npair + p)),
            pl.BlockSpec((CTX_LEN, LANES), lambda b, p: (cblk + b, 2 * npair + p)),
            pl.BlockSpec((1, NA_KH, 2 * GRID_W, KEYS_LOC), lambda b, p: (p, 0, 0, 0)),
        ],
        out_specs=pl.BlockSpec((SEQ, LANES), lambda b, p: (b, p)),
        out_shape=jax.ShapeDtypeStruct((T_LAT, D_NA), BF16),
        scratch_shapes=[pltpu.VMEM((SEQ, 2 * LANES), BF16), pltpu.VMEM((CTX_LEN, 2 * LANES), BF16)],
        compiler_params=_cparams("arbitrary", "arbitrary"),
    )(qkv, qkv, qkv, qkv, qkv, table)


def _attn_ctx_kernel(q_ref, k_ref, v_ref, o_ref):
    lane = lax.broadcasted_iota(jnp.int32, (CTX_LEN, LANES), 1)
    lo = lane < HEAD_DIM
    q = q_ref[...]
    k = k_ref[...]
    v = v_ref[...]
    outs = []
    for hh in range(2):
        qm = jnp.where(lo if hh == 0 else jnp.logical_not(lo), q, jnp.zeros_like(q))
        s = lax.dot_general(qm, k, _NT_DIMS, preferred_element_type=F32)
        outs.append(_softmax_pv([(s, v)]))
    o_ref[...] = jnp.where(lo, outs[0], outs[1]).astype(BF16)


def _attn_ctx(qkv):
    npair = NA_HEADS // 2
    cblk = T_LAT // CTX_LEN
    return pl.pallas_call(
        _attn_ctx_kernel,
        name="attn_context",
        grid=(BATCH, npair),
        in_specs=[
            pl.BlockSpec((CTX_LEN, LANES), lambda b, p: (cblk + b, p)),
            pl.BlockSpec((CTX_LEN, LANES), lambda b, p: (cblk + b, npair + p)),
            pl.BlockSpec((CTX_LEN, LANES), lambda b, p: (cblk + b, 2 * npair + p)),
        ],
        out_specs=pl.BlockSpec((CTX_LEN, LANES), lambda b, p: (b, p)),
        out_shape=jax.ShapeDtypeStruct((T_CTX, D_NA), BF16),
        compiler_params=_cparams("arbitrary", "arbitrary"),
    )(qkv, qkv, qkv)


ROUTE_COLS = N_GROUPS + N_EXPERTS


def _to_token_major(ref, val):
    rows = val.shape[0]
    for j in range(D_MODEL // LANES):
        ref[pl.ds(j, rows, stride=SUBLANES), :] = val[:, j * LANES:(j + 1) * LANES]


def _from_token_major(ref, rows, j):
    return ref[pl.ds(j, rows, stride=SUBLANES), :]


OUT_SPLIT = 2


def _outproj_kernel(xa_ref, xb_ref, ffa_ref, ffb_ref, ata_ref, atb_ref, cva_ref, cvb_ref,
                    mod_ref, n2_ref, wo_ref, wr_ref, br_ref, x1_ref, hx_ref, route_ref):
    rows = TM // OUT_SPLIT
    for s in range(OUT_SPLIT):
        sl = pl.ds(s * rows, rows)
        ins = [r.at[sl] for r in (xa_ref, xb_ref, ffa_ref, ffb_ref, ata_ref, atb_ref, cva_ref, cvb_ref)]
        _outproj_rows(*ins, mod_ref, n2_ref, wo_ref, wr_ref, br_ref, x1_ref.at[sl],
                      hx_ref.at[pl.ds(s * rows * SUBLANES, rows * SUBLANES)], route_ref.at[sl])


def _outproj_rows(xa_ref, xb_ref, ffa_ref, ffb_ref, ata_ref, atb_ref, cva_ref, cvb_ref,
                  mod_ref, n2_ref, wo_ref, wr_ref, br_ref, x1_ref, hx_ref, route_ref):
    r1 = D_FOURIER
    r2 = D_FOURIER + D_NA
    mix = (jnp.dot(_pick_stream(ffa_ref, ffb_ref), wo_ref[0, 0:r1, :], preferred_element_type=F32)
           + jnp.dot(_pick_stream(ata_ref, atb_ref), wo_ref[0, r1:r2, :], preferred_element_type=F32)
           + jnp.dot(_pick_stream(cva_ref, cvb_ref), wo_ref[0, r2:D_MODEL, :], preferred_element_type=F32))
    g1 = mod_ref[:, 2 * D_MODEL:3 * D_MODEL]
    x1 = _pick_stream(xa_ref, xb_ref) + g1 * mix
    x1_ref[...] = x1
    ms = jnp.mean(x1 * x1, axis=-1, keepdims=True)
    y = x1 * lax.rsqrt(ms + EPS) * n2_ref[0]
    sh2 = mod_ref[:, 3 * D_MODEL:4 * D_MODEL]
    sc2 = mod_ref[:, 4 * D_MODEL:5 * D_MODEL]
    hx = y * (1.0 + sc2) + sh2
    _to_token_major(hx_ref, hx)

    hx_hi = hx.astype(BF16)
    hx_lo = (hx - hx_hi.astype(F32)).astype(BF16)
    part = (jnp.dot(hx_hi, wr_ref[0], preferred_element_type=F32)
            + jnp.dot(hx_lo, wr_ref[0], preferred_element_type=F32))
    logits = part[:, 0:LANES] + part[:, LANES:2 * LANES] + br_ref[0]
    tm = logits.shape[0]
    lane = lax.broadcasted_iota(jnp.int32, (tm, LANES), 1)
    lane_f = lane.astype(F32)
    group_of_lane = ((lane - N_GROUPS) >> 3).astype(F32)
    big = float(LANES)
    gl = jnp.where(lane < N_GROUPS, logits, -jnp.inf)
    gmax = jnp.max(gl, axis=1, keepdims=True)
    gidx = jnp.min(jnp.where(gl == gmax, lane_f, big), axis=1, keepdims=True)
    g_w = 1.0 / jnp.sum(jnp.exp(gl - gmax), axis=1, keepdims=True)
    in_group = (lane >= N_GROUPS) & (lane < N_GROUPS + N_EXPERTS) & (group_of_lane == gidx)
    es = jnp.where(in_group, logits, -jnp.inf)
    t1 = jnp.max(es, axis=1, keepdims=True)
    i1 = jnp.min(jnp.where(es == t1, lane_f, big), axis=1, keepdims=True)
    es2 = jnp.where(lane_f == i1, -jnp.inf, es)
    t2 = jnp.max(es2, axis=1, keepdims=True)
    i2 = jnp.min(jnp.where(es2 == t2, lane_f, big), axis=1, keepdims=True)
    dlt = jnp.exp(t2 - t1)
    w1 = g_w / (1.0 + dlt)
    w2 = g_w * dlt / (1.0 + dlt)
    e1 = i1 - float(N_GROUPS)
    e2 = i2 - float(N_GROUPS)
    route_ref[...] = jnp.where(lane == 0, e1, jnp.where(lane == 1, e2,
                               jnp.where(lane == 2, w1, jnp.where(lane == 3, w2, 0.0))))


def _out_projection(x_pair, ctx_first_block, ff_pair, at_pair, cv_pair, mod4, norm2, w_out_bf, w_r, b_r, li, n_tok):
    nt = n_tok // TM
    return pl.pallas_call(
        _outproj_kernel,
        name="out_projection",
        grid=(nt,),
        in_specs=[
            _lat_spec(D_MODEL), _ctx_spec(D_MODEL, ctx_first_block),
            _lat_spec(D_FOURIER), _ctx_spec(D_FOURIER, 0),
            _lat_spec(D_NA), _ctx_spec(D_NA, 0),
            _lat_spec(D_CONV), _ctx_spec(D_CONV, 0),
            pl.BlockSpec((None, None, 1, 6 * D_MODEL), lambda i: (li, _mod_row(i, TM), 0, 0)),
            pl.BlockSpec((1, 1, D_MODEL), lambda i: (li, 0, 0)),
            pl.BlockSpec((1, D_MODEL, D_MODEL), lambda i: (li, 0, 0)),
            pl.BlockSpec((1, D_MODEL, 2 * LANES), lambda i: (li, 0, 0)),
            pl.BlockSpec((1, 1, LANES), lambda i: (li, 0, 0)),
        ],
        out_specs=[
            pl.BlockSpec((TM, D_MODEL), lambda i: (i, 0)),
            pl.BlockSpec((TM * SUBLANES, LANES), lambda i: (i, 0)),
            pl.BlockSpec((TM, LANES), lambda i: (i, 0)),
        ],
        out_shape=[
            jax.ShapeDtypeStruct((n_tok, D_MODEL), F32),
            jax.ShapeDtypeStruct((n_tok * SUBLANES, LANES), F32),
            jax.ShapeDtypeStruct((n_tok, LANES), F32),
        ],
        compiler_params=_cparams("arbitrary"),
    )(*x_pair, *ff_pair, *at_pair, *cv_pair, mod4, norm2.reshape(DEPTH, 1, D_MODEL), w_out_bf, w_r, b_r)


def _plan_kernel(route_ref, pos_ref, cnt_ref, ends_ref, carry_ref, offs_ref):
    ph = pl.program_id(0)
    i = pl.program_id(1)
    tm = PLAN_SUB
    lane = lax.broadcasted_iota(jnp.int32, (tm, LANES), 1)

    def one_hots(k):
        r = route_ref[k * tm:(k + 1) * tm, :]
        oh1 = lane == r[:, 0:1].astype(jnp.int32)
        oh2 = lane == r[:, 1:2].astype(jnp.int32)
        return oh1, oh2, jnp.where(oh1 | oh2, 1.0, 0.0)

    @pl.when((ph == 0) & (i == 0))
    def _():
        carry_ref[...] = jnp.zeros_like(carry_ref)

    @pl.when(ph == 0)
    def _():
        total = carry_ref[...]
        for k in range(PLAN_TM // tm):
            total = total + jnp.sum(one_hots(k)[2], axis=0, keepdims=True)
        carry_ref[...] = total

    @pl.when((ph == 1) & (i == 0))
    def _():
        cnt = carry_ref[...]
        cnt_ref[...] = jnp.broadcast_to(cnt, cnt_ref.shape)
        tiles = jnp.ceil(cnt * (1.0 / TMX))
        a = lax.broadcasted_iota(jnp.int32, (LANES, LANES), 0)
        b = lax.broadcasted_iota(jnp.int32, (LANES, LANES), 1)
        upper = jnp.where(a < b, 1.0, 0.0).astype(BF16)
        tiles8 = jnp.broadcast_to(tiles, (SUBLANES, LANES))
        first = jnp.dot(tiles8.astype(BF16), upper, preferred_element_type=F32)
        offs_ref[...] = first[0:1, :] * TMX
        ends_ref[...] = (first + tiles8) * TMX
        carry_ref[...] = jnp.zeros_like(carry_ref)

    @pl.when(ph == 1)
    def _():
        row = lax.broadcasted_iota(jnp.int32, (tm, tm), 0)
        col = lax.broadcasted_iota(jnp.int32, (tm, tm), 1)
        tri = jnp.where(row > col, 1.0, 0.0).astype(BF16)
        base = carry_ref[...] + offs_ref[...]
        for k in range(PLAN_TM // tm):
            oh1, oh2, oh = one_hots(k)
            cum = jnp.dot(tri, oh.astype(BF16), preferred_element_type=F32) + base
            p1 = jnp.sum(jnp.where(oh1, cum, 0.0), axis=1, keepdims=True)
            p2 = jnp.sum(jnp.where(oh2, cum, 0.0), axis=1, keepdims=True)
            pos_ref[k * tm:(k + 1) * tm, :] = (
                jnp.where(lane == 0, p1, jnp.where(lane == 1, p2, 0.0)).astype(jnp.int32))
            base = base + jnp.sum(oh, axis=0, keepdims=True)
        carry_ref[...] = base - offs_ref[...]


PLAN_TM = 2048
PLAN_SUB = 512


def _plan(route, n_tok):
    nt = n_tok // PLAN_TM
    return pl.pallas_call(
        _plan_kernel,
        name="moe_plan",
        grid=(2, nt),
        in_specs=[pl.BlockSpec((PLAN_TM, LANES), lambda ph, i: (i, 0))],
        out_specs=[
            pl.BlockSpec((PLAN_TM, LANES), lambda ph, i: (i * ph, 0)),
            pl.BlockSpec((SUBLANES, LANES), lambda ph, i: (0, 0)),
            pl.BlockSpec((SUBLANES, LANES), lambda ph, i: (0, 0)),
        ],
        out_shape=[
            jax.ShapeDtypeStruct((n_tok, LANES), jnp.int32),
            jax.ShapeDtypeStruct((SUBLANES, LANES), F32),
            jax.ShapeDtypeStruct((SUBLANES, LANES), F32),
        ],
        scratch_shapes=[pltpu.VMEM((1, LANES), F32), pltpu.VMEM((1, LANES), F32)],
        compiler_params=_cparams("arbitrary", "arbitrary"),
    )(route)


def _row_tile(ref, row):
    return ref.at[pl.ds(pl.multiple_of(row * SUBLANES, SUBLANES), SUBLANES), :]


DISPATCH_UNROLL = 8
PAD_CHUNKS = tuple(1 << b for b in reversed(range(TMX.bit_length() - 1)))


def _dispatch_kernel(pos_ref, pstart_ref, plen_ref, tail_ref, hx_ref, xs_ref, zbuf, sem, zsem):
    i = pl.program_id(0)

    def pad_copies(fn):
        def ebody(e, carry):
            n = plen_ref[e]
            off = pstart_ref[e]
            for rows in PAD_CHUNKS:
                @pl.when((n & rows) != 0)
                def _():
                    fn(pltpu.make_async_copy(
                        zbuf.at[pl.ds(0, rows * SUBLANES), :],
                        xs_ref.at[pl.ds(pl.multiple_of(off * SUBLANES, SUBLANES), rows * SUBLANES), :], zsem.at[0]))
                off = off + (n & rows)
            return carry

        lax.fori_loop(0, N_EXPERTS, ebody, 0)

        def tbody(k, carry):
            row = pl.multiple_of((tail_ref[0] + k * PAD_CHUNKS[0]) * SUBLANES, SUBLANES)
            fn(pltpu.make_async_copy(zbuf, xs_ref.at[pl.ds(row, PAD_CHUNKS[0] * SUBLANES), :], zsem.at[0]))
            return carry

        lax.fori_loop(0, tail_ref[1], tbody, 0)

    @pl.when(i == 0)
    def _():
        zbuf[...] = jnp.zeros_like(zbuf)
        pad_copies(lambda cp: cp.start())

    def body(k, carry):
        for u in range(DISPATCH_UNROLL):
            t = k * DISPATCH_UNROLL + u
            src = _row_tile(hx_ref, t)
            for j in range(2):
                pltpu.make_async_copy(src, _row_tile(xs_ref, pos_ref[(i * TD + t) * 2 + j]),
                                      sem.at[0]).start(priority=j)
        return carry

    lax.fori_loop(0, TD // DISPATCH_UNROLL, body, 0)
    nrow = TD * SUBLANES
    for j in range(2):
        pltpu.make_async_copy(hx_ref, xs_ref.at[pl.ds(0, nrow), :], sem.at[0]).wait()

    @pl.when(i == 0)
    def _():
        pad_copies(lambda cp: cp.wait())


def _dispatch(pos_flat, pad_start, pad_len, tail, hx_tm, n_tok, n_rows):
    return pl.pallas_call(
        _dispatch_kernel,
        name="moe_dispatch",
        grid_spec=pltpu.PrefetchScalarGridSpec(
            num_scalar_prefetch=4,
            grid=(n_tok // TD,),
            in_specs=[pl.BlockSpec((TD * SUBLANES, LANES), lambda i, pos, ps, pn, tl: (i, 0))],
            out_specs=pl.BlockSpec(memory_space=pl.ANY),
            scratch_shapes=[pltpu.VMEM((PAD_CHUNKS[0] * SUBLANES, LANES), F32),
                            pltpu.SemaphoreType.DMA((1,)), pltpu.SemaphoreType.DMA((1,))],
        ),
        out_shape=jax.ShapeDtypeStruct((n_rows * SUBLANES, LANES), F32),
        compiler_params=_cparams("arbitrary"),
    )(pos_flat, pad_start, pad_len, tail, hx_tm)


def _moe_kernel(te_ref, na_ref, first_ref, wslot_ref, nxt_ref, xs_ref, wg_hbm, wu_hbm, wd_hbm, os_ref,
                wg_f, wu_f, wd_f, wg_s, wu_s, wd_s, wsem, *, li):
    i = pl.program_id(0)
    active = i < na_ref[0]

    def weight_copies(e, slot):
        return [pltpu.make_async_copy(wg_hbm.at[li, e], wg_f.at[slot], wsem.at[slot, 0]),
                pltpu.make_async_copy(wu_hbm.at[li, e], wu_f.at[slot], wsem.at[slot, 1]),
                pltpu.make_async_copy(wd_hbm.at[li, e], wd_f.at[slot], wsem.at[slot, 2])]

    @pl.when(active & (first_ref[i] == 1))
    def _():
        slot = wslot_ref[i]

        @pl.when(i == 0)
        def _():
            for cp in weight_copies(te_ref[0], 0):
                cp.start()

        @pl.when(nxt_ref[i] >= 0)
        def _():
            for cp in weight_copies(nxt_ref[i], 1 - slot):
                cp.start()

        for cp in weight_copies(te_ref[i], slot):
            cp.wait()
        wg_s[...] = wg_f[slot].astype(BF16)
        wu_s[...] = wu_f[slot].astype(BF16)
        wd_s[...] = wd_f[slot].astype(BF16)

    @pl.when(active)
    def _():
        x = jnp.concatenate([_from_token_major(xs_ref, TMX, j).astype(BF16)
                             for j in range(D_MODEL // LANES)], axis=-1)
        g = jnp.dot(x, wg_s[...], preferred_element_type=F32)
        u = jnp.dot(x, wu_s[...], preferred_element_type=F32)
        h = (g * jax.nn.sigmoid(g) * u).astype(BF16)
        _to_token_major(os_ref, jnp.dot(h, wd_s[...], preferred_element_type=F32))

    @pl.when(jnp.logical_not(active))
    def _():
        os_ref[...] = jnp.zeros_like(os_ref)


def _moe(tile_expert, n_active, has_rows, xs, w_gate, w_up, w_down, li, n_tiles):
    eid = jnp.arange(N_EXPERTS, dtype=jnp.int32)
    later = jnp.where(has_rows[None, :] & (eid[None, :] > eid[:, None]), eid[None, :], N_EXPERTS)
    next_of_expert = jnp.min(later, axis=1)
    next_of_expert = jnp.where(next_of_expert == N_EXPERTS, -1, next_of_expert)
    first = jnp.concatenate([jnp.ones((1,), jnp.int32),
                             (tile_expert[1:] != tile_expert[:-1]).astype(jnp.int32)])
    wslot = (jnp.cumsum(first) - 1) % 2
    nxt = jnp.sum(jnp.where(tile_expert[:, None] == eid[None, :], next_of_expert[None, :], 0), axis=1)

    def row_map(i, te, na, fi, ws, nx):
        return (jnp.minimum(i, na[0] - 1), 0)

    return pl.pallas_call(
        functools.partial(_moe_kernel, li=li),
        name="moe_experts",
        grid_spec=pltpu.PrefetchScalarGridSpec(
            num_scalar_prefetch=5,
            grid=(n_tiles,),
            in_specs=[
                pl.BlockSpec((TMX * SUBLANES, LANES), row_map),
                pl.BlockSpec(memory_space=pl.ANY),
                pl.BlockSpec(memory_space=pl.ANY),
                pl.BlockSpec(memory_space=pl.ANY),
            ],
            out_specs=pl.BlockSpec((TMX * SUBLANES, LANES), lambda i, te, na, fi, ws, nx: (i, 0)),
            scratch_shapes=[
                pltpu.VMEM((2, D_MODEL, D_EXPERT), F32),
                pltpu.VMEM((2, D_MODEL, D_EXPERT), F32),
                pltpu.VMEM((2, D_EXPERT, D_MODEL), F32),
                pltpu.VMEM((D_MODEL, D_EXPERT), BF16),
                pltpu.VMEM((D_MODEL, D_EXPERT), BF16),
                pltpu.VMEM((D_EXPERT, D_MODEL), BF16),
                pltpu.SemaphoreType.DMA((2, 3)),
            ],
        ),
        out_shape=jax.ShapeDtypeStruct((n_tiles * TMX * SUBLANES, LANES), F32),
        compiler_params=_cparams("arbitrary"),
    )(tile_expert, n_active, first, wslot.astype(jnp.int32), nxt.astype(jnp.int32), xs, w_gate, w_up, w_down)


GATHER_UNROLL = 16


def _start_row_gathers(pos_ref, os_ref, buf, sem, tile, slot, n):
    def body(k, carry):
        for u in range(GATHER_UNROLL):
            t = k * GATHER_UNROLL + u
            for j in range(2):
                pltpu.make_async_copy(_row_tile(os_ref, pos_ref[(tile * n + t) * 2 + j]),
                                      _row_tile(buf.at[slot, j], t), sem.at[slot, j]).start(priority=j)
        return carry

    lax.fori_loop(0, n // GATHER_UNROLL, body, 0)


def _rows_ready(pos_ref, os_ref, buf, sem, n):
    i = pl.program_id(0)
    slot = i % 2

    @pl.when(i == 0)
    def _():
        _start_row_gathers(pos_ref, os_ref, buf, sem, 0, 0, n)

    @pl.when(i + 1 < pl.num_programs(0))
    def _():
        _start_row_gathers(pos_ref, os_ref, buf, sem, i + 1, 1 - slot, n)

    for j in range(2):
        pltpu.make_async_copy(os_ref.at[pl.ds(0, n * SUBLANES), :], buf.at[slot, j], sem.at[slot, j]).wait()
    return slot


def _combined_rows(x1_ref, route_ref, mod_ref, buf, slot, r0, rows):
    sl = pl.ds(r0, rows)
    w1 = route_ref[sl, 2:3]
    w2 = route_ref[sl, 3:4]
    tm_rows = pl.ds(r0 * SUBLANES, rows * SUBLANES)
    cols = []
    for c in range(D_MODEL // LANES):
        y = (w1 * _from_token_major(buf.at[slot, 0, tm_rows], rows, c)
             + w2 * _from_token_major(buf.at[slot, 1, tm_rows], rows, c))
        g2 = mod_ref[:, 5 * D_MODEL + c * LANES:5 * D_MODEL + (c + 1) * LANES]
        cols.append(x1_ref[sl, c * LANES:(c + 1) * LANES] + g2 * y)
    return jnp.concatenate(cols, axis=-1)


ROW_SPLIT = 2


def _combine_final_kernel(pos_ref, x1_ref, route_ref, mod_ref, nf_ref, os_ref, o_ref, buf, sem):
    slot = _rows_ready(pos_ref, os_ref, buf, sem, TC)
    x2 = _combined_rows(x1_ref, route_ref, mod_ref, buf, slot, 0, TC)
    ms = jnp.mean(x2 * x2, axis=-1, keepdims=True)
    o_ref[...] = x2 * lax.rsqrt(ms + EPS) * nf_ref[...]


def _gather_scratch(n):
    return [pltpu.VMEM((2, 2, n * SUBLANES, LANES), F32), pltpu.SemaphoreType.DMA((2, 2))]


def _combine_inproj_kernel(pos_ref, x1_ref, route_ref, mod_ref, modn_ref, n1_ref, w_ref, dft_ref, os_ref,
                           x2_ref, yf_ref, qkv_ref, cv_ref, buf, sem):
    slot = _rows_ready(pos_ref, os_ref, buf, sem, TM)
    rows = TM // ROW_SPLIT
    for s in range(ROW_SPLIT):
        sl = pl.ds(s * rows, rows)
        x2 = _combined_rows(x1_ref, route_ref, mod_ref, buf, slot, s * rows, rows)
        x2_ref[sl, :] = x2
        _inproj_math(x2, modn_ref, n1_ref, w_ref, dft_ref, yf_ref.at[sl], qkv_ref.at[sl], cv_ref.at[sl])


def _combine_inproj(pos_flat, x1, route, mod4, norm1, w_in_bf, dft_ch, osrt, li):
    def tmap(i, pos):
        return (i, 0)

    return pl.pallas_call(
        _combine_inproj_kernel,
        name="moe_combine_in_projection",
        grid_spec=pltpu.PrefetchScalarGridSpec(
            num_scalar_prefetch=1,
            grid=(T_ALL // TM,),
            in_specs=[
                pl.BlockSpec((TM, D_MODEL), tmap),
                pl.BlockSpec((TM, LANES), tmap),
                pl.BlockSpec((None, None, 1, 6 * D_MODEL), lambda i, pos: (li, _mod_row(i, TM), 0, 0)),
                pl.BlockSpec((None, None, 1, 6 * D_MODEL), lambda i, pos: (li + 1, _mod_row(i, TM), 0, 0)),
                pl.BlockSpec((1, 1, D_MODEL), lambda i, pos: (li + 1, 0, 0)),
                pl.BlockSpec((1, D_MODEL, D_IN_PROJ), lambda i, pos: (li + 1, 0, 0)),
                pl.BlockSpec((D_FOURIER, 2 * D_FOURIER), lambda i, pos: (0, 0)),
                pl.BlockSpec(memory_space=pl.ANY),
            ],
            out_specs=[
                pl.BlockSpec((TM, D_MODEL), tmap),
                pl.BlockSpec((TM, 2 * D_FOURIER), tmap),
                pl.BlockSpec((TM, 3 * D_NA), tmap),
                pl.BlockSpec((TM, 3 * D_CONV), tmap),
            ],
            scratch_shapes=_gather_scratch(TM),
        ),
        out_shape=[
            jax.ShapeDtypeStruct((T_ALL, D_MODEL), F32),
            jax.ShapeDtypeStruct((T_ALL, 2 * D_FOURIER), BF16),
            jax.ShapeDtypeStruct((T_ALL, 3 * D_NA), BF16),
            jax.ShapeDtypeStruct((T_ALL, 3 * D_CONV), F32),
        ],
        compiler_params=_cparams("arbitrary"),
    )(pos_flat, x1, route, mod4, mod4, norm1.reshape(DEPTH, 1, D_MODEL), w_in_bf, dft_ch, osrt)


def _combine_final(pos_flat, x1, route, mod4, norm_final, osrt, li, n_tok):
    def tmap(i, pos):
        return (i, 0)

    return pl.pallas_call(
        _combine_final_kernel,
        name="moe_combine_final",
        grid_spec=pltpu.PrefetchScalarGridSpec(
            num_scalar_prefetch=1,
            grid=(n_tok // TC,),
            in_specs=[
                pl.BlockSpec((TC, D_MODEL), tmap),
                pl.BlockSpec((TC, LANES), tmap),
                pl.BlockSpec((None, None, 1, 6 * D_MODEL), lambda i, pos: (li, _mod_row(i, TC), 0, 0)),
                pl.BlockSpec((1, D_MODEL), lambda i, pos: (0, 0)),
                pl.BlockSpec(memory_space=pl.ANY),
            ],
            out_specs=pl.BlockSpec((TC, D_MODEL), tmap),
            scratch_shapes=_gather_scratch(TC),
        ),
        out_shape=jax.ShapeDtypeStruct((n_tok, D_MODEL), F32),
        compiler_params=_cparams("arbitrary"),
    )(pos_flat, x1, route, mod4, norm_final.reshape(1, D_MODEL), osrt)


def _moe_block(hx_tm, route, w_gate, w_up, w_down, li, n_tok):
    n_tiles = (2 * n_tok) // TMX + N_EXPERTS
    n_rows = n_tiles * TMX
    pos, cnt, ends = _plan(route, n_tok)
    pos = pos[:, 0:2].reshape(-1)
    cnt = cnt[0, :N_EXPERTS].astype(jnp.int32)
    ends = ends[0, :N_EXPERTS].astype(jnp.int32)
    padded = ((cnt + TMX - 1) // TMX) * TMX
    pad_start = ends - padded + cnt
    tile_start = jnp.arange(n_tiles, dtype=jnp.int32) * TMX
    tile_expert = jnp.minimum(jnp.sum((tile_start[:, None] >= ends[None, :]).astype(jnp.int32), axis=1),
                              N_EXPERTS - 1)
    n_active = (ends[-1:] // TMX).astype(jnp.int32)
    tail = jnp.concatenate([ends[-1:], (n_rows - ends[-1:]) // PAD_CHUNKS[0]])

    xs = _dispatch(pos, pad_start, padded - cnt, tail, hx_tm, n_tok, n_rows)
    return pos, _moe(tile_expert, n_active, padded > 0, xs, w_gate, w_up, w_down, li, n_tiles)


def kernel(x, c, ctx, c_ctx, w_ada, b_ada, norm1, norm2, w_in, w_fourier, w_conv, rpb, w_out, w_rg, b_rg,
           w_re, b_re, w_gate, w_up, w_down, norm_final):
    x_pair, ctx_first = (x.reshape(T_LAT, D_MODEL), ctx.reshape(T_CTX, D_MODEL)), 0
    cc =jnp.concatenate([c, c_ctx[None, :], jnp.zeros((MOD_ROWS - BATCH - 1, D_MODEL), F32)], axis=0)
    mod4 = _modulation(cc, w_ada, b_ada).reshape(DEPTH, MOD_ROWS, 1, 6 * D_MODEL)

    w_in_bf = w_in.astype(BF16)
    w_out_bf = w_out.astype(BF16)
    pad = jnp.zeros((DEPTH, D_MODEL, LANES - ROUTE_COLS), F32)
    w_r = jnp.concatenate([w_rg, w_re, pad], axis=-1)
    w_r_hi = w_r.astype(BF16)
    w_r = jnp.concatenate([w_r_hi, (w_r - w_r_hi.astype(F32)).astype(BF16)], axis=-1)
    b_r = jnp.concatenate([b_rg, b_re, pad[:, 0, :]], axis=-1).reshape(DEPTH, 1, LANES)
    dft_ch = jnp.asarray(_channel_dft(), dtype=F32).astype(BF16)
    ctx_blk = T_LAT // CTX_LEN

    for li in range(DEPTH):
        last = li == DEPTH - 1
        if li == 0:
            yf, qkv, cv = _in_projection(*x_pair, ctx_first, mod4, norm1, w_in_bf, dft_ch, li)
        table = _bias_table(rpb[li].reshape(-1))
        ff = _fourier_mix(yf, w_fourier, li, SEQ, 0)
        cvo = _short_conv(cv, w_conv, li, SEQ, 0)
        at = _attn_latent(qkv, table)
        if last:
            n_tok = T_LAT
            ff_c, cvo_c, at_c = ff, cvo, at
        else:
            n_tok = T_ALL
            ff_c = _fourier_mix(yf, w_fourier, li, CTX_LEN, ctx_blk)
            cvo_c = _short_conv(cv, w_conv, li, CTX_LEN, ctx_blk)
            at_c = _attn_ctx(qkv)
        x1, hx_tm, route = _out_projection(x_pair, ctx_first, (ff, ff_c), (at, at_c), (cvo, cvo_c), mod4, norm2,
                                           w_out_bf, w_r, b_r, li, n_tok)
        pos, osrt = _moe_block(hx_tm, route, w_gate, w_up, w_down, li, n_tok)
        if last:
            out = _combine_final(pos, x1, route, mod4, norm_final, osrt, li, n_tok)
            return out.reshape(BATCH, SEQ, D_MODEL)
        xt, yf, qkv, cv = _combine_inproj(pos, x1, route, mod4, norm1, w_in_bf, dft_ch, osrt, li)
        x_pair, ctx_first = (xt, xt), NT_LAT
```

```python
import functools
import math

import numpy as np
import jax
import jax.numpy as jnp
from jax import lax
from jax.experimental import pallas as pl
from jax.experimental.pallas import tpu as pltpu

F32 = jnp.float32
BF16 = jnp.bfloat16

D_MODEL = 1024
BATCH = 8
SEQ = 2048
DEPTH = 2
GRID_W = 64
ROWS = SEQ // GRID_W
CTX_LEN = 256
T_LAT = BATCH * SEQ
T_CTX = BATCH * CTX_LEN
T_ALL = T_LAT + T_CTX

D_FOURIER = 256
D_FG = 64
HEAD_DIM = 64
NA_HEADS = 8
D_NA = NA_HEADS * HEAD_DIM
D_CONV = 256
D_IN_PROJ = D_FOURIER + 3 * D_NA + 3 * D_CONV
NA_KH = 8
NA_KW = 16
N_GROUPS = 4
EXPERTS_PER_GROUP = 8
N_EXPERTS = 32
D_EXPERT = 512
EPS = 1e-6
NEG_INF = -1e30

LANES = 128
SUBLANES = 8
MOD_ROWS = 16
TM = 512
TMX = 512
TD = 512
TC = 512
VMEM_LIMIT = 56 * 1024 * 1024


def _cparams(*sem):
    return pltpu.CompilerParams(dimension_semantics=sem, vmem_limit_bytes=VMEM_LIMIT)


def _ada_kernel(cc_ref, w_ref, b_ref, o_ref):
    s = cc_ref[...]
    s = s * jax.nn.sigmoid(s)
    acc = jnp.dot(s.astype(BF16), w_ref[0].astype(BF16), preferred_element_type=F32)
    o_ref[0] = acc + b_ref[0]


def _modulation(cc, w_ada, b_ada):
    tn = 1536
    nj = 6 * D_MODEL // tn
    return pl.pallas_call(
        _ada_kernel,
        name="ada_modulation",
        grid=(DEPTH, nj),
        in_specs=[
            pl.BlockSpec((MOD_ROWS, D_MODEL), lambda l, j: (0, 0)),
            pl.BlockSpec((1, D_MODEL, tn), lambda l, j: (l, 0, j)),
            pl.BlockSpec((1, 1, tn), lambda l, j: (l, 0, j)),
        ],
        out_specs=pl.BlockSpec((1, MOD_ROWS, tn), lambda l, j: (l, 0, j)),
        out_shape=jax.ShapeDtypeStruct((DEPTH, MOD_ROWS, 6 * D_MODEL), F32),
        compiler_params=_cparams("arbitrary", "arbitrary"),
    )(cc, w_ada, b_ada.reshape(DEPTH, 1, 6 * D_MODEL))


def _mod_row(i, tile):
    return jnp.minimum((i * tile) // SEQ, BATCH)


NT_LAT = T_LAT // TM


def _lat_spec(cols):
    return pl.BlockSpec((TM, cols), lambda i: (jnp.minimum(i, NT_LAT - 1), 0))


def _ctx_spec(cols, first_block):
    return pl.BlockSpec((TM, cols), lambda i: (jnp.maximum(i - NT_LAT, 0) + first_block, 0))


def _pick_stream(lat_ref, ctx_ref):
    return jnp.where(pl.program_id(0) >= NT_LAT, ctx_ref[...], lat_ref[...])


def _inproj_kernel(xa_ref, xb_ref, mod_ref, n1_ref, w_ref, dft_ref, yf_ref, qkv_ref, cv_ref):
    rows = TM // ROW_SPLIT
    for s in range(ROW_SPLIT):
        sl = pl.ds(s * rows, rows)
        _inproj_math(_pick_stream(xa_ref.at[sl], xb_ref.at[sl]), mod_ref, n1_ref, w_ref, dft_ref,
                     yf_ref.at[sl], qkv_ref.at[sl], cv_ref.at[sl])


def _inproj_math(x, mod_ref, n1_ref, w_ref, dft_ref, yf_ref, qkv_ref, cv_ref):
    ms = jnp.mean(x * x, axis=-1, keepdims=True)
    y = x * lax.rsqrt(ms + EPS) * n1_ref[0]
    sh = mod_ref[:, 0:D_MODEL]
    sc = mod_ref[:, D_MODEL:2 * D_MODEL]
    h = y * (1.0 + sc) + sh
    u = jnp.dot(h.astype(BF16), w_ref[0], preferred_element_type=F32)
    yf = jnp.dot(u[:, 0:D_FOURIER].astype(BF16), dft_ref[...], preferred_element_type=F32)
    yf_ref[...] = yf.astype(BF16)
    q0 = D_FOURIER
    qkv_ref[:, 0:D_NA] = (u[:, q0:q0 + D_NA] * (1.0 / math.sqrt(HEAD_DIM))).astype(BF16)
    qkv_ref[:, D_NA:3 * D_NA] = u[:, q0 + D_NA:q0 + 3 * D_NA].astype(BF16)
    cv_ref[...] = u[:, q0 + 3 * D_NA:D_IN_PROJ]


def _in_projection(x_lat, x_ctx, ctx_first_block, mod4, norm1, w_in_bf, dft_ch, li):
    nt = T_ALL // TM
    return pl.pallas_call(
        _inproj_kernel,
        name="in_projection",
        grid=(nt,),
        in_specs=[
            _lat_spec(D_MODEL),
            _ctx_spec(D_MODEL, ctx_first_block),
            pl.BlockSpec((None, None, 1, 6 * D_MODEL), lambda i: (li, _mod_row(i, TM), 0, 0)),
            pl.BlockSpec((1, 1, D_MODEL), lambda i: (li, 0, 0)),
            pl.BlockSpec((1, D_MODEL, D_IN_PROJ), lambda i: (li, 0, 0)),
            pl.BlockSpec((D_FOURIER, 2 * D_FOURIER), lambda i: (0, 0)),
        ],
        out_specs=[
            pl.BlockSpec((TM, 2 * D_FOURIER), lambda i: (i, 0)),
            pl.BlockSpec((TM, 3 * D_NA), lambda i: (i, 0)),
            pl.BlockSpec((TM, 3 * D_CONV), lambda i: (i, 0)),
        ],
        out_shape=[
            jax.ShapeDtypeStruct((T_ALL, 2 * D_FOURIER), BF16),
            jax.ShapeDtypeStruct((T_ALL, 3 * D_NA), BF16),
            jax.ShapeDtypeStruct((T_ALL, 3 * D_CONV), F32),
        ],
        compiler_params=_cparams("arbitrary"),
    )(x_lat, x_ctx, mod4, norm1.reshape(DEPTH, 1, D_MODEL), w_in_bf, dft_ch)


def _dft_tables(n):
    k = np.arange(n, dtype=np.int64)
    ang = 2.0 * np.pi * ((k[:, None] * k[None, :]) % n).astype(np.float64) / n
    s = 1.0 / math.sqrt(n)
    return np.cos(ang) * s, np.sin(ang) * s


def _channel_dft():
    c, s = _dft_tables(D_FG)
    eye = np.eye(D_FOURIER // D_FG)
    return np.concatenate([np.kron(eye, c), np.kron(eye, s)], axis=1)


def _fourier_kernel(ch_ref, sh_ref, j_ref, y_ref, wf_ref, o_ref, fold_ref, rev_ref, *, n):
    h = n // 2
    nb = h // LANES
    df = D_FOURIER
    scale = 1.0 / math.sqrt(n)
    rev = j_ref[...]
    zero_rows = jnp.zeros((SUBLANES, 2 * df), F32)

    fold_ref[0:SUBLANES, :] = zero_rows
    for b in range(nb):
        blk = y_ref[n - LANES * (b + 1):n - LANES * b, :]
        fold_ref[SUBLANES + LANES * b:SUBLANES + LANES * (b + 1), :] = jnp.dot(rev, blk, preferred_element_type=F32)
    mirrored = fold_ref[SUBLANES - 1:SUBLANES - 1 + h, :]
    y_lo = y_ref[0:h, :].astype(F32)
    yc_even = (y_lo[:, 0:df] + mirrored[:, 0:df]).astype(BF16)
    ys_odd = (y_lo[:, df:2 * df] - mirrored[:, df:2 * df]).astype(BF16)

    parity = lax.broadcasted_iota(jnp.int32, (h, 1), 0) & 1
    sign = jnp.where(parity == 1, -scale, scale)
    y_mid = y_ref[h:h + 1, 0:df].astype(F32)
    p = jnp.dot(ch_ref[...], yc_even, preferred_element_type=F32) + sign * y_mid
    q = jnp.dot(sh_ref[...], ys_odd, preferred_element_type=F32)
    lo = (p - q).astype(BF16)
    w = (p + q).astype(BF16)
    mid = jnp.sum(sign * (y_lo[:, 0:df] + y_ref[h:n, 0:df].astype(F32)), axis=0, keepdims=True)

    for b in range(nb):
        blk = w[h - LANES * (b + 1):h - LANES * b, :]
        rev_ref[SUBLANES + LANES * b:SUBLANES + LANES * (b + 1), :] = jnp.dot(rev, blk, preferred_element_type=F32)
    rev_ref[0:SUBLANES, :] = jnp.broadcast_to(mid, (SUBLANES, df))
    hi = rev_ref[SUBLANES - 1:SUBLANES - 1 + h, :].astype(BF16)
    z = jnp.concatenate([lo, hi], axis=0)
    o_ref[...] = jnp.dot(z, wf_ref[0].astype(BF16), preferred_element_type=F32).astype(BF16)


def _fourier_mix(yf, w_fourier, li, n, first_block):
    h = n // 2
    cn, sn = _dft_tables(n)
    ch = jnp.asarray(cn[:h, :h], dtype=F32).astype(BF16)
    sh = jnp.asarray(sn[:h, :h], dtype=F32).astype(BF16)
    anti = jnp.asarray(np.eye(LANES)[::-1].copy(), dtype=F32).astype(BF16)
    return pl.pallas_call(
        functools.partial(_fourier_kernel, n=n),
        name="fourier_mix",
        grid=(BATCH,),
        in_specs=[
            pl.BlockSpec((h, h), lambda b: (0, 0)),
            pl.BlockSpec((h, h), lambda b: (0, 0)),
            pl.BlockSpec((LANES, LANES), lambda b: (0, 0)),
            pl.BlockSpec((n, 2 * D_FOURIER), lambda b: (first_block + b, 0)),
            pl.BlockSpec((1, D_FOURIER, D_FOURIER), lambda b: (li, 0, 0)),
        ],
        out_specs=pl.BlockSpec((n, D_FOURIER), lambda b: (b, 0)),
        out_shape=jax.ShapeDtypeStruct((BATCH * n, D_FOURIER), BF16),
        scratch_shapes=[pltpu.VMEM((h + SUBLANES, 2 * D_FOURIER), F32), pltpu.VMEM((h + SUBLANES, D_FOURIER), F32)],
        compiler_params=_cparams("arbitrary"),
    )(ch, sh, anti, yf, w_fourier)


def _conv_kernel(cv_ref, wc_ref, o_ref, pad_ref):
    n = cv_ref.shape[0]
    gb = cv_ref[:, 0:D_CONV]
    g = cv_ref[:, D_CONV:2 * D_CONV] * cv_ref[:, 2 * D_CONV:3 * D_CONV]
    zero = jnp.zeros((SUBLANES, D_CONV), F32)
    pad_ref[0:SUBLANES, :] = zero
    pad_ref[n + SUBLANES:n + 2 * SUBLANES, :] = zero
    pad_ref[SUBLANES:n + SUBLANES, :] = g
    prev = pad_ref[SUBLANES - 1:n + SUBLANES - 1, :]
    nxt = pad_ref[SUBLANES + 1:n + SUBLANES + 1, :]
    w = wc_ref[0]
    o_ref[...] = (gb * (w[0:1, :] * prev + w[1:2, :] * g + w[2:3, :] * nxt)).astype(BF16)


def _short_conv(cv, w_conv, li, n, first_block):
    return pl.pallas_call(
        _conv_kernel,
        name="short_conv",
        grid=(BATCH,),
        in_specs=[
            pl.BlockSpec((n, 3 * D_CONV), lambda b: (first_block + b, 0)),
            pl.BlockSpec((1, 3, D_CONV), lambda b: (li, 0, 0)),
        ],
        out_specs=pl.BlockSpec((n, D_CONV), lambda b: (b, 0)),
        out_shape=jax.ShapeDtypeStruct((BATCH * n, D_CONV), BF16),
        scratch_shapes=[pltpu.VMEM((n + 2 * SUBLANES, D_CONV), F32)],
        compiler_params=_cparams("arbitrary"),
    )(cv, w_conv)


RPB_H = 2 * NA_KH - 1
RPB_W = 2 * NA_KW - 1
KEYS_LOC = NA_KH * GRID_W


def _bias_kernel(rpb_ref, o_ref):
    h = pl.program_id(0)
    qi = lax.broadcasted_iota(jnp.int32, (GRID_W, LANES), 0)
    lj = lax.broadcasted_iota(jnp.int32, (GRID_W, LANES), 1)
    kc = lj & (GRID_W - 1)
    hi = lj >= GRID_W
    d = kc - qi + (NA_KW - 1)
    cs = jnp.clip(qi - NA_KW // 2, 0, GRID_W - NA_KW)
    valid = (kc >= cs) & (kc < cs + NA_KW)
    tiles = []
    for a in range(RPB_H - 1):
        acc = jnp.zeros((GRID_W, LANES), F32)
        for b in range(RPB_W):
            va = rpb_ref[(h * RPB_H + a) * RPB_W + b]
            vb = rpb_ref[(h * RPB_H + a + 1) * RPB_W + b]
            acc = jnp.where(d == b, jnp.where(hi, vb, va), acc)
        tiles.append(jnp.where(valid, acc, NEG_INF))
    for cls in range(NA_KH):
        for m in range(NA_KH // 2):
            o_ref[0, cls, :, m * LANES:(m + 1) * LANES] = tiles[2 * m - cls + NA_KH - 1]


def _bias_table(rpb_flat):
    return pl.pallas_call(
        _bias_kernel,
        name="attn_bias_table",
        grid=(NA_HEADS,),
        in_specs=[pl.BlockSpec(memory_space=pltpu.SMEM)],
        out_specs=pl.BlockSpec((1, NA_KH, GRID_W, KEYS_LOC), lambda h: (h // 2, 0, h % 2, 0)),
        out_shape=jax.ShapeDtypeStruct((NA_HEADS // 2, NA_KH, 2 * GRID_W, KEYS_LOC), F32),
        compiler_params=_cparams("arbitrary"),
    )(rpb_flat)


_NT_DIMS = (((1,), (1,)), ((), ()))


def _softmax_pv(parts):
    m = None
    for s, _ in parts:
        ms = jnp.max(s, axis=1, keepdims=True)
        m = ms if m is None else jnp.maximum(m, ms)
    l = None
    o = None
    for s, v in parts:
        p = jnp.exp(s - m)
        ls = jnp.sum(p, axis=1, keepdims=True)
        os_ = jnp.dot(p.astype(BF16), v, preferred_element_type=F32)
        l = ls if l is None else l + ls
        o = os_ if o is None else o + os_
    return o / l


def _attn_latent_kernel(q_ref, k_ref, v_ref, kc_ref, vc_ref, tab_ref, o_ref, ve_ref, vce_ref):
    lane = lax.broadcasted_iota(jnp.int32, (GRID_W, LANES), 1)
    lo = lane < HEAD_DIM
    ve_ref[:, 0:LANES] = v_ref[...]
    ve_ref[:, LANES:2 * LANES] = jnp.ones((SEQ, LANES), BF16)
    vce_ref[:, 0:LANES] = vc_ref[...]
    vce_ref[:, LANES:2 * LANES] = jnp.ones((CTX_LEN, LANES), BF16)
    kc = kc_ref[...]
    vce = vce_ref[...]

    for r in range(ROWS):
        rs = min(max(r - NA_KH // 2, 0), ROWS - NA_KH)
        cls = r - rs
        q = q_ref[r * GRID_W:(r + 1) * GRID_W, :]
        kl = k_ref[rs * GRID_W:rs * GRID_W + KEYS_LOC, :]
        vl = ve_ref[rs * GRID_W:rs * GRID_W + KEYS_LOC, :]
        zero = jnp.zeros_like(q)
        q2 = jnp.concatenate([jnp.where(lo, q, zero), jnp.where(lo, zero, q)], axis=0)
        s1 = lax.dot_general(q2, kl, _NT_DIMS, preferred_element_type=F32) + tab_ref[0, cls]
        s2 = lax.dot_general(q2, kc, _NT_DIMS, preferred_element_type=F32)
        m = jnp.maximum(jnp.max(s1, axis=1, keepdims=True), jnp.max(s2, axis=1, keepdims=True))
        p1 = jnp.exp(s1 - m).astype(BF16)
        p2 = jnp.exp(s2 - m).astype(BF16)
        oe = (jnp.dot(p1, vl, preferred_element_type=F32) + jnp.dot(p2, vce, preferred_element_type=F32))
        o = oe[:, 0:LANES] / oe[:, LANES:2 * LANES]
        o_ref[r * GRID_W:(r + 1) * GRID_W, :] = jnp.where(lo, o[0:GRID_W], o[GRID_W:2 * GRID_W]).astype(BF16)


def _attn_latent(qkv, table):
    npair = NA_HEADS // 2
    cblk = T_LAT // CTX_LEN
    return pl.pallas_call(
        _attn_latent_kernel,
        name="attn_latent",
        grid=(BATCH, npair),
        in_specs=[
            pl.BlockSpec((SEQ, LANES), lambda b, p: (b, p)),
            pl.BlockSpec((SEQ, LANES), lambda b, p: (b, npair + p)),
            pl.BlockSpec((SEQ, LANES), lambda b, p: (b, 2 * npair + p)),
            pl.BlockSpec((CTX_LEN, LANES), lambda b, p: (cblk + b, npair + p)),
            pl.BlockSpec((CTX_LEN, LANES), lambda b, p: (cblk + b, 2 * npair + p)),
            pl.BlockSpec((1, NA_KH, 2 * GRID_W, KEYS_LOC), lambda b, p: (p, 0, 0, 0)),
        ],
        out_specs=pl.BlockSpec((SEQ, LANES), lambda b, p: (b, p)),
        out_shape=jax.ShapeDtypeStruct((T_LAT, D_NA), BF16),
        scratch_shapes=[pltpu.VMEM((SEQ, 2 * LANES), BF16), pltpu.VMEM((CTX_LEN, 2 * LANES), BF16)],
        compiler_params=_cparams("arbitrary", "arbitrary"),
    )(qkv, qkv, qkv, qkv, qkv, table)


def _attn_ctx_kernel(q_ref, k_ref, v_ref, o_ref):
    lane = lax.broadcasted_iota(jnp.int32, (CTX_LEN, LANES), 1)
    lo = lane < HEAD_DIM
    q = q_ref[...]
    k = k_ref[...]
    v = v_ref[...]
    outs = []
    for hh in range(2):
        qm = jnp.where(lo if hh == 0 else jnp.logical_not(lo), q, jnp.zeros_like(q))
        s = lax.dot_general(qm, k, _NT_DIMS, preferred_element_type=F32)
        outs.append(_softmax_pv([(s, v)]))
    o_ref[...] = jnp.where(lo, outs[0], outs[1]).astype(BF16)


def _attn_ctx(qkv):
    npair = NA_HEADS // 2
    cblk = T_LAT // CTX_LEN
    return pl.pallas_call(
        _attn_ctx_kernel,
        name="attn_context",
        grid=(BATCH, npair),
        in_specs=[
            pl.BlockSpec((CTX_LEN, LANES), lambda b, p: (cblk + b, p)),
            pl.BlockSpec((CTX_LEN, LANES), lambda b, p: (cblk + b, npair + p)),
            pl.BlockSpec((CTX_LEN, LANES), lambda b, p: (cblk + b, 2 * npair + p)),
        ],
        out_specs=pl.BlockSpec((CTX_LEN, LANES), lambda b, p: (b, p)),
        out_shape=jax.ShapeDtypeStruct((T_CTX, D_NA), BF16),
        compiler_params=_cparams("arbitrary", "arbitrary"),
    )(qkv, qkv, qkv)


ROUTE_COLS = N_GROUPS + N_EXPERTS


def _to_token_major(ref, val):
    rows = val.shape[0]
    for j in range(D_MODEL // LANES):
        ref[pl.ds(j, rows, stride=SUBLANES), :] = val[:, j * LANES:(j + 1) * LANES]


def _from_token_major(ref, rows, j):
    return ref[pl.ds(j, rows, stride=SUBLANES), :]


OUT_SPLIT = 2


def _outproj_kernel(xa_ref, xb_ref, ffa_ref, ffb_ref, ata_ref, atb_ref, cva_ref, cvb_ref,
                    mod_ref, n2_ref, wo_ref, wr_ref, br_ref, x1_ref, hx_ref, route_ref):
    rows = TM // OUT_SPLIT
    for s in range(OUT_SPLIT):
        sl = pl.ds(s * rows, rows)
        ins = [r.at[sl] for r in (xa_ref, xb_ref, ffa_ref, ffb_ref, ata_ref, atb_ref, cva_ref, cvb_ref)]
        _outproj_rows(*ins, mod_ref, n2_ref, wo_ref, wr_ref, br_ref, x1_ref.at[sl],
                      hx_ref.at[pl.ds(s * rows * SUBLANES, rows * SUBLANES)], route_ref.at[sl])


def _outproj_rows(xa_ref, xb_ref, ffa_ref, ffb_ref, ata_ref, atb_ref, cva_ref, cvb_ref,
                  mod_ref, n2_ref, wo_ref, wr_ref, br_ref, x1_ref, hx_ref, route_ref):
    r1 = D_FOURIER
    r2 = D_FOURIER + D_NA
    mix = (jnp.dot(_pick_stream(ffa_ref, ffb_ref), wo_ref[0, 0:r1, :], preferred_element_type=F32)
           + jnp.dot(_pick_stream(ata_ref, atb_ref), wo_ref[0, r1:r2, :], preferred_element_type=F32)
           + jnp.dot(_pick_stream(cva_ref, cvb_ref), wo_ref[0, r2:D_MODEL, :], preferred_element_type=F32))
    g1 = mod_ref[:, 2 * D_MODEL:3 * D_MODEL]
    x1 = _pick_stream(xa_ref, xb_ref) + g1 * mix
    x1_ref[...] = x1
    ms = jnp.mean(x1 * x1, axis=-1, keepdims=True)
    y = x1 * lax.rsqrt(ms + EPS) * n2_ref[0]
    sh2 = mod_ref[:, 3 * D_MODEL:4 * D_MODEL]
    sc2 = mod_ref[:, 4 * D_MODEL:5 * D_MODEL]
    hx = y * (1.0 + sc2) + sh2
    _to_token_major(hx_ref, hx)

    hx_hi = hx.astype(BF16)
    hx_lo = (hx - hx_hi.astype(F32)).astype(BF16)
    part = (jnp.dot(hx_hi, wr_ref[0], preferred_element_type=F32)
            + jnp.dot(hx_lo, wr_ref[0], preferred_element_type=F32))
    logits = part[:, 0:LANES] + part[:, LANES:2 * LANES] + br_ref[0]
    tm = logits.shape[0]
    lane = lax.broadcasted_iota(jnp.int32, (tm, LANES), 1)
    lane_f = lane.astype(F32)
    group_of_lane = ((lane - N_GROUPS) >> 3).astype(F32)
    big = float(LANES)
    gl = jnp.where(lane < N_GROUPS, logits, -jnp.inf)
    gmax = jnp.max(gl, axis=1, keepdims=True)
    gidx = jnp.min(jnp.where(gl == gmax, lane_f, big), axis=1, keepdims=True)
    g_w = 1.0 / jnp.sum(jnp.exp(gl - gmax), axis=1, keepdims=True)
    in_group = (lane >= N_GROUPS) & (lane < N_GROUPS + N_EXPERTS) & (group_of_lane == gidx)
    es = jnp.where(in_group, logits, -jnp.inf)
    t1 = jnp.max(es, axis=1, keepdims=True)
    i1 = jnp.min(jnp.where(es == t1, lane_f, big), axis=1, keepdims=True)
    es2 = jnp.where(lane_f == i1, -jnp.inf, es)
    t2 = jnp.max(es2, axis=1, keepdims=True)
    i2 = jnp.min(jnp.where(es2 == t2, lane_f, big), axis=1, keepdims=True)
    dlt = jnp.exp(t2 - t1)
    w1 = g_w / (1.0 + dlt)
    w2 = g_w * dlt / (1.0 + dlt)
    e1 = i1 - float(N_GROUPS)
    e2 = i2 - float(N_GROUPS)
    route_ref[...] = jnp.where(lane == 0, e1, jnp.where(lane == 1, e2,
                               jnp.where(lane == 2, w1, jnp.where(lane == 3, w2, 0.0))))


def _out_projection(x_pair, ctx_first_block, ff_pair, at_pair, cv_pair, mod4, norm2, w_out_bf, w_r, b_r, li, n_tok):
    nt = n_tok // TM
    return pl.pallas_call(
        _outproj_kernel,
        name="out_projection",
        grid=(nt,),
        in_specs=[
            _lat_spec(D_MODEL), _ctx_spec(D_MODEL, ctx_first_block),
            _lat_spec(D_FOURIER), _ctx_spec(D_FOURIER, 0),
            _lat_spec(D_NA), _ctx_spec(D_NA, 0),
            _lat_spec(D_CONV), _ctx_spec(D_CONV, 0),
            pl.BlockSpec((None, None, 1, 6 * D_MODEL), lambda i: (li, _mod_row(i, TM), 0, 0)),
            pl.BlockSpec((1, 1, D_MODEL), lambda i: (li, 0, 0)),
            pl.BlockSpec((1, D_MODEL, D_MODEL), lambda i: (li, 0, 0)),
            pl.BlockSpec((1, D_MODEL, 2 * LANES), lambda i: (li, 0, 0)),
            pl.BlockSpec((1, 1, LANES), lambda i: (li, 0, 0)),
        ],
        out_specs=[
            pl.BlockSpec((TM, D_MODEL), lambda i: (i, 0)),
            pl.BlockSpec((TM * SUBLANES, LANES), lambda i: (i, 0)),
            pl.BlockSpec((TM, LANES), lambda i: (i, 0)),
        ],
        out_shape=[
            jax.ShapeDtypeStruct((n_tok, D_MODEL), F32),
            jax.ShapeDtypeStruct((n_tok * SUBLANES, LANES), F32),
            jax.ShapeDtypeStruct((n_tok, LANES), F32),
        ],
        compiler_params=_cparams("arbitrary"),
    )(*x_pair, *ff_pair, *at_pair, *cv_pair, mod4, norm2.reshape(DEPTH, 1, D_MODEL), w_out_bf, w_r, b_r)


def _plan_kernel(route_ref, pos_ref, cnt_ref, ends_ref, carry_ref, offs_ref):
    ph = pl.program_id(0)
    i = pl.program_id(1)
    tm = PLAN_SUB
    lane = lax.broadcasted_iota(jnp.int32, (tm, LANES), 1)

    def one_hots(k):
        r = route_ref[k * tm:(k + 1) * tm, :]
        oh1 = lane == r[:, 0:1].astype(jnp.int32)
        oh2 = lane == r[:, 1:2].astype(jnp.int32)
        return oh1, oh2, jnp.where(oh1 | oh2, 1.0, 0.0)

    @pl.when((ph == 0) & (i == 0))
    def _():
        carry_ref[...] = jnp.zeros_like(carry_ref)

    @pl.when(ph == 0)
    def _():
        total = carry_ref[...]
        for k in range(PLAN_TM // tm):
            total = total + jnp.sum(one_hots(k)[2], axis=0, keepdims=True)
        carry_ref[...] = total

    @pl.when((ph == 1) & (i == 0))
    def _():
        cnt = carry_ref[...]
        cnt_ref[...] = jnp.broadcast_to(cnt, cnt_ref.shape)
        tiles = jnp.ceil(cnt * (1.0 / TMX))
        a = lax.broadcasted_iota(jnp.int32, (LANES, LANES), 0)
        b = lax.broadcasted_iota(jnp.int32, (LANES, LANES), 1)
        upper = jnp.where(a < b, 1.0, 0.0).astype(BF16)
        tiles8 = jnp.broadcast_to(tiles, (SUBLANES, LANES))
        first = jnp.dot(tiles8.astype(BF16), upper, preferred_element_type=F32)
        offs_ref[...] = first[0:1, :] * TMX
        ends_ref[...] = (first + tiles8) * TMX
        carry_ref[...] = jnp.zeros_like(carry_ref)

    @pl.when(ph == 1)
    def _():
        row = lax.broadcasted_iota(jnp.int32, (tm, tm), 0)
        col = lax.broadcasted_iota(jnp.int32, (tm, tm), 1)
        tri = jnp.where(row > col, 1.0, 0.0).astype(BF16)
        base = carry_ref[...] + offs_ref[...]
        for k in range(PLAN_TM // tm):
            oh1, oh2, oh = one_hots(k)
            cum = jnp.dot(tri, oh.astype(BF16), preferred_element_type=F32) + base
            p1 = jnp.sum(jnp.where(oh1, cum, 0.0), axis=1, keepdims=True)
            p2 = jnp.sum(jnp.where(oh2, cum, 0.0), axis=1, keepdims=True)
            pos_ref[k * tm:(k + 1) * tm, :] = (
                jnp.where(lane == 0, p1, jnp.where(lane == 1, p2, 0.0)).astype(jnp.int32))
            base = base + jnp.sum(oh, axis=0, keepdims=True)
        carry_ref[...] = base - offs_ref[...]


PLAN_TM = 2048
PLAN_SUB = 512


def _plan(route, n_tok):
    nt = n_tok // PLAN_TM
    return pl.pallas_call(
        _plan_kernel,
        name="moe_plan",
        grid=(2, nt),
        in_specs=[pl.BlockSpec((PLAN_TM, LANES), lambda ph, i: (i, 0))],
        out_specs=[
            pl.BlockSpec((PLAN_TM, LANES), lambda ph, i: (i * ph, 0)),
            pl.BlockSpec((SUBLANES, LANES), lambda ph, i: (0, 0)),
            pl.BlockSpec((SUBLANES, LANES), lambda ph, i: (0, 0)),
        ],
        out_shape=[
            jax.ShapeDtypeStruct((n_tok, LANES), jnp.int32),
            jax.ShapeDtypeStruct((SUBLANES, LANES), F32),
            jax.ShapeDtypeStruct((SUBLANES, LANES), F32),
        ],
        scratch_shapes=[pltpu.VMEM((1, LANES), F32), pltpu.VMEM((1, LANES), F32)],
        compiler_params=_cparams("arbitrary", "arbitrary"),
    )(route)


def _row_tile(ref, row):
    return ref.at[pl.ds(pl.multiple_of(row * SUBLANES, SUBLANES), SUBLANES), :]


DISPATCH_UNROLL = 8
PAD_CHUNKS = tuple(1 << b for b in reversed(range(TMX.bit_length() - 1)))


def _dispatch_kernel(pos_ref, pstart_ref, plen_ref, tail_ref, hx_ref, xs_ref, zbuf, sem, zsem):
    i = pl.program_id(0)

    def pad_copies(fn):
        def ebody(e, carry):
            n = plen_ref[e]
            off = pstart_ref[e]
            for rows in PAD_CHUNKS:
                @pl.when((n & rows) != 0)
                def _():
                    fn(pltpu.make_async_copy(
                        zbuf.at[pl.ds(0, rows * SUBLANES), :],
                        xs_ref.at[pl.ds(pl.multiple_of(off * SUBLANES, SUBLANES), rows * SUBLANES), :], zsem.at[0]))
                off = off + (n & rows)
            return carry

        lax.fori_loop(0, N_EXPERTS, ebody, 0)

        def tbody(k, carry):
            row = pl.multiple_of((tail_ref[0] + k * PAD_CHUNKS[0]) * SUBLANES, SUBLANES)
            fn(pltpu.make_async_copy(zbuf, xs_ref.at[pl.ds(row, PAD_CHUNKS[0] * SUBLANES), :], zsem.at[0]))
            return carry

        lax.fori_loop(0, tail_ref[1], tbody, 0)

    @pl.when(i == 0)
    def _():
        zbuf[...] = jnp.zeros_like(zbuf)
        pad_copies(lambda cp: cp.start())

    def body(k, carry):
        for u in range(DISPATCH_UNROLL):
            t = k * DISPATCH_UNROLL + u
            src = _row_tile(hx_ref, t)
            for j in range(2):
                pltpu.make_async_copy(src, _row_tile(xs_ref, pos_ref[(i * TD + t) * 2 + j]),
                                      sem.at[0]).start(priority=j)
        return carry

    lax.fori_loop(0, TD // DISPATCH_UNROLL, body, 0)
    nrow = TD * SUBLANES
    for j in range(2):
        pltpu.make_async_copy(hx_ref, xs_ref.at[pl.ds(0, nrow), :], sem.at[0]).wait()

    @pl.when(i == 0)
    def _():
        pad_copies(lambda cp: cp.wait())


def _dispatch(pos_flat, pad_start, pad_len, tail, hx_tm, n_tok, n_rows):
    return pl.pallas_call(
        _dispatch_kernel,
        name="moe_dispatch",
        grid_spec=pltpu.PrefetchScalarGridSpec(
            num_scalar_prefetch=4,
            grid=(n_tok // TD,),
            in_specs=[pl.BlockSpec((TD * SUBLANES, LANES), lambda i, pos, ps, pn, tl: (i, 0))],
            out_specs=pl.BlockSpec(memory_space=pl.ANY),
            scratch_shapes=[pltpu.VMEM((PAD_CHUNKS[0] * SUBLANES, LANES), F32),
                            pltpu.SemaphoreType.DMA((1,)), pltpu.SemaphoreType.DMA((1,))],
        ),
        out_shape=jax.ShapeDtypeStruct((n_rows * SUBLANES, LANES), F32),
        compiler_params=_cparams("arbitrary"),
    )(pos_flat, pad_start, pad_len, tail, hx_tm)


XS_SLOTS = 3


def _moe_kernel(te_ref, na_ref, first_ref, wslot_ref, nxt_ref, xs_hbm, wg_hbm, wu_hbm, wd_hbm, os_ref,
                xbuf, wg_f, wu_f, wd_f, wg_s, wu_s, wd_s, xsem, wsem, *, li):
    i = pl.program_id(0)
    n_act = na_ref[0]
    active = i < n_act

    def xs_copy(tile, slot):
        rows = pl.ds(pl.multiple_of(tile * (TMX * SUBLANES), SUBLANES), TMX * SUBLANES)
        return pltpu.make_async_copy(xs_hbm.at[rows, :], xbuf.at[slot], xsem.at[slot])

    @pl.when(i == 0)
    def _():
        xs_copy(0, 0).start()

        @pl.when(n_act > 1)
        def _():
            xs_copy(1, 1).start()

    @pl.when(i + 2 < n_act)
    def _():
        xs_copy(i + 2, (i + 2) % XS_SLOTS).start()

    def weight_copies(e, slot):
        return [pltpu.make_async_copy(wg_hbm.at[li, e], wg_f.at[slot], wsem.at[slot, 0]),
                pltpu.make_async_copy(wu_hbm.at[li, e], wu_f.at[slot], wsem.at[slot, 1]),
                pltpu.make_async_copy(wd_hbm.at[li, e], wd_f.at[slot], wsem.at[slot, 2])]

    @pl.when(active & (first_ref[i] == 1))
    def _():
        slot = wslot_ref[i]

        @pl.when(i == 0)
        def _():
            for cp in weight_copies(te_ref[0], 0):
                cp.start()

        @pl.when(nxt_ref[i] >= 0)
        def _():
            for cp in weight_copies(nxt_ref[i], 1 - slot):
                cp.start()

        for cp in weight_copies(te_ref[i], slot):
            cp.wait()
        wg_s[...] = wg_f[slot].astype(BF16)
        wu_s[...] = wu_f[slot].astype(BF16)
        wd_s[...] = wd_f[slot].astype(BF16)

    @pl.when(active)
    def _():
        xs_copy(i, i % XS_SLOTS).wait()
        xs_ref = xbuf.at[i % XS_SLOTS]
        x = jnp.concatenate([_from_token_major(xs_ref, TMX, j).astype(BF16)
                             for j in range(D_MODEL // LANES)], axis=-1)
        g = jnp.dot(x, wg_s[...], preferred_element_type=F32)
        u = jnp.dot(x, wu_s[...], preferred_element_type=F32)
        h = (g * jax.nn.sigmoid(g) * u).astype(BF16)
        _to_token_major(os_ref, jnp.dot(h, wd_s[...], preferred_element_type=F32))

    @pl.when(jnp.logical_not(active))
    def _():
        os_ref[...] = jnp.zeros_like(os_ref)


def _moe(tile_expert, n_active, has_rows, xs, w_gate, w_up, w_down, li, n_tiles):
    eid = jnp.arange(N_EXPERTS, dtype=jnp.int32)
    later = jnp.where(has_rows[None, :] & (eid[None, :] > eid[:, None]), eid[None, :], N_EXPERTS)
    next_of_expert = jnp.min(later, axis=1)
    next_of_expert = jnp.where(next_of_expert == N_EXPERTS, -1, next_of_expert)
    first = jnp.concatenate([jnp.ones((1,), jnp.int32),
                             (tile_expert[1:] != tile_expert[:-1]).astype(jnp.int32)])
    wslot = (jnp.cumsum(first) - 1) % 2
    nxt = jnp.sum(jnp.where(tile_expert[:, None] == eid[None, :], next_of_expert[None, :], 0), axis=1)


    return pl.pallas_call(
        functools.partial(_moe_kernel, li=li),
        name="moe_experts",
        grid_spec=pltpu.PrefetchScalarGridSpec(
            num_scalar_prefetch=5,
            grid=(n_tiles,),
            in_specs=[
                pl.BlockSpec(memory_space=pl.ANY),
                pl.BlockSpec(memory_space=pl.ANY),
                pl.BlockSpec(memory_space=pl.ANY),
                pl.BlockSpec(memory_space=pl.ANY),
            ],
            out_specs=pl.BlockSpec((TMX * SUBLANES, LANES), lambda i, te, na, fi, ws, nx: (i, 0)),
            scratch_shapes=[
                pltpu.VMEM((XS_SLOTS, TMX * SUBLANES, LANES), F32),
                pltpu.VMEM((2, D_MODEL, D_EXPERT), F32),
                pltpu.VMEM((2, D_MODEL, D_EXPERT), F32),
                pltpu.VMEM((2, D_EXPERT, D_MODEL), F32),
                pltpu.VMEM((D_MODEL, D_EXPERT), BF16),
                pltpu.VMEM((D_MODEL, D_EXPERT), BF16),
                pltpu.VMEM((D_EXPERT, D_MODEL), BF16),
                pltpu.SemaphoreType.DMA((XS_SLOTS,)),
                pltpu.SemaphoreType.DMA((2, 3)),
            ],
        ),
        out_shape=jax.ShapeDtypeStruct((n_tiles * TMX * SUBLANES, LANES), F32),
        compiler_params=_cparams("arbitrary"),
    )(tile_expert, n_active, first, wslot.astype(jnp.int32), nxt.astype(jnp.int32), xs, w_gate, w_up, w_down)


GATHER_UNROLL = 16


def _start_row_gathers(pos_ref, os_ref, buf, sem, tile, slot, n):
    def body(k, carry):
        for u in range(GATHER_UNROLL):
            t = k * GATHER_UNROLL + u
            for j in range(2):
                pltpu.make_async_copy(_row_tile(os_ref, pos_ref[(tile * n + t) * 2 + j]),
                                      _row_tile(buf.at[slot, j], t), sem.at[slot, j]).start(priority=j)
        return carry

    lax.fori_loop(0, n // GATHER_UNROLL, body, 0)


def _rows_ready(pos_ref, os_ref, buf, sem, n):
    i = pl.program_id(0)
    slot = i % 2

    @pl.when(i == 0)
    def _():
        _start_row_gathers(pos_ref, os_ref, buf, sem, 0, 0, n)

    @pl.when(i + 1 < pl.num_programs(0))
    def _():
        _start_row_gathers(pos_ref, os_ref, buf, sem, i + 1, 1 - slot, n)

    for j in range(2):
        pltpu.make_async_copy(os_ref.at[pl.ds(0, n * SUBLANES), :], buf.at[slot, j], sem.at[slot, j]).wait()
    return slot


def _combined_rows(x1_ref, route_ref, mod_ref, buf, slot, r0, rows):
    sl = pl.ds(r0, rows)
    w1 = route_ref[sl, 2:3]
    w2 = route_ref[sl, 3:4]
    tm_rows = pl.ds(r0 * SUBLANES, rows * SUBLANES)
    cols = []
    for c in range(D_MODEL // LANES):
        y = (w1 * _from_token_major(buf.at[slot, 0, tm_rows], rows, c)
             + w2 * _from_token_major(buf.at[slot, 1, tm_rows], rows, c))
        g2 = mod_ref[:, 5 * D_MODEL + c * LANES:5 * D_MODEL + (c + 1) * LANES]
        cols.append(x1_ref[sl, c * LANES:(c + 1) * LANES] + g2 * y)
    return jnp.concatenate(cols, axis=-1)


ROW_SPLIT = 2


def _combine_final_kernel(pos_ref, x1_ref, route_ref, mod_ref, nf_ref, os_ref, o_ref, buf, sem):
    slot = _rows_ready(pos_ref, os_ref, buf, sem, TC)
    x2 = _combined_rows(x1_ref, route_ref, mod_ref, buf, slot, 0, TC)
    ms = jnp.mean(x2 * x2, axis=-1, keepdims=True)
    o_ref[...] = x2 * lax.rsqrt(ms + EPS) * nf_ref[...]


def _gather_scratch(n):
    return [pltpu.VMEM((2, 2, n * SUBLANES, LANES), F32), pltpu.SemaphoreType.DMA((2, 2))]


def _combine_inproj_kernel(pos_ref, x1_ref, route_ref, mod_ref, modn_ref, n1_ref, w_ref, dft_ref, os_ref,
                           x2_ref, yf_ref, qkv_ref, cv_ref, buf, sem):
    slot = _rows_ready(pos_ref, os_ref, buf, sem, TM)
    rows = TM // ROW_SPLIT
    for s in range(ROW_SPLIT):
        sl = pl.ds(s * rows, rows)
        x2 = _combined_rows(x1_ref, route_ref, mod_ref, buf, slot, s * rows, rows)
        x2_ref[sl, :] = x2
        _inproj_math(x2, modn_ref, n1_ref, w_ref, dft_ref, yf_ref.at[sl], qkv_ref.at[sl], cv_ref.at[sl])


def _combine_inproj(pos_flat, x1, route, mod4, norm1, w_in_bf, dft_ch, osrt, li):
    def tmap(i, pos):
        return (i, 0)

    return pl.pallas_call(
        _combine_inproj_kernel,
        name="moe_combine_in_projection",
        grid_spec=pltpu.PrefetchScalarGridSpec(
            num_scalar_prefetch=1,
            grid=(T_ALL // TM,),
            in_specs=[
                pl.BlockSpec((TM, D_MODEL), tmap),
                pl.BlockSpec((TM, LANES), tmap),
                pl.BlockSpec((None, None, 1, 6 * D_MODEL), lambda i, pos: (li, _mod_row(i, TM), 0, 0)),
                pl.BlockSpec((None, None, 1, 6 * D_MODEL), lambda i, pos: (li + 1, _mod_row(i, TM), 0, 0)),
                pl.BlockSpec((1, 1, D_MODEL), lambda i, pos: (li + 1, 0, 0)),
                pl.BlockSpec((1, D_MODEL, D_IN_PROJ), lambda i, pos: (li + 1, 0, 0)),
                pl.BlockSpec((D_FOURIER, 2 * D_FOURIER), lambda i, pos: (0, 0)),
                pl.BlockSpec(memory_space=pl.ANY),
            ],
            out_specs=[
                pl.BlockSpec((TM, D_MODEL), tmap),
                pl.BlockSpec((TM, 2 * D_FOURIER), tmap),
                pl.BlockSpec((TM, 3 * D_NA), tmap),
                pl.BlockSpec((TM, 3 * D_CONV), tmap),
            ],
            scratch_shapes=_gather_scratch(TM),
        ),
        out_shape=[
            jax.ShapeDtypeStruct((T_ALL, D_MODEL), F32),
            jax.ShapeDtypeStruct((T_ALL, 2 * D_FOURIER), BF16),
            jax.ShapeDtypeStruct((T_ALL, 3 * D_NA), BF16),
            jax.ShapeDtypeStruct((T_ALL, 3 * D_CONV), F32),
        ],
        compiler_params=_cparams("arbitrary"),
    )(pos_flat, x1, route, mod4, mod4, norm1.reshape(DEPTH, 1, D_MODEL), w_in_bf, dft_ch, osrt)


def _combine_final(pos_flat, x1, route, mod4, norm_final, osrt, li, n_tok):
    def tmap(i, pos):
        return (i, 0)

    return pl.pallas_call(
        _combine_final_kernel,
        name="moe_combine_final",
        grid_spec=pltpu.PrefetchScalarGridSpec(
            num_scalar_prefetch=1,
            grid=(n_tok // TC,),
            in_specs=[
                pl.BlockSpec((TC, D_MODEL), tmap),
                pl.BlockSpec((TC, LANES), tmap),
                pl.BlockSpec((None, None, 1, 6 * D_MODEL), lambda i, pos: (li, _mod_row(i, TC), 0, 0)),
                pl.BlockSpec((1, D_MODEL), lambda i, pos: (0, 0)),
                pl.BlockSpec(memory_space=pl.ANY),
            ],
            out_specs=pl.BlockSpec((TC, D_MODEL), tmap),
            scratch_shapes=_gather_scratch(TC),
        ),
        out_shape=jax.ShapeDtypeStruct((n_tok, D_MODEL), F32),
        compiler_params=_cparams("arbitrary"),
    )(pos_flat, x1, route, mod4, norm_final.reshape(1, D_MODEL), osrt)


def _moe_block(hx_tm, route, w_gate, w_up, w_down, li, n_tok):
    n_tiles = (2 * n_tok) // TMX + N_EXPERTS
    n_rows = n_tiles * TMX
    pos, cnt, ends = _plan(route, n_tok)
    pos = pos[:, 0:2].reshape(-1)
    cnt = cnt[0, :N_EXPERTS].astype(jnp.int32)
    ends = ends[0, :N_EXPERTS].astype(jnp.int32)
    padded = ((cnt + TMX - 1) // TMX) * TMX
    pad_start = ends - padded + cnt
    tile_start = jnp.arange(n_tiles, dtype=jnp.int32) * TMX
    tile_expert = jnp.minimum(jnp.sum((tile_start[:, None] >= ends[None, :]).astype(jnp.int32), axis=1),
                              N_EXPERTS - 1)
    n_active = (ends[-1:] // TMX).astype(jnp.int32)
    tail = jnp.concatenate([ends[-1:], (n_rows - ends[-1:]) // PAD_CHUNKS[0]])

    xs = _dispatch(pos, pad_start, padded - cnt, tail, hx_tm, n_tok, n_rows)
    return pos, _moe(tile_expert, n_active, padded > 0, xs, w_gate, w_up, w_down, li, n_tiles)


def kernel(x, c, ctx, c_ctx, w_ada, b_ada, norm1, norm2, w_in, w_fourier, w_conv, rpb, w_out, w_rg, b_rg,
           w_re, b_re, w_gate, w_up, w_down, norm_final):
    x_pair, ctx_first = (x.reshape(T_LAT, D_MODEL), ctx.reshape(T_CTX, D_MODEL)), 0
    cc =jnp.concatenate([c, c_ctx[None, :], jnp.zeros((MOD_ROWS - BATCH - 1, D_MODEL), F32)], axis=0)
    mod4 = _modulation(cc, w_ada, b_ada).reshape(DEPTH, MOD_ROWS, 1, 6 * D_MODEL)

    w_in_bf = w_in.astype(BF16)
    w_out_bf = w_out.astype(BF16)
    pad = jnp.zeros((DEPTH, D_MODEL, LANES - ROUTE_COLS), F32)
    w_r = jnp.concatenate([w_rg, w_re, pad], axis=-1)
    w_r_hi = w_r.astype(BF16)
    w_r = jnp.concatenate([w_r_hi, (w_r - w_r_hi.astype(F32)).astype(BF16)], axis=-1)
    b_r = jnp.concatenate([b_rg, b_re, pad[:, 0, :]], axis=-1).reshape(DEPTH, 1, LANES)
    dft_ch = jnp.asarray(_channel_dft(), dtype=F32).astype(BF16)
    ctx_blk = T_LAT // CTX_LEN

    for li in range(DEPTH):
        last = li == DEPTH - 1
        if li == 0:
            yf, qkv, cv = _in_projection(*x_pair, ctx_first, mod4, norm1, w_in_bf, dft_ch, li)
        table = _bias_table(rpb[li].reshape(-1))
        ff = _fourier_mix(yf, w_fourier, li, SEQ, 0)
        cvo = _short_conv(cv, w_conv, li, SEQ, 0)
        at = _attn_latent(qkv, table)
        if last:
            n_tok = T_LAT
            ff_c, cvo_c, at_c = ff, cvo, at
        else:
            n_tok = T_ALL
            ff_c = _fourier_mix(yf, w_fourier, li, CTX_LEN, ctx_blk)
            cvo_c = _short_conv(cv, w_conv, li, CTX_LEN, ctx_blk)
            at_c = _attn_ctx(qkv)
        x1, hx_tm, route = _out_projection(x_pair, ctx_first, (ff, ff_c), (at, at_c), (cvo, cvo_c), mod4, norm2,
                                           w_out_bf, w_r, b_r, li, n_tok)
        pos, osrt = _moe_block(hx_tm, route, w_gate, w_up, w_down, li, n_tok)
        if last:
            out = _combine_final(pos, x1, route, mod4, norm_final, osrt, li, n_tok)
            return out.reshape(BATCH, SEQ, D_MODEL)
        xt, yf, qkv, cv = _combine_inproj(pos, x1, route, mod4, norm1, w_in_bf, dft_ch, osrt, li)
        x_pair, ctx_first = (xt, xt), NT_LAT
```

```python
import functools
import math

import numpy as np
import jax
import jax.numpy as jnp
from jax import lax
from jax.experimental import pallas as pl
from jax.experimental.pallas import tpu as pltpu

F32 = jnp.float32
BF16 = jnp.bfloat16

D_MODEL = 1024
BATCH = 8
SEQ = 2048
DEPTH = 2
GRID_W = 64
ROWS = SEQ // GRID_W
CTX_LEN = 256
T_LAT = BATCH * SEQ
T_CTX = BATCH * CTX_LEN
T_ALL = T_LAT + T_CTX

D_FOURIER = 256
D_FG = 64
HEAD_DIM = 64
NA_HEADS = 8
D_NA = NA_HEADS * HEAD_DIM
D_CONV = 256
D_IN_PROJ = D_FOURIER + 3 * D_NA + 3 * D_CONV
NA_KH = 8
NA_KW = 16
N_GROUPS = 4
EXPERTS_PER_GROUP = 8
N_EXPERTS = 32
D_EXPERT = 512
EPS = 1e-6
NEG_INF = -1e30

LANES = 128
SUBLANES = 8
MOD_ROWS = 16
TM = 512
TMX = 512
TD = 512
TC = 512
VMEM_LIMIT = 56 * 1024 * 1024


def _cparams(*sem):
    return pltpu.CompilerParams(dimension_semantics=sem, vmem_limit_bytes=VMEM_LIMIT)


def _ada_kernel(cc_ref, w_ref, b_ref, o_ref):
    s = cc_ref[...]
    s = s * jax.nn.sigmoid(s)
    acc = jnp.dot(s.astype(BF16), w_ref[0].astype(BF16), preferred_element_type=F32)
    o_ref[0] = acc + b_ref[0]


def _modulation(cc, w_ada, b_ada):
    tn = 1536
    nj = 6 * D_MODEL // tn
    return pl.pallas_call(
        _ada_kernel,
        name="ada_modulation",
        grid=(DEPTH, nj),
        in_specs=[
            pl.BlockSpec((MOD_ROWS, D_MODEL), lambda l, j: (0, 0)),
            pl.BlockSpec((1, D_MODEL, tn), lambda l, j: (l, 0, j)),
            pl.BlockSpec((1, 1, tn), lambda l, j: (l, 0, j)),
        ],
        out_specs=pl.BlockSpec((1, MOD_ROWS, tn), lambda l, j: (l, 0, j)),
        out_shape=jax.ShapeDtypeStruct((DEPTH, MOD_ROWS, 6 * D_MODEL), F32),
        compiler_params=_cparams("arbitrary", "arbitrary"),
    )(cc, w_ada, b_ada.reshape(DEPTH, 1, 6 * D_MODEL))


def _mod_row(i, tile):
    return jnp.minimum((i * tile) // SEQ, BATCH)


NT_LAT = T_LAT // TM


def _lat_spec(cols):
    return pl.BlockSpec((TM, cols), lambda i: (jnp.minimum(i, NT_LAT - 1), 0))


def _ctx_spec(cols, first_block):
    return pl.BlockSpec((TM, cols), lambda i: (jnp.maximum(i - NT_LAT, 0) + first_block, 0))


def _pick_stream(lat_ref, ctx_ref):
    return jnp.where(pl.program_id(0) >= NT_LAT, ctx_ref[...], lat_ref[...])


def _inproj_kernel(xa_ref, xb_ref, mod_ref, n1_ref, w_ref, dft_ref, yf_ref, qkv_ref, cv_ref):
    rows = TM // ROW_SPLIT
    for s in range(ROW_SPLIT):
        sl = pl.ds(s * rows, rows)
        _inproj_math(_pick_stream(xa_ref.at[sl], xb_ref.at[sl]), mod_ref, n1_ref, w_ref, dft_ref,
                     yf_ref.at[sl], qkv_ref.at[sl], cv_ref.at[sl])


def _inproj_math(x, mod_ref, n1_ref, w_ref, dft_ref, yf_ref, qkv_ref, cv_ref):
    ms = jnp.mean(x * x, axis=-1, keepdims=True)
    y = x * lax.rsqrt(ms + EPS) * n1_ref[0]
    sh = mod_ref[:, 0:D_MODEL]
    sc = mod_ref[:, D_MODEL:2 * D_MODEL]
    h = y * (1.0 + sc) + sh
    u = jnp.dot(h.astype(BF16), w_ref[0], preferred_element_type=F32)
    yf = jnp.dot(u[:, 0:D_FOURIER].astype(BF16), dft_ref[...], preferred_element_type=F32)
    yf_ref[...] = yf.astype(BF16)
    q0 = D_FOURIER
    qkv_ref[:, 0:D_NA] = (u[:, q0:q0 + D_NA] * (1.0 / math.sqrt(HEAD_DIM))).astype(BF16)
    qkv_ref[:, D_NA:3 * D_NA] = u[:, q0 + D_NA:q0 + 3 * D_NA].astype(BF16)
    cv_ref[...] = u[:, q0 + 3 * D_NA:D_IN_PROJ]


def _in_projection(x_lat, x_ctx, ctx_first_block, mod4, norm1, w_in_bf, dft_ch, li):
    nt = T_ALL // TM
    return pl.pallas_call(
        _inproj_kernel,
        name="in_projection",
        grid=(nt,),
        in_specs=[
            _lat_spec(D_MODEL),
            _ctx_spec(D_MODEL, ctx_first_block),
            pl.BlockSpec((None, None, 1, 6 * D_MODEL), lambda i: (li, _mod_row(i, TM), 0, 0)),
            pl.BlockSpec((1, 1, D_MODEL), lambda i: (li, 0, 0)),
            pl.BlockSpec((1, D_MODEL, D_IN_PROJ), lambda i: (li, 0, 0)),
            pl.BlockSpec((D_FOURIER, 2 * D_FOURIER), lambda i: (0, 0)),
        ],
        out_specs=[
            pl.BlockSpec((TM, 2 * D_FOURIER), lambda i: (i, 0)),
            pl.BlockSpec((TM, 3 * D_NA), lambda i: (i, 0)),
            pl.BlockSpec((TM, 3 * D_CONV), lambda i: (i, 0)),
        ],
        out_shape=[
            jax.ShapeDtypeStruct((T_ALL, 2 * D_FOURIER), BF16),
            jax.ShapeDtypeStruct((T_ALL, 3 * D_NA), BF16),
            jax.ShapeDtypeStruct((T_ALL, 3 * D_CONV), F32),
        ],
        compiler_params=_cparams("arbitrary"),
    )(x_lat, x_ctx, mod4, norm1.reshape(DEPTH, 1, D_MODEL), w_in_bf, dft_ch)


def _dft_tables(n):
    k = np.arange(n, dtype=np.int64)
    ang = 2.0 * np.pi * ((k[:, None] * k[None, :]) % n).astype(np.float64) / n
    s = 1.0 / math.sqrt(n)
    return np.cos(ang) * s, np.sin(ang) * s


def _channel_dft():
    c, s = _dft_tables(D_FG)
    eye = np.eye(D_FOURIER // D_FG)
    return np.concatenate([np.kron(eye, c), np.kron(eye, s)], axis=1)


def _fourier_kernel(ch_ref, sh_ref, j_ref, y_ref, wf_ref, o_ref, fold_ref, rev_ref, *, n):
    h = n // 2
    nb = h // LANES
    df = D_FOURIER
    scale = 1.0 / math.sqrt(n)
    rev = j_ref[...]
    zero_rows = jnp.zeros((SUBLANES, 2 * df), F32)

    fold_ref[0:SUBLANES, :] = zero_rows
    for b in range(nb):
        blk = y_ref[n - LANES * (b + 1):n - LANES * b, :]
        fold_ref[SUBLANES + LANES * b:SUBLANES + LANES * (b + 1), :] = jnp.dot(rev, blk, preferred_element_type=F32)
    mirrored = fold_ref[SUBLANES - 1:SUBLANES - 1 + h, :]
    y_lo = y_ref[0:h, :].astype(F32)
    yc_even = (y_lo[:, 0:df] + mirrored[:, 0:df]).astype(BF16)
    ys_odd = (y_lo[:, df:2 * df] - mirrored[:, df:2 * df]).astype(BF16)

    parity = lax.broadcasted_iota(jnp.int32, (h, 1), 0) & 1
    sign = jnp.where(parity == 1, -scale, scale)
    y_mid = y_ref[h:h + 1, 0:df].astype(F32)
    p = jnp.dot(ch_ref[...], yc_even, preferred_element_type=F32) + sign * y_mid
    q = jnp.dot(sh_ref[...], ys_odd, preferred_element_type=F32)
    lo = (p - q).astype(BF16)
    w = (p + q).astype(BF16)
    mid = jnp.sum(sign * (y_lo[:, 0:df] + y_ref[h:n, 0:df].astype(F32)), axis=0, keepdims=True)

    for b in range(nb):
        blk = w[h - LANES * (b + 1):h - LANES * b, :]
        rev_ref[SUBLANES + LANES * b:SUBLANES + LANES * (b + 1), :] = jnp.dot(rev, blk, preferred_element_type=F32)
    rev_ref[0:SUBLANES, :] = jnp.broadcast_to(mid, (SUBLANES, df))
    hi = rev_ref[SUBLANES - 1:SUBLANES - 1 + h, :].astype(BF16)
    z = jnp.concatenate([lo, hi], axis=0)
    o_ref[...] = jnp.dot(z, wf_ref[0].astype(BF16), preferred_element_type=F32).astype(BF16)


def _fourier_mix(yf, w_fourier, li, n, first_block):
    h = n // 2
    cn, sn = _dft_tables(n)
    ch = jnp.asarray(cn[:h, :h], dtype=F32).astype(BF16)
    sh = jnp.asarray(sn[:h, :h], dtype=F32).astype(BF16)
    anti = jnp.asarray(np.eye(LANES)[::-1].copy(), dtype=F32).astype(BF16)
    return pl.pallas_call(
        functools.partial(_fourier_kernel, n=n),
        name="fourier_mix",
        grid=(BATCH,),
        in_specs=[
            pl.BlockSpec((h, h), lambda b: (0, 0)),
            pl.BlockSpec((h, h), lambda b: (0, 0)),
            pl.BlockSpec((LANES, LANES), lambda b: (0, 0)),
            pl.BlockSpec((n, 2 * D_FOURIER), lambda b: (first_block + b, 0)),
            pl.BlockSpec((1, D_FOURIER, D_FOURIER), lambda b: (li, 0, 0)),
        ],
        out_specs=pl.BlockSpec((n, D_FOURIER), lambda b: (b, 0)),
        out_shape=jax.ShapeDtypeStruct((BATCH * n, D_FOURIER), BF16),
        scratch_shapes=[pltpu.VMEM((h + SUBLANES, 2 * D_FOURIER), F32), pltpu.VMEM((h + SUBLANES, D_FOURIER), F32)],
        compiler_params=_cparams("arbitrary"),
    )(ch, sh, anti, yf, w_fourier)


def _conv_kernel(cv_ref, wc_ref, o_ref, pad_ref):
    n = cv_ref.shape[0]
    gb = cv_ref[:, 0:D_CONV]
    g = cv_ref[:, D_CONV:2 * D_CONV] * cv_ref[:, 2 * D_CONV:3 * D_CONV]
    zero = jnp.zeros((SUBLANES, D_CONV), F32)
    pad_ref[0:SUBLANES, :] = zero
    pad_ref[n + SUBLANES:n + 2 * SUBLANES, :] = zero
    pad_ref[SUBLANES:n + SUBLANES, :] = g
    prev = pad_ref[SUBLANES - 1:n + SUBLANES - 1, :]
    nxt = pad_ref[SUBLANES + 1:n + SUBLANES + 1, :]
    w = wc_ref[0]
    o_ref[...] = (gb * (w[0:1, :] * prev + w[1:2, :] * g + w[2:3, :] * nxt)).astype(BF16)


def _short_conv(cv, w_conv, li, n, first_block):
    return pl.pallas_call(
        _conv_kernel,
        name="short_conv",
        grid=(BATCH,),
        in_specs=[
            pl.BlockSpec((n, 3 * D_CONV), lambda b: (first_block + b, 0)),
            pl.BlockSpec((1, 3, D_CONV), lambda b: (li, 0, 0)),
        ],
        out_specs=pl.BlockSpec((n, D_CONV), lambda b: (b, 0)),
        out_shape=jax.ShapeDtypeStruct((BATCH * n, D_CONV), BF16),
        scratch_shapes=[pltpu.VMEM((n + 2 * SUBLANES, D_CONV), F32)],
        compiler_params=_cparams("arbitrary"),
    )(cv, w_conv)


RPB_H = 2 * NA_KH - 1
RPB_W = 2 * NA_KW - 1
KEYS_LOC = NA_KH * GRID_W


def _bias_kernel(rpb_ref, o_ref):
    h = pl.program_id(0)
    qi = lax.broadcasted_iota(jnp.int32, (GRID_W, LANES), 0)
    lj = lax.broadcasted_iota(jnp.int32, (GRID_W, LANES), 1)
    kc = lj & (GRID_W - 1)
    hi = lj >= GRID_W
    d = kc - qi + (NA_KW - 1)
    cs = jnp.clip(qi - NA_KW // 2, 0, GRID_W - NA_KW)
    valid = (kc >= cs) & (kc < cs + NA_KW)
    tiles = []
    for a in range(RPB_H - 1):
        acc = jnp.zeros((GRID_W, LANES), F32)
        for b in range(RPB_W):
            va = rpb_ref[(h * RPB_H + a) * RPB_W + b]
            vb = rpb_ref[(h * RPB_H + a + 1) * RPB_W + b]
            acc = jnp.where(d == b, jnp.where(hi, vb, va), acc)
        tiles.append(jnp.where(valid, acc, NEG_INF))
    for cls in range(NA_KH):
        for m in range(NA_KH // 2):
            o_ref[0, cls, :, m * LANES:(m + 1) * LANES] = tiles[2 * m - cls + NA_KH - 1]


def _bias_table(rpb_flat):
    return pl.pallas_call(
        _bias_kernel,
        name="attn_bias_table",
        grid=(NA_HEADS,),
        in_specs=[pl.BlockSpec(memory_space=pltpu.SMEM)],
        out_specs=pl.BlockSpec((1, NA_KH, GRID_W, KEYS_LOC), lambda h: (h // 2, 0, h % 2, 0)),
        out_shape=jax.ShapeDtypeStruct((NA_HEADS // 2, NA_KH, 2 * GRID_W, KEYS_LOC), F32),
        compiler_params=_cparams("arbitrary"),
    )(rpb_flat)


_NT_DIMS = (((1,), (1,)), ((), ()))


def _softmax_pv(parts):
    m = None
    for s, _ in parts:
        ms = jnp.max(s, axis=1, keepdims=True)
        m = ms if m is None else jnp.maximum(m, ms)
    l = None
    o = None
    for s, v in parts:
        p = jnp.exp(s - m)
        ls = jnp.sum(p, axis=1, keepdims=True)
        os_ = jnp.dot(p.astype(BF16), v, preferred_element_type=F32)
        l = ls if l is None else l + ls
        o = os_ if o is None else o + os_
    return o / l


def _attn_latent_kernel(q_ref, k_ref, v_ref, kc_ref, vc_ref, tab_ref, o_ref, ve_ref, vce_ref):
    lane = lax.broadcasted_iota(jnp.int32, (GRID_W, LANES), 1)
    lo = lane < HEAD_DIM
    ve_ref[:, 0:LANES] = v_ref[...]
    ve_ref[:, LANES:2 * LANES] = jnp.ones((SEQ, LANES), BF16)
    vce_ref[:, 0:LANES] = vc_ref[...]
    vce_ref[:, LANES:2 * LANES] = jnp.ones((CTX_LEN, LANES), BF16)
    kc = kc_ref[...]
    vce = vce_ref[...]

    for r in range(ROWS):
        rs = min(max(r - NA_KH // 2, 0), ROWS - NA_KH)
        cls = r - rs
        q = q_ref[r * GRID_W:(r + 1) * GRID_W, :]
        kl = k_ref[rs * GRID_W:rs * GRID_W + KEYS_LOC, :]
        vl = ve_ref[rs * GRID_W:rs * GRID_W + KEYS_LOC, :]
        zero = jnp.zeros_like(q)
        q2 = jnp.concatenate([jnp.where(lo, q, zero), jnp.where(lo, zero, q)], axis=0)
        s1 = lax.dot_general(q2, kl, _NT_DIMS, preferred_element_type=F32) + tab_ref[0, cls]
        s2 = lax.dot_general(q2, kc, _NT_DIMS, preferred_element_type=F32)
        m = jnp.maximum(jnp.max(s1, axis=1, keepdims=True), jnp.max(s2, axis=1, keepdims=True))
        p1 = jnp.exp(s1 - m).astype(BF16)
        p2 = jnp.exp(s2 - m).astype(BF16)
        oe = (jnp.dot(p1, vl, preferred_element_type=F32) + jnp.dot(p2, vce, preferred_element_type=F32))
        o = oe[:, 0:LANES] / oe[:, LANES:2 * LANES]
        o_ref[r * GRID_W:(r + 1) * GRID_W, :] = jnp.where(lo, o[0:GRID_W], o[GRID_W:2 * GRID_W]).astype(BF16)


def _attn_latent(qkv, table):
    npair = NA_HEADS // 2
    cblk = T_LAT // CTX_LEN
    return pl.pallas_call(
        _attn_latent_kernel,
        name="attn_latent",
        grid=(BATCH, npair),
        in_specs=[
            pl.BlockSpec((SEQ, LANES), lambda b, p: (b, p)),
            pl.BlockSpec((SEQ, LANES), lambda b, p: (b, npair + p)),
            pl.BlockSpec((SEQ, LANES), lambda b, p: (b, 2 * npair + p)),
            pl.BlockSpec((CTX_LEN, LANES), lambda b, p: (cblk + b, npair + p)),
            pl.BlockSpec((CTX_LEN, LANES), lambda b, p: (cblk + b, 2 * npair + p)),
            pl.BlockSpec((1, NA_KH, 2 * GRID_W, KEYS_LOC), lambda b, p: (p, 0, 0, 0)),
        ],
        out_specs=pl.BlockSpec((SEQ, LANES), lambda b, p: (b, p)),
        out_shape=jax.ShapeDtypeStruct((T_LAT, D_NA), BF16),
        scratch_shapes=[pltpu.VMEM((SEQ, 2 * LANES), BF16), pltpu.VMEM((CTX_LEN, 2 * LANES), BF16)],
        compiler_params=_cparams("arbitrary", "arbitrary"),
    )(qkv, qkv, qkv, qkv, qkv, table)


def _attn_ctx_kernel(q_ref, k_ref, v_ref, o_ref):
    lane = lax.broadcasted_iota(jnp.int32, (CTX_LEN, LANES), 1)
    lo = lane < HEAD_DIM
    q = q_ref[...]
    k = k_ref[...]
    v = v_ref[...]
    outs = []
    for hh in range(2):
        qm = jnp.where(lo if hh == 0 else jnp.logical_not(lo), q, jnp.zeros_like(q))
        s = lax.dot_general(qm, k, _NT_DIMS, preferred_element_type=F32)
        outs.append(_softmax_pv([(s, v)]))
    o_ref[...] = jnp.where(lo, outs[0], outs[1]).astype(BF16)


def _attn_ctx(qkv):
    npair = NA_HEADS // 2
    cblk = T_LAT // CTX_LEN
    return pl.pallas_call(
        _attn_ctx_kernel,
        name="attn_context",
        grid=(BATCH, npair),
        in_specs=[
            pl.BlockSpec((CTX_LEN, LANES), lambda b, p: (cblk + b, p)),
            pl.BlockSpec((CTX_LEN, LANES), lambda b, p: (cblk + b, npair + p)),
            pl.BlockSpec((CTX_LEN, LANES), lambda b, p: (cblk + b, 2 * npair + p)),
        ],
        out_specs=pl.BlockSpec((CTX_LEN, LANES), lambda b, p: (b, p)),
        out_shape=jax.ShapeDtypeStruct((T_CTX, D_NA), BF16),
        compiler_params=_cparams("arbitrary", "arbitrary"),
    )(qkv, qkv, qkv)


ROUTE_COLS = N_GROUPS + N_EXPERTS


def _to_token_major(ref, val):
    rows = val.shape[0]
    for j in range(D_MODEL // LANES):
        ref[pl.ds(j, rows, stride=SUBLANES), :] = val[:, j * LANES:(j + 1) * LANES]


def _from_token_major(ref, rows, j):
    return ref[pl.ds(j, rows, stride=SUBLANES), :]


OUT_SPLIT = 2


def _outproj_kernel(xa_ref, xb_ref, ffa_ref, ffb_ref, ata_ref, atb_ref, cva_ref, cvb_ref,
                    mod_ref, n2_ref, wo_ref, wr_ref, br_ref, x1_ref, hx_ref, route_ref):
    rows = TM // OUT_SPLIT
    for s in range(OUT_SPLIT):
        sl = pl.ds(s * rows, rows)
        ins = [r.at[sl] for r in (xa_ref, xb_ref, ffa_ref, ffb_ref, ata_ref, atb_ref, cva_ref, cvb_ref)]
        _outproj_rows(*ins, mod_ref, n2_ref, wo_ref, wr_ref, br_ref, x1_ref.at[sl],
                      hx_ref.at[pl.ds(s * rows * SUBLANES, rows * SUBLANES)], route_ref.at[sl])


def _outproj_rows(xa_ref, xb_ref, ffa_ref, ffb_ref, ata_ref, atb_ref, cva_ref, cvb_ref,
                  mod_ref, n2_ref, wo_ref, wr_ref, br_ref, x1_ref, hx_ref, route_ref):
    r1 = D_FOURIER
    r2 = D_FOURIER + D_NA
    mix = (jnp.dot(_pick_stream(ffa_ref, ffb_ref), wo_ref[0, 0:r1, :], preferred_element_type=F32)
           + jnp.dot(_pick_stream(ata_ref, atb_ref), wo_ref[0, r1:r2, :], preferred_element_type=F32)
           + jnp.dot(_pick_stream(cva_ref, cvb_ref), wo_ref[0, r2:D_MODEL, :], preferred_element_type=F32))
    g1 = mod_ref[:, 2 * D_MODEL:3 * D_MODEL]
    x1 = _pick_stream(xa_ref, xb_ref) + g1 * mix
    x1_ref[...] = x1
    ms = jnp.mean(x1 * x1, axis=-1, keepdims=True)
    y = x1 * lax.rsqrt(ms + EPS) * n2_ref[0]
    sh2 = mod_ref[:, 3 * D_MODEL:4 * D_MODEL]
    sc2 = mod_ref[:, 4 * D_MODEL:5 * D_MODEL]
    hx = y * (1.0 + sc2) + sh2
    _to_token_major(hx_ref, hx)

    hx_hi = hx.astype(BF16)
    hx_lo = (hx - hx_hi.astype(F32)).astype(BF16)
    part = (jnp.dot(hx_hi, wr_ref[0], preferred_element_type=F32)
            + jnp.dot(hx_lo, wr_ref[0], preferred_element_type=F32))
    logits = part[:, 0:LANES] + part[:, LANES:2 * LANES] + br_ref[0]
    tm = logits.shape[0]
    lane = lax.broadcasted_iota(jnp.int32, (tm, LANES), 1)
    lane_f = lane.astype(F32)
    group_of_lane = ((lane - N_GROUPS) >> 3).astype(F32)
    big = float(LANES)
    gl = jnp.where(lane < N_GROUPS, logits, -jnp.inf)
    gmax = jnp.max(gl, axis=1, keepdims=True)
    gidx = jnp.min(jnp.where(gl == gmax, lane_f, big), axis=1, keepdims=True)
    g_w = 1.0 / jnp.sum(jnp.exp(gl - gmax), axis=1, keepdims=True)
    in_group = (lane >= N_GROUPS) & (lane < N_GROUPS + N_EXPERTS) & (group_of_lane == gidx)
    es = jnp.where(in_group, logits, -jnp.inf)
    t1 = jnp.max(es, axis=1, keepdims=True)
    i1 = jnp.min(jnp.where(es == t1, lane_f, big), axis=1, keepdims=True)
    es2 = jnp.where(lane_f == i1, -jnp.inf, es)
    t2 = jnp.max(es2, axis=1, keepdims=True)
    i2 = jnp.min(jnp.where(es2 == t2, lane_f, big), axis=1, keepdims=True)
    dlt = jnp.exp(t2 - t1)
    w1 = g_w / (1.0 + dlt)
    w2 = g_w * dlt / (1.0 + dlt)
    e1 = i1 - float(N_GROUPS)
    e2 = i2 - float(N_GROUPS)
    route_ref[...] = jnp.where(lane == 0, e1, jnp.where(lane == 1, e2,
                               jnp.where(lane == 2, w1, jnp.where(lane == 3, w2, 0.0))))


def _out_projection(x_pair, ctx_first_block, ff_pair, at_pair, cv_pair, mod4, norm2, w_out_bf, w_r, b_r, li, n_tok):
    nt = n_tok // TM
    return pl.pallas_call(
        _outproj_kernel,
        name="out_projection",
        grid=(nt,),
        in_specs=[
            _lat_spec(D_MODEL), _ctx_spec(D_MODEL, ctx_first_block),
            _lat_spec(D_FOURIER), _ctx_spec(D_FOURIER, 0),
            _lat_spec(D_NA), _ctx_spec(D_NA, 0),
            _lat_spec(D_CONV), _ctx_spec(D_CONV, 0),
            pl.BlockSpec((None, None, 1, 6 * D_MODEL), lambda i: (li, _mod_row(i, TM), 0, 0)),
            pl.BlockSpec((1, 1, D_MODEL), lambda i: (li, 0, 0)),
            pl.BlockSpec((1, D_MODEL, D_MODEL), lambda i: (li, 0, 0)),
            pl.BlockSpec((1, D_MODEL, 2 * LANES), lambda i: (li, 0, 0)),
            pl.BlockSpec((1, 1, LANES), lambda i: (li, 0, 0)),
        ],
        out_specs=[
            pl.BlockSpec((TM, D_MODEL), lambda i: (i, 0)),
            pl.BlockSpec((TM * SUBLANES, LANES), lambda i: (i, 0)),
            pl.BlockSpec((TM, LANES), lambda i: (i, 0)),
        ],
        out_shape=[
            jax.ShapeDtypeStruct((n_tok, D_MODEL), F32),
            jax.ShapeDtypeStruct((n_tok * SUBLANES, LANES), F32),
            jax.ShapeDtypeStruct((n_tok, LANES), F32),
        ],
        compiler_params=_cparams("arbitrary"),
    )(*x_pair, *ff_pair, *at_pair, *cv_pair, mod4, norm2.reshape(DEPTH, 1, D_MODEL), w_out_bf, w_r, b_r)


def _plan_kernel(route_ref, pos_ref, cnt_ref, ends_ref, carry_ref, offs_ref):
    ph = pl.program_id(0)
    i = pl.program_id(1)
    tm = PLAN_SUB
    lane = lax.broadcasted_iota(jnp.int32, (tm, LANES), 1)

    def one_hots(k):
        r = route_ref[k * tm:(k + 1) * tm, :]
        oh1 = lane == r[:, 0:1].astype(jnp.int32)
        oh2 = lane == r[:, 1:2].astype(jnp.int32)
        return oh1, oh2, jnp.where(oh1 | oh2, 1.0, 0.0)

    @pl.when((ph == 0) & (i == 0))
    def _():
        carry_ref[...] = jnp.zeros_like(carry_ref)

    @pl.when(ph == 0)
    def _():
        total = carry_ref[...]
        for k in range(PLAN_TM // tm):
            total = total + jnp.sum(one_hots(k)[2], axis=0, keepdims=True)
        carry_ref[...] = total

    @pl.when((ph == 1) & (i == 0))
    def _():
        cnt = carry_ref[...]
        cnt_ref[...] = jnp.broadcast_to(cnt, cnt_ref.shape)
        tiles = jnp.ceil(cnt * (1.0 / TMX))
        a = lax.broadcasted_iota(jnp.int32, (LANES, LANES), 0)
        b = lax.broadcasted_iota(jnp.int32, (LANES, LANES), 1)
        upper = jnp.where(a < b, 1.0, 0.0).astype(BF16)
        tiles8 = jnp.broadcast_to(tiles, (SUBLANES, LANES))
        first = jnp.dot(tiles8.astype(BF16), upper, preferred_element_type=F32)
        offs_ref[...] = first[0:1, :] * TMX
        ends_ref[...] = (first + tiles8) * TMX
        carry_ref[...] = jnp.zeros_like(carry_ref)

    @pl.when(ph == 1)
    def _():
        row = lax.broadcasted_iota(jnp.int32, (tm, tm), 0)
        col = lax.broadcasted_iota(jnp.int32, (tm, tm), 1)
        tri = jnp.where(row > col, 1.0, 0.0).astype(BF16)
        base = carry_ref[...] + offs_ref[...]
        for k in range(PLAN_TM // tm):
            oh1, oh2, oh = one_hots(k)
            cum = jnp.dot(tri, oh.astype(BF16), preferred_element_type=F32) + base
            p1 = jnp.sum(jnp.where(oh1, cum, 0.0), axis=1, keepdims=True)
            p2 = jnp.sum(jnp.where(oh2, cum, 0.0), axis=1, keepdims=True)
            pos_ref[k * tm:(k + 1) * tm, :] = (
                jnp.where(lane == 0, p1, jnp.where(lane == 1, p2, 0.0)).astype(jnp.int32))
            base = base + jnp.sum(oh, axis=0, keepdims=True)
        carry_ref[...] = base - offs_ref[...]


PLAN_TM = 2048
PLAN_SUB = 512


def _plan(route, n_tok):
    nt = n_tok // PLAN_TM
    return pl.pallas_call(
        _plan_kernel,
        name="moe_plan",
        grid=(2, nt),
        in_specs=[pl.BlockSpec((PLAN_TM, LANES), lambda ph, i: (i, 0))],
        out_specs=[
            pl.BlockSpec((PLAN_TM, LANES), lambda ph, i: (i * ph, 0)),
            pl.BlockSpec((SUBLANES, LANES), lambda ph, i: (0, 0)),
            pl.BlockSpec((SUBLANES, LANES), lambda ph, i: (0, 0)),
        ],
        out_shape=[
            jax.ShapeDtypeStruct((n_tok, LANES), jnp.int32),
            jax.ShapeDtypeStruct((SUBLANES, LANES), F32),
            jax.ShapeDtypeStruct((SUBLANES, LANES), F32),
        ],
        scratch_shapes=[pltpu.VMEM((1, LANES), F32), pltpu.VMEM((1, LANES), F32)],
        compiler_params=_cparams("arbitrary", "arbitrary"),
    )(route)


def _row_tile(ref, row):
    return ref.at[pl.ds(pl.multiple_of(row * SUBLANES, SUBLANES), SUBLANES), :]


DISPATCH_UNROLL = 8
PAD_CHUNKS = tuple(1 << b for b in reversed(range(TMX.bit_length() - 1)))


def _dispatch_kernel(pos_ref, pstart_ref, plen_ref, tail_ref, hx_ref, xs_ref, zbuf, sem, zsem):
    i = pl.program_id(0)

    def pad_copies(fn):
        def ebody(e, carry):
            n = plen_ref[e]
            off = pstart_ref[e]
            for rows in PAD_CHUNKS:
                @pl.when((n & rows) != 0)
                def _():
                    fn(pltpu.make_async_copy(
                        zbuf.at[pl.ds(0, rows * SUBLANES), :],
                        xs_ref.at[pl.ds(pl.multiple_of(off * SUBLANES, SUBLANES), rows * SUBLANES), :], zsem.at[0]))
                off = off + (n & rows)
            return carry

        lax.fori_loop(0, N_EXPERTS, ebody, 0)

        def tbody(k, carry):
            row = pl.multiple_of((tail_ref[0] + k * PAD_CHUNKS[0]) * SUBLANES, SUBLANES)
            fn(pltpu.make_async_copy(zbuf, xs_ref.at[pl.ds(row, PAD_CHUNKS[0] * SUBLANES), :], zsem.at[0]))
            return carry

        lax.fori_loop(0, tail_ref[1], tbody, 0)

    @pl.when(i == 0)
    def _():
        zbuf[...] = jnp.zeros_like(zbuf)
        pad_copies(lambda cp: cp.start())

    def body(k, carry):
        for u in range(DISPATCH_UNROLL):
            t = k * DISPATCH_UNROLL + u
            src = _row_tile(hx_ref, t)
            for j in range(2):
                pltpu.make_async_copy(src, _row_tile(xs_ref, pos_ref[(i * TD + t) * 2 + j]),
                                      sem.at[0]).start(priority=j)
        return carry

    lax.fori_loop(0, TD // DISPATCH_UNROLL, body, 0)
    nrow = TD * SUBLANES
    for j in range(2):
        pltpu.make_async_copy(hx_ref, xs_ref.at[pl.ds(0, nrow), :], sem.at[0]).wait()

    @pl.when(i == 0)
    def _():
        pad_copies(lambda cp: cp.wait())


def _dispatch(pos_flat, pad_start, pad_len, tail, hx_tm, n_tok, n_rows):
    return pl.pallas_call(
        _dispatch_kernel,
        name="moe_dispatch",
        grid_spec=pltpu.PrefetchScalarGridSpec(
            num_scalar_prefetch=4,
            grid=(n_tok // TD,),
            in_specs=[pl.BlockSpec((TD * SUBLANES, LANES), lambda i, pos, ps, pn, tl: (i, 0))],
            out_specs=pl.BlockSpec(memory_space=pl.ANY),
            scratch_shapes=[pltpu.VMEM((PAD_CHUNKS[0] * SUBLANES, LANES), F32),
                            pltpu.SemaphoreType.DMA((1,)), pltpu.SemaphoreType.DMA((1,))],
        ),
        out_shape=jax.ShapeDtypeStruct((n_rows * SUBLANES, LANES), F32),
        compiler_params=_cparams("arbitrary"),
    )(pos_flat, pad_start, pad_len, tail, hx_tm)


XS_SLOTS = 3


def _moe_kernel(te_ref, na_ref, first_ref, wslot_ref, nxt_ref, xs_hbm, wg_hbm, wu_hbm, wd_hbm, os_ref,
                xbuf, wg_f, wu_f, wd_f, wg_s, wu_s, wd_s, xsem, wsem, *, li):
    i = pl.program_id(0)
    n_act = na_ref[0]
    active = i < n_act

    def xs_copy(tile, slot):
        rows = pl.ds(pl.multiple_of(tile * (TMX * SUBLANES), SUBLANES), TMX * SUBLANES)
        return pltpu.make_async_copy(xs_hbm.at[rows, :], xbuf.at[slot], xsem.at[slot])

    @pl.when(i == 0)
    def _():
        xs_copy(0, 0).start()

        @pl.when(n_act > 1)
        def _():
            xs_copy(1, 1).start()

    @pl.when(i + 2 < n_act)
    def _():
        xs_copy(i + 2, (i + 2) % XS_SLOTS).start()

    def weight_copies(e, slot):
        return [pltpu.make_async_copy(wg_hbm.at[li, e], wg_f.at[slot], wsem.at[slot, 0]),
                pltpu.make_async_copy(wu_hbm.at[li, e], wu_f.at[slot], wsem.at[slot, 1]),
                pltpu.make_async_copy(wd_hbm.at[li, e], wd_f.at[slot], wsem.at[slot, 2])]

    @pl.when(active & (first_ref[i] == 1))
    def _():
        slot = wslot_ref[i]

        @pl.when(i == 0)
        def _():
            for cp in weight_copies(te_ref[0], 0):
                cp.start()

        @pl.when(nxt_ref[i] >= 0)
        def _():
            for cp in weight_copies(nxt_ref[i], 1 - slot):
                cp.start()

        for cp in weight_copies(te_ref[i], slot):
            cp.wait()
        wg_s[...] = wg_f[slot].astype(BF16)
        wu_s[...] = wu_f[slot].astype(BF16)
        wd_s[...] = wd_f[slot].astype(BF16)

    @pl.when(active)
    def _():
        xs_copy(i, i % XS_SLOTS).wait()
        xs_ref = xbuf.at[i % XS_SLOTS]
        x = jnp.concatenate([_from_token_major(xs_ref, TMX, j).astype(BF16)
                             for j in range(D_MODEL // LANES)], axis=-1)
        g = jnp.dot(x, wg_s[...], preferred_element_type=F32)
        u = jnp.dot(x, wu_s[...], preferred_element_type=F32)
        h = (g * jax.nn.sigmoid(g) * u).astype(BF16)
        _to_token_major(os_ref, jnp.dot(h, wd_s[...], preferred_element_type=F32))

    @pl.when(jnp.logical_not(active))
    def _():
        os_ref[...] = jnp.zeros_like(os_ref)


def _moe(tile_expert, n_active, has_rows, xs, w_gate, w_up, w_down, li, n_tiles):
    eid = jnp.arange(N_EXPERTS, dtype=jnp.int32)
    later = jnp.where(has_rows[None, :] & (eid[None, :] > eid[:, None]), eid[None, :], N_EXPERTS)
    next_of_expert = jnp.min(later, axis=1)
    next_of_expert = jnp.where(next_of_expert == N_EXPERTS, -1, next_of_expert)
    first = jnp.concatenate([jnp.ones((1,), jnp.int32),
                             (tile_expert[1:] != tile_expert[:-1]).astype(jnp.int32)])
    wslot = (jnp.cumsum(first) - 1) % 2
    nxt = jnp.sum(jnp.where(tile_expert[:, None] == eid[None, :], next_of_expert[None, :], 0), axis=1)


    return pl.pallas_call(
        functools.partial(_moe_kernel, li=li),
        name="moe_experts",
        grid_spec=pltpu.PrefetchScalarGridSpec(
            num_scalar_prefetch=5,
            grid=(n_tiles,),
            in_specs=[
                pl.BlockSpec(memory_space=pl.ANY),
                pl.BlockSpec(memory_space=pl.ANY),
                pl.BlockSpec(memory_space=pl.ANY),
                pl.BlockSpec(memory_space=pl.ANY),
            ],
            out_specs=pl.BlockSpec((TMX * SUBLANES, LANES), lambda i, te, na, fi, ws, nx: (i, 0)),
            scratch_shapes=[
                pltpu.VMEM((XS_SLOTS, TMX * SUBLANES, LANES), F32),
                pltpu.VMEM((2, D_MODEL, D_EXPERT), F32),
                pltpu.VMEM((2, D_MODEL, D_EXPERT), F32),
                pltpu.VMEM((2, D_EXPERT, D_MODEL), F32),
                pltpu.VMEM((D_MODEL, D_EXPERT), BF16),
                pltpu.VMEM((D_MODEL, D_EXPERT), BF16),
                pltpu.VMEM((D_EXPERT, D_MODEL), BF16),
                pltpu.SemaphoreType.DMA((XS_SLOTS,)),
                pltpu.SemaphoreType.DMA((2, 3)),
            ],
        ),
        out_shape=jax.ShapeDtypeStruct((n_tiles * TMX * SUBLANES, LANES), F32),
        compiler_params=_cparams("arbitrary"),
    )(tile_expert, n_active, first, wslot.astype(jnp.int32), nxt.astype(jnp.int32), xs, w_gate, w_up, w_down)


GATHER_UNROLL = 16


def _start_row_gathers(pos_ref, os_ref, buf, sem, tile, slot, n):
    def body(k, carry):
        for u in range(GATHER_UNROLL):
            t = k * GATHER_UNROLL + u
            for j in range(2):
                pltpu.make_async_copy(_row_tile(os_ref, pos_ref[(tile * n + t) * 2 + j]),
                                      _row_tile(buf.at[slot, j], t), sem.at[slot, j]).start(priority=j)
        return carry

    lax.fori_loop(0, n // GATHER_UNROLL, body, 0)


def _gather_pipeline(pos_ref, os_ref, buf_even, buf_odd, sem, n, compute):
    i = pl.program_id(0)
    last = pl.num_programs(0) - 1
    bufs = (buf_even, buf_odd)

    def wait_rows(k):
        for j in range(2):
            pltpu.make_async_copy(os_ref.at[pl.ds(0, n * SUBLANES), :], bufs[k].at[j], sem.at[k, j]).wait()

    @pl.when(i == 0)
    def _():
        def body(t, carry):
            for j in range(2):
                pltpu.make_async_copy(_row_tile(os_ref, pos_ref[t * 2 + j]), _row_tile(buf_even.at[j], t),
                                      sem.at[0, j]).start(priority=j)
            return carry

        lax.fori_loop(0, n, body, 0)

    def step(cur, nxt):
        wait_rows(cur)
        base = jnp.minimum(i + 1, last) * n
        for t in range(n):
            for j in range(2):
                pltpu.make_async_copy(_row_tile(os_ref, pos_ref[(base + t) * 2 + j]),
                                      bufs[nxt].at[j, pl.ds(t * SUBLANES, SUBLANES), :],
                                      sem.at[nxt, j]).start(priority=j)
        compute(bufs[cur])

        @pl.when(i == last)
        def _():
            wait_rows(nxt)

    @pl.when(i % 2 == 0)
    def _():
        step(0, 1)

    @pl.when(i % 2 == 1)
    def _():
        step(1, 0)


def _combined_rows(x1_ref, route_ref, mod_ref, buf, r0, rows):
    sl = pl.ds(r0, rows)
    w1 = route_ref[sl, 2:3]
    w2 = route_ref[sl, 3:4]
    tm_rows = pl.ds(r0 * SUBLANES, rows * SUBLANES)
    cols = []
    for c in range(D_MODEL // LANES):
        y = (w1 * _from_token_major(buf.at[0, tm_rows], rows, c)
             + w2 * _from_token_major(buf.at[1, tm_rows], rows, c))
        g2 = mod_ref[:, 5 * D_MODEL + c * LANES:5 * D_MODEL + (c + 1) * LANES]
        cols.append(x1_ref[sl, c * LANES:(c + 1) * LANES] + g2 * y)
    return jnp.concatenate(cols, axis=-1)


ROW_SPLIT = 2


def _combine_final_kernel(pos_ref, x1_ref, route_ref, mod_ref, nf_ref, os_ref, o_ref, buf_even, buf_odd, sem):
    def compute(buf):
        x2 = _combined_rows(x1_ref, route_ref, mod_ref, buf, 0, TC)
        ms = jnp.mean(x2 * x2, axis=-1, keepdims=True)
        o_ref[...] = x2 * lax.rsqrt(ms + EPS) * nf_ref[...]

    _gather_pipeline(pos_ref, os_ref, buf_even, buf_odd, sem, TC, compute)


def _gather_scratch(n):
    return [pltpu.VMEM((2, n * SUBLANES, LANES), F32), pltpu.VMEM((2, n * SUBLANES, LANES), F32),
            pltpu.SemaphoreType.DMA((2, 2))]


def _combine_inproj_kernel(pos_ref, x1_ref, route_ref, mod_ref, modn_ref, n1_ref, w_ref, dft_ref, os_ref,
                           x2_ref, yf_ref, qkv_ref, cv_ref, buf_even, buf_odd, sem):
    def compute(buf):
        rows = TM // ROW_SPLIT
        for s in range(ROW_SPLIT):
            sl = pl.ds(s * rows, rows)
            x2 = _combined_rows(x1_ref, route_ref, mod_ref, buf, s * rows, rows)
            x2_ref[sl, :] = x2
            _inproj_math(x2, modn_ref, n1_ref, w_ref, dft_ref, yf_ref.at[sl], qkv_ref.at[sl], cv_ref.at[sl])

    _gather_pipeline(pos_ref, os_ref, buf_even, buf_odd, sem, TM, compute)


def _combine_inproj(pos_flat, x1, route, mod4, norm1, w_in_bf, dft_ch, osrt, li):
    def tmap(i, pos):
        return (i, 0)

    return pl.pallas_call(
        _combine_inproj_kernel,
        name="moe_combine_in_projection",
        grid_spec=pltpu.PrefetchScalarGridSpec(
            num_scalar_prefetch=1,
            grid=(T_ALL // TM,),
            in_specs=[
                pl.BlockSpec((TM, D_MODEL), tmap),
                pl.BlockSpec((TM, LANES), tmap),
                pl.BlockSpec((None, None, 1, 6 * D_MODEL), lambda i, pos: (li, _mod_row(i, TM), 0, 0)),
                pl.BlockSpec((None, None, 1, 6 * D_MODEL), lambda i, pos: (li + 1, _mod_row(i, TM), 0, 0)),
                pl.BlockSpec((1, 1, D_MODEL), lambda i, pos: (li + 1, 0, 0)),
                pl.BlockSpec((1, D_MODEL, D_IN_PROJ), lambda i, pos: (li + 1, 0, 0)),
                pl.BlockSpec((D_FOURIER, 2 * D_FOURIER), lambda i, pos: (0, 0)),
                pl.BlockSpec(memory_space=pl.ANY),
            ],
            out_specs=[
                pl.BlockSpec((TM, D_MODEL), tmap),
                pl.BlockSpec((TM, 2 * D_FOURIER), tmap),
                pl.BlockSpec((TM, 3 * D_NA), tmap),
                pl.BlockSpec((TM, 3 * D_CONV), tmap),
            ],
            scratch_shapes=_gather_scratch(TM),
        ),
        out_shape=[
            jax.ShapeDtypeStruct((T_ALL, D_MODEL), F32),
            jax.ShapeDtypeStruct((T_ALL, 2 * D_FOURIER), BF16),
            jax.ShapeDtypeStruct((T_ALL, 3 * D_NA), BF16),
            jax.ShapeDtypeStruct((T_ALL, 3 * D_CONV), F32),
        ],
        compiler_params=_cparams("arbitrary"),
    )(pos_flat, x1, route, mod4, mod4, norm1.reshape(DEPTH, 1, D_MODEL), w_in_bf, dft_ch, osrt)


def _combine_final(pos_flat, x1, route, mod4, norm_final, osrt, li, n_tok):
    def tmap(i, pos):
        return (i, 0)

    return pl.pallas_call(
        _combine_final_kernel,
        name="moe_combine_final",
        grid_spec=pltpu.PrefetchScalarGridSpec(
            num_scalar_prefetch=1,
            grid=(n_tok // TC,),
            in_specs=[
                pl.BlockSpec((TC, D_MODEL), tmap),
                pl.BlockSpec((TC, LANES), tmap),
                pl.BlockSpec((None, None, 1, 6 * D_MODEL), lambda i, pos: (li, _mod_row(i, TC), 0, 0)),
                pl.BlockSpec((1, D_MODEL), lambda i, pos: (0, 0)),
                pl.BlockSpec(memory_space=pl.ANY),
            ],
            out_specs=pl.BlockSpec((TC, D_MODEL), tmap),
            scratch_shapes=_gather_scratch(TC),
        ),
        out_shape=jax.ShapeDtypeStruct((n_tok, D_MODEL), F32),
        compiler_params=_cparams("arbitrary"),
    )(pos_flat, x1, route, mod4, norm_final.reshape(1, D_MODEL), osrt)


def _moe_block(hx_tm, route, w_gate, w_up, w_down, li, n_tok):
    n_tiles = (2 * n_tok) // TMX + N_EXPERTS
    n_rows = n_tiles * TMX
    pos, cnt, ends = _plan(route, n_tok)
    pos = pos[:, 0:2].reshape(-1)
    cnt = cnt[0, :N_EXPERTS].astype(jnp.int32)
    ends = ends[0, :N_EXPERTS].astype(jnp.int32)
    padded = ((cnt + TMX - 1) // TMX) * TMX
    pad_start = ends - padded + cnt
    tile_start = jnp.arange(n_tiles, dtype=jnp.int32) * TMX
    tile_expert = jnp.minimum(jnp.sum((tile_start[:, None] >= ends[None, :]).astype(jnp.int32), axis=1),
                              N_EXPERTS - 1)
    n_active = (ends[-1:] // TMX).astype(jnp.int32)
    tail = jnp.concatenate([ends[-1:], (n_rows - ends[-1:]) // PAD_CHUNKS[0]])

    xs = _dispatch(pos, pad_start, padded - cnt, tail, hx_tm, n_tok, n_rows)
    return pos, _moe(tile_expert, n_active, padded > 0, xs, w_gate, w_up, w_down, li, n_tiles)


def kernel(x, c, ctx, c_ctx, w_ada, b_ada, norm1, norm2, w_in, w_fourier, w_conv, rpb, w_out, w_rg, b_rg,
           w_re, b_re, w_gate, w_up, w_down, norm_final):
    x_pair, ctx_first = (x.reshape(T_LAT, D_MODEL), ctx.reshape(T_CTX, D_MODEL)), 0
    cc =jnp.concatenate([c, c_ctx[None, :], jnp.zeros((MOD_ROWS - BATCH - 1, D_MODEL), F32)], axis=0)
    mod4 = _modulation(cc, w_ada, b_ada).reshape(DEPTH, MOD_ROWS, 1, 6 * D_MODEL)

    w_in_bf = w_in.astype(BF16)
    w_out_bf = w_out.astype(BF16)
    pad = jnp.zeros((DEPTH, D_MODEL, LANES - ROUTE_COLS), F32)
    w_r = jnp.concatenate([w_rg, w_re, pad], axis=-1)
    w_r_hi = w_r.astype(BF16)
    w_r = jnp.concatenate([w_r_hi, (w_r - w_r_hi.astype(F32)).astype(BF16)], axis=-1)
    b_r = jnp.concatenate([b_rg, b_re, pad[:, 0, :]], axis=-1).reshape(DEPTH, 1, LANES)
    dft_ch = jnp.asarray(_channel_dft(), dtype=F32).astype(BF16)
    ctx_blk = T_LAT // CTX_LEN

    for li in range(DEPTH):
        last = li == DEPTH - 1
        if li == 0:
            yf, qkv, cv = _in_projection(*x_pair, ctx_first, mod4, norm1, w_in_bf, dft_ch, li)
        table = _bias_table(rpb[li].reshape(-1))
        ff = _fourier_mix(yf, w_fourier, li, SEQ, 0)
        cvo = _short_conv(cv, w_conv, li, SEQ, 0)
        at = _attn_latent(qkv, table)
        if last:
            n_tok = T_LAT
            ff_c, cvo_c, at_c = ff, cvo, at
        else:
            n_tok = T_ALL
            ff_c = _fourier_mix(yf, w_fourier, li, CTX_LEN, ctx_blk)
            cvo_c = _short_conv(cv, w_conv, li, CTX_LEN, ctx_blk)
            at_c = _attn_ctx(qkv)
        x1, hx_tm, route = _out_projection(x_pair, ctx_first, (ff, ff_c), (at, at_c), (cvo, cvo_c), mod4, norm2,
                                           w_out_bf, w_r, b_r, li, n_tok)
        pos, osrt = _moe_block(hx_tm, route, w_gate, w_up, w_down, li, n_tok)
        if last:
            out = _combine_final(pos, x1, route, mod4, norm_final, osrt, li, n_tok)
            return out.reshape(BATCH, SEQ, D_MODEL)
        xt, yf, qkv, cv = _combine_inproj(pos, x1, route, mod4, norm1, w_in_bf, dft_ch, osrt, li)
        x_pair, ctx_first = (xt, xt), NT_LAT
```

```python
import functools
import math

import numpy as np
import jax
import jax.numpy as jnp
from jax import lax
from jax.experimental import pallas as pl
from jax.experimental.pallas import tpu as pltpu

F32 = jnp.float32
BF16 = jnp.bfloat16

D_MODEL = 1024
BATCH = 8
SEQ = 2048
DEPTH = 2
GRID_W = 64
ROWS = SEQ // GRID_W
CTX_LEN = 256
T_LAT = BATCH * SEQ
T_CTX = BATCH * CTX_LEN
T_ALL = T_LAT + T_CTX

D_FOURIER = 256
D_FG = 64
HEAD_DIM = 64
NA_HEADS = 8
D_NA = NA_HEADS * HEAD_DIM
D_CONV = 256
D_IN_PROJ = D_FOURIER + 3 * D_NA + 3 * D_CONV
NA_KH = 8
NA_KW = 16
N_GROUPS = 4
EXPERTS_PER_GROUP = 8
N_EXPERTS = 32
D_EXPERT = 512
EPS = 1e-6
NEG_INF = -1e30

LANES = 128
SUBLANES = 8
MOD_ROWS = 16
TM = 512
TMX = 512
TD = 512
TC = 512
VMEM_LIMIT = 56 * 1024 * 1024


def _cparams(*sem):
    return pltpu.CompilerParams(dimension_semantics=sem, vmem_limit_bytes=VMEM_LIMIT)


def _ada_kernel(cc_ref, w_ref, b_ref, o_ref):
    s = cc_ref[...]
    s = s * jax.nn.sigmoid(s)
    acc = jnp.dot(s.astype(BF16), w_ref[0].astype(BF16), preferred_element_type=F32)
    o_ref[0] = acc + b_ref[0]


def _modulation(cc, w_ada, b_ada):
    tn = 1536
    nj = 6 * D_MODEL // tn
    return pl.pallas_call(
        _ada_kernel,
        name="ada_modulation",
        grid=(DEPTH, nj),
        in_specs=[
            pl.BlockSpec((MOD_ROWS, D_MODEL), lambda l, j: (0, 0)),
            pl.BlockSpec((1, D_MODEL, tn), lambda l, j: (l, 0, j)),
            pl.BlockSpec((1, 1, tn), lambda l, j: (l, 0, j)),
        ],
        out_specs=pl.BlockSpec((1, MOD_ROWS, tn), lambda l, j: (l, 0, j)),
        out_shape=jax.ShapeDtypeStruct((DEPTH, MOD_ROWS, 6 * D_MODEL), F32),
        compiler_params=_cparams("arbitrary", "arbitrary"),
    )(cc, w_ada, b_ada.reshape(DEPTH, 1, 6 * D_MODEL))


def _mod_row(i, tile):
    return jnp.minimum((i * tile) // SEQ, BATCH)


NT_LAT = T_LAT // TM


def _lat_spec(cols):
    return pl.BlockSpec((TM, cols), lambda i: (jnp.minimum(i, NT_LAT - 1), 0))


def _ctx_spec(cols, first_block):
    return pl.BlockSpec((TM, cols), lambda i: (jnp.maximum(i - NT_LAT, 0) + first_block, 0))


def _pick_stream(lat_ref, ctx_ref):
    return jnp.where(pl.program_id(0) >= NT_LAT, ctx_ref[...], lat_ref[...])


def _inproj_kernel(xa_ref, xb_ref, mod_ref, n1_ref, w_ref, dft_ref, yf_ref, qkv_ref, cv_ref):
    rows = TM // ROW_SPLIT
    for s in range(ROW_SPLIT):
        sl = pl.ds(s * rows, rows)
        _inproj_math(_pick_stream(xa_ref.at[sl], xb_ref.at[sl]), mod_ref, n1_ref, w_ref, dft_ref,
                     yf_ref.at[sl], qkv_ref.at[sl], cv_ref.at[sl])


def _inproj_math(x, mod_ref, n1_ref, w_ref, dft_ref, yf_ref, qkv_ref, cv_ref):
    ms = jnp.mean(x * x, axis=-1, keepdims=True)
    y = x * lax.rsqrt(ms + EPS) * n1_ref[0]
    sh = mod_ref[:, 0:D_MODEL]
    sc = mod_ref[:, D_MODEL:2 * D_MODEL]
    h = y * (1.0 + sc) + sh
    u = jnp.dot(h.astype(BF16), w_ref[0], preferred_element_type=F32)
    yf = jnp.dot(u[:, 0:D_FOURIER].astype(BF16), dft_ref[...], preferred_element_type=F32)
    yf_ref[...] = yf.astype(BF16)
    q0 = D_FOURIER
    qkv_ref[:, 0:D_NA] = (u[:, q0:q0 + D_NA] * (1.0 / math.sqrt(HEAD_DIM))).astype(BF16)
    qkv_ref[:, D_NA:3 * D_NA] = u[:, q0 + D_NA:q0 + 3 * D_NA].astype(BF16)
    cv_ref[...] = u[:, q0 + 3 * D_NA:D_IN_PROJ]


def _in_projection(x_lat, x_ctx, ctx_first_block, mod4, norm1, w_in_bf, dft_ch, li):
    nt = T_ALL // TM
    return pl.pallas_call(
        _inproj_kernel,
        name="in_projection",
        grid=(nt,),
        in_specs=[
            _lat_spec(D_MODEL),
            _ctx_spec(D_MODEL, ctx_first_block),
            pl.BlockSpec((None, None, 1, 6 * D_MODEL), lambda i: (li, _mod_row(i, TM), 0, 0)),
            pl.BlockSpec((1, 1, D_MODEL), lambda i: (li, 0, 0)),
            pl.BlockSpec((1, D_MODEL, D_IN_PROJ), lambda i: (li, 0, 0)),
            pl.BlockSpec((D_FOURIER, 2 * D_FOURIER), lambda i: (0, 0)),
        ],
        out_specs=[
            pl.BlockSpec((TM, 2 * D_FOURIER), lambda i: (i, 0)),
            pl.BlockSpec((TM, 3 * D_NA), lambda i: (i, 0)),
            pl.BlockSpec((TM, 3 * D_CONV), lambda i: (i, 0)),
        ],
        out_shape=[
            jax.ShapeDtypeStruct((T_ALL, 2 * D_FOURIER), BF16),
            jax.ShapeDtypeStruct((T_ALL, 3 * D_NA), BF16),
            jax.ShapeDtypeStruct((T_ALL, 3 * D_CONV), F32),
        ],
        compiler_params=_cparams("arbitrary"),
    )(x_lat, x_ctx, mod4, norm1.reshape(DEPTH, 1, D_MODEL), w_in_bf, dft_ch)


def _dft_tables(n):
    k = np.arange(n, dtype=np.int64)
    ang = 2.0 * np.pi * ((k[:, None] * k[None, :]) % n).astype(np.float64) / n
    s = 1.0 / math.sqrt(n)
    return np.cos(ang) * s, np.sin(ang) * s


def _channel_dft():
    c, s = _dft_tables(D_FG)
    eye = np.eye(D_FOURIER // D_FG)
    return np.concatenate([np.kron(eye, c), np.kron(eye, s)], axis=1)


def _fourier_kernel(ch_ref, sh_ref, j_ref, y_ref, wf_ref, o_ref, fold_ref, rev_ref, *, n):
    h = n // 2
    nb = h // LANES
    df = D_FOURIER
    scale = 1.0 / math.sqrt(n)
    rev = j_ref[...]
    zero_rows = jnp.zeros((SUBLANES, 2 * df), F32)

    fold_ref[0:SUBLANES, :] = zero_rows
    for b in range(nb):
        blk = y_ref[n - LANES * (b + 1):n - LANES * b, :]
        fold_ref[SUBLANES + LANES * b:SUBLANES + LANES * (b + 1), :] = jnp.dot(rev, blk, preferred_element_type=F32)
    mirrored = fold_ref[SUBLANES - 1:SUBLANES - 1 + h, :]
    y_lo = y_ref[0:h, :].astype(F32)
    yc_even = (y_lo[:, 0:df] + mirrored[:, 0:df]).astype(BF16)
    ys_odd = (y_lo[:, df:2 * df] - mirrored[:, df:2 * df]).astype(BF16)

    parity = lax.broadcasted_iota(jnp.int32, (h, 1), 0) & 1
    sign = jnp.where(parity == 1, -scale, scale)
    y_mid = y_ref[h:h + 1, 0:df].astype(F32)
    p = jnp.dot(ch_ref[...], yc_even, preferred_element_type=F32) + sign * y_mid
    q = jnp.dot(sh_ref[...], ys_odd, preferred_element_type=F32)
    lo = (p - q).astype(BF16)
    w = (p + q).astype(BF16)
    mid = jnp.sum(sign * (y_lo[:, 0:df] + y_ref[h:n, 0:df].astype(F32)), axis=0, keepdims=True)

    for b in range(nb):
        blk = w[h - LANES * (b + 1):h - LANES * b, :]
        rev_ref[SUBLANES + LANES * b:SUBLANES + LANES * (b + 1), :] = jnp.dot(rev, blk, preferred_element_type=F32)
    rev_ref[0:SUBLANES, :] = jnp.broadcast_to(mid, (SUBLANES, df))
    hi = rev_ref[SUBLANES - 1:SUBLANES - 1 + h, :].astype(BF16)
    z = jnp.concatenate([lo, hi], axis=0)
    o_ref[...] = jnp.dot(z, wf_ref[0].astype(BF16), preferred_element_type=F32).astype(BF16)


def _fourier_mix(yf, w_fourier, li, n, first_block):
    h = n // 2
    cn, sn = _dft_tables(n)
    ch = jnp.asarray(cn[:h, :h], dtype=F32).astype(BF16)
    sh = jnp.asarray(sn[:h, :h], dtype=F32).astype(BF16)
    anti = jnp.asarray(np.eye(LANES)[::-1].copy(), dtype=F32).astype(BF16)
    return pl.pallas_call(
        functools.partial(_fourier_kernel, n=n),
        name="fourier_mix",
        grid=(BATCH,),
        in_specs=[
            pl.BlockSpec((h, h), lambda b: (0, 0)),
            pl.BlockSpec((h, h), lambda b: (0, 0)),
            pl.BlockSpec((LANES, LANES), lambda b: (0, 0)),
            pl.BlockSpec((n, 2 * D_FOURIER), lambda b: (first_block + b, 0)),
            pl.BlockSpec((1, D_FOURIER, D_FOURIER), lambda b: (li, 0, 0)),
        ],
        out_specs=pl.BlockSpec((n, D_FOURIER), lambda b: (b, 0)),
        out_shape=jax.ShapeDtypeStruct((BATCH * n, D_FOURIER), BF16),
        scratch_shapes=[pltpu.VMEM((h + SUBLANES, 2 * D_FOURIER), F32), pltpu.VMEM((h + SUBLANES, D_FOURIER), F32)],
        compiler_params=_cparams("arbitrary"),
    )(ch, sh, anti, yf, w_fourier)


def _conv_kernel(cv_ref, wc_ref, o_ref, pad_ref):
    n = cv_ref.shape[0]
    gb = cv_ref[:, 0:D_CONV]
    g = cv_ref[:, D_CONV:2 * D_CONV] * cv_ref[:, 2 * D_CONV:3 * D_CONV]
    zero = jnp.zeros((SUBLANES, D_CONV), F32)
    pad_ref[0:SUBLANES, :] = zero
    pad_ref[n + SUBLANES:n + 2 * SUBLANES, :] = zero
    pad_ref[SUBLANES:n + SUBLANES, :] = g
    prev = pad_ref[SUBLANES - 1:n + SUBLANES - 1, :]
    nxt = pad_ref[SUBLANES + 1:n + SUBLANES + 1, :]
    w = wc_ref[0]
    o_ref[...] = (gb * (w[0:1, :] * prev + w[1:2, :] * g + w[2:3, :] * nxt)).astype(BF16)


def _short_conv(cv, w_conv, li, n, first_block):
    return pl.pallas_call(
        _conv_kernel,
        name="short_conv",
        grid=(BATCH,),
        in_specs=[
            pl.BlockSpec((n, 3 * D_CONV), lambda b: (first_block + b, 0)),
            pl.BlockSpec((1, 3, D_CONV), lambda b: (li, 0, 0)),
        ],
        out_specs=pl.BlockSpec((n, D_CONV), lambda b: (b, 0)),
        out_shape=jax.ShapeDtypeStruct((BATCH * n, D_CONV), BF16),
        scratch_shapes=[pltpu.VMEM((n + 2 * SUBLANES, D_CONV), F32)],
        compiler_params=_cparams("arbitrary"),
    )(cv, w_conv)


RPB_H = 2 * NA_KH - 1
RPB_W = 2 * NA_KW - 1
KEYS_LOC = NA_KH * GRID_W


def _bias_kernel(rpb_ref, o_ref):
    h = pl.program_id(0)
    qi = lax.broadcasted_iota(jnp.int32, (GRID_W, LANES), 0)
    lj = lax.broadcasted_iota(jnp.int32, (GRID_W, LANES), 1)
    kc = lj & (GRID_W - 1)
    hi = lj >= GRID_W
    d = kc - qi + (NA_KW - 1)
    cs = jnp.clip(qi - NA_KW // 2, 0, GRID_W - NA_KW)
    valid = (kc >= cs) & (kc < cs + NA_KW)
    tiles = []
    for a in range(RPB_H - 1):
        acc = jnp.zeros((GRID_W, LANES), F32)
        for b in range(RPB_W):
            va = rpb_ref[(h * RPB_H + a) * RPB_W + b]
            vb = rpb_ref[(h * RPB_H + a + 1) * RPB_W + b]
            acc = jnp.where(d == b, jnp.where(hi, vb, va), acc)
        tiles.append(jnp.where(valid, acc, NEG_INF))
    for cls in range(NA_KH):
        for m in range(NA_KH // 2):
            o_ref[0, cls, :, m * LANES:(m + 1) * LANES] = tiles[2 * m - cls + NA_KH - 1]


def _bias_table(rpb_flat):
    return pl.pallas_call(
        _bias_kernel,
        name="attn_bias_table",
        grid=(NA_HEADS,),
        in_specs=[pl.BlockSpec(memory_space=pltpu.SMEM)],
        out_specs=pl.BlockSpec((1, NA_KH, GRID_W, KEYS_LOC), lambda h: (h // 2, 0, h % 2, 0)),
        out_shape=jax.ShapeDtypeStruct((NA_HEADS // 2, NA_KH, 2 * GRID_W, KEYS_LOC), F32),
        compiler_params=_cparams("arbitrary"),
    )(rpb_flat)


_NT_DIMS = (((1,), (1,)), ((), ()))


def _softmax_pv(parts):
    m = None
    for s, _ in parts:
        ms = jnp.max(s, axis=1, keepdims=True)
        m = ms if m is None else jnp.maximum(m, ms)
    l = None
    o = None
    for s, v in parts:
        p = jnp.exp(s - m)
        ls = jnp.sum(p, axis=1, keepdims=True)
        os_ = jnp.dot(p.astype(BF16), v, preferred_element_type=F32)
        l = ls if l is None else l + ls
        o = os_ if o is None else o + os_
    return o / l


def _attn_latent_kernel(q_ref, k_ref, v_ref, kc_ref, vc_ref, tab_ref, o_ref, ve_ref, vce_ref):
    lane = lax.broadcasted_iota(jnp.int32, (GRID_W, LANES), 1)
    lo = lane < HEAD_DIM
    ve_ref[:, 0:LANES] = v_ref[...]
    ve_ref[:, LANES:2 * LANES] = jnp.ones((SEQ, LANES), BF16)
    vce_ref[:, 0:LANES] = vc_ref[...]
    vce_ref[:, LANES:2 * LANES] = jnp.ones((CTX_LEN, LANES), BF16)
    kc = kc_ref[...]
    vce = vce_ref[...]

    for r in range(ROWS):
        rs = min(max(r - NA_KH // 2, 0), ROWS - NA_KH)
        cls = r - rs
        q = q_ref[r * GRID_W:(r + 1) * GRID_W, :]
        kl = k_ref[rs * GRID_W:rs * GRID_W + KEYS_LOC, :]
        vl = ve_ref[rs * GRID_W:rs * GRID_W + KEYS_LOC, :]
        zero = jnp.zeros_like(q)
        q2 = jnp.concatenate([jnp.where(lo, q, zero), jnp.where(lo, zero, q)], axis=0)
        s1 = lax.dot_general(q2, kl, _NT_DIMS, preferred_element_type=F32) + tab_ref[0, cls]
        s2 = lax.dot_general(q2, kc, _NT_DIMS, preferred_element_type=F32)
        m = jnp.maximum(jnp.max(s1, axis=1, keepdims=True), jnp.max(s2, axis=1, keepdims=True))
        p1 = jnp.exp(s1 - m).astype(BF16)
        p2 = jnp.exp(s2 - m).astype(BF16)
        oe = (jnp.dot(p1, vl, preferred_element_type=F32) + jnp.dot(p2, vce, preferred_element_type=F32))
        o = oe[:, 0:LANES] / oe[:, LANES:2 * LANES]
        o_ref[r * GRID_W:(r + 1) * GRID_W, :] = jnp.where(lo, o[0:GRID_W], o[GRID_W:2 * GRID_W]).astype(BF16)


def _attn_latent(qkv, table):
    npair = NA_HEADS // 2
    cblk = T_LAT // CTX_LEN
    return pl.pallas_call(
        _attn_latent_kernel,
        name="attn_latent",
        grid=(BATCH, npair),
        in_specs=[
            pl.BlockSpec((SEQ, LANES), lambda b, p: (b, p)),
            pl.BlockSpec((SEQ, LANES), lambda b, p: (b, npair + p)),
            pl.BlockSpec((SEQ, LANES), lambda b, p: (b, 2 * npair + p)),
            pl.BlockSpec((CTX_LEN, LANES), lambda b, p: (cblk + b, npair + p)),
            pl.BlockSpec((CTX_LEN, LANES), lambda b, p: (cblk + b, 2 * npair + p)),
            pl.BlockSpec((1, NA_KH, 2 * GRID_W, KEYS_LOC), lambda b, p: (p, 0, 0, 0)),
        ],
        out_specs=pl.BlockSpec((SEQ, LANES), lambda b, p: (b, p)),
        out_shape=jax.ShapeDtypeStruct((T_LAT, D_NA), BF16),
        scratch_shapes=[pltpu.VMEM((SEQ, 2 * LANES), BF16), pltpu.VMEM((CTX_LEN, 2 * LANES), BF16)],
        compiler_params=_cparams("arbitrary", "arbitrary"),
    )(qkv, qkv, qkv, qkv, qkv, table)


def _attn_ctx_kernel(q_ref, k_ref, v_ref, o_ref):
    lane = lax.broadcasted_iota(jnp.int32, (CTX_LEN, LANES), 1)
    lo = lane < HEAD_DIM
    q = q_ref[...]
    k = k_ref[...]
    v = v_ref[...]
    outs = []
    for hh in range(2):
        qm = jnp.where(lo if hh == 0 else jnp.logical_not(lo), q, jnp.zeros_like(q))
        s = lax.dot_general(qm, k, _NT_DIMS, preferred_element_type=F32)
        outs.append(_softmax_pv([(s, v)]))
    o_ref[...] = jnp.where(lo, outs[0], outs[1]).astype(BF16)


def _attn_ctx(qkv):
    npair = NA_HEADS // 2
    cblk = T_LAT // CTX_LEN
    return pl.pallas_call(
        _attn_ctx_kernel,
        name="attn_context",
        grid=(BATCH, npair),
        in_specs=[
            pl.BlockSpec((CTX_LEN, LANES), lambda b, p: (cblk + b, p)),
            pl.BlockSpec((CTX_LEN, LANES), lambda b, p: (cblk + b, npair + p)),
            pl.BlockSpec((CTX_LEN, LANES), lambda b, p: (cblk + b, 2 * npair + p)),
        ],
        out_specs=pl.BlockSpec((CTX_LEN, LANES), lambda b, p: (b, p)),
        out_shape=jax.ShapeDtypeStruct((T_CTX, D_NA), BF16),
        compiler_params=_cparams("arbitrary", "arbitrary"),
    )(qkv, qkv, qkv)


ROUTE_COLS = N_GROUPS + N_EXPERTS


def _to_token_major(ref, val):
    rows = val.shape[0]
    for j in range(D_MODEL // LANES):
        ref[pl.ds(j, rows, stride=SUBLANES), :] = val[:, j * LANES:(j + 1) * LANES]


def _from_token_major(ref, rows, j):
    return ref[pl.ds(j, rows, stride=SUBLANES), :]


OUT_SPLIT = 2


def _outproj_kernel(xa_ref, xb_ref, ffa_ref, ffb_ref, ata_ref, atb_ref, cva_ref, cvb_ref,
                    mod_ref, n2_ref, wo_ref, wr_ref, br_ref, x1_ref, hx_ref, route_ref):
    rows = TM // OUT_SPLIT
    for s in range(OUT_SPLIT):
        sl = pl.ds(s * rows, rows)
        ins = [r.at[sl] for r in (xa_ref, xb_ref, ffa_ref, ffb_ref, ata_ref, atb_ref, cva_ref, cvb_ref)]
        _outproj_rows(*ins, mod_ref, n2_ref, wo_ref, wr_ref, br_ref, x1_ref.at[sl],
                      hx_ref.at[pl.ds(s * rows * SUBLANES, rows * SUBLANES)], route_ref.at[sl])


def _outproj_rows(xa_ref, xb_ref, ffa_ref, ffb_ref, ata_ref, atb_ref, cva_ref, cvb_ref,
                  mod_ref, n2_ref, wo_ref, wr_ref, br_ref, x1_ref, hx_ref, route_ref):
    r1 = D_FOURIER
    r2 = D_FOURIER + D_NA
    mix = (jnp.dot(_pick_stream(ffa_ref, ffb_ref), wo_ref[0, 0:r1, :], preferred_element_type=F32)
           + jnp.dot(_pick_stream(ata_ref, atb_ref), wo_ref[0, r1:r2, :], preferred_element_type=F32)
           + jnp.dot(_pick_stream(cva_ref, cvb_ref), wo_ref[0, r2:D_MODEL, :], preferred_element_type=F32))
    g1 = mod_ref[:, 2 * D_MODEL:3 * D_MODEL]
    x1 = _pick_stream(xa_ref, xb_ref) + g1 * mix
    x1_ref[...] = x1
    ms = jnp.mean(x1 * x1, axis=-1, keepdims=True)
    y = x1 * lax.rsqrt(ms + EPS) * n2_ref[0]
    sh2 = mod_ref[:, 3 * D_MODEL:4 * D_MODEL]
    sc2 = mod_ref[:, 4 * D_MODEL:5 * D_MODEL]
    hx = y * (1.0 + sc2) + sh2
    _to_token_major(hx_ref, hx)

    hx_hi = hx.astype(BF16)
    hx_lo = (hx - hx_hi.astype(F32)).astype(BF16)
    part = (jnp.dot(hx_hi, wr_ref[0], preferred_element_type=F32)
            + jnp.dot(hx_lo, wr_ref[0], preferred_element_type=F32))
    logits = part[:, 0:LANES] + part[:, LANES:2 * LANES] + br_ref[0]
    tm = logits.shape[0]
    lane = lax.broadcasted_iota(jnp.int32, (tm, LANES), 1)
    lane_f = lane.astype(F32)
    group_of_lane = ((lane - N_GROUPS) >> 3).astype(F32)
    big = float(LANES)
    gl = jnp.where(lane < N_GROUPS, logits, -jnp.inf)
    gmax = jnp.max(gl, axis=1, keepdims=True)
    gidx = jnp.min(jnp.where(gl == gmax, lane_f, big), axis=1, keepdims=True)
    g_w = 1.0 / jnp.sum(jnp.exp(gl - gmax), axis=1, keepdims=True)
    in_group = (lane >= N_GROUPS) & (lane < N_GROUPS + N_EXPERTS) & (group_of_lane == gidx)
    es = jnp.where(in_group, logits, -jnp.inf)
    t1 = jnp.max(es, axis=1, keepdims=True)
    i1 = jnp.min(jnp.where(es == t1, lane_f, big), axis=1, keepdims=True)
    es2 = jnp.where(lane_f == i1, -jnp.inf, es)
    t2 = jnp.max(es2, axis=1, keepdims=True)
    i2 = jnp.min(jnp.where(es2 == t2, lane_f, big), axis=1, keepdims=True)
    dlt = jnp.exp(t2 - t1)
    w1 = g_w / (1.0 + dlt)
    w2 = g_w * dlt / (1.0 + dlt)
    e1 = i1 - float(N_GROUPS)
    e2 = i2 - float(N_GROUPS)
    route_ref[...] = jnp.where(lane == 0, e1, jnp.where(lane == 1, e2,
                               jnp.where(lane == 2, w1, jnp.where(lane == 3, w2, 0.0))))


def _out_projection(x_pair, ctx_first_block, ff_pair, at_pair, cv_pair, mod4, norm2, w_out_bf, w_r, b_r, li, n_tok):
    nt = n_tok // TM
    return pl.pallas_call(
        _outproj_kernel,
        name="out_projection",
        grid=(nt,),
        in_specs=[
            _lat_spec(D_MODEL), _ctx_spec(D_MODEL, ctx_first_block),
            _lat_spec(D_FOURIER), _ctx_spec(D_FOURIER, 0),
            _lat_spec(D_NA), _ctx_spec(D_NA, 0),
            _lat_spec(D_CONV), _ctx_spec(D_CONV, 0),
            pl.BlockSpec((None, None, 1, 6 * D_MODEL), lambda i: (li, _mod_row(i, TM), 0, 0)),
            pl.BlockSpec((1, 1, D_MODEL), lambda i: (li, 0, 0)),
            pl.BlockSpec((1, D_MODEL, D_MODEL), lambda i: (li, 0, 0)),
            pl.BlockSpec((1, D_MODEL, 2 * LANES), lambda i: (li, 0, 0)),
            pl.BlockSpec((1, 1, LANES), lambda i: (li, 0, 0)),
        ],
        out_specs=[
            pl.BlockSpec((TM, D_MODEL), lambda i: (i, 0)),
            pl.BlockSpec((TM * SUBLANES, LANES), lambda i: (i, 0)),
            pl.BlockSpec((TM, LANES), lambda i: (i, 0)),
        ],
        out_shape=[
            jax.ShapeDtypeStruct((n_tok, D_MODEL), F32),
            jax.ShapeDtypeStruct((n_tok * SUBLANES, LANES), F32),
            jax.ShapeDtypeStruct((n_tok, LANES), F32),
        ],
        compiler_params=_cparams("arbitrary"),
    )(*x_pair, *ff_pair, *at_pair, *cv_pair, mod4, norm2.reshape(DEPTH, 1, D_MODEL), w_out_bf, w_r, b_r)


def _plan_kernel(route_ref, pos_ref, cnt_ref, ends_ref, carry_ref, offs_ref):
    ph = pl.program_id(0)
    i = pl.program_id(1)
    tm = PLAN_SUB
    lane = lax.broadcasted_iota(jnp.int32, (tm, LANES), 1)

    def one_hots(k):
        r = route_ref[k * tm:(k + 1) * tm, :]
        oh1 = lane == r[:, 0:1].astype(jnp.int32)
        oh2 = lane == r[:, 1:2].astype(jnp.int32)
        return oh1, oh2, jnp.where(oh1 | oh2, 1.0, 0.0)

    @pl.when((ph == 0) & (i == 0))
    def _():
        carry_ref[...] = jnp.zeros_like(carry_ref)

    @pl.when(ph == 0)
    def _():
        total = carry_ref[...]
        for k in range(PLAN_TM // tm):
            total = total + jnp.sum(one_hots(k)[2], axis=0, keepdims=True)
        carry_ref[...] = total

    @pl.when((ph == 1) & (i == 0))
    def _():
        cnt = carry_ref[...]
        cnt_ref[...] = jnp.broadcast_to(cnt, cnt_ref.shape)
        tiles = jnp.ceil(cnt * (1.0 / TMX))
        a = lax.broadcasted_iota(jnp.int32, (LANES, LANES), 0)
        b = lax.broadcasted_iota(jnp.int32, (LANES, LANES), 1)
        upper = jnp.where(a < b, 1.0, 0.0).astype(BF16)
        tiles8 = jnp.broadcast_to(tiles, (SUBLANES, LANES))
        first = jnp.dot(tiles8.astype(BF16), upper, preferred_element_type=F32)
        offs_ref[...] = first[0:1, :] * TMX
        ends_ref[...] = (first + tiles8) * TMX
        carry_ref[...] = jnp.zeros_like(carry_ref)

    @pl.when(ph == 1)
    def _():
        row = lax.broadcasted_iota(jnp.int32, (tm, tm), 0)
        col = lax.broadcasted_iota(jnp.int32, (tm, tm), 1)
        tri = jnp.where(row > col, 1.0, 0.0).astype(BF16)
        base = carry_ref[...] + offs_ref[...]
        for k in range(PLAN_TM // tm):
            oh1, oh2, oh = one_hots(k)
            cum = jnp.dot(tri, oh.astype(BF16), preferred_element_type=F32) + base
            p1 = jnp.sum(jnp.where(oh1, cum, 0.0), axis=1, keepdims=True)
            p2 = jnp.sum(jnp.where(oh2, cum, 0.0), axis=1, keepdims=True)
            pos_ref[k * tm:(k + 1) * tm, :] = (
                jnp.where(lane == 0, p1, jnp.where(lane == 1, p2, 0.0)).astype(jnp.int32))
            base = base + jnp.sum(oh, axis=0, keepdims=True)
        carry_ref[...] = base - offs_ref[...]


PLAN_TM = 2048
PLAN_SUB = 512


def _plan(route, n_tok):
    nt = n_tok // PLAN_TM
    return pl.pallas_call(
        _plan_kernel,
        name="moe_plan",
        grid=(2, nt),
        in_specs=[pl.BlockSpec((PLAN_TM, LANES), lambda ph, i: (i, 0))],
        out_specs=[
            pl.BlockSpec((PLAN_TM, LANES), lambda ph, i: (i * ph, 0)),
            pl.BlockSpec((SUBLANES, LANES), lambda ph, i: (0, 0)),
            pl.BlockSpec((SUBLANES, LANES), lambda ph, i: (0, 0)),
        ],
        out_shape=[
            jax.ShapeDtypeStruct((n_tok, LANES), jnp.int32),
            jax.ShapeDtypeStruct((SUBLANES, LANES), F32),
            jax.ShapeDtypeStruct((SUBLANES, LANES), F32),
        ],
        scratch_shapes=[pltpu.VMEM((1, LANES), F32), pltpu.VMEM((1, LANES), F32)],
        compiler_params=_cparams("arbitrary", "arbitrary"),
    )(route)


def _row_tile(ref, row):
    return ref.at[pl.ds(pl.multiple_of(row * SUBLANES, SUBLANES), SUBLANES), :]


DISPATCH_UNROLL = 8
PAD_CHUNKS = tuple(1 << b for b in reversed(range(TMX.bit_length() - 1)))


def _dispatch_kernel(pos_ref, pstart_ref, plen_ref, tail_ref, hx_ref, xs_ref, zbuf, sem, zsem):
    i = pl.program_id(0)

    def pad_copies(fn):
        def ebody(e, carry):
            n = plen_ref[e]
            off = pstart_ref[e]
            for rows in PAD_CHUNKS:
                @pl.when((n & rows) != 0)
                def _():
                    fn(pltpu.make_async_copy(
                        zbuf.at[pl.ds(0, rows * SUBLANES), :],
                        xs_ref.at[pl.ds(pl.multiple_of(off * SUBLANES, SUBLANES), rows * SUBLANES), :], zsem.at[0]))
                off = off + (n & rows)
            return carry

        lax.fori_loop(0, N_EXPERTS, ebody, 0)

        def tbody(k, carry):
            row = pl.multiple_of((tail_ref[0] + k * PAD_CHUNKS[0]) * SUBLANES, SUBLANES)
            fn(pltpu.make_async_copy(zbuf, xs_ref.at[pl.ds(row, PAD_CHUNKS[0] * SUBLANES), :], zsem.at[0]))
            return carry

        lax.fori_loop(0, tail_ref[1], tbody, 0)

    @pl.when(i == 0)
    def _():
        zbuf[...] = jnp.zeros_like(zbuf)
        pad_copies(lambda cp: cp.start())

    def body(k, carry):
        for u in range(DISPATCH_UNROLL):
            t = k * DISPATCH_UNROLL + u
            src = _row_tile(hx_ref, t)
            for j in range(2):
                pltpu.make_async_copy(src, _row_tile(xs_ref, pos_ref[(i * TD + t) * 2 + j]),
                                      sem.at[0]).start(priority=j)
        return carry

    lax.fori_loop(0, TD // DISPATCH_UNROLL, body, 0)
    nrow = TD * SUBLANES
    for j in range(2):
        pltpu.make_async_copy(hx_ref, xs_ref.at[pl.ds(0, nrow), :], sem.at[0]).wait()

    @pl.when(i == 0)
    def _():
        pad_copies(lambda cp: cp.wait())


def _dispatch(pos_flat, pad_start, pad_len, tail, hx_tm, n_tok, n_rows):
    return pl.pallas_call(
        _dispatch_kernel,
        name="moe_dispatch",
        grid_spec=pltpu.PrefetchScalarGridSpec(
            num_scalar_prefetch=4,
            grid=(n_tok // TD,),
            in_specs=[pl.BlockSpec((TD * SUBLANES, LANES), lambda i, pos, ps, pn, tl: (i, 0))],
            out_specs=pl.BlockSpec(memory_space=pl.ANY),
            scratch_shapes=[pltpu.VMEM((PAD_CHUNKS[0] * SUBLANES, LANES), F32),
                            pltpu.SemaphoreType.DMA((1,)), pltpu.SemaphoreType.DMA((1,))],
        ),
        out_shape=jax.ShapeDtypeStruct((n_rows * SUBLANES, LANES), F32),
        compiler_params=_cparams("arbitrary"),
    )(pos_flat, pad_start, pad_len, tail, hx_tm)


XS_SLOTS = 3


def _moe_kernel(te_ref, na_ref, first_ref, wslot_ref, nxt_ref, xs_hbm, wg_hbm, wu_hbm, wd_hbm, os_ref,
                xbuf, wg_f, wu_f, wd_f, wg_s, wu_s, wd_s, xsem, wsem, *, li):
    i = pl.program_id(0)
    n_act = na_ref[0]
    active = i < n_act

    def xs_copy(tile, slot):
        rows = pl.ds(pl.multiple_of(tile * (TMX * SUBLANES), SUBLANES), TMX * SUBLANES)
        return pltpu.make_async_copy(xs_hbm.at[rows, :], xbuf.at[slot], xsem.at[slot])

    @pl.when(i == 0)
    def _():
        xs_copy(0, 0).start()

        @pl.when(n_act > 1)
        def _():
            xs_copy(1, 1).start()

    @pl.when(i + 2 < n_act)
    def _():
        xs_copy(i + 2, (i + 2) % XS_SLOTS).start()

    def weight_copies(e, slot):
        return [pltpu.make_async_copy(wg_hbm.at[li, e], wg_f.at[slot], wsem.at[slot, 0]),
                pltpu.make_async_copy(wu_hbm.at[li, e], wu_f.at[slot], wsem.at[slot, 1]),
                pltpu.make_async_copy(wd_hbm.at[li, e], wd_f.at[slot], wsem.at[slot, 2])]

    @pl.when(active & (first_ref[i] == 1))
    def _():
        slot = wslot_ref[i]

        @pl.when(i == 0)
        def _():
            for cp in weight_copies(te_ref[0], 0):
                cp.start()

        @pl.when(nxt_ref[i] >= 0)
        def _():
            for cp in weight_copies(nxt_ref[i], 1 - slot):
                cp.start()

        for cp in weight_copies(te_ref[i], slot):
            cp.wait()
        wg_s[...] = wg_f[slot].astype(BF16)
        wu_s[...] = wu_f[slot].astype(BF16)
        wd_s[...] = wd_f[slot].astype(BF16)

    @pl.when(active)
    def _():
        xs_copy(i, i % XS_SLOTS).wait()
        xs_ref = xbuf.at[i % XS_SLOTS]
        x = jnp.concatenate([_from_token_major(xs_ref, TMX, j).astype(BF16)
                             for j in range(D_MODEL // LANES)], axis=-1)
        g = jnp.dot(x, wg_s[...], preferred_element_type=F32)
        u = jnp.dot(x, wu_s[...], preferred_element_type=F32)
        h = (g * jax.nn.sigmoid(g) * u).astype(BF16)
        _to_token_major(os_ref, jnp.dot(h, wd_s[...], preferred_element_type=F32))

    @pl.when(jnp.logical_not(active))
    def _():
        os_ref[...] = jnp.zeros_like(os_ref)


def _moe(tile_expert, n_active, has_rows, xs, w_gate, w_up, w_down, li, n_tiles):
    eid = jnp.arange(N_EXPERTS, dtype=jnp.int32)
    later = jnp.where(has_rows[None, :] & (eid[None, :] > eid[:, None]), eid[None, :], N_EXPERTS)
    next_of_expert = jnp.min(later, axis=1)
    next_of_expert = jnp.where(next_of_expert == N_EXPERTS, -1, next_of_expert)
    first = jnp.concatenate([jnp.ones((1,), jnp.int32),
                             (tile_expert[1:] != tile_expert[:-1]).astype(jnp.int32)])
    wslot = (jnp.cumsum(first) - 1) % 2
    nxt = jnp.sum(jnp.where(tile_expert[:, None] == eid[None, :], next_of_expert[None, :], 0), axis=1)


    return pl.pallas_call(
        functools.partial(_moe_kernel, li=li),
        name="moe_experts",
        grid_spec=pltpu.PrefetchScalarGridSpec(
            num_scalar_prefetch=5,
            grid=(n_tiles,),
            in_specs=[
                pl.BlockSpec(memory_space=pl.ANY),
                pl.BlockSpec(memory_space=pl.ANY),
                pl.BlockSpec(memory_space=pl.ANY),
                pl.BlockSpec(memory_space=pl.ANY),
            ],
            out_specs=pl.BlockSpec((TMX * SUBLANES, LANES), lambda i, te, na, fi, ws, nx: (i, 0)),
            scratch_shapes=[
                pltpu.VMEM((XS_SLOTS, TMX * SUBLANES, LANES), F32),
                pltpu.VMEM((2, D_MODEL, D_EXPERT), F32),
                pltpu.VMEM((2, D_MODEL, D_EXPERT), F32),
                pltpu.VMEM((2, D_EXPERT, D_MODEL), F32),
                pltpu.VMEM((D_MODEL, D_EXPERT), BF16),
                pltpu.VMEM((D_MODEL, D_EXPERT), BF16),
                pltpu.VMEM((D_EXPERT, D_MODEL), BF16),
                pltpu.SemaphoreType.DMA((XS_SLOTS,)),
                pltpu.SemaphoreType.DMA((2, 3)),
            ],
        ),
        out_shape=jax.ShapeDtypeStruct((n_tiles * TMX * SUBLANES, LANES), F32),
        compiler_params=_cparams("arbitrary"),
    )(tile_expert, n_active, first, wslot.astype(jnp.int32), nxt.astype(jnp.int32), xs, w_gate, w_up, w_down)


def _gather_pipeline(pos_ref, os_ref, buf_even, buf_odd, sem, n, compute):
    i = pl.program_id(0)
    last = pl.num_programs(0) - 1
    bufs = (buf_even, buf_odd)

    def wait_rows(k):
        for j in range(2):
            pltpu.make_async_copy(os_ref.at[pl.ds(0, n * SUBLANES), :], bufs[k].at[j], sem.at[k, j]).wait()

    @pl.when(i == 0)
    def _():
        def body(t, carry):
            for j in range(2):
                pltpu.make_async_copy(_row_tile(os_ref, pos_ref[t * 2 + j]), _row_tile(buf_even.at[j], t),
                                      sem.at[0, j]).start(priority=j)
            return carry

        lax.fori_loop(0, n, body, 0)

    def step(cur, nxt):
        wait_rows(cur)
        base = jnp.minimum(i + 1, last) * n
        for t in range(n):
            for j in range(2):
                pltpu.make_async_copy(_row_tile(os_ref, pos_ref[(base + t) * 2 + j]),
                                      bufs[nxt].at[j, pl.ds(t * SUBLANES, SUBLANES), :],
                                      sem.at[nxt, j]).start(priority=j)
        compute(bufs[cur])

        @pl.when(i == last)
        def _():
            wait_rows(nxt)

    @pl.when(i % 2 == 0)
    def _():
        step(0, 1)

    @pl.when(i % 2 == 1)
    def _():
        step(1, 0)


def _combined_rows(x1_ref, route_ref, mod_ref, buf, r0, rows):
    sl = pl.ds(r0, rows)
    w1 = route_ref[sl, 2:3]
    w2 = route_ref[sl, 3:4]
    tm_rows = pl.ds(r0 * SUBLANES, rows * SUBLANES)
    cols = []
    for c in range(D_MODEL // LANES):
        y = (w1 * _from_token_major(buf.at[0, tm_rows], rows, c)
             + w2 * _from_token_major(buf.at[1, tm_rows], rows, c))
        g2 = mod_ref[:, 5 * D_MODEL + c * LANES:5 * D_MODEL + (c + 1) * LANES]
        cols.append(x1_ref[sl, c * LANES:(c + 1) * LANES] + g2 * y)
    return jnp.concatenate(cols, axis=-1)


ROW_SPLIT = 2


def _combine_final_kernel(pos_ref, x1_ref, route_ref, mod_ref, nf_ref, os_ref, o_ref, buf_even, buf_odd, sem):
    def compute(buf):
        x2 = _combined_rows(x1_ref, route_ref, mod_ref, buf, 0, TC)
        ms = jnp.mean(x2 * x2, axis=-1, keepdims=True)
        o_ref[...] = x2 * lax.rsqrt(ms + EPS) * nf_ref[...]

    _gather_pipeline(pos_ref, os_ref, buf_even, buf_odd, sem, TC, compute)


def _gather_scratch(n):
    return [pltpu.VMEM((2, n * SUBLANES, LANES), F32), pltpu.VMEM((2, n * SUBLANES, LANES), F32),
            pltpu.SemaphoreType.DMA((2, 2))]


def _combine_inproj_kernel(pos_ref, x1_ref, route_ref, mod_ref, modn_ref, n1_ref, w_ref, dft_ref, os_ref,
                           x2_ref, yf_ref, qkv_ref, cv_ref, buf_even, buf_odd, sem):
    def compute(buf):
        rows = TM // ROW_SPLIT
        for s in range(ROW_SPLIT):
            sl = pl.ds(s * rows, rows)
            x2 = _combined_rows(x1_ref, route_ref, mod_ref, buf, s * rows, rows)
            x2_ref[sl, :] = x2
            _inproj_math(x2, modn_ref, n1_ref, w_ref, dft_ref, yf_ref.at[sl], qkv_ref.at[sl], cv_ref.at[sl])

    _gather_pipeline(pos_ref, os_ref, buf_even, buf_odd, sem, TM, compute)


def _combine_inproj(pos_flat, x1, route, mod4, norm1, w_in_bf, dft_ch, osrt, li):
    def tmap(i, pos):
        return (i, 0)

    return pl.pallas_call(
        _combine_inproj_kernel,
        name="moe_combine_in_projection",
        grid_spec=pltpu.PrefetchScalarGridSpec(
            num_scalar_prefetch=1,
            grid=(T_ALL // TM,),
            in_specs=[
                pl.BlockSpec((TM, D_MODEL), tmap),
                pl.BlockSpec((TM, LANES), tmap),
                pl.BlockSpec((None, None, 1, 6 * D_MODEL), lambda i, pos: (li, _mod_row(i, TM), 0, 0)),
                pl.BlockSpec((None, None, 1, 6 * D_MODEL), lambda i, pos: (li + 1, _mod_row(i, TM), 0, 0)),
                pl.BlockSpec((1, 1, D_MODEL), lambda i, pos: (li + 1, 0, 0)),
                pl.BlockSpec((1, D_MODEL, D_IN_PROJ), lambda i, pos: (li + 1, 0, 0)),
                pl.BlockSpec((D_FOURIER, 2 * D_FOURIER), lambda i, pos: (0, 0)),
                pl.BlockSpec(memory_space=pl.ANY),
            ],
            out_specs=[
                pl.BlockSpec((TM, D_MODEL), tmap),
                pl.BlockSpec((TM, 2 * D_FOURIER), tmap),
                pl.BlockSpec((TM, 3 * D_NA), tmap),
                pl.BlockSpec((TM, 3 * D_CONV), tmap),
            ],
            scratch_shapes=_gather_scratch(TM),
        ),
        out_shape=[
            jax.ShapeDtypeStruct((T_ALL, D_MODEL), F32),
            jax.ShapeDtypeStruct((T_ALL, 2 * D_FOURIER), BF16),
            jax.ShapeDtypeStruct((T_ALL, 3 * D_NA), BF16),
            jax.ShapeDtypeStruct((T_ALL, 3 * D_CONV), F32),
        ],
        compiler_params=_cparams("arbitrary"),
    )(pos_flat, x1, route, mod4, mod4, norm1.reshape(DEPTH, 1, D_MODEL), w_in_bf, dft_ch, osrt)


def _combine_final(pos_flat, x1, route, mod4, norm_final, osrt, li, n_tok):
    def tmap(i, pos):
        return (i, 0)

    return pl.pallas_call(
        _combine_final_kernel,
        name="moe_combine_final",
        grid_spec=pltpu.PrefetchScalarGridSpec(
            num_scalar_prefetch=1,
            grid=(n_tok // TC,),
            in_specs=[
                pl.BlockSpec((TC, D_MODEL), tmap),
                pl.BlockSpec((TC, LANES), tmap),
                pl.BlockSpec((None, None, 1, 6 * D_MODEL), lambda i, pos: (li, _mod_row(i, TC), 0, 0)),
                pl.BlockSpec((1, D_MODEL), lambda i, pos: (0, 0)),
                pl.BlockSpec(memory_space=pl.ANY),
            ],
            out_specs=pl.BlockSpec((TC, D_MODEL), tmap),
            scratch_shapes=_gather_scratch(TC),
        ),
        out_shape=jax.ShapeDtypeStruct((n_tok, D_MODEL), F32),
        compiler_params=_cparams("arbitrary"),
    )(pos_flat, x1, route, mod4, norm_final.reshape(1, D_MODEL), osrt)


def _moe_block(hx_tm, route, w_gate, w_up, w_down, li, n_tok):
    n_tiles = (2 * n_tok) // TMX + N_EXPERTS
    n_rows = n_tiles * TMX
    pos, cnt, ends = _plan(route, n_tok)
    pos = pos[:, 0:2].reshape(-1)
    cnt = cnt[0, :N_EXPERTS].astype(jnp.int32)
    ends = ends[0, :N_EXPERTS].astype(jnp.int32)
    padded = ((cnt + TMX - 1) // TMX) * TMX
    pad_start = ends - padded + cnt
    tile_start = jnp.arange(n_tiles, dtype=jnp.int32) * TMX
    tile_expert = jnp.minimum(jnp.sum((tile_start[:, None] >= ends[None, :]).astype(jnp.int32), axis=1),
                              N_EXPERTS - 1)
    n_active = (ends[-1:] // TMX).astype(jnp.int32)
    tail = jnp.concatenate([ends[-1:], (n_rows - ends[-1:]) // PAD_CHUNKS[0]])

    xs = _dispatch(pos, pad_start, padded - cnt, tail, hx_tm, n_tok, n_rows)
    return pos, _moe(tile_expert, n_active, padded > 0, xs, w_gate, w_up, w_down, li, n_tiles)


def kernel(x, c, ctx, c_ctx, w_ada, b_ada, norm1, norm2, w_in, w_fourier, w_conv, rpb, w_out, w_rg, b_rg,
           w_re, b_re, w_gate, w_up, w_down, norm_final):
    x_pair, ctx_first = (x.reshape(T_LAT, D_MODEL), ctx.reshape(T_CTX, D_MODEL)), 0
    cc =jnp.concatenate([c, c_ctx[None, :], jnp.zeros((MOD_ROWS - BATCH - 1, D_MODEL), F32)], axis=0)
    mod4 = _modulation(cc, w_ada, b_ada).reshape(DEPTH, MOD_ROWS, 1, 6 * D_MODEL)

    w_in_bf = w_in.astype(BF16)
    w_out_bf = w_out.astype(BF16)
    pad = jnp.zeros((DEPTH, D_MODEL, LANES - ROUTE_COLS), F32)
    w_r = jnp.concatenate([w_rg, w_re, pad], axis=-1)
    w_r_hi = w_r.astype(BF16)
    w_r = jnp.concatenate([w_r_hi, (w_r - w_r_hi.astype(F32)).astype(BF16)], axis=-1)
    b_r = jnp.concatenate([b_rg, b_re, pad[:, 0, :]], axis=-1).reshape(DEPTH, 1, LANES)
    dft_ch = jnp.asarray(_channel_dft(), dtype=F32).astype(BF16)
    ctx_blk = T_LAT // CTX_LEN

    for li in range(DEPTH):
        last = li == DEPTH - 1
        if li == 0:
            yf, qkv, cv = _in_projection(*x_pair, ctx_first, mod4, norm1, w_in_bf, dft_ch, li)
        table = _bias_table(rpb[li].reshape(-1))
        ff = _fourier_mix(yf, w_fourier, li, SEQ, 0)
        cvo = _short_conv(cv, w_conv, li, SEQ, 0)
        at = _attn_latent(qkv, table)
        if last:
            n_tok = T_LAT
            ff_c, cvo_c, at_c = ff, cvo, at
        else:
            n_tok = T_ALL
            ff_c = _fourier_mix(yf, w_fourier, li, CTX_LEN, ctx_blk)
            cvo_c = _short_conv(cv, w_conv, li, CTX_LEN, ctx_blk)
            at_c = _attn_ctx(qkv)
        x1, hx_tm, route = _out_projection(x_pair, ctx_first, (ff, ff_c), (at, at_c), (cvo, cvo_c), mod4, norm2,
                                           w_out_bf, w_r, b_r, li, n_tok)
        pos, osrt = _moe_block(hx_tm, route, w_gate, w_up, w_down, li, n_tok)
        if last:
            out = _combine_final(pos, x1, route, mod4, norm_final, osrt, li, n_tok)
            return out.reshape(BATCH, SEQ, D_MODEL)
        xt, yf, qkv, cv = _combine_inproj(pos, x1, route, mod4, norm1, w_in_bf, dft_ch, osrt, li)
        x_pair, ctx_first = (xt, xt), NT_LAT
```

```python
import functools
import math

import numpy as np
import jax
import jax.numpy as jnp
from jax import lax
from jax.experimental import pallas as pl
from jax.experimental.pallas import tpu as pltpu

F32 = jnp.float32
BF16 = jnp.bfloat16

D_MODEL = 1024
BATCH = 8
SEQ = 2048
DEPTH = 2
GRID_W = 64
ROWS = SEQ // GRID_W
CTX_LEN = 256
T_LAT = BATCH * SEQ
T_CTX = BATCH * CTX_LEN
T_ALL = T_LAT + T_CTX

D_FOURIER = 256
D_FG = 64
HEAD_DIM = 64
NA_HEADS = 8
D_NA = NA_HEADS * HEAD_DIM
D_CONV = 256
D_IN_PROJ = D_FOURIER + 3 * D_NA + 3 * D_CONV
NA_KH = 8
NA_KW = 16
N_GROUPS = 4
EXPERTS_PER_GROUP = 8
N_EXPERTS = 32
D_EXPERT = 512
EPS = 1e-6
NEG_INF = -1e30

LANES = 128
SUBLANES = 8
MOD_ROWS = 16
TM = 512
TMX = 512
TD = 512
TC = 512
VMEM_LIMIT = 56 * 1024 * 1024


def _cparams(*sem):
    return pltpu.CompilerParams(dimension_semantics=sem, vmem_limit_bytes=VMEM_LIMIT)


def _ada_kernel(cc_ref, w_ref, b_ref, o_ref):
    s = cc_ref[...]
    s = s * jax.nn.sigmoid(s)
    acc = jnp.dot(s.astype(BF16), w_ref[0].astype(BF16), preferred_element_type=F32)
    o_ref[0] = acc + b_ref[0]


def _modulation(cc, w_ada, b_ada):
    tn = 1536
    nj = 6 * D_MODEL // tn
    return pl.pallas_call(
        _ada_kernel,
        name="ada_modulation",
        grid=(DEPTH, nj),
        in_specs=[
            pl.BlockSpec((MOD_ROWS, D_MODEL), lambda l, j: (0, 0)),
            pl.BlockSpec((1, D_MODEL, tn), lambda l, j: (l, 0, j)),
            pl.BlockSpec((1, 1, tn), lambda l, j: (l, 0, j)),
        ],
        out_specs=pl.BlockSpec((1, MOD_ROWS, tn), lambda l, j: (l, 0, j)),
        out_shape=jax.ShapeDtypeStruct((DEPTH, MOD_ROWS, 6 * D_MODEL), F32),
        compiler_params=_cparams("arbitrary", "arbitrary"),
    )(cc, w_ada, b_ada.reshape(DEPTH, 1, 6 * D_MODEL))


def _mod_row(i, tile):
    return jnp.minimum((i * tile) // SEQ, BATCH)


NT_LAT = T_LAT // TM


def _lat_spec(cols):
    return pl.BlockSpec((TM, cols), lambda i: (jnp.minimum(i, NT_LAT - 1), 0))


def _ctx_spec(cols, first_block):
    return pl.BlockSpec((TM, cols), lambda i: (jnp.maximum(i - NT_LAT, 0) + first_block, 0))


def _pick_stream(lat_ref, ctx_ref):
    return jnp.where(pl.program_id(0) >= NT_LAT, ctx_ref[...], lat_ref[...])


def _inproj_kernel(xa_ref, xb_ref, mod_ref, n1_ref, w_ref, dft_ref, yf_ref, qkv_ref, cv_ref):
    rows = TM // ROW_SPLIT
    for s in range(ROW_SPLIT):
        sl = pl.ds(s * rows, rows)
        _inproj_math(_pick_stream(xa_ref.at[sl], xb_ref.at[sl]), mod_ref, n1_ref, w_ref, dft_ref,
                     yf_ref.at[sl], qkv_ref.at[sl], cv_ref.at[sl])


def _inproj_math(x, mod_ref, n1_ref, w_ref, dft_ref, yf_ref, qkv_ref, cv_ref):
    ms = jnp.mean(x * x, axis=-1, keepdims=True)
    y = x * lax.rsqrt(ms + EPS) * n1_ref[0]
    sh = mod_ref[:, 0:D_MODEL]
    sc = mod_ref[:, D_MODEL:2 * D_MODEL]
    h = y * (1.0 + sc) + sh
    u = jnp.dot(h.astype(BF16), w_ref[0], preferred_element_type=F32)
    yf = jnp.dot(u[:, 0:D_FOURIER].astype(BF16), dft_ref[...], preferred_element_type=F32)
    yf_ref[...] = yf.astype(BF16)
    q0 = D_FOURIER
    qkv_ref[:, 0:D_NA] = (u[:, q0:q0 + D_NA] * (1.0 / math.sqrt(HEAD_DIM))).astype(BF16)
    qkv_ref[:, D_NA:3 * D_NA] = u[:, q0 + D_NA:q0 + 3 * D_NA].astype(BF16)
    cv_ref[...] = u[:, q0 + 3 * D_NA:D_IN_PROJ]


def _in_projection(x_lat, x_ctx, ctx_first_block, mod4, norm1, w_in_bf, dft_ch, li):
    nt = T_ALL // TM
    return pl.pallas_call(
        _inproj_kernel,
        name="in_projection",
        grid=(nt,),
        in_specs=[
            _lat_spec(D_MODEL),
            _ctx_spec(D_MODEL, ctx_first_block),
            pl.BlockSpec((None, None, 1, 6 * D_MODEL), lambda i: (li, _mod_row(i, TM), 0, 0)),
            pl.BlockSpec((1, 1, D_MODEL), lambda i: (li, 0, 0)),
            pl.BlockSpec((1, D_MODEL, D_IN_PROJ), lambda i: (li, 0, 0)),
            pl.BlockSpec((D_FOURIER, 2 * D_FOURIER), lambda i: (0, 0)),
        ],
        out_specs=[
            pl.BlockSpec((TM, 2 * D_FOURIER), lambda i: (i, 0)),
            pl.BlockSpec((TM, 3 * D_NA), lambda i: (i, 0)),
            pl.BlockSpec((TM, 3 * D_CONV), lambda i: (i, 0)),
        ],
        out_shape=[
            jax.ShapeDtypeStruct((T_ALL, 2 * D_FOURIER), BF16),
            jax.ShapeDtypeStruct((T_ALL, 3 * D_NA), BF16),
            jax.ShapeDtypeStruct((T_ALL, 3 * D_CONV), F32),
        ],
        compiler_params=_cparams("arbitrary"),
    )(x_lat, x_ctx, mod4, norm1.reshape(DEPTH, 1, D_MODEL), w_in_bf, dft_ch)


def _dft_tables(n):
    k = np.arange(n, dtype=np.int64)
    ang = 2.0 * np.pi * ((k[:, None] * k[None, :]) % n).astype(np.float64) / n
    s = 1.0 / math.sqrt(n)
    return np.cos(ang) * s, np.sin(ang) * s


def _channel_dft():
    c, s = _dft_tables(D_FG)
    eye = np.eye(D_FOURIER // D_FG)
    return np.concatenate([np.kron(eye, c), np.kron(eye, s)], axis=1)


def _fourier_kernel(ch_ref, sh_ref, j_ref, y_ref, wf_ref, o_ref, fold_ref, rev_ref, *, n):
    h = n // 2
    nb = h // LANES
    df = D_FOURIER
    scale = 1.0 / math.sqrt(n)
    rev = j_ref[...]
    zero_rows = jnp.zeros((SUBLANES, 2 * df), F32)

    fold_ref[0:SUBLANES, :] = zero_rows
    for b in range(nb):
        blk = y_ref[n - LANES * (b + 1):n - LANES * b, :]
        fold_ref[SUBLANES + LANES * b:SUBLANES + LANES * (b + 1), :] = jnp.dot(rev, blk, preferred_element_type=F32)
    mirrored = fold_ref[SUBLANES - 1:SUBLANES - 1 + h, :]
    y_lo = y_ref[0:h, :].astype(F32)
    yc_even = (y_lo[:, 0:df] + mirrored[:, 0:df]).astype(BF16)
    ys_odd = (y_lo[:, df:2 * df] - mirrored[:, df:2 * df]).astype(BF16)

    parity = lax.broadcasted_iota(jnp.int32, (h, 1), 0) & 1
    sign = jnp.where(parity == 1, -scale, scale)
    y_mid = y_ref[h:h + 1, 0:df].astype(F32)
    p = jnp.dot(ch_ref[...], yc_even, preferred_element_type=F32) + sign * y_mid
    q = jnp.dot(sh_ref[...], ys_odd, preferred_element_type=F32)
    lo = (p - q).astype(BF16)
    w = (p + q).astype(BF16)
    mid = jnp.sum(sign * (y_lo[:, 0:df] + y_ref[h:n, 0:df].astype(F32)), axis=0, keepdims=True)

    for b in range(nb):
        blk = w[h - LANES * (b + 1):h - LANES * b, :]
        rev_ref[SUBLANES + LANES * b:SUBLANES + LANES * (b + 1), :] = jnp.dot(rev, blk, preferred_element_type=F32)
    rev_ref[0:SUBLANES, :] = jnp.broadcast_to(mid, (SUBLANES, df))
    hi = rev_ref[SUBLANES - 1:SUBLANES - 1 + h, :].astype(BF16)
    z = jnp.concatenate([lo, hi], axis=0)
    o_ref[...] = jnp.dot(z, wf_ref[0].astype(BF16), preferred_element_type=F32).astype(BF16)


def _fourier_mix(yf, w_fourier, li, n, first_block):
    h = n // 2
    cn, sn = _dft_tables(n)
    ch = jnp.asarray(cn[:h, :h], dtype=F32).astype(BF16)
    sh = jnp.asarray(sn[:h, :h], dtype=F32).astype(BF16)
    anti = jnp.asarray(np.eye(LANES)[::-1].copy(), dtype=F32).astype(BF16)
    return pl.pallas_call(
        functools.partial(_fourier_kernel, n=n),
        name="fourier_mix",
        grid=(BATCH,),
        in_specs=[
            pl.BlockSpec((h, h), lambda b: (0, 0)),
            pl.BlockSpec((h, h), lambda b: (0, 0)),
            pl.BlockSpec((LANES, LANES), lambda b: (0, 0)),
            pl.BlockSpec((n, 2 * D_FOURIER), lambda b: (first_block + b, 0)),
            pl.BlockSpec((1, D_FOURIER, D_FOURIER), lambda b: (li, 0, 0)),
        ],
        out_specs=pl.BlockSpec((n, D_FOURIER), lambda b: (b, 0)),
        out_shape=jax.ShapeDtypeStruct((BATCH * n, D_FOURIER), BF16),
        scratch_shapes=[pltpu.VMEM((h + SUBLANES, 2 * D_FOURIER), F32), pltpu.VMEM((h + SUBLANES, D_FOURIER), F32)],
        compiler_params=_cparams("arbitrary"),
    )(ch, sh, anti, yf, w_fourier)


def _conv_kernel(cv_ref, wc_ref, o_ref, pad_ref):
    n = cv_ref.shape[0]
    gb = cv_ref[:, 0:D_CONV]
    g = cv_ref[:, D_CONV:2 * D_CONV] * cv_ref[:, 2 * D_CONV:3 * D_CONV]
    zero = jnp.zeros((SUBLANES, D_CONV), F32)
    pad_ref[0:SUBLANES, :] = zero
    pad_ref[n + SUBLANES:n + 2 * SUBLANES, :] = zero
    pad_ref[SUBLANES:n + SUBLANES, :] = g
    prev = pad_ref[SUBLANES - 1:n + SUBLANES - 1, :]
    nxt = pad_ref[SUBLANES + 1:n + SUBLANES + 1, :]
    w = wc_ref[0]
    o_ref[...] = (gb * (w[0:1, :] * prev + w[1:2, :] * g + w[2:3, :] * nxt)).astype(BF16)


def _short_conv(cv, w_conv, li, n, first_block):
    return pl.pallas_call(
        _conv_kernel,
        name="short_conv",
        grid=(BATCH,),
        in_specs=[
            pl.BlockSpec((n, 3 * D_CONV), lambda b: (first_block + b, 0)),
            pl.BlockSpec((1, 3, D_CONV), lambda b: (li, 0, 0)),
        ],
        out_specs=pl.BlockSpec((n, D_CONV), lambda b: (b, 0)),
        out_shape=jax.ShapeDtypeStruct((BATCH * n, D_CONV), BF16),
        scratch_shapes=[pltpu.VMEM((n + 2 * SUBLANES, D_CONV), F32)],
        compiler_params=_cparams("arbitrary"),
    )(cv, w_conv)


RPB_H = 2 * NA_KH - 1
RPB_W = 2 * NA_KW - 1
KEYS_LOC = NA_KH * GRID_W


def _bias_kernel(rpb_ref, o_ref):
    h = pl.program_id(0)
    qi = lax.broadcasted_iota(jnp.int32, (GRID_W, LANES), 0)
    lj = lax.broadcasted_iota(jnp.int32, (GRID_W, LANES), 1)
    kc = lj & (GRID_W - 1)
    hi = lj >= GRID_W
    d = kc - qi + (NA_KW - 1)
    cs = jnp.clip(qi - NA_KW // 2, 0, GRID_W - NA_KW)
    valid = (kc >= cs) & (kc < cs + NA_KW)
    tiles = []
    for a in range(RPB_H - 1):
        acc = jnp.zeros((GRID_W, LANES), F32)
        for b in range(RPB_W):
            va = rpb_ref[(h * RPB_H + a) * RPB_W + b]
            vb = rpb_ref[(h * RPB_H + a + 1) * RPB_W + b]
            acc = jnp.where(d == b, jnp.where(hi, vb, va), acc)
        tiles.append(jnp.where(valid, acc, NEG_INF))
    for cls in range(NA_KH):
        for m in range(NA_KH // 2):
            o_ref[0, cls, :, m * LANES:(m + 1) * LANES] = tiles[2 * m - cls + NA_KH - 1]


def _bias_table(rpb_flat):
    return pl.pallas_call(
        _bias_kernel,
        name="attn_bias_table",
        grid=(NA_HEADS,),
        in_specs=[pl.BlockSpec(memory_space=pltpu.SMEM)],
        out_specs=pl.BlockSpec((1, NA_KH, GRID_W, KEYS_LOC), lambda h: (h // 2, 0, h % 2, 0)),
        out_shape=jax.ShapeDtypeStruct((NA_HEADS // 2, NA_KH, 2 * GRID_W, KEYS_LOC), F32),
        compiler_params=_cparams("arbitrary"),
    )(rpb_flat)


_NT_DIMS = (((1,), (1,)), ((), ()))


def _attn_latent_kernel(q_ref, k_ref, v_ref, kc_ref, vc_ref, tab_ref, *rest, with_ctx):
    if with_ctx:
        qc_ref, o_ref, oc_ref, ve_ref, vce_ref = rest
    else:
        o_ref, ve_ref, vce_ref = rest
    lane = lax.broadcasted_iota(jnp.int32, (GRID_W, LANES), 1)
    lo = lane < HEAD_DIM
    ve_ref[:, 0:LANES] = v_ref[...]
    ve_ref[:, LANES:2 * LANES] = jnp.ones((SEQ, LANES), BF16)
    vce_ref[:, 0:LANES] = vc_ref[...]
    vce_ref[:, LANES:2 * LANES] = jnp.ones((CTX_LEN, LANES), BF16)
    kc = kc_ref[...]
    vce = vce_ref[...]

    for r in range(ROWS):
        rs = min(max(r - NA_KH // 2, 0), ROWS - NA_KH)
        cls = r - rs
        q = q_ref[r * GRID_W:(r + 1) * GRID_W, :]
        kl = k_ref[rs * GRID_W:rs * GRID_W + KEYS_LOC, :]
        vl = ve_ref[rs * GRID_W:rs * GRID_W + KEYS_LOC, :]
        zero = jnp.zeros_like(q)
        q2 = jnp.concatenate([jnp.where(lo, q, zero), jnp.where(lo, zero, q)], axis=0)
        s1 = lax.dot_general(q2, kl, _NT_DIMS, preferred_element_type=F32) + tab_ref[0, cls]
        s2 = lax.dot_general(q2, kc, _NT_DIMS, preferred_element_type=F32)
        m = jnp.maximum(jnp.max(s1, axis=1, keepdims=True), jnp.max(s2, axis=1, keepdims=True))
        p1 = jnp.exp(s1 - m).astype(BF16)
        p2 = jnp.exp(s2 - m).astype(BF16)
        oe = (jnp.dot(p1, vl, preferred_element_type=F32) + jnp.dot(p2, vce, preferred_element_type=F32))
        o = oe[:, 0:LANES] / oe[:, LANES:2 * LANES]
        o_ref[r * GRID_W:(r + 1) * GRID_W, :] = jnp.where(lo, o[0:GRID_W], o[GRID_W:2 * GRID_W]).astype(BF16)

    if with_ctx:
        lo_c = lax.broadcasted_iota(jnp.int32, (CTX_LEN, LANES), 1) < HEAD_DIM
        qc = qc_ref[...]
        zero_c = jnp.zeros_like(qc)
        qc2 = jnp.concatenate([jnp.where(lo_c, qc, zero_c), jnp.where(lo_c, zero_c, qc)], axis=0)
        sc = lax.dot_general(qc2, kc, _NT_DIMS, preferred_element_type=F32)
        pc = jnp.exp(sc - jnp.max(sc, axis=1, keepdims=True)).astype(BF16)
        oc = jnp.dot(pc, vce, preferred_element_type=F32)
        oc = oc[:, 0:LANES] / oc[:, LANES:2 * LANES]
        oc_ref[...] = jnp.where(lo_c, oc[0:CTX_LEN], oc[CTX_LEN:2 * CTX_LEN]).astype(BF16)


def _attn_latent(qkv, table, with_ctx):
    npair = NA_HEADS // 2
    cblk = T_LAT // CTX_LEN
    in_specs = [
        pl.BlockSpec((SEQ, LANES), lambda b, p: (b, p)),
        pl.BlockSpec((SEQ, LANES), lambda b, p: (b, npair + p)),
        pl.BlockSpec((SEQ, LANES), lambda b, p: (b, 2 * npair + p)),
        pl.BlockSpec((CTX_LEN, LANES), lambda b, p: (cblk + b, npair + p)),
        pl.BlockSpec((CTX_LEN, LANES), lambda b, p: (cblk + b, 2 * npair + p)),
        pl.BlockSpec((1, NA_KH, 2 * GRID_W, KEYS_LOC), lambda b, p: (p, 0, 0, 0)),
    ]
    out_specs = [pl.BlockSpec((SEQ, LANES), lambda b, p: (b, p))]
    out_shape = [jax.ShapeDtypeStruct((T_LAT, D_NA), BF16)]
    operands = [qkv, qkv, qkv, qkv, qkv, table]
    if with_ctx:
        in_specs.append(pl.BlockSpec((CTX_LEN, LANES), lambda b, p: (cblk + b, p)))
        out_specs.append(pl.BlockSpec((CTX_LEN, LANES), lambda b, p: (b, p)))
        out_shape.append(jax.ShapeDtypeStruct((T_CTX, D_NA), BF16))
        operands.append(qkv)
    return pl.pallas_call(
        functools.partial(_attn_latent_kernel, with_ctx=with_ctx),
        name="attn_latent",
        grid=(BATCH, npair),
        in_specs=in_specs,
        out_specs=out_specs,
        out_shape=out_shape,
        scratch_shapes=[pltpu.VMEM((SEQ, 2 * LANES), BF16), pltpu.VMEM((CTX_LEN, 2 * LANES), BF16)],
        compiler_params=_cparams("arbitrary", "arbitrary"),
    )(*operands)


ROUTE_COLS = N_GROUPS + N_EXPERTS


def _to_token_major(ref, val):
    rows = val.shape[0]
    for j in range(D_MODEL // LANES):
        ref[pl.ds(j, rows, stride=SUBLANES), :] = val[:, j * LANES:(j + 1) * LANES]


def _from_token_major(ref, rows, j):
    return ref[pl.ds(j, rows, stride=SUBLANES), :]


OUT_SPLIT = 2


def _outproj_kernel(xa_ref, xb_ref, ffa_ref, ffb_ref, ata_ref, atb_ref, cva_ref, cvb_ref,
                    mod_ref, n2_ref, wo_ref, wr_ref, br_ref, x1_ref, hx_ref, route_ref):
    rows = TM // OUT_SPLIT
    for s in range(OUT_SPLIT):
        sl = pl.ds(s * rows, rows)
        ins = [r.at[sl] for r in (xa_ref, xb_ref, ffa_ref, ffb_ref, ata_ref, atb_ref, cva_ref, cvb_ref)]
        _outproj_rows(*ins, mod_ref, n2_ref, wo_ref, wr_ref, br_ref, x1_ref.at[sl],
                      hx_ref.at[pl.ds(s * rows * SUBLANES, rows * SUBLANES)], route_ref.at[sl])


def _outproj_rows(xa_ref, xb_ref, ffa_ref, ffb_ref, ata_ref, atb_ref, cva_ref, cvb_ref,
                  mod_ref, n2_ref, wo_ref, wr_ref, br_ref, x1_ref, hx_ref, route_ref):
    r1 = D_FOURIER
    r2 = D_FOURIER + D_NA
    mix = (jnp.dot(_pick_stream(ffa_ref, ffb_ref), wo_ref[0, 0:r1, :], preferred_element_type=F32)
           + jnp.dot(_pick_stream(ata_ref, atb_ref), wo_ref[0, r1:r2, :], preferred_element_type=F32)
           + jnp.dot(_pick_stream(cva_ref, cvb_ref), wo_ref[0, r2:D_MODEL, :], preferred_element_type=F32))
    g1 = mod_ref[:, 2 * D_MODEL:3 * D_MODEL]
    x1 = _pick_stream(xa_ref, xb_ref) + g1 * mix
    x1_ref[...] = x1
    ms = jnp.mean(x1 * x1, axis=-1, keepdims=True)
    y = x1 * lax.rsqrt(ms + EPS) * n2_ref[0]
    sh2 = mod_ref[:, 3 * D_MODEL:4 * D_MODEL]
    sc2 = mod_ref[:, 4 * D_MODEL:5 * D_MODEL]
    hx = y * (1.0 + sc2) + sh2
    _to_token_major(hx_ref, hx)

    hx_hi = hx.astype(BF16)
    hx_lo = (hx - hx_hi.astype(F32)).astype(BF16)
    part = (jnp.dot(hx_hi, wr_ref[0], preferred_element_type=F32)
            + jnp.dot(hx_lo, wr_ref[0], preferred_element_type=F32))
    logits = part[:, 0:LANES] + part[:, LANES:2 * LANES] + br_ref[0]
    tm = logits.shape[0]
    lane = lax.broadcasted_iota(jnp.int32, (tm, LANES), 1)
    lane_f = lane.astype(F32)
    group_of_lane = ((lane - N_GROUPS) >> 3).astype(F32)
    big = float(LANES)
    gl = jnp.where(lane < N_GROUPS, logits, -jnp.inf)
    gmax = jnp.max(gl, axis=1, keepdims=True)
    gidx = jnp.min(jnp.where(gl == gmax, lane_f, big), axis=1, keepdims=True)
    g_w = 1.0 / jnp.sum(jnp.exp(gl - gmax), axis=1, keepdims=True)
    in_group = (lane >= N_GROUPS) & (lane < N_GROUPS + N_EXPERTS) & (group_of_lane == gidx)
    es = jnp.where(in_group, logits, -jnp.inf)
    t1 = jnp.max(es, axis=1, keepdims=True)
    i1 = jnp.min(jnp.where(es == t1, lane_f, big), axis=1, keepdims=True)
    es2 = jnp.where(lane_f == i1, -jnp.inf, es)
    t2 = jnp.max(es2, axis=1, keepdims=True)
    i2 = jnp.min(jnp.where(es2 == t2, lane_f, big), axis=1, keepdims=True)
    dlt = jnp.exp(t2 - t1)
    w1 = g_w / (1.0 + dlt)
    w2 = g_w * dlt / (1.0 + dlt)
    e1 = i1 - float(N_GROUPS)
    e2 = i2 - float(N_GROUPS)
    route_ref[...] = jnp.where(lane == 0, e1, jnp.where(lane == 1, e2,
                               jnp.where(lane == 2, w1, jnp.where(lane == 3, w2, 0.0))))


def _out_projection(x_pair, ctx_first_block, ff_pair, at_pair, cv_pair, mod4, norm2, w_out_bf, w_r, b_r, li, n_tok):
    nt = n_tok // TM
    return pl.pallas_call(
        _outproj_kernel,
        name="out_projection",
        grid=(nt,),
        in_specs=[
            _lat_spec(D_MODEL), _ctx_spec(D_MODEL, ctx_first_block),
            _lat_spec(D_FOURIER), _ctx_spec(D_FOURIER, 0),
            _lat_spec(D_NA), _ctx_spec(D_NA, 0),
            _lat_spec(D_CONV), _ctx_spec(D_CONV, 0),
            pl.BlockSpec((None, None, 1, 6 * D_MODEL), lambda i: (li, _mod_row(i, TM), 0, 0)),
            pl.BlockSpec((1, 1, D_MODEL), lambda i: (li, 0, 0)),
            pl.BlockSpec((1, D_MODEL, D_MODEL), lambda i: (li, 0, 0)),
            pl.BlockSpec((1, D_MODEL, 2 * LANES), lambda i: (li, 0, 0)),
            pl.BlockSpec((1, 1, LANES), lambda i: (li, 0, 0)),
        ],
        out_specs=[
            pl.BlockSpec((TM, D_MODEL), lambda i: (i, 0)),
            pl.BlockSpec((TM * SUBLANES, LANES), lambda i: (i, 0)),
            pl.BlockSpec((TM, LANES), lambda i: (i, 0)),
        ],
        out_shape=[
            jax.ShapeDtypeStruct((n_tok, D_MODEL), F32),
            jax.ShapeDtypeStruct((n_tok * SUBLANES, LANES), F32),
            jax.ShapeDtypeStruct((n_tok, LANES), F32),
        ],
        compiler_params=_cparams("arbitrary"),
    )(*x_pair, *ff_pair, *at_pair, *cv_pair, mod4, norm2.reshape(DEPTH, 1, D_MODEL), w_out_bf, w_r, b_r)


def _plan_kernel(route_ref, pos_ref, cnt_ref, ends_ref, carry_ref, offs_ref):
    ph = pl.program_id(0)
    i = pl.program_id(1)
    tm = PLAN_SUB
    lane = lax.broadcasted_iota(jnp.int32, (tm, LANES), 1)

    def one_hots(k):
        r = route_ref[k * tm:(k + 1) * tm, :]
        oh1 = lane == r[:, 0:1].astype(jnp.int32)
        oh2 = lane == r[:, 1:2].astype(jnp.int32)
        return oh1, oh2, jnp.where(oh1 | oh2, 1.0, 0.0)

    @pl.when((ph == 0) & (i == 0))
    def _():
        carry_ref[...] = jnp.zeros_like(carry_ref)

    @pl.when(ph == 0)
    def _():
        total = carry_ref[...]
        for k in range(PLAN_TM // tm):
            total = total + jnp.sum(one_hots(k)[2], axis=0, keepdims=True)
        carry_ref[...] = total

    @pl.when((ph == 1) & (i == 0))
    def _():
        cnt = carry_ref[...]
        cnt_ref[...] = jnp.broadcast_to(cnt, cnt_ref.shape)
        tiles = jnp.ceil(cnt * (1.0 / TMX))
        a = lax.broadcasted_iota(jnp.int32, (LANES, LANES), 0)
        b = lax.broadcasted_iota(jnp.int32, (LANES, LANES), 1)
        upper = jnp.where(a < b, 1.0, 0.0).astype(BF16)
        tiles8 = jnp.broadcast_to(tiles, (SUBLANES, LANES))
        first = jnp.dot(tiles8.astype(BF16), upper, preferred_element_type=F32)
        offs_ref[...] = first[0:1, :] * TMX
        ends_ref[...] = (first + tiles8) * TMX
        carry_ref[...] = jnp.zeros_like(carry_ref)

    @pl.when(ph == 1)
    def _():
        row = lax.broadcasted_iota(jnp.int32, (tm, tm), 0)
        col = lax.broadcasted_iota(jnp.int32, (tm, tm), 1)
        tri = jnp.where(row > col, 1.0, 0.0).astype(BF16)
        base = carry_ref[...] + offs_ref[...]
        for k in range(PLAN_TM // tm):
            oh1, oh2, oh = one_hots(k)
            cum = jnp.dot(tri, oh.astype(BF16), preferred_element_type=F32) + base
            p1 = jnp.sum(jnp.where(oh1, cum, 0.0), axis=1, keepdims=True)
            p2 = jnp.sum(jnp.where(oh2, cum, 0.0), axis=1, keepdims=True)
            pos_ref[k * tm:(k + 1) * tm, :] = (
                jnp.where(lane == 0, p1, jnp.where(lane == 1, p2, 0.0)).astype(jnp.int32))
            base = base + jnp.sum(oh, axis=0, keepdims=True)
        carry_ref[...] = base - offs_ref[...]


PLAN_TM = 2048
PLAN_SUB = 512


def _plan(route, n_tok):
    nt = n_tok // PLAN_TM
    return pl.pallas_call(
        _plan_kernel,
        name="moe_plan",
        grid=(2, nt),
        in_specs=[pl.BlockSpec((PLAN_TM, LANES), lambda ph, i: (i, 0))],
        out_specs=[
            pl.BlockSpec((PLAN_TM, LANES), lambda ph, i: (i * ph, 0)),
            pl.BlockSpec((SUBLANES, LANES), lambda ph, i: (0, 0)),
            pl.BlockSpec((SUBLANES, LANES), lambda ph, i: (0, 0)),
        ],
        out_shape=[
            jax.ShapeDtypeStruct((n_tok, LANES), jnp.int32),
            jax.ShapeDtypeStruct((SUBLANES, LANES), F32),
            jax.ShapeDtypeStruct((SUBLANES, LANES), F32),
        ],
        scratch_shapes=[pltpu.VMEM((1, LANES), F32), pltpu.VMEM((1, LANES), F32)],
        compiler_params=_cparams("arbitrary", "arbitrary"),
    )(route)


def _row_tile(ref, row):
    return ref.at[pl.ds(pl.multiple_of(row * SUBLANES, SUBLANES), SUBLANES), :]


DISPATCH_UNROLL = 8
PAD_CHUNKS = tuple(1 << b for b in reversed(range(TMX.bit_length() - 1)))


def _dispatch_kernel(pos_ref, pstart_ref, plen_ref, tail_ref, hx_ref, xs_ref, zbuf, sem, zsem):
    i = pl.program_id(0)

    def pad_copies(fn):
        def ebody(e, carry):
            n = plen_ref[e]
            off = pstart_ref[e]
            for rows in PAD_CHUNKS:
                @pl.when((n & rows) != 0)
                def _():
                    fn(pltpu.make_async_copy(
                        zbuf.at[pl.ds(0, rows * SUBLANES), :],
                        xs_ref.at[pl.ds(pl.multiple_of(off * SUBLANES, SUBLANES), rows * SUBLANES), :], zsem.at[0]))
                off = off + (n & rows)
            return carry

        lax.fori_loop(0, N_EXPERTS, ebody, 0)

        def tbody(k, carry):
            row = pl.multiple_of((tail_ref[0] + k * PAD_CHUNKS[0]) * SUBLANES, SUBLANES)
            fn(pltpu.make_async_copy(zbuf, xs_ref.at[pl.ds(row, PAD_CHUNKS[0] * SUBLANES), :], zsem.at[0]))
            return carry

        lax.fori_loop(0, tail_ref[1], tbody, 0)

    @pl.when(i == 0)
    def _():
        zbuf[...] = jnp.zeros_like(zbuf)
        pad_copies(lambda cp: cp.start())

    def body(k, carry):
        for u in range(DISPATCH_UNROLL):
            t = k * DISPATCH_UNROLL + u
            src = _row_tile(hx_ref, t)
            for j in range(2):
                pltpu.make_async_copy(src, _row_tile(xs_ref, pos_ref[(i * TD + t) * 2 + j]),
                                      sem.at[0]).start(priority=j)
        return carry

    lax.fori_loop(0, TD // DISPATCH_UNROLL, body, 0)
    nrow = TD * SUBLANES
    for j in range(2):
        pltpu.make_async_copy(hx_ref, xs_ref.at[pl.ds(0, nrow), :], sem.at[0]).wait()

    @pl.when(i == 0)
    def _():
        pad_copies(lambda cp: cp.wait())


def _dispatch(pos_flat, pad_start, pad_len, tail, hx_tm, n_tok, n_rows):
    return pl.pallas_call(
        _dispatch_kernel,
        name="moe_dispatch",
        grid_spec=pltpu.PrefetchScalarGridSpec(
            num_scalar_prefetch=4,
            grid=(n_tok // TD,),
            in_specs=[pl.BlockSpec((TD * SUBLANES, LANES), lambda i, pos, ps, pn, tl: (i, 0))],
            out_specs=pl.BlockSpec(memory_space=pl.ANY),
            scratch_shapes=[pltpu.VMEM((PAD_CHUNKS[0] * SUBLANES, LANES), F32),
                            pltpu.SemaphoreType.DMA((1,)), pltpu.SemaphoreType.DMA((1,))],
        ),
        out_shape=jax.ShapeDtypeStruct((n_rows * SUBLANES, LANES), F32),
        compiler_params=_cparams("arbitrary"),
    )(pos_flat, pad_start, pad_len, tail, hx_tm)


XS_SLOTS = 3


def _moe_kernel(te_ref, na_ref, first_ref, wslot_ref, nxt_ref, xs_hbm, wg_hbm, wu_hbm, wd_hbm, os_ref,
                xbuf, wg_f, wu_f, wd_f, wg_s, wu_s, wd_s, xsem, wsem, *, li):
    i = pl.program_id(0)
    n_act = na_ref[0]
    active = i < n_act

    def xs_copy(tile, slot):
        rows = pl.ds(pl.multiple_of(tile * (TMX * SUBLANES), SUBLANES), TMX * SUBLANES)
        return pltpu.make_async_copy(xs_hbm.at[rows, :], xbuf.at[slot], xsem.at[slot])

    @pl.when(i == 0)
    def _():
        xs_copy(0, 0).start()

        @pl.when(n_act > 1)
        def _():
            xs_copy(1, 1).start()

    @pl.when(i + 2 < n_act)
    def _():
        xs_copy(i + 2, (i + 2) % XS_SLOTS).start()

    def weight_copies(e, slot):
        return [pltpu.make_async_copy(wg_hbm.at[li, e], wg_f.at[slot], wsem.at[slot, 0]),
                pltpu.make_async_copy(wu_hbm.at[li, e], wu_f.at[slot], wsem.at[slot, 1]),
                pltpu.make_async_copy(wd_hbm.at[li, e], wd_f.at[slot], wsem.at[slot, 2])]

    @pl.when(active & (first_ref[i] == 1))
    def _():
        slot = wslot_ref[i]

        @pl.when(i == 0)
        def _():
            for cp in weight_copies(te_ref[0], 0):
                cp.start()

        @pl.when(nxt_ref[i] >= 0)
        def _():
            for cp in weight_copies(nxt_ref[i], 1 - slot):
                cp.start()

        for cp in weight_copies(te_ref[i], slot):
            cp.wait()
        wg_s[...] = wg_f[slot].astype(BF16)
        wu_s[...] = wu_f[slot].astype(BF16)
        wd_s[...] = wd_f[slot].astype(BF16)

    @pl.when(active)
    def _():
        xs_copy(i, i % XS_SLOTS).wait()
        xs_ref = xbuf.at[i % XS_SLOTS]
        x = jnp.concatenate([_from_token_major(xs_ref, TMX, j).astype(BF16)
                             for j in range(D_MODEL // LANES)], axis=-1)
        g = jnp.dot(x, wg_s[...], preferred_element_type=F32)
        u = jnp.dot(x, wu_s[...], preferred_element_type=F32)
        h = (g * jax.nn.sigmoid(g) * u).astype(BF16)
        _to_token_major(os_ref, jnp.dot(h, wd_s[...], preferred_element_type=F32))

    @pl.when(jnp.logical_not(active))
    def _():
        os_ref[...] = jnp.zeros_like(os_ref)


def _moe(tile_expert, n_active, has_rows, xs, w_gate, w_up, w_down, li, n_tiles):
    eid = jnp.arange(N_EXPERTS, dtype=jnp.int32)
    later = jnp.where(has_rows[None, :] & (eid[None, :] > eid[:, None]), eid[None, :], N_EXPERTS)
    next_of_expert = jnp.min(later, axis=1)
    next_of_expert = jnp.where(next_of_expert == N_EXPERTS, -1, next_of_expert)
    first = jnp.concatenate([jnp.ones((1,), jnp.int32),
                             (tile_expert[1:] != tile_expert[:-1]).astype(jnp.int32)])
    wslot = (jnp.cumsum(first) - 1) % 2
    nxt = jnp.sum(jnp.where(tile_expert[:, None] == eid[None, :], next_of_expert[None, :], 0), axis=1)


    return pl.pallas_call(
        functools.partial(_moe_kernel, li=li),
        name="moe_experts",
        grid_spec=pltpu.PrefetchScalarGridSpec(
            num_scalar_prefetch=5,
            grid=(n_tiles,),
            in_specs=[
                pl.BlockSpec(memory_space=pl.ANY),
                pl.BlockSpec(memory_space=pl.ANY),
                pl.BlockSpec(memory_space=pl.ANY),
                pl.BlockSpec(memory_space=pl.ANY),
            ],
            out_specs=pl.BlockSpec((TMX * SUBLANES, LANES), lambda i, te, na, fi, ws, nx: (i, 0)),
            scratch_shapes=[
                pltpu.VMEM((XS_SLOTS, TMX * SUBLANES, LANES), F32),
                pltpu.VMEM((2, D_MODEL, D_EXPERT), F32),
                pltpu.VMEM((2, D_MODEL, D_EXPERT), F32),
                pltpu.VMEM((2, D_EXPERT, D_MODEL), F32),
                pltpu.VMEM((D_MODEL, D_EXPERT), BF16),
                pltpu.VMEM((D_MODEL, D_EXPERT), BF16),
                pltpu.VMEM((D_EXPERT, D_MODEL), BF16),
                pltpu.SemaphoreType.DMA((XS_SLOTS,)),
                pltpu.SemaphoreType.DMA((2, 3)),
            ],
        ),
        out_shape=jax.ShapeDtypeStruct((n_tiles * TMX * SUBLANES, LANES), F32),
        compiler_params=_cparams("arbitrary"),
    )(tile_expert, n_active, first, wslot.astype(jnp.int32), nxt.astype(jnp.int32), xs, w_gate, w_up, w_down)


def _gather_pipeline(pos_ref, os_ref, buf_even, buf_odd, sem, n, compute):
    i = pl.program_id(0)
    last = pl.num_programs(0) - 1
    bufs = (buf_even, buf_odd)

    def wait_rows(k):
        for j in range(2):
            pltpu.make_async_copy(os_ref.at[pl.ds(0, n * SUBLANES), :], bufs[k].at[j], sem.at[k, j]).wait()

    @pl.when(i == 0)
    def _():
        def body(t, carry):
            for j in range(2):
                pltpu.make_async_copy(_row_tile(os_ref, pos_ref[t * 2 + j]), _row_tile(buf_even.at[j], t),
                                      sem.at[0, j]).start(priority=j)
            return carry

        lax.fori_loop(0, n, body, 0)

    def step(cur, nxt):
        wait_rows(cur)
        base = jnp.minimum(i + 1, last) * n
        for t in range(n):
            for j in range(2):
                pltpu.make_async_copy(_row_tile(os_ref, pos_ref[(base + t) * 2 + j]),
                                      bufs[nxt].at[j, pl.ds(t * SUBLANES, SUBLANES), :],
                                      sem.at[nxt, j]).start(priority=j)
        compute(bufs[cur])

        @pl.when(i == last)
        def _():
            wait_rows(nxt)

    @pl.when(i % 2 == 0)
    def _():
        step(0, 1)

    @pl.when(i % 2 == 1)
    def _():
        step(1, 0)


def _combined_rows(x1_ref, route_ref, mod_ref, buf, r0, rows):
    sl = pl.ds(r0, rows)
    w1 = route_ref[sl, 2:3]
    w2 = route_ref[sl, 3:4]
    tm_rows = pl.ds(r0 * SUBLANES, rows * SUBLANES)
    cols = []
    for c in range(D_MODEL // LANES):
        y = (w1 * _from_token_major(buf.at[0, tm_rows], rows, c)
             + w2 * _from_token_major(buf.at[1, tm_rows], rows, c))
        g2 = mod_ref[:, 5 * D_MODEL + c * LANES:5 * D_MODEL + (c + 1) * LANES]
        cols.append(x1_ref[sl, c * LANES:(c + 1) * LANES] + g2 * y)
    return jnp.concatenate(cols, axis=-1)


ROW_SPLIT = 2


def _combine_final_kernel(pos_ref, x1_ref, route_ref, mod_ref, nf_ref, os_ref, o_ref, buf_even, buf_odd, sem):
    def compute(buf):
        x2 = _combined_rows(x1_ref, route_ref, mod_ref, buf, 0, TC)
        ms = jnp.mean(x2 * x2, axis=-1, keepdims=True)
        o_ref[...] = x2 * lax.rsqrt(ms + EPS) * nf_ref[...]

    _gather_pipeline(pos_ref, os_ref, buf_even, buf_odd, sem, TC, compute)


def _gather_scratch(n):
    return [pltpu.VMEM((2, n * SUBLANES, LANES), F32), pltpu.VMEM((2, n * SUBLANES, LANES), F32),
            pltpu.SemaphoreType.DMA((2, 2))]


def _combine_inproj_kernel(pos_ref, x1_ref, route_ref, mod_ref, modn_ref, n1_ref, w_ref, dft_ref, os_ref,
                           x2_ref, yf_ref, qkv_ref, cv_ref, buf_even, buf_odd, sem):
    def compute(buf):
        rows = TM // ROW_SPLIT
        for s in range(ROW_SPLIT):
            sl = pl.ds(s * rows, rows)
            x2 = _combined_rows(x1_ref, route_ref, mod_ref, buf, s * rows, rows)
            x2_ref[sl, :] = x2
            _inproj_math(x2, modn_ref, n1_ref, w_ref, dft_ref, yf_ref.at[sl], qkv_ref.at[sl], cv_ref.at[sl])

    _gather_pipeline(pos_ref, os_ref, buf_even, buf_odd, sem, TM, compute)


def _combine_inproj(pos_flat, x1, route, mod4, norm1, w_in_bf, dft_ch, osrt, li):
    def tmap(i, pos):
        return (i, 0)

    return pl.pallas_call(
        _combine_inproj_kernel,
        name="moe_combine_in_projection",
        grid_spec=pltpu.PrefetchScalarGridSpec(
            num_scalar_prefetch=1,
            grid=(T_ALL // TM,),
            in_specs=[
                pl.BlockSpec((TM, D_MODEL), tmap),
                pl.BlockSpec((TM, LANES), tmap),
                pl.BlockSpec((None, None, 1, 6 * D_MODEL), lambda i, pos: (li, _mod_row(i, TM), 0, 0)),
                pl.BlockSpec((None, None, 1, 6 * D_MODEL), lambda i, pos: (li + 1, _mod_row(i, TM), 0, 0)),
                pl.BlockSpec((1, 1, D_MODEL), lambda i, pos: (li + 1, 0, 0)),
                pl.BlockSpec((1, D_MODEL, D_IN_PROJ), lambda i, pos: (li + 1, 0, 0)),
                pl.BlockSpec((D_FOURIER, 2 * D_FOURIER), lambda i, pos: (0, 0)),
                pl.BlockSpec(memory_space=pl.ANY),
            ],
            out_specs=[
                pl.BlockSpec((TM, D_MODEL), tmap),
                pl.BlockSpec((TM, 2 * D_FOURIER), tmap),
                pl.BlockSpec((TM, 3 * D_NA), tmap),
                pl.BlockSpec((TM, 3 * D_CONV), tmap),
            ],
            scratch_shapes=_gather_scratch(TM),
        ),
        out_shape=[
            jax.ShapeDtypeStruct((T_ALL, D_MODEL), F32),
            jax.ShapeDtypeStruct((T_ALL, 2 * D_FOURIER), BF16),
            jax.ShapeDtypeStruct((T_ALL, 3 * D_NA), BF16),
            jax.ShapeDtypeStruct((T_ALL, 3 * D_CONV), F32),
        ],
        compiler_params=_cparams("arbitrary"),
    )(pos_flat, x1, route, mod4, mod4, norm1.reshape(DEPTH, 1, D_MODEL), w_in_bf, dft_ch, osrt)


def _combine_final(pos_flat, x1, route, mod4, norm_final, osrt, li, n_tok):
    def tmap(i, pos):
        return (i, 0)

    return pl.pallas_call(
        _combine_final_kernel,
        name="moe_combine_final",
        grid_spec=pltpu.PrefetchScalarGridSpec(
            num_scalar_prefetch=1,
            grid=(n_tok // TC,),
            in_specs=[
                pl.BlockSpec((TC, D_MODEL), tmap),
                pl.BlockSpec((TC, LANES), tmap),
                pl.BlockSpec((None, None, 1, 6 * D_MODEL), lambda i, pos: (li, _mod_row(i, TC), 0, 0)),
                pl.BlockSpec((1, D_MODEL), lambda i, pos: (0, 0)),
                pl.BlockSpec(memory_space=pl.ANY),
            ],
            out_specs=pl.BlockSpec((TC, D_MODEL), tmap),
            scratch_shapes=_gather_scratch(TC),
        ),
        out_shape=jax.ShapeDtypeStruct((n_tok, D_MODEL), F32),
        compiler_params=_cparams("arbitrary"),
    )(pos_flat, x1, route, mod4, norm_final.reshape(1, D_MODEL), osrt)


def _moe_block(hx_tm, route, w_gate, w_up, w_down, li, n_tok):
    n_tiles = (2 * n_tok) // TMX + N_EXPERTS
    n_rows = n_tiles * TMX
    pos, cnt, ends = _plan(route, n_tok)
    pos = pos[:, 0:2].reshape(-1)
    cnt = cnt[0, :N_EXPERTS].astype(jnp.int32)
    ends = ends[0, :N_EXPERTS].astype(jnp.int32)
    padded = ((cnt + TMX - 1) // TMX) * TMX
    pad_start = ends - padded + cnt
    tile_start = jnp.arange(n_tiles, dtype=jnp.int32) * TMX
    tile_expert = jnp.minimum(jnp.sum((tile_start[:, None] >= ends[None, :]).astype(jnp.int32), axis=1),
                              N_EXPERTS - 1)
    n_active = (ends[-1:] // TMX).astype(jnp.int32)
    tail = jnp.concatenate([ends[-1:], (n_rows - ends[-1:]) // PAD_CHUNKS[0]])

    xs = _dispatch(pos, pad_start, padded - cnt, tail, hx_tm, n_tok, n_rows)
    return pos, _moe(tile_expert, n_active, padded > 0, xs, w_gate, w_up, w_down, li, n_tiles)


def kernel(x, c, ctx, c_ctx, w_ada, b_ada, norm1, norm2, w_in, w_fourier, w_conv, rpb, w_out, w_rg, b_rg,
           w_re, b_re, w_gate, w_up, w_down, norm_final):
    x_pair, ctx_first = (x.reshape(T_LAT, D_MODEL), ctx.reshape(T_CTX, D_MODEL)), 0
    cc =jnp.concatenate([c, c_ctx[None, :], jnp.zeros((MOD_ROWS - BATCH - 1, D_MODEL), F32)], axis=0)
    mod4 = _modulation(cc, w_ada, b_ada).reshape(DEPTH, MOD_ROWS, 1, 6 * D_MODEL)

    w_in_bf = w_in.astype(BF16)
    w_out_bf = w_out.astype(BF16)
    pad = jnp.zeros((DEPTH, D_MODEL, LANES - ROUTE_COLS), F32)
    w_r = jnp.concatenate([w_rg, w_re, pad], axis=-1)
    w_r_hi = w_r.astype(BF16)
    w_r = jnp.concatenate([w_r_hi, (w_r - w_r_hi.astype(F32)).astype(BF16)], axis=-1)
    b_r = jnp.concatenate([b_rg, b_re, pad[:, 0, :]], axis=-1).reshape(DEPTH, 1, LANES)
    dft_ch = jnp.asarray(_channel_dft(), dtype=F32).astype(BF16)
    ctx_blk = T_LAT // CTX_LEN

    for li in range(DEPTH):
        last = li == DEPTH - 1
        if li == 0:
            yf, qkv, cv = _in_projection(*x_pair, ctx_first, mod4, norm1, w_in_bf, dft_ch, li)
        table = _bias_table(rpb[li].reshape(-1))
        ff = _fourier_mix(yf, w_fourier, li, SEQ, 0)
        cvo = _short_conv(cv, w_conv, li, SEQ, 0)
        if last:
            n_tok = T_LAT
            at, = _attn_latent(qkv, table, with_ctx=False)
            ff_c, cvo_c, at_c = ff, cvo, at
        else:
            n_tok = T_ALL
            ff_c = _fourier_mix(yf, w_fourier, li, CTX_LEN, ctx_blk)
            cvo_c = _short_conv(cv, w_conv, li, CTX_LEN, ctx_blk)
            at, at_c = _attn_latent(qkv, table, with_ctx=True)
        x1, hx_tm, route = _out_projection(x_pair, ctx_first, (ff, ff_c), (at, at_c), (cvo, cvo_c), mod4, norm2,
                                           w_out_bf, w_r, b_r, li, n_tok)
        pos, osrt = _moe_block(hx_tm, route, w_gate, w_up, w_down, li, n_tok)
        if last:
            out = _combine_final(pos, x1, route, mod4, norm_final, osrt, li, n_tok)
            return out.reshape(BATCH, SEQ, D_MODEL)
        xt, yf, qkv, cv = _combine_inproj(pos, x1, route, mod4, norm1, w_in_bf, dft_ch, osrt, li)
        x_pair, ctx_first = (xt, xt), NT_LAT
```
